```python
import math
import jax, jax.numpy as jnp
from jax import lax
import numpy as np

D_MODEL = 1024
BATCH = 16
SEQ = 2048
DEPTH = 1

D_FF = 2816
CHUNK = 128
D_A = 512
SGU_GROUPS = 4
SGU_GROUP_DIM = D_A // SGU_GROUPS
HEAD_DIM = 64
N_Q_HEADS = 8
N_KV_HEADS = 2
Q_PER_KV = N_Q_HEADS // N_KV_HEADS
WINDOW = 128
BLOCK = 128
D_B = N_Q_HEADS * HEAD_DIM
SPLITS = (D_A, D_A, D_B, N_KV_HEADS * HEAD_DIM, N_KV_HEADS * HEAD_DIM, D_MODEL, D_MODEL)
IN_COLS = sum(SPLITS)
N_MOD = 9
EPS = 1e-6
NEG = -1e30

kernel_name = "hybrid_gmlp_swa_sink_macaron_adaln"


def rms_norm(x, g):
    xf = x.astype(jnp.float32)
    y = xf * lax.rsqrt(jnp.mean(xf * xf, axis=-1, keepdims=True) + EPS)
    return (y * g.astype(jnp.float32)).astype(x.dtype)


def layer_norm(x, g, b):
    xf = x.astype(jnp.float32)
    mu = jnp.mean(xf, axis=-1, keepdims=True)
    var = jnp.mean(jnp.square(xf - mu), axis=-1, keepdims=True)
    y = (xf - mu) * lax.rsqrt(var + EPS)
    return (y * g.astype(jnp.float32) + b.astype(jnp.float32)).astype(x.dtype)


def modulate(xn, shift, scale):
    return xn * (1 + scale[:, None, :]) + shift[:, None, :]


def swiglu(x, w_gate, w_up, w_down):
    return (jax.nn.silu(x @ w_gate) * (x @ w_up)) @ w_down


def gmlp_sgu(u, v, g_ln, b_ln, w_s, b_s):
    bsz, seq, _ = v.shape
    n_chunks = seq // CHUNK
    v = layer_norm(v, g_ln, b_ln)
    vg = v.reshape(bsz, n_chunks, CHUNK, SGU_GROUPS, SGU_GROUP_DIM)
    causal = jnp.tril(jnp.ones((CHUNK, CHUNK), dtype=bool))
    ws = jnp.where(causal[None], w_s, jnp.zeros_like(w_s))
    z = jnp.einsum('gts,bnsgc->bntgc', ws, vg)
    z = z + b_s.T[None, None, :, :, None]
    return u * z.reshape(bsz, seq, D_A)


def sliding_window_attention(q, k, v, g_q, g_k, sinks):
    bsz, seq = q.shape[0], q.shape[1]
    nb = seq // BLOCK
    q = rms_norm(q, g_q)
    k = rms_norm(k, g_k)
    pad = ((0, 0), (BLOCK, 0), (0, 0), (0, 0))
    k_prev = jnp.pad(k, pad)[:, :seq]
    v_prev = jnp.pad(v, pad)[:, :seq]
    kb = jnp.concatenate([k_prev.reshape(bsz, nb, BLOCK, N_KV_HEADS, HEAD_DIM),
                          k.reshape(bsz, nb, BLOCK, N_KV_HEADS, HEAD_DIM)], axis=2)
    vb = jnp.concatenate([v_prev.reshape(bsz, nb, BLOCK, N_KV_HEADS, HEAD_DIM),
                          v.reshape(bsz, nb, BLOCK, N_KV_HEADS, HEAD_DIM)], axis=2)
    qb = q.reshape(bsz, nb, BLOCK, N_KV_HEADS, Q_PER_KV, HEAD_DIM)
    scores = jnp.einsum('bnqhgd,bnkhd->bnhgqk', qb, kb).astype(jnp.float32)
    scores = scores * (HEAD_DIM ** -0.5)
    blk = jnp.arange(nb)[:, None, None]
    qpos = blk * BLOCK + jnp.arange(BLOCK)[None, :, None]
    kpos = (blk - 1) * BLOCK + jnp.arange(2 * BLOCK)[None, None, :]
    diff = qpos - kpos
    valid = (diff >= 0) & (diff < WINDOW) & (kpos >= 0)
    scores = jnp.where(valid[None, :, None, None], scores, NEG)
    sink = jnp.broadcast_to(
        sinks.astype(jnp.float32).reshape(1, 1, N_KV_HEADS, Q_PER_KV, 1, 1),
        scores.shape[:-1] + (1,))
    probs = jax.nn.softmax(jnp.concatenate([scores, sink], axis=-1), axis=-1)[..., :-1]
    out = jnp.einsum('bnhgqk,bnkhd->bnqhgd', probs.astype(v.dtype), vb)
    return out.reshape(bsz, seq, D_B)


def _fwd_setup_inputs(seed: int = 0) -> dict:
    key = jax.random.key(seed)
    ks = jax.random.split(key, 26)
    f32 = jnp.float32
    L = DEPTH

    def nrm(k, shape, fan_in):
        return jax.random.normal(k, shape, f32) * (fan_in ** -0.5)

    def gain(k, shape):
        return 1.0 + 0.02 * jax.random.normal(k, shape, f32)

    return {
        "x": jax.random.normal(ks[0], (BATCH, SEQ, D_MODEL), f32),
        "c": jax.random.normal(ks[1], (BATCH, D_MODEL), f32),
        "w_ada": nrm(ks[2], (L, D_MODEL, N_MOD * D_MODEL), D_MODEL) * 0.5,
        "b_ada": 0.01 * jax.random.normal(ks[3], (L, N_MOD * D_MODEL), f32),
        "g_norm1": gain(ks[4], (L, D_MODEL)),
        "ffn1_w_gate": nrm(ks[5], (L, D_MODEL, D_FF), D_MODEL),
        "ffn1_w_up": nrm(ks[6], (L, D_MODEL, D_FF), D_MODEL),
        "ffn1_w_down": nrm(ks[7], (L, D_FF, D_MODEL), D_FF),
        "g_norm2": gain(ks[8], (L, D_MODEL)),
        "w_in": nrm(ks[9], (L, D_MODEL, IN_COLS), D_MODEL),
        "g_sgu_ln": gain(ks[10], (L, D_A)),
        "b_sgu_ln": 0.01 * jax.random.normal(ks[11], (L, D_A), f32),
        "w_spatial": nrm(ks[12], (L, SGU_GROUPS, CHUNK, CHUNK), CHUNK),
        "b_spatial": 1.0 + 0.02 * jax.random.normal(ks[13], (L, SGU_GROUPS, CHUNK), f32),
        "g_q": gain(ks[14], (L, HEAD_DIM)),
        "g_k": gain(ks[15], (L, HEAD_DIM)),
        "attn_sinks": jax.random.normal(ks[16], (L, N_Q_HEADS), f32),
        "w_branch_a": nrm(ks[17], (L, D_A, D_MODEL), D_A),
        "w_branch_b": nrm(ks[18], (L, D_B, D_MODEL), D_B),
        "w_out": nrm(ks[19], (L, D_MODEL, D_MODEL), D_MODEL),
        "g_norm3": gain(ks[20], (L, D_MODEL)),
        "ffn2_w_gate": nrm(ks[21], (L, D_MODEL, D_FF), D_MODEL),
        "ffn2_w_up": nrm(ks[22], (L, D_MODEL, D_FF), D_MODEL),
        "ffn2_w_down": nrm(ks[23], (L, D_FF, D_MODEL), D_FF),
    }


def _fwd_reference(x, c, w_ada, b_ada, g_norm1, ffn1_w_gate, ffn1_w_up, ffn1_w_down,
              g_norm2, w_in, g_sgu_ln, b_sgu_ln, w_spatial, b_spatial, g_q, g_k,
              attn_sinks, w_branch_a, w_branch_b, w_out, g_norm3,
              ffn2_w_gate, ffn2_w_up, ffn2_w_down):
    bsz, seq, _ = x.shape
    cond = jax.nn.silu(c)
    split_idx = list(np.cumsum(SPLITS)[:-1])
    h = x
    for l in range(DEPTH):
        mods = cond @ w_ada[l] + b_ada[l]
        sh1, sc1, ga1, sh2, sc2, ga2, sh3, sc3, ga3 = jnp.split(mods, N_MOD, axis=-1)

        xn = modulate(rms_norm(h, g_norm1[l]), sh1, sc1)
        h = h + 0.5 * ga1[:, None, :] * swiglu(xn, ffn1_w_gate[l], ffn1_w_up[l], ffn1_w_down[l])

        xn = modulate(rms_norm(h, g_norm2[l]), sh2, sc2)
        proj = xn @ w_in[l]
        a_u, a_v, q, k, v, gate_a, gate_b = jnp.split(proj, split_idx, axis=-1)

        y_a = gmlp_sgu(jax.nn.gelu(a_u), jax.nn.gelu(a_v), g_sgu_ln[l], b_sgu_ln[l],
                       w_spatial[l], b_spatial[l]) @ w_branch_a[l]

        y_b = sliding_window_attention(
            q.reshape(bsz, seq, N_Q_HEADS, HEAD_DIM),
            k.reshape(bsz, seq, N_KV_HEADS, HEAD_DIM),
            v.reshape(bsz, seq, N_KV_HEADS, HEAD_DIM),
            g_q[l], g_k[l], attn_sinks[l]) @ w_branch_b[l]

        merged = jax.nn.sigmoid(gate_a) * y_a + jax.nn.sigmoid(gate_b) * y_b
        h = h + ga2[:, None, :] * (merged @ w_out[l])

        xn = modulate(rms_norm(h, g_norm3[l]), sh3, sc3)
        h = h + 0.5 * ga3[:, None, :] * swiglu(xn, ffn2_w_gate[l], ffn2_w_up[l], ffn2_w_down[l])
    return h


import jax as _jax
import jax.numpy as _jnp

TWIN_FORMAT = 'train_step'
FWD_PARAMS = ['x', 'c', 'w_ada', 'b_ada', 'g_norm1', 'ffn1_w_gate', 'ffn1_w_up', 'ffn1_w_down', 'g_norm2', 'w_in', 'g_sgu_ln', 'b_sgu_ln', 'w_spatial', 'b_spatial', 'g_q', 'g_k', 'attn_sinks', 'w_branch_a', 'w_branch_b', 'w_out', 'g_norm3', 'ffn2_w_gate', 'ffn2_w_up', 'ffn2_w_down']
TWIN_WEIGHTS = ['w_ada', 'b_ada', 'g_norm1', 'ffn1_w_gate', 'ffn1_w_up', 'ffn1_w_down', 'g_norm2', 'w_in', 'g_sgu_ln', 'b_sgu_ln', 'w_spatial', 'b_spatial', 'g_q', 'g_k', 'attn_sinks', 'w_branch_a', 'w_branch_b', 'w_out', 'g_norm3', 'ffn2_w_gate', 'ffn2_w_up', 'ffn2_w_down']
TWIN_DIFF_INPUT = 'x'
TWIN_INPUTS = ['x', 'c', 'w_ada', 'b_ada', 'g_norm1', 'ffn1_w_gate', 'ffn1_w_up', 'ffn1_w_down', 'g_norm2', 'w_in', 'g_sgu_ln', 'b_sgu_ln', 'w_spatial', 'b_spatial', 'g_q', 'g_k', 'attn_sinks', 'w_branch_a', 'w_branch_b', 'w_out', 'g_norm3', 'ffn2_w_gate', 'ffn2_w_up', 'ffn2_w_down', 'loss_target', 'm_w_ada', 'm_b_ada', 'm_g_norm1', 'm_ffn1_w_gate', 'm_ffn1_w_up', 'm_ffn1_w_down', 'm_g_norm2', 'm_w_in', 'm_g_sgu_ln', 'm_b_sgu_ln', 'm_w_spatial', 'm_b_spatial', 'm_g_q', 'm_g_k', 'm_attn_sinks', 'm_w_branch_a', 'm_w_branch_b', 'm_w_out', 'm_g_norm3', 'm_ffn2_w_gate', 'm_ffn2_w_up', 'm_ffn2_w_down', 'v_w_ada', 'v_b_ada', 'v_g_norm1', 'v_ffn1_w_gate', 'v_ffn1_w_up', 'v_ffn1_w_down', 'v_g_norm2', 'v_w_in', 'v_g_sgu_ln', 'v_b_sgu_ln', 'v_w_spatial', 'v_b_spatial', 'v_g_q', 'v_g_k', 'v_attn_sinks', 'v_w_branch_a', 'v_w_branch_b', 'v_w_out', 'v_g_norm3', 'v_ffn2_w_gate', 'v_ffn2_w_up', 'v_ffn2_w_down']
TWIN_OUTPUTS = ['loss', 'grad_x', 'grad_w_ada', 'grad_b_ada', 'grad_g_norm1', 'grad_ffn1_w_gate', 'grad_ffn1_w_up', 'grad_ffn1_w_down', 'grad_g_norm2', 'grad_w_in', 'grad_g_sgu_ln', 'grad_b_sgu_ln', 'grad_w_spatial', 'grad_b_spatial', 'grad_g_q', 'grad_g_k', 'grad_attn_sinks', 'grad_w_branch_a', 'grad_w_branch_b', 'grad_w_out', 'grad_g_norm3', 'grad_ffn2_w_gate', 'grad_ffn2_w_up', 'grad_ffn2_w_down', 'delta_w_ada', 'delta_b_ada', 'delta_g_norm1', 'delta_ffn1_w_gate', 'delta_ffn1_w_up', 'delta_ffn1_w_down', 'delta_g_norm2', 'delta_w_in', 'delta_g_sgu_ln', 'delta_b_sgu_ln', 'delta_w_spatial', 'delta_b_spatial', 'delta_g_q', 'delta_g_k', 'delta_attn_sinks', 'delta_w_branch_a', 'delta_w_branch_b', 'delta_w_out', 'delta_g_norm3', 'delta_ffn2_w_gate', 'delta_ffn2_w_up', 'delta_ffn2_w_down', 'new_m_w_ada', 'new_m_b_ada', 'new_m_g_norm1', 'new_m_ffn1_w_gate', 'new_m_ffn1_w_up', 'new_m_ffn1_w_down', 'new_m_g_norm2', 'new_m_w_in', 'new_m_g_sgu_ln', 'new_m_b_sgu_ln', 'new_m_w_spatial', 'new_m_b_spatial', 'new_m_g_q', 'new_m_g_k', 'new_m_attn_sinks', 'new_m_w_branch_a', 'new_m_w_branch_b', 'new_m_w_out', 'new_m_g_norm3', 'new_m_ffn2_w_gate', 'new_m_ffn2_w_up', 'new_m_ffn2_w_down', 'new_v_w_ada', 'new_v_b_ada', 'new_v_g_norm1', 'new_v_ffn1_w_gate', 'new_v_ffn1_w_up', 'new_v_ffn1_w_down', 'new_v_g_norm2', 'new_v_w_in', 'new_v_g_sgu_ln', 'new_v_b_sgu_ln', 'new_v_w_spatial', 'new_v_b_spatial', 'new_v_g_q', 'new_v_g_k', 'new_v_attn_sinks', 'new_v_w_branch_a', 'new_v_w_branch_b', 'new_v_w_out', 'new_v_g_norm3', 'new_v_ffn2_w_gate', 'new_v_ffn2_w_up', 'new_v_ffn2_w_down']
TWIN_LEAF_KINDS = {'loss': 'loss', 'grad_x': 'grad_x', 'grad_w_ada': 'grad_w', 'grad_b_ada': 'grad_w', 'grad_g_norm1': 'grad_w', 'grad_ffn1_w_gate': 'grad_w', 'grad_ffn1_w_up': 'grad_w', 'grad_ffn1_w_down': 'grad_w', 'grad_g_norm2': 'grad_w', 'grad_w_in': 'grad_w', 'grad_g_sgu_ln': 'grad_w', 'grad_b_sgu_ln': 'grad_w', 'grad_w_spatial': 'grad_w', 'grad_b_spatial': 'grad_w', 'grad_g_q': 'grad_w', 'grad_g_k': 'grad_w', 'grad_attn_sinks': 'grad_w', 'grad_w_branch_a': 'grad_w', 'grad_w_branch_b': 'grad_w', 'grad_w_out': 'grad_w', 'grad_g_norm3': 'grad_w', 'grad_ffn2_w_gate': 'grad_w', 'grad_ffn2_w_up': 'grad_w', 'grad_ffn2_w_down': 'grad_w', 'delta_w_ada': 'delta_w', 'delta_b_ada': 'delta_w', 'delta_g_norm1': 'delta_w', 'delta_ffn1_w_gate': 'delta_w', 'delta_ffn1_w_up': 'delta_w', 'delta_ffn1_w_down': 'delta_w', 'delta_g_norm2': 'delta_w', 'delta_w_in': 'delta_w', 'delta_g_sgu_ln': 'delta_w', 'delta_b_sgu_ln': 'delta_w', 'delta_w_spatial': 'delta_w', 'delta_b_spatial': 'delta_w', 'delta_g_q': 'delta_w', 'delta_g_k': 'delta_w', 'delta_attn_sinks': 'delta_w', 'delta_w_branch_a': 'delta_w', 'delta_w_branch_b': 'delta_w', 'delta_w_out': 'delta_w', 'delta_g_norm3': 'delta_w', 'delta_ffn2_w_gate': 'delta_w', 'delta_ffn2_w_up': 'delta_w', 'delta_ffn2_w_down': 'delta_w', 'new_m_w_ada': 'new_m', 'new_m_b_ada': 'new_m', 'new_m_g_norm1': 'new_m', 'new_m_ffn1_w_gate': 'new_m', 'new_m_ffn1_w_up': 'new_m', 'new_m_ffn1_w_down': 'new_m', 'new_m_g_norm2': 'new_m', 'new_m_w_in': 'new_m', 'new_m_g_sgu_ln': 'new_m', 'new_m_b_sgu_ln': 'new_m', 'new_m_w_spatial': 'new_m', 'new_m_b_spatial': 'new_m', 'new_m_g_q': 'new_m', 'new_m_g_k': 'new_m', 'new_m_attn_sinks': 'new_m', 'new_m_w_branch_a': 'new_m', 'new_m_w_branch_b': 'new_m', 'new_m_w_out': 'new_m', 'new_m_g_norm3': 'new_m', 'new_m_ffn2_w_gate': 'new_m', 'new_m_ffn2_w_up': 'new_m', 'new_m_ffn2_w_down': 'new_m', 'new_v_w_ada': 'new_v', 'new_v_b_ada': 'new_v', 'new_v_g_norm1': 'new_v', 'new_v_ffn1_w_gate': 'new_v', 'new_v_ffn1_w_up': 'new_v', 'new_v_ffn1_w_down': 'new_v', 'new_v_g_norm2': 'new_v', 'new_v_w_in': 'new_v', 'new_v_g_sgu_ln': 'new_v', 'new_v_b_sgu_ln': 'new_v', 'new_v_w_spatial': 'new_v', 'new_v_b_spatial': 'new_v', 'new_v_g_q': 'new_v', 'new_v_g_k': 'new_v', 'new_v_attn_sinks': 'new_v', 'new_v_w_branch_a': 'new_v', 'new_v_w_branch_b': 'new_v', 'new_v_w_out': 'new_v', 'new_v_g_norm3': 'new_v', 'new_v_ffn2_w_gate': 'new_v', 'new_v_ffn2_w_up': 'new_v', 'new_v_ffn2_w_down': 'new_v'}


def _forward(args):
    return _fwd_reference(*[args[k] for k in FWD_PARAMS])


def _output_shape():
    out = _jax.eval_shape(lambda: _forward(_fwd_setup_inputs(0)))
    return out.shape, out.dtype

N_MICROBATCH = 1
ADAM_LR = 0.001
ADAM_B1 = 0.9
ADAM_B2 = 0.999
ADAM_EPS = 1e-08
ADAM_WD = 0.01
ADAM_STEP = 10
PER_EXAMPLE_BATCH_AXIS = {'x': 0, 'c': 0, 'loss_target': 0}
SHARED_INPUTS = []
_WEIGHT_DTYPES = {'w_ada': _jnp.float32, 'b_ada': _jnp.float32, 'g_norm1': _jnp.float32, 'ffn1_w_gate': _jnp.float32, 'ffn1_w_up': _jnp.float32, 'ffn1_w_down': _jnp.float32, 'g_norm2': _jnp.float32, 'w_in': _jnp.float32, 'g_sgu_ln': _jnp.float32, 'b_sgu_ln': _jnp.float32, 'w_spatial': _jnp.float32, 'b_spatial': _jnp.float32, 'g_q': _jnp.float32, 'g_k': _jnp.float32, 'attn_sinks': _jnp.float32, 'w_branch_a': _jnp.float32, 'w_branch_b': _jnp.float32, 'w_out': _jnp.float32, 'g_norm3': _jnp.float32, 'ffn2_w_gate': _jnp.float32, 'ffn2_w_up': _jnp.float32, 'ffn2_w_down': _jnp.float32}
MOMENT_SCALE = {'w_ada': 3.617752e-01, 'b_ada': 7.616074e-01, 'g_norm1': 7.515064e-01, 'ffn1_w_gate': 2.197570e-02, 'ffn1_w_up': 2.112829e-02, 'ffn1_w_down': 3.429790e-02, 'g_norm2': 8.275465e-01, 'w_in': 9.150437e-02, 'g_sgu_ln': 4.676429e-01, 'b_sgu_ln': 6.369388e-02, 'w_spatial': 5.039719e-02, 'b_spatial': 9.063384e-01, 'g_q': 1.521138e-01, 'g_k': 1.514851e-01, 'attn_sinks': 7.896832e-02, 'w_branch_a': 1.425713e-01, 'w_branch_b': 9.870712e-02, 'w_out': 1.351198e-01, 'g_norm3': 8.263692e-01, 'ffn2_w_gate': 2.241116e-02, 'ffn2_w_up': 2.100902e-02, 'ffn2_w_down': 3.431637e-02}


def _to_microbatches(a, axis):
    t = _jnp.moveaxis(a, axis, 0)
    t = t.reshape((N_MICROBATCH, t.shape[0] // N_MICROBATCH) + t.shape[1:])
    return _jnp.moveaxis(t, 1, axis + 1)


def setup_inputs(seed: int = 0) -> dict:
    inp = _fwd_setup_inputs(seed)
    key = _jax.random.fold_in(_jax.random.key(seed), 7919)
    shape, _ = _output_shape()
    out = dict(inp)
    out["loss_target"] = _jax.random.normal(_jax.random.fold_in(key, 0), shape, _jnp.float32)
    for i, name in enumerate(TWIN_WEIGHTS):
        w = inp[name].astype(_jnp.float32)
        if MOMENT_SCALE is None:
            s = _jnp.sqrt(_jnp.mean(_jnp.square(w)) + 1e-30)
        else:
            s = MOMENT_SCALE[name]
        km, kv = _jax.random.split(_jax.random.fold_in(key, i + 1))
        out[name] = w
        out["m_" + name] = s * _jax.random.normal(km, w.shape, _jnp.float32)
        out["v_" + name] = (s * s) * _jax.random.uniform(kv, w.shape, _jnp.float32, 0.5, 1.5)
    if N_MICROBATCH > 1:
        for name, axis in PER_EXAMPLE_BATCH_AXIS.items():
            out[name] = _to_microbatches(out[name], axis)
    return {'x': out['x'], 'c': out['c'], 'w_ada': out['w_ada'], 'b_ada': out['b_ada'], 'g_norm1': out['g_norm1'], 'ffn1_w_gate': out['ffn1_w_gate'], 'ffn1_w_up': out['ffn1_w_up'], 'ffn1_w_down': out['ffn1_w_down'], 'g_norm2': out['g_norm2'], 'w_in': out['w_in'], 'g_sgu_ln': out['g_sgu_ln'], 'b_sgu_ln': out['b_sgu_ln'], 'w_spatial': out['w_spatial'], 'b_spatial': out['b_spatial'], 'g_q': out['g_q'], 'g_k': out['g_k'], 'attn_sinks': out['attn_sinks'], 'w_branch_a': out['w_branch_a'], 'w_branch_b': out['w_branch_b'], 'w_out': out['w_out'], 'g_norm3': out['g_norm3'], 'ffn2_w_gate': out['ffn2_w_gate'], 'ffn2_w_up': out['ffn2_w_up'], 'ffn2_w_down': out['ffn2_w_down'], 'loss_target': out['loss_target'], 'm_w_ada': out['m_w_ada'], 'm_b_ada': out['m_b_ada'], 'm_g_norm1': out['m_g_norm1'], 'm_ffn1_w_gate': out['m_ffn1_w_gate'], 'm_ffn1_w_up': out['m_ffn1_w_up'], 'm_ffn1_w_down': out['m_ffn1_w_down'], 'm_g_norm2': out['m_g_norm2'], 'm_w_in': out['m_w_in'], 'm_g_sgu_ln': out['m_g_sgu_ln'], 'm_b_sgu_ln': out['m_b_sgu_ln'], 'm_w_spatial': out['m_w_spatial'], 'm_b_spatial': out['m_b_spatial'], 'm_g_q': out['m_g_q'], 'm_g_k': out['m_g_k'], 'm_attn_sinks': out['m_attn_sinks'], 'm_w_branch_a': out['m_w_branch_a'], 'm_w_branch_b': out['m_w_branch_b'], 'm_w_out': out['m_w_out'], 'm_g_norm3': out['m_g_norm3'], 'm_ffn2_w_gate': out['m_ffn2_w_gate'], 'm_ffn2_w_up': out['m_ffn2_w_up'], 'm_ffn2_w_down': out['m_ffn2_w_down'], 'v_w_ada': out['v_w_ada'], 'v_b_ada': out['v_b_ada'], 'v_g_norm1': out['v_g_norm1'], 'v_ffn1_w_gate': out['v_ffn1_w_gate'], 'v_ffn1_w_up': out['v_ffn1_w_up'], 'v_ffn1_w_down': out['v_ffn1_w_down'], 'v_g_norm2': out['v_g_norm2'], 'v_w_in': out['v_w_in'], 'v_g_sgu_ln': out['v_g_sgu_ln'], 'v_b_sgu_ln': out['v_b_sgu_ln'], 'v_w_spatial': out['v_w_spatial'], 'v_b_spatial': out['v_b_spatial'], 'v_g_q': out['v_g_q'], 'v_g_k': out['v_g_k'], 'v_attn_sinks': out['v_attn_sinks'], 'v_w_branch_a': out['v_w_branch_a'], 'v_w_branch_b': out['v_w_branch_b'], 'v_w_out': out['v_w_out'], 'v_g_norm3': out['v_g_norm3'], 'v_ffn2_w_gate': out['v_ffn2_w_gate'], 'v_ffn2_w_up': out['v_ffn2_w_up'], 'v_ffn2_w_down': out['v_ffn2_w_down']}


def _loss(weights, diff, rest, loss_target):
    with _jax.named_scope("forward"):
        args = {**rest, TWIN_DIFF_INPUT: diff, **{k: w.astype(_WEIGHT_DTYPES[k]) for k, w in weights.items()}}
        y = _forward(args)
    with _jax.named_scope("loss_head"):
        err = _jnp.square(y.astype(_jnp.float32) - loss_target)
        return 0.5 * _jnp.sum(_jnp.mean(err, axis=-1)) if err.ndim else 0.5 * err


def _adamw(w, g, m, v):
    m = ADAM_B1 * m + (1.0 - ADAM_B1) * g
    v = ADAM_B2 * v + (1.0 - ADAM_B2) * _jnp.square(g)
    m_hat = m / (1.0 - ADAM_B1 ** ADAM_STEP)
    v_hat = v / (1.0 - ADAM_B2 ** ADAM_STEP)
    delta = -ADAM_LR * (m_hat / (_jnp.sqrt(v_hat) + ADAM_EPS) + ADAM_WD * w)
    return delta, m, v


def reference(x, c, w_ada, b_ada, g_norm1, ffn1_w_gate, ffn1_w_up, ffn1_w_down, g_norm2, w_in, g_sgu_ln, b_sgu_ln, w_spatial, b_spatial, g_q, g_k, attn_sinks, w_branch_a, w_branch_b, w_out, g_norm3, ffn2_w_gate, ffn2_w_up, ffn2_w_down, loss_target, m_w_ada, m_b_ada, m_g_norm1, m_ffn1_w_gate, m_ffn1_w_up, m_ffn1_w_down, m_g_norm2, m_w_in, m_g_sgu_ln, m_b_sgu_ln, m_w_spatial, m_b_spatial, m_g_q, m_g_k, m_attn_sinks, m_w_branch_a, m_w_branch_b, m_w_out, m_g_norm3, m_ffn2_w_gate, m_ffn2_w_up, m_ffn2_w_down, v_w_ada, v_b_ada, v_g_norm1, v_ffn1_w_gate, v_ffn1_w_up, v_ffn1_w_down, v_g_norm2, v_w_in, v_g_sgu_ln, v_b_sgu_ln, v_w_spatial, v_b_spatial, v_g_q, v_g_k, v_attn_sinks, v_w_branch_a, v_w_branch_b, v_w_out, v_g_norm3, v_ffn2_w_gate, v_ffn2_w_up, v_ffn2_w_down):
    given = dict(x=x, c=c, w_ada=w_ada, b_ada=b_ada, g_norm1=g_norm1, ffn1_w_gate=ffn1_w_gate, ffn1_w_up=ffn1_w_up, ffn1_w_down=ffn1_w_down, g_norm2=g_norm2, w_in=w_in, g_sgu_ln=g_sgu_ln, b_sgu_ln=b_sgu_ln, w_spatial=w_spatial, b_spatial=b_spatial, g_q=g_q, g_k=g_k, attn_sinks=attn_sinks, w_branch_a=w_branch_a, w_branch_b=w_branch_b, w_out=w_out, g_norm3=g_norm3, ffn2_w_gate=ffn2_w_gate, ffn2_w_up=ffn2_w_up, ffn2_w_down=ffn2_w_down, loss_target=loss_target, m_w_ada=m_w_ada, m_b_ada=m_b_ada, m_g_norm1=m_g_norm1, m_ffn1_w_gate=m_ffn1_w_gate, m_ffn1_w_up=m_ffn1_w_up, m_ffn1_w_down=m_ffn1_w_down, m_g_norm2=m_g_norm2, m_w_in=m_w_in, m_g_sgu_ln=m_g_sgu_ln, m_b_sgu_ln=m_b_sgu_ln, m_w_spatial=m_w_spatial, m_b_spatial=m_b_spatial, m_g_q=m_g_q, m_g_k=m_g_k, m_attn_sinks=m_attn_sinks, m_w_branch_a=m_w_branch_a, m_w_branch_b=m_w_branch_b, m_w_out=m_w_out, m_g_norm3=m_g_norm3, m_ffn2_w_gate=m_ffn2_w_gate, m_ffn2_w_up=m_ffn2_w_up, m_ffn2_w_down=m_ffn2_w_down, v_w_ada=v_w_ada, v_b_ada=v_b_ada, v_g_norm1=v_g_norm1, v_ffn1_w_gate=v_ffn1_w_gate, v_ffn1_w_up=v_ffn1_w_up, v_ffn1_w_down=v_ffn1_w_down, v_g_norm2=v_g_norm2, v_w_in=v_w_in, v_g_sgu_ln=v_g_sgu_ln, v_b_sgu_ln=v_b_sgu_ln, v_w_spatial=v_w_spatial, v_b_spatial=v_b_spatial, v_g_q=v_g_q, v_g_k=v_g_k, v_attn_sinks=v_attn_sinks, v_w_branch_a=v_w_branch_a, v_w_branch_b=v_w_branch_b, v_w_out=v_w_out, v_g_norm3=v_g_norm3, v_ffn2_w_gate=v_ffn2_w_gate, v_ffn2_w_up=v_ffn2_w_up, v_ffn2_w_down=v_ffn2_w_down)
    weights = {n: given[n] for n in TWIN_WEIGHTS}
    shared = {n: given[n] for n in SHARED_INPUTS}
    per_example = {n: given[n] for n in ['x', 'c']}
    grad_fn = _jax.value_and_grad(_loss, argnums=(0, 1))

    def one_microbatch(ex, loss_target):
        ex = dict(ex)
        diff = ex.pop(TWIN_DIFF_INPUT)
        return grad_fn(weights, diff, {**shared, **ex}, loss_target)

    if N_MICROBATCH == 1:
        loss, (grad_w, grad_x) = one_microbatch(per_example, given["loss_target"])
    else:
        def body(carry, xs):
            loss_sum, grad_sum = carry
            l_k, (gw_k, gx_k) = one_microbatch(xs[0], xs[1])
            with _jax.named_scope("update"):
                return (loss_sum + l_k, _jax.tree.map(_jnp.add, grad_sum, gw_k)), gx_k

        init = (_jnp.zeros((), _jnp.float32), _jax.tree.map(_jnp.zeros_like, weights))
        (loss, grad_w), grad_x = _jax.lax.scan(body, init, (per_example, given["loss_target"]))
    with _jax.named_scope("update"):
        delta_w, new_m, new_v = {}, {}, {}
        for n in TWIN_WEIGHTS:
            delta_w[n], new_m[n], new_v[n] = _adamw(weights[n], grad_w[n], given["m_" + n], given["v_" + n])
    return (loss, grad_x, *[grad_w[n] for n in TWIN_WEIGHTS], *[delta_w[n] for n in TWIN_WEIGHTS],
            *[new_m[n] for n in TWIN_WEIGHTS], *[new_v[n] for n in TWIN_WEIGHTS])
```

```python
import functools
import math

import jax
import jax.numpy as jnp
from jax import lax
from jax.experimental import pallas as pl
from jax.experimental.pallas import tpu as pltpu

F32 = jnp.float32
BF16 = jnp.bfloat16
MESH = pl.DeviceIdType.MESH
AXES = ("x", "y", "c")
N_DEV = 8

VMEM_LIMIT = 56 * 1024 * 1024

D = 1024
FF = 2816
FC = 1408
D_A = 512
D_B = 512
HD = 64
N_KV = 2
Q_PER_KV = 4
BLK = 128
N_GRP = 4
IN_COLS = 3840
PIECES = (("au", 0, 512), ("av", 512, 512), ("q", 1024, 512), ("k", 1536, 128), ("v", 1664, 128),
          ("ga", 1792, 1024), ("gb", 2816, 1024))
EPS = 1e-6
NEG = -1e30
GELU_C = math.sqrt(2.0 / math.pi)

ADAM_LR = 0.001
ADAM_B1 = 0.9
ADAM_B2 = 0.999
ADAM_EPS = 1e-08
ADAM_WD = 0.01
ADAM_STEP = 10

NT = (((1,), (1,)), ((), ()))
TN = (((0,), (0,)), ((), ()))

BIG = (("ffn1_w_gate", 1, D, FF), ("ffn1_w_up", 1, D, FF), ("ffn1_w_down", 0, FF, D),
       ("w_in", 1, D, IN_COLS), ("w_branch_a", 1, D_A, D), ("w_branch_b", 1, D_B, D), ("w_out", 0, D, D),
       ("ffn2_w_gate", 1, D, FF), ("ffn2_w_up", 1, D, FF), ("ffn2_w_down", 0, FF, D))
PACK_ROWS = sum(r * c // N_DEV // 1024 for _, _, r, c in BIG)
SMALL = ("b_ada", "g_norm1", "g_norm2", "g_sgu_ln", "b_sgu_ln", "w_spatial", "b_spatial", "g_q", "g_k",
         "attn_sinks", "g_norm3")


def _dot(a, b):
    return jnp.dot(a, b, preferred_element_type=F32)


def _dot_nt(a, b):
    return lax.dot_general(a, b, NT, preferred_element_type=F32)


def _dot_tn(a, b):
    return lax.dot_general(a, b, TN, preferred_element_type=F32)


def _vmem():
    return pl.BlockSpec(memory_space=pltpu.VMEM)


def _any():
    return pl.BlockSpec(memory_space=pl.ANY)


def _rms_mod(h, g, sh, sc):
    inv = lax.rsqrt(jnp.mean(h * h, axis=-1, keepdims=True) + EPS)
    r = h * inv
    return (r * g) * (1.0 + sc) + sh, r, inv


def _rms_mod_bwd(dxn, r, inv, g, sc):
    dr = dxn * (g * (1.0 + sc))
    dh = inv * (dr - r * jnp.mean(dr * r, axis=-1, keepdims=True))
    return dh, jnp.sum(dxn, axis=0, keepdims=True), jnp.sum(dxn * r, axis=0, keepdims=True)


def _gelu(x):
    t = jnp.tanh(GELU_C * (x + 0.044715 * (x * x * x)))
    return 0.5 * x * (1.0 + t), t


def _gelu_grad(x, t):
    return 0.5 * (1.0 + t) + 0.5 * x * (1.0 - t * t) * (GELU_C * (1.0 + 3.0 * 0.044715 * x * x))


def _adamw(w, g, m, v):
    m = ADAM_B1 * m + (1.0 - ADAM_B1) * g
    v = ADAM_B2 * v + (1.0 - ADAM_B2) * (g * g)
    m_hat = m / (1.0 - ADAM_B1 ** ADAM_STEP)
    v_hat = v / (1.0 - ADAM_B2 ** ADAM_STEP)
    delta = -ADAM_LR * (m_hat / (jnp.sqrt(v_hat) + ADAM_EPS) + ADAM_WD * w)
    return delta, m, v


def _token_tile(seq, cap=512):
    return min(cap, seq)


def _params(*semantics):
    return pltpu.CompilerParams(dimension_semantics=semantics, vmem_limit_bytes=VMEM_LIMIT)


def _ffn_fwd(h, mods, gn, wg, wu, wd, row0, seq, tgt=None):
    T = h.shape[0]
    tm = _token_tile(seq)
    tps = seq // tm
    n_t = T // tm
    with_loss = tgt is not None

    def body(h_ref, m_ref, g_ref, wg_ref, wu_ref, wd_ref, *rest):
        if with_loss:
            tgt_ref, out_ref, y_ref, loss_ref = rest
        else:
            out_ref, y_ref = rest
        hv = h_ref[...]
        sh = m_ref[0, row0:row0 + 1, :]
        sc = m_ref[0, row0 + 1:row0 + 2, :]
        ga = m_ref[0, row0 + 2:row0 + 3, :]
        xn, _, _ = _rms_mod(hv, g_ref[...], sh, sc)
        xb = xn.astype(BF16)
        acc = jnp.zeros((tm, D), F32)
        for c0 in range(0, FF, FC):
            gg = _dot(xb, wg_ref[:, c0:c0 + FC])
            uu = _dot(xb, wu_ref[:, c0:c0 + FC])
            a = (gg * jax.nn.sigmoid(gg)) * uu
            acc = acc + _dot(a.astype(BF16), wd_ref[c0:c0 + FC, :])
        y_ref[...] = acc
        hout = hv + (0.5 * ga) * acc
        if with_loss:
            d = hout - tgt_ref[...]
            out_ref[...] = d * (1.0 / D)
            loss_ref[...] = jnp.full((1, 8, 128), 0.5 / D, F32) * jnp.sum(d * d)
        else:
            out_ref[...] = hout

    tok = pl.BlockSpec((tm, D), lambda i: (i, 0))
    in_specs = [tok, pl.BlockSpec((1, 9, D), lambda i: (i // tps, 0, 0)), pl.BlockSpec((1, D), lambda i: (0, 0)),
                _vmem(), _vmem(), _vmem()]
    out_shape = [jax.ShapeDtypeStruct((T, D), F32), jax.ShapeDtypeStruct((T, D), F32)]
    out_specs = [tok, tok]
    args = [h, mods, gn, wg, wu, wd]
    if with_loss:
        in_specs.append(tok)
        args.append(tgt)
        out_shape.append(jax.ShapeDtypeStruct((n_t, 8, 128), F32))
        out_specs.append(pl.BlockSpec((1, 8, 128), lambda i: (i, 0, 0)))
    return pl.pallas_call(
        body, name="ffn_fwd_loss" if with_loss else "ffn_fwd", grid=(n_t,), in_specs=in_specs, out_specs=out_specs,
        out_shape=out_shape, compiler_params=_params("parallel"),
    )(*args)


def _ffn_bwd(h, dhn, y, mods, gn, wg, wu, wd, row0, seq):
    T = h.shape[0]
    B = T // seq
    tm = _token_tile(seq, 256)
    tps = seq // tm
    n_t = T // tm

    def body(h_ref, dhn_ref, y_ref, m_ref, g_ref, wg_ref, wu_ref, wd_ref,
             dh_ref, xb_ref, dyb_ref, a_ref, dg_ref, du_ref, mg_ref):
        i = pl.program_id(0)
        hv = h_ref[...]
        dhn = dhn_ref[...]
        sh = m_ref[0, row0:row0 + 1, :]
        sc = m_ref[0, row0 + 1:row0 + 2, :]
        ga = m_ref[0, row0 + 2:row0 + 3, :]
        g = g_ref[...]
        xn, r, inv = _rms_mod(hv, g, sh, sc)
        xb = xn.astype(BF16)
        xb_ref[...] = xb
        dyb = ((0.5 * ga) * dhn).astype(BF16)
        dyb_ref[...] = dyb
        dga = 0.5 * jnp.sum(dhn * y_ref[...], axis=0, keepdims=True)
        dxn = jnp.zeros((tm, D), F32)
        for c0 in range(0, FF, FC):
            wgc = wg_ref[:, c0:c0 + FC]
            wuc = wu_ref[:, c0:c0 + FC]
            gg = _dot(xb, wgc)
            uu = _dot(xb, wuc)
            sig = jax.nn.sigmoid(gg)
            s = gg * sig
            a_ref[:, c0:c0 + FC] = (s * uu).astype(BF16)
            da = _dot_nt(dyb, wd_ref[c0:c0 + FC, :])
            dub = (da * s).astype(BF16)
            dgb = (da * uu * (sig * (1.0 + gg * (1.0 - sig)))).astype(BF16)
            dg_ref[:, c0:c0 + FC] = dgb
            du_ref[:, c0:c0 + FC] = dub
            dxn = dxn + _dot_nt(dgb, wgc) + _dot_nt(dub, wuc)
        dh, s_dxn, s_dxr = _rms_mod_bwd(dxn, r, inv, g, sc)
        dh_ref[...] = dhn + dh

        @pl.when(i % tps == 0)
        def _():
            mg_ref[...] = jnp.zeros(mg_ref.shape, F32)

        mg_ref[0, 0:1, :] += s_dxn
        mg_ref[0, 1:2, :] += s_dxr
        mg_ref[0, 2:3, :] += dga

    tok = pl.BlockSpec((tm, D), lambda i: (i, 0))
    tokf = pl.BlockSpec((tm, FF), lambda i: (i, 0))
    return pl.pallas_call(
        body, name="ffn_bwd", grid=(n_t,),
        in_specs=[tok, tok, tok, pl.BlockSpec((1, 9, D), lambda i: (i // tps, 0, 0)),
                  pl.BlockSpec((1, D), lambda i: (0, 0)), _vmem(), _vmem(), _vmem()],
        out_specs=[tok, tok, tok, tokf, tokf, tokf, pl.BlockSpec((1, 8, D), lambda i: (i // tps, 0, 0))],
        out_shape=[jax.ShapeDtypeStruct((T, D), F32), jax.ShapeDtypeStruct((T, D), BF16),
                   jax.ShapeDtypeStruct((T, D), BF16), jax.ShapeDtypeStruct((T, FF), BF16),
                   jax.ShapeDtypeStruct((T, FF), BF16), jax.ShapeDtypeStruct((T, FF), BF16),
                   jax.ShapeDtypeStruct((B, 8, D), F32)],
        compiler_params=_params("arbitrary"),
    )(h, dhn, y, mods, gn, wg, wu, wd)


def _wgrad(a, b, name):
    T, da = a.shape
    db = b.shape[1]
    bm = {1024: 1024, 2816: 1408, 512: 512}[da]
    bn = {1024: 1024, 2816: 1408, 3840: 1280}[db]
    tk = min(1024, T)
    nk = T // tk

    def body(a_ref, b_ref, o_ref):
        @pl.when(pl.program_id(2) == 0)
        def _():
            o_ref[...] = jnp.zeros(o_ref.shape, F32)

        o_ref[...] += _dot_tn(a_ref[...], b_ref[...])

    return pl.pallas_call(
        body, name=name, grid=(da // bm, db // bn, nk),
        in_specs=[pl.BlockSpec((tk, bm), lambda i, j, k: (k, i)), pl.BlockSpec((tk, bn), lambda i, j, k: (k, j))],
        out_specs=pl.BlockSpec((bm, bn), lambda i, j, k: (i, j)),
        out_shape=jax.ShapeDtypeStruct((da, db), F32),
        compiler_params=_params("parallel", "parallel", "arbitrary"),
    )(a, b)


def _inproj_fwd(h, mods, gn, w_in, seq):
    T = h.shape[0]
    tm = _token_tile(seq)
    tps = seq // tm

    def body(h_ref, m_ref, g_ref, w_ref, *outs):
        xn, _, _ = _rms_mod(h_ref[...], g_ref[...], m_ref[0, 3:4, :], m_ref[0, 4:5, :])
        xb = xn.astype(BF16)
        for (_, c0, w), o_ref in zip(PIECES, outs):
            o_ref[...] = _dot(xb, w_ref[:, c0:c0 + w])

    return pl.pallas_call(
        body, name="inproj_fwd", grid=(T // tm,),
        in_specs=[pl.BlockSpec((tm, D), lambda i: (i, 0)), pl.BlockSpec((1, 9, D), lambda i: (i // tps, 0, 0)),
                  pl.BlockSpec((1, D), lambda i: (0, 0)), _vmem()],
        out_specs=[pl.BlockSpec((tm, w), lambda i: (i, 0)) for _, _, w in PIECES],
        out_shape=[jax.ShapeDtypeStruct((T, w), F32) for _, _, w in PIECES],
        compiler_params=_params("parallel"),
    )(h, mods, gn, w_in)


def _inproj_bwd(h, dh_res, dpieces, mods, gn, w_in, seq):
    T = h.shape[0]
    B = T // seq
    tm = _token_tile(seq, 256)
    tps = seq // tm

    def body(h_ref, dres_ref, *rest):
        dp_refs = rest[:len(PIECES)]
        m_ref, g_ref, w_ref, dh_ref, xb_ref, dpb_ref, mg_ref = rest[len(PIECES):]
        i = pl.program_id(0)
        g = g_ref[...]
        sc = m_ref[0, 4:5, :]
        xn, r, inv = _rms_mod(h_ref[...], g, m_ref[0, 3:4, :], sc)
        xb_ref[...] = xn.astype(BF16)
        dxn = jnp.zeros((tm, D), F32)
        for (_, c0, w), dp_ref in zip(PIECES, dp_refs):
            dpb = dp_ref[...].astype(BF16)
            dpb_ref[:, c0:c0 + w] = dpb
            dxn = dxn + _dot_nt(dpb, w_ref[:, c0:c0 + w])
        dh, s_dxn, s_dxr = _rms_mod_bwd(dxn, r, inv, g, sc)
        dh_ref[...] = dres_ref[...] + dh

        @pl.when(i % tps == 0)
        def _():
            mg_ref[...] = jnp.zeros(mg_ref.shape, F32)

        mg_ref[0, 0:1, :] += s_dxn
        mg_ref[0, 1:2, :] += s_dxr

    tok = pl.BlockSpec((tm, D), lambda i: (i, 0))
    return pl.pallas_call(
        body, name="inproj_bwd", grid=(T // tm,),
        in_specs=[tok, tok] + [pl.BlockSpec((tm, w), lambda i: (i, 0)) for _, _, w in PIECES]
        + [pl.BlockSpec((1, 9, D), lambda i: (i // tps, 0, 0)), pl.BlockSpec((1, D), lambda i: (0, 0)), _vmem()],
        out_specs=[tok, tok, pl.BlockSpec((tm, IN_COLS), lambda i: (i, 0)),
                   pl.BlockSpec((1, 8, D), lambda i: (i // tps, 0, 0))],
        out_shape=[jax.ShapeDtypeStruct((T, D), F32), jax.ShapeDtypeStruct((T, D), BF16),
                   jax.ShapeDtypeStruct((T, IN_COLS), BF16), jax.ShapeDtypeStruct((B, 8, D), F32)],
        compiler_params=_params("arbitrary"),
    )(h, dh_res, *dpieces, mods, gn, w_in)


def _attn_math(q, kp, kc, vp, vc, gq, gk, sink, first):
    def rms(x, g):
        return (x * lax.rsqrt(jnp.mean(x * x, axis=-1, keepdims=True) + EPS)) * g

    qn = rms(q, gq)
    kn = rms(jnp.concatenate([kp, kc], axis=0), gk)
    v = jnp.concatenate([vp, vc], axis=0)
    s = _dot_nt(qn.astype(BF16), kn.astype(BF16)) * (HD ** -0.5)
    nq = Q_PER_KV * BLK
    row = lax.broadcasted_iota(jnp.int32, (nq, 2 * BLK), 0) & (BLK - 1)
    col = lax.broadcasted_iota(jnp.int32, (nq, 2 * BLK), 1)
    lo = jnp.where(first, BLK, 0)
    valid = (col <= row + BLK) & (col > row) & (col >= lo)
    s = jnp.where(valid, s, NEG)
    m = lax.stop_gradient(jnp.maximum(jnp.max(s, axis=-1, keepdims=True), sink))
    p = jnp.exp(s - m)
    den = jnp.sum(p, axis=-1, keepdims=True) + jnp.exp(sink - m)
    return _dot((p / den).astype(BF16), v.astype(BF16))


def _attn_specs(nbs):
    nq = Q_PER_KV * BLK
    qs = pl.BlockSpec((1, 1, nq, HD), lambda h, b: (h, b, 0, 0))
    prev = pl.BlockSpec((1, 1, BLK, HD), lambda h, b: (h, jnp.maximum(b - 1, 0), 0, 0))
    cur = pl.BlockSpec((1, 1, BLK, HD), lambda h, b: (h, b, 0, 0))
    gs = pl.BlockSpec((1, HD), lambda h, b: (0, 0))
    sk = pl.BlockSpec((1, nq, 128), lambda h, b: (h, 0, 0))
    return qs, prev, cur, gs, sk


def _attn_fwd(q_hm, k_hm, v_hm, gq, gk, sink_b, nbs):
    nblk = q_hm.shape[1]
    qs, prev, cur, gs, sk = _attn_specs(nbs)

    def body(q_ref, kp_ref, kc_ref, vp_ref, vc_ref, gq_ref, gk_ref, sk_ref, o_ref):
        first = (pl.program_id(1) % nbs) == 0
        o_ref[0, 0] = _attn_math(q_ref[0, 0], kp_ref[0, 0], kc_ref[0, 0], vp_ref[0, 0], vc_ref[0, 0],
                                 gq_ref[...], gk_ref[...], sk_ref[0, :, 0:1], first)

    return pl.pallas_call(
        body, name="attn_fwd", grid=(N_KV, nblk), in_specs=[qs, prev, cur, prev, cur, gs, gs, sk], out_specs=qs,
        out_shape=jax.ShapeDtypeStruct(q_hm.shape, F32),
        compiler_params=_params("parallel", "parallel"),
    )(q_hm, k_hm, k_hm, v_hm, v_hm, gq, gk, sink_b)


def _attn_bwd(q_hm, k_hm, v_hm, gq, gk, sink_b, do_hm, nbs):
    nblk = q_hm.shape[1]
    qs, prev, cur, gs, sk = _attn_specs(nbs)
    nq = Q_PER_KV * BLK

    def body(q_ref, kp_ref, kc_ref, vp_ref, vc_ref, gq_ref, gk_ref, sk_ref, do_ref,
             dq_ref, dkp_ref, dkc_ref, dvp_ref, dvc_ref, dgq_ref, dgk_ref, dsk_ref):
        h = pl.program_id(0)
        b = pl.program_id(1)
        first = (b % nbs) == 0
        f = functools.partial(_attn_math, first=first)
        _, vjp = jax.vjp(f, q_ref[0, 0], kp_ref[0, 0], kc_ref[0, 0], vp_ref[0, 0], vc_ref[0, 0],
                         gq_ref[...], gk_ref[...], sk_ref[0, :, 0:1])
        dq, dkp, dkc, dvp, dvc, dgq, dgk, dsk = vjp(do_ref[0, 0])
        dq_ref[0, 0] = dq
        dkp_ref[0, 0] = dkp
        dkc_ref[0, 0] = dkc
        dvp_ref[0, 0] = dvp
        dvc_ref[0, 0] = dvc

        @pl.when((h == 0) & (b == 0))
        def _():
            for r in (dgq_ref, dgk_ref, dsk_ref):
                r[...] = jnp.zeros(r.shape, F32)

        dgq_ref[...] += dgq
        dgk_ref[...] += dgk
        lane = lax.broadcasted_iota(jnp.int32, (8, 128), 1)
        upd = jnp.zeros((8, 128), F32)
        for g in range(Q_PER_KV):
            tot = jnp.sum(dsk[g * BLK:(g + 1) * BLK, :])
            upd = upd + jnp.where(lane == h * Q_PER_KV + g, tot, 0.0)
        dsk_ref[...] += upd

    acc = pl.BlockSpec((1, HD), lambda h, b: (0, 0))
    return pl.pallas_call(
        body, name="attn_bwd", grid=(N_KV, nblk), in_specs=[qs, prev, cur, prev, cur, gs, gs, sk, qs],
        out_specs=[qs, cur, cur, cur, cur, acc, acc, pl.BlockSpec((8, 128), lambda h, b: (0, 0))],
        out_shape=[jax.ShapeDtypeStruct(q_hm.shape, F32)] + [jax.ShapeDtypeStruct(k_hm.shape, F32)] * 4
        + [jax.ShapeDtypeStruct((1, HD), F32)] * 2 + [jax.ShapeDtypeStruct((8, 128), F32)],
        compiler_params=_params("arbitrary", "arbitrary"),
    )(q_hm, k_hm, k_hm, v_hm, v_hm, gq, gk, sink_b, do_hm)


def _to_heads(q, k, v):
    nblk = q.shape[0] // BLK
    q_hm = q.reshape(nblk, BLK, N_KV, Q_PER_KV, HD).transpose(2, 0, 3, 1, 4).reshape(N_KV, nblk, Q_PER_KV * BLK, HD)
    k_hm = k.reshape(nblk, BLK, N_KV, HD).transpose(2, 0, 1, 3)
    v_hm = v.reshape(nblk, BLK, N_KV, HD).transpose(2, 0, 1, 3)
    return q_hm, k_hm, v_hm


def _q_from_heads(q_hm):
    nblk = q_hm.shape[1]
    return q_hm.reshape(N_KV, nblk, Q_PER_KV, BLK, HD).transpose(1, 3, 0, 2, 4).reshape(nblk * BLK, N_KV * Q_PER_KV * HD)


def _kv_from_heads(k_hm):
    nblk = k_hm.shape[1]
    return k_hm.transpose(1, 2, 0, 3).reshape(nblk * BLK, N_KV * HD)


def _sgu_norm(av, g_ln, b_ln):
    t, th = _gelu(av)
    mu = jnp.mean(t, axis=-1, keepdims=True)
    tc = t - mu
    rstd = lax.rsqrt(jnp.mean(tc * tc, axis=-1, keepdims=True) + EPS)
    vhat = tc * rstd
    return vhat * g_ln + b_ln, vhat, rstd, th


def _masked_ws(ws_ref):
    tril = lax.broadcasted_iota(jnp.int32, (BLK, BLK), 0) >= lax.broadcasted_iota(jnp.int32, (BLK, BLK), 1)
    return [jnp.where(tril, ws_ref[g], 0.0).astype(BF16) for g in range(N_GRP)]


def _mix_fwd(au, av, gta, gtb, ob, h, mods, g_ln, b_ln, ws, bsb, wa, wb, wout, seq):
    T = h.shape[0]
    tm = _token_tile(seq)
    tps = seq // tm

    def body(au_ref, av_ref, gta_ref, gtb_ref, ob_ref, h_ref, m_ref, gl_ref, bl_ref, ws_ref, bs_ref,
             wa_ref, wb_ref, wo_ref, out_ref, vvb_s, z_s):
        u, _ = _gelu(au_ref[...])
        vv, _, _, _ = _sgu_norm(av_ref[...], gl_ref[...], bl_ref[...])
        vvb_s[...] = vv.astype(BF16)
        wsm = _masked_ws(ws_ref)
        for c in range(tm // BLK):
            rows = slice(c * BLK, (c + 1) * BLK)
            for g in range(N_GRP):
                cols = slice(g * BLK, (g + 1) * BLK)
                z_s[rows, cols] = _dot(wsm[g], vvb_s[rows, cols]) + bs_ref[g]
        ya = _dot((u * z_s[...]).astype(BF16), wa_ref[...])
        yb = _dot(ob_ref[...].astype(BF16), wb_ref[...])
        merged = jax.nn.sigmoid(gta_ref[...]) * ya + jax.nn.sigmoid(gtb_ref[...]) * yb
        out_ref[...] = h_ref[...] + m_ref[0, 5:6, :] * _dot(merged.astype(BF16), wo_ref[...])

    def tok(w):
        return pl.BlockSpec((tm, w), lambda i: (i, 0))

    def full(shape):
        return pl.BlockSpec(shape, lambda i: (0,) * len(shape))

    return pl.pallas_call(
        body, name="mix_fwd", grid=(T // tm,),
        in_specs=[tok(D_A), tok(D_A), tok(D), tok(D), tok(D_B), tok(D),
                  pl.BlockSpec((1, 9, D), lambda i: (i // tps, 0, 0)), full((1, D_A)), full((1, D_A)),
                  full((N_GRP, BLK, BLK)), full((N_GRP, BLK, BLK)), _vmem(), _vmem(), _vmem()],
        out_specs=tok(D), out_shape=jax.ShapeDtypeStruct((T, D), F32),
        scratch_shapes=[pltpu.VMEM((tm, D_A), BF16), pltpu.VMEM((tm, D_A), F32)],
        compiler_params=_params("parallel"),
    )(au, av, gta, gtb, ob, h, mods, g_ln, b_ln, ws, bsb, wa, wb, wout)


def _mix_bwd(au, av, gta, gtb, ob, dh, mods, g_ln, b_ln, ws, bsb, wa, wb, wout, seq):
    T = dh.shape[0]
    B = T // seq
    tm = _token_tile(seq, 256)
    tps = seq // tm

    def body(au_ref, av_ref, gta_ref, gtb_ref, ob_ref, dh_ref, m_ref, gl_ref, bl_ref, ws_ref, bs_ref,
             wa_ref, wb_ref, wo_ref,
             dau_ref, dav_ref, dgta_ref, dgtb_ref, dob_ref, dwo_ref, dwa_ref, dwb_ref, dws_ref, dbs_ref, dln_ref,
             mg_ref, vvb_s, z_s, dz_s, dzb_s, dvv_s):
        i = pl.program_id(0)

        @pl.when(i == 0)
        def _():
            for r in (dwo_ref, dwa_ref, dwb_ref, dws_ref, dbs_ref, dln_ref):
                r[...] = jnp.zeros(r.shape, F32)

        @pl.when(i % tps == 0)
        def _():
            mg_ref[...] = jnp.zeros(mg_ref.shape, F32)

        auv = au_ref[...]
        avv = av_ref[...]
        u, thu = _gelu(auv)
        g_ln = gl_ref[...]
        vv, vhat, rstd, thv = _sgu_norm(avv, g_ln, bl_ref[...])
        vvb_s[...] = vv.astype(BF16)
        wsm = _masked_ws(ws_ref)
        for c in range(tm // BLK):
            rows = slice(c * BLK, (c + 1) * BLK)
            for g in range(N_GRP):
                cols = slice(g * BLK, (g + 1) * BLK)
                z_s[rows, cols] = _dot(wsm[g], vvb_s[rows, cols]) + bs_ref[g]
        z = z_s[...]
        yab = (u * z).astype(BF16)
        obb = ob_ref[...].astype(BF16)
        ya = _dot(yab, wa_ref[...])
        yb = _dot(obb, wb_ref[...])
        sa = jax.nn.sigmoid(gta_ref[...])
        sb = jax.nn.sigmoid(gtb_ref[...])
        mb = (sa * ya + sb * yb).astype(BF16)
        dhv = dh_ref[...]
        mg_ref[0, 0:1, :] += jnp.sum(dhv * _dot(mb, wo_ref[...]), axis=0, keepdims=True)
        dmob = (m_ref[0, 5:6, :] * dhv).astype(BF16)
        dwo_ref[...] += _dot_tn(mb, dmob)
        dmerged = _dot_nt(dmob, wo_ref[...])
        dya = dmerged * sa
        dyb = dmerged * sb
        dgta_ref[...] = dya * ya * (1.0 - sa)
        dgtb_ref[...] = dyb * yb * (1.0 - sb)
        dyab = dya.astype(BF16)
        dybb = dyb.astype(BF16)
        dwa_ref[...] += _dot_tn(yab, dyab)
        dwb_ref[...] += _dot_tn(obb, dybb)
        dob_ref[...] = _dot_nt(dybb, wb_ref[...])
        dyap = _dot_nt(dyab, wa_ref[...])
        dau_ref[...] = (dyap * z) * _gelu_grad(auv, thu)
        dz = dyap * u
        dz_s[...] = dz
        dzb_s[...] = dz.astype(BF16)
        for c in range(tm // BLK):
            rows = slice(c * BLK, (c + 1) * BLK)
            for g in range(N_GRP):
                cols = slice(g * BLK, (g + 1) * BLK)
                dzb = dzb_s[rows, cols]
                dvv_s[rows, cols] = _dot_tn(wsm[g], dzb)
                dws_ref[g] += _dot_nt(dzb, vvb_s[rows, cols])
                dbs_ref[g] += dz_s[rows, cols]
        dvv = dvv_s[...]
        dln_ref[0:1, :] += jnp.sum(dvv * vhat, axis=0, keepdims=True)
        dln_ref[1:2, :] += jnp.sum(dvv, axis=0, keepdims=True)
        dvh = dvv * g_ln
        dt = rstd * (dvh - jnp.mean(dvh, axis=-1, keepdims=True)
                     - vhat * jnp.mean(dvh * vhat, axis=-1, keepdims=True))
        dav_ref[...] = dt * _gelu_grad(avv, thv)

    def tok(w):
        return pl.BlockSpec((tm, w), lambda i: (i, 0))

    def full(shape):
        return pl.BlockSpec(shape, lambda i: (0,) * len(shape))

    return pl.pallas_call(
        body, name="mix_bwd", grid=(T // tm,),
        in_specs=[tok(D_A), tok(D_A), tok(D), tok(D), tok(D_B), tok(D),
                  pl.BlockSpec((1, 9, D), lambda i: (i // tps, 0, 0)), full((1, D_A)), full((1, D_A)),
                  full((N_GRP, BLK, BLK)), full((N_GRP, BLK, BLK)), _vmem(), _vmem(), _vmem()],
        out_specs=[tok(D_A), tok(D_A), tok(D), tok(D), tok(D_B), full((D, D)), full((D_A, D)), full((D_B, D)),
                   full((N_GRP, BLK, BLK)), full((N_GRP, BLK, BLK)), full((8, D_A)),
                   pl.BlockSpec((1, 8, D), lambda i: (i // tps, 0, 0))],
        out_shape=[jax.ShapeDtypeStruct((T, D_A), F32), jax.ShapeDtypeStruct((T, D_A), F32),
                   jax.ShapeDtypeStruct((T, D), F32), jax.ShapeDtypeStruct((T, D), F32),
                   jax.ShapeDtypeStruct((T, D_B), F32), jax.ShapeDtypeStruct((D, D), F32),
                   jax.ShapeDtypeStruct((D_A, D), F32), jax.ShapeDtypeStruct((D_B, D), F32),
                   jax.ShapeDtypeStruct((N_GRP, BLK, BLK), F32), jax.ShapeDtypeStruct((N_GRP, BLK, BLK), F32),
                   jax.ShapeDtypeStruct((8, D_A), F32), jax.ShapeDtypeStruct((B, 8, D), F32)],
        scratch_shapes=[pltpu.VMEM((tm, D_A), BF16), pltpu.VMEM((tm, D_A), F32), pltpu.VMEM((tm, D_A), F32),
                        pltpu.VMEM((tm, D_A), BF16), pltpu.VMEM((tm, D_A), F32)],
        compiler_params=_params("arbitrary"),
    )(au, av, gta, gtb, ob, dh, mods, g_ln, b_ln, ws, bsb, wa, wb, wout)


def _ada_fwd(c_all, w_ada):
    def body(c_ref, w_ref, o_ref):
        cv = c_ref[...]
        o_ref[...] = _dot((cv * jax.nn.sigmoid(cv)).astype(BF16), w_ref[...].astype(BF16))

    return pl.pallas_call(body, name="ada_fwd", in_specs=[_vmem(), _vmem()], out_specs=_vmem(),
                          out_shape=jax.ShapeDtypeStruct((c_all.shape[0], w_ada.shape[1]), F32))(c_all, w_ada)


def _ada_update(c_all, dm_cols, w, m, v):
    n, cols = c_all.shape[0], w.shape[1]

    def body(c_ref, dm_ref, w_ref, m_ref, v_ref, g_ref, d_ref, nm_ref, nv_ref):
        cv = c_ref[...]
        g = _dot_tn((cv * jax.nn.sigmoid(cv)).astype(BF16), dm_ref[...].astype(BF16))
        g_ref[...] = g
        d_ref[...], nm_ref[...], nv_ref[...] = _adamw(w_ref[...], g, m_ref[...], v_ref[...])

    col = pl.BlockSpec((D, 128), lambda j: (0, j))
    return pl.pallas_call(
        body, name="ada_update", grid=(cols // 128,),
        in_specs=[pl.BlockSpec((n, D), lambda j: (0, 0)), pl.BlockSpec((n, 128), lambda j: (0, j)), col, col, col],
        out_specs=[col] * 4, out_shape=[jax.ShapeDtypeStruct(w.shape, F32)] * 4, compiler_params=_params("parallel"),
    )(c_all, dm_cols, w, m, v)


def _mod_finish(mg1, mg2, mg2g, mg3, mods, g1, g2, g3):
    B = mods.shape[0]

    def body(mg1_ref, mg2_ref, mg2g_ref, mg3_ref, m_ref, g1_ref, g2_ref, g3_ref, dm_ref, dgn_ref):
        dgn_ref[...] = jnp.zeros(dgn_ref.shape, F32)
        for k, (mg, g_ref) in enumerate(((mg1_ref, g1_ref), (mg2_ref, g2_ref), (mg3_ref, g3_ref))):
            for b in range(B):
                s_dxr = mg[b, 1:2, :]
                dm_ref[b, 3 * k:3 * k + 1, :] = mg[b, 0:1, :]
                dm_ref[b, 3 * k + 1:3 * k + 2, :] = g_ref[...] * s_dxr
                dm_ref[b, 3 * k + 2:3 * k + 3, :] = mg2g_ref[b, 0:1, :] if k == 1 else mg[b, 2:3, :]
                dgn_ref[k:k + 1, :] += (1.0 + m_ref[b, 3 * k + 1:3 * k + 2, :]) * s_dxr

    return pl.pallas_call(body, name="mod_finish", in_specs=[_vmem()] * 8, out_specs=[_vmem()] * 2,
                          out_shape=[jax.ShapeDtypeStruct((B, 9, D), F32), jax.ShapeDtypeStruct((8, D), F32)],
                          )(mg1, mg2, mg2g, mg3, mods, g1, g2, g3)


def _small_update(gathered, params, ms, vs):
    n = len(SMALL)
    B = gathered[0].shape[1]

    def body(*refs):
        gdm, ggn, gln, gws, gbs, ggq, ggk, gsk = refs[:8]
        w = dict(zip(SMALL, refs[8:8 + n]))
        m = dict(zip(SMALL, refs[8 + n:8 + 2 * n]))
        v = dict(zip(SMALL, refs[8 + 2 * n:8 + 3 * n]))
        outs = refs[8 + 3 * n:]
        out = {name: outs[4 * k:4 * k + 4] for k, name in enumerate(SMALL)}

        def total(ref, idx):
            acc = ref[(0,) + idx]
            for dev in range(1, N_DEV):
                acc = acc + ref[(dev,) + idx]
            return acc

        def finish(name, g, idx=(Ellipsis,)):
            d, nm, nv = _adamw(w[name][idx], g, m[name][idx], v[name][idx])
            for ref, val in zip(out[name], (g, d, nm, nv)):
                ref[idx] = val

        g_bada = total(gdm, (slice(0, 1),))
        for b in range(1, B):
            g_bada = g_bada + total(gdm, (slice(b, b + 1),))
        finish("b_ada", g_bada)
        finish("g_norm1", total(ggn, (slice(0, 1),)))
        finish("g_norm2", total(ggn, (slice(1, 2),)))
        finish("g_norm3", total(ggn, (slice(2, 3),)))
        finish("g_sgu_ln", total(gln, (slice(0, 1),)))
        finish("b_sgu_ln", total(gln, (slice(1, 2),)))
        tril = lax.broadcasted_iota(jnp.int32, (BLK, BLK), 0) >= lax.broadcasted_iota(jnp.int32, (BLK, BLK), 1)
        for g in range(N_GRP):
            finish("w_spatial", jnp.where(tril, total(gws, (g,)), 0.0), (0, g))
            finish("b_spatial", jnp.sum(total(gbs, (g,)).T, axis=0, keepdims=True), (0, slice(g, g + 1)))
        finish("g_q", total(ggq, ()))
        finish("g_k", total(ggk, ()))
        finish("attn_sinks", total(gsk, (slice(0, 1), slice(0, N_KV * Q_PER_KV))))

    args = list(gathered) + [params[k] for k in SMALL] + [ms[k] for k in SMALL] + [vs[k] for k in SMALL]
    out_shape = []
    for k in SMALL:
        out_shape += [jax.ShapeDtypeStruct(params[k].shape, F32)] * 4
    res = pl.pallas_call(body, name="small_update", in_specs=[_vmem()] * len(args),
                         out_specs=[_vmem()] * len(out_shape), out_shape=out_shape)(*args)
    return {k: res[4 * i:4 * i + 4] for i, k in enumerate(SMALL)}


def _place():
    return lax.axis_index("x"), lax.axis_index("y"), lax.axis_index("c")


def _all_gather(arrays, name):
    n = len(arrays)

    def body(*refs):
        ins = refs[:n]
        outs = refs[n:2 * n]
        send_sems, recv_sems, local_sems = refs[2 * n:]
        x, y, c = _place()
        me, sibling = (x, y, c), (x, y, 1 - c)
        chips = [(1 - x, y), (x, 1 - y), (1 - x, 1 - y)]

        def slot(a, px, py, pc):
            return outs[a].at[4 * px + 2 * py + pc]

        def copy(a, k, block, to, src=None):
            return pltpu.make_async_remote_copy(
                src_ref=slot(a, *block) if src is None else src, dst_ref=slot(a, *block),
                send_sem=send_sems.at[a, k], recv_sem=recv_sems.at[a, k], device_id=to, device_id_type=MESH)

        mine = [pltpu.make_async_copy(ins[a], slot(a, *me), local_sems.at[a]) for a in range(n)]
        first = []
        for a in range(n):
            mine[a].start()
            first.append(copy(a, 0, me, sibling, src=ins[a]))
            first += [copy(a, 1 + j, me, (*chip, c), src=ins[a]) for j, chip in enumerate(chips)]
        for cp in first:
            cp.start()
        passed = []
        for j, chip in enumerate(chips):
            for a in range(n):
                copy(a, 1 + j, (*chip, c), me).wait_recv()
                cp = copy(a, 4 + j, (*chip, c), sibling)
                cp.start()
                passed.append(cp)
        for a in range(n):
            copy(a, 0, sibling, me).wait_recv()
        for j, chip in enumerate(chips):
            for a in range(n):
                copy(a, 4 + j, (*chip, 1 - c), me).wait_recv()
        for cp in first + passed:
            cp.wait_send()
        for a in range(n):
            mine[a].wait()

    return pl.pallas_call(
        body, name=name, in_specs=[_any()] * n, out_specs=[_any()] * n,
        out_shape=[jax.ShapeDtypeStruct((N_DEV,) + a.shape, a.dtype) for a in arrays],
        scratch_shapes=[pltpu.SemaphoreType.DMA((n, 7)), pltpu.SemaphoreType.DMA((n, 7)),
                        pltpu.SemaphoreType.DMA((n,))],
    )(*arrays)


def _rs_sibling(p):
    def body(p_ref, r_ref, send_sems, recv_sems):
        x, y, c = _place()
        copies = []
        for q in range(4):
            copies.append(pltpu.make_async_remote_copy(
                src_ref=p_ref.at[2 * q + (1 - c)], dst_ref=r_ref.at[q], send_sem=send_sems.at[q],
                recv_sem=recv_sems.at[q], device_id=(x, y, 1 - c), device_id_type=MESH))
        for cp in copies:
            cp.start()
        for cp in copies:
            cp.wait()

    return pl.pallas_call(
        body, name="rs_sibling", in_specs=[_any()], out_specs=_any(),
        out_shape=jax.ShapeDtypeStruct((4,) + p.shape[1:], p.dtype),
        scratch_shapes=[pltpu.SemaphoreType.DMA((4,)), pltpu.SemaphoreType.DMA((4,))],
    )(p)


def _rs_add(p, r, core):
    rows = p.shape[1]

    def body(c_ref, p_ref, r_ref, o_ref):
        o_ref[...] = (p_ref[...] + r_ref[...]).astype(BF16)

    return pl.pallas_call(
        body, name="rs_add", out_shape=jax.ShapeDtypeStruct((4, rows, 1024), BF16),
        grid_spec=pltpu.PrefetchScalarGridSpec(
            num_scalar_prefetch=1, grid=(4, 8),
            in_specs=[pl.BlockSpec((1, rows, 128), lambda k, j, c_ref: (2 * k + c_ref[0], 0, j)),
                      pl.BlockSpec((1, rows, 128), lambda k, j, c_ref: (k, 0, j))],
            out_specs=pl.BlockSpec((1, rows, 128), lambda k, j, c_ref: (k, 0, j))),
        compiler_params=_params("parallel", "parallel"),
    )(core, p, r)


def _rs_chips(q):
    def body(q_ref, r_ref, send_sems, recv_sems, local_sem):
        x, y, c = _place()
        my_chip = 2 * x + y
        chips = [(1 - x, y), (x, 1 - y), (1 - x, 1 - y)]
        own = pltpu.make_async_copy(q_ref.at[my_chip], r_ref.at[my_chip], local_sem)
        own.start()
        copies = []
        for j, (px, py) in enumerate(chips):
            copies.append(pltpu.make_async_remote_copy(
                src_ref=q_ref.at[2 * px + py], dst_ref=r_ref.at[my_chip], send_sem=send_sems.at[j],
                recv_sem=recv_sems.at[j], device_id=(px, py, c), device_id_type=MESH))
        for cp in copies:
            cp.start()
        for j, (px, py) in enumerate(chips):
            pltpu.make_async_remote_copy(
                src_ref=q_ref.at[my_chip], dst_ref=r_ref.at[2 * px + py], send_sem=send_sems.at[j],
                recv_sem=recv_sems.at[j], device_id=(px, py, c), device_id_type=MESH).wait_recv()
        for cp in copies:
            cp.wait_send()
        own.wait()

    return pl.pallas_call(
        body, name="rs_chips", in_specs=[_any()], out_specs=_any(), out_shape=jax.ShapeDtypeStruct(q.shape, q.dtype),
        scratch_shapes=[pltpu.SemaphoreType.DMA((3,)), pltpu.SemaphoreType.DMA((3,)), pltpu.SemaphoreType.DMA],
    )(q)


def _rs_finish(r, w, m, v):
    rows = w.shape[0]

    def body(r_ref, w_ref, m_ref, v_ref, g_ref, d_ref, nm_ref, nv_ref):
        g = r_ref[0].astype(F32)
        for k in range(1, 4):
            g = g + r_ref[k].astype(F32)
        g_ref[...] = g
        d_ref[...], nm_ref[...], nv_ref[...] = _adamw(w_ref[...], g, m_ref[...], v_ref[...])

    col = pl.BlockSpec((rows, 128), lambda j: (0, j))
    return pl.pallas_call(
        body, name="rs_finish", grid=(8,), in_specs=[pl.BlockSpec((4, rows, 128), lambda j: (0, 0, j)), col, col, col],
        out_specs=[col] * 4, out_shape=[jax.ShapeDtypeStruct((rows, 1024), F32)] * 4,
        compiler_params=_params("parallel"),
    )(r, w, m, v)


def _pack_shards(shards):
    return jnp.concatenate([shards[name].reshape(-1, 1024) for name, _, _, _ in BIG], axis=0)


def _unpack_shards(packed):
    out, r0 = {}, 0
    for name, axis, rows, cols in BIG:
        n = rows * cols // N_DEV // 1024
        shape = (1, rows, cols // N_DEV) if axis == 1 else (1, rows // N_DEV, cols)
        out[name] = packed[r0:r0 + n].reshape(shape)
        r0 += n
    return out


def _unpack_full(gathered):
    out, r0 = {}, 0
    for name, axis, rows, cols in BIG:
        n = rows * cols // N_DEV // 1024
        part = gathered[:, r0:r0 + n]
        if axis == 1:
            out[name] = part.reshape(N_DEV, rows, cols // N_DEV).transpose(1, 0, 2).reshape(rows, cols)
        else:
            out[name] = part.reshape(rows, cols)
        r0 += n
    return out


def _pack_full(full):
    parts = []
    for name, axis, rows, cols in BIG:
        g = full[name]
        if axis == 1:
            g = g.reshape(rows, N_DEV, cols // N_DEV).transpose(1, 0, 2)
        parts.append(g.reshape(N_DEV, -1, 1024))
    return jnp.concatenate(parts, axis=1)


def _local_step(x, tgt, mods, wts, small, seq):
    T = x.shape[0]
    nbs = seq // BLK
    g1, g2, g3 = small["g_norm1"], small["g_norm2"], small["g_norm3"]
    g_ln, b_ln = small["g_sgu_ln"], small["b_sgu_ln"]
    ws = small["w_spatial"]
    bsb = jnp.broadcast_to(small["b_spatial"][:, :, None], (N_GRP, BLK, BLK))
    gq, gk = small["g_q"], small["g_k"]
    sink_b = jnp.broadcast_to(
        jnp.repeat(small["attn_sinks"].reshape(N_KV, Q_PER_KV), BLK, axis=1)[:, :, None], (N_KV, Q_PER_KV * BLK, 128))
    f1 = (wts["ffn1_w_gate"], wts["ffn1_w_up"], wts["ffn1_w_down"])
    f2 = (wts["ffn2_w_gate"], wts["ffn2_w_up"], wts["ffn2_w_down"])
    mixw = (wts["w_branch_a"], wts["w_branch_b"], wts["w_out"])

    h1, y1 = _ffn_fwd(x, mods, g1, *f1, 0, seq)
    au, av, q, k, v, gta, gtb = _inproj_fwd(h1, mods, g2, wts["w_in"], seq)
    q_hm, k_hm, v_hm = _to_heads(q, k, v)
    ob = _q_from_heads(_attn_fwd(q_hm, k_hm, v_hm, gq, gk, sink_b, nbs))
    h2 = _mix_fwd(au, av, gta, gtb, ob, h1, mods, g_ln, b_ln, ws, bsb, *mixw, seq)
    dh3, y3, loss_parts = _ffn_fwd(h2, mods, g3, *f2, 6, seq, tgt=tgt)

    grads = {}
    dh2, xb, dyb, a, dg, du, mg3 = _ffn_bwd(h2, dh3, y3, mods, g3, *f2, 6, seq)
    grads["ffn2_w_gate"] = _wgrad(xb, dg, "wgrad_ffn2_gate")
    grads["ffn2_w_up"] = _wgrad(xb, du, "wgrad_ffn2_up")
    grads["ffn2_w_down"] = _wgrad(a, dyb, "wgrad_ffn2_down")

    (dau, dav, dgta, dgtb, dob, grads["w_out"], grads["w_branch_a"], grads["w_branch_b"], dws, dbs, dln,
     mg2g) = _mix_bwd(au, av, gta, gtb, ob, dh2, mods, g_ln, b_ln, ws, bsb, *mixw, seq)
    do_hm, _, _ = _to_heads(dob, k, v)
    dq_hm, dkp, dkc, dvp, dvc, dgq, dgk, dsk = _attn_bwd(q_hm, k_hm, v_hm, gq, gk, sink_b, do_hm, nbs)

    def shift(prev_part):
        return jnp.concatenate([prev_part[:, 1:], jnp.zeros_like(prev_part[:, :1])], axis=1)

    dq = _q_from_heads(dq_hm)
    dk = _kv_from_heads(dkc + shift(dkp))
    dv = _kv_from_heads(dvc + shift(dvp))
    dh1, xb2, dpb, mg2 = _inproj_bwd(h1, dh2, (dau, dav, dq, dk, dv, dgta, dgtb), mods, g2, wts["w_in"], seq)
    grads["w_in"] = _wgrad(xb2, dpb, "wgrad_w_in")

    dx, xb, dyb, a, dg, du, mg1 = _ffn_bwd(x, dh1, y1, mods, g1, *f1, 0, seq)
    grads["ffn1_w_gate"] = _wgrad(xb, dg, "wgrad_ffn1_gate")
    grads["ffn1_w_up"] = _wgrad(xb, du, "wgrad_ffn1_up")
    grads["ffn1_w_down"] = _wgrad(a, dyb, "wgrad_ffn1_down")

    dmods, dgn = _mod_finish(mg1, mg2, mg2g, mg3, mods, g1, g2, g3)
    return loss_parts, dx, grads, dmods, (dgn, dln, dws, dbs, dgq, dgk, dsk)


def kernel(x, c, w_ada, b_ada, g_norm1, ffn1_w_gate, ffn1_w_up, ffn1_w_down, g_norm2, w_in, g_sgu_ln, b_sgu_ln, w_spatial, b_spatial, g_q, g_k, attn_sinks, w_branch_a, w_branch_b, w_out, g_norm3, ffn2_w_gate, ffn2_w_up, ffn2_w_down, loss_target, m_w_ada, m_b_ada, m_g_norm1, m_ffn1_w_gate, m_ffn1_w_up, m_ffn1_w_down, m_g_norm2, m_w_in, m_g_sgu_ln, m_b_sgu_ln, m_w_spatial, m_b_spatial, m_g_q, m_g_k, m_attn_sinks, m_w_branch_a, m_w_branch_b, m_w_out, m_g_norm3, m_ffn2_w_gate, m_ffn2_w_up, m_ffn2_w_down, v_w_ada, v_b_ada, v_g_norm1, v_ffn1_w_gate, v_ffn1_w_up, v_ffn1_w_down, v_g_norm2, v_w_in, v_g_sgu_ln, v_b_sgu_ln, v_w_spatial, v_b_spatial, v_g_q, v_g_k, v_attn_sinks, v_w_branch_a, v_w_branch_b, v_w_out, v_g_norm3, v_ffn2_w_gate, v_ffn2_w_up, v_ffn2_w_down):
    names = ("w_ada", "b_ada", "g_norm1", "ffn1_w_gate", "ffn1_w_up", "ffn1_w_down", "g_norm2", "w_in", "g_sgu_ln",
             "b_sgu_ln", "w_spatial", "b_spatial", "g_q", "g_k", "attn_sinks", "w_branch_a", "w_branch_b", "w_out",
             "g_norm3", "ffn2_w_gate", "ffn2_w_up", "ffn2_w_down")
    w = dict(zip(names, (w_ada, b_ada, g_norm1, ffn1_w_gate, ffn1_w_up, ffn1_w_down, g_norm2, w_in, g_sgu_ln,
                         b_sgu_ln, w_spatial, b_spatial, g_q, g_k, attn_sinks, w_branch_a, w_branch_b, w_out, g_norm3,
                         ffn2_w_gate, ffn2_w_up, ffn2_w_down)))
    m = dict(zip(names, (m_w_ada, m_b_ada, m_g_norm1, m_ffn1_w_gate, m_ffn1_w_up, m_ffn1_w_down, m_g_norm2, m_w_in,
                         m_g_sgu_ln, m_b_sgu_ln, m_w_spatial, m_b_spatial, m_g_q, m_g_k, m_attn_sinks, m_w_branch_a,
                         m_w_branch_b, m_w_out, m_g_norm3, m_ffn2_w_gate, m_ffn2_w_up, m_ffn2_w_down)))
    v = dict(zip(names, (v_w_ada, v_b_ada, v_g_norm1, v_ffn1_w_gate, v_ffn1_w_up, v_ffn1_w_down, v_g_norm2, v_w_in,
                         v_g_sgu_ln, v_b_sgu_ln, v_w_spatial, v_b_spatial, v_g_q, v_g_k, v_attn_sinks, v_w_branch_a,
                         v_w_branch_b, v_w_out, v_g_norm3, v_ffn2_w_gate, v_ffn2_w_up, v_ffn2_w_down)))
    B, seq, _ = x.shape
    T = B * seq
    xi, yi, ci = _place()
    me = 4 * xi + 2 * yi + ci
    big_names = [name for name, _, _, _ in BIG]

    (gathered_w,) = _all_gather([_pack_shards({k: w[k][0].astype(BF16) for k in big_names})], "gather_weights")
    wts = _unpack_full(gathered_w)

    (c_all,) = _all_gather([c], "gather_c")
    c_all = c_all.reshape(N_DEV * B, D)
    (mods_cols,) = _all_gather([_ada_fwd(c_all, w_ada[0])], "gather_mods")
    mine = lax.dynamic_slice_in_dim(mods_cols, B * me, B, axis=1)
    mods = (mine.transpose(1, 0, 2).reshape(B, 9 * D) + b_ada).reshape(B, 9, D)

    small = {k: w[k] for k in ("g_q", "g_k", "g_norm1", "g_norm2", "g_norm3", "g_sgu_ln", "b_sgu_ln")}
    small.update({k: w[k][0] for k in ("w_spatial", "b_spatial", "attn_sinks")})
    loss_parts, dx, grads, dmods, small_parts = _local_step(
        x.reshape(T, D), loss_target.reshape(T, D), mods, wts, small, seq)
    loss = lax.psum(jnp.sum(loss_parts[:, 0, 0]), AXES)

    gathered = _all_gather([dmods.reshape(B, 9 * D)] + list(small_parts), "gather_small")
    small_out = _small_update(gathered, w, m, v)
    dm_cols = lax.dynamic_slice_in_dim(gathered[0].reshape(N_DEV * B, 9 * D), (9 * D // N_DEV) * me,
                                       9 * D // N_DEV, axis=1)
    ada_out = _ada_update(c_all, dm_cols, w_ada[0], m_w_ada[0], v_w_ada[0])

    p = _pack_full(grads)
    q = _rs_add(p, _rs_sibling(p), jnp.reshape(ci, (1,)).astype(jnp.int32))
    big_out = _rs_finish(_rs_chips(q), _pack_shards({k: w[k] for k in big_names}),
                         _pack_shards({k: m[k] for k in big_names}), _pack_shards({k: v[k] for k in big_names}))
    big_out = [_unpack_shards(o) for o in big_out]

    def leaf(kind, name):
        if name == "w_ada":
            return ada_out[kind][None]
        if name in SMALL:
            return small_out[name][kind]
        return big_out[kind][name]

    return (loss, dx.reshape(B, seq, D), *[leaf(kind, name) for kind in range(4) for name in names])
```

```python
import functools
import math

import jax
import jax.numpy as jnp
from jax import lax
from jax.experimental import pallas as pl
from jax.experimental.pallas import tpu as pltpu

F32 = jnp.float32
BF16 = jnp.bfloat16
MESH = pl.DeviceIdType.MESH
AXES = ("x", "y", "c")
N_DEV = 8

VMEM_LIMIT = 56 * 1024 * 1024

D = 1024
FF = 2816
FC = 1408
D_A = 512
D_B = 512
HD = 64
N_KV = 2
Q_PER_KV = 4
BLK = 128
N_GRP = 4
IN_COLS = 3840
PIECES = (("au", 0, 512), ("av", 512, 512), ("q", 1024, 512), ("k", 1536, 128), ("v", 1664, 128),
          ("ga", 1792, 1024), ("gb", 2816, 1024))
EPS = 1e-6
NEG = -1e30
GELU_C = math.sqrt(2.0 / math.pi)

ADAM_LR = 0.001
ADAM_B1 = 0.9
ADAM_B2 = 0.999
ADAM_EPS = 1e-08
ADAM_WD = 0.01
ADAM_STEP = 10

NT = (((1,), (1,)), ((), ()))
TN = (((0,), (0,)), ((), ()))

BIG = (("ffn1_w_gate", True, FF // N_DEV, D), ("ffn1_w_up", True, FF // N_DEV, D),
       ("ffn1_w_down", False, FF // N_DEV, D), ("w_in", True, IN_COLS // N_DEV, D),
       ("w_branch_a", True, D // N_DEV, D_A), ("w_branch_b", True, D // N_DEV, D_B), ("w_out", False, D // N_DEV, D),
       ("ffn2_w_gate", True, FF // N_DEV, D), ("ffn2_w_up", True, FF // N_DEV, D),
       ("ffn2_w_down", False, FF // N_DEV, D))
SMALL = ("b_ada", "g_norm1", "g_norm2", "g_sgu_ln", "b_sgu_ln", "w_spatial", "b_spatial", "g_q", "g_k",
         "attn_sinks", "g_norm3")


def _dot(a, b):
    return jnp.dot(a, b, preferred_element_type=F32)


def _dot_nt(a, b):
    return lax.dot_general(a, b, NT, preferred_element_type=F32)


def _dot_tn(a, b):
    return lax.dot_general(a, b, TN, preferred_element_type=F32)


def _vmem():
    return pl.BlockSpec(memory_space=pltpu.VMEM)


def _any():
    return pl.BlockSpec(memory_space=pl.ANY)


def _rms_mod(h, g, sh, sc):
    inv = lax.rsqrt(jnp.mean(h * h, axis=-1, keepdims=True) + EPS)
    r = h * inv
    return (r * g) * (1.0 + sc) + sh, r, inv


def _rms_mod_bwd(dxn, r, inv, g, sc):
    dr = dxn * (g * (1.0 + sc))
    dh = inv * (dr - r * jnp.mean(dr * r, axis=-1, keepdims=True))
    return dh, jnp.sum(dxn, axis=0, keepdims=True), jnp.sum(dxn * r, axis=0, keepdims=True)


def _gelu(x):
    t = jnp.tanh(GELU_C * (x + 0.044715 * (x * x * x)))
    return 0.5 * x * (1.0 + t), t


def _gelu_grad(x, t):
    return 0.5 * (1.0 + t) + 0.5 * x * (1.0 - t * t) * (GELU_C * (1.0 + 3.0 * 0.044715 * x * x))


def _adamw(w, g, m, v):
    m = ADAM_B1 * m + (1.0 - ADAM_B1) * g
    v = ADAM_B2 * v + (1.0 - ADAM_B2) * (g * g)
    m_hat = m / (1.0 - ADAM_B1 ** ADAM_STEP)
    v_hat = v / (1.0 - ADAM_B2 ** ADAM_STEP)
    delta = -ADAM_LR * (m_hat / (jnp.sqrt(v_hat) + ADAM_EPS) + ADAM_WD * w)
    return delta, m, v


def _token_tile(seq, cap=512):
    return min(cap, seq)


def _params(*semantics):
    return pltpu.CompilerParams(dimension_semantics=semantics, vmem_limit_bytes=VMEM_LIMIT)


def _ffn_fwd(h, mods, gn, wg, wu, wd, row0, seq, tgt=None):
    T = h.shape[0]
    tm = _token_tile(seq)
    tps = seq // tm
    n_t = T // tm
    with_loss = tgt is not None

    def body(h_ref, m_ref, g_ref, wg_ref, wu_ref, wd_ref, *rest):
        if with_loss:
            tgt_ref, out_ref, y_ref, loss_ref = rest
        else:
            out_ref, y_ref = rest
        hv = h_ref[...]
        sh = m_ref[0, row0:row0 + 1, :]
        sc = m_ref[0, row0 + 1:row0 + 2, :]
        ga = m_ref[0, row0 + 2:row0 + 3, :]
        xn, _, _ = _rms_mod(hv, g_ref[...], sh, sc)
        xb = xn.astype(BF16)
        acc = jnp.zeros((tm, D), F32)
        for c0 in range(0, FF, FC):
            gg = _dot_nt(xb, wg_ref[c0:c0 + FC, :])
            uu = _dot_nt(xb, wu_ref[c0:c0 + FC, :])
            a = (gg * jax.nn.sigmoid(gg)) * uu
            acc = acc + _dot(a.astype(BF16), wd_ref[c0:c0 + FC, :])
        y_ref[...] = acc
        hout = hv + (0.5 * ga) * acc
        if with_loss:
            d = hout - tgt_ref[...]
            out_ref[...] = d * (1.0 / D)
            loss_ref[...] = jnp.full((1, 8, 128), 0.5 / D, F32) * jnp.sum(d * d)
        else:
            out_ref[...] = hout

    tok = pl.BlockSpec((tm, D), lambda i: (i, 0))
    in_specs = [tok, pl.BlockSpec((1, 9, D), lambda i: (i // tps, 0, 0)), pl.BlockSpec((1, D), lambda i: (0, 0)),
                _vmem(), _vmem(), _vmem()]
    out_shape = [jax.ShapeDtypeStruct((T, D), F32), jax.ShapeDtypeStruct((T, D), F32)]
    out_specs = [tok, tok]
    args = [h, mods, gn, wg, wu, wd]
    if with_loss:
        in_specs.append(tok)
        args.append(tgt)
        out_shape.append(jax.ShapeDtypeStruct((n_t, 8, 128), F32))
        out_specs.append(pl.BlockSpec((1, 8, 128), lambda i: (i, 0, 0)))
    return pl.pallas_call(
        body, name="ffn_fwd_loss" if with_loss else "ffn_fwd", grid=(n_t,), in_specs=in_specs, out_specs=out_specs,
        out_shape=out_shape, compiler_params=_params("parallel"),
    )(*args)


def _ffn_bwd(h, dhn, y, mods, gn, wg, wu, wd, row0, seq):
    T = h.shape[0]
    B = T // seq
    tm = _token_tile(seq, 256)
    tps = seq // tm
    n_t = T // tm

    def body(h_ref, dhn_ref, y_ref, m_ref, g_ref, wg_ref, wu_ref, wd_ref,
             dh_ref, xb_ref, dyb_ref, a_ref, dg_ref, du_ref, mg_ref):
        i = pl.program_id(0)
        hv = h_ref[...]
        dhn = dhn_ref[...]
        sh = m_ref[0, row0:row0 + 1, :]
        sc = m_ref[0, row0 + 1:row0 + 2, :]
        ga = m_ref[0, row0 + 2:row0 + 3, :]
        g = g_ref[...]
        xn, r, inv = _rms_mod(hv, g, sh, sc)
        xb = xn.astype(BF16)
        xb_ref[...] = xb
        dyb = ((0.5 * ga) * dhn).astype(BF16)
        dyb_ref[...] = dyb
        dga = 0.5 * jnp.sum(dhn * y_ref[...], axis=0, keepdims=True)
        dxn = jnp.zeros((tm, D), F32)
        for c0 in range(0, FF, FC):
            wgc = wg_ref[c0:c0 + FC, :]
            wuc = wu_ref[c0:c0 + FC, :]
            gg = _dot_nt(xb, wgc)
            uu = _dot_nt(xb, wuc)
            sig = jax.nn.sigmoid(gg)
            s = gg * sig
            a_ref[:, c0:c0 + FC] = (s * uu).astype(BF16)
            da = _dot_nt(dyb, wd_ref[c0:c0 + FC, :])
            dub = (da * s).astype(BF16)
            dgb = (da * uu * (sig * (1.0 + gg * (1.0 - sig)))).astype(BF16)
            dg_ref[:, c0:c0 + FC] = dgb
            du_ref[:, c0:c0 + FC] = dub
            dxn = dxn + _dot(dgb, wgc) + _dot(dub, wuc)
        dh, s_dxn, s_dxr = _rms_mod_bwd(dxn, r, inv, g, sc)
        dh_ref[...] = dhn + dh

        @pl.when(i % tps == 0)
        def _():
            mg_ref[...] = jnp.zeros(mg_ref.shape, F32)

        mg_ref[0, 0:1, :] += s_dxn
        mg_ref[0, 1:2, :] += s_dxr
        mg_ref[0, 2:3, :] += dga

    tok = pl.BlockSpec((tm, D), lambda i: (i, 0))
    tokf = pl.BlockSpec((tm, FF), lambda i: (i, 0))
    return pl.pallas_call(
        body, name="ffn_bwd", grid=(n_t,),
        in_specs=[tok, tok, tok, pl.BlockSpec((1, 9, D), lambda i: (i // tps, 0, 0)),
                  pl.BlockSpec((1, D), lambda i: (0, 0)), _vmem(), _vmem(), _vmem()],
        out_specs=[tok, tok, tok, tokf, tokf, tokf, pl.BlockSpec((1, 8, D), lambda i: (i // tps, 0, 0))],
        out_shape=[jax.ShapeDtypeStruct((T, D), F32), jax.ShapeDtypeStruct((T, D), BF16),
                   jax.ShapeDtypeStruct((T, D), BF16), jax.ShapeDtypeStruct((T, FF), BF16),
                   jax.ShapeDtypeStruct((T, FF), BF16), jax.ShapeDtypeStruct((T, FF), BF16),
                   jax.ShapeDtypeStruct((B, 8, D), F32)],
        compiler_params=_params("arbitrary"),
    )(h, dhn, y, mods, gn, wg, wu, wd)


def _wgrad(a, b, name):
    T, da = a.shape
    db = b.shape[1]
    bm = {2816: 1408, 3840: 1280}[da]
    bn = db
    tk = min(1024, T)
    nk = T // tk

    def body(a_ref, b_ref, o_ref):
        @pl.when(pl.program_id(2) == 0)
        def _():
            o_ref[...] = jnp.zeros(o_ref.shape, F32)

        o_ref[...] += _dot_tn(a_ref[...], b_ref[...])

    return pl.pallas_call(
        body, name=name, grid=(da // bm, db // bn, nk),
        in_specs=[pl.BlockSpec((tk, bm), lambda i, j, k: (k, i)), pl.BlockSpec((tk, bn), lambda i, j, k: (k, j))],
        out_specs=pl.BlockSpec((bm, bn), lambda i, j, k: (i, j)),
        out_shape=jax.ShapeDtypeStruct((da, db), F32),
        compiler_params=_params("parallel", "parallel", "arbitrary"),
    )(a, b)


def _inproj_fwd(h, mods, gn, w_in, seq):
    T = h.shape[0]
    tm = _token_tile(seq)
    tps = seq // tm

    def body(h_ref, m_ref, g_ref, w_ref, *outs):
        xn, _, _ = _rms_mod(h_ref[...], g_ref[...], m_ref[0, 3:4, :], m_ref[0, 4:5, :])
        xb = xn.astype(BF16)
        for (_, c0, w), o_ref in zip(PIECES, outs):
            o_ref[...] = _dot_nt(xb, w_ref[c0:c0 + w, :])

    return pl.pallas_call(
        body, name="inproj_fwd", grid=(T // tm,),
        in_specs=[pl.BlockSpec((tm, D), lambda i: (i, 0)), pl.BlockSpec((1, 9, D), lambda i: (i // tps, 0, 0)),
                  pl.BlockSpec((1, D), lambda i: (0, 0)), _vmem()],
        out_specs=[pl.BlockSpec((tm, w), lambda i: (i, 0)) for _, _, w in PIECES],
        out_shape=[jax.ShapeDtypeStruct((T, w), F32) for _, _, w in PIECES],
        compiler_params=_params("parallel"),
    )(h, mods, gn, w_in)


def _inproj_bwd(h, dh_res, dpieces, mods, gn, w_in, seq):
    T = h.shape[0]
    B = T // seq
    tm = _token_tile(seq, 256)
    tps = seq // tm

    def body(h_ref, dres_ref, *rest):
        dp_refs = rest[:len(PIECES)]
        m_ref, g_ref, w_ref, dh_ref, xb_ref, dpb_ref, mg_ref = rest[len(PIECES):]
        i = pl.program_id(0)
        g = g_ref[...]
        sc = m_ref[0, 4:5, :]
        xn, r, inv = _rms_mod(h_ref[...], g, m_ref[0, 3:4, :], sc)
        xb_ref[...] = xn.astype(BF16)
        dxn = jnp.zeros((tm, D), F32)
        for (_, c0, w), dp_ref in zip(PIECES, dp_refs):
            dpb = dp_ref[...].astype(BF16)
            dpb_ref[:, c0:c0 + w] = dpb
            dxn = dxn + _dot(dpb, w_ref[c0:c0 + w, :])
        dh, s_dxn, s_dxr = _rms_mod_bwd(dxn, r, inv, g, sc)
        dh_ref[...] = dres_ref[...] + dh

        @pl.when(i % tps == 0)
        def _():
            mg_ref[...] = jnp.zeros(mg_ref.shape, F32)

        mg_ref[0, 0:1, :] += s_dxn
        mg_ref[0, 1:2, :] += s_dxr

    tok = pl.BlockSpec((tm, D), lambda i: (i, 0))
    return pl.pallas_call(
        body, name="inproj_bwd", grid=(T // tm,),
        in_specs=[tok, tok] + [pl.BlockSpec((tm, w), lambda i: (i, 0)) for _, _, w in PIECES]
        + [pl.BlockSpec((1, 9, D), lambda i: (i // tps, 0, 0)), pl.BlockSpec((1, D), lambda i: (0, 0)), _vmem()],
        out_specs=[tok, tok, pl.BlockSpec((tm, IN_COLS), lambda i: (i, 0)),
                   pl.BlockSpec((1, 8, D), lambda i: (i // tps, 0, 0))],
        out_shape=[jax.ShapeDtypeStruct((T, D), F32), jax.ShapeDtypeStruct((T, D), BF16),
                   jax.ShapeDtypeStruct((T, IN_COLS), BF16), jax.ShapeDtypeStruct((B, 8, D), F32)],
        compiler_params=_params("arbitrary"),
    )(h, dh_res, *dpieces, mods, gn, w_in)


def _attn_math(q, kp, kc, vp, vc, gq, gk, sink, first):
    def rms(x, g):
        return (x * lax.rsqrt(jnp.mean(x * x, axis=-1, keepdims=True) + EPS)) * g

    qn = rms(q, gq)
    kn = rms(jnp.concatenate([kp, kc], axis=0), gk)
    v = jnp.concatenate([vp, vc], axis=0)
    s = _dot_nt(qn.astype(BF16), kn.astype(BF16)) * (HD ** -0.5)
    nq = Q_PER_KV * BLK
    row = lax.broadcasted_iota(jnp.int32, (nq, 2 * BLK), 0) & (BLK - 1)
    col = lax.broadcasted_iota(jnp.int32, (nq, 2 * BLK), 1)
    lo = jnp.where(first, BLK, 0)
    valid = (col <= row + BLK) & (col > row) & (col >= lo)
    s = jnp.where(valid, s, NEG)
    m = lax.stop_gradient(jnp.maximum(jnp.max(s, axis=-1, keepdims=True), sink))
    p = jnp.exp(s - m)
    den = jnp.sum(p, axis=-1, keepdims=True) + jnp.exp(sink - m)
    return _dot((p / den).astype(BF16), v.astype(BF16))


def _attn_specs(nbs):
    nq = Q_PER_KV * BLK
    qs = pl.BlockSpec((1, 1, nq, HD), lambda h, b: (h, b, 0, 0))
    prev = pl.BlockSpec((1, 1, BLK, HD), lambda h, b: (h, jnp.maximum(b - 1, 0), 0, 0))
    cur = pl.BlockSpec((1, 1, BLK, HD), lambda h, b: (h, b, 0, 0))
    gs = pl.BlockSpec((1, HD), lambda h, b: (0, 0))
    sk = pl.BlockSpec((1, nq, 128), lambda h, b: (h, 0, 0))
    return qs, prev, cur, gs, sk


def _attn_fwd(q_hm, k_hm, v_hm, gq, gk, sink_b, nbs):
    nblk = q_hm.shape[1]
    qs, prev, cur, gs, sk = _attn_specs(nbs)

    def body(q_ref, kp_ref, kc_ref, vp_ref, vc_ref, gq_ref, gk_ref, sk_ref, o_ref):
        first = (pl.program_id(1) % nbs) == 0
        o_ref[0, 0] = _attn_math(q_ref[0, 0], kp_ref[0, 0], kc_ref[0, 0], vp_ref[0, 0], vc_ref[0, 0],
                                 gq_ref[...], gk_ref[...], sk_ref[0, :, 0:1], first)

    return pl.pallas_call(
        body, name="attn_fwd", grid=(N_KV, nblk), in_specs=[qs, prev, cur, prev, cur, gs, gs, sk], out_specs=qs,
        out_shape=jax.ShapeDtypeStruct(q_hm.shape, F32),
        compiler_params=_params("parallel", "parallel"),
    )(q_hm, k_hm, k_hm, v_hm, v_hm, gq, gk, sink_b)


def _attn_bwd(q_hm, k_hm, v_hm, gq, gk, sink_b, do_hm, nbs):
    nblk = q_hm.shape[1]
    qs, prev, cur, gs, sk = _attn_specs(nbs)
    nq = Q_PER_KV * BLK

    def body(q_ref, kp_ref, kc_ref, vp_ref, vc_ref, gq_ref, gk_ref, sk_ref, do_ref,
             dq_ref, dkp_ref, dkc_ref, dvp_ref, dvc_ref, dgq_ref, dgk_ref, dsk_ref):
        h = pl.program_id(0)
        b = pl.program_id(1)
        first = (b % nbs) == 0
        f = functools.partial(_attn_math, first=first)
        _, vjp = jax.vjp(f, q_ref[0, 0], kp_ref[0, 0], kc_ref[0, 0], vp_ref[0, 0], vc_ref[0, 0],
                         gq_ref[...], gk_ref[...], sk_ref[0, :, 0:1])
        dq, dkp, dkc, dvp, dvc, dgq, dgk, dsk = vjp(do_ref[0, 0])
        dq_ref[0, 0] = dq
        dkp_ref[0, 0] = dkp
        dkc_ref[0, 0] = dkc
        dvp_ref[0, 0] = dvp
        dvc_ref[0, 0] = dvc

        @pl.when((h == 0) & (b == 0))
        def _():
            for r in (dgq_ref, dgk_ref, dsk_ref):
                r[...] = jnp.zeros(r.shape, F32)

        dgq_ref[...] += dgq
        dgk_ref[...] += dgk
        lane = lax.broadcasted_iota(jnp.int32, (8, 128), 1)
        upd = jnp.zeros((8, 128), F32)
        for g in range(Q_PER_KV):
            tot = jnp.sum(dsk[g * BLK:(g + 1) * BLK, :])
            upd = upd + jnp.where(lane == h * Q_PER_KV + g, tot, 0.0)
        dsk_ref[...] += upd

    acc = pl.BlockSpec((1, HD), lambda h, b: (0, 0))
    return pl.pallas_call(
        body, name="attn_bwd", grid=(N_KV, nblk), in_specs=[qs, prev, cur, prev, cur, gs, gs, sk, qs],
        out_specs=[qs, cur, cur, cur, cur, acc, acc, pl.BlockSpec((8, 128), lambda h, b: (0, 0))],
        out_shape=[jax.ShapeDtypeStruct(q_hm.shape, F32)] + [jax.ShapeDtypeStruct(k_hm.shape, F32)] * 4
        + [jax.ShapeDtypeStruct((1, HD), F32)] * 2 + [jax.ShapeDtypeStruct((8, 128), F32)],
        compiler_params=_params("arbitrary", "arbitrary"),
    )(q_hm, k_hm, k_hm, v_hm, v_hm, gq, gk, sink_b, do_hm)


def _to_heads(q, k, v):
    nblk = q.shape[0] // BLK
    q_hm = q.reshape(nblk, BLK, N_KV, Q_PER_KV, HD).transpose(2, 0, 3, 1, 4).reshape(N_KV, nblk, Q_PER_KV * BLK, HD)
    k_hm = k.reshape(nblk, BLK, N_KV, HD).transpose(2, 0, 1, 3)
    v_hm = v.reshape(nblk, BLK, N_KV, HD).transpose(2, 0, 1, 3)
    return q_hm, k_hm, v_hm


def _q_from_heads(q_hm):
    nblk = q_hm.shape[1]
    return q_hm.reshape(N_KV, nblk, Q_PER_KV, BLK, HD).transpose(1, 3, 0, 2, 4).reshape(nblk * BLK, N_KV * Q_PER_KV * HD)


def _kv_from_heads(k_hm):
    nblk = k_hm.shape[1]
    return k_hm.transpose(1, 2, 0, 3).reshape(nblk * BLK, N_KV * HD)


def _sgu_norm(av, g_ln, b_ln):
    t, th = _gelu(av)
    mu = jnp.mean(t, axis=-1, keepdims=True)
    tc = t - mu
    rstd = lax.rsqrt(jnp.mean(tc * tc, axis=-1, keepdims=True) + EPS)
    vhat = tc * rstd
    return vhat * g_ln + b_ln, vhat, rstd, th


def _masked_ws(ws_ref):
    tril = lax.broadcasted_iota(jnp.int32, (BLK, BLK), 0) >= lax.broadcasted_iota(jnp.int32, (BLK, BLK), 1)
    return [jnp.where(tril, ws_ref[g], 0.0).astype(BF16) for g in range(N_GRP)]


def _mix_fwd(au, av, gta, gtb, ob, h, mods, g_ln, b_ln, ws, bsb, wa, wb, wout, seq):
    T = h.shape[0]
    tm = _token_tile(seq)
    tps = seq // tm

    def body(au_ref, av_ref, gta_ref, gtb_ref, ob_ref, h_ref, m_ref, gl_ref, bl_ref, ws_ref, bs_ref,
             wa_ref, wb_ref, wo_ref, out_ref, vvb_s, z_s):
        u, _ = _gelu(au_ref[...])
        vv, _, _, _ = _sgu_norm(av_ref[...], gl_ref[...], bl_ref[...])
        vvb_s[...] = vv.astype(BF16)
        wsm = _masked_ws(ws_ref)
        for c in range(tm // BLK):
            rows = slice(c * BLK, (c + 1) * BLK)
            for g in range(N_GRP):
                cols = slice(g * BLK, (g + 1) * BLK)
                z_s[rows, cols] = _dot(wsm[g], vvb_s[rows, cols]) + bs_ref[g]
        ya = _dot_nt((u * z_s[...]).astype(BF16), wa_ref[...])
        yb = _dot_nt(ob_ref[...].astype(BF16), wb_ref[...])
        merged = jax.nn.sigmoid(gta_ref[...]) * ya + jax.nn.sigmoid(gtb_ref[...]) * yb
        out_ref[...] = h_ref[...] + m_ref[0, 5:6, :] * _dot(merged.astype(BF16), wo_ref[...])

    def tok(w):
        return pl.BlockSpec((tm, w), lambda i: (i, 0))

    def full(shape):
        return pl.BlockSpec(shape, lambda i: (0,) * len(shape))

    return pl.pallas_call(
        body, name="mix_fwd", grid=(T // tm,),
        in_specs=[tok(D_A), tok(D_A), tok(D), tok(D), tok(D_B), tok(D),
                  pl.BlockSpec((1, 9, D), lambda i: (i // tps, 0, 0)), full((1, D_A)), full((1, D_A)),
                  full((N_GRP, BLK, BLK)), full((N_GRP, BLK, BLK)), _vmem(), _vmem(), _vmem()],
        out_specs=tok(D), out_shape=jax.ShapeDtypeStruct((T, D), F32),
        scratch_shapes=[pltpu.VMEM((tm, D_A), BF16), pltpu.VMEM((tm, D_A), F32)],
        compiler_params=_params("parallel"),
    )(au, av, gta, gtb, ob, h, mods, g_ln, b_ln, ws, bsb, wa, wb, wout)


def _mix_bwd(au, av, gta, gtb, ob, dh, mods, g_ln, b_ln, ws, bsb, wa, wb, wout, seq):
    T = dh.shape[0]
    B = T // seq
    tm = _token_tile(seq, 256)
    tps = seq // tm

    def body(au_ref, av_ref, gta_ref, gtb_ref, ob_ref, dh_ref, m_ref, gl_ref, bl_ref, ws_ref, bs_ref,
             wa_ref, wb_ref, wo_ref,
             dau_ref, dav_ref, dgta_ref, dgtb_ref, dob_ref, dwo_ref, dwa_ref, dwb_ref, dws_ref, dbs_ref, dln_ref,
             mg_ref, vvb_s, z_s, dz_s, dzb_s, dvv_s):
        i = pl.program_id(0)

        @pl.when(i == 0)
        def _():
            for r in (dwo_ref, dwa_ref, dwb_ref, dws_ref, dbs_ref, dln_ref):
                r[...] = jnp.zeros(r.shape, F32)

        @pl.when(i % tps == 0)
        def _():
            mg_ref[...] = jnp.zeros(mg_ref.shape, F32)

        auv = au_ref[...]
        avv = av_ref[...]
        u, thu = _gelu(auv)
        g_ln = gl_ref[...]
        vv, vhat, rstd, thv = _sgu_norm(avv, g_ln, bl_ref[...])
        vvb_s[...] = vv.astype(BF16)
        wsm = _masked_ws(ws_ref)
        for c in range(tm // BLK):
            rows = slice(c * BLK, (c + 1) * BLK)
            for g in range(N_GRP):
                cols = slice(g * BLK, (g + 1) * BLK)
                z_s[rows, cols] = _dot(wsm[g], vvb_s[rows, cols]) + bs_ref[g]
        z = z_s[...]
        yab = (u * z).astype(BF16)
        obb = ob_ref[...].astype(BF16)
        ya = _dot_nt(yab, wa_ref[...])
        yb = _dot_nt(obb, wb_ref[...])
        sa = jax.nn.sigmoid(gta_ref[...])
        sb = jax.nn.sigmoid(gtb_ref[...])
        mb = (sa * ya + sb * yb).astype(BF16)
        dhv = dh_ref[...]
        mg_ref[0, 0:1, :] += jnp.sum(dhv * _dot(mb, wo_ref[...]), axis=0, keepdims=True)
        dmob = (m_ref[0, 5:6, :] * dhv).astype(BF16)
        dwo_ref[...] += _dot_tn(mb, dmob)
        dmerged = _dot_nt(dmob, wo_ref[...])
        dya = dmerged * sa
        dyb = dmerged * sb
        dgta_ref[...] = dya * ya * (1.0 - sa)
        dgtb_ref[...] = dyb * yb * (1.0 - sb)
        dyab = dya.astype(BF16)
        dybb = dyb.astype(BF16)
        dwa_ref[...] += _dot_tn(dyab, yab)
        dwb_ref[...] += _dot_tn(dybb, obb)
        dob_ref[...] = _dot(dybb, wb_ref[...])
        dyap = _dot(dyab, wa_ref[...])
        dau_ref[...] = (dyap * z) * _gelu_grad(auv, thu)
        dz = dyap * u
        dz_s[...] = dz
        dzb_s[...] = dz.astype(BF16)
        for c in range(tm // BLK):
            rows = slice(c * BLK, (c + 1) * BLK)
            for g in range(N_GRP):
                cols = slice(g * BLK, (g + 1) * BLK)
                dzb = dzb_s[rows, cols]
                dvv_s[rows, cols] = _dot_tn(wsm[g], dzb)
                dws_ref[g] += _dot_nt(dzb, vvb_s[rows, cols])
                dbs_ref[g] += dz_s[rows, cols]
        dvv = dvv_s[...]
        dln_ref[0:1, :] += jnp.sum(dvv * vhat, axis=0, keepdims=True)
        dln_ref[1:2, :] += jnp.sum(dvv, axis=0, keepdims=True)
        dvh = dvv * g_ln
        dt = rstd * (dvh - jnp.mean(dvh, axis=-1, keepdims=True)
                     - vhat * jnp.mean(dvh * vhat, axis=-1, keepdims=True))
        dav_ref[...] = dt * _gelu_grad(avv, thv)

    def tok(w):
        return pl.BlockSpec((tm, w), lambda i: (i, 0))

    def full(shape):
        return pl.BlockSpec(shape, lambda i: (0,) * len(shape))

    return pl.pallas_call(
        body, name="mix_bwd", grid=(T // tm,),
        in_specs=[tok(D_A), tok(D_A), tok(D), tok(D), tok(D_B), tok(D),
                  pl.BlockSpec((1, 9, D), lambda i: (i // tps, 0, 0)), full((1, D_A)), full((1, D_A)),
                  full((N_GRP, BLK, BLK)), full((N_GRP, BLK, BLK)), _vmem(), _vmem(), _vmem()],
        out_specs=[tok(D_A), tok(D_A), tok(D), tok(D), tok(D_B), full((D, D)), full((D, D_A)), full((D, D_B)),
                   full((N_GRP, BLK, BLK)), full((N_GRP, BLK, BLK)), full((8, D_A)),
                   pl.BlockSpec((1, 8, D), lambda i: (i // tps, 0, 0))],
        out_shape=[jax.ShapeDtypeStruct((T, D_A), F32), jax.ShapeDtypeStruct((T, D_A), F32),
                   jax.ShapeDtypeStruct((T, D), F32), jax.ShapeDtypeStruct((T, D), F32),
                   jax.ShapeDtypeStruct((T, D_B), F32), jax.ShapeDtypeStruct((D, D), F32),
                   jax.ShapeDtypeStruct((D, D_A), F32), jax.ShapeDtypeStruct((D, D_B), F32),
                   jax.ShapeDtypeStruct((N_GRP, BLK, BLK), F32), jax.ShapeDtypeStruct((N_GRP, BLK, BLK), F32),
                   jax.ShapeDtypeStruct((8, D_A), F32), jax.ShapeDtypeStruct((B, 8, D), F32)],
        scratch_shapes=[pltpu.VMEM((tm, D_A), BF16), pltpu.VMEM((tm, D_A), F32), pltpu.VMEM((tm, D_A), F32),
                        pltpu.VMEM((tm, D_A), BF16), pltpu.VMEM((tm, D_A), F32)],
        compiler_params=_params("arbitrary"),
    )(au, av, gta, gtb, ob, dh, mods, g_ln, b_ln, ws, bsb, wa, wb, wout)


def _ada_fwd(c_all, w_ada):
    def body(c_ref, w_ref, o_ref):
        cv = c_ref[...]
        o_ref[...] = _dot((cv * jax.nn.sigmoid(cv)).astype(BF16), w_ref[...].astype(BF16))

    return pl.pallas_call(body, name="ada_fwd", in_specs=[_vmem(), _vmem()], out_specs=_vmem(),
                          out_shape=jax.ShapeDtypeStruct((c_all.shape[0], w_ada.shape[1]), F32))(c_all, w_ada)


def _ada_update(c_all, dm_cols, w, m, v):
    n, cols = c_all.shape[0], w.shape[1]

    def body(c_ref, dm_ref, w_ref, m_ref, v_ref, g_ref, d_ref, nm_ref, nv_ref):
        cv = c_ref[...]
        g = _dot_tn((cv * jax.nn.sigmoid(cv)).astype(BF16), dm_ref[...].astype(BF16))
        g_ref[...] = g
        d_ref[...], nm_ref[...], nv_ref[...] = _adamw(w_ref[...], g, m_ref[...], v_ref[...])

    col = pl.BlockSpec((D, 128), lambda j: (0, j))
    return pl.pallas_call(
        body, name="ada_update", grid=(cols // 128,),
        in_specs=[pl.BlockSpec((n, D), lambda j: (0, 0)), pl.BlockSpec((n, 128), lambda j: (0, j)), col, col, col],
        out_specs=[col] * 4, out_shape=[jax.ShapeDtypeStruct(w.shape, F32)] * 4, compiler_params=_params("parallel"),
    )(c_all, dm_cols, w, m, v)


def _mod_finish(mg1, mg2, mg2g, mg3, mods, g1, g2, g3):
    B = mods.shape[0]

    def body(mg1_ref, mg2_ref, mg2g_ref, mg3_ref, m_ref, g1_ref, g2_ref, g3_ref, dm_ref, dgn_ref):
        dgn_ref[...] = jnp.zeros(dgn_ref.shape, F32)
        for k, (mg, g_ref) in enumerate(((mg1_ref, g1_ref), (mg2_ref, g2_ref), (mg3_ref, g3_ref))):
            for b in range(B):
                s_dxr = mg[b, 1:2, :]
                dm_ref[b, 3 * k:3 * k + 1, :] = mg[b, 0:1, :]
                dm_ref[b, 3 * k + 1:3 * k + 2, :] = g_ref[...] * s_dxr
                dm_ref[b, 3 * k + 2:3 * k + 3, :] = mg2g_ref[b, 0:1, :] if k == 1 else mg[b, 2:3, :]
                dgn_ref[k:k + 1, :] += (1.0 + m_ref[b, 3 * k + 1:3 * k + 2, :]) * s_dxr

    return pl.pallas_call(body, name="mod_finish", in_specs=[_vmem()] * 8, out_specs=[_vmem()] * 2,
                          out_shape=[jax.ShapeDtypeStruct((B, 9, D), F32), jax.ShapeDtypeStruct((8, D), F32)],
                          )(mg1, mg2, mg2g, mg3, mods, g1, g2, g3)


def _small_update(gathered, params, ms, vs):
    n = len(SMALL)
    B = gathered[0].shape[1]

    def body(*refs):
        gdm, ggn, gln, gws, gbs, ggq, ggk, gsk = refs[:8]
        w = dict(zip(SMALL, refs[8:8 + n]))
        m = dict(zip(SMALL, refs[8 + n:8 + 2 * n]))
        v = dict(zip(SMALL, refs[8 + 2 * n:8 + 3 * n]))
        outs = refs[8 + 3 * n:]
        out = {name: outs[4 * k:4 * k + 4] for k, name in enumerate(SMALL)}

        def total(ref, idx):
            acc = ref[(0,) + idx]
            for dev in range(1, N_DEV):
                acc = acc + ref[(dev,) + idx]
            return acc

        def finish(name, g, idx=(Ellipsis,)):
            d, nm, nv = _adamw(w[name][idx], g, m[name][idx], v[name][idx])
            for ref, val in zip(out[name], (g, d, nm, nv)):
                ref[idx] = val

        g_bada = total(gdm, (slice(0, 1),))
        for b in range(1, B):
            g_bada = g_bada + total(gdm, (slice(b, b + 1),))
        finish("b_ada", g_bada)
        finish("g_norm1", total(ggn, (slice(0, 1),)))
        finish("g_norm2", total(ggn, (slice(1, 2),)))
        finish("g_norm3", total(ggn, (slice(2, 3),)))
        finish("g_sgu_ln", total(gln, (slice(0, 1),)))
        finish("b_sgu_ln", total(gln, (slice(1, 2),)))
        tril = lax.broadcasted_iota(jnp.int32, (BLK, BLK), 0) >= lax.broadcasted_iota(jnp.int32, (BLK, BLK), 1)
        for g in range(N_GRP):
            finish("w_spatial", jnp.where(tril, total(gws, (g,)), 0.0), (0, g))
            finish("b_spatial", jnp.sum(total(gbs, (g,)).T, axis=0, keepdims=True), (0, slice(g, g + 1)))
        finish("g_q", total(ggq, ()))
        finish("g_k", total(ggk, ()))
        finish("attn_sinks", total(gsk, (slice(0, 1), slice(0, N_KV * Q_PER_KV))))

    args = list(gathered) + [params[k] for k in SMALL] + [ms[k] for k in SMALL] + [vs[k] for k in SMALL]
    out_shape = []
    for k in SMALL:
        out_shape += [jax.ShapeDtypeStruct(params[k].shape, F32)] * 4
    res = pl.pallas_call(body, name="small_update", in_specs=[_vmem()] * len(args),
                         out_specs=[_vmem()] * len(out_shape), out_shape=out_shape)(*args)
    return {k: res[4 * i:4 * i + 4] for i, k in enumerate(SMALL)}


def _place():
    return lax.axis_index("x"), lax.axis_index("y"), lax.axis_index("c")


def _all_gather(arrays, name):
    n = len(arrays)

    def body(*refs):
        ins = refs[:n]
        outs = refs[n:2 * n]
        send_sems, recv_sems, local_sems = refs[2 * n:]
        x, y, c = _place()
        me, sibling = (x, y, c), (x, y, 1 - c)
        chips = [(1 - x, y), (x, 1 - y), (1 - x, 1 - y)]

        def slot(a, px, py, pc):
            return outs[a].at[4 * px + 2 * py + pc]

        def copy(a, k, block, to, src=None):
            return pltpu.make_async_remote_copy(
                src_ref=slot(a, *block) if src is None else src, dst_ref=slot(a, *block),
                send_sem=send_sems.at[a, k], recv_sem=recv_sems.at[a, k], device_id=to, device_id_type=MESH)

        mine = [pltpu.make_async_copy(ins[a], slot(a, *me), local_sems.at[a]) for a in range(n)]
        first = []
        for a in range(n):
            mine[a].start()
            first.append(copy(a, 0, me, sibling, src=ins[a]))
            first += [copy(a, 1 + j, me, (*chip, c), src=ins[a]) for j, chip in enumerate(chips)]
        for cp in first:
            cp.start()
        passed = []
        for j, chip in enumerate(chips):
            for a in range(n):
                copy(a, 1 + j, (*chip, c), me).wait_recv()
                cp = copy(a, 4 + j, (*chip, c), sibling)
                cp.start()
                passed.append(cp)
        for a in range(n):
            copy(a, 0, sibling, me).wait_recv()
        for j, chip in enumerate(chips):
            for a in range(n):
                copy(a, 4 + j, (*chip, 1 - c), me).wait_recv()
        for cp in first + passed:
            cp.wait_send()
        for a in range(n):
            mine[a].wait()

    return pl.pallas_call(
        body, name=name, in_specs=[_any()] * n, out_specs=[_any()] * n,
        out_shape=[jax.ShapeDtypeStruct((N_DEV,) + a.shape, a.dtype) for a in arrays],
        scratch_shapes=[pltpu.SemaphoreType.DMA((n, 7)), pltpu.SemaphoreType.DMA((n, 7)),
                        pltpu.SemaphoreType.DMA((n,))],
    )(*arrays)


def _rs_sibling(ps):
    n = len(ps)

    def body(*refs):
        p_refs, r_refs = refs[:n], refs[n:2 * n]
        send_sems, recv_sems = refs[2 * n:]
        x, y, c = _place()
        copies = []
        for a in range(n):
            for q in range(4):
                copies.append(pltpu.make_async_remote_copy(
                    src_ref=p_refs[a].at[2 * q + (1 - c)], dst_ref=r_refs[a].at[q], send_sem=send_sems.at[a, q],
                    recv_sem=recv_sems.at[a, q], device_id=(x, y, 1 - c), device_id_type=MESH))
        for cp in copies:
            cp.start()
        for cp in copies:
            cp.wait()

    return pl.pallas_call(
        body, name="rs_sibling", in_specs=[_any()] * n, out_specs=[_any()] * n,
        out_shape=[jax.ShapeDtypeStruct((4,) + p.shape[1:], p.dtype) for p in ps],
        scratch_shapes=[pltpu.SemaphoreType.DMA((n, 4)), pltpu.SemaphoreType.DMA((n, 4))],
    )(*ps)


def _rs_add(p, r, core, name):
    _, rows, width = p.shape

    def body(c_ref, p_ref, r_ref, o_ref):
        o_ref[...] = (p_ref[...] + r_ref[...]).astype(BF16)

    return pl.pallas_call(
        body, name="rs_add_" + name, out_shape=jax.ShapeDtypeStruct((4, rows, width), BF16),
        grid_spec=pltpu.PrefetchScalarGridSpec(
            num_scalar_prefetch=1, grid=(4,),
            in_specs=[pl.BlockSpec((1, rows, width), lambda k, c_ref: (2 * k + c_ref[0], 0, 0)),
                      pl.BlockSpec((1, rows, width), lambda k, c_ref: (k, 0, 0))],
            out_specs=pl.BlockSpec((1, rows, width), lambda k, c_ref: (k, 0, 0))),
        compiler_params=_params("parallel"),
    )(core, p, r)


def _rs_chips(qs):
    n = len(qs)

    def body(*refs):
        q_refs, r_refs = refs[:n], refs[n:2 * n]
        send_sems, recv_sems, local_sems = refs[2 * n:]
        x, y, c = _place()
        my_chip = 2 * x + y
        chips = [(1 - x, y), (x, 1 - y), (1 - x, 1 - y)]

        def copy(a, j):
            px, py = chips[j]
            return pltpu.make_async_remote_copy(
                src_ref=q_refs[a].at[2 * px + py], dst_ref=r_refs[a].at[my_chip], send_sem=send_sems.at[a, j],
                recv_sem=recv_sems.at[a, j], device_id=(px, py, c), device_id_type=MESH)

        def arrival(a, j):
            px, py = chips[j]
            return pltpu.make_async_remote_copy(
                src_ref=q_refs[a].at[my_chip], dst_ref=r_refs[a].at[2 * px + py], send_sem=send_sems.at[a, j],
                recv_sem=recv_sems.at[a, j], device_id=(px, py, c), device_id_type=MESH)

        own = [pltpu.make_async_copy(q_refs[a].at[my_chip], r_refs[a].at[my_chip], local_sems.at[a])
               for a in range(n)]
        copies = [copy(a, j) for a in range(n) for j in range(3)]
        for cp in own + copies:
            cp.start()
        for a in range(n):
            for j in range(3):
                arrival(a, j).wait_recv()
        for cp in copies:
            cp.wait_send()
        for cp in own:
            cp.wait()

    return pl.pallas_call(
        body, name="rs_chips", in_specs=[_any()] * n, out_specs=[_any()] * n,
        out_shape=[jax.ShapeDtypeStruct(q.shape, q.dtype) for q in qs],
        scratch_shapes=[pltpu.SemaphoreType.DMA((n, 3)), pltpu.SemaphoreType.DMA((n, 3)),
                        pltpu.SemaphoreType.DMA((n,))],
    )(*qs)


def _rs_sum(r, name):
    def body(r_ref, g_ref):
        g = r_ref[0].astype(F32)
        for k in range(1, 4):
            g = g + r_ref[k].astype(F32)
        g_ref[...] = g

    return pl.pallas_call(body, name="rs_sum_" + name, in_specs=[_vmem()], out_specs=_vmem(),
                          out_shape=jax.ShapeDtypeStruct(r.shape[1:], F32))(r)


def _adam_big(w, m, v, name, g=None, r=None):
    from_parts = r is not None

    def body(g_ref, w_ref, m_ref, v_ref, go_ref, d_ref, nm_ref, nv_ref):
        if from_parts:
            g = g_ref[0].astype(F32)
            for k in range(1, 4):
                g = g + g_ref[k].astype(F32)
        else:
            g = g_ref[...]
        go_ref[...] = g
        d_ref[...], nm_ref[...], nv_ref[...] = _adamw(w_ref[...], g, m_ref[...], v_ref[...])

    return pl.pallas_call(body, name="adam_" + name, in_specs=[_vmem()] * 4, out_specs=[_vmem()] * 4,
                          out_shape=[jax.ShapeDtypeStruct(w.shape, F32)] * 4,
                          compiler_params=pltpu.CompilerParams(vmem_limit_bytes=VMEM_LIMIT),
                          )(r if from_parts else g, w, m, v)


def _local_step(x, tgt, mods, wts, small, seq):
    T = x.shape[0]
    nbs = seq // BLK
    g1, g2, g3 = small["g_norm1"], small["g_norm2"], small["g_norm3"]
    g_ln, b_ln = small["g_sgu_ln"], small["b_sgu_ln"]
    ws = small["w_spatial"]
    bsb = jnp.broadcast_to(small["b_spatial"][:, :, None], (N_GRP, BLK, BLK))
    gq, gk = small["g_q"], small["g_k"]
    sink_b = jnp.broadcast_to(
        jnp.repeat(small["attn_sinks"].reshape(N_KV, Q_PER_KV), BLK, axis=1)[:, :, None], (N_KV, Q_PER_KV * BLK, 128))
    f1 = (wts["ffn1_w_gate"], wts["ffn1_w_up"], wts["ffn1_w_down"])
    f2 = (wts["ffn2_w_gate"], wts["ffn2_w_up"], wts["ffn2_w_down"])
    mixw = (wts["w_branch_a"], wts["w_branch_b"], wts["w_out"])

    h1, y1 = _ffn_fwd(x, mods, g1, *f1, 0, seq)
    au, av, q, k, v, gta, gtb = _inproj_fwd(h1, mods, g2, wts["w_in"], seq)
    q_hm, k_hm, v_hm = _to_heads(q, k, v)
    ob = _q_from_heads(_attn_fwd(q_hm, k_hm, v_hm, gq, gk, sink_b, nbs))
    h2 = _mix_fwd(au, av, gta, gtb, ob, h1, mods, g_ln, b_ln, ws, bsb, *mixw, seq)
    dh3, y3, loss_parts = _ffn_fwd(h2, mods, g3, *f2, 6, seq, tgt=tgt)

    grads = {}
    dh2, xb, dyb, a, dg, du, mg3 = _ffn_bwd(h2, dh3, y3, mods, g3, *f2, 6, seq)
    grads["ffn2_w_gate"] = _wgrad(dg, xb, "wgrad_ffn2_gate")
    grads["ffn2_w_up"] = _wgrad(du, xb, "wgrad_ffn2_up")
    grads["ffn2_w_down"] = _wgrad(a, dyb, "wgrad_ffn2_down")

    (dau, dav, dgta, dgtb, dob, grads["w_out"], grads["w_branch_a"], grads["w_branch_b"], dws, dbs, dln,
     mg2g) = _mix_bwd(au, av, gta, gtb, ob, dh2, mods, g_ln, b_ln, ws, bsb, *mixw, seq)
    do_hm, _, _ = _to_heads(dob, k, v)
    dq_hm, dkp, dkc, dvp, dvc, dgq, dgk, dsk = _attn_bwd(q_hm, k_hm, v_hm, gq, gk, sink_b, do_hm, nbs)

    def shift(prev_part):
        return jnp.concatenate([prev_part[:, 1:], jnp.zeros_like(prev_part[:, :1])], axis=1)

    dq = _q_from_heads(dq_hm)
    dk = _kv_from_heads(dkc + shift(dkp))
    dv = _kv_from_heads(dvc + shift(dvp))
    dh1, xb2, dpb, mg2 = _inproj_bwd(h1, dh2, (dau, dav, dq, dk, dv, dgta, dgtb), mods, g2, wts["w_in"], seq)
    grads["w_in"] = _wgrad(dpb, xb2, "wgrad_w_in")

    dx, xb, dyb, a, dg, du, mg1 = _ffn_bwd(x, dh1, y1, mods, g1, *f1, 0, seq)
    grads["ffn1_w_gate"] = _wgrad(dg, xb, "wgrad_ffn1_gate")
    grads["ffn1_w_up"] = _wgrad(du, xb, "wgrad_ffn1_up")
    grads["ffn1_w_down"] = _wgrad(a, dyb, "wgrad_ffn1_down")

    dmods, dgn = _mod_finish(mg1, mg2, mg2g, mg3, mods, g1, g2, g3)
    return loss_parts, dx, grads, dmods, (dgn, dln, dws, dbs, dgq, dgk, dsk)


def kernel(x, c, w_ada, b_ada, g_norm1, ffn1_w_gate, ffn1_w_up, ffn1_w_down, g_norm2, w_in, g_sgu_ln, b_sgu_ln, w_spatial, b_spatial, g_q, g_k, attn_sinks, w_branch_a, w_branch_b, w_out, g_norm3, ffn2_w_gate, ffn2_w_up, ffn2_w_down, loss_target, m_w_ada, m_b_ada, m_g_norm1, m_ffn1_w_gate, m_ffn1_w_up, m_ffn1_w_down, m_g_norm2, m_w_in, m_g_sgu_ln, m_b_sgu_ln, m_w_spatial, m_b_spatial, m_g_q, m_g_k, m_attn_sinks, m_w_branch_a, m_w_branch_b, m_w_out, m_g_norm3, m_ffn2_w_gate, m_ffn2_w_up, m_ffn2_w_down, v_w_ada, v_b_ada, v_g_norm1, v_ffn1_w_gate, v_ffn1_w_up, v_ffn1_w_down, v_g_norm2, v_w_in, v_g_sgu_ln, v_b_sgu_ln, v_w_spatial, v_b_spatial, v_g_q, v_g_k, v_attn_sinks, v_w_branch_a, v_w_branch_b, v_w_out, v_g_norm3, v_ffn2_w_gate, v_ffn2_w_up, v_ffn2_w_down):
    names = ("w_ada", "b_ada", "g_norm1", "ffn1_w_gate", "ffn1_w_up", "ffn1_w_down", "g_norm2", "w_in", "g_sgu_ln",
             "b_sgu_ln", "w_spatial", "b_spatial", "g_q", "g_k", "attn_sinks", "w_branch_a", "w_branch_b", "w_out",
             "g_norm3", "ffn2_w_gate", "ffn2_w_up", "ffn2_w_down")
    w = dict(zip(names, (w_ada, b_ada, g_norm1, ffn1_w_gate, ffn1_w_up, ffn1_w_down, g_norm2, w_in, g_sgu_ln,
                         b_sgu_ln, w_spatial, b_spatial, g_q, g_k, attn_sinks, w_branch_a, w_branch_b, w_out, g_norm3,
                         ffn2_w_gate, ffn2_w_up, ffn2_w_down)))
    m = dict(zip(names, (m_w_ada, m_b_ada, m_g_norm1, m_ffn1_w_gate, m_ffn1_w_up, m_ffn1_w_down, m_g_norm2, m_w_in,
                         m_g_sgu_ln, m_b_sgu_ln, m_w_spatial, m_b_spatial, m_g_q, m_g_k, m_attn_sinks, m_w_branch_a,
                         m_w_branch_b, m_w_out, m_g_norm3, m_ffn2_w_gate, m_ffn2_w_up, m_ffn2_w_down)))
    v = dict(zip(names, (v_w_ada, v_b_ada, v_g_norm1, v_ffn1_w_gate, v_ffn1_w_up, v_ffn1_w_down, v_g_norm2, v_w_in,
                         v_g_sgu_ln, v_b_sgu_ln, v_w_spatial, v_b_spatial, v_g_q, v_g_k, v_attn_sinks, v_w_branch_a,
                         v_w_branch_b, v_w_out, v_g_norm3, v_ffn2_w_gate, v_ffn2_w_up, v_ffn2_w_down)))
    B, seq, _ = x.shape
    T = B * seq
    xi, yi, ci = _place()
    me = 4 * xi + 2 * yi + ci
    shards = [w[name][0].astype(BF16).T if tform else w[name][0].astype(BF16) for name, tform, _, _ in BIG]
    gathered_w = _all_gather(shards, "gather_weights")
    wts = {name: g.reshape(N_DEV * n, width) for (name, _, n, width), g in zip(BIG, gathered_w)}

    (c_all,) = _all_gather([c], "gather_c")
    c_all = c_all.reshape(N_DEV * B, D)
    (mods_cols,) = _all_gather([_ada_fwd(c_all, w_ada[0])], "gather_mods")
    mine = lax.dynamic_slice_in_dim(mods_cols, B * me, B, axis=1)
    mods = (mine.transpose(1, 0, 2).reshape(B, 9 * D) + b_ada).reshape(B, 9, D)

    small = {k: w[k] for k in ("g_q", "g_k", "g_norm1", "g_norm2", "g_norm3", "g_sgu_ln", "b_sgu_ln")}
    small.update({k: w[k][0] for k in ("w_spatial", "b_spatial", "attn_sinks")})
    loss_parts, dx, grads, dmods, small_parts = _local_step(
        x.reshape(T, D), loss_target.reshape(T, D), mods, wts, small, seq)
    loss = lax.psum(jnp.sum(loss_parts[:, 0, 0]), AXES)

    gathered = _all_gather([dmods.reshape(B, 9 * D)] + list(small_parts), "gather_small")
    small_out = _small_update(gathered, w, m, v)
    dm_cols = lax.dynamic_slice_in_dim(gathered[0].reshape(N_DEV * B, 9 * D), (9 * D // N_DEV) * me,
                                       9 * D // N_DEV, axis=1)
    ada_out = _ada_update(c_all, dm_cols, w_ada[0], m_w_ada[0], v_w_ada[0])

    core = jnp.reshape(ci, (1,)).astype(jnp.int32)
    ps = [grads[name].reshape(N_DEV, n, width) for name, _, n, width in BIG]
    qs = [_rs_add(p, r, core, name) for p, r, (name, _, _, _) in zip(ps, _rs_sibling(ps), BIG)]
    big_out = {}
    for r, (name, tform, _, _) in zip(_rs_chips(qs), BIG):
        if tform:
            res = _adam_big(w[name][0], m[name][0], v[name][0], name, g=_rs_sum(r, name).T)
        else:
            res = _adam_big(w[name][0], m[name][0], v[name][0], name, r=r)
        big_out[name] = [o[None] for o in res]

    def leaf(kind, name):
        if name == "w_ada":
            return ada_out[kind][None]
        if name in SMALL:
            return small_out[name][kind]
        return big_out[name][kind]

    return (loss, dx.reshape(B, seq, D), *[leaf(kind, name) for kind in range(4) for name in names])
```

```python
import functools
import math

import jax
import jax.numpy as jnp
from jax import lax
from jax.experimental import pallas as pl
from jax.experimental.pallas import tpu as pltpu

F32 = jnp.float32
BF16 = jnp.bfloat16
MESH = pl.DeviceIdType.MESH
AXES = ("x", "y", "c")
N_DEV = 8

VMEM_LIMIT = 56 * 1024 * 1024

D = 1024
FF = 2816
FC = 1408
D_A = 512
D_B = 512
HD = 64
N_KV = 2
Q_PER_KV = 4
BLK = 128
N_GRP = 4
IN_COLS = 3840
PIECES = (("au", 0, 512), ("av", 512, 512), ("q", 1024, 512), ("k", 1536, 128), ("v", 1664, 128),
          ("ga", 1792, 1024), ("gb", 2816, 1024))
EPS = 1e-6
NEG = -1e30
GELU_C = math.sqrt(2.0 / math.pi)

ADAM_LR = 0.001
ADAM_B1 = 0.9
ADAM_B2 = 0.999
ADAM_EPS = 1e-08
ADAM_WD = 0.01
ADAM_STEP = 10

NT = (((1,), (1,)), ((), ()))
TN = (((0,), (0,)), ((), ()))

BIG = (("ffn1_w_gate", True, FF // N_DEV, D), ("ffn1_w_up", True, FF // N_DEV, D),
       ("ffn1_w_down", False, FF // N_DEV, D), ("w_in", True, IN_COLS // N_DEV, D),
       ("w_branch_a", True, D // N_DEV, D_A), ("w_branch_b", True, D // N_DEV, D_B), ("w_out", False, D // N_DEV, D),
       ("ffn2_w_gate", True, FF // N_DEV, D), ("ffn2_w_up", True, FF // N_DEV, D),
       ("ffn2_w_down", False, FF // N_DEV, D))
SMALL = ("b_ada", "g_norm1", "g_norm2", "g_sgu_ln", "b_sgu_ln", "w_spatial", "b_spatial", "g_q", "g_k",
         "attn_sinks", "g_norm3")


def _dot(a, b):
    return jnp.dot(a, b, preferred_element_type=F32)


def _dot_nt(a, b):
    return lax.dot_general(a, b, NT, preferred_element_type=F32)


def _dot_tn(a, b):
    return lax.dot_general(a, b, TN, preferred_element_type=F32)


def _vmem():
    return pl.BlockSpec(memory_space=pltpu.VMEM)


def _any():
    return pl.BlockSpec(memory_space=pl.ANY)


def _rms_mod(h, g, sh, sc):
    inv = lax.rsqrt(jnp.mean(h * h, axis=-1, keepdims=True) + EPS)
    r = h * inv
    return (r * g) * (1.0 + sc) + sh, r, inv


def _rms_mod_bwd(dxn, r, inv, g, sc):
    dr = dxn * (g * (1.0 + sc))
    dh = inv * (dr - r * jnp.mean(dr * r, axis=-1, keepdims=True))
    return dh, jnp.sum(dxn, axis=0, keepdims=True), jnp.sum(dxn * r, axis=0, keepdims=True)


def _gelu(x):
    t = jnp.tanh(GELU_C * (x + 0.044715 * (x * x * x)))
    return 0.5 * x * (1.0 + t), t


def _gelu_grad(x, t):
    return 0.5 * (1.0 + t) + 0.5 * x * (1.0 - t * t) * (GELU_C * (1.0 + 3.0 * 0.044715 * x * x))


def _adamw(w, g, m, v):
    m = ADAM_B1 * m + (1.0 - ADAM_B1) * g
    v = ADAM_B2 * v + (1.0 - ADAM_B2) * (g * g)
    m_hat = m / (1.0 - ADAM_B1 ** ADAM_STEP)
    v_hat = v / (1.0 - ADAM_B2 ** ADAM_STEP)
    delta = -ADAM_LR * (m_hat / (jnp.sqrt(v_hat) + ADAM_EPS) + ADAM_WD * w)
    return delta, m, v


def _token_tile(seq, cap=512):
    return min(cap, seq)


def _params(*semantics):
    return pltpu.CompilerParams(dimension_semantics=semantics, vmem_limit_bytes=VMEM_LIMIT)


def _place():
    return lax.axis_index("x"), lax.axis_index("y"), lax.axis_index("c")


class _Gather:
    def __init__(self, arrays):
        n = len(arrays)
        self.ins = list(arrays)
        self.out_shape = [jax.ShapeDtypeStruct((N_DEV,) + a.shape, a.dtype) for a in arrays]
        self.scratch = [pltpu.SemaphoreType.DMA((n, 7)), pltpu.SemaphoreType.DMA((n, 7)),
                        pltpu.SemaphoreType.DMA((n,))]

    def _copies(self, ins, outs, sems):
        send_sems, recv_sems, local_sems = sems
        n = len(ins)
        x, y, c = _place()
        me, sibling = (x, y, c), (x, y, 1 - c)
        chips = [(1 - x, y), (x, 1 - y), (1 - x, 1 - y)]

        def slot(a, px, py, pc):
            return outs[a].at[4 * px + 2 * py + pc]

        def copy(a, k, block, to, src=None):
            return pltpu.make_async_remote_copy(
                src_ref=slot(a, *block) if src is None else src, dst_ref=slot(a, *block),
                send_sem=send_sems.at[a, k], recv_sem=recv_sems.at[a, k], device_id=to, device_id_type=MESH)

        mine = [pltpu.make_async_copy(ins[a], slot(a, *me), local_sems.at[a]) for a in range(n)]
        first = [copy(a, 0, me, sibling, src=ins[a]) for a in range(n)]
        first += [copy(a, 1 + j, me, (*chip, c), src=ins[a]) for a in range(n) for j, chip in enumerate(chips)]
        landed = [[copy(a, 1 + j, (*chip, c), me) for a in range(n)] for j, chip in enumerate(chips)]
        passed = [[copy(a, 4 + j, (*chip, c), sibling) for a in range(n)] for j, chip in enumerate(chips)]
        from_sibling = [copy(a, 0, sibling, me) for a in range(n)]
        from_sibling += [copy(a, 4 + j, (*chip, 1 - c), me) for a in range(n) for j, chip in enumerate(chips)]
        return mine, first, landed, passed, from_sibling

    def start(self, ins, outs, sems):
        mine, first, _, _, _ = self._copies(ins, outs, sems)
        for cp in mine + first:
            cp.start()

    def finish(self, ins, outs, sems):
        mine, first, landed, passed, from_sibling = self._copies(ins, outs, sems)
        for arrivals, forwards in zip(landed, passed):
            for arrival, forward in zip(arrivals, forwards):
                arrival.wait_recv()
                forward.start()
        for cp in from_sibling:
            cp.wait_recv()
        for cp in first + [f for fs in passed for f in fs]:
            cp.wait_send()
        for cp in mine:
            cp.wait()


class _RsSibling:
    def __init__(self, ps):
        n = len(ps)
        self.ins = list(ps)
        self.out_shape = [jax.ShapeDtypeStruct((4,) + p.shape[1:], p.dtype) for p in ps]
        self.scratch = [pltpu.SemaphoreType.DMA((n, 4)), pltpu.SemaphoreType.DMA((n, 4))]

    def _copies(self, ins, outs, sems):
        send_sems, recv_sems = sems
        x, y, c = _place()
        return [pltpu.make_async_remote_copy(
            src_ref=ins[a].at[2 * q + (1 - c)], dst_ref=outs[a].at[q], send_sem=send_sems.at[a, q],
            recv_sem=recv_sems.at[a, q], device_id=(x, y, 1 - c), device_id_type=MESH)
            for a in range(len(ins)) for q in range(4)]

    def start(self, ins, outs, sems):
        for cp in self._copies(ins, outs, sems):
            cp.start()

    def finish(self, ins, outs, sems):
        for cp in self._copies(ins, outs, sems):
            cp.wait()


class _RsChips:
    def __init__(self, qs):
        n = len(qs)
        self.ins = list(qs)
        self.out_shape = [jax.ShapeDtypeStruct(q.shape, q.dtype) for q in qs]
        self.scratch = [pltpu.SemaphoreType.DMA((n, 3)), pltpu.SemaphoreType.DMA((n, 3)),
                        pltpu.SemaphoreType.DMA((n,))]

    def _copies(self, ins, outs, sems):
        send_sems, recv_sems, local_sems = sems
        n = len(ins)
        x, y, c = _place()
        my_chip = 2 * x + y
        chips = [(1 - x, y), (x, 1 - y), (1 - x, 1 - y)]

        def copy(a, j, src_slot, dst_slot):
            px, py = chips[j]
            return pltpu.make_async_remote_copy(
                src_ref=ins[a].at[src_slot], dst_ref=outs[a].at[dst_slot], send_sem=send_sems.at[a, j],
                recv_sem=recv_sems.at[a, j], device_id=(px, py, c), device_id_type=MESH)

        own = [pltpu.make_async_copy(ins[a].at[my_chip], outs[a].at[my_chip], local_sems.at[a]) for a in range(n)]
        sends = [copy(a, j, 2 * px + py, my_chip) for a in range(n) for j, (px, py) in enumerate(chips)]
        arrivals = [copy(a, j, my_chip, 2 * px + py) for a in range(n) for j, (px, py) in enumerate(chips)]
        return own, sends, arrivals

    def start(self, ins, outs, sems):
        own, sends, _ = self._copies(ins, outs, sems)
        for cp in own + sends:
            cp.start()

    def finish(self, ins, outs, sems):
        own, sends, arrivals = self._copies(ins, outs, sems)
        for cp in arrivals:
            cp.wait_recv()
        for cp in sends:
            cp.wait_send()
        for cp in own:
            cp.wait()


def _split_plans(plans, refs_in, refs_out, refs_scr, phase):
    i = o = s = 0
    for p in plans:
        ni, no, ns = len(p.ins), len(p.out_shape), len(p.scratch)
        getattr(p, phase)(refs_in[i:i + ni], refs_out[o:o + no], refs_scr[s:s + ns])
        i, o, s = i + ni, o + no, s + ns


def _plan_results(plans, res):
    out = []
    for p in plans:
        out.append(list(res[:len(p.out_shape)]))
        res = res[len(p.out_shape):]
    return out


def _exchange(plans, name):
    c_in = [a for p in plans for a in p.ins]
    c_out = [s for p in plans for s in p.out_shape]
    c_scr = [s for p in plans for s in p.scratch]

    def body(*refs):
        cin, cout, cscr = refs[:len(c_in)], refs[len(c_in):len(c_in) + len(c_out)], refs[len(c_in) + len(c_out):]
        _split_plans(plans, cin, cout, cscr, "start")
        _split_plans(plans, cin, cout, cscr, "finish")

    res = pl.pallas_call(body, name=name, in_specs=[_any()] * len(c_in), out_specs=[_any()] * len(c_out),
                         out_shape=c_out, scratch_shapes=c_scr)(*c_in)
    return _plan_results(plans, res)


def _call(body, *, name, grid, in_specs, out_specs, out_shape, args, semantics, scratch_shapes=(), plans=()):
    n_in, n_out, n_scr = len(in_specs), len(out_specs), len(scratch_shapes)
    c_in = [a for p in plans for a in p.ins]
    c_out = [s for p in plans for s in p.out_shape]
    c_scr = [s for p in plans for s in p.scratch]

    def wrapped(*refs):
        ins, refs = refs[:n_in], refs[n_in:]
        cin, refs = refs[:len(c_in)], refs[len(c_in):]
        outs, refs = refs[:n_out], refs[n_out:]
        cout, refs = refs[:len(c_out)], refs[len(c_out):]
        scr, cscr = refs[:n_scr], refs[n_scr:]
        if plans:
            ids = [pl.program_id(d) for d in range(len(grid))]
            first = functools.reduce(jnp.logical_and, [i == 0 for i in ids])
            last = functools.reduce(jnp.logical_and, [i == g - 1 for i, g in zip(ids, grid)])
            pl.when(first)(lambda: _split_plans(plans, cin, cout, cscr, "start"))
        body(*ins, *outs, *scr)
        if plans:
            pl.when(last)(lambda: _split_plans(plans, cin, cout, cscr, "finish"))

    res = pl.pallas_call(
        wrapped, name=name, grid=grid, in_specs=list(in_specs) + [_any()] * len(c_in),
        out_specs=list(out_specs) + [_any()] * len(c_out), out_shape=list(out_shape) + c_out,
        scratch_shapes=list(scratch_shapes) + c_scr,
        compiler_params=_params(*(("arbitrary",) * len(grid) if plans else semantics)),
    )(*args, *c_in)
    return list(res[:n_out]), _plan_results(plans, res[n_out:])


def _ffn_fwd(h, mods, gn, wg, wu, wd, row0, seq, tgt=None, plans=()):
    T = h.shape[0]
    tm = _token_tile(seq)
    tps = seq // tm
    n_t = T // tm
    with_loss = tgt is not None

    def body(h_ref, m_ref, g_ref, wg_ref, wu_ref, wd_ref, *rest):
        if with_loss:
            tgt_ref, out_ref, y_ref, loss_ref = rest
        else:
            out_ref, y_ref = rest
        hv = h_ref[...]
        sh = m_ref[0, row0:row0 + 1, :]
        sc = m_ref[0, row0 + 1:row0 + 2, :]
        ga = m_ref[0, row0 + 2:row0 + 3, :]
        xn, _, _ = _rms_mod(hv, g_ref[...], sh, sc)
        xb = xn.astype(BF16)
        acc = jnp.zeros((tm, D), F32)
        for c0 in range(0, FF, FC):
            gg = _dot_nt(xb, wg_ref[c0:c0 + FC, :])
            uu = _dot_nt(xb, wu_ref[c0:c0 + FC, :])
            a = (gg * jax.nn.sigmoid(gg)) * uu
            acc = acc + _dot(a.astype(BF16), wd_ref[c0:c0 + FC, :])
        y_ref[...] = acc
        hout = hv + (0.5 * ga) * acc
        if with_loss:
            d = hout - tgt_ref[...]
            out_ref[...] = d * (1.0 / D)
            loss_ref[...] = jnp.full((1, 8, 128), 0.5 / D, F32) * jnp.sum(d * d)
        else:
            out_ref[...] = hout

    tok = pl.BlockSpec((tm, D), lambda i: (i, 0))
    in_specs = [tok, pl.BlockSpec((1, 9, D), lambda i: (i // tps, 0, 0)), pl.BlockSpec((1, D), lambda i: (0, 0)),
                _vmem(), _vmem(), _vmem()]
    out_shape = [jax.ShapeDtypeStruct((T, D), F32), jax.ShapeDtypeStruct((T, D), F32)]
    out_specs = [tok, tok]
    args = [h, mods, gn, wg, wu, wd]
    if with_loss:
        in_specs.append(tok)
        args.append(tgt)
        out_shape.append(jax.ShapeDtypeStruct((n_t, 8, 128), F32))
        out_specs.append(pl.BlockSpec((1, 8, 128), lambda i: (i, 0, 0)))
    return _call(body, name="ffn_fwd_loss" if with_loss else "ffn_fwd", grid=(n_t,), in_specs=in_specs,
                 out_specs=out_specs, out_shape=out_shape, args=args, semantics=("parallel",), plans=plans)


def _ffn_bwd(h, dhn, y, mods, gn, wg, wu, wd, row0, seq, plans=()):
    T = h.shape[0]
    B = T // seq
    tm = _token_tile(seq, 256)
    tps = seq // tm
    n_t = T // tm

    def body(h_ref, dhn_ref, y_ref, m_ref, g_ref, wg_ref, wu_ref, wd_ref,
             dh_ref, xb_ref, dyb_ref, a_ref, dg_ref, du_ref, mg_ref):
        i = pl.program_id(0)
        hv = h_ref[...]
        dhn = dhn_ref[...]
        sh = m_ref[0, row0:row0 + 1, :]
        sc = m_ref[0, row0 + 1:row0 + 2, :]
        ga = m_ref[0, row0 + 2:row0 + 3, :]
        g = g_ref[...]
        xn, r, inv = _rms_mod(hv, g, sh, sc)
        xb = xn.astype(BF16)
        xb_ref[...] = xb
        dyb = ((0.5 * ga) * dhn).astype(BF16)
        dyb_ref[...] = dyb
        dga = 0.5 * jnp.sum(dhn * y_ref[...], axis=0, keepdims=True)
        dxn = jnp.zeros((tm, D), F32)
        for c0 in range(0, FF, FC):
            wgc = wg_ref[c0:c0 + FC, :]
            wuc = wu_ref[c0:c0 + FC, :]
            gg = _dot_nt(xb, wgc)
            uu = _dot_nt(xb, wuc)
            sig = jax.nn.sigmoid(gg)
            s = gg * sig
            a_ref[:, c0:c0 + FC] = (s * uu).astype(BF16)
            da = _dot_nt(dyb, wd_ref[c0:c0 + FC, :])
            dub = (da * s).astype(BF16)
            dgb = (da * uu * (sig * (1.0 + gg * (1.0 - sig)))).astype(BF16)
            dg_ref[:, c0:c0 + FC] = dgb
            du_ref[:, c0:c0 + FC] = dub
            dxn = dxn + _dot(dgb, wgc) + _dot(dub, wuc)
        dh, s_dxn, s_dxr = _rms_mod_bwd(dxn, r, inv, g, sc)
        dh_ref[...] = dhn + dh

        @pl.when(i % tps == 0)
        def _():
            mg_ref[...] = jnp.zeros(mg_ref.shape, F32)

        mg_ref[0, 0:1, :] += s_dxn
        mg_ref[0, 1:2, :] += s_dxr
        mg_ref[0, 2:3, :] += dga

    tok = pl.BlockSpec((tm, D), lambda i: (i, 0))
    tokf = pl.BlockSpec((tm, FF), lambda i: (i, 0))
    return _call(
        body, name="ffn_bwd", grid=(n_t,),
        in_specs=[tok, tok, tok, pl.BlockSpec((1, 9, D), lambda i: (i // tps, 0, 0)),
                  pl.BlockSpec((1, D), lambda i: (0, 0)), _vmem(), _vmem(), _vmem()],
        out_specs=[tok, tok, tok, tokf, tokf, tokf, pl.BlockSpec((1, 8, D), lambda i: (i // tps, 0, 0))],
        out_shape=[jax.ShapeDtypeStruct((T, D), F32), jax.ShapeDtypeStruct((T, D), BF16),
                   jax.ShapeDtypeStruct((T, D), BF16), jax.ShapeDtypeStruct((T, FF), BF16),
                   jax.ShapeDtypeStruct((T, FF), BF16), jax.ShapeDtypeStruct((T, FF), BF16),
                   jax.ShapeDtypeStruct((B, 8, D), F32)],
        args=(h, dhn, y, mods, gn, wg, wu, wd), semantics=("arbitrary",), plans=plans)


def _wgrad(a, b, name, plans=()):
    T, da = a.shape
    db = b.shape[1]
    bm = {2816: 1408, 3840: 1280}[da]
    bn = db
    tk = min(1024, T)
    nk = T // tk

    def body(a_ref, b_ref, o_ref):
        @pl.when(pl.program_id(2) == 0)
        def _():
            o_ref[...] = jnp.zeros(o_ref.shape, F32)

        o_ref[...] += _dot_tn(a_ref[...], b_ref[...])

    (out,), plan_outs = _call(
        body, name=name, grid=(da // bm, db // bn, nk),
        in_specs=[pl.BlockSpec((tk, bm), lambda i, j, k: (k, i)), pl.BlockSpec((tk, bn), lambda i, j, k: (k, j))],
        out_specs=[pl.BlockSpec((bm, bn), lambda i, j, k: (i, j))], out_shape=[jax.ShapeDtypeStruct((da, db), F32)],
        args=(a, b), semantics=("parallel", "parallel", "arbitrary"), plans=plans)
    return out, plan_outs


def _inproj_fwd(h, mods, gn, w_in, seq):
    T = h.shape[0]
    tm = _token_tile(seq)
    tps = seq // tm

    def body(h_ref, m_ref, g_ref, w_ref, *outs):
        xn, _, _ = _rms_mod(h_ref[...], g_ref[...], m_ref[0, 3:4, :], m_ref[0, 4:5, :])
        xb = xn.astype(BF16)
        for (_, c0, w), o_ref in zip(PIECES, outs):
            o_ref[...] = _dot_nt(xb, w_ref[c0:c0 + w, :])

    return pl.pallas_call(
        body, name="inproj_fwd", grid=(T // tm,),
        in_specs=[pl.BlockSpec((tm, D), lambda i: (i, 0)), pl.BlockSpec((1, 9, D), lambda i: (i // tps, 0, 0)),
                  pl.BlockSpec((1, D), lambda i: (0, 0)), _vmem()],
        out_specs=[pl.BlockSpec((tm, w), lambda i: (i, 0)) for _, _, w in PIECES],
        out_shape=[jax.ShapeDtypeStruct((T, w), F32) for _, _, w in PIECES],
        compiler_params=_params("parallel"),
    )(h, mods, gn, w_in)


def _inproj_bwd(h, dh_res, dpieces, mods, gn, w_in, seq, plans=()):
    T = h.shape[0]
    B = T // seq
    tm = _token_tile(seq, 256)
    tps = seq // tm

    def body(h_ref, dres_ref, *rest):
        dp_refs = rest[:len(PIECES)]
        m_ref, g_ref, w_ref, dh_ref, xb_ref, dpb_ref, mg_ref = rest[len(PIECES):]
        i = pl.program_id(0)
        g = g_ref[...]
        sc = m_ref[0, 4:5, :]
        xn, r, inv = _rms_mod(h_ref[...], g, m_ref[0, 3:4, :], sc)
        xb_ref[...] = xn.astype(BF16)
        dxn = jnp.zeros((tm, D), F32)
        for (_, c0, w), dp_ref in zip(PIECES, dp_refs):
            dpb = dp_ref[...].astype(BF16)
            dpb_ref[:, c0:c0 + w] = dpb
            dxn = dxn + _dot(dpb, w_ref[c0:c0 + w, :])
        dh, s_dxn, s_dxr = _rms_mod_bwd(dxn, r, inv, g, sc)
        dh_ref[...] = dres_ref[...] + dh

        @pl.when(i % tps == 0)
        def _():
            mg_ref[...] = jnp.zeros(mg_ref.shape, F32)

        mg_ref[0, 0:1, :] += s_dxn
        mg_ref[0, 1:2, :] += s_dxr

    tok = pl.BlockSpec((tm, D), lambda i: (i, 0))
    return _call(
        body, name="inproj_bwd", grid=(T // tm,),
        in_specs=[tok, tok] + [pl.BlockSpec((tm, w), lambda i: (i, 0)) for _, _, w in PIECES]
        + [pl.BlockSpec((1, 9, D), lambda i: (i // tps, 0, 0)), pl.BlockSpec((1, D), lambda i: (0, 0)), _vmem()],
        out_specs=[tok, tok, pl.BlockSpec((tm, IN_COLS), lambda i: (i, 0)),
                   pl.BlockSpec((1, 8, D), lambda i: (i // tps, 0, 0))],
        out_shape=[jax.ShapeDtypeStruct((T, D), F32), jax.ShapeDtypeStruct((T, D), BF16),
                   jax.ShapeDtypeStruct((T, IN_COLS), BF16), jax.ShapeDtypeStruct((B, 8, D), F32)],
        args=(h, dh_res, *dpieces, mods, gn, w_in), semantics=("arbitrary",), plans=plans)


def _attn_math(q, kp, kc, vp, vc, gq, gk, sink, first):
    def rms(x, g):
        return (x * lax.rsqrt(jnp.mean(x * x, axis=-1, keepdims=True) + EPS)) * g

    qn = rms(q, gq)
    kn = rms(jnp.concatenate([kp, kc], axis=0), gk)
    v = jnp.concatenate([vp, vc], axis=0)
    s = _dot_nt(qn.astype(BF16), kn.astype(BF16)) * (HD ** -0.5)
    nq = Q_PER_KV * BLK
    row = lax.broadcasted_iota(jnp.int32, (nq, 2 * BLK), 0) & (BLK - 1)
    col = lax.broadcasted_iota(jnp.int32, (nq, 2 * BLK), 1)
    lo = jnp.where(first, BLK, 0)
    valid = (col <= row + BLK) & (col > row) & (col >= lo)
    s = jnp.where(valid, s, NEG)
    m = lax.stop_gradient(jnp.maximum(jnp.max(s, axis=-1, keepdims=True), sink))
    p = jnp.exp(s - m)
    den = jnp.sum(p, axis=-1, keepdims=True) + jnp.exp(sink - m)
    return _dot((p / den).astype(BF16), v.astype(BF16))


def _attn_specs(nbs):
    nq = Q_PER_KV * BLK
    qs = pl.BlockSpec((1, 1, nq, HD), lambda h, b: (h, b, 0, 0))
    prev = pl.BlockSpec((1, 1, BLK, HD), lambda h, b: (h, jnp.maximum(b - 1, 0), 0, 0))
    cur = pl.BlockSpec((1, 1, BLK, HD), lambda h, b: (h, b, 0, 0))
    gs = pl.BlockSpec((1, HD), lambda h, b: (0, 0))
    sk = pl.BlockSpec((1, nq, 128), lambda h, b: (h, 0, 0))
    return qs, prev, cur, gs, sk


def _attn_fwd(q_hm, k_hm, v_hm, gq, gk, sink_b, nbs, plans=()):
    nblk = q_hm.shape[1]
    qs, prev, cur, gs, sk = _attn_specs(nbs)

    def body(q_ref, kp_ref, kc_ref, vp_ref, vc_ref, gq_ref, gk_ref, sk_ref, o_ref):
        first = (pl.program_id(1) % nbs) == 0
        o_ref[0, 0] = _attn_math(q_ref[0, 0], kp_ref[0, 0], kc_ref[0, 0], vp_ref[0, 0], vc_ref[0, 0],
                                 gq_ref[...], gk_ref[...], sk_ref[0, :, 0:1], first)

    (out,), plan_outs = _call(
        body, name="attn_fwd", grid=(N_KV, nblk), in_specs=[qs, prev, cur, prev, cur, gs, gs, sk], out_specs=[qs],
        out_shape=[jax.ShapeDtypeStruct(q_hm.shape, F32)], args=(q_hm, k_hm, k_hm, v_hm, v_hm, gq, gk, sink_b),
        semantics=("parallel", "parallel"), plans=plans)
    return out, plan_outs


def _attn_bwd(q_hm, k_hm, v_hm, gq, gk, sink_b, do_hm, nbs, plans=()):
    nblk = q_hm.shape[1]
    qs, prev, cur, gs, sk = _attn_specs(nbs)
    nq = Q_PER_KV * BLK

    def body(q_ref, kp_ref, kc_ref, vp_ref, vc_ref, gq_ref, gk_ref, sk_ref, do_ref,
             dq_ref, dkp_ref, dkc_ref, dvp_ref, dvc_ref, dgq_ref, dgk_ref, dsk_ref):
        h = pl.program_id(0)
        b = pl.program_id(1)
        first = (b % nbs) == 0
        f = functools.partial(_attn_math, first=first)
        _, vjp = jax.vjp(f, q_ref[0, 0], kp_ref[0, 0], kc_ref[0, 0], vp_ref[0, 0], vc_ref[0, 0],
                         gq_ref[...], gk_ref[...], sk_ref[0, :, 0:1])
        dq, dkp, dkc, dvp, dvc, dgq, dgk, dsk = vjp(do_ref[0, 0])
        dq_ref[0, 0] = dq
        dkp_ref[0, 0] = dkp
        dkc_ref[0, 0] = dkc
        dvp_ref[0, 0] = dvp
        dvc_ref[0, 0] = dvc

        @pl.when((h == 0) & (b == 0))
        def _():
            for r in (dgq_ref, dgk_ref, dsk_ref):
                r[...] = jnp.zeros(r.shape, F32)

        dgq_ref[...] += dgq
        dgk_ref[...] += dgk
        lane = lax.broadcasted_iota(jnp.int32, (8, 128), 1)
        upd = jnp.zeros((8, 128), F32)
        for g in range(Q_PER_KV):
            tot = jnp.sum(dsk[g * BLK:(g + 1) * BLK, :])
            upd = upd + jnp.where(lane == h * Q_PER_KV + g, tot, 0.0)
        dsk_ref[...] += upd

    acc = pl.BlockSpec((1, HD), lambda h, b: (0, 0))
    return _call(
        body, name="attn_bwd", grid=(N_KV, nblk), in_specs=[qs, prev, cur, prev, cur, gs, gs, sk, qs],
        out_specs=[qs, cur, cur, cur, cur, acc, acc, pl.BlockSpec((8, 128), lambda h, b: (0, 0))],
        out_shape=[jax.ShapeDtypeStruct(q_hm.shape, F32)] + [jax.ShapeDtypeStruct(k_hm.shape, F32)] * 4
        + [jax.ShapeDtypeStruct((1, HD), F32)] * 2 + [jax.ShapeDtypeStruct((8, 128), F32)],
        args=(q_hm, k_hm, k_hm, v_hm, v_hm, gq, gk, sink_b, do_hm), semantics=("arbitrary", "arbitrary"),
        plans=plans)


def _to_heads(q, k, v):
    nblk = q.shape[0] // BLK
    q_hm = q.reshape(nblk, BLK, N_KV, Q_PER_KV, HD).transpose(2, 0, 3, 1, 4).reshape(N_KV, nblk, Q_PER_KV * BLK, HD)
    k_hm = k.reshape(nblk, BLK, N_KV, HD).transpose(2, 0, 1, 3)
    v_hm = v.reshape(nblk, BLK, N_KV, HD).transpose(2, 0, 1, 3)
    return q_hm, k_hm, v_hm


def _q_from_heads(q_hm):
    nblk = q_hm.shape[1]
    return q_hm.reshape(N_KV, nblk, Q_PER_KV, BLK, HD).transpose(1, 3, 0, 2, 4).reshape(nblk * BLK, N_KV * Q_PER_KV * HD)


def _kv_from_heads(k_hm):
    nblk = k_hm.shape[1]
    return k_hm.transpose(1, 2, 0, 3).reshape(nblk * BLK, N_KV * HD)


def _sgu_norm(av, g_ln, b_ln):
    t, th = _gelu(av)
    mu = jnp.mean(t, axis=-1, keepdims=True)
    tc = t - mu
    rstd = lax.rsqrt(jnp.mean(tc * tc, axis=-1, keepdims=True) + EPS)
    vhat = tc * rstd
    return vhat * g_ln + b_ln, vhat, rstd, th


def _masked_ws(ws_ref):
    tril = lax.broadcasted_iota(jnp.int32, (BLK, BLK), 0) >= lax.broadcasted_iota(jnp.int32, (BLK, BLK), 1)
    return [jnp.where(tril, ws_ref[g], 0.0).astype(BF16) for g in range(N_GRP)]


def _mix_fwd(au, av, gta, gtb, ob, h, mods, g_ln, b_ln, ws, bsb, wa, wb, wout, seq):
    T = h.shape[0]
    tm = _token_tile(seq)
    tps = seq // tm

    def body(au_ref, av_ref, gta_ref, gtb_ref, ob_ref, h_ref, m_ref, gl_ref, bl_ref, ws_ref, bs_ref,
             wa_ref, wb_ref, wo_ref, out_ref, vvb_s, z_s):
        u, _ = _gelu(au_ref[...])
        vv, _, _, _ = _sgu_norm(av_ref[...], gl_ref[...], bl_ref[...])
        vvb_s[...] = vv.astype(BF16)
        wsm = _masked_ws(ws_ref)
        for c in range(tm // BLK):
            rows = slice(c * BLK, (c + 1) * BLK)
            for g in range(N_GRP):
                cols = slice(g * BLK, (g + 1) * BLK)
                z_s[rows, cols] = _dot(wsm[g], vvb_s[rows, cols]) + bs_ref[g]
        ya = _dot_nt((u * z_s[...]).astype(BF16), wa_ref[...])
        yb = _dot_nt(ob_ref[...].astype(BF16), wb_ref[...])
        merged = jax.nn.sigmoid(gta_ref[...]) * ya + jax.nn.sigmoid(gtb_ref[...]) * yb
        out_ref[...] = h_ref[...] + m_ref[0, 5:6, :] * _dot(merged.astype(BF16), wo_ref[...])

    def tok(w):
        return pl.BlockSpec((tm, w), lambda i: (i, 0))

    def full(shape):
        return pl.BlockSpec(shape, lambda i: (0,) * len(shape))

    return pl.pallas_call(
        body, name="mix_fwd", grid=(T // tm,),
        in_specs=[tok(D_A), tok(D_A), tok(D), tok(D), tok(D_B), tok(D),
                  pl.BlockSpec((1, 9, D), lambda i: (i // tps, 0, 0)), full((1, D_A)), full((1, D_A)),
                  full((N_GRP, BLK, BLK)), full((N_GRP, BLK, BLK)), _vmem(), _vmem(), _vmem()],
        out_specs=tok(D), out_shape=jax.ShapeDtypeStruct((T, D), F32),
        scratch_shapes=[pltpu.VMEM((tm, D_A), BF16), pltpu.VMEM((tm, D_A), F32)],
        compiler_params=_params("parallel"),
    )(au, av, gta, gtb, ob, h, mods, g_ln, b_ln, ws, bsb, wa, wb, wout)


def _mix_bwd(au, av, gta, gtb, ob, dh, mods, g_ln, b_ln, ws, bsb, wa, wb, wout, seq, plans=()):
    T = dh.shape[0]
    B = T // seq
    tm = _token_tile(seq, 256)
    tps = seq // tm

    def body(au_ref, av_ref, gta_ref, gtb_ref, ob_ref, dh_ref, m_ref, gl_ref, bl_ref, ws_ref, bs_ref,
             wa_ref, wb_ref, wo_ref,
             dau_ref, dav_ref, dgta_ref, dgtb_ref, dob_ref, dwo_ref, dwa_ref, dwb_ref, dws_ref, dbs_ref, dln_ref,
             mg_ref, vvb_s, z_s, dz_s, dzb_s, dvv_s):
        i = pl.program_id(0)

        @pl.when(i == 0)
        def _():
            for r in (dwo_ref, dwa_ref, dwb_ref, dws_ref, dbs_ref, dln_ref):
                r[...] = jnp.zeros(r.shape, F32)

        @pl.when(i % tps == 0)
        def _():
            mg_ref[...] = jnp.zeros(mg_ref.shape, F32)

        auv = au_ref[...]
        avv = av_ref[...]
        u, thu = _gelu(auv)
        g_ln = gl_ref[...]
        vv, vhat, rstd, thv = _sgu_norm(avv, g_ln, bl_ref[...])
        vvb_s[...] = vv.astype(BF16)
        wsm = _masked_ws(ws_ref)
        for c in range(tm // BLK):
            rows = slice(c * BLK, (c + 1) * BLK)
            for g in range(N_GRP):
                cols = slice(g * BLK, (g + 1) * BLK)
                z_s[rows, cols] = _dot(wsm[g], vvb_s[rows, cols]) + bs_ref[g]
        z = z_s[...]
        yab = (u * z).astype(BF16)
        obb = ob_ref[...].astype(BF16)
        ya = _dot_nt(yab, wa_ref[...])
        yb = _dot_nt(obb, wb_ref[...])
        sa = jax.nn.sigmoid(gta_ref[...])
        sb = jax.nn.sigmoid(gtb_ref[...])
        mb = (sa * ya + sb * yb).astype(BF16)
        dhv = dh_ref[...]
        mg_ref[0, 0:1, :] += jnp.sum(dhv * _dot(mb, wo_ref[...]), axis=0, keepdims=True)
        dmob = (m_ref[0, 5:6, :] * dhv).astype(BF16)
        dwo_ref[...] += _dot_tn(mb, dmob)
        dmerged = _dot_nt(dmob, wo_ref[...])
        dya = dmerged * sa
        dyb = dmerged * sb
        dgta_ref[...] = dya * ya * (1.0 - sa)
        dgtb_ref[...] = dyb * yb * (1.0 - sb)
        dyab = dya.astype(BF16)
        dybb = dyb.astype(BF16)
        dwa_ref[...] += _dot_tn(dyab, yab)
        dwb_ref[...] += _dot_tn(dybb, obb)
        dob_ref[...] = _dot(dybb, wb_ref[...])
        dyap = _dot(dyab, wa_ref[...])
        dau_ref[...] = (dyap * z) * _gelu_grad(auv, thu)
        dz = dyap * u
        dz_s[...] = dz
        dzb_s[...] = dz.astype(BF16)
        for c in range(tm // BLK):
            rows = slice(c * BLK, (c + 1) * BLK)
            for g in range(N_GRP):
                cols = slice(g * BLK, (g + 1) * BLK)
                dzb = dzb_s[rows, cols]
                dvv_s[rows, cols] = _dot_tn(wsm[g], dzb)
                dws_ref[g] += _dot_nt(dzb, vvb_s[rows, cols])
                dbs_ref[g] += dz_s[rows, cols]
        dvv = dvv_s[...]
        dln_ref[0:1, :] += jnp.sum(dvv * vhat, axis=0, keepdims=True)
        dln_ref[1:2, :] += jnp.sum(dvv, axis=0, keepdims=True)
        dvh = dvv * g_ln
        dt = rstd * (dvh - jnp.mean(dvh, axis=-1, keepdims=True)
                     - vhat * jnp.mean(dvh * vhat, axis=-1, keepdims=True))
        dav_ref[...] = dt * _gelu_grad(avv, thv)

    def tok(w):
        return pl.BlockSpec((tm, w), lambda i: (i, 0))

    def full(shape):
        return pl.BlockSpec(shape, lambda i: (0,) * len(shape))

    return _call(
        body, name="mix_bwd", grid=(T // tm,),
        in_specs=[tok(D_A), tok(D_A), tok(D), tok(D), tok(D_B), tok(D),
                  pl.BlockSpec((1, 9, D), lambda i: (i // tps, 0, 0)), full((1, D_A)), full((1, D_A)),
                  full((N_GRP, BLK, BLK)), full((N_GRP, BLK, BLK)), _vmem(), _vmem(), _vmem()],
        out_specs=[tok(D_A), tok(D_A), tok(D), tok(D), tok(D_B), full((D, D)), full((D, D_A)), full((D, D_B)),
                   full((N_GRP, BLK, BLK)), full((N_GRP, BLK, BLK)), full((8, D_A)),
                   pl.BlockSpec((1, 8, D), lambda i: (i // tps, 0, 0))],
        out_shape=[jax.ShapeDtypeStruct((T, D_A), F32), jax.ShapeDtypeStruct((T, D_A), F32),
                   jax.ShapeDtypeStruct((T, D), F32), jax.ShapeDtypeStruct((T, D), F32),
                   jax.ShapeDtypeStruct((T, D_B), F32), jax.ShapeDtypeStruct((D, D), F32),
                   jax.ShapeDtypeStruct((D, D_A), F32), jax.ShapeDtypeStruct((D, D_B), F32),
                   jax.ShapeDtypeStruct((N_GRP, BLK, BLK), F32), jax.ShapeDtypeStruct((N_GRP, BLK, BLK), F32),
                   jax.ShapeDtypeStruct((8, D_A), F32), jax.ShapeDtypeStruct((B, 8, D), F32)],
        scratch_shapes=[pltpu.VMEM((tm, D_A), BF16), pltpu.VMEM((tm, D_A), F32), pltpu.VMEM((tm, D_A), F32),
                        pltpu.VMEM((tm, D_A), BF16), pltpu.VMEM((tm, D_A), F32)],
        args=(au, av, gta, gtb, ob, dh, mods, g_ln, b_ln, ws, bsb, wa, wb, wout), semantics=("arbitrary",),
        plans=plans)


def _ada_fwd(c_all, w_ada):
    def body(c_ref, w_ref, o_ref):
        cv = c_ref[...]
        o_ref[...] = _dot((cv * jax.nn.sigmoid(cv)).astype(BF16), w_ref[...].astype(BF16))

    return pl.pallas_call(body, name="ada_fwd", in_specs=[_vmem(), _vmem()], out_specs=_vmem(),
                          out_shape=jax.ShapeDtypeStruct((c_all.shape[0], w_ada.shape[1]), F32))(c_all, w_ada)


def _ada_update(c_all, dm_cols, w, m, v, plans=()):
    n, cols = c_all.shape[0], w.shape[1]

    def body(c_ref, dm_ref, w_ref, m_ref, v_ref, g_ref, d_ref, nm_ref, nv_ref):
        cv = c_ref[...]
        g = _dot_tn((cv * jax.nn.sigmoid(cv)).astype(BF16), dm_ref[...].astype(BF16))
        g_ref[...] = g
        d_ref[...], nm_ref[...], nv_ref[...] = _adamw(w_ref[...], g, m_ref[...], v_ref[...])

    col = pl.BlockSpec((D, 128), lambda j: (0, j))
    return _call(
        body, name="ada_update", grid=(cols // 128,),
        in_specs=[pl.BlockSpec((n, D), lambda j: (0, 0)), pl.BlockSpec((n, 128), lambda j: (0, j)), col, col, col],
        out_specs=[col] * 4, out_shape=[jax.ShapeDtypeStruct(w.shape, F32)] * 4, args=(c_all, dm_cols, w, m, v),
        semantics=("parallel",), plans=plans)


def _mod_finish(mg1, mg2, mg2g, mg3, mods, g1, g2, g3):
    B = mods.shape[0]

    def body(mg1_ref, mg2_ref, mg2g_ref, mg3_ref, m_ref, g1_ref, g2_ref, g3_ref, dm_ref, dgn_ref):
        dgn_ref[...] = jnp.zeros(dgn_ref.shape, F32)
        for k, (mg, g_ref) in enumerate(((mg1_ref, g1_ref), (mg2_ref, g2_ref), (mg3_ref, g3_ref))):
            for b in range(B):
                s_dxr = mg[b, 1:2, :]
                dm_ref[b, 3 * k:3 * k + 1, :] = mg[b, 0:1, :]
                dm_ref[b, 3 * k + 1:3 * k + 2, :] = g_ref[...] * s_dxr
                dm_ref[b, 3 * k + 2:3 * k + 3, :] = mg2g_ref[b, 0:1, :] if k == 1 else mg[b, 2:3, :]
                dgn_ref[k:k + 1, :] += (1.0 + m_ref[b, 3 * k + 1:3 * k + 2, :]) * s_dxr

    return pl.pallas_call(body, name="mod_finish", in_specs=[_vmem()] * 8, out_specs=[_vmem()] * 2,
                          out_shape=[jax.ShapeDtypeStruct((B, 9, D), F32), jax.ShapeDtypeStruct((8, D), F32)],
                          )(mg1, mg2, mg2g, mg3, mods, g1, g2, g3)


def _small_update(gathered, params, ms, vs):
    n = len(SMALL)
    B = gathered[0].shape[1]

    def body(*refs):
        gdm, ggn, gln, gws, gbs, ggq, ggk, gsk = refs[:8]
        w = dict(zip(SMALL, refs[8:8 + n]))
        m = dict(zip(SMALL, refs[8 + n:8 + 2 * n]))
        v = dict(zip(SMALL, refs[8 + 2 * n:8 + 3 * n]))
        outs = refs[8 + 3 * n:]
        out = {name: outs[4 * k:4 * k + 4] for k, name in enumerate(SMALL)}

        def total(ref, idx):
            acc = ref[(0,) + idx]
            for dev in range(1, N_DEV):
                acc = acc + ref[(dev,) + idx]
            return acc

        def finish(name, g, idx=(Ellipsis,)):
            d, nm, nv = _adamw(w[name][idx], g, m[name][idx], v[name][idx])
            for ref, val in zip(out[name], (g, d, nm, nv)):
                ref[idx] = val

        g_bada = total(gdm, (slice(0, 1),))
        for b in range(1, B):
            g_bada = g_bada + total(gdm, (slice(b, b + 1),))
        finish("b_ada", g_bada)
        finish("g_norm1", total(ggn, (slice(0, 1),)))
        finish("g_norm2", total(ggn, (slice(1, 2),)))
        finish("g_norm3", total(ggn, (slice(2, 3),)))
        finish("g_sgu_ln", total(gln, (slice(0, 1),)))
        finish("b_sgu_ln", total(gln, (slice(1, 2),)))
        tril = lax.broadcasted_iota(jnp.int32, (BLK, BLK), 0) >= lax.broadcasted_iota(jnp.int32, (BLK, BLK), 1)
        for g in range(N_GRP):
            finish("w_spatial", jnp.where(tril, total(gws, (g,)), 0.0), (0, g))
            finish("b_spatial", jnp.sum(total(gbs, (g,)).T, axis=0, keepdims=True), (0, slice(g, g + 1)))
        finish("g_q", total(ggq, ()))
        finish("g_k", total(ggk, ()))
        finish("attn_sinks", total(gsk, (slice(0, 1), slice(0, N_KV * Q_PER_KV))))

    args = list(gathered) + [params[k] for k in SMALL] + [ms[k] for k in SMALL] + [vs[k] for k in SMALL]
    out_shape = []
    for k in SMALL:
        out_shape += [jax.ShapeDtypeStruct(params[k].shape, F32)] * 4
    res = pl.pallas_call(body, name="small_update", in_specs=[_vmem()] * len(args),
                         out_specs=[_vmem()] * len(out_shape), out_shape=out_shape)(*args)
    return {k: res[4 * i:4 * i + 4] for i, k in enumerate(SMALL)}


def _rs_add(p, r, core, name):
    _, rows, width = p.shape

    def body(c_ref, p_ref, r_ref, o_ref):
        o_ref[...] = (p_ref[...] + r_ref[...]).astype(BF16)

    return pl.pallas_call(
        body, name="rs_add_" + name, out_shape=jax.ShapeDtypeStruct((4, rows, width), BF16),
        grid_spec=pltpu.PrefetchScalarGridSpec(
            num_scalar_prefetch=1, grid=(4,),
            in_specs=[pl.BlockSpec((1, rows, width), lambda k, c_ref: (2 * k + c_ref[0], 0, 0)),
                      pl.BlockSpec((1, rows, width), lambda k, c_ref: (k, 0, 0))],
            out_specs=pl.BlockSpec((1, rows, width), lambda k, c_ref: (k, 0, 0))),
        compiler_params=_params("parallel"),
    )(core, p, r)


def _rs_sum(r, name):
    def body(r_ref, g_ref):
        g = r_ref[0].astype(F32)
        for k in range(1, 4):
            g = g + r_ref[k].astype(F32)
        g_ref[...] = g

    return pl.pallas_call(body, name="rs_sum_" + name, in_specs=[_vmem()], out_specs=_vmem(),
                          out_shape=jax.ShapeDtypeStruct(r.shape[1:], F32))(r)


def _adam_big(w, m, v, name, g=None, r=None):
    from_parts = r is not None

    def body(g_ref, w_ref, m_ref, v_ref, go_ref, d_ref, nm_ref, nv_ref):
        if from_parts:
            g = g_ref[0].astype(F32)
            for k in range(1, 4):
                g = g + g_ref[k].astype(F32)
        else:
            g = g_ref[...]
        go_ref[...] = g
        d_ref[...], nm_ref[...], nv_ref[...] = _adamw(w_ref[...], g, m_ref[...], v_ref[...])

    return pl.pallas_call(body, name="adam_" + name, in_specs=[_vmem()] * 4, out_specs=[_vmem()] * 4,
                          out_shape=[jax.ShapeDtypeStruct(w.shape, F32)] * 4,
                          compiler_params=pltpu.CompilerParams(vmem_limit_bytes=VMEM_LIMIT),
                          )(r if from_parts else g, w, m, v)


def kernel(x, c, w_ada, b_ada, g_norm1, ffn1_w_gate, ffn1_w_up, ffn1_w_down, g_norm2, w_in, g_sgu_ln, b_sgu_ln, w_spatial, b_spatial, g_q, g_k, attn_sinks, w_branch_a, w_branch_b, w_out, g_norm3, ffn2_w_gate, ffn2_w_up, ffn2_w_down, loss_target, m_w_ada, m_b_ada, m_g_norm1, m_ffn1_w_gate, m_ffn1_w_up, m_ffn1_w_down, m_g_norm2, m_w_in, m_g_sgu_ln, m_b_sgu_ln, m_w_spatial, m_b_spatial, m_g_q, m_g_k, m_attn_sinks, m_w_branch_a, m_w_branch_b, m_w_out, m_g_norm3, m_ffn2_w_gate, m_ffn2_w_up, m_ffn2_w_down, v_w_ada, v_b_ada, v_g_norm1, v_ffn1_w_gate, v_ffn1_w_up, v_ffn1_w_down, v_g_norm2, v_w_in, v_g_sgu_ln, v_b_sgu_ln, v_w_spatial, v_b_spatial, v_g_q, v_g_k, v_attn_sinks, v_w_branch_a, v_w_branch_b, v_w_out, v_g_norm3, v_ffn2_w_gate, v_ffn2_w_up, v_ffn2_w_down):
    names = ("w_ada", "b_ada", "g_norm1", "ffn1_w_gate", "ffn1_w_up", "ffn1_w_down", "g_norm2", "w_in", "g_sgu_ln",
             "b_sgu_ln", "w_spatial", "b_spatial", "g_q", "g_k", "attn_sinks", "w_branch_a", "w_branch_b", "w_out",
             "g_norm3", "ffn2_w_gate", "ffn2_w_up", "ffn2_w_down")
    w = dict(zip(names, (w_ada, b_ada, g_norm1, ffn1_w_gate, ffn1_w_up, ffn1_w_down, g_norm2, w_in, g_sgu_ln,
                         b_sgu_ln, w_spatial, b_spatial, g_q, g_k, attn_sinks, w_branch_a, w_branch_b, w_out, g_norm3,
                         ffn2_w_gate, ffn2_w_up, ffn2_w_down)))
    m = dict(zip(names, (m_w_ada, m_b_ada, m_g_norm1, m_ffn1_w_gate, m_ffn1_w_up, m_ffn1_w_down, m_g_norm2, m_w_in,
                         m_g_sgu_ln, m_b_sgu_ln, m_w_spatial, m_b_spatial, m_g_q, m_g_k, m_attn_sinks, m_w_branch_a,
                         m_w_branch_b, m_w_out, m_g_norm3, m_ffn2_w_gate, m_ffn2_w_up, m_ffn2_w_down)))
    v = dict(zip(names, (v_w_ada, v_b_ada, v_g_norm1, v_ffn1_w_gate, v_ffn1_w_up, v_ffn1_w_down, v_g_norm2, v_w_in,
                         v_g_sgu_ln, v_b_sgu_ln, v_w_spatial, v_b_spatial, v_g_q, v_g_k, v_attn_sinks, v_w_branch_a,
                         v_w_branch_b, v_w_out, v_g_norm3, v_ffn2_w_gate, v_ffn2_w_up, v_ffn2_w_down)))
    B, seq, _ = x.shape
    T = B * seq
    nbs = seq // BLK
    xi, yi, ci = _place()
    me = 4 * xi + 2 * yi + ci
    core = jnp.reshape(ci, (1,)).astype(jnp.int32)
    layout = {name: (tform, n, width) for name, tform, n, width in BIG}
    shard = {name: w[name][0].astype(BF16).T if tform else w[name][0].astype(BF16) for name, tform, _, _ in BIG}
    wts, parts, big_out = {}, {}, {}

    def gather_plan(group):
        return _Gather([shard[k] for k in group])

    def take(group, gathered):
        for k, g in zip(group, gathered):
            wts[k] = g.reshape(N_DEV * layout[k][1], layout[k][2])

    def blocks(name, grad):
        return grad.reshape(N_DEV, layout[name][1], layout[name][2])

    def to_sibling(names, grads):
        return _RsSibling([blocks(k, g) for k, g in zip(names, grads)])

    def add(names, grads, from_sibling):
        for k, g, r in zip(names, grads, from_sibling):
            parts[k] = _rs_add(blocks(k, g), r, core, k)

    def to_chips(names):
        return _RsChips([parts[k] for k in names])

    def update(names, from_chips):
        for k, r in zip(names, from_chips):
            if layout[k][0]:
                res = _adam_big(w[k][0], m[k][0], v[k][0], k, g=_rs_sum(r, k).T)
            else:
                res = _adam_big(w[k][0], m[k][0], v[k][0], k, r=r)
            big_out[k] = [o[None] for o in res]

    g1, g2, g3, g_ln, b_ln, gq, gk = g_norm1, g_norm2, g_norm3, g_sgu_ln, b_sgu_ln, g_q, g_k
    ws = w_spatial[0]
    bsb = jnp.broadcast_to(b_spatial[0][:, :, None], (N_GRP, BLK, BLK))
    sink_b = jnp.broadcast_to(
        jnp.repeat(attn_sinks[0].reshape(N_KV, Q_PER_KV), BLK, axis=1)[:, :, None], (N_KV, Q_PER_KV * BLK, 128))
    xf = x.reshape(T, D)
    tgt = loss_target.reshape(T, D)

    ((c_all,),) = _exchange([_Gather([c])], "gather_c")
    c_all = c_all.reshape(N_DEV * B, D)
    ffn1 = ("ffn1_w_gate", "ffn1_w_up", "ffn1_w_down")
    ffn2 = ("ffn2_w_gate", "ffn2_w_up", "ffn2_w_down")
    ((mods_cols, *gathered),) = _exchange(
        [_Gather([_ada_fwd(c_all, w_ada[0])] + [shard[k] for k in ffn1])], "gather_first")
    take(ffn1, gathered)
    mine = lax.dynamic_slice_in_dim(mods_cols, B * me, B, axis=1)
    mods = (mine.transpose(1, 0, 2).reshape(B, 9 * D) + b_ada).reshape(B, 9, D)

    group = ("w_in", "w_branch_a", "w_branch_b", "w_out", "ffn2_w_gate")
    (h1, y1), (gathered,) = _ffn_fwd(xf, mods, g1, *[wts[k] for k in ffn1], 0, seq, plans=[gather_plan(group)])
    take(group, gathered)
    au, av, q_tok, k_tok, v_tok, gta, gtb = _inproj_fwd(h1, mods, g2, wts["w_in"], seq)
    q_hm, k_hm, v_hm = _to_heads(q_tok, k_tok, v_tok)
    group = ("ffn2_w_up", "ffn2_w_down")
    o_hm, (gathered,) = _attn_fwd(q_hm, k_hm, v_hm, gq, gk, sink_b, nbs, plans=[gather_plan(group)])
    take(group, gathered)
    ob = _q_from_heads(o_hm)
    mixw = (wts["w_branch_a"], wts["w_branch_b"], wts["w_out"])
    h2 = _mix_fwd(au, av, gta, gtb, ob, h1, mods, g_ln, b_ln, ws, bsb, *mixw, seq)
    (dh3, y3, loss_parts), _ = _ffn_fwd(h2, mods, g3, *[wts[k] for k in ffn2], 6, seq, tgt=tgt)
    loss = lax.psum(jnp.sum(loss_parts[:, 0, 0]), AXES)

    (dh2, xb, dyb, a, dg, du, mg3), _ = _ffn_bwd(h2, dh3, y3, mods, g3, *[wts[k] for k in ffn2], 6, seq)
    d_gate, _ = _wgrad(dg, xb, "wgrad_ffn2_gate")
    d_up, (r,) = _wgrad(du, xb, "wgrad_ffn2_up", plans=[to_sibling(ffn2[:1], [d_gate])])
    add(ffn2[:1], [d_gate], r)
    d_down, (r,) = _wgrad(a, dyb, "wgrad_ffn2_down", plans=[to_sibling(ffn2[1:2], [d_up])])
    add(ffn2[1:2], [d_up], r)

    (dau, dav, dgta, dgtb, dob, d_out, d_a, d_b, dws, dbs, dln, mg2g), (r, from_chips) = _mix_bwd(
        au, av, gta, gtb, ob, dh2, mods, g_ln, b_ln, ws, bsb, *mixw, seq,
        plans=[to_sibling(ffn2[2:], [d_down]), to_chips(ffn2[:2])])
    add(ffn2[2:], [d_down], r)
    update(ffn2[:2], from_chips)
    do_hm, _, _ = _to_heads(dob, k_tok, v_tok)
    mixers = ("w_out", "w_branch_a", "w_branch_b")
    (dq_hm, dkp, dkc, dvp, dvc, dgq, dgk, dsk), (from_chips, r) = _attn_bwd(
        q_hm, k_hm, v_hm, gq, gk, sink_b, do_hm, nbs,
        plans=[to_chips(ffn2[2:]), to_sibling(mixers, [d_out, d_a, d_b])])
    update(ffn2[2:], from_chips)
    add(mixers, [d_out, d_a, d_b], r)

    def shift(prev_part):
        return jnp.concatenate([prev_part[:, 1:], jnp.zeros_like(prev_part[:, :1])], axis=1)

    dq = _q_from_heads(dq_hm)
    dk = _kv_from_heads(dkc + shift(dkp))
    dv = _kv_from_heads(dvc + shift(dvp))
    (dh1, xb2, dpb, mg2), (from_chips,) = _inproj_bwd(
        h1, dh2, (dau, dav, dq, dk, dv, dgta, dgtb), mods, g2, wts["w_in"], seq, plans=[to_chips(mixers)])
    update(mixers, from_chips)
    d_in, _ = _wgrad(dpb, xb2, "wgrad_w_in")

    (dx, xb, dyb, a, dg, du, mg1), (r,) = _ffn_bwd(
        xf, dh1, y1, mods, g1, *[wts[k] for k in ffn1], 0, seq, plans=[to_sibling(("w_in",), [d_in])])
    add(("w_in",), [d_in], r)
    dmods, dgn = _mod_finish(mg1, mg2, mg2g, mg3, mods, g1, g2, g3)
    small_parts = [dmods.reshape(B, 9 * D), dgn, dln, dws, dbs, dgq, dgk, dsk]
    d_gate, (from_chips, gathered) = _wgrad(dg, xb, "wgrad_ffn1_gate",
                                            plans=[to_chips(("w_in",)), _Gather(small_parts)])
    update(("w_in",), from_chips)
    d_up, (r,) = _wgrad(du, xb, "wgrad_ffn1_up", plans=[to_sibling(ffn1[:1], [d_gate])])
    add(ffn1[:1], [d_gate], r)
    d_down, (r, from_chips) = _wgrad(a, dyb, "wgrad_ffn1_down",
                                     plans=[to_sibling(ffn1[1:2], [d_up]), to_chips(ffn1[:1])])
    add(ffn1[1:2], [d_up], r)
    update(ffn1[:1], from_chips)
    small_out = _small_update(gathered, w, m, v)
    dm_cols = lax.dynamic_slice_in_dim(gathered[0].reshape(N_DEV * B, 9 * D), (9 * D // N_DEV) * me,
                                       9 * D // N_DEV, axis=1)
    ada_out, (r,) = _ada_update(c_all, dm_cols, w_ada[0], m_w_ada[0], v_w_ada[0],
                                plans=[to_sibling(ffn1[2:], [d_down])])
    add(ffn1[2:], [d_down], r)
    (from_chips,) = _exchange([to_chips(ffn1[1:])], "rs_last")
    update(ffn1[1:], from_chips)

    def leaf(kind, name):
        if name == "w_ada":
            return ada_out[kind][None]
        if name in SMALL:
            return small_out[name][kind]
        return big_out[name][kind]

    return (loss, dx.reshape(B, seq, D), *[leaf(kind, name) for kind in range(4) for name in names])
```

```python
import functools
import math

import jax
import jax.numpy as jnp
from jax import lax
from jax.experimental import pallas as pl
from jax.experimental.pallas import tpu as pltpu

F32 = jnp.float32
BF16 = jnp.bfloat16
MESH = pl.DeviceIdType.MESH
AXES = ("x", "y", "c")
N_DEV = 8

VMEM_LIMIT = 56 * 1024 * 1024

D = 1024
FF = 2816
FC = 1408
D_A = 512
D_B = 512
HD = 64
N_KV = 2
Q_PER_KV = 4
BLK = 128
N_GRP = 4
IN_COLS = 3840
PIECES = (("au", 0, 512), ("av", 512, 512), ("q", 1024, 512), ("k", 1536, 128), ("v", 1664, 128),
          ("ga", 1792, 1024), ("gb", 2816, 1024))
EPS = 1e-6
NEG = -1e30
GELU_C = math.sqrt(2.0 / math.pi)

ADAM_LR = 0.001
ADAM_B1 = 0.9
ADAM_B2 = 0.999
ADAM_EPS = 1e-08
ADAM_WD = 0.01
ADAM_STEP = 10

NT = (((1,), (1,)), ((), ()))
TN = (((0,), (0,)), ((), ()))

BIG = (("ffn1_w_gate", True, FF // N_DEV, D), ("ffn1_w_up", True, FF // N_DEV, D),
       ("ffn1_w_down", False, FF // N_DEV, D), ("w_in", True, IN_COLS // N_DEV, D),
       ("w_branch_a", True, D // N_DEV, D_A), ("w_branch_b", True, D // N_DEV, D_B), ("w_out", False, D // N_DEV, D),
       ("ffn2_w_gate", True, FF // N_DEV, D), ("ffn2_w_up", True, FF // N_DEV, D),
       ("ffn2_w_down", False, FF // N_DEV, D))
SMALL = ("b_ada", "g_norm1", "g_norm2", "g_sgu_ln", "b_sgu_ln", "w_spatial", "b_spatial", "g_q", "g_k",
         "attn_sinks", "g_norm3")


def _dot(a, b):
    return jnp.dot(a, b, preferred_element_type=F32)


def _dot_nt(a, b):
    return lax.dot_general(a, b, NT, preferred_element_type=F32)


def _dot_tn(a, b):
    return lax.dot_general(a, b, TN, preferred_element_type=F32)


def _vmem():
    return pl.BlockSpec(memory_space=pltpu.VMEM)


def _any():
    return pl.BlockSpec(memory_space=pl.ANY)


def _rms_mod(h, g, sh, sc):
    inv = lax.rsqrt(jnp.mean(h * h, axis=-1, keepdims=True) + EPS)
    r = h * inv
    return (r * g) * (1.0 + sc) + sh, r, inv


def _rms_mod_bwd(dxn, r, inv, g, sc):
    dr = dxn * (g * (1.0 + sc))
    dh = inv * (dr - r * jnp.mean(dr * r, axis=-1, keepdims=True))
    return dh, jnp.sum(dxn, axis=0, keepdims=True), jnp.sum(dxn * r, axis=0, keepdims=True)


def _gelu(x):
    t = jnp.tanh(GELU_C * (x + 0.044715 * (x * x * x)))
    return 0.5 * x * (1.0 + t), t


def _gelu_grad(x, t):
    return 0.5 * (1.0 + t) + 0.5 * x * (1.0 - t * t) * (GELU_C * (1.0 + 3.0 * 0.044715 * x * x))


def _adamw(w, g, m, v):
    m = ADAM_B1 * m + (1.0 - ADAM_B1) * g
    v = ADAM_B2 * v + (1.0 - ADAM_B2) * (g * g)
    m_hat = m / (1.0 - ADAM_B1 ** ADAM_STEP)
    v_hat = v / (1.0 - ADAM_B2 ** ADAM_STEP)
    delta = -ADAM_LR * (m_hat / (jnp.sqrt(v_hat) + ADAM_EPS) + ADAM_WD * w)
    return delta, m, v


def _token_tile(seq, cap=512):
    return min(cap, seq)


def _params(*semantics):
    return pltpu.CompilerParams(dimension_semantics=semantics, vmem_limit_bytes=VMEM_LIMIT)


def _place():
    return lax.axis_index("x"), lax.axis_index("y"), lax.axis_index("c")


class _Gather:
    def __init__(self, arrays):
        n = len(arrays)
        self.ins = list(arrays)
        self.out_shape = [jax.ShapeDtypeStruct((N_DEV,) + a.shape, a.dtype) for a in arrays]
        self.scratch = [pltpu.SemaphoreType.DMA((n, 7)), pltpu.SemaphoreType.DMA((n, 7)),
                        pltpu.SemaphoreType.DMA((n,))]

    def _copies(self, ins, outs, sems):
        send_sems, recv_sems, local_sems = sems
        n = len(ins)
        x, y, c = _place()
        me, sibling = (x, y, c), (x, y, 1 - c)
        chips = [(1 - x, y), (x, 1 - y), (1 - x, 1 - y)]

        def slot(a, px, py, pc):
            return outs[a].at[4 * px + 2 * py + pc]

        def copy(a, k, block, to, src=None):
            return pltpu.make_async_remote_copy(
                src_ref=slot(a, *block) if src is None else src, dst_ref=slot(a, *block),
                send_sem=send_sems.at[a, k], recv_sem=recv_sems.at[a, k], device_id=to, device_id_type=MESH)

        mine = [pltpu.make_async_copy(ins[a], slot(a, *me), local_sems.at[a]) for a in range(n)]
        first = [copy(a, 0, me, sibling, src=ins[a]) for a in range(n)]
        first += [copy(a, 1 + j, me, (*chip, c), src=ins[a]) for a in range(n) for j, chip in enumerate(chips)]
        landed = [[copy(a, 1 + j, (*chip, c), me) for a in range(n)] for j, chip in enumerate(chips)]
        passed = [[copy(a, 4 + j, (*chip, c), sibling) for a in range(n)] for j, chip in enumerate(chips)]
        from_sibling = [copy(a, 0, sibling, me) for a in range(n)]
        from_sibling += [copy(a, 4 + j, (*chip, 1 - c), me) for a in range(n) for j, chip in enumerate(chips)]
        return mine, first, landed, passed, from_sibling

    def start(self, ins, outs, sems):
        mine, first, _, _, _ = self._copies(ins, outs, sems)
        for cp in mine + first:
            cp.start()

    def finish(self, ins, outs, sems):
        mine, first, landed, passed, from_sibling = self._copies(ins, outs, sems)
        for arrivals, forwards in zip(landed, passed):
            for arrival, forward in zip(arrivals, forwards):
                arrival.wait_recv()
                forward.start()
        for cp in from_sibling:
            cp.wait_recv()
        for cp in first + [f for fs in passed for f in fs]:
            cp.wait_send()
        for cp in mine:
            cp.wait()


class _RsSibling:
    def __init__(self, ps):
        n = len(ps)
        self.ins = list(ps)
        self.out_shape = [jax.ShapeDtypeStruct((4,) + p.shape[1:], p.dtype) for p in ps]
        self.scratch = [pltpu.SemaphoreType.DMA((n, 4)), pltpu.SemaphoreType.DMA((n, 4))]

    def _copies(self, ins, outs, sems):
        send_sems, recv_sems = sems
        x, y, c = _place()
        return [pltpu.make_async_remote_copy(
            src_ref=ins[a].at[2 * q + (1 - c)], dst_ref=outs[a].at[q], send_sem=send_sems.at[a, q],
            recv_sem=recv_sems.at[a, q], device_id=(x, y, 1 - c), device_id_type=MESH)
            for a in range(len(ins)) for q in range(4)]

    def start(self, ins, outs, sems):
        for cp in self._copies(ins, outs, sems):
            cp.start()

    def finish(self, ins, outs, sems):
        for cp in self._copies(ins, outs, sems):
            cp.wait()


class _RsChips:
    def __init__(self, qs):
        n = len(qs)
        self.ins = list(qs)
        self.out_shape = [jax.ShapeDtypeStruct(q.shape, q.dtype) for q in qs]
        self.scratch = [pltpu.SemaphoreType.DMA((n, 3)), pltpu.SemaphoreType.DMA((n, 3)),
                        pltpu.SemaphoreType.DMA((n,))]

    def _copies(self, ins, outs, sems):
        send_sems, recv_sems, local_sems = sems
        n = len(ins)
        x, y, c = _place()
        my_chip = 2 * x + y
        chips = [(1 - x, y), (x, 1 - y), (1 - x, 1 - y)]

        def copy(a, j, src_slot, dst_slot):
            px, py = chips[j]
            return pltpu.make_async_remote_copy(
                src_ref=ins[a].at[src_slot], dst_ref=outs[a].at[dst_slot], send_sem=send_sems.at[a, j],
                recv_sem=recv_sems.at[a, j], device_id=(px, py, c), device_id_type=MESH)

        own = [pltpu.make_async_copy(ins[a].at[my_chip], outs[a].at[my_chip], local_sems.at[a]) for a in range(n)]
        sends = [copy(a, j, 2 * px + py, my_chip) for a in range(n) for j, (px, py) in enumerate(chips)]
        arrivals = [copy(a, j, my_chip, 2 * px + py) for a in range(n) for j, (px, py) in enumerate(chips)]
        return own, sends, arrivals

    def start(self, ins, outs, sems):
        own, sends, _ = self._copies(ins, outs, sems)
        for cp in own + sends:
            cp.start()

    def finish(self, ins, outs, sems):
        own, sends, arrivals = self._copies(ins, outs, sems)
        for cp in arrivals:
            cp.wait_recv()
        for cp in sends:
            cp.wait_send()
        for cp in own:
            cp.wait()


def _split_plans(plans, refs_in, refs_out, refs_scr, phase):
    i = o = s = 0
    for p in plans:
        ni, no, ns = len(p.ins), len(p.out_shape), len(p.scratch)
        getattr(p, phase)(refs_in[i:i + ni], refs_out[o:o + no], refs_scr[s:s + ns])
        i, o, s = i + ni, o + no, s + ns


def _plan_results(plans, res):
    out = []
    for p in plans:
        out.append(list(res[:len(p.out_shape)]))
        res = res[len(p.out_shape):]
    return out


def _exchange(plans, name):
    c_in = [a for p in plans for a in p.ins]
    c_out = [s for p in plans for s in p.out_shape]
    c_scr = [s for p in plans for s in p.scratch]

    def body(*refs):
        cin, cout, cscr = refs[:len(c_in)], refs[len(c_in):len(c_in) + len(c_out)], refs[len(c_in) + len(c_out):]
        _split_plans(plans, cin, cout, cscr, "start")
        _split_plans(plans, cin, cout, cscr, "finish")

    res = pl.pallas_call(body, name=name, in_specs=[_any()] * len(c_in), out_specs=[_any()] * len(c_out),
                         out_shape=c_out, scratch_shapes=c_scr)(*c_in)
    return _plan_results(plans, res)


def _call(body, *, name, grid, in_specs, out_specs, out_shape, args, semantics, scratch_shapes=(), plans=()):
    n_in, n_out, n_scr = len(in_specs), len(out_specs), len(scratch_shapes)
    c_in = [a for p in plans for a in p.ins]
    c_out = [s for p in plans for s in p.out_shape]
    c_scr = [s for p in plans for s in p.scratch]

    def wrapped(*refs):
        ins, refs = refs[:n_in], refs[n_in:]
        cin, refs = refs[:len(c_in)], refs[len(c_in):]
        outs, refs = refs[:n_out], refs[n_out:]
        cout, refs = refs[:len(c_out)], refs[len(c_out):]
        scr, cscr = refs[:n_scr], refs[n_scr:]
        if plans:
            ids = [pl.program_id(d) for d in range(len(grid))]
            first = functools.reduce(jnp.logical_and, [i == 0 for i in ids])
            last = functools.reduce(jnp.logical_and, [i == g - 1 for i, g in zip(ids, grid)])
            pl.when(first)(lambda: _split_plans(plans, cin, cout, cscr, "start"))
        body(*ins, *outs, *scr)
        if plans:
            pl.when(last)(lambda: _split_plans(plans, cin, cout, cscr, "finish"))

    res = pl.pallas_call(
        wrapped, name=name, grid=grid, in_specs=list(in_specs) + [_any()] * len(c_in),
        out_specs=list(out_specs) + [_any()] * len(c_out), out_shape=list(out_shape) + c_out,
        scratch_shapes=list(scratch_shapes) + c_scr,
        compiler_params=_params(*(("arbitrary",) * len(grid) if plans else semantics)),
    )(*args, *c_in)
    return list(res[:n_out]), _plan_results(plans, res[n_out:])


def _ffn_fwd(h, mods, gn, wg, wu, wd, row0, seq, tgt=None, plans=()):
    T = h.shape[0]
    tm = _token_tile(seq, 256)
    tps = seq // tm
    n_t = T // tm
    with_loss = tgt is not None

    def body(h_ref, m_ref, g_ref, wg_ref, wu_ref, wd_ref, *rest):
        if with_loss:
            tgt_ref, out_ref, y_ref, gb_ref, ub_ref, loss_ref = rest
        else:
            out_ref, y_ref, gb_ref, ub_ref = rest
        hv = h_ref[...]
        sh = m_ref[0, row0:row0 + 1, :]
        sc = m_ref[0, row0 + 1:row0 + 2, :]
        ga = m_ref[0, row0 + 2:row0 + 3, :]
        xn, _, _ = _rms_mod(hv, g_ref[...], sh, sc)
        xb = xn.astype(BF16)
        acc = jnp.zeros((tm, D), F32)
        for c0 in range(0, FF, FC):
            gg = _dot_nt(xb, wg_ref[c0:c0 + FC, :])
            uu = _dot_nt(xb, wu_ref[c0:c0 + FC, :])
            gb_ref[:, c0:c0 + FC] = gg.astype(BF16)
            ub_ref[:, c0:c0 + FC] = uu.astype(BF16)
            a = (gg * jax.nn.sigmoid(gg)) * uu
            acc = acc + _dot(a.astype(BF16), wd_ref[c0:c0 + FC, :])
        y_ref[...] = acc
        hout = hv + (0.5 * ga) * acc
        if with_loss:
            d = hout - tgt_ref[...]
            out_ref[...] = d * (1.0 / D)
            loss_ref[...] = jnp.full((1, 8, 128), 0.5 / D, F32) * jnp.sum(d * d)
        else:
            out_ref[...] = hout

    tok = pl.BlockSpec((tm, D), lambda i: (i, 0))
    tokf = pl.BlockSpec((tm, FF), lambda i: (i, 0))
    in_specs = [tok, pl.BlockSpec((1, 9, D), lambda i: (i // tps, 0, 0)), pl.BlockSpec((1, D), lambda i: (0, 0)),
                _vmem(), _vmem(), _vmem()]
    out_shape = [jax.ShapeDtypeStruct((T, D), F32), jax.ShapeDtypeStruct((T, D), F32),
                 jax.ShapeDtypeStruct((T, FF), BF16), jax.ShapeDtypeStruct((T, FF), BF16)]
    out_specs = [tok, tok, tokf, tokf]
    args = [h, mods, gn, wg, wu, wd]
    if with_loss:
        in_specs.append(tok)
        args.append(tgt)
        out_shape.append(jax.ShapeDtypeStruct((n_t, 8, 128), F32))
        out_specs.append(pl.BlockSpec((1, 8, 128), lambda i: (i, 0, 0)))
    return _call(body, name="ffn_fwd_loss" if with_loss else "ffn_fwd", grid=(n_t,), in_specs=in_specs,
                 out_specs=out_specs, out_shape=out_shape, args=args, semantics=("parallel",), plans=plans)


def _ffn_bwd(h, dhn, y, gb, ub, mods, gn, wg, wu, wd, row0, seq, plans=()):
    T = h.shape[0]
    B = T // seq
    tm = _token_tile(seq, 256)
    tps = seq // tm
    n_t = T // tm

    def body(h_ref, dhn_ref, y_ref, gb_ref, ub_ref, m_ref, g_ref, wg_ref, wu_ref, wd_ref,
             dh_ref, xb_ref, dyb_ref, a_ref, dg_ref, du_ref, mg_ref):
        i = pl.program_id(0)
        hv = h_ref[...]
        dhn = dhn_ref[...]
        sh = m_ref[0, row0:row0 + 1, :]
        sc = m_ref[0, row0 + 1:row0 + 2, :]
        ga = m_ref[0, row0 + 2:row0 + 3, :]
        g = g_ref[...]
        xn, r, inv = _rms_mod(hv, g, sh, sc)
        xb_ref[...] = xn.astype(BF16)
        dyb = ((0.5 * ga) * dhn).astype(BF16)
        dyb_ref[...] = dyb
        dga = 0.5 * jnp.sum(dhn * y_ref[...], axis=0, keepdims=True)
        dxn = jnp.zeros((tm, D), F32)
        for c0 in range(0, FF, FC):
            wgc = wg_ref[c0:c0 + FC, :]
            wuc = wu_ref[c0:c0 + FC, :]
            gg = gb_ref[:, c0:c0 + FC].astype(F32)
            uu = ub_ref[:, c0:c0 + FC].astype(F32)
            sig = jax.nn.sigmoid(gg)
            s = gg * sig
            a_ref[:, c0:c0 + FC] = (s * uu).astype(BF16)
            da = _dot_nt(dyb, wd_ref[c0:c0 + FC, :])
            dub = (da * s).astype(BF16)
            dgb = (da * uu * (sig * (1.0 + gg * (1.0 - sig)))).astype(BF16)
            dg_ref[:, c0:c0 + FC] = dgb
            du_ref[:, c0:c0 + FC] = dub
            dxn = dxn + _dot(dgb, wgc) + _dot(dub, wuc)
        dh, s_dxn, s_dxr = _rms_mod_bwd(dxn, r, inv, g, sc)
        dh_ref[...] = dhn + dh

        @pl.when(i % tps == 0)
        def _():
            mg_ref[...] = jnp.zeros(mg_ref.shape, F32)

        mg_ref[0, 0:1, :] += s_dxn
        mg_ref[0, 1:2, :] += s_dxr
        mg_ref[0, 2:3, :] += dga

    tok = pl.BlockSpec((tm, D), lambda i: (i, 0))
    tokf = pl.BlockSpec((tm, FF), lambda i: (i, 0))
    return _call(
        body, name="ffn_bwd", grid=(n_t,),
        in_specs=[tok, tok, tok, tokf, tokf, pl.BlockSpec((1, 9, D), lambda i: (i // tps, 0, 0)),
                  pl.BlockSpec((1, D), lambda i: (0, 0)), _vmem(), _vmem(), _vmem()],
        out_specs=[tok, tok, tok, tokf, tokf, tokf, pl.BlockSpec((1, 8, D), lambda i: (i // tps, 0, 0))],
        out_shape=[jax.ShapeDtypeStruct((T, D), F32), jax.ShapeDtypeStruct((T, D), BF16),
                   jax.ShapeDtypeStruct((T, D), BF16), jax.ShapeDtypeStruct((T, FF), BF16),
                   jax.ShapeDtypeStruct((T, FF), BF16), jax.ShapeDtypeStruct((T, FF), BF16),
                   jax.ShapeDtypeStruct((B, 8, D), F32)],
        args=(h, dhn, y, gb, ub, mods, gn, wg, wu, wd), semantics=("arbitrary",), plans=plans)


def _wgrad(a, b, name, plans=()):
    T, da = a.shape
    db = b.shape[1]
    bm = {2816: 1408, 3840: 1280}[da]
    bn = db
    tk = min(1024, T)
    nk = T // tk

    def body(a_ref, b_ref, o_ref):
        @pl.when(pl.program_id(2) == 0)
        def _():
            o_ref[...] = jnp.zeros(o_ref.shape, F32)

        o_ref[...] += _dot_tn(a_ref[...], b_ref[...])

    (out,), plan_outs = _call(
        body, name=name, grid=(da // bm, db // bn, nk),
        in_specs=[pl.BlockSpec((tk, bm), lambda i, j, k: (k, i)), pl.BlockSpec((tk, bn), lambda i, j, k: (k, j))],
        out_specs=[pl.BlockSpec((bm, bn), lambda i, j, k: (i, j))], out_shape=[jax.ShapeDtypeStruct((da, db), F32)],
        args=(a, b), semantics=("parallel", "parallel", "arbitrary"), plans=plans)
    return out, plan_outs


def _inproj_fwd(h, mods, gn, w_in, seq):
    T = h.shape[0]
    tm = _token_tile(seq)
    tps = seq // tm

    def body(h_ref, m_ref, g_ref, w_ref, *outs):
        xn, _, _ = _rms_mod(h_ref[...], g_ref[...], m_ref[0, 3:4, :], m_ref[0, 4:5, :])
        xb = xn.astype(BF16)
        for (_, c0, w), o_ref in zip(PIECES, outs):
            o_ref[...] = _dot_nt(xb, w_ref[c0:c0 + w, :])

    return pl.pallas_call(
        body, name="inproj_fwd", grid=(T // tm,),
        in_specs=[pl.BlockSpec((tm, D), lambda i: (i, 0)), pl.BlockSpec((1, 9, D), lambda i: (i // tps, 0, 0)),
                  pl.BlockSpec((1, D), lambda i: (0, 0)), _vmem()],
        out_specs=[pl.BlockSpec((tm, w), lambda i: (i, 0)) for _, _, w in PIECES],
        out_shape=[jax.ShapeDtypeStruct((T, w), F32) for _, _, w in PIECES],
        compiler_params=_params("parallel"),
    )(h, mods, gn, w_in)


def _inproj_bwd(h, dh_res, dpieces, mods, gn, w_in, seq, plans=()):
    T = h.shape[0]
    B = T // seq
    tm = _token_tile(seq, 256)
    tps = seq // tm

    def body(h_ref, dres_ref, *rest):
        dp_refs = rest[:len(PIECES)]
        m_ref, g_ref, w_ref, dh_ref, xb_ref, dpb_ref, mg_ref = rest[len(PIECES):]
        i = pl.program_id(0)
        g = g_ref[...]
        sc = m_ref[0, 4:5, :]
        xn, r, inv = _rms_mod(h_ref[...], g, m_ref[0, 3:4, :], sc)
        xb_ref[...] = xn.astype(BF16)
        dxn = jnp.zeros((tm, D), F32)
        for (_, c0, w), dp_ref in zip(PIECES, dp_refs):
            dpb = dp_ref[...].astype(BF16)
            dpb_ref[:, c0:c0 + w] = dpb
            dxn = dxn + _dot(dpb, w_ref[c0:c0 + w, :])
        dh, s_dxn, s_dxr = _rms_mod_bwd(dxn, r, inv, g, sc)
        dh_ref[...] = dres_ref[...] + dh

        @pl.when(i % tps == 0)
        def _():
            mg_ref[...] = jnp.zeros(mg_ref.shape, F32)

        mg_ref[0, 0:1, :] += s_dxn
        mg_ref[0, 1:2, :] += s_dxr

    tok = pl.BlockSpec((tm, D), lambda i: (i, 0))
    return _call(
        body, name="inproj_bwd", grid=(T // tm,),
        in_specs=[tok, tok] + [pl.BlockSpec((tm, w), lambda i: (i, 0)) for _, _, w in PIECES]
        + [pl.BlockSpec((1, 9, D), lambda i: (i // tps, 0, 0)), pl.BlockSpec((1, D), lambda i: (0, 0)), _vmem()],
        out_specs=[tok, tok, pl.BlockSpec((tm, IN_COLS), lambda i: (i, 0)),
                   pl.BlockSpec((1, 8, D), lambda i: (i // tps, 0, 0))],
        out_shape=[jax.ShapeDtypeStruct((T, D), F32), jax.ShapeDtypeStruct((T, D), BF16),
                   jax.ShapeDtypeStruct((T, IN_COLS), BF16), jax.ShapeDtypeStruct((B, 8, D), F32)],
        args=(h, dh_res, *dpieces, mods, gn, w_in), semantics=("arbitrary",), plans=plans)


def _attn_math(q, kp, kc, vp, vc, gq, gk, sink, first):
    def rms(x, g):
        return (x * lax.rsqrt(jnp.mean(x * x, axis=-1, keepdims=True) + EPS)) * g

    qn = rms(q, gq)
    kn = rms(jnp.concatenate([kp, kc], axis=0), gk)
    v = jnp.concatenate([vp, vc], axis=0)
    s = _dot_nt(qn.astype(BF16), kn.astype(BF16)) * (HD ** -0.5)
    nq = Q_PER_KV * BLK
    row = lax.broadcasted_iota(jnp.int32, (nq, 2 * BLK), 0) & (BLK - 1)
    col = lax.broadcasted_iota(jnp.int32, (nq, 2 * BLK), 1)
    lo = jnp.where(first, BLK, 0)
    valid = (col <= row + BLK) & (col > row) & (col >= lo)
    s = jnp.where(valid, s, NEG)
    m = lax.stop_gradient(jnp.maximum(jnp.max(s, axis=-1, keepdims=True), sink))
    p = jnp.exp(s - m)
    den = jnp.sum(p, axis=-1, keepdims=True) + jnp.exp(sink - m)
    return _dot((p / den).astype(BF16), v.astype(BF16))


def _attn_group(nbs):
    return min(4, nbs)


def _attn_specs(nbs):
    grp = _attn_group(nbs)
    nq = Q_PER_KV * BLK
    qs = pl.BlockSpec((1, grp, nq, HD), lambda h, b: (h, b, 0, 0))
    halo = pl.BlockSpec((1, 1, BLK, HD), lambda h, b: (h, jnp.maximum(b * grp - 1, 0), 0, 0))
    cur = pl.BlockSpec((1, grp, BLK, HD), lambda h, b: (h, b, 0, 0))
    gs = pl.BlockSpec((1, HD), lambda h, b: (0, 0))
    sk = pl.BlockSpec((1, nq, 128), lambda h, b: (h, 0, 0))
    return qs, halo, cur, gs, sk


def _attn_fwd(q_hm, k_hm, v_hm, gq, gk, sink_b, nbs, plans=()):
    nblk = q_hm.shape[1]
    grp = _attn_group(nbs)
    qs, halo, cur, gs, sk = _attn_specs(nbs)

    def body(q_ref, kh_ref, kc_ref, vh_ref, vc_ref, gq_ref, gk_ref, sk_ref, o_ref):
        seq_start = ((pl.program_id(1) * grp) % nbs) == 0
        for g in range(grp):
            kp = kh_ref[0, 0] if g == 0 else kc_ref[0, g - 1]
            vp = vh_ref[0, 0] if g == 0 else vc_ref[0, g - 1]
            o_ref[0, g] = _attn_math(q_ref[0, g], kp, kc_ref[0, g], vp, vc_ref[0, g], gq_ref[...], gk_ref[...],
                                     sk_ref[0, :, 0:1], seq_start if g == 0 else False)

    (out,), plan_outs = _call(
        body, name="attn_fwd", grid=(N_KV, nblk // grp), in_specs=[qs, halo, cur, halo, cur, gs, gs, sk],
        out_specs=[qs], out_shape=[jax.ShapeDtypeStruct(q_hm.shape, F32)],
        args=(q_hm, k_hm, k_hm, v_hm, v_hm, gq, gk, sink_b), semantics=("parallel", "parallel"), plans=plans)
    return out, plan_outs


def _attn_bwd(q_hm, k_hm, v_hm, gq, gk, sink_b, do_hm, nbs, plans=()):
    nblk = q_hm.shape[1]
    grp = _attn_group(nbs)
    qs, halo, cur, gs, sk = _attn_specs(nbs)

    def body(q_ref, kh_ref, kc_ref, vh_ref, vc_ref, gq_ref, gk_ref, sk_ref, do_ref,
             dq_ref, dk_ref, dv_ref, dkh_ref, dvh_ref, dgq_ref, dgk_ref, dsk_ref):
        h = pl.program_id(0)
        b = pl.program_id(1)
        seq_start = ((b * grp) % nbs) == 0

        @pl.when((h == 0) & (b == 0))
        def _():
            for r in (dgq_ref, dgk_ref, dsk_ref):
                r[...] = jnp.zeros(r.shape, F32)

        res = []
        for g in range(grp):
            kp = kh_ref[0, 0] if g == 0 else kc_ref[0, g - 1]
            vp = vh_ref[0, 0] if g == 0 else vc_ref[0, g - 1]
            f = functools.partial(_attn_math, first=seq_start if g == 0 else False)
            _, vjp = jax.vjp(f, q_ref[0, g], kp, kc_ref[0, g], vp, vc_ref[0, g], gq_ref[...], gk_ref[...],
                             sk_ref[0, :, 0:1])
            res.append(vjp(do_ref[0, g]))
        lane = lax.broadcasted_iota(jnp.int32, (8, 128), 1)
        upd = jnp.zeros((8, 128), F32)
        for g, (dq, dkp, dkc, dvp, dvc, dgq, dgk, dsk) in enumerate(res):
            dq_ref[0, g] = dq
            dk_ref[0, g] = dkc + res[g + 1][1] if g + 1 < grp else dkc
            dv_ref[0, g] = dvc + res[g + 1][3] if g + 1 < grp else dvc
            dgq_ref[...] += dgq
            dgk_ref[...] += dgk
            for j in range(Q_PER_KV):
                upd = upd + jnp.where(lane == h * Q_PER_KV + j, jnp.sum(dsk[j * BLK:(j + 1) * BLK, :]), 0.0)
        dkh_ref[0, 0] = res[0][1]
        dvh_ref[0, 0] = res[0][3]
        dsk_ref[...] += upd

    acc = pl.BlockSpec((1, HD), lambda h, b: (0, 0))
    one = pl.BlockSpec((1, 1, BLK, HD), lambda h, b: (h, b, 0, 0))
    halo_shape = jax.ShapeDtypeStruct((N_KV, nblk // grp, BLK, HD), F32)
    return _call(
        body, name="attn_bwd", grid=(N_KV, nblk // grp), in_specs=[qs, halo, cur, halo, cur, gs, gs, sk, qs],
        out_specs=[qs, cur, cur, one, one, acc, acc, pl.BlockSpec((8, 128), lambda h, b: (0, 0))],
        out_shape=[jax.ShapeDtypeStruct(q_hm.shape, F32)] + [jax.ShapeDtypeStruct(k_hm.shape, F32)] * 2
        + [halo_shape] * 2 + [jax.ShapeDtypeStruct((1, HD), F32)] * 2 + [jax.ShapeDtypeStruct((8, 128), F32)],
        args=(q_hm, k_hm, k_hm, v_hm, v_hm, gq, gk, sink_b, do_hm), semantics=("arbitrary", "arbitrary"),
        plans=plans)


def _add_halo(dk, dk_halo):
    n_kv, nblk, _, _ = dk.shape
    steps = dk_halo.shape[1]
    nxt = jnp.concatenate([dk_halo[:, 1:], jnp.zeros_like(dk_halo[:, :1])], axis=1)[:, :, None]
    dk = dk.reshape(n_kv, steps, nblk // steps, BLK, HD)
    return jnp.concatenate([dk[:, :, :-1], dk[:, :, -1:] + nxt], axis=2).reshape(n_kv, nblk, BLK, HD)


def _to_heads(q, k, v):
    nblk = q.shape[0] // BLK
    q_hm = q.reshape(nblk, BLK, N_KV, Q_PER_KV, HD).transpose(2, 0, 3, 1, 4).reshape(N_KV, nblk, Q_PER_KV * BLK, HD)
    k_hm = k.reshape(nblk, BLK, N_KV, HD).transpose(2, 0, 1, 3)
    v_hm = v.reshape(nblk, BLK, N_KV, HD).transpose(2, 0, 1, 3)
    return q_hm, k_hm, v_hm


def _q_from_heads(q_hm):
    nblk = q_hm.shape[1]
    return q_hm.reshape(N_KV, nblk, Q_PER_KV, BLK, HD).transpose(1, 3, 0, 2, 4).reshape(nblk * BLK, N_KV * Q_PER_KV * HD)


def _kv_from_heads(k_hm):
    nblk = k_hm.shape[1]
    return k_hm.transpose(1, 2, 0, 3).reshape(nblk * BLK, N_KV * HD)


def _sgu_norm(av, g_ln, b_ln):
    t, th = _gelu(av)
    mu = jnp.mean(t, axis=-1, keepdims=True)
    tc = t - mu
    rstd = lax.rsqrt(jnp.mean(tc * tc, axis=-1, keepdims=True) + EPS)
    vhat = tc * rstd
    return vhat * g_ln + b_ln, vhat, rstd, th


def _masked_ws(ws_ref):
    tril = lax.broadcasted_iota(jnp.int32, (BLK, BLK), 0) >= lax.broadcasted_iota(jnp.int32, (BLK, BLK), 1)
    return [jnp.where(tril, ws_ref[g], 0.0).astype(BF16) for g in range(N_GRP)]


def _mix_fwd(au, av, gta, gtb, ob, h, mods, g_ln, b_ln, ws, bsb, wa, wb, wout, seq):
    T = h.shape[0]
    tm = _token_tile(seq)
    tps = seq // tm

    def body(au_ref, av_ref, gta_ref, gtb_ref, ob_ref, h_ref, m_ref, gl_ref, bl_ref, ws_ref, bs_ref,
             wa_ref, wb_ref, wo_ref, out_ref, vvb_s, z_s):
        u, _ = _gelu(au_ref[...])
        vv, _, _, _ = _sgu_norm(av_ref[...], gl_ref[...], bl_ref[...])
        vvb_s[...] = vv.astype(BF16)
        wsm = _masked_ws(ws_ref)
        for c in range(tm // BLK):
            rows = slice(c * BLK, (c + 1) * BLK)
            for g in range(N_GRP):
                cols = slice(g * BLK, (g + 1) * BLK)
                z_s[rows, cols] = _dot(wsm[g], vvb_s[rows, cols]) + bs_ref[g]
        ya = _dot_nt((u * z_s[...]).astype(BF16), wa_ref[...])
        yb = _dot_nt(ob_ref[...].astype(BF16), wb_ref[...])
        merged = jax.nn.sigmoid(gta_ref[...]) * ya + jax.nn.sigmoid(gtb_ref[...]) * yb
        out_ref[...] = h_ref[...] + m_ref[0, 5:6, :] * _dot(merged.astype(BF16), wo_ref[...])

    def tok(w):
        return pl.BlockSpec((tm, w), lambda i: (i, 0))

    def full(shape):
        return pl.BlockSpec(shape, lambda i: (0,) * len(shape))

    return pl.pallas_call(
        body, name="mix_fwd", grid=(T // tm,),
        in_specs=[tok(D_A), tok(D_A), tok(D), tok(D), tok(D_B), tok(D),
                  pl.BlockSpec((1, 9, D), lambda i: (i // tps, 0, 0)), full((1, D_A)), full((1, D_A)),
                  full((N_GRP, BLK, BLK)), full((N_GRP, BLK, BLK)), _vmem(), _vmem(), _vmem()],
        out_specs=tok(D), out_shape=jax.ShapeDtypeStruct((T, D), F32),
        scratch_shapes=[pltpu.VMEM((tm, D_A), BF16), pltpu.VMEM((tm, D_A), F32)],
        compiler_params=_params("parallel"),
    )(au, av, gta, gtb, ob, h, mods, g_ln, b_ln, ws, bsb, wa, wb, wout)


def _mix_bwd(au, av, gta, gtb, ob, dh, mods, g_ln, b_ln, ws, bsb, wa, wb, wout, seq, plans=()):
    T = dh.shape[0]
    B = T // seq
    tm = _token_tile(seq, 256)
    tps = seq // tm

    def body(au_ref, av_ref, gta_ref, gtb_ref, ob_ref, dh_ref, m_ref, gl_ref, bl_ref, ws_ref, bs_ref,
             wa_ref, wb_ref, wo_ref,
             dau_ref, dav_ref, dgta_ref, dgtb_ref, dob_ref, dwo_ref, dwa_ref, dwb_ref, dws_ref, dbs_ref, dln_ref,
             mg_ref, vvb_s, z_s, dz_s, dzb_s, dvv_s):
        i = pl.program_id(0)

        @pl.when(i == 0)
        def _():
            for r in (dwo_ref, dwa_ref, dwb_ref, dws_ref, dbs_ref, dln_ref):
                r[...] = jnp.zeros(r.shape, F32)

        @pl.when(i % tps == 0)
        def _():
            mg_ref[...] = jnp.zeros(mg_ref.shape, F32)

        auv = au_ref[...]
        avv = av_ref[...]
        u, thu = _gelu(auv)
        g_ln = gl_ref[...]
        vv, vhat, rstd, thv = _sgu_norm(avv, g_ln, bl_ref[...])
        vvb_s[...] = vv.astype(BF16)
        wsm = _masked_ws(ws_ref)
        for c in range(tm // BLK):
            rows = slice(c * BLK, (c + 1) * BLK)
            for g in range(N_GRP):
                cols = slice(g * BLK, (g + 1) * BLK)
                z_s[rows, cols] = _dot(wsm[g], vvb_s[rows, cols]) + bs_ref[g]
        z = z_s[...]
        yab = (u * z).astype(BF16)
        obb = ob_ref[...].astype(BF16)
        ya = _dot_nt(yab, wa_ref[...])
        yb = _dot_nt(obb, wb_ref[...])
        sa = jax.nn.sigmoid(gta_ref[...])
        sb = jax.nn.sigmoid(gtb_ref[...])
        mb = (sa * ya + sb * yb).astype(BF16)
        dhv = dh_ref[...]
        mg_ref[0, 0:1, :] += jnp.sum(dhv * _dot(mb, wo_ref[...]), axis=0, keepdims=True)
        dmob = (m_ref[0, 5:6, :] * dhv).astype(BF16)
        dwo_ref[...] += _dot_tn(mb, dmob)
        dmerged = _dot_nt(dmob, wo_ref[...])
        dya = dmerged * sa
        dyb = dmerged * sb
        dgta_ref[...] = dya * ya * (1.0 - sa)
        dgtb_ref[...] = dyb * yb * (1.0 - sb)
        dyab = dya.astype(BF16)
        dybb = dyb.astype(BF16)
        dwa_ref[...] += _dot_tn(dyab, yab)
        dwb_ref[...] += _dot_tn(dybb, obb)
        dob_ref[...] = _dot(dybb, wb_ref[...])
        dyap = _dot(dyab, wa_ref[...])
        dau_ref[...] = (dyap * z) * _gelu_grad(auv, thu)
        dz = dyap * u
        dz_s[...] = dz
        dzb_s[...] = dz.astype(BF16)
        for c in range(tm // BLK):
            rows = slice(c * BLK, (c + 1) * BLK)
            for g in range(N_GRP):
                cols = slice(g * BLK, (g + 1) * BLK)
                dzb = dzb_s[rows, cols]
                dvv_s[rows, cols] = _dot_tn(wsm[g], dzb)
                dws_ref[g] += _dot_nt(dzb, vvb_s[rows, cols])
                dbs_ref[g] += dz_s[rows, cols]
        dvv = dvv_s[...]
        dln_ref[0:1, :] += jnp.sum(dvv * vhat, axis=0, keepdims=True)
        dln_ref[1:2, :] += jnp.sum(dvv, axis=0, keepdims=True)
        dvh = dvv * g_ln
        dt = rstd * (dvh - jnp.mean(dvh, axis=-1, keepdims=True)
                     - vhat * jnp.mean(dvh * vhat, axis=-1, keepdims=True))
        dav_ref[...] = dt * _gelu_grad(avv, thv)

    def tok(w):
        return pl.BlockSpec((tm, w), lambda i: (i, 0))

    def full(shape):
        return pl.BlockSpec(shape, lambda i: (0,) * len(shape))

    return _call(
        body, name="mix_bwd", grid=(T // tm,),
        in_specs=[tok(D_A), tok(D_A), tok(D), tok(D), tok(D_B), tok(D),
                  pl.BlockSpec((1, 9, D), lambda i: (i // tps, 0, 0)), full((1, D_A)), full((1, D_A)),
                  full((N_GRP, BLK, BLK)), full((N_GRP, BLK, BLK)), _vmem(), _vmem(), _vmem()],
        out_specs=[tok(D_A), tok(D_A), tok(D), tok(D), tok(D_B), full((D, D)), full((D, D_A)), full((D, D_B)),
                   full((N_GRP, BLK, BLK)), full((N_GRP, BLK, BLK)), full((8, D_A)),
                   pl.BlockSpec((1, 8, D), lambda i: (i // tps, 0, 0))],
        out_shape=[jax.ShapeDtypeStruct((T, D_A), F32), jax.ShapeDtypeStruct((T, D_A), F32),
                   jax.ShapeDtypeStruct((T, D), F32), jax.ShapeDtypeStruct((T, D), F32),
                   jax.ShapeDtypeStruct((T, D_B), F32), jax.ShapeDtypeStruct((D, D), F32),
                   jax.ShapeDtypeStruct((D, D_A), F32), jax.ShapeDtypeStruct((D, D_B), F32),
                   jax.ShapeDtypeStruct((N_GRP, BLK, BLK), F32), jax.ShapeDtypeStruct((N_GRP, BLK, BLK), F32),
                   jax.ShapeDtypeStruct((8, D_A), F32), jax.ShapeDtypeStruct((B, 8, D), F32)],
        scratch_shapes=[pltpu.VMEM((tm, D_A), BF16), pltpu.VMEM((tm, D_A), F32), pltpu.VMEM((tm, D_A), F32),
                        pltpu.VMEM((tm, D_A), BF16), pltpu.VMEM((tm, D_A), F32)],
        args=(au, av, gta, gtb, ob, dh, mods, g_ln, b_ln, ws, bsb, wa, wb, wout), semantics=("arbitrary",),
        plans=plans)


def _ada_fwd(c_all, w_ada):
    def body(c_ref, w_ref, o_ref):
        cv = c_ref[...]
        o_ref[...] = _dot((cv * jax.nn.sigmoid(cv)).astype(BF16), w_ref[...].astype(BF16))

    return pl.pallas_call(body, name="ada_fwd", in_specs=[_vmem(), _vmem()], out_specs=_vmem(),
                          out_shape=jax.ShapeDtypeStruct((c_all.shape[0], w_ada.shape[1]), F32))(c_all, w_ada)


def _ada_update(c_all, dm_cols, w, m, v, plans=()):
    n, cols = c_all.shape[0], w.shape[1]

    def body(c_ref, dm_ref, w_ref, m_ref, v_ref, g_ref, d_ref, nm_ref, nv_ref):
        cv = c_ref[...]
        g = _dot_tn((cv * jax.nn.sigmoid(cv)).astype(BF16), dm_ref[...].astype(BF16))
        g_ref[...] = g
        d_ref[...], nm_ref[...], nv_ref[...] = _adamw(w_ref[...], g, m_ref[...], v_ref[...])

    col = pl.BlockSpec((D, 128), lambda j: (0, j))
    return _call(
        body, name="ada_update", grid=(cols // 128,),
        in_specs=[pl.BlockSpec((n, D), lambda j: (0, 0)), pl.BlockSpec((n, 128), lambda j: (0, j)), col, col, col],
        out_specs=[col] * 4, out_shape=[jax.ShapeDtypeStruct(w.shape, F32)] * 4, args=(c_all, dm_cols, w, m, v),
        semantics=("parallel",), plans=plans)


def _mod_finish(mg1, mg2, mg2g, mg3, mods, g1, g2, g3):
    B = mods.shape[0]

    def body(mg1_ref, mg2_ref, mg2g_ref, mg3_ref, m_ref, g1_ref, g2_ref, g3_ref, dm_ref, dgn_ref):
        dgn_ref[...] = jnp.zeros(dgn_ref.shape, F32)
        for k, (mg, g_ref) in enumerate(((mg1_ref, g1_ref), (mg2_ref, g2_ref), (mg3_ref, g3_ref))):
            for b in range(B):
                s_dxr = mg[b, 1:2, :]
                dm_ref[b, 3 * k:3 * k + 1, :] = mg[b, 0:1, :]
                dm_ref[b, 3 * k + 1:3 * k + 2, :] = g_ref[...] * s_dxr
                dm_ref[b, 3 * k + 2:3 * k + 3, :] = mg2g_ref[b, 0:1, :] if k == 1 else mg[b, 2:3, :]
                dgn_ref[k:k + 1, :] += (1.0 + m_ref[b, 3 * k + 1:3 * k + 2, :]) * s_dxr

    return pl.pallas_call(body, name="mod_finish", in_specs=[_vmem()] * 8, out_specs=[_vmem()] * 2,
                          out_shape=[jax.ShapeDtypeStruct((B, 9, D), F32), jax.ShapeDtypeStruct((8, D), F32)],
                          )(mg1, mg2, mg2g, mg3, mods, g1, g2, g3)


def _small_update(gathered, params, ms, vs):
    n = len(SMALL)
    B = gathered[0].shape[1]

    def body(*refs):
        gdm, ggn, gln, gws, gbs, ggq, ggk, gsk = refs[:8]
        w = dict(zip(SMALL, refs[8:8 + n]))
        m = dict(zip(SMALL, refs[8 + n:8 + 2 * n]))
        v = dict(zip(SMALL, refs[8 + 2 * n:8 + 3 * n]))
        outs = refs[8 + 3 * n:]
        out = {name: outs[4 * k:4 * k + 4] for k, name in enumerate(SMALL)}

        def total(ref, idx):
            acc = ref[(0,) + idx]
            for dev in range(1, N_DEV):
                acc = acc + ref[(dev,) + idx]
            return acc

        def finish(name, g, idx=(Ellipsis,)):
            d, nm, nv = _adamw(w[name][idx], g, m[name][idx], v[name][idx])
            for ref, val in zip(out[name], (g, d, nm, nv)):
                ref[idx] = val

        g_bada = total(gdm, (slice(0, 1),))
        for b in range(1, B):
            g_bada = g_bada + total(gdm, (slice(b, b + 1),))
        finish("b_ada", g_bada)
        finish("g_norm1", total(ggn, (slice(0, 1),)))
        finish("g_norm2", total(ggn, (slice(1, 2),)))
        finish("g_norm3", total(ggn, (slice(2, 3),)))
        finish("g_sgu_ln", total(gln, (slice(0, 1),)))
        finish("b_sgu_ln", total(gln, (slice(1, 2),)))
        tril = lax.broadcasted_iota(jnp.int32, (BLK, BLK), 0) >= lax.broadcasted_iota(jnp.int32, (BLK, BLK), 1)
        for g in range(N_GRP):
            finish("w_spatial", jnp.where(tril, total(gws, (g,)), 0.0), (0, g))
            finish("b_spatial", jnp.sum(total(gbs, (g,)).T, axis=0, keepdims=True), (0, slice(g, g + 1)))
        finish("g_q", total(ggq, ()))
        finish("g_k", total(ggk, ()))
        finish("attn_sinks", total(gsk, (slice(0, 1), slice(0, N_KV * Q_PER_KV))))

    args = list(gathered) + [params[k] for k in SMALL] + [ms[k] for k in SMALL] + [vs[k] for k in SMALL]
    out_shape = []
    for k in SMALL:
        out_shape += [jax.ShapeDtypeStruct(params[k].shape, F32)] * 4
    res = pl.pallas_call(body, name="small_update", in_specs=[_vmem()] * len(args),
                         out_specs=[_vmem()] * len(out_shape), out_shape=out_shape)(*args)
    return {k: res[4 * i:4 * i + 4] for i, k in enumerate(SMALL)}


def _rs_add(p, r, core, name):
    _, rows, width = p.shape

    def body(c_ref, p_ref, r_ref, o_ref):
        o_ref[...] = (p_ref[...] + r_ref[...]).astype(BF16)

    return pl.pallas_call(
        body, name="rs_add_" + name, out_shape=jax.ShapeDtypeStruct((4, rows, width), BF16),
        grid_spec=pltpu.PrefetchScalarGridSpec(
            num_scalar_prefetch=1, grid=(4,),
            in_specs=[pl.BlockSpec((1, rows, width), lambda k, c_ref: (2 * k + c_ref[0], 0, 0)),
                      pl.BlockSpec((1, rows, width), lambda k, c_ref: (k, 0, 0))],
            out_specs=pl.BlockSpec((1, rows, width), lambda k, c_ref: (k, 0, 0))),
        compiler_params=_params("parallel"),
    )(core, p, r)


def _rs_sum(r, name):
    def body(r_ref, g_ref):
        g = r_ref[0].astype(F32)
        for k in range(1, 4):
            g = g + r_ref[k].astype(F32)
        g_ref[...] = g

    return pl.pallas_call(body, name="rs_sum_" + name, in_specs=[_vmem()], out_specs=_vmem(),
                          out_shape=jax.ShapeDtypeStruct(r.shape[1:], F32))(r)


def _adam_big(w, m, v, name, g=None, r=None):
    from_parts = r is not None

    def body(g_ref, w_ref, m_ref, v_ref, go_ref, d_ref, nm_ref, nv_ref):
        if from_parts:
            g = g_ref[0].astype(F32)
            for k in range(1, 4):
                g = g + g_ref[k].astype(F32)
        else:
            g = g_ref[...]
        go_ref[...] = g
        d_ref[...], nm_ref[...], nv_ref[...] = _adamw(w_ref[...], g, m_ref[...], v_ref[...])

    return pl.pallas_call(body, name="adam_" + name, in_specs=[_vmem()] * 4, out_specs=[_vmem()] * 4,
                          out_shape=[jax.ShapeDtypeStruct(w.shape, F32)] * 4,
                          compiler_params=pltpu.CompilerParams(vmem_limit_bytes=VMEM_LIMIT),
                          )(r if from_parts else g, w, m, v)


def kernel(x, c, w_ada, b_ada, g_norm1, ffn1_w_gate, ffn1_w_up, ffn1_w_down, g_norm2, w_in, g_sgu_ln, b_sgu_ln, w_spatial, b_spatial, g_q, g_k, attn_sinks, w_branch_a, w_branch_b, w_out, g_norm3, ffn2_w_gate, ffn2_w_up, ffn2_w_down, loss_target, m_w_ada, m_b_ada, m_g_norm1, m_ffn1_w_gate, m_ffn1_w_up, m_ffn1_w_down, m_g_norm2, m_w_in, m_g_sgu_ln, m_b_sgu_ln, m_w_spatial, m_b_spatial, m_g_q, m_g_k, m_attn_sinks, m_w_branch_a, m_w_branch_b, m_w_out, m_g_norm3, m_ffn2_w_gate, m_ffn2_w_up, m_ffn2_w_down, v_w_ada, v_b_ada, v_g_norm1, v_ffn1_w_gate, v_ffn1_w_up, v_ffn1_w_down, v_g_norm2, v_w_in, v_g_sgu_ln, v_b_sgu_ln, v_w_spatial, v_b_spatial, v_g_q, v_g_k, v_attn_sinks, v_w_branch_a, v_w_branch_b, v_w_out, v_g_norm3, v_ffn2_w_gate, v_ffn2_w_up, v_ffn2_w_down):
    names = ("w_ada", "b_ada", "g_norm1", "ffn1_w_gate", "ffn1_w_up", "ffn1_w_down", "g_norm2", "w_in", "g_sgu_ln",
             "b_sgu_ln", "w_spatial", "b_spatial", "g_q", "g_k", "attn_sinks", "w_branch_a", "w_branch_b", "w_out",
             "g_norm3", "ffn2_w_gate", "ffn2_w_up", "ffn2_w_down")
    w = dict(zip(names, (w_ada, b_ada, g_norm1, ffn1_w_gate, ffn1_w_up, ffn1_w_down, g_norm2, w_in, g_sgu_ln,
                         b_sgu_ln, w_spatial, b_spatial, g_q, g_k, attn_sinks, w_branch_a, w_branch_b, w_out, g_norm3,
                         ffn2_w_gate, ffn2_w_up, ffn2_w_down)))
    m = dict(zip(names, (m_w_ada, m_b_ada, m_g_norm1, m_ffn1_w_gate, m_ffn1_w_up, m_ffn1_w_down, m_g_norm2, m_w_in,
                         m_g_sgu_ln, m_b_sgu_ln, m_w_spatial, m_b_spatial, m_g_q, m_g_k, m_attn_sinks, m_w_branch_a,
                         m_w_branch_b, m_w_out, m_g_norm3, m_ffn2_w_gate, m_ffn2_w_up, m_ffn2_w_down)))
    v = dict(zip(names, (v_w_ada, v_b_ada, v_g_norm1, v_ffn1_w_gate, v_ffn1_w_up, v_ffn1_w_down, v_g_norm2, v_w_in,
                         v_g_sgu_ln, v_b_sgu_ln, v_w_spatial, v_b_spatial, v_g_q, v_g_k, v_attn_sinks, v_w_branch_a,
                         v_w_branch_b, v_w_out, v_g_norm3, v_ffn2_w_gate, v_ffn2_w_up, v_ffn2_w_down)))
    B, seq, _ = x.shape
    T = B * seq
    nbs = seq // BLK
    xi, yi, ci = _place()
    me = 4 * xi + 2 * yi + ci
    core = jnp.reshape(ci, (1,)).astype(jnp.int32)
    layout = {name: (tform, n, width) for name, tform, n, width in BIG}
    shard = {name: w[name][0].astype(BF16).T if tform else w[name][0].astype(BF16) for name, tform, _, _ in BIG}
    wts, parts, big_out = {}, {}, {}

    def gather_plan(group):
        return _Gather([shard[k] for k in group])

    def take(group, gathered):
        for k, g in zip(group, gathered):
            wts[k] = g.reshape(N_DEV * layout[k][1], layout[k][2])

    def blocks(name, grad):
        return grad.reshape(N_DEV, layout[name][1], layout[name][2])

    def to_sibling(names, grads):
        return _RsSibling([blocks(k, g) for k, g in zip(names, grads)])

    def add(names, grads, from_sibling):
        for k, g, r in zip(names, grads, from_sibling):
            parts[k] = _rs_add(blocks(k, g), r, core, k)

    def to_chips(names):
        return _RsChips([parts[k] for k in names])

    def update(names, from_chips):
        for k, r in zip(names, from_chips):
            if layout[k][0]:
                res = _adam_big(w[k][0], m[k][0], v[k][0], k, g=_rs_sum(r, k).T)
            else:
                res = _adam_big(w[k][0], m[k][0], v[k][0], k, r=r)
            big_out[k] = [o[None] for o in res]

    g1, g2, g3, g_ln, b_ln, gq, gk = g_norm1, g_norm2, g_norm3, g_sgu_ln, b_sgu_ln, g_q, g_k
    ws = w_spatial[0]
    bsb = jnp.broadcast_to(b_spatial[0][:, :, None], (N_GRP, BLK, BLK))
    sink_b = jnp.broadcast_to(
        jnp.repeat(attn_sinks[0].reshape(N_KV, Q_PER_KV), BLK, axis=1)[:, :, None], (N_KV, Q_PER_KV * BLK, 128))
    xf = x.reshape(T, D)
    tgt = loss_target.reshape(T, D)

    ((c_all,),) = _exchange([_Gather([c])], "gather_c")
    c_all = c_all.reshape(N_DEV * B, D)
    ffn1 = ("ffn1_w_gate", "ffn1_w_up", "ffn1_w_down")
    ffn2 = ("ffn2_w_gate", "ffn2_w_up", "ffn2_w_down")
    ((mods_cols, *gathered),) = _exchange(
        [_Gather([_ada_fwd(c_all, w_ada[0])] + [shard[k] for k in ffn1])], "gather_first")
    take(ffn1, gathered)
    mine = lax.dynamic_slice_in_dim(mods_cols, B * me, B, axis=1)
    mods = (mine.transpose(1, 0, 2).reshape(B, 9 * D) + b_ada).reshape(B, 9, D)

    group = ("w_in", "w_branch_a", "w_branch_b", "w_out", "ffn2_w_gate")
    (h1, y1, gb1, ub1), (gathered,) = _ffn_fwd(xf, mods, g1, *[wts[k] for k in ffn1], 0, seq,
                                               plans=[gather_plan(group)])
    take(group, gathered)
    au, av, q_tok, k_tok, v_tok, gta, gtb = _inproj_fwd(h1, mods, g2, wts["w_in"], seq)
    q_hm, k_hm, v_hm = _to_heads(q_tok, k_tok, v_tok)
    group = ("ffn2_w_up", "ffn2_w_down")
    o_hm, (gathered,) = _attn_fwd(q_hm, k_hm, v_hm, gq, gk, sink_b, nbs, plans=[gather_plan(group)])
    take(group, gathered)
    ob = _q_from_heads(o_hm)
    mixw = (wts["w_branch_a"], wts["w_branch_b"], wts["w_out"])
    h2 = _mix_fwd(au, av, gta, gtb, ob, h1, mods, g_ln, b_ln, ws, bsb, *mixw, seq)
    (dh3, y3, gb3, ub3, loss_parts), _ = _ffn_fwd(h2, mods, g3, *[wts[k] for k in ffn2], 6, seq, tgt=tgt)
    loss = lax.psum(jnp.sum(loss_parts[:, 0, 0]), AXES)

    (dh2, xb, dyb, a, dg, du, mg3), _ = _ffn_bwd(h2, dh3, y3, gb3, ub3, mods, g3, *[wts[k] for k in ffn2], 6, seq)
    d_gate, _ = _wgrad(dg, xb, "wgrad_ffn2_gate")
    d_up, (r,) = _wgrad(du, xb, "wgrad_ffn2_up", plans=[to_sibling(ffn2[:1], [d_gate])])
    add(ffn2[:1], [d_gate], r)
    d_down, (r,) = _wgrad(a, dyb, "wgrad_ffn2_down", plans=[to_sibling(ffn2[1:2], [d_up])])
    add(ffn2[1:2], [d_up], r)

    (dau, dav, dgta, dgtb, dob, d_out, d_a, d_b, dws, dbs, dln, mg2g), (r, from_chips) = _mix_bwd(
        au, av, gta, gtb, ob, dh2, mods, g_ln, b_ln, ws, bsb, *mixw, seq,
        plans=[to_sibling(ffn2[2:], [d_down]), to_chips(ffn2[:2])])
    add(ffn2[2:], [d_down], r)
    update(ffn2[:2], from_chips)
    do_hm, _, _ = _to_heads(dob, k_tok, v_tok)
    mixers = ("w_out", "w_branch_a", "w_branch_b")
    (dq_hm, dk_hm, dv_hm, dk_halo, dv_halo, dgq, dgk, dsk), (from_chips, r) = _attn_bwd(
        q_hm, k_hm, v_hm, gq, gk, sink_b, do_hm, nbs,
        plans=[to_chips(ffn2[2:]), to_sibling(mixers, [d_out, d_a, d_b])])
    update(ffn2[2:], from_chips)
    add(mixers, [d_out, d_a, d_b], r)


    dq = _q_from_heads(dq_hm)
    dk = _kv_from_heads(_add_halo(dk_hm, dk_halo))
    dv = _kv_from_heads(_add_halo(dv_hm, dv_halo))
    (dh1, xb2, dpb, mg2), (from_chips,) = _inproj_bwd(
        h1, dh2, (dau, dav, dq, dk, dv, dgta, dgtb), mods, g2, wts["w_in"], seq, plans=[to_chips(mixers)])
    update(mixers, from_chips)
    d_in, _ = _wgrad(dpb, xb2, "wgrad_w_in")

    (dx, xb, dyb, a, dg, du, mg1), (r,) = _ffn_bwd(
        xf, dh1, y1, gb1, ub1, mods, g1, *[wts[k] for k in ffn1], 0, seq, plans=[to_sibling(("w_in",), [d_in])])
    add(("w_in",), [d_in], r)
    dmods, dgn = _mod_finish(mg1, mg2, mg2g, mg3, mods, g1, g2, g3)
    small_parts = [dmods.reshape(B, 9 * D), dgn, dln, dws, dbs, dgq, dgk, dsk]
    d_gate, (from_chips, gathered) = _wgrad(dg, xb, "wgrad_ffn1_gate",
                                            plans=[to_chips(("w_in",)), _Gather(small_parts)])
    update(("w_in",), from_chips)
    d_up, (r,) = _wgrad(du, xb, "wgrad_ffn1_up", plans=[to_sibling(ffn1[:1], [d_gate])])
    add(ffn1[:1], [d_gate], r)
    d_down, (r, from_chips) = _wgrad(a, dyb, "wgrad_ffn1_down",
                                     plans=[to_sibling(ffn1[1:2], [d_up]), to_chips(ffn1[:1])])
    add(ffn1[1:2], [d_up], r)
    update(ffn1[:1], from_chips)
    small_out = _small_update(gathered, w, m, v)
    dm_cols = lax.dynamic_slice_in_dim(gathered[0].reshape(N_DEV * B, 9 * D), (9 * D // N_DEV) * me,
                                       9 * D // N_DEV, axis=1)
    ada_out, (r,) = _ada_update(c_all, dm_cols, w_ada[0], m_w_ada[0], v_w_ada[0],
                                plans=[to_sibling(ffn1[2:], [d_down])])
    add(ffn1[2:], [d_down], r)
    (from_chips,) = _exchange([to_chips(ffn1[1:])], "rs_last")
    update(ffn1[1:], from_chips)

    def leaf(kind, name):
        if name == "w_ada":
            return ada_out[kind][None]
        if name in SMALL:
            return small_out[name][kind]
        return big_out[name][kind]

    return (loss, dx.reshape(B, seq, D), *[leaf(kind, name) for kind in range(4) for name in names])
```

```python
import functools
import math

import jax
import jax.numpy as jnp
from jax import lax
from jax.experimental import pallas as pl
from jax.experimental.pallas import tpu as pltpu

F32 = jnp.float32
BF16 = jnp.bfloat16
MESH = pl.DeviceIdType.MESH
AXES = ("x", "y", "c")
N_DEV = 8

VMEM_LIMIT = 56 * 1024 * 1024

D = 1024
FF = 2816
FC = 1408
D_A = 512
D_B = 512
HD = 64
N_KV = 2
Q_PER_KV = 4
BLK = 128
N_GRP = 4
IN_COLS = 3840
PIECES = (("au", 0, 512), ("av", 512, 512), ("q", 1024, 512), ("k", 1536, 128), ("v", 1664, 128),
          ("ga", 1792, 1024), ("gb", 2816, 1024))
EPS = 1e-6
NEG = -1e30
GELU_C = math.sqrt(2.0 / math.pi)

ADAM_LR = 0.001
ADAM_B1 = 0.9
ADAM_B2 = 0.999
ADAM_EPS = 1e-08
ADAM_WD = 0.01
ADAM_STEP = 10

NT = (((1,), (1,)), ((), ()))
TN = (((0,), (0,)), ((), ()))

BIG = (("ffn1_w_gate", True, FF // N_DEV, D), ("ffn1_w_up", True, FF // N_DEV, D),
       ("ffn1_w_down", False, FF // N_DEV, D), ("w_in", True, IN_COLS // N_DEV, D),
       ("w_branch_a", True, D // N_DEV, D_A), ("w_branch_b", True, D // N_DEV, D_B), ("w_out", False, D // N_DEV, D),
       ("ffn2_w_gate", True, FF // N_DEV, D), ("ffn2_w_up", True, FF // N_DEV, D),
       ("ffn2_w_down", False, FF // N_DEV, D))
SMALL = ("b_ada", "g_norm1", "g_norm2", "g_sgu_ln", "b_sgu_ln", "w_spatial", "b_spatial", "g_q", "g_k",
         "attn_sinks", "g_norm3")


def _dot(a, b):
    return jnp.dot(a, b, preferred_element_type=F32)


def _dot_nt(a, b):
    return lax.dot_general(a, b, NT, preferred_element_type=F32)


def _dot_tn(a, b):
    return lax.dot_general(a, b, TN, preferred_element_type=F32)


def _vmem():
    return pl.BlockSpec(memory_space=pltpu.VMEM)


def _any():
    return pl.BlockSpec(memory_space=pl.ANY)


def _whole(a):
    return pl.BlockSpec(a.shape, lambda i: (0,) * len(a.shape))


def _rms_mod(h, g, sh, sc):
    inv = lax.rsqrt(jnp.mean(h * h, axis=-1, keepdims=True) + EPS)
    r = h * inv
    return (r * g) * (1.0 + sc) + sh, r, inv


def _rms_mod_bwd(dxn, r, inv, g, sc):
    dr = dxn * (g * (1.0 + sc))
    dh = inv * (dr - r * jnp.mean(dr * r, axis=-1, keepdims=True))
    return dh, jnp.sum(dxn, axis=0, keepdims=True), jnp.sum(dxn * r, axis=0, keepdims=True)


def _gelu(x):
    t = jnp.tanh(GELU_C * (x + 0.044715 * (x * x * x)))
    return 0.5 * x * (1.0 + t), t


def _gelu_grad(x, t):
    return 0.5 * (1.0 + t) + 0.5 * x * (1.0 - t * t) * (GELU_C * (1.0 + 3.0 * 0.044715 * x * x))


def _adamw(w, g, m, v):
    m = ADAM_B1 * m + (1.0 - ADAM_B1) * g
    v = ADAM_B2 * v + (1.0 - ADAM_B2) * (g * g)
    m_hat = m / (1.0 - ADAM_B1 ** ADAM_STEP)
    v_hat = v / (1.0 - ADAM_B2 ** ADAM_STEP)
    delta = -ADAM_LR * (m_hat / (jnp.sqrt(v_hat) + ADAM_EPS) + ADAM_WD * w)
    return delta, m, v


def _token_tile(seq, cap=512):
    return min(cap, seq)


def _params(*semantics):
    return pltpu.CompilerParams(dimension_semantics=semantics, vmem_limit_bytes=VMEM_LIMIT)


def _place():
    return lax.axis_index("x"), lax.axis_index("y"), lax.axis_index("c")


class _Gather:
    def __init__(self, arrays):
        n = len(arrays)
        self.ins = list(arrays)
        self.out_shape = [jax.ShapeDtypeStruct((N_DEV,) + a.shape, a.dtype) for a in arrays]
        self.scratch = [pltpu.SemaphoreType.DMA((n, 7)), pltpu.SemaphoreType.DMA((n, 7)),
                        pltpu.SemaphoreType.DMA((n,))]

    def _copies(self, ins, outs, sems):
        send_sems, recv_sems, local_sems = sems
        n = len(ins)
        x, y, c = _place()
        me, sibling = (x, y, c), (x, y, 1 - c)
        chips = [(1 - x, y), (x, 1 - y), (1 - x, 1 - y)]

        def slot(a, px, py, pc):
            return outs[a].at[4 * px + 2 * py + pc]

        def copy(a, k, block, to, src=None):
            return pltpu.make_async_remote_copy(
                src_ref=slot(a, *block) if src is None else src, dst_ref=slot(a, *block),
                send_sem=send_sems.at[a, k], recv_sem=recv_sems.at[a, k], device_id=to, device_id_type=MESH)

        mine = [pltpu.make_async_copy(ins[a], slot(a, *me), local_sems.at[a]) for a in range(n)]
        first = [copy(a, 0, me, sibling, src=ins[a]) for a in range(n)]
        first += [copy(a, 1 + j, me, (*chip, c), src=ins[a]) for a in range(n) for j, chip in enumerate(chips)]
        landed = [[copy(a, 1 + j, (*chip, c), me) for a in range(n)] for j, chip in enumerate(chips)]
        passed = [[copy(a, 4 + j, (*chip, c), sibling) for a in range(n)] for j, chip in enumerate(chips)]
        from_sibling = [copy(a, 0, sibling, me) for a in range(n)]
        from_sibling += [copy(a, 4 + j, (*chip, 1 - c), me) for a in range(n) for j, chip in enumerate(chips)]
        return mine, first, landed, passed, from_sibling

    def start(self, ins, outs, sems):
        mine, first, _, _, _ = self._copies(ins, outs, sems)
        for cp in mine + first:
            cp.start()

    def finish(self, ins, outs, sems):
        mine, first, landed, passed, from_sibling = self._copies(ins, outs, sems)
        for arrivals, forwards in zip(landed, passed):
            for arrival, forward in zip(arrivals, forwards):
                arrival.wait_recv()
                forward.start()
        for cp in from_sibling:
            cp.wait_recv()
        for cp in first + [f for fs in passed for f in fs]:
            cp.wait_send()
        for cp in mine:
            cp.wait()


class _RsSibling:
    def __init__(self, ps):
        n = len(ps)
        self.ins = list(ps)
        self.out_shape = [jax.ShapeDtypeStruct((4,) + p.shape[1:], p.dtype) for p in ps]
        self.scratch = [pltpu.SemaphoreType.DMA((n, 4)), pltpu.SemaphoreType.DMA((n, 4))]

    def _copies(self, ins, outs, sems):
        send_sems, recv_sems = sems
        x, y, c = _place()
        return [pltpu.make_async_remote_copy(
            src_ref=ins[a].at[2 * q + (1 - c)], dst_ref=outs[a].at[q], send_sem=send_sems.at[a, q],
            recv_sem=recv_sems.at[a, q], device_id=(x, y, 1 - c), device_id_type=MESH)
            for a in range(len(ins)) for q in range(4)]

    def start(self, ins, outs, sems):
        for cp in self._copies(ins, outs, sems):
            cp.start()

    def finish(self, ins, outs, sems):
        for cp in self._copies(ins, outs, sems):
            cp.wait()


class _RsChips:
    def __init__(self, qs):
        n = len(qs)
        self.ins = list(qs)
        self.out_shape = [jax.ShapeDtypeStruct(q.shape, q.dtype) for q in qs]
        self.scratch = [pltpu.SemaphoreType.DMA((n, 3)), pltpu.SemaphoreType.DMA((n, 3)),
                        pltpu.SemaphoreType.DMA((n,))]

    def _copies(self, ins, outs, sems):
        send_sems, recv_sems, local_sems = sems
        n = len(ins)
        x, y, c = _place()
        my_chip = 2 * x + y
        chips = [(1 - x, y), (x, 1 - y), (1 - x, 1 - y)]

        def copy(a, j, src_slot, dst_slot):
            px, py = chips[j]
            return pltpu.make_async_remote_copy(
                src_ref=ins[a].at[src_slot], dst_ref=outs[a].at[dst_slot], send_sem=send_sems.at[a, j],
                recv_sem=recv_sems.at[a, j], device_id=(px, py, c), device_id_type=MESH)

        own = [pltpu.make_async_copy(ins[a].at[my_chip], outs[a].at[my_chip], local_sems.at[a]) for a in range(n)]
        sends = [copy(a, j, 2 * px + py, my_chip) for a in range(n) for j, (px, py) in enumerate(chips)]
        arrivals = [copy(a, j, my_chip, 2 * px + py) for a in range(n) for j, (px, py) in enumerate(chips)]
        return own, sends, arrivals

    def start(self, ins, outs, sems):
        own, sends, _ = self._copies(ins, outs, sems)
        for cp in own + sends:
            cp.start()

    def finish(self, ins, outs, sems):
        own, sends, arrivals = self._copies(ins, outs, sems)
        for cp in arrivals:
            cp.wait_recv()
        for cp in sends:
            cp.wait_send()
        for cp in own:
            cp.wait()


def _split_plans(plans, refs_in, refs_out, refs_scr, phase):
    i = o = s = 0
    for p in plans:
        ni, no, ns = len(p.ins), len(p.out_shape), len(p.scratch)
        getattr(p, phase)(refs_in[i:i + ni], refs_out[o:o + no], refs_scr[s:s + ns])
        i, o, s = i + ni, o + no, s + ns


def _plan_results(plans, res):
    out = []
    for p in plans:
        out.append(list(res[:len(p.out_shape)]))
        res = res[len(p.out_shape):]
    return out


def _exchange(plans, name):
    c_in = [a for p in plans for a in p.ins]
    c_out = [s for p in plans for s in p.out_shape]
    c_scr = [s for p in plans for s in p.scratch]

    def body(*refs):
        cin, cout, cscr = refs[:len(c_in)], refs[len(c_in):len(c_in) + len(c_out)], refs[len(c_in) + len(c_out):]
        _split_plans(plans, cin, cout, cscr, "start")
        _split_plans(plans, cin, cout, cscr, "finish")

    res = pl.pallas_call(body, name=name, in_specs=[_any()] * len(c_in), out_specs=[_any()] * len(c_out),
                         out_shape=c_out, scratch_shapes=c_scr)(*c_in)
    return _plan_results(plans, res)


def _call(body, *, name, grid, in_specs, out_specs, out_shape, args, semantics, scratch_shapes=(), plans=()):
    n_in, n_out, n_scr = len(in_specs), len(out_specs), len(scratch_shapes)
    c_in = [a for p in plans for a in p.ins]
    c_out = [s for p in plans for s in p.out_shape]
    c_scr = [s for p in plans for s in p.scratch]

    def wrapped(*refs):
        ins, refs = refs[:n_in], refs[n_in:]
        cin, refs = refs[:len(c_in)], refs[len(c_in):]
        outs, refs = refs[:n_out], refs[n_out:]
        cout, refs = refs[:len(c_out)], refs[len(c_out):]
        scr, cscr = refs[:n_scr], refs[n_scr:]
        if plans:
            ids = [pl.program_id(d) for d in range(len(grid))]
            first = functools.reduce(jnp.logical_and, [i == 0 for i in ids])
            last = functools.reduce(jnp.logical_and, [i == g - 1 for i, g in zip(ids, grid)])
            pl.when(first)(lambda: _split_plans(plans, cin, cout, cscr, "start"))
        body(*ins, *outs, *scr)
        if plans:
            pl.when(last)(lambda: _split_plans(plans, cin, cout, cscr, "finish"))

    res = pl.pallas_call(
        wrapped, name=name, grid=grid, in_specs=list(in_specs) + [_any()] * len(c_in),
        out_specs=list(out_specs) + [_any()] * len(c_out), out_shape=list(out_shape) + c_out,
        scratch_shapes=list(scratch_shapes) + c_scr,
        compiler_params=_params(*(("arbitrary",) * len(grid) if plans else semantics)),
    )(*args, *c_in)
    return list(res[:n_out]), _plan_results(plans, res[n_out:])


def _ffn_fwd(h, mods, gn, wg, wu, wd, row0, seq, tgt=None, plans=()):
    T = h.shape[0]
    tm = _token_tile(seq, 256)
    tps = seq // tm
    n_t = T // tm
    with_loss = tgt is not None

    def body(h_ref, m_ref, g_ref, wg_ref, wu_ref, wd_ref, *rest):
        if with_loss:
            tgt_ref, out_ref, y_ref, gb_ref, ub_ref, loss_ref = rest
        else:
            out_ref, y_ref, gb_ref, ub_ref = rest
        hv = h_ref[...]
        sh = m_ref[0, row0:row0 + 1, :]
        sc = m_ref[0, row0 + 1:row0 + 2, :]
        ga = m_ref[0, row0 + 2:row0 + 3, :]
        xn, _, _ = _rms_mod(hv, g_ref[...], sh, sc)
        xb = xn.astype(BF16)
        acc = jnp.zeros((tm, D), F32)
        for c0 in range(0, FF, FC):
            gg = _dot_nt(xb, wg_ref[c0:c0 + FC, :])
            uu = _dot_nt(xb, wu_ref[c0:c0 + FC, :])
            gb_ref[:, c0:c0 + FC] = gg.astype(BF16)
            ub_ref[:, c0:c0 + FC] = uu.astype(BF16)
            a = (gg * jax.nn.sigmoid(gg)) * uu
            acc = acc + _dot(a.astype(BF16), wd_ref[c0:c0 + FC, :])
        y_ref[...] = acc
        hout = hv + (0.5 * ga) * acc
        if with_loss:
            d = hout - tgt_ref[...]
            out_ref[...] = d * (1.0 / D)
            loss_ref[...] = jnp.full((1, 8, 128), 0.5 / D, F32) * jnp.sum(d * d)
        else:
            out_ref[...] = hout

    tok = pl.BlockSpec((tm, D), lambda i: (i, 0))
    tokf = pl.BlockSpec((tm, FF), lambda i: (i, 0))
    in_specs = [tok, pl.BlockSpec((1, 9, D), lambda i: (i // tps, 0, 0)), pl.BlockSpec((1, D), lambda i: (0, 0)),
                _vmem(), _vmem(), _vmem()]
    out_shape = [jax.ShapeDtypeStruct((T, D), F32), jax.ShapeDtypeStruct((T, D), F32),
                 jax.ShapeDtypeStruct((T, FF), BF16), jax.ShapeDtypeStruct((T, FF), BF16)]
    out_specs = [tok, tok, tokf, tokf]
    args = [h, mods, gn, wg, wu, wd]
    if with_loss:
        in_specs.append(tok)
        args.append(tgt)
        out_shape.append(jax.ShapeDtypeStruct((n_t, 8, 128), F32))
        out_specs.append(pl.BlockSpec((1, 8, 128), lambda i: (i, 0, 0)))
    return _call(body, name="ffn_fwd_loss" if with_loss else "ffn_fwd", grid=(n_t,), in_specs=in_specs,
                 out_specs=out_specs, out_shape=out_shape, args=args, semantics=("parallel",), plans=plans)


def _ffn_bwd(h, dhn, y, gb, ub, mods, gn, wg, wu, wd, row0, seq, plans=()):
    T = h.shape[0]
    B = T // seq
    tm = _token_tile(seq, 256)
    tps = seq // tm
    n_t = T // tm

    def body(h_ref, dhn_ref, y_ref, gb_ref, ub_ref, m_ref, g_ref, wg_ref, wu_ref, wd_ref,
             dh_ref, xb_ref, dyb_ref, a_ref, dg_ref, du_ref, mg_ref):
        i = pl.program_id(0)
        hv = h_ref[...]
        dhn = dhn_ref[...]
        sh = m_ref[0, row0:row0 + 1, :]
        sc = m_ref[0, row0 + 1:row0 + 2, :]
        ga = m_ref[0, row0 + 2:row0 + 3, :]
        g = g_ref[...]
        xn, r, inv = _rms_mod(hv, g, sh, sc)
        xb_ref[...] = xn.astype(BF16)
        dyb = ((0.5 * ga) * dhn).astype(BF16)
        dyb_ref[...] = dyb
        dga = 0.5 * jnp.sum(dhn * y_ref[...], axis=0, keepdims=True)
        dxn = jnp.zeros((tm, D), F32)
        for c0 in range(0, FF, FC):
            wgc = wg_ref[c0:c0 + FC, :]
            wuc = wu_ref[c0:c0 + FC, :]
            gg = gb_ref[:, c0:c0 + FC].astype(F32)
            uu = ub_ref[:, c0:c0 + FC].astype(F32)
            sig = jax.nn.sigmoid(gg)
            s = gg * sig
            a_ref[:, c0:c0 + FC] = (s * uu).astype(BF16)
            da = _dot_nt(dyb, wd_ref[c0:c0 + FC, :])
            dub = (da * s).astype(BF16)
            dgb = (da * uu * (sig * (1.0 + gg * (1.0 - sig)))).astype(BF16)
            dg_ref[:, c0:c0 + FC] = dgb
            du_ref[:, c0:c0 + FC] = dub
            dxn = dxn + _dot(dgb, wgc) + _dot(dub, wuc)
        dh, s_dxn, s_dxr = _rms_mod_bwd(dxn, r, inv, g, sc)
        dh_ref[...] = dhn + dh

        @pl.when(i % tps == 0)
        def _():
            mg_ref[...] = jnp.zeros(mg_ref.shape, F32)

        mg_ref[0, 0:1, :] += s_dxn
        mg_ref[0, 1:2, :] += s_dxr
        mg_ref[0, 2:3, :] += dga

    tok = pl.BlockSpec((tm, D), lambda i: (i, 0))
    tokf = pl.BlockSpec((tm, FF), lambda i: (i, 0))
    return _call(
        body, name="ffn_bwd", grid=(n_t,),
        in_specs=[tok, tok, tok, tokf, tokf, pl.BlockSpec((1, 9, D), lambda i: (i // tps, 0, 0)),
                  pl.BlockSpec((1, D), lambda i: (0, 0)), _vmem(), _vmem(), _vmem()],
        out_specs=[tok, tok, tok, tokf, tokf, tokf, pl.BlockSpec((1, 8, D), lambda i: (i // tps, 0, 0))],
        out_shape=[jax.ShapeDtypeStruct((T, D), F32), jax.ShapeDtypeStruct((T, D), BF16),
                   jax.ShapeDtypeStruct((T, D), BF16), jax.ShapeDtypeStruct((T, FF), BF16),
                   jax.ShapeDtypeStruct((T, FF), BF16), jax.ShapeDtypeStruct((T, FF), BF16),
                   jax.ShapeDtypeStruct((B, 8, D), F32)],
        args=(h, dhn, y, gb, ub, mods, gn, wg, wu, wd), semantics=("arbitrary",), plans=plans)


def _wgrad(a, b, name, plans=()):
    T, da = a.shape
    db = b.shape[1]
    bm = {2816: 1408, 3840: 1280}[da]
    bn = db
    tk = min(1024, T)
    nk = T // tk

    def body(a_ref, b_ref, o_ref):
        @pl.when(pl.program_id(2) == 0)
        def _():
            o_ref[...] = jnp.zeros(o_ref.shape, F32)

        o_ref[...] += _dot_tn(a_ref[...], b_ref[...])

    (out,), plan_outs = _call(
        body, name=name, grid=(da // bm, db // bn, nk),
        in_specs=[pl.BlockSpec((tk, bm), lambda i, j, k: (k, i)), pl.BlockSpec((tk, bn), lambda i, j, k: (k, j))],
        out_specs=[pl.BlockSpec((bm, bn), lambda i, j, k: (i, j))], out_shape=[jax.ShapeDtypeStruct((da, db), F32)],
        args=(a, b), semantics=("parallel", "parallel", "arbitrary"), plans=plans)
    return out, plan_outs


def _inproj_fwd(h, mods, gn, w_in, seq):
    T = h.shape[0]
    tm = _token_tile(seq)
    tps = seq // tm

    def body(h_ref, m_ref, g_ref, w_ref, *outs):
        xn, _, _ = _rms_mod(h_ref[...], g_ref[...], m_ref[0, 3:4, :], m_ref[0, 4:5, :])
        xb = xn.astype(BF16)
        for (_, c0, w), o_ref in zip(PIECES, outs):
            o_ref[...] = _dot_nt(xb, w_ref[c0:c0 + w, :])

    return pl.pallas_call(
        body, name="inproj_fwd", grid=(T // tm,),
        in_specs=[pl.BlockSpec((tm, D), lambda i: (i, 0)), pl.BlockSpec((1, 9, D), lambda i: (i // tps, 0, 0)),
                  pl.BlockSpec((1, D), lambda i: (0, 0)), _vmem()],
        out_specs=[pl.BlockSpec((tm, w), lambda i: (i, 0)) for _, _, w in PIECES],
        out_shape=[jax.ShapeDtypeStruct((T, w), F32) for _, _, w in PIECES],
        compiler_params=_params("parallel"),
    )(h, mods, gn, w_in)


def _inproj_bwd(h, dh_res, dpieces, mods, gn, w_in, seq, plans=()):
    T = h.shape[0]
    B = T // seq
    tm = _token_tile(seq, 256)
    tps = seq // tm

    def body(h_ref, dres_ref, *rest):
        dp_refs = rest[:len(PIECES)]
        m_ref, g_ref, w_ref, dh_ref, xb_ref, dpb_ref, mg_ref = rest[len(PIECES):]
        i = pl.program_id(0)
        g = g_ref[...]
        sc = m_ref[0, 4:5, :]
        xn, r, inv = _rms_mod(h_ref[...], g, m_ref[0, 3:4, :], sc)
        xb_ref[...] = xn.astype(BF16)
        dxn = jnp.zeros((tm, D), F32)
        for (_, c0, w), dp_ref in zip(PIECES, dp_refs):
            dpb = dp_ref[...].astype(BF16)
            dpb_ref[:, c0:c0 + w] = dpb
            dxn = dxn + _dot(dpb, w_ref[c0:c0 + w, :])
        dh, s_dxn, s_dxr = _rms_mod_bwd(dxn, r, inv, g, sc)
        dh_ref[...] = dres_ref[...] + dh

        @pl.when(i % tps == 0)
        def _():
            mg_ref[...] = jnp.zeros(mg_ref.shape, F32)

        mg_ref[0, 0:1, :] += s_dxn
        mg_ref[0, 1:2, :] += s_dxr

    tok = pl.BlockSpec((tm, D), lambda i: (i, 0))
    return _call(
        body, name="inproj_bwd", grid=(T // tm,),
        in_specs=[tok, tok] + [pl.BlockSpec((tm, w), lambda i: (i, 0)) for _, _, w in PIECES]
        + [pl.BlockSpec((1, 9, D), lambda i: (i // tps, 0, 0)), pl.BlockSpec((1, D), lambda i: (0, 0)), _vmem()],
        out_specs=[tok, tok, pl.BlockSpec((tm, IN_COLS), lambda i: (i, 0)),
                   pl.BlockSpec((1, 8, D), lambda i: (i // tps, 0, 0))],
        out_shape=[jax.ShapeDtypeStruct((T, D), F32), jax.ShapeDtypeStruct((T, D), BF16),
                   jax.ShapeDtypeStruct((T, IN_COLS), BF16), jax.ShapeDtypeStruct((B, 8, D), F32)],
        args=(h, dh_res, *dpieces, mods, gn, w_in), semantics=("arbitrary",), plans=plans)


def _seg_mean(x):
    i = lax.broadcasted_iota(jnp.int32, (128, 128), 0) >> 6
    j = lax.broadcasted_iota(jnp.int32, (128, 128), 1) >> 6
    return jnp.dot(x, jnp.where(i == j, 1.0 / HD, 0.0).astype(F32), precision=lax.Precision.HIGHEST,
                   preferred_element_type=F32)


def _head_norm(x, g2):
    inv = lax.rsqrt(_seg_mean(x * x) + EPS)
    y = x * inv
    return y * g2, y, inv


def _head_norm_bwd(dxn, y, inv, g2):
    dy = dxn * g2
    return inv * (dy - y * _seg_mean(dy * y)), jnp.sum(dxn * y, axis=0, keepdims=True)


def _swa_block(q, kk, vv, gq2, gk2, sinks, first, do=None):
    lo = lax.broadcasted_iota(jnp.int32, (1, 128), 1) < HD
    kn, ky, kinv = _head_norm(kk, gk2)

    def operands(x):
        xr = pltpu.roll(x, HD, 1)
        own_lo, own_hi = jnp.where(lo, x, 0.0).astype(BF16), jnp.where(lo, 0.0, x).astype(BF16)
        rolled_lo, rolled_hi = jnp.where(lo, xr, 0.0).astype(BF16), jnp.where(lo, 0.0, xr).astype(BF16)
        return (own_lo, rolled_hi), (rolled_lo, own_hi)

    def restore(parts):
        (own_lo, rolled_hi), (rolled_lo, own_hi) = parts
        return (jnp.where(lo, own_lo, own_hi)
                + pltpu.roll(jnp.where(lo, rolled_lo, rolled_hi), HD, 1))

    k_ops, v_ops = operands(kn), operands(vv)
    k2 = [jnp.concatenate(pair, axis=0) for pair in k_ops]
    row = lax.broadcasted_iota(jnp.int32, (BLK, 2 * BLK), 0)
    col = lax.broadcasted_iota(jnp.int32, (BLK, 2 * BLK), 1)
    valid = (col <= row + BLK) & (col > row) & (col >= jnp.where(first, BLK, 0))
    scale = HD ** -0.5
    outs, dqs, dsinks = [], [], []
    dk2 = [jnp.zeros((4 * BLK, 128), F32) for _ in range(N_KV)]
    dv_ops = [[jnp.zeros((2 * BLK, 128), F32) for _ in range(2)] for _ in range(N_KV)]
    dgq2 = jnp.zeros((1, 128), F32)
    for p in range(4):
        kh = p // 2
        qn, qy, qinv = _head_norm(q[:, 128 * p:128 * (p + 1)], gq2)
        qnb = qn.astype(BF16)
        s2 = _dot_nt(qnb, k2[kh]) * scale
        probs, p_sink = [], []
        for j in range(2):
            s = jnp.where(valid, s2[:, 2 * BLK * j:2 * BLK * (j + 1)], NEG)
            sink = sinks[:, 2 * p + j:2 * p + j + 1]
            m = jnp.maximum(jnp.max(s, axis=-1, keepdims=True), sink)
            e = jnp.exp(s - m)
            e_sink = jnp.exp(sink - m)
            rden = 1.0 / (jnp.sum(e, axis=-1, keepdims=True) + e_sink)
            probs.append(e * rden)
            p_sink.append(e_sink * rden)
        if do is None:
            outs.append(_dot(probs[0].astype(BF16), v_ops[kh][0]) + _dot(probs[1].astype(BF16), v_ops[kh][1]))
            continue
        dob = do[:, 128 * p:128 * (p + 1)].astype(BF16)
        ds = []
        for j in range(2):
            dv_ops[kh][j] = dv_ops[kh][j] + _dot_tn(probs[j].astype(BF16), dob)
            dp = _dot_nt(dob, v_ops[kh][j])
            t = jnp.sum(dp * probs[j], axis=-1, keepdims=True)
            ds.append(probs[j] * (dp - t) * scale)
            dsinks.append(-jnp.sum(p_sink[j] * t))
        dsb = jnp.concatenate(ds, axis=1).astype(BF16)
        dk2[kh] = dk2[kh] + _dot_tn(dsb, qnb)
        dq_p, dg = _head_norm_bwd(_dot(dsb, k2[kh]), qy, qinv, gq2)
        dqs.append(dq_p)
        dgq2 = dgq2 + dg
    if do is None:
        return jnp.concatenate(outs, axis=1)
    dkn = restore(tuple((d[:2 * BLK], d[2 * BLK:]) for d in dk2))
    dvv = restore(tuple(tuple(d) for d in dv_ops))
    dkk, dgk2 = _head_norm_bwd(dkn, ky, kinv, gk2)
    return jnp.concatenate(dqs, axis=1), dkk, dvv, dgq2, dgk2, dsinks


SWA_GROUP = 2


def _swa_specs(nbs):
    grp = min(SWA_GROUP, nbs)
    rows = grp * BLK

    def tok(w):
        return pl.BlockSpec((rows, w), lambda i: (i, 0))

    halo = pl.BlockSpec((BLK, 128), lambda i: (jnp.maximum(i * grp - 1, 0), 0))
    vec = pl.BlockSpec((1, 128), lambda i: (0, 0))
    sk = pl.BlockSpec((1, 8), lambda i: (0, 0))
    return grp, tok, halo, vec, sk


def _swa_fwd(q, k, v, gq2, gk2, sinks, nbs, plans=()):
    T = q.shape[0]
    grp, tok, halo, vec, sk = _swa_specs(nbs)

    def body(q_ref, kh_ref, kc_ref, vh_ref, vc_ref, gq_ref, gk_ref, sk_ref, o_ref):
        seq_start = ((pl.program_id(0) * grp) % nbs) == 0
        for g in range(grp):
            rows = slice(g * BLK, (g + 1) * BLK)
            prev = slice((g - 1) * BLK, g * BLK)
            kk = jnp.concatenate([kh_ref[...] if g == 0 else kc_ref[prev, :], kc_ref[rows, :]], axis=0)
            vv = jnp.concatenate([vh_ref[...] if g == 0 else vc_ref[prev, :], vc_ref[rows, :]], axis=0)
            o_ref[rows, :] = _swa_block(q_ref[rows, :], kk, vv, gq_ref[...], gk_ref[...], sk_ref[...],
                                        seq_start if g == 0 else False)

    (out,), plan_outs = _call(
        body, name="swa_fwd", grid=(T // (grp * BLK),),
        in_specs=[tok(D_B), halo, tok(128), halo, tok(128), vec, vec, sk], out_specs=[tok(D_B)],
        out_shape=[jax.ShapeDtypeStruct((T, D_B), F32)], args=(q, k, k, v, v, gq2, gk2, sinks),
        semantics=("parallel",), plans=plans)
    return out, plan_outs


def _swa_bwd(q, k, v, gq2, gk2, sinks, do, nbs, plans=()):
    T = q.shape[0]
    grp, tok, halo, vec, sk = _swa_specs(nbs)
    steps = T // (grp * BLK)

    def body(q_ref, kh_ref, kc_ref, vh_ref, vc_ref, gq_ref, gk_ref, sk_ref, do_ref,
             dq_ref, dk_ref, dv_ref, dkh_ref, dvh_ref, dgq_ref, dgk_ref, dsk_ref):
        i = pl.program_id(0)
        seq_start = ((i * grp) % nbs) == 0

        @pl.when(i == 0)
        def _():
            for r in (dgq_ref, dgk_ref, dsk_ref):
                r[...] = jnp.zeros(r.shape, F32)

        res = []
        for g in range(grp):
            rows = slice(g * BLK, (g + 1) * BLK)
            prev = slice((g - 1) * BLK, g * BLK)
            kk = jnp.concatenate([kh_ref[...] if g == 0 else kc_ref[prev, :], kc_ref[rows, :]], axis=0)
            vv = jnp.concatenate([vh_ref[...] if g == 0 else vc_ref[prev, :], vc_ref[rows, :]], axis=0)
            res.append(_swa_block(q_ref[rows, :], kk, vv, gq_ref[...], gk_ref[...], sk_ref[...],
                                  seq_start if g == 0 else False, do=do_ref[rows, :]))
        lane = lax.broadcasted_iota(jnp.int32, (8, 128), 1)
        upd = jnp.zeros((8, 128), F32)
        for g, (dq, dkk, dvv, dgq2, dgk2, dsinks) in enumerate(res):
            rows = slice(g * BLK, (g + 1) * BLK)
            dq_ref[rows, :] = dq
            dk_ref[rows, :] = dkk[BLK:] + res[g + 1][1][:BLK] if g + 1 < grp else dkk[BLK:]
            dv_ref[rows, :] = dvv[BLK:] + res[g + 1][2][:BLK] if g + 1 < grp else dvv[BLK:]
            dgq_ref[...] += dgq2
            dgk_ref[...] += dgk2
            for h, d in enumerate(dsinks):
                upd = upd + jnp.where(lane == h, d, 0.0)
        dkh_ref[0] = res[0][1][:BLK]
        dvh_ref[0] = res[0][2][:BLK]
        dsk_ref[...] += upd

    one = pl.BlockSpec((1, BLK, 128), lambda i: (i, 0, 0))
    halo_shape = jax.ShapeDtypeStruct((steps, BLK, 128), F32)
    return _call(
        body, name="swa_bwd", grid=(steps,),
        in_specs=[tok(D_B), halo, tok(128), halo, tok(128), vec, vec, sk, tok(D_B)],
        out_specs=[tok(D_B), tok(128), tok(128), one, one, vec, vec, pl.BlockSpec((8, 128), lambda i: (0, 0))],
        out_shape=[jax.ShapeDtypeStruct((T, D_B), F32), jax.ShapeDtypeStruct((T, 128), F32),
                   jax.ShapeDtypeStruct((T, 128), F32), halo_shape, halo_shape, jax.ShapeDtypeStruct((1, 128), F32),
                   jax.ShapeDtypeStruct((1, 128), F32), jax.ShapeDtypeStruct((8, 128), F32)],
        args=(q, k, k, v, v, gq2, gk2, sinks, do), semantics=("arbitrary",), plans=plans)


def _swa_add_halo(dk, dk_halo):
    steps = dk_halo.shape[0]
    nxt = jnp.concatenate([dk_halo[1:], jnp.zeros_like(dk_halo[:1])], axis=0)[:, None]
    dk = dk.reshape(steps, -1, BLK, 128)
    return jnp.concatenate([dk[:, :-1], dk[:, -1:] + nxt], axis=1).reshape(-1, 128)


def _sgu_norm(av, g_ln, b_ln):
    t, th = _gelu(av)
    mu = jnp.mean(t, axis=-1, keepdims=True)
    tc = t - mu
    rstd = lax.rsqrt(jnp.mean(tc * tc, axis=-1, keepdims=True) + EPS)
    vhat = tc * rstd
    return vhat * g_ln + b_ln, vhat, rstd, th


def _masked_ws(ws_ref):
    tril = lax.broadcasted_iota(jnp.int32, (BLK, BLK), 0) >= lax.broadcasted_iota(jnp.int32, (BLK, BLK), 1)
    return [jnp.where(tril, ws_ref[g], 0.0).astype(BF16) for g in range(N_GRP)]


def _mix_fwd(au, av, gta, gtb, ob, h, mods, g_ln, b_ln, ws, bsb, wa, wb, wout, seq):
    T = h.shape[0]
    tm = _token_tile(seq)
    tps = seq // tm

    def body(au_ref, av_ref, gta_ref, gtb_ref, ob_ref, h_ref, m_ref, gl_ref, bl_ref, ws_ref, bs_ref,
             wa_ref, wb_ref, wo_ref, out_ref, vvb_s, z_s):
        u, _ = _gelu(au_ref[...])
        vv, _, _, _ = _sgu_norm(av_ref[...], gl_ref[...], bl_ref[...])
        vvb_s[...] = vv.astype(BF16)
        wsm = _masked_ws(ws_ref)
        for c in range(tm // BLK):
            rows = slice(c * BLK, (c + 1) * BLK)
            for g in range(N_GRP):
                cols = slice(g * BLK, (g + 1) * BLK)
                z_s[rows, cols] = _dot(wsm[g], vvb_s[rows, cols]) + bs_ref[g]
        ya = _dot_nt((u * z_s[...]).astype(BF16), wa_ref[...])
        yb = _dot_nt(ob_ref[...].astype(BF16), wb_ref[...])
        merged = jax.nn.sigmoid(gta_ref[...]) * ya + jax.nn.sigmoid(gtb_ref[...]) * yb
        out_ref[...] = h_ref[...] + m_ref[0, 5:6, :] * _dot(merged.astype(BF16), wo_ref[...])

    def tok(w):
        return pl.BlockSpec((tm, w), lambda i: (i, 0))

    def full(shape):
        return pl.BlockSpec(shape, lambda i: (0,) * len(shape))

    return pl.pallas_call(
        body, name="mix_fwd", grid=(T // tm,),
        in_specs=[tok(D_A), tok(D_A), tok(D), tok(D), tok(D_B), tok(D),
                  pl.BlockSpec((1, 9, D), lambda i: (i // tps, 0, 0)), full((1, D_A)), full((1, D_A)),
                  full((N_GRP, BLK, BLK)), full((N_GRP, BLK, BLK)), _vmem(), _vmem(), _vmem()],
        out_specs=tok(D), out_shape=jax.ShapeDtypeStruct((T, D), F32),
        scratch_shapes=[pltpu.VMEM((tm, D_A), BF16), pltpu.VMEM((tm, D_A), F32)],
        compiler_params=_params("parallel"),
    )(au, av, gta, gtb, ob, h, mods, g_ln, b_ln, ws, bsb, wa, wb, wout)


def _mix_bwd(au, av, gta, gtb, ob, dh, mods, g_ln, b_ln, ws, bsb, wa, wb, wout, seq, plans=()):
    T = dh.shape[0]
    B = T // seq
    tm = _token_tile(seq, 256)
    tps = seq // tm

    def body(au_ref, av_ref, gta_ref, gtb_ref, ob_ref, dh_ref, m_ref, gl_ref, bl_ref, ws_ref, bs_ref,
             wa_ref, wb_ref, wo_ref,
             dau_ref, dav_ref, dgta_ref, dgtb_ref, dob_ref, dwo_ref, dwa_ref, dwb_ref, dws_ref, dbs_ref, dln_ref,
             mg_ref, vvb_s, z_s, dz_s, dzb_s, dvv_s):
        i = pl.program_id(0)

        @pl.when(i == 0)
        def _():
            for r in (dwo_ref, dwa_ref, dwb_ref, dws_ref, dbs_ref, dln_ref):
                r[...] = jnp.zeros(r.shape, F32)

        @pl.when(i % tps == 0)
        def _():
            mg_ref[...] = jnp.zeros(mg_ref.shape, F32)

        auv = au_ref[...]
        avv = av_ref[...]
        u, thu = _gelu(auv)
        g_ln = gl_ref[...]
        vv, vhat, rstd, thv = _sgu_norm(avv, g_ln, bl_ref[...])
        vvb_s[...] = vv.astype(BF16)
        wsm = _masked_ws(ws_ref)
        for c in range(tm // BLK):
            rows = slice(c * BLK, (c + 1) * BLK)
            for g in range(N_GRP):
                cols = slice(g * BLK, (g + 1) * BLK)
                z_s[rows, cols] = _dot(wsm[g], vvb_s[rows, cols]) + bs_ref[g]
        z = z_s[...]
        yab = (u * z).astype(BF16)
        obb = ob_ref[...].astype(BF16)
        ya = _dot_nt(yab, wa_ref[...])
        yb = _dot_nt(obb, wb_ref[...])
        sa = jax.nn.sigmoid(gta_ref[...])
        sb = jax.nn.sigmoid(gtb_ref[...])
        mb = (sa * ya + sb * yb).astype(BF16)
        dhv = dh_ref[...]
        mg_ref[0, 0:1, :] += jnp.sum(dhv * _dot(mb, wo_ref[...]), axis=0, keepdims=True)
        dmob = (m_ref[0, 5:6, :] * dhv).astype(BF16)
        dwo_ref[...] += _dot_tn(mb, dmob)
        dmerged = _dot_nt(dmob, wo_ref[...])
        dya = dmerged * sa
        dyb = dmerged * sb
        dgta_ref[...] = dya * ya * (1.0 - sa)
        dgtb_ref[...] = dyb * yb * (1.0 - sb)
        dyab = dya.astype(BF16)
        dybb = dyb.astype(BF16)
        dwa_ref[...] += _dot_tn(dyab, yab)
        dwb_ref[...] += _dot_tn(dybb, obb)
        dob_ref[...] = _dot(dybb, wb_ref[...])
        dyap = _dot(dyab, wa_ref[...])
        dau_ref[...] = (dyap * z) * _gelu_grad(auv, thu)
        dz = dyap * u
        dz_s[...] = dz
        dzb_s[...] = dz.astype(BF16)
        for c in range(tm // BLK):
            rows = slice(c * BLK, (c + 1) * BLK)
            for g in range(N_GRP):
                cols = slice(g * BLK, (g + 1) * BLK)
                dzb = dzb_s[rows, cols]
                dvv_s[rows, cols] = _dot_tn(wsm[g], dzb)
                dws_ref[g] += _dot_nt(dzb, vvb_s[rows, cols])
                dbs_ref[g] += dz_s[rows, cols]
        dvv = dvv_s[...]
        dln_ref[0:1, :] += jnp.sum(dvv * vhat, axis=0, keepdims=True)
        dln_ref[1:2, :] += jnp.sum(dvv, axis=0, keepdims=True)
        dvh = dvv * g_ln
        dt = rstd * (dvh - jnp.mean(dvh, axis=-1, keepdims=True)
                     - vhat * jnp.mean(dvh * vhat, axis=-1, keepdims=True))
        dav_ref[...] = dt * _gelu_grad(avv, thv)

    def tok(w):
        return pl.BlockSpec((tm, w), lambda i: (i, 0))

    def full(shape):
        return pl.BlockSpec(shape, lambda i: (0,) * len(shape))

    return _call(
        body, name="mix_bwd", grid=(T // tm,),
        in_specs=[tok(D_A), tok(D_A), tok(D), tok(D), tok(D_B), tok(D),
                  pl.BlockSpec((1, 9, D), lambda i: (i // tps, 0, 0)), full((1, D_A)), full((1, D_A)),
                  full((N_GRP, BLK, BLK)), full((N_GRP, BLK, BLK)), _vmem(), _vmem(), _vmem()],
        out_specs=[tok(D_A), tok(D_A), tok(D), tok(D), tok(D_B), full((D, D)), full((D, D_A)), full((D, D_B)),
                   full((N_GRP, BLK, BLK)), full((N_GRP, BLK, BLK)), full((8, D_A)),
                   pl.BlockSpec((1, 8, D), lambda i: (i // tps, 0, 0))],
        out_shape=[jax.ShapeDtypeStruct((T, D_A), F32), jax.ShapeDtypeStruct((T, D_A), F32),
                   jax.ShapeDtypeStruct((T, D), F32), jax.ShapeDtypeStruct((T, D), F32),
                   jax.ShapeDtypeStruct((T, D_B), F32), jax.ShapeDtypeStruct((D, D), F32),
                   jax.ShapeDtypeStruct((D, D_A), F32), jax.ShapeDtypeStruct((D, D_B), F32),
                   jax.ShapeDtypeStruct((N_GRP, BLK, BLK), F32), jax.ShapeDtypeStruct((N_GRP, BLK, BLK), F32),
                   jax.ShapeDtypeStruct((8, D_A), F32), jax.ShapeDtypeStruct((B, 8, D), F32)],
        scratch_shapes=[pltpu.VMEM((tm, D_A), BF16), pltpu.VMEM((tm, D_A), F32), pltpu.VMEM((tm, D_A), F32),
                        pltpu.VMEM((tm, D_A), BF16), pltpu.VMEM((tm, D_A), F32)],
        args=(au, av, gta, gtb, ob, dh, mods, g_ln, b_ln, ws, bsb, wa, wb, wout), semantics=("arbitrary",),
        plans=plans)


def _ada_fwd(c_all, w_ada):
    def body(c_ref, w_ref, o_ref):
        cv = c_ref[...]
        o_ref[...] = _dot((cv * jax.nn.sigmoid(cv)).astype(BF16), w_ref[...].astype(BF16))

    out = jax.ShapeDtypeStruct((c_all.shape[0], w_ada.shape[1]), F32)
    return pl.pallas_call(body, name="ada_fwd", grid=(1,), in_specs=[_whole(c_all), _whole(w_ada)],
                          out_specs=_whole(out), out_shape=out, compiler_params=_params("arbitrary"))(c_all, w_ada)


def _ada_update(c_all, dm_cols, w, m, v, plans=()):
    n, cols = c_all.shape[0], w.shape[1]

    def body(c_ref, dm_ref, w_ref, m_ref, v_ref, g_ref, d_ref, nm_ref, nv_ref):
        cv = c_ref[...]
        g = _dot_tn((cv * jax.nn.sigmoid(cv)).astype(BF16), dm_ref[...].astype(BF16))
        g_ref[...] = g
        d_ref[...], nm_ref[...], nv_ref[...] = _adamw(w_ref[...], g, m_ref[...], v_ref[...])

    col = pl.BlockSpec((D, 128), lambda j: (0, j))
    return _call(
        body, name="ada_update", grid=(cols // 128,),
        in_specs=[pl.BlockSpec((n, D), lambda j: (0, 0)), pl.BlockSpec((n, 128), lambda j: (0, j)), col, col, col],
        out_specs=[col] * 4, out_shape=[jax.ShapeDtypeStruct(w.shape, F32)] * 4, args=(c_all, dm_cols, w, m, v),
        semantics=("parallel",), plans=plans)


def _mod_finish(mg1, mg2, mg2g, mg3, mods, g1, g2, g3):
    B = mods.shape[0]

    def body(mg1_ref, mg2_ref, mg2g_ref, mg3_ref, m_ref, g1_ref, g2_ref, g3_ref, dm_ref, dgn_ref):
        dgn_ref[...] = jnp.zeros(dgn_ref.shape, F32)
        for k, (mg, g_ref) in enumerate(((mg1_ref, g1_ref), (mg2_ref, g2_ref), (mg3_ref, g3_ref))):
            for b in range(B):
                s_dxr = mg[b, 1:2, :]
                dm_ref[b, 3 * k:3 * k + 1, :] = mg[b, 0:1, :]
                dm_ref[b, 3 * k + 1:3 * k + 2, :] = g_ref[...] * s_dxr
                dm_ref[b, 3 * k + 2:3 * k + 3, :] = mg2g_ref[b, 0:1, :] if k == 1 else mg[b, 2:3, :]
                dgn_ref[k:k + 1, :] += (1.0 + m_ref[b, 3 * k + 1:3 * k + 2, :]) * s_dxr

    args = (mg1, mg2, mg2g, mg3, mods, g1, g2, g3)
    out_shape = [jax.ShapeDtypeStruct((B, 9, D), F32), jax.ShapeDtypeStruct((8, D), F32)]
    return pl.pallas_call(body, name="mod_finish", grid=(1,), in_specs=[_whole(a) for a in args],
                          out_specs=[_whole(o) for o in out_shape], out_shape=out_shape,
                          compiler_params=_params("arbitrary"))(*args)


def _small_update(gathered, params, ms, vs):
    n = len(SMALL)
    B = gathered[0].shape[1]

    def body(*refs):
        gdm, ggn, gln, gws, gbs, ggq, ggk, gsk = refs[:8]
        w = dict(zip(SMALL, refs[8:8 + n]))
        m = dict(zip(SMALL, refs[8 + n:8 + 2 * n]))
        v = dict(zip(SMALL, refs[8 + 2 * n:8 + 3 * n]))
        outs = refs[8 + 3 * n:]
        out = {name: outs[4 * k:4 * k + 4] for k, name in enumerate(SMALL)}

        def total(ref, idx):
            acc = ref[(0,) + idx]
            for dev in range(1, N_DEV):
                acc = acc + ref[(dev,) + idx]
            return acc

        def finish(name, g, idx=(Ellipsis,)):
            d, nm, nv = _adamw(w[name][idx], g, m[name][idx], v[name][idx])
            for ref, val in zip(out[name], (g, d, nm, nv)):
                ref[idx] = val

        g_bada = total(gdm, (slice(0, 1),))
        for b in range(1, B):
            g_bada = g_bada + total(gdm, (slice(b, b + 1),))
        finish("b_ada", g_bada)
        finish("g_norm1", total(ggn, (slice(0, 1),)))
        finish("g_norm2", total(ggn, (slice(1, 2),)))
        finish("g_norm3", total(ggn, (slice(2, 3),)))
        finish("g_sgu_ln", total(gln, (slice(0, 1),)))
        finish("b_sgu_ln", total(gln, (slice(1, 2),)))
        tril = lax.broadcasted_iota(jnp.int32, (BLK, BLK), 0) >= lax.broadcasted_iota(jnp.int32, (BLK, BLK), 1)
        for g in range(N_GRP):
            finish("w_spatial", jnp.where(tril, total(gws, (g,)), 0.0), (0, g))
            finish("b_spatial", jnp.sum(total(gbs, (g,)).T, axis=0, keepdims=True), (0, slice(g, g + 1)))
        finish("g_q", total(ggq, ()))
        finish("g_k", total(ggk, ()))
        finish("attn_sinks", total(gsk, (slice(0, 1), slice(0, N_KV * Q_PER_KV))))

    args = list(gathered) + [params[k] for k in SMALL] + [ms[k] for k in SMALL] + [vs[k] for k in SMALL]
    out_shape = []
    for k in SMALL:
        out_shape += [jax.ShapeDtypeStruct(params[k].shape, F32)] * 4
    res = pl.pallas_call(body, name="small_update", grid=(1,), in_specs=[_whole(a) for a in args],
                         out_specs=[_whole(o) for o in out_shape], out_shape=out_shape,
                         compiler_params=_params("arbitrary"))(*args)
    return {k: res[4 * i:4 * i + 4] for i, k in enumerate(SMALL)}


def _rs_add(p, r, core, name):
    _, rows, width = p.shape

    def body(c_ref, p_ref, r_ref, o_ref):
        o_ref[...] = (p_ref[...] + r_ref[...]).astype(BF16)

    return pl.pallas_call(
        body, name="rs_add_" + name, out_shape=jax.ShapeDtypeStruct((4, rows, width), BF16),
        grid_spec=pltpu.PrefetchScalarGridSpec(
            num_scalar_prefetch=1, grid=(4,),
            in_specs=[pl.BlockSpec((1, rows, width), lambda k, c_ref: (2 * k + c_ref[0], 0, 0)),
                      pl.BlockSpec((1, rows, width), lambda k, c_ref: (k, 0, 0))],
            out_specs=pl.BlockSpec((1, rows, width), lambda k, c_ref: (k, 0, 0))),
        compiler_params=_params("parallel"),
    )(core, p, r)


def _rs_sum(r, name):
    def body(r_ref, g_ref):
        g = r_ref[0].astype(F32)
        for k in range(1, 4):
            g = g + r_ref[k].astype(F32)
        g_ref[...] = g

    out = jax.ShapeDtypeStruct(r.shape[1:], F32)
    return pl.pallas_call(body, name="rs_sum_" + name, grid=(1,), in_specs=[_whole(r)], out_specs=_whole(out),
                          out_shape=out, compiler_params=_params("arbitrary"))(r)


def _adam_big(w, m, v, name, g=None, r=None):
    from_parts = r is not None
    _, rows, cols = w.shape
    br = 256 if rows % 256 == 0 else rows

    def body(g_ref, w_ref, m_ref, v_ref, go_ref, d_ref, nm_ref, nv_ref):
        if from_parts:
            g = g_ref[0].astype(F32)
            for k in range(1, 4):
                g = g + g_ref[k].astype(F32)
        else:
            g = g_ref[...]
        go_ref[0] = g
        d_ref[0], nm_ref[0], nv_ref[0] = _adamw(w_ref[0], g, m_ref[0], v_ref[0])

    shard = pl.BlockSpec((1, br, cols), lambda i: (0, i, 0))
    grad = pl.BlockSpec((4, br, cols), lambda i: (0, i, 0)) if from_parts else pl.BlockSpec((br, cols), lambda i: (i, 0))
    return pl.pallas_call(body, name="adam_" + name, grid=(rows // br,), in_specs=[grad, shard, shard, shard],
                          out_specs=[shard] * 4, out_shape=[jax.ShapeDtypeStruct(w.shape, F32)] * 4,
                          compiler_params=_params("parallel"))(r if from_parts else g, w, m, v)


def kernel(x, c, w_ada, b_ada, g_norm1, ffn1_w_gate, ffn1_w_up, ffn1_w_down, g_norm2, w_in, g_sgu_ln, b_sgu_ln, w_spatial, b_spatial, g_q, g_k, attn_sinks, w_branch_a, w_branch_b, w_out, g_norm3, ffn2_w_gate, ffn2_w_up, ffn2_w_down, loss_target, m_w_ada, m_b_ada, m_g_norm1, m_ffn1_w_gate, m_ffn1_w_up, m_ffn1_w_down, m_g_norm2, m_w_in, m_g_sgu_ln, m_b_sgu_ln, m_w_spatial, m_b_spatial, m_g_q, m_g_k, m_attn_sinks, m_w_branch_a, m_w_branch_b, m_w_out, m_g_norm3, m_ffn2_w_gate, m_ffn2_w_up, m_ffn2_w_down, v_w_ada, v_b_ada, v_g_norm1, v_ffn1_w_gate, v_ffn1_w_up, v_ffn1_w_down, v_g_norm2, v_w_in, v_g_sgu_ln, v_b_sgu_ln, v_w_spatial, v_b_spatial, v_g_q, v_g_k, v_attn_sinks, v_w_branch_a, v_w_branch_b, v_w_out, v_g_norm3, v_ffn2_w_gate, v_ffn2_w_up, v_ffn2_w_down):
    names = ("w_ada", "b_ada", "g_norm1", "ffn1_w_gate", "ffn1_w_up", "ffn1_w_down", "g_norm2", "w_in", "g_sgu_ln",
             "b_sgu_ln", "w_spatial", "b_spatial", "g_q", "g_k", "attn_sinks", "w_branch_a", "w_branch_b", "w_out",
             "g_norm3", "ffn2_w_gate", "ffn2_w_up", "ffn2_w_down")
    w = dict(zip(names, (w_ada, b_ada, g_norm1, ffn1_w_gate, ffn1_w_up, ffn1_w_down, g_norm2, w_in, g_sgu_ln,
                         b_sgu_ln, w_spatial, b_spatial, g_q, g_k, attn_sinks, w_branch_a, w_branch_b, w_out, g_norm3,
                         ffn2_w_gate, ffn2_w_up, ffn2_w_down)))
    m = dict(zip(names, (m_w_ada, m_b_ada, m_g_norm1, m_ffn1_w_gate, m_ffn1_w_up, m_ffn1_w_down, m_g_norm2, m_w_in,
                         m_g_sgu_ln, m_b_sgu_ln, m_w_spatial, m_b_spatial, m_g_q, m_g_k, m_attn_sinks, m_w_branch_a,
                         m_w_branch_b, m_w_out, m_g_norm3, m_ffn2_w_gate, m_ffn2_w_up, m_ffn2_w_down)))
    v = dict(zip(names, (v_w_ada, v_b_ada, v_g_norm1, v_ffn1_w_gate, v_ffn1_w_up, v_ffn1_w_down, v_g_norm2, v_w_in,
                         v_g_sgu_ln, v_b_sgu_ln, v_w_spatial, v_b_spatial, v_g_q, v_g_k, v_attn_sinks, v_w_branch_a,
                         v_w_branch_b, v_w_out, v_g_norm3, v_ffn2_w_gate, v_ffn2_w_up, v_ffn2_w_down)))
    B, seq, _ = x.shape
    T = B * seq
    nbs = seq // BLK
    xi, yi, ci = _place()
    me = 4 * xi + 2 * yi + ci
    core = jnp.reshape(ci, (1,)).astype(jnp.int32)
    layout = {name: (tform, n, width) for name, tform, n, width in BIG}
    shard = {name: w[name][0].astype(BF16).T if tform else w[name][0].astype(BF16) for name, tform, _, _ in BIG}
    wts, parts, big_out = {}, {}, {}

    def gather_plan(group):
        return _Gather([shard[k] for k in group])

    def take(group, gathered):
        for k, g in zip(group, gathered):
            wts[k] = g.reshape(N_DEV * layout[k][1], layout[k][2])

    def blocks(name, grad):
        return grad.reshape(N_DEV, layout[name][1], layout[name][2])

    def to_sibling(names, grads):
        return _RsSibling([blocks(k, g) for k, g in zip(names, grads)])

    def add(names, grads, from_sibling):
        for k, g, r in zip(names, grads, from_sibling):
            parts[k] = _rs_add(blocks(k, g), r, core, k)

    def to_chips(names):
        return _RsChips([parts[k] for k in names])

    def update(names, from_chips):
        for k, r in zip(names, from_chips):
            if layout[k][0]:
                big_out[k] = _adam_big(w[k], m[k], v[k], k, g=_rs_sum(r, k).T)
            else:
                big_out[k] = _adam_big(w[k], m[k], v[k], k, r=r)

    g1, g2, g3, g_ln, b_ln = g_norm1, g_norm2, g_norm3, g_sgu_ln, b_sgu_ln
    gq2, gk2 = jnp.tile(g_q, (1, 2)), jnp.tile(g_k, (1, 2))
    ws = w_spatial[0]
    bsb = jnp.broadcast_to(b_spatial[0][:, :, None], (N_GRP, BLK, BLK))
    xf = x.reshape(T, D)
    tgt = loss_target.reshape(T, D)

    ((c_all,),) = _exchange([_Gather([c])], "gather_c")
    c_all = c_all.reshape(N_DEV * B, D)
    ffn1 = ("ffn1_w_gate", "ffn1_w_up", "ffn1_w_down")
    ffn2 = ("ffn2_w_gate", "ffn2_w_up", "ffn2_w_down")
    ((mods_cols, *gathered),) = _exchange(
        [_Gather([_ada_fwd(c_all, w_ada[0])] + [shard[k] for k in ffn1])], "gather_first")
    take(ffn1, gathered)
    mine = lax.dynamic_slice_in_dim(mods_cols, B * me, B, axis=1)
    mods = (mine.transpose(1, 0, 2).reshape(B, 9 * D) + b_ada).reshape(B, 9, D)

    group = ("w_in", "w_branch_a", "w_branch_b", "w_out", "ffn2_w_gate")
    (h1, y1, gb1, ub1), (gathered,) = _ffn_fwd(xf, mods, g1, *[wts[k] for k in ffn1], 0, seq,
                                               plans=[gather_plan(group)])
    take(group, gathered)
    au, av, q_tok, k_tok, v_tok, gta, gtb = _inproj_fwd(h1, mods, g2, wts["w_in"], seq)
    group = ("ffn2_w_up", "ffn2_w_down")
    ob, (gathered,) = _swa_fwd(q_tok, k_tok, v_tok, gq2, gk2, attn_sinks, nbs, plans=[gather_plan(group)])
    take(group, gathered)
    mixw = (wts["w_branch_a"], wts["w_branch_b"], wts["w_out"])
    h2 = _mix_fwd(au, av, gta, gtb, ob, h1, mods, g_ln, b_ln, ws, bsb, *mixw, seq)
    (dh3, y3, gb3, ub3, loss_parts), _ = _ffn_fwd(h2, mods, g3, *[wts[k] for k in ffn2], 6, seq, tgt=tgt)
    loss = lax.psum(jnp.sum(loss_parts[:, 0, 0]), AXES)

    (dh2, xb, dyb, a, dg, du, mg3), _ = _ffn_bwd(h2, dh3, y3, gb3, ub3, mods, g3, *[wts[k] for k in ffn2], 6, seq)
    d_gate, _ = _wgrad(dg, xb, "wgrad_ffn2_gate")
    d_up, (r,) = _wgrad(du, xb, "wgrad_ffn2_up", plans=[to_sibling(ffn2[:1], [d_gate])])
    add(ffn2[:1], [d_gate], r)
    d_down, (r,) = _wgrad(a, dyb, "wgrad_ffn2_down", plans=[to_sibling(ffn2[1:2], [d_up])])
    add(ffn2[1:2], [d_up], r)

    (dau, dav, dgta, dgtb, dob, d_out, d_a, d_b, dws, dbs, dln, mg2g), (r, from_chips) = _mix_bwd(
        au, av, gta, gtb, ob, dh2, mods, g_ln, b_ln, ws, bsb, *mixw, seq,
        plans=[to_sibling(ffn2[2:], [d_down]), to_chips(ffn2[:2])])
    add(ffn2[2:], [d_down], r)
    update(ffn2[:2], from_chips)
    mixers = ("w_out", "w_branch_a", "w_branch_b")
    (dq, dk, dv, dk_halo, dv_halo, dgq2, dgk2, dsk), (from_chips, r) = _swa_bwd(
        q_tok, k_tok, v_tok, gq2, gk2, attn_sinks, dob, nbs,
        plans=[to_chips(ffn2[2:]), to_sibling(mixers, [d_out, d_a, d_b])])
    update(ffn2[2:], from_chips)
    add(mixers, [d_out, d_a, d_b], r)
    dk, dv = _swa_add_halo(dk, dk_halo), _swa_add_halo(dv, dv_halo)
    dgq, dgk = dgq2[:, :HD] + dgq2[:, HD:], dgk2[:, :HD] + dgk2[:, HD:]
    (dh1, xb2, dpb, mg2), (from_chips,) = _inproj_bwd(
        h1, dh2, (dau, dav, dq, dk, dv, dgta, dgtb), mods, g2, wts["w_in"], seq, plans=[to_chips(mixers)])
    update(mixers, from_chips)
    d_in, _ = _wgrad(dpb, xb2, "wgrad_w_in")

    (dx, xb, dyb, a, dg, du, mg1), (r,) = _ffn_bwd(
        xf, dh1, y1, gb1, ub1, mods, g1, *[wts[k] for k in ffn1], 0, seq, plans=[to_sibling(("w_in",), [d_in])])
    add(("w_in",), [d_in], r)
    dmods, dgn = _mod_finish(mg1, mg2, mg2g, mg3, mods, g1, g2, g3)
    small_parts = [dmods.reshape(B, 9 * D), dgn, dln, dws, dbs, dgq, dgk, dsk]
    d_gate, (from_chips, gathered) = _wgrad(dg, xb, "wgrad_ffn1_gate",
                                            plans=[to_chips(("w_in",)), _Gather(small_parts)])
    update(("w_in",), from_chips)
    d_up, (r,) = _wgrad(du, xb, "wgrad_ffn1_up", plans=[to_sibling(ffn1[:1], [d_gate])])
    add(ffn1[:1], [d_gate], r)
    d_down, (r, from_chips) = _wgrad(a, dyb, "wgrad_ffn1_down",
                                     plans=[to_sibling(ffn1[1:2], [d_up]), to_chips(ffn1[:1])])
    add(ffn1[1:2], [d_up], r)
    update(ffn1[:1], from_chips)
    small_out = _small_update(gathered, w, m, v)
    dm_cols = lax.dynamic_slice_in_dim(gathered[0].reshape(N_DEV * B, 9 * D), (9 * D // N_DEV) * me,
                                       9 * D // N_DEV, axis=1)
    ada_out, (r,) = _ada_update(c_all, dm_cols, w_ada[0], m_w_ada[0], v_w_ada[0],
                                plans=[to_sibling(ffn1[2:], [d_down])])
    add(ffn1[2:], [d_down], r)
    (from_chips,) = _exchange([to_chips(ffn1[1:])], "rs_last")
    update(ffn1[1:], from_chips)

    def leaf(kind, name):
        if name == "w_ada":
            return ada_out[kind][None]
        if name in SMALL:
            return small_out[name][kind]
        return big_out[name][kind]

    return (loss, dx.reshape(B, seq, D), *[leaf(kind, name) for kind in range(4) for name in names])
```

```python
import functools
import math

import jax
import jax.numpy as jnp
from jax import lax
from jax.experimental import pallas as pl
from jax.experimental.pallas import tpu as pltpu

F32 = jnp.float32
BF16 = jnp.bfloat16
MESH = pl.DeviceIdType.MESH
AXES = ("x", "y", "c")
N_DEV = 8

VMEM_LIMIT = 56 * 1024 * 1024

D = 1024
FF = 2816
FC = 1408
D_A = 512
D_B = 512
HD = 64
N_KV = 2
Q_PER_KV = 4
BLK = 128
N_GRP = 4
IN_COLS = 3840
PIECES = (("au", 0, 512), ("av", 512, 512), ("q", 1024, 512), ("k", 1536, 128), ("v", 1664, 128),
          ("ga", 1792, 1024), ("gb", 2816, 1024))
EPS = 1e-6
NEG = -1e30
GELU_C = math.sqrt(2.0 / math.pi)

ADAM_LR = 0.001
ADAM_B1 = 0.9
ADAM_B2 = 0.999
ADAM_EPS = 1e-08
ADAM_WD = 0.01
ADAM_STEP = 10

NT = (((1,), (1,)), ((), ()))
TN = (((0,), (0,)), ((), ()))

BIG = (("ffn1_w_gate", True, FF // N_DEV, D), ("ffn1_w_up", True, FF // N_DEV, D),
       ("ffn1_w_down", False, FF // N_DEV, D), ("w_in", True, IN_COLS // N_DEV, D),
       ("w_branch_a", True, D // N_DEV, D_A), ("w_branch_b", True, D // N_DEV, D_B), ("w_out", False, D // N_DEV, D),
       ("ffn2_w_gate", True, FF // N_DEV, D), ("ffn2_w_up", True, FF // N_DEV, D),
       ("ffn2_w_down", False, FF // N_DEV, D))
SMALL = ("b_ada", "g_norm1", "g_norm2", "g_sgu_ln", "b_sgu_ln", "w_spatial", "b_spatial", "g_q", "g_k",
         "attn_sinks", "g_norm3")


def _dot(a, b):
    return jnp.dot(a, b, preferred_element_type=F32)


def _dot_nt(a, b):
    return lax.dot_general(a, b, NT, preferred_element_type=F32)


def _dot_tn(a, b):
    return lax.dot_general(a, b, TN, preferred_element_type=F32)


def _vmem():
    return pl.BlockSpec(memory_space=pltpu.VMEM)


def _any():
    return pl.BlockSpec(memory_space=pl.ANY)


def _whole(a):
    return pl.BlockSpec(a.shape, lambda i: (0,) * len(a.shape))


def _rms_mod(h, g, sh, sc):
    inv = lax.rsqrt(jnp.mean(h * h, axis=-1, keepdims=True) + EPS)
    r = h * inv
    return (r * g) * (1.0 + sc) + sh, r, inv


def _rms_mod_bwd(dxn, r, inv, g, sc):
    dr = dxn * (g * (1.0 + sc))
    dh = inv * (dr - r * jnp.mean(dr * r, axis=-1, keepdims=True))
    return dh, jnp.sum(dxn, axis=0, keepdims=True), jnp.sum(dxn * r, axis=0, keepdims=True)


def _gelu(x):
    t = jnp.tanh(GELU_C * (x + 0.044715 * (x * x * x)))
    return 0.5 * x * (1.0 + t), t


def _gelu_grad(x, t):
    return 0.5 * (1.0 + t) + 0.5 * x * (1.0 - t * t) * (GELU_C * (1.0 + 3.0 * 0.044715 * x * x))


def _adamw(w, g, m, v):
    m = ADAM_B1 * m + (1.0 - ADAM_B1) * g
    v = ADAM_B2 * v + (1.0 - ADAM_B2) * (g * g)
    m_hat = m / (1.0 - ADAM_B1 ** ADAM_STEP)
    v_hat = v / (1.0 - ADAM_B2 ** ADAM_STEP)
    delta = -ADAM_LR * (m_hat / (jnp.sqrt(v_hat) + ADAM_EPS) + ADAM_WD * w)
    return delta, m, v


def _token_tile(seq, cap=512):
    return min(cap, seq)


def _params(*semantics):
    return pltpu.CompilerParams(dimension_semantics=semantics, vmem_limit_bytes=VMEM_LIMIT)


def _place():
    return lax.axis_index("x"), lax.axis_index("y"), lax.axis_index("c")


class _Gather:
    def __init__(self, arrays):
        n = len(arrays)
        self.ins = list(arrays)
        self.out_shape = [jax.ShapeDtypeStruct((N_DEV,) + a.shape, a.dtype) for a in arrays]
        self.scratch = [pltpu.SemaphoreType.DMA((n, 7)), pltpu.SemaphoreType.DMA((n, 7)),
                        pltpu.SemaphoreType.DMA((n,))]

    def _copies(self, ins, outs, sems):
        send_sems, recv_sems, local_sems = sems
        n = len(ins)
        x, y, c = _place()
        me, sibling = (x, y, c), (x, y, 1 - c)
        chips = [(1 - x, y), (x, 1 - y), (1 - x, 1 - y)]

        def slot(a, px, py, pc):
            return outs[a].at[4 * px + 2 * py + pc]

        def copy(a, k, block, to, src=None):
            return pltpu.make_async_remote_copy(
                src_ref=slot(a, *block) if src is None else src, dst_ref=slot(a, *block),
                send_sem=send_sems.at[a, k], recv_sem=recv_sems.at[a, k], device_id=to, device_id_type=MESH)

        mine = [pltpu.make_async_copy(ins[a], slot(a, *me), local_sems.at[a]) for a in range(n)]
        first = [copy(a, 0, me, sibling, src=ins[a]) for a in range(n)]
        first += [copy(a, 1 + j, me, (*chip, c), src=ins[a]) for a in range(n) for j, chip in enumerate(chips)]
        landed = [[copy(a, 1 + j, (*chip, c), me) for a in range(n)] for j, chip in enumerate(chips)]
        passed = [[copy(a, 4 + j, (*chip, c), sibling) for a in range(n)] for j, chip in enumerate(chips)]
        from_sibling = [copy(a, 0, sibling, me) for a in range(n)]
        from_sibling += [copy(a, 4 + j, (*chip, 1 - c), me) for a in range(n) for j, chip in enumerate(chips)]
        return mine, first, landed, passed, from_sibling

    def start(self, ins, outs, sems):
        mine, first, _, _, _ = self._copies(ins, outs, sems)
        for cp in mine + first:
            cp.start()

    def finish(self, ins, outs, sems):
        mine, first, landed, passed, from_sibling = self._copies(ins, outs, sems)
        for arrivals, forwards in zip(landed, passed):
            for arrival, forward in zip(arrivals, forwards):
                arrival.wait_recv()
                forward.start()
        for cp in from_sibling:
            cp.wait_recv()
        for cp in first + [f for fs in passed for f in fs]:
            cp.wait_send()
        for cp in mine:
            cp.wait()


class _RsSibling:
    def __init__(self, ps):
        n = len(ps)
        self.ins = list(ps)
        self.out_shape = [jax.ShapeDtypeStruct((4,) + p.shape[1:], p.dtype) for p in ps]
        self.scratch = [pltpu.SemaphoreType.DMA((n, 4)), pltpu.SemaphoreType.DMA((n, 4))]

    def _copies(self, ins, outs, sems):
        send_sems, recv_sems = sems
        x, y, c = _place()
        return [pltpu.make_async_remote_copy(
            src_ref=ins[a].at[2 * q + (1 - c)], dst_ref=outs[a].at[q], send_sem=send_sems.at[a, q],
            recv_sem=recv_sems.at[a, q], device_id=(x, y, 1 - c), device_id_type=MESH)
            for a in range(len(ins)) for q in range(4)]

    def start(self, ins, outs, sems):
        for cp in self._copies(ins, outs, sems):
            cp.start()

    def finish(self, ins, outs, sems):
        for cp in self._copies(ins, outs, sems):
            cp.wait()


class _RsChips:
    def __init__(self, qs):
        n = len(qs)
        self.ins = list(qs)
        self.out_shape = [jax.ShapeDtypeStruct(q.shape, q.dtype) for q in qs]
        self.scratch = [pltpu.SemaphoreType.DMA((n, 3)), pltpu.SemaphoreType.DMA((n, 3)),
                        pltpu.SemaphoreType.DMA((n,))]

    def _copies(self, ins, outs, sems):
        send_sems, recv_sems, local_sems = sems
        n = len(ins)
        x, y, c = _place()
        my_chip = 2 * x + y
        chips = [(1 - x, y), (x, 1 - y), (1 - x, 1 - y)]

        def copy(a, j, src_slot, dst_slot):
            px, py = chips[j]
            return pltpu.make_async_remote_copy(
                src_ref=ins[a].at[src_slot], dst_ref=outs[a].at[dst_slot], send_sem=send_sems.at[a, j],
                recv_sem=recv_sems.at[a, j], device_id=(px, py, c), device_id_type=MESH)

        own = [pltpu.make_async_copy(ins[a].at[my_chip], outs[a].at[my_chip], local_sems.at[a]) for a in range(n)]
        sends = [copy(a, j, 2 * px + py, my_chip) for a in range(n) for j, (px, py) in enumerate(chips)]
        arrivals = [copy(a, j, my_chip, 2 * px + py) for a in range(n) for j, (px, py) in enumerate(chips)]
        return own, sends, arrivals

    def start(self, ins, outs, sems):
        own, sends, _ = self._copies(ins, outs, sems)
        for cp in own + sends:
            cp.start()

    def finish(self, ins, outs, sems):
        own, sends, arrivals = self._copies(ins, outs, sems)
        for cp in arrivals:
            cp.wait_recv()
        for cp in sends:
            cp.wait_send()
        for cp in own:
            cp.wait()


def _split_plans(plans, refs_in, refs_out, refs_scr, phase):
    i = o = s = 0
    for p in plans:
        ni, no, ns = len(p.ins), len(p.out_shape), len(p.scratch)
        getattr(p, phase)(refs_in[i:i + ni], refs_out[o:o + no], refs_scr[s:s + ns])
        i, o, s = i + ni, o + no, s + ns


def _plan_results(plans, res):
    out = []
    for p in plans:
        out.append(list(res[:len(p.out_shape)]))
        res = res[len(p.out_shape):]
    return out


def _exchange(plans, name):
    c_in = [a for p in plans for a in p.ins]
    c_out = [s for p in plans for s in p.out_shape]
    c_scr = [s for p in plans for s in p.scratch]

    def body(*refs):
        cin, cout, cscr = refs[:len(c_in)], refs[len(c_in):len(c_in) + len(c_out)], refs[len(c_in) + len(c_out):]
        _split_plans(plans, cin, cout, cscr, "start")
        _split_plans(plans, cin, cout, cscr, "finish")

    res = pl.pallas_call(body, name=name, in_specs=[_any()] * len(c_in), out_specs=[_any()] * len(c_out),
                         out_shape=c_out, scratch_shapes=c_scr)(*c_in)
    return _plan_results(plans, res)


def _call(body, *, name, grid, in_specs, out_specs, out_shape, args, semantics, scratch_shapes=(), plans=()):
    n_in, n_out, n_scr = len(in_specs), len(out_specs), len(scratch_shapes)
    c_in = [a for p in plans for a in p.ins]
    c_out = [s for p in plans for s in p.out_shape]
    c_scr = [s for p in plans for s in p.scratch]

    def wrapped(*refs):
        ins, refs = refs[:n_in], refs[n_in:]
        cin, refs = refs[:len(c_in)], refs[len(c_in):]
        outs, refs = refs[:n_out], refs[n_out:]
        cout, refs = refs[:len(c_out)], refs[len(c_out):]
        scr, cscr = refs[:n_scr], refs[n_scr:]
        if plans:
            ids = [pl.program_id(d) for d in range(len(grid))]
            first = functools.reduce(jnp.logical_and, [i == 0 for i in ids])
            last = functools.reduce(jnp.logical_and, [i == g - 1 for i, g in zip(ids, grid)])
            pl.when(first)(lambda: _split_plans(plans, cin, cout, cscr, "start"))
        body(*ins, *outs, *scr)
        if plans:
            pl.when(last)(lambda: _split_plans(plans, cin, cout, cscr, "finish"))

    res = pl.pallas_call(
        wrapped, name=name, grid=grid, in_specs=list(in_specs) + [_any()] * len(c_in),
        out_specs=list(out_specs) + [_any()] * len(c_out), out_shape=list(out_shape) + c_out,
        scratch_shapes=list(scratch_shapes) + c_scr,
        compiler_params=_params(*(("arbitrary",) * len(grid) if plans else semantics)),
    )(*args, *c_in)
    return list(res[:n_out]), _plan_results(plans, res[n_out:])


def _ffn_fwd(h, mods, gn, wg, wu, wd, row0, seq, tgt=None, plans=()):
    T = h.shape[0]
    tm = _token_tile(seq, 256)
    tps = seq // tm
    n_t = T // tm
    with_loss = tgt is not None

    def body(h_ref, m_ref, g_ref, wg_ref, wu_ref, wd_ref, *rest):
        if with_loss:
            tgt_ref, out_ref, y_ref, gb_ref, ub_ref, loss_ref = rest
        else:
            out_ref, y_ref, gb_ref, ub_ref = rest
        hv = h_ref[...]
        sh = m_ref[0, row0:row0 + 1, :]
        sc = m_ref[0, row0 + 1:row0 + 2, :]
        ga = m_ref[0, row0 + 2:row0 + 3, :]
        xn, _, _ = _rms_mod(hv, g_ref[...], sh, sc)
        xb = xn.astype(BF16)
        acc = jnp.zeros((tm, D), F32)
        for c0 in range(0, FF, FC):
            gg = _dot_nt(xb, wg_ref[c0:c0 + FC, :])
            uu = _dot_nt(xb, wu_ref[c0:c0 + FC, :])
            gb_ref[:, c0:c0 + FC] = gg.astype(BF16)
            ub_ref[:, c0:c0 + FC] = uu.astype(BF16)
            a = (gg * jax.nn.sigmoid(gg)) * uu
            acc = acc + _dot(a.astype(BF16), wd_ref[c0:c0 + FC, :])
        y_ref[...] = acc
        hout = hv + (0.5 * ga) * acc
        if with_loss:
            d = hout - tgt_ref[...]
            out_ref[...] = d * (1.0 / D)
            loss_ref[...] = jnp.full((1, 8, 128), 0.5 / D, F32) * jnp.sum(d * d)
        else:
            out_ref[...] = hout

    tok = pl.BlockSpec((tm, D), lambda i: (i, 0))
    tokf = pl.BlockSpec((tm, FF), lambda i: (i, 0))
    in_specs = [tok, pl.BlockSpec((1, 9, D), lambda i: (i // tps, 0, 0)), pl.BlockSpec((1, D), lambda i: (0, 0)),
                _vmem(), _vmem(), _vmem()]
    out_shape = [jax.ShapeDtypeStruct((T, D), F32), jax.ShapeDtypeStruct((T, D), F32),
                 jax.ShapeDtypeStruct((T, FF), BF16), jax.ShapeDtypeStruct((T, FF), BF16)]
    out_specs = [tok, tok, tokf, tokf]
    args = [h, mods, gn, wg, wu, wd]
    if with_loss:
        in_specs.append(tok)
        args.append(tgt)
        out_shape.append(jax.ShapeDtypeStruct((n_t, 8, 128), F32))
        out_specs.append(pl.BlockSpec((1, 8, 128), lambda i: (i, 0, 0)))
    return _call(body, name="ffn_fwd_loss" if with_loss else "ffn_fwd", grid=(n_t,), in_specs=in_specs,
                 out_specs=out_specs, out_shape=out_shape, args=args, semantics=("parallel",), plans=plans)


def _ffn_bwd(h, dhn, y, gb, ub, mods, gn, wg, wu, wd, row0, seq, plans=()):
    T = h.shape[0]
    B = T // seq
    tm = _token_tile(seq, 256)
    tps = seq // tm
    n_t = T // tm

    def body(h_ref, dhn_ref, y_ref, gb_ref, ub_ref, m_ref, g_ref, wg_ref, wu_ref, wd_ref,
             dh_ref, xb_ref, dyb_ref, a_ref, dg_ref, du_ref, mg_ref):
        i = pl.program_id(0)
        hv = h_ref[...]
        dhn = dhn_ref[...]
        sh = m_ref[0, row0:row0 + 1, :]
        sc = m_ref[0, row0 + 1:row0 + 2, :]
        ga = m_ref[0, row0 + 2:row0 + 3, :]
        g = g_ref[...]
        xn, r, inv = _rms_mod(hv, g, sh, sc)
        xb_ref[...] = xn.astype(BF16)
        dyb = ((0.5 * ga) * dhn).astype(BF16)
        dyb_ref[...] = dyb
        dga = 0.5 * jnp.sum(dhn * y_ref[...], axis=0, keepdims=True)
        dxn = jnp.zeros((tm, D), F32)
        for c0 in range(0, FF, FC):
            wgc = wg_ref[c0:c0 + FC, :]
            wuc = wu_ref[c0:c0 + FC, :]
            gg = gb_ref[:, c0:c0 + FC].astype(F32)
            uu = ub_ref[:, c0:c0 + FC].astype(F32)
            sig = jax.nn.sigmoid(gg)
            s = gg * sig
            a_ref[:, c0:c0 + FC] = (s * uu).astype(BF16)
            da = _dot_nt(dyb, wd_ref[c0:c0 + FC, :])
            dub = (da * s).astype(BF16)
            dgb = (da * uu * (sig * (1.0 + gg * (1.0 - sig)))).astype(BF16)
            dg_ref[:, c0:c0 + FC] = dgb
            du_ref[:, c0:c0 + FC] = dub
            dxn = dxn + _dot(dgb, wgc) + _dot(dub, wuc)
        dh, s_dxn, s_dxr = _rms_mod_bwd(dxn, r, inv, g, sc)
        dh_ref[...] = dhn + dh

        @pl.when(i % tps == 0)
        def _():
            mg_ref[...] = jnp.zeros(mg_ref.shape, F32)

        mg_ref[0, 0:1, :] += s_dxn
        mg_ref[0, 1:2, :] += s_dxr
        mg_ref[0, 2:3, :] += dga

    tok = pl.BlockSpec((tm, D), lambda i: (i, 0))
    tokf = pl.BlockSpec((tm, FF), lambda i: (i, 0))
    return _call(
        body, name="ffn_bwd", grid=(n_t,),
        in_specs=[tok, tok, tok, tokf, tokf, pl.BlockSpec((1, 9, D), lambda i: (i // tps, 0, 0)),
                  pl.BlockSpec((1, D), lambda i: (0, 0)), _vmem(), _vmem(), _vmem()],
        out_specs=[tok, tok, tok, tokf, tokf, tokf, pl.BlockSpec((1, 8, D), lambda i: (i // tps, 0, 0))],
        out_shape=[jax.ShapeDtypeStruct((T, D), F32), jax.ShapeDtypeStruct((T, D), BF16),
                   jax.ShapeDtypeStruct((T, D), BF16), jax.ShapeDtypeStruct((T, FF), BF16),
                   jax.ShapeDtypeStruct((T, FF), BF16), jax.ShapeDtypeStruct((T, FF), BF16),
                   jax.ShapeDtypeStruct((B, 8, D), F32)],
        args=(h, dhn, y, gb, ub, mods, gn, wg, wu, wd), semantics=("arbitrary",), plans=plans)


def _wgrad(a, b, name, plans=()):
    T, da = a.shape
    db = b.shape[1]
    bm = {2816: 1408, 3840: 1280}[da]
    bn = db
    tk = min(1024, T)
    nk = T // tk

    def body(a_ref, b_ref, o_ref):
        @pl.when(pl.program_id(2) == 0)
        def _():
            o_ref[...] = jnp.zeros(o_ref.shape, F32)

        o_ref[...] += _dot_tn(a_ref[...], b_ref[...])

    (out,), plan_outs = _call(
        body, name=name, grid=(da // bm, db // bn, nk),
        in_specs=[pl.BlockSpec((tk, bm), lambda i, j, k: (k, i)), pl.BlockSpec((tk, bn), lambda i, j, k: (k, j))],
        out_specs=[pl.BlockSpec((bm, bn), lambda i, j, k: (i, j))], out_shape=[jax.ShapeDtypeStruct((da, db), F32)],
        args=(a, b), semantics=("parallel", "parallel", "arbitrary"), plans=plans)
    return out, plan_outs


def _inproj_fwd(h, mods, gn, w_in, seq):
    T = h.shape[0]
    tm = _token_tile(seq)
    tps = seq // tm

    def body(h_ref, m_ref, g_ref, w_ref, *outs):
        xn, _, _ = _rms_mod(h_ref[...], g_ref[...], m_ref[0, 3:4, :], m_ref[0, 4:5, :])
        xb = xn.astype(BF16)
        for (_, c0, w), o_ref in zip(PIECES, outs):
            o_ref[...] = _dot_nt(xb, w_ref[c0:c0 + w, :])

    return pl.pallas_call(
        body, name="inproj_fwd", grid=(T // tm,),
        in_specs=[pl.BlockSpec((tm, D), lambda i: (i, 0)), pl.BlockSpec((1, 9, D), lambda i: (i // tps, 0, 0)),
                  pl.BlockSpec((1, D), lambda i: (0, 0)), _vmem()],
        out_specs=[pl.BlockSpec((tm, w), lambda i: (i, 0)) for _, _, w in PIECES],
        out_shape=[jax.ShapeDtypeStruct((T, w), F32) for _, _, w in PIECES],
        compiler_params=_params("parallel"),
    )(h, mods, gn, w_in)


def _inproj_bwd(h, dh_res, dpieces, mods, gn, w_in, seq, plans=()):
    T = h.shape[0]
    B = T // seq
    tm = _token_tile(seq, 256)
    tps = seq // tm

    def body(h_ref, dres_ref, *rest):
        dp_refs = rest[:len(PIECES)]
        m_ref, g_ref, w_ref, dh_ref, xb_ref, dpb_ref, mg_ref = rest[len(PIECES):]
        i = pl.program_id(0)
        g = g_ref[...]
        sc = m_ref[0, 4:5, :]
        xn, r, inv = _rms_mod(h_ref[...], g, m_ref[0, 3:4, :], sc)
        xb_ref[...] = xn.astype(BF16)
        dxn = jnp.zeros((tm, D), F32)
        for (_, c0, w), dp_ref in zip(PIECES, dp_refs):
            dpb = dp_ref[...].astype(BF16)
            dpb_ref[:, c0:c0 + w] = dpb
            dxn = dxn + _dot(dpb, w_ref[c0:c0 + w, :])
        dh, s_dxn, s_dxr = _rms_mod_bwd(dxn, r, inv, g, sc)
        dh_ref[...] = dres_ref[...] + dh

        @pl.when(i % tps == 0)
        def _():
            mg_ref[...] = jnp.zeros(mg_ref.shape, F32)

        mg_ref[0, 0:1, :] += s_dxn
        mg_ref[0, 1:2, :] += s_dxr

    tok = pl.BlockSpec((tm, D), lambda i: (i, 0))
    return _call(
        body, name="inproj_bwd", grid=(T // tm,),
        in_specs=[tok, tok] + [pl.BlockSpec((tm, w), lambda i: (i, 0)) for _, _, w in PIECES]
        + [pl.BlockSpec((1, 9, D), lambda i: (i // tps, 0, 0)), pl.BlockSpec((1, D), lambda i: (0, 0)), _vmem()],
        out_specs=[tok, tok, pl.BlockSpec((tm, IN_COLS), lambda i: (i, 0)),
                   pl.BlockSpec((1, 8, D), lambda i: (i // tps, 0, 0))],
        out_shape=[jax.ShapeDtypeStruct((T, D), F32), jax.ShapeDtypeStruct((T, D), BF16),
                   jax.ShapeDtypeStruct((T, IN_COLS), BF16), jax.ShapeDtypeStruct((B, 8, D), F32)],
        args=(h, dh_res, *dpieces, mods, gn, w_in), semantics=("arbitrary",), plans=plans)


def _seg_mean(x):
    i = lax.broadcasted_iota(jnp.int32, (128, 128), 0) >> 6
    j = lax.broadcasted_iota(jnp.int32, (128, 128), 1) >> 6
    return jnp.dot(x, jnp.where(i == j, 1.0 / HD, 0.0).astype(F32), precision=lax.Precision.HIGHEST,
                   preferred_element_type=F32)


def _head_norm(x, g2):
    inv = lax.rsqrt(_seg_mean(x * x) + EPS)
    y = x * inv
    return y * g2, y, inv


def _head_norm_bwd(dxn, y, inv, g2):
    dy = dxn * g2
    return inv * (dy - y * _seg_mean(dy * y)), jnp.sum(dxn * y, axis=0, keepdims=True)


def _swa_block(q, kk, vv, gq2, gk2, sinks, first, do=None):
    lo = lax.broadcasted_iota(jnp.int32, (1, 128), 1) < HD
    kn, ky, kinv = _head_norm(kk, gk2)

    def operands(x):
        xr = pltpu.roll(x, HD, 1)
        own_lo, own_hi = jnp.where(lo, x, 0.0).astype(BF16), jnp.where(lo, 0.0, x).astype(BF16)
        rolled_lo, rolled_hi = jnp.where(lo, xr, 0.0).astype(BF16), jnp.where(lo, 0.0, xr).astype(BF16)
        return (own_lo, rolled_hi), (rolled_lo, own_hi)

    def restore(parts):
        (own_lo, rolled_hi), (rolled_lo, own_hi) = parts
        return (jnp.where(lo, own_lo, own_hi)
                + pltpu.roll(jnp.where(lo, rolled_lo, rolled_hi), HD, 1))

    k_ops, v_ops = operands(kn), operands(vv)
    k2 = [jnp.concatenate(pair, axis=0) for pair in k_ops]
    def stack(x):
        return jnp.concatenate([x[:, 128 * p:128 * (p + 1)] for p in range(4)], axis=0)

    def unstack(x):
        return jnp.concatenate([x[BLK * p:BLK * (p + 1)] for p in range(4)], axis=1)

    def of_head(x, kh):
        return x[2 * BLK * kh:2 * BLK * (kh + 1)]

    nq = 4 * BLK
    pair = lax.broadcasted_iota(jnp.int32, (nq, 1), 0) >> 7
    row = lax.broadcasted_iota(jnp.int32, (nq, 2 * BLK), 0) & (BLK - 1)
    col = lax.broadcasted_iota(jnp.int32, (nq, 2 * BLK), 1)
    valid = (col <= row + BLK) & (col > row) & (col >= jnp.where(first, BLK, 0))
    scale = HD ** -0.5
    qn, qy, qinv = _head_norm(stack(q), gq2)
    qnb = qn.astype(BF16)
    s2 = jnp.concatenate([_dot_nt(of_head(qnb, kh), k2[kh]) for kh in range(N_KV)], axis=0) * scale
    probs, p_sink = [], []
    for j in range(2):
        s = jnp.where(valid, s2[:, 2 * BLK * j:2 * BLK * (j + 1)], NEG)
        sink = jnp.zeros((nq, 1), F32)
        for p in range(4):
            sink = jnp.where(pair == p, sinks[:, 2 * p + j:2 * p + j + 1], sink)
        m = jnp.maximum(jnp.max(s, axis=-1, keepdims=True), sink)
        e = jnp.exp(s - m)
        e_sink = jnp.exp(sink - m)
        rden = 1.0 / (jnp.sum(e, axis=-1, keepdims=True) + e_sink)
        probs.append(e * rden)
        p_sink.append(e_sink * rden)
    pb = [p.astype(BF16) for p in probs]
    if do is None:
        return unstack(jnp.concatenate(
            [_dot(of_head(pb[0], kh), v_ops[kh][0]) + _dot(of_head(pb[1], kh), v_ops[kh][1]) for kh in range(N_KV)],
            axis=0))
    dob = stack(do).astype(BF16)
    ds, dsinks = [], [None] * 8
    for j in range(2):
        dp = jnp.concatenate([_dot_nt(of_head(dob, kh), v_ops[kh][j]) for kh in range(N_KV)], axis=0)
        t = jnp.sum(dp * probs[j], axis=-1, keepdims=True)
        ds.append(probs[j] * (dp - t) * scale)
        lost = p_sink[j] * t
        for p in range(4):
            dsinks[2 * p + j] = -jnp.sum(lost[BLK * p:BLK * (p + 1)])
    dsb = jnp.concatenate(ds, axis=1).astype(BF16)
    dqn = jnp.concatenate([_dot(of_head(dsb, kh), k2[kh]) for kh in range(N_KV)], axis=0)
    dq, dgq2 = _head_norm_bwd(dqn, qy, qinv, gq2)
    dk2 = [_dot_tn(of_head(dsb, kh), of_head(qnb, kh)) for kh in range(N_KV)]
    dv_ops = [[_dot_tn(of_head(pb[j], kh), of_head(dob, kh)) for j in range(2)] for kh in range(N_KV)]
    dkn = restore(tuple((d[:2 * BLK], d[2 * BLK:]) for d in dk2))
    dvv = restore(tuple(tuple(d) for d in dv_ops))
    dkk, dgk2 = _head_norm_bwd(dkn, ky, kinv, gk2)
    return unstack(dq), dkk, dvv, dgq2, dgk2, dsinks


SWA_GROUP = 2


def _swa_specs(nbs):
    grp = min(SWA_GROUP, nbs)
    rows = grp * BLK

    def tok(w):
        return pl.BlockSpec((rows, w), lambda i: (i, 0))

    halo = pl.BlockSpec((BLK, 128), lambda i: (jnp.maximum(i * grp - 1, 0), 0))
    vec = pl.BlockSpec((1, 128), lambda i: (0, 0))
    sk = pl.BlockSpec((1, 8), lambda i: (0, 0))
    return grp, tok, halo, vec, sk


def _swa_fwd(q, k, v, gq2, gk2, sinks, nbs, plans=()):
    T = q.shape[0]
    grp, tok, halo, vec, sk = _swa_specs(nbs)

    def body(q_ref, kh_ref, kc_ref, vh_ref, vc_ref, gq_ref, gk_ref, sk_ref, o_ref):
        seq_start = ((pl.program_id(0) * grp) % nbs) == 0
        for g in range(grp):
            rows = slice(g * BLK, (g + 1) * BLK)
            prev = slice((g - 1) * BLK, g * BLK)
            kk = jnp.concatenate([kh_ref[...] if g == 0 else kc_ref[prev, :], kc_ref[rows, :]], axis=0)
            vv = jnp.concatenate([vh_ref[...] if g == 0 else vc_ref[prev, :], vc_ref[rows, :]], axis=0)
            o_ref[rows, :] = _swa_block(q_ref[rows, :], kk, vv, gq_ref[...], gk_ref[...], sk_ref[...],
                                        seq_start if g == 0 else False)

    (out,), plan_outs = _call(
        body, name="swa_fwd", grid=(T // (grp * BLK),),
        in_specs=[tok(D_B), halo, tok(128), halo, tok(128), vec, vec, sk], out_specs=[tok(D_B)],
        out_shape=[jax.ShapeDtypeStruct((T, D_B), F32)], args=(q, k, k, v, v, gq2, gk2, sinks),
        semantics=("parallel",), plans=plans)
    return out, plan_outs


def _swa_bwd(q, k, v, gq2, gk2, sinks, do, nbs, plans=()):
    T = q.shape[0]
    grp, tok, halo, vec, sk = _swa_specs(nbs)
    steps = T // (grp * BLK)

    def body(q_ref, kh_ref, kc_ref, vh_ref, vc_ref, gq_ref, gk_ref, sk_ref, do_ref,
             dq_ref, dk_ref, dv_ref, dkh_ref, dvh_ref, dgq_ref, dgk_ref, dsk_ref):
        i = pl.program_id(0)
        seq_start = ((i * grp) % nbs) == 0

        @pl.when(i == 0)
        def _():
            for r in (dgq_ref, dgk_ref, dsk_ref):
                r[...] = jnp.zeros(r.shape, F32)

        res = []
        for g in range(grp):
            rows = slice(g * BLK, (g + 1) * BLK)
            prev = slice((g - 1) * BLK, g * BLK)
            kk = jnp.concatenate([kh_ref[...] if g == 0 else kc_ref[prev, :], kc_ref[rows, :]], axis=0)
            vv = jnp.concatenate([vh_ref[...] if g == 0 else vc_ref[prev, :], vc_ref[rows, :]], axis=0)
            res.append(_swa_block(q_ref[rows, :], kk, vv, gq_ref[...], gk_ref[...], sk_ref[...],
                                  seq_start if g == 0 else False, do=do_ref[rows, :]))
        lane = lax.broadcasted_iota(jnp.int32, (8, 128), 1)
        upd = jnp.zeros((8, 128), F32)
        for g, (dq, dkk, dvv, dgq2, dgk2, dsinks) in enumerate(res):
            rows = slice(g * BLK, (g + 1) * BLK)
            dq_ref[rows, :] = dq
            dk_ref[rows, :] = dkk[BLK:] + res[g + 1][1][:BLK] if g + 1 < grp else dkk[BLK:]
            dv_ref[rows, :] = dvv[BLK:] + res[g + 1][2][:BLK] if g + 1 < grp else dvv[BLK:]
            dgq_ref[...] += dgq2
            dgk_ref[...] += dgk2
            for h, d in enumerate(dsinks):
                upd = upd + jnp.where(lane == h, d, 0.0)
        dkh_ref[0] = res[0][1][:BLK]
        dvh_ref[0] = res[0][2][:BLK]
        dsk_ref[...] += upd

    one = pl.BlockSpec((1, BLK, 128), lambda i: (i, 0, 0))
    halo_shape = jax.ShapeDtypeStruct((steps, BLK, 128), F32)
    return _call(
        body, name="swa_bwd", grid=(steps,),
        in_specs=[tok(D_B), halo, tok(128), halo, tok(128), vec, vec, sk, tok(D_B)],
        out_specs=[tok(D_B), tok(128), tok(128), one, one, vec, vec, pl.BlockSpec((8, 128), lambda i: (0, 0))],
        out_shape=[jax.ShapeDtypeStruct((T, D_B), F32), jax.ShapeDtypeStruct((T, 128), F32),
                   jax.ShapeDtypeStruct((T, 128), F32), halo_shape, halo_shape, jax.ShapeDtypeStruct((1, 128), F32),
                   jax.ShapeDtypeStruct((1, 128), F32), jax.ShapeDtypeStruct((8, 128), F32)],
        args=(q, k, k, v, v, gq2, gk2, sinks, do), semantics=("arbitrary",), plans=plans)


def _swa_add_halo(dk, dk_halo):
    steps = dk_halo.shape[0]
    nxt = jnp.concatenate([dk_halo[1:], jnp.zeros_like(dk_halo[:1])], axis=0)[:, None]
    dk = dk.reshape(steps, -1, BLK, 128)
    return jnp.concatenate([dk[:, :-1], dk[:, -1:] + nxt], axis=1).reshape(-1, 128)


def _sgu_norm(av, g_ln, b_ln):
    t, th = _gelu(av)
    mu = jnp.mean(t, axis=-1, keepdims=True)
    tc = t - mu
    rstd = lax.rsqrt(jnp.mean(tc * tc, axis=-1, keepdims=True) + EPS)
    vhat = tc * rstd
    return vhat * g_ln + b_ln, vhat, rstd, th


def _masked_ws(ws_ref):
    tril = lax.broadcasted_iota(jnp.int32, (BLK, BLK), 0) >= lax.broadcasted_iota(jnp.int32, (BLK, BLK), 1)
    return [jnp.where(tril, ws_ref[g], 0.0).astype(BF16) for g in range(N_GRP)]


def _mix_fwd(au, av, gta, gtb, ob, h, mods, g_ln, b_ln, ws, bsb, wa, wb, wout, seq):
    T = h.shape[0]
    tm = _token_tile(seq)
    tps = seq // tm

    def body(au_ref, av_ref, gta_ref, gtb_ref, ob_ref, h_ref, m_ref, gl_ref, bl_ref, ws_ref, bs_ref,
             wa_ref, wb_ref, wo_ref, out_ref, vvb_s, z_s):
        u, _ = _gelu(au_ref[...])
        vv, _, _, _ = _sgu_norm(av_ref[...], gl_ref[...], bl_ref[...])
        vvb_s[...] = vv.astype(BF16)
        wsm = _masked_ws(ws_ref)
        for c in range(tm // BLK):
            rows = slice(c * BLK, (c + 1) * BLK)
            for g in range(N_GRP):
                cols = slice(g * BLK, (g + 1) * BLK)
                z_s[rows, cols] = _dot(wsm[g], vvb_s[rows, cols]) + bs_ref[g]
        ya = _dot_nt((u * z_s[...]).astype(BF16), wa_ref[...])
        yb = _dot_nt(ob_ref[...].astype(BF16), wb_ref[...])
        merged = jax.nn.sigmoid(gta_ref[...]) * ya + jax.nn.sigmoid(gtb_ref[...]) * yb
        out_ref[...] = h_ref[...] + m_ref[0, 5:6, :] * _dot(merged.astype(BF16), wo_ref[...])

    def tok(w):
        return pl.BlockSpec((tm, w), lambda i: (i, 0))

    def full(shape):
        return pl.BlockSpec(shape, lambda i: (0,) * len(shape))

    return pl.pallas_call(
        body, name="mix_fwd", grid=(T // tm,),
        in_specs=[tok(D_A), tok(D_A), tok(D), tok(D), tok(D_B), tok(D),
                  pl.BlockSpec((1, 9, D), lambda i: (i // tps, 0, 0)), full((1, D_A)), full((1, D_A)),
                  full((N_GRP, BLK, BLK)), full((N_GRP, BLK, BLK)), _vmem(), _vmem(), _vmem()],
        out_specs=tok(D), out_shape=jax.ShapeDtypeStruct((T, D), F32),
        scratch_shapes=[pltpu.VMEM((tm, D_A), BF16), pltpu.VMEM((tm, D_A), F32)],
        compiler_params=_params("parallel"),
    )(au, av, gta, gtb, ob, h, mods, g_ln, b_ln, ws, bsb, wa, wb, wout)


def _mix_bwd(au, av, gta, gtb, ob, dh, mods, g_ln, b_ln, ws, bsb, wa, wb, wout, seq, plans=()):
    T = dh.shape[0]
    B = T // seq
    tm = _token_tile(seq, 256)
    tps = seq // tm

    def body(au_ref, av_ref, gta_ref, gtb_ref, ob_ref, dh_ref, m_ref, gl_ref, bl_ref, ws_ref, bs_ref,
             wa_ref, wb_ref, wo_ref,
             dau_ref, dav_ref, dgta_ref, dgtb_ref, dob_ref, dwo_ref, dwa_ref, dwb_ref, dws_ref, dbs_ref, dln_ref,
             mg_ref, vvb_s, z_s, dz_s, dzb_s, dvv_s):
        i = pl.program_id(0)

        @pl.when(i == 0)
        def _():
            for r in (dwo_ref, dwa_ref, dwb_ref, dws_ref, dbs_ref, dln_ref):
                r[...] = jnp.zeros(r.shape, F32)

        @pl.when(i % tps == 0)
        def _():
            mg_ref[...] = jnp.zeros(mg_ref.shape, F32)

        auv = au_ref[...]
        avv = av_ref[...]
        u, thu = _gelu(auv)
        g_ln = gl_ref[...]
        vv, vhat, rstd, thv = _sgu_norm(avv, g_ln, bl_ref[...])
        vvb_s[...] = vv.astype(BF16)
        wsm = _masked_ws(ws_ref)
        for c in range(tm // BLK):
            rows = slice(c * BLK, (c + 1) * BLK)
            for g in range(N_GRP):
                cols = slice(g * BLK, (g + 1) * BLK)
                z_s[rows, cols] = _dot(wsm[g], vvb_s[rows, cols]) + bs_ref[g]
        z = z_s[...]
        yab = (u * z).astype(BF16)
        obb = ob_ref[...].astype(BF16)
        ya = _dot_nt(yab, wa_ref[...])
        yb = _dot_nt(obb, wb_ref[...])
        sa = jax.nn.sigmoid(gta_ref[...])
        sb = jax.nn.sigmoid(gtb_ref[...])
        mb = (sa * ya + sb * yb).astype(BF16)
        dhv = dh_ref[...]
        mg_ref[0, 0:1, :] += jnp.sum(dhv * _dot(mb, wo_ref[...]), axis=0, keepdims=True)
        dmob = (m_ref[0, 5:6, :] * dhv).astype(BF16)
        dwo_ref[...] += _dot_tn(mb, dmob)
        dmerged = _dot_nt(dmob, wo_ref[...])
        dya = dmerged * sa
        dyb = dmerged * sb
        dgta_ref[...] = dya * ya * (1.0 - sa)
        dgtb_ref[...] = dyb * yb * (1.0 - sb)
        dyab = dya.astype(BF16)
        dybb = dyb.astype(BF16)
        dwa_ref[...] += _dot_tn(dyab, yab)
        dwb_ref[...] += _dot_tn(dybb, obb)
        dob_ref[...] = _dot(dybb, wb_ref[...])
        dyap = _dot(dyab, wa_ref[...])
        dau_ref[...] = (dyap * z) * _gelu_grad(auv, thu)
        dz = dyap * u
        dz_s[...] = dz
        dzb_s[...] = dz.astype(BF16)
        for c in range(tm // BLK):
            rows = slice(c * BLK, (c + 1) * BLK)
            for g in range(N_GRP):
                cols = slice(g * BLK, (g + 1) * BLK)
                dzb = dzb_s[rows, cols]
                dvv_s[rows, cols] = _dot_tn(wsm[g], dzb)
                dws_ref[g] += _dot_nt(dzb, vvb_s[rows, cols])
                dbs_ref[g] += dz_s[rows, cols]
        dvv = dvv_s[...]
        dln_ref[0:1, :] += jnp.sum(dvv * vhat, axis=0, keepdims=True)
        dln_ref[1:2, :] += jnp.sum(dvv, axis=0, keepdims=True)
        dvh = dvv * g_ln
        dt = rstd * (dvh - jnp.mean(dvh, axis=-1, keepdims=True)
                     - vhat * jnp.mean(dvh * vhat, axis=-1, keepdims=True))
        dav_ref[...] = dt * _gelu_grad(avv, thv)

    def tok(w):
        return pl.BlockSpec((tm, w), lambda i: (i, 0))

    def full(shape):
        return pl.BlockSpec(shape, lambda i: (0,) * len(shape))

    return _call(
        body, name="mix_bwd", grid=(T // tm,),
        in_specs=[tok(D_A), tok(D_A), tok(D), tok(D), tok(D_B), tok(D),
                  pl.BlockSpec((1, 9, D), lambda i: (i // tps, 0, 0)), full((1, D_A)), full((1, D_A)),
                  full((N_GRP, BLK, BLK)), full((N_GRP, BLK, BLK)), _vmem(), _vmem(), _vmem()],
        out_specs=[tok(D_A), tok(D_A), tok(D), tok(D), tok(D_B), full((D, D)), full((D, D_A)), full((D, D_B)),
                   full((N_GRP, BLK, BLK)), full((N_GRP, BLK, BLK)), full((8, D_A)),
                   pl.BlockSpec((1, 8, D), lambda i: (i // tps, 0, 0))],
        out_shape=[jax.ShapeDtypeStruct((T, D_A), F32), jax.ShapeDtypeStruct((T, D_A), F32),
                   jax.ShapeDtypeStruct((T, D), F32), jax.ShapeDtypeStruct((T, D), F32),
                   jax.ShapeDtypeStruct((T, D_B), F32), jax.ShapeDtypeStruct((D, D), F32),
                   jax.ShapeDtypeStruct((D, D_A), F32), jax.ShapeDtypeStruct((D, D_B), F32),
                   jax.ShapeDtypeStruct((N_GRP, BLK, BLK), F32), jax.ShapeDtypeStruct((N_GRP, BLK, BLK), F32),
                   jax.ShapeDtypeStruct((8, D_A), F32), jax.ShapeDtypeStruct((B, 8, D), F32)],
        scratch_shapes=[pltpu.VMEM((tm, D_A), BF16), pltpu.VMEM((tm, D_A), F32), pltpu.VMEM((tm, D_A), F32),
                        pltpu.VMEM((tm, D_A), BF16), pltpu.VMEM((tm, D_A), F32)],
        args=(au, av, gta, gtb, ob, dh, mods, g_ln, b_ln, ws, bsb, wa, wb, wout), semantics=("arbitrary",),
        plans=plans)


def _ada_fwd(c_all, w_ada):
    def body(c_ref, w_ref, o_ref):
        cv = c_ref[...]
        o_ref[...] = _dot((cv * jax.nn.sigmoid(cv)).astype(BF16), w_ref[...].astype(BF16))

    out = jax.ShapeDtypeStruct((c_all.shape[0], w_ada.shape[1]), F32)
    return pl.pallas_call(body, name="ada_fwd", grid=(1,), in_specs=[_whole(c_all), _whole(w_ada)],
                          out_specs=_whole(out), out_shape=out, compiler_params=_params("arbitrary"))(c_all, w_ada)


def _ada_update(c_all, dm_cols, w, m, v, plans=()):
    n, cols = c_all.shape[0], w.shape[1]

    def body(c_ref, dm_ref, w_ref, m_ref, v_ref, g_ref, d_ref, nm_ref, nv_ref):
        cv = c_ref[...]
        g = _dot_tn((cv * jax.nn.sigmoid(cv)).astype(BF16), dm_ref[...].astype(BF16))
        g_ref[...] = g
        d_ref[...], nm_ref[...], nv_ref[...] = _adamw(w_ref[...], g, m_ref[...], v_ref[...])

    col = pl.BlockSpec((D, 128), lambda j: (0, j))
    return _call(
        body, name="ada_update", grid=(cols // 128,),
        in_specs=[pl.BlockSpec((n, D), lambda j: (0, 0)), pl.BlockSpec((n, 128), lambda j: (0, j)), col, col, col],
        out_specs=[col] * 4, out_shape=[jax.ShapeDtypeStruct(w.shape, F32)] * 4, args=(c_all, dm_cols, w, m, v),
        semantics=("parallel",), plans=plans)


def _mod_finish(mg1, mg2, mg2g, mg3, mods, g1, g2, g3):
    B = mods.shape[0]

    def body(mg1_ref, mg2_ref, mg2g_ref, mg3_ref, m_ref, g1_ref, g2_ref, g3_ref, dm_ref, dgn_ref):
        dgn_ref[...] = jnp.zeros(dgn_ref.shape, F32)
        for k, (mg, g_ref) in enumerate(((mg1_ref, g1_ref), (mg2_ref, g2_ref), (mg3_ref, g3_ref))):
            for b in range(B):
                s_dxr = mg[b, 1:2, :]
                dm_ref[b, 3 * k:3 * k + 1, :] = mg[b, 0:1, :]
                dm_ref[b, 3 * k + 1:3 * k + 2, :] = g_ref[...] * s_dxr
                dm_ref[b, 3 * k + 2:3 * k + 3, :] = mg2g_ref[b, 0:1, :] if k == 1 else mg[b, 2:3, :]
                dgn_ref[k:k + 1, :] += (1.0 + m_ref[b, 3 * k + 1:3 * k + 2, :]) * s_dxr

    args = (mg1, mg2, mg2g, mg3, mods, g1, g2, g3)
    out_shape = [jax.ShapeDtypeStruct((B, 9, D), F32), jax.ShapeDtypeStruct((8, D), F32)]
    return pl.pallas_call(body, name="mod_finish", grid=(1,), in_specs=[_whole(a) for a in args],
                          out_specs=[_whole(o) for o in out_shape], out_shape=out_shape,
                          compiler_params=_params("arbitrary"))(*args)


def _small_update(gathered, params, ms, vs):
    n = len(SMALL)
    B = gathered[0].shape[1]

    def body(*refs):
        gdm, ggn, gln, gws, gbs, ggq, ggk, gsk = refs[:8]
        w = dict(zip(SMALL, refs[8:8 + n]))
        m = dict(zip(SMALL, refs[8 + n:8 + 2 * n]))
        v = dict(zip(SMALL, refs[8 + 2 * n:8 + 3 * n]))
        outs = refs[8 + 3 * n:]
        out = {name: outs[4 * k:4 * k + 4] for k, name in enumerate(SMALL)}

        def total(ref, idx):
            acc = ref[(0,) + idx]
            for dev in range(1, N_DEV):
                acc = acc + ref[(dev,) + idx]
            return acc

        def finish(name, g, idx=(Ellipsis,)):
            d, nm, nv = _adamw(w[name][idx], g, m[name][idx], v[name][idx])
            for ref, val in zip(out[name], (g, d, nm, nv)):
                ref[idx] = val

        g_bada = total(gdm, (slice(0, 1),))
        for b in range(1, B):
            g_bada = g_bada + total(gdm, (slice(b, b + 1),))
        finish("b_ada", g_bada)
        finish("g_norm1", total(ggn, (slice(0, 1),)))
        finish("g_norm2", total(ggn, (slice(1, 2),)))
        finish("g_norm3", total(ggn, (slice(2, 3),)))
        finish("g_sgu_ln", total(gln, (slice(0, 1),)))
        finish("b_sgu_ln", total(gln, (slice(1, 2),)))
        tril = lax.broadcasted_iota(jnp.int32, (BLK, BLK), 0) >= lax.broadcasted_iota(jnp.int32, (BLK, BLK), 1)
        for g in range(N_GRP):
            finish("w_spatial", jnp.where(tril, total(gws, (g,)), 0.0), (0, g))
            finish("b_spatial", jnp.sum(total(gbs, (g,)).T, axis=0, keepdims=True), (0, slice(g, g + 1)))
        finish("g_q", total(ggq, ()))
        finish("g_k", total(ggk, ()))
        finish("attn_sinks", total(gsk, (slice(0, 1), slice(0, N_KV * Q_PER_KV))))

    args = list(gathered) + [params[k] for k in SMALL] + [ms[k] for k in SMALL] + [vs[k] for k in SMALL]
    out_shape = []
    for k in SMALL:
        out_shape += [jax.ShapeDtypeStruct(params[k].shape, F32)] * 4
    res = pl.pallas_call(body, name="small_update", grid=(1,), in_specs=[_whole(a) for a in args],
                         out_specs=[_whole(o) for o in out_shape], out_shape=out_shape,
                         compiler_params=_params("arbitrary"))(*args)
    return {k: res[4 * i:4 * i + 4] for i, k in enumerate(SMALL)}


def _rs_add(p, r, core, name):
    _, rows, width = p.shape

    def body(c_ref, p_ref, r_ref, o_ref):
        o_ref[...] = (p_ref[...] + r_ref[...]).astype(BF16)

    return pl.pallas_call(
        body, name="rs_add_" + name, out_shape=jax.ShapeDtypeStruct((4, rows, width), BF16),
        grid_spec=pltpu.PrefetchScalarGridSpec(
            num_scalar_prefetch=1, grid=(4,),
            in_specs=[pl.BlockSpec((1, rows, width), lambda k, c_ref: (2 * k + c_ref[0], 0, 0)),
                      pl.BlockSpec((1, rows, width), lambda k, c_ref: (k, 0, 0))],
            out_specs=pl.BlockSpec((1, rows, width), lambda k, c_ref: (k, 0, 0))),
        compiler_params=_params("parallel"),
    )(core, p, r)


def _adam_big(w, m, v, r, name, transposed):
    _, rows, cols = w.shape
    br = 256 if rows % 256 == 0 else rows
    pad = -cols % 128

    def body(r_ref, w_ref, m_ref, v_ref, go_ref, d_ref, nm_ref, nv_ref, *scratch):
        g = r_ref[0].astype(F32)
        for k in range(1, 4):
            g = g + r_ref[k].astype(F32)
        if transposed:
            if pad:
                g = jnp.concatenate([g, jnp.zeros((pad, br), F32)], axis=0)
            scratch[0][...] = g.T
            g = scratch[0][:, 0:cols]
        go_ref[0] = g
        d_ref[0], nm_ref[0], nv_ref[0] = _adamw(w_ref[0], g, m_ref[0], v_ref[0])

    shard = pl.BlockSpec((1, br, cols), lambda i: (0, i, 0))
    parts = pl.BlockSpec((4, cols, br), lambda i: (0, 0, i)) if transposed else pl.BlockSpec((4, br, cols),
                                                                                           lambda i: (0, i, 0))
    return pl.pallas_call(body, name="adam_" + name, grid=(rows // br,), in_specs=[parts, shard, shard, shard],
                          out_specs=[shard] * 4, out_shape=[jax.ShapeDtypeStruct(w.shape, F32)] * 4,
                          scratch_shapes=[pltpu.VMEM((br, cols + pad), F32)] if transposed else [],
                          compiler_params=_params("parallel"))(r, w, m, v)


def kernel(x, c, w_ada, b_ada, g_norm1, ffn1_w_gate, ffn1_w_up, ffn1_w_down, g_norm2, w_in, g_sgu_ln, b_sgu_ln, w_spatial, b_spatial, g_q, g_k, attn_sinks, w_branch_a, w_branch_b, w_out, g_norm3, ffn2_w_gate, ffn2_w_up, ffn2_w_down, loss_target, m_w_ada, m_b_ada, m_g_norm1, m_ffn1_w_gate, m_ffn1_w_up, m_ffn1_w_down, m_g_norm2, m_w_in, m_g_sgu_ln, m_b_sgu_ln, m_w_spatial, m_b_spatial, m_g_q, m_g_k, m_attn_sinks, m_w_branch_a, m_w_branch_b, m_w_out, m_g_norm3, m_ffn2_w_gate, m_ffn2_w_up, m_ffn2_w_down, v_w_ada, v_b_ada, v_g_norm1, v_ffn1_w_gate, v_ffn1_w_up, v_ffn1_w_down, v_g_norm2, v_w_in, v_g_sgu_ln, v_b_sgu_ln, v_w_spatial, v_b_spatial, v_g_q, v_g_k, v_attn_sinks, v_w_branch_a, v_w_branch_b, v_w_out, v_g_norm3, v_ffn2_w_gate, v_ffn2_w_up, v_ffn2_w_down):
    names = ("w_ada", "b_ada", "g_norm1", "ffn1_w_gate", "ffn1_w_up", "ffn1_w_down", "g_norm2", "w_in", "g_sgu_ln",
             "b_sgu_ln", "w_spatial", "b_spatial", "g_q", "g_k", "attn_sinks", "w_branch_a", "w_branch_b", "w_out",
             "g_norm3", "ffn2_w_gate", "ffn2_w_up", "ffn2_w_down")
    w = dict(zip(names, (w_ada, b_ada, g_norm1, ffn1_w_gate, ffn1_w_up, ffn1_w_down, g_norm2, w_in, g_sgu_ln,
                         b_sgu_ln, w_spatial, b_spatial, g_q, g_k, attn_sinks, w_branch_a, w_branch_b, w_out, g_norm3,
                         ffn2_w_gate, ffn2_w_up, ffn2_w_down)))
    m = dict(zip(names, (m_w_ada, m_b_ada, m_g_norm1, m_ffn1_w_gate, m_ffn1_w_up, m_ffn1_w_down, m_g_norm2, m_w_in,
                         m_g_sgu_ln, m_b_sgu_ln, m_w_spatial, m_b_spatial, m_g_q, m_g_k, m_attn_sinks, m_w_branch_a,
                         m_w_branch_b, m_w_out, m_g_norm3, m_ffn2_w_gate, m_ffn2_w_up, m_ffn2_w_down)))
    v = dict(zip(names, (v_w_ada, v_b_ada, v_g_norm1, v_ffn1_w_gate, v_ffn1_w_up, v_ffn1_w_down, v_g_norm2, v_w_in,
                         v_g_sgu_ln, v_b_sgu_ln, v_w_spatial, v_b_spatial, v_g_q, v_g_k, v_attn_sinks, v_w_branch_a,
                         v_w_branch_b, v_w_out, v_g_norm3, v_ffn2_w_gate, v_ffn2_w_up, v_ffn2_w_down)))
    B, seq, _ = x.shape
    T = B * seq
    nbs = seq // BLK
    xi, yi, ci = _place()
    me = 4 * xi + 2 * yi + ci
    core = jnp.reshape(ci, (1,)).astype(jnp.int32)
    layout = {name: (tform, n, width) for name, tform, n, width in BIG}
    shard = {name: w[name][0].astype(BF16).T if tform else w[name][0].astype(BF16) for name, tform, _, _ in BIG}
    wts, parts, big_out = {}, {}, {}

    def gather_plan(group):
        return _Gather([shard[k] for k in group])

    def take(group, gathered):
        for k, g in zip(group, gathered):
            wts[k] = g.reshape(N_DEV * layout[k][1], layout[k][2])

    def blocks(name, grad):
        return grad.reshape(N_DEV, layout[name][1], layout[name][2])

    def to_sibling(names, grads):
        return _RsSibling([blocks(k, g) for k, g in zip(names, grads)])

    def add(names, grads, from_sibling):
        for k, g, r in zip(names, grads, from_sibling):
            parts[k] = _rs_add(blocks(k, g), r, core, k)

    def to_chips(names):
        return _RsChips([parts[k] for k in names])

    def update(names, from_chips):
        for k, r in zip(names, from_chips):
            big_out[k] = _adam_big(w[k], m[k], v[k], r, k, transposed=layout[k][0])

    g1, g2, g3, g_ln, b_ln = g_norm1, g_norm2, g_norm3, g_sgu_ln, b_sgu_ln
    gq2, gk2 = jnp.tile(g_q, (1, 2)), jnp.tile(g_k, (1, 2))
    ws = w_spatial[0]
    bsb = jnp.broadcast_to(b_spatial[0][:, :, None], (N_GRP, BLK, BLK))
    xf = x.reshape(T, D)
    tgt = loss_target.reshape(T, D)

    ((c_all,),) = _exchange([_Gather([c])], "gather_c")
    c_all = c_all.reshape(N_DEV * B, D)
    ffn1 = ("ffn1_w_gate", "ffn1_w_up", "ffn1_w_down")
    ffn2 = ("ffn2_w_gate", "ffn2_w_up", "ffn2_w_down")
    ((mods_cols, *gathered),) = _exchange(
        [_Gather([_ada_fwd(c_all, w_ada[0])] + [shard[k] for k in ffn1])], "gather_first")
    take(ffn1, gathered)
    mine = lax.dynamic_slice_in_dim(mods_cols, B * me, B, axis=1)
    mods = (mine.transpose(1, 0, 2).reshape(B, 9 * D) + b_ada).reshape(B, 9, D)

    group = ("w_in", "w_branch_a", "w_branch_b", "w_out", "ffn2_w_gate")
    (h1, y1, gb1, ub1), (gathered,) = _ffn_fwd(xf, mods, g1, *[wts[k] for k in ffn1], 0, seq,
                                               plans=[gather_plan(group)])
    take(group, gathered)
    au, av, q_tok, k_tok, v_tok, gta, gtb = _inproj_fwd(h1, mods, g2, wts["w_in"], seq)
    group = ("ffn2_w_up", "ffn2_w_down")
    ob, (gathered,) = _swa_fwd(q_tok, k_tok, v_tok, gq2, gk2, attn_sinks, nbs, plans=[gather_plan(group)])
    take(group, gathered)
    mixw = (wts["w_branch_a"], wts["w_branch_b"], wts["w_out"])
    h2 = _mix_fwd(au, av, gta, gtb, ob, h1, mods, g_ln, b_ln, ws, bsb, *mixw, seq)
    (dh3, y3, gb3, ub3, loss_parts), _ = _ffn_fwd(h2, mods, g3, *[wts[k] for k in ffn2], 6, seq, tgt=tgt)
    loss = lax.psum(jnp.sum(loss_parts[:, 0, 0]), AXES)

    (dh2, xb, dyb, a, dg, du, mg3), _ = _ffn_bwd(h2, dh3, y3, gb3, ub3, mods, g3, *[wts[k] for k in ffn2], 6, seq)
    d_gate, _ = _wgrad(dg, xb, "wgrad_ffn2_gate")
    d_up, (r,) = _wgrad(du, xb, "wgrad_ffn2_up", plans=[to_sibling(ffn2[:1], [d_gate])])
    add(ffn2[:1], [d_gate], r)
    d_down, (r,) = _wgrad(a, dyb, "wgrad_ffn2_down", plans=[to_sibling(ffn2[1:2], [d_up])])
    add(ffn2[1:2], [d_up], r)

    (dau, dav, dgta, dgtb, dob, d_out, d_a, d_b, dws, dbs, dln, mg2g), (r, from_chips) = _mix_bwd(
        au, av, gta, gtb, ob, dh2, mods, g_ln, b_ln, ws, bsb, *mixw, seq,
        plans=[to_sibling(ffn2[2:], [d_down]), to_chips(ffn2[:2])])
    add(ffn2[2:], [d_down], r)
    update(ffn2[:2], from_chips)
    mixers = ("w_out", "w_branch_a", "w_branch_b")
    (dq, dk, dv, dk_halo, dv_halo, dgq2, dgk2, dsk), (from_chips, r) = _swa_bwd(
        q_tok, k_tok, v_tok, gq2, gk2, attn_sinks, dob, nbs,
        plans=[to_chips(ffn2[2:]), to_sibling(mixers, [d_out, d_a, d_b])])
    update(ffn2[2:], from_chips)
    add(mixers, [d_out, d_a, d_b], r)
    dk, dv = _swa_add_halo(dk, dk_halo), _swa_add_halo(dv, dv_halo)
    dgq, dgk = dgq2[:, :HD] + dgq2[:, HD:], dgk2[:, :HD] + dgk2[:, HD:]
    (dh1, xb2, dpb, mg2), (from_chips,) = _inproj_bwd(
        h1, dh2, (dau, dav, dq, dk, dv, dgta, dgtb), mods, g2, wts["w_in"], seq, plans=[to_chips(mixers)])
    update(mixers, from_chips)

    (dx, xb, dyb, a, dg, du, mg1), (early,) = _ffn_bwd(
        xf, dh1, y1, gb1, ub1, mods, g1, *[wts[k] for k in ffn1], 0, seq,
        plans=[_Gather([dln, dws, dbs, dgq, dgk, dsk])])
    dmods, dgn = _mod_finish(mg1, mg2, mg2g, mg3, mods, g1, g2, g3)
    d_gate, (late,) = _wgrad(dg, xb, "wgrad_ffn1_gate", plans=[_Gather([dmods.reshape(B, 9 * D), dgn])])
    gathered = late + early
    d_up, (r,) = _wgrad(du, xb, "wgrad_ffn1_up", plans=[to_sibling(ffn1[:1], [d_gate])])
    add(ffn1[:1], [d_gate], r)
    d_down, (r, from_chips) = _wgrad(a, dyb, "wgrad_ffn1_down",
                                     plans=[to_sibling(ffn1[1:2], [d_up]), to_chips(ffn1[:1])])
    add(ffn1[1:2], [d_up], r)
    update(ffn1[:1], from_chips)
    d_in, (r, from_chips) = _wgrad(dpb, xb2, "wgrad_w_in",
                                   plans=[to_sibling(ffn1[2:], [d_down]), to_chips(ffn1[1:2])])
    add(ffn1[2:], [d_down], r)
    update(ffn1[1:2], from_chips)
    small_out = _small_update(gathered, w, m, v)
    dm_cols = lax.dynamic_slice_in_dim(gathered[0].reshape(N_DEV * B, 9 * D), (9 * D // N_DEV) * me,
                                       9 * D // N_DEV, axis=1)
    ada_out, (r, from_chips) = _ada_update(c_all, dm_cols, w_ada[0], m_w_ada[0], v_w_ada[0],
                                           plans=[to_sibling(("w_in",), [d_in]), to_chips(ffn1[2:])])
    add(("w_in",), [d_in], r)
    update(ffn1[2:], from_chips)
    (from_chips,) = _exchange([to_chips(("w_in",))], "rs_last")
    update(("w_in",), from_chips)

    def leaf(kind, name):
        if name == "w_ada":
            return ada_out[kind][None]
        if name in SMALL:
            return small_out[name][kind]
        return big_out[name][kind]

    return (loss, dx.reshape(B, seq, D), *[leaf(kind, name) for kind in range(4) for name in names])
```

```python
import functools
import math

import jax
import jax.numpy as jnp
from jax import lax
from jax.experimental import pallas as pl
from jax.experimental.pallas import tpu as pltpu

F32 = jnp.float32
BF16 = jnp.bfloat16
MESH = pl.DeviceIdType.MESH
AXES = ("x", "y", "c")
N_DEV = 8

VMEM_LIMIT = 56 * 1024 * 1024

D = 1024
FF = 2816
FC = 1408
D_A = 512
D_B = 512
HD = 64
N_KV = 2
Q_PER_KV = 4
BLK = 128
N_GRP = 4
IN_COLS = 3840
PIECES = (("au", 0, 512), ("av", 512, 512), ("q", 1024, 512), ("k", 1536, 128), ("v", 1664, 128),
          ("ga", 1792, 1024), ("gb", 2816, 1024))
EPS = 1e-6
NEG = -1e30
GELU_C = math.sqrt(2.0 / math.pi)

ADAM_LR = 0.001
ADAM_B1 = 0.9
ADAM_B2 = 0.999
ADAM_EPS = 1e-08
ADAM_WD = 0.01
ADAM_STEP = 10

NT = (((1,), (1,)), ((), ()))
TN = (((0,), (0,)), ((), ()))

BIG = (("ffn1_w_gate", True, FF // N_DEV, D), ("ffn1_w_up", True, FF // N_DEV, D),
       ("ffn1_w_down", False, FF // N_DEV, D), ("w_in", True, IN_COLS // N_DEV, D),
       ("w_branch_a", True, D // N_DEV, D_A), ("w_branch_b", True, D // N_DEV, D_B), ("w_out", False, D // N_DEV, D),
       ("ffn2_w_gate", True, FF // N_DEV, D), ("ffn2_w_up", True, FF // N_DEV, D),
       ("ffn2_w_down", False, FF // N_DEV, D))
SMALL = ("b_ada", "g_norm1", "g_norm2", "g_sgu_ln", "b_sgu_ln", "w_spatial", "b_spatial", "g_q", "g_k",
         "attn_sinks", "g_norm3")


def _dot(a, b):
    return jnp.dot(a, b, preferred_element_type=F32)


def _dot_nt(a, b):
    return lax.dot_general(a, b, NT, preferred_element_type=F32)


def _dot_tn(a, b):
    return lax.dot_general(a, b, TN, preferred_element_type=F32)


def _vmem():
    return pl.BlockSpec(memory_space=pltpu.VMEM)


def _any():
    return pl.BlockSpec(memory_space=pl.ANY)


def _whole(a):
    return pl.BlockSpec(a.shape, lambda i: (0,) * len(a.shape))


def _rms_mod(h, g, sh, sc):
    inv = lax.rsqrt(jnp.mean(h * h, axis=-1, keepdims=True) + EPS)
    r = h * inv
    return (r * g) * (1.0 + sc) + sh, r, inv


def _rms_mod_bwd(dxn, r, inv, g, sc):
    dr = dxn * (g * (1.0 + sc))
    dh = inv * (dr - r * jnp.mean(dr * r, axis=-1, keepdims=True))
    return dh, jnp.sum(dxn, axis=0, keepdims=True), jnp.sum(dxn * r, axis=0, keepdims=True)


def _gelu(x):
    t = jnp.tanh(GELU_C * (x + 0.044715 * (x * x * x)))
    return 0.5 * x * (1.0 + t), t


def _gelu_grad(x, t):
    return 0.5 * (1.0 + t) + 0.5 * x * (1.0 - t * t) * (GELU_C * (1.0 + 3.0 * 0.044715 * x * x))


def _adamw(w, g, m, v):
    m = ADAM_B1 * m + (1.0 - ADAM_B1) * g
    v = ADAM_B2 * v + (1.0 - ADAM_B2) * (g * g)
    m_hat = m / (1.0 - ADAM_B1 ** ADAM_STEP)
    v_hat = v / (1.0 - ADAM_B2 ** ADAM_STEP)
    delta = -ADAM_LR * (m_hat / (jnp.sqrt(v_hat) + ADAM_EPS) + ADAM_WD * w)
    return delta, m, v


def _token_tile(seq, cap=512):
    return min(cap, seq)


def _params(*semantics):
    return pltpu.CompilerParams(dimension_semantics=semantics, vmem_limit_bytes=VMEM_LIMIT)


def _place():
    return lax.axis_index("x"), lax.axis_index("y"), lax.axis_index("c")


class _Gather:
    def __init__(self, arrays):
        n = len(arrays)
        self.ins = list(arrays)
        self.out_shape = [jax.ShapeDtypeStruct((N_DEV,) + a.shape, a.dtype) for a in arrays]
        self.scratch = [pltpu.SemaphoreType.DMA((n, 7)), pltpu.SemaphoreType.DMA((n, 7)),
                        pltpu.SemaphoreType.DMA((n,))]

    def _copies(self, ins, outs, sems):
        send_sems, recv_sems, local_sems = sems
        n = len(ins)
        x, y, c = _place()
        me, sibling = (x, y, c), (x, y, 1 - c)
        chips = [(1 - x, y), (x, 1 - y), (1 - x, 1 - y)]

        def slot(a, px, py, pc):
            return outs[a].at[4 * px + 2 * py + pc]

        def copy(a, k, block, to, src=None):
            return pltpu.make_async_remote_copy(
                src_ref=slot(a, *block) if src is None else src, dst_ref=slot(a, *block),
                send_sem=send_sems.at[a, k], recv_sem=recv_sems.at[a, k], device_id=to, device_id_type=MESH)

        mine = [pltpu.make_async_copy(ins[a], slot(a, *me), local_sems.at[a]) for a in range(n)]
        first = [copy(a, 0, me, sibling, src=ins[a]) for a in range(n)]
        first += [copy(a, 1 + j, me, (*chip, c), src=ins[a]) for a in range(n) for j, chip in enumerate(chips)]
        landed = [[copy(a, 1 + j, (*chip, c), me) for a in range(n)] for j, chip in enumerate(chips)]
        passed = [[copy(a, 4 + j, (*chip, c), sibling) for a in range(n)] for j, chip in enumerate(chips)]
        from_sibling = [copy(a, 0, sibling, me) for a in range(n)]
        from_sibling += [copy(a, 4 + j, (*chip, 1 - c), me) for a in range(n) for j, chip in enumerate(chips)]
        return mine, first, landed, passed, from_sibling

    def start(self, ins, outs, sems):
        mine, first, _, _, _ = self._copies(ins, outs, sems)
        for cp in mine + first:
            cp.start()

    def finish(self, ins, outs, sems):
        mine, first, landed, passed, from_sibling = self._copies(ins, outs, sems)
        for arrivals, forwards in zip(landed, passed):
            for arrival, forward in zip(arrivals, forwards):
                arrival.wait_recv()
                forward.start()
        for cp in from_sibling:
            cp.wait_recv()
        for cp in first + [f for fs in passed for f in fs]:
            cp.wait_send()
        for cp in mine:
            cp.wait()


class _RsSibling:
    def __init__(self, ps):
        n = len(ps)
        self.ins = list(ps)
        self.out_shape = [jax.ShapeDtypeStruct((4,) + p.shape[1:], p.dtype) for p in ps]
        self.scratch = [pltpu.SemaphoreType.DMA((n, 4)), pltpu.SemaphoreType.DMA((n, 4))]

    def _copies(self, ins, outs, sems):
        send_sems, recv_sems = sems
        x, y, c = _place()
        return [pltpu.make_async_remote_copy(
            src_ref=ins[a].at[2 * q + (1 - c)], dst_ref=outs[a].at[q], send_sem=send_sems.at[a, q],
            recv_sem=recv_sems.at[a, q], device_id=(x, y, 1 - c), device_id_type=MESH)
            for a in range(len(ins)) for q in range(4)]

    def start(self, ins, outs, sems):
        for cp in self._copies(ins, outs, sems):
            cp.start()

    def finish(self, ins, outs, sems):
        for cp in self._copies(ins, outs, sems):
            cp.wait()


class _RsChips:
    def __init__(self, qs):
        n = len(qs)
        self.ins = list(qs)
        self.out_shape = [jax.ShapeDtypeStruct(q.shape, q.dtype) for q in qs]
        self.scratch = [pltpu.SemaphoreType.DMA((n, 3)), pltpu.SemaphoreType.DMA((n, 3)),
                        pltpu.SemaphoreType.DMA((n,))]

    def _copies(self, ins, outs, sems):
        send_sems, recv_sems, local_sems = sems
        n = len(ins)
        x, y, c = _place()
        my_chip = 2 * x + y
        chips = [(1 - x, y), (x, 1 - y), (1 - x, 1 - y)]

        def copy(a, j, src_slot, dst_slot):
            px, py = chips[j]
            return pltpu.make_async_remote_copy(
                src_ref=ins[a].at[src_slot], dst_ref=outs[a].at[dst_slot], send_sem=send_sems.at[a, j],
                recv_sem=recv_sems.at[a, j], device_id=(px, py, c), device_id_type=MESH)

        own = [pltpu.make_async_copy(ins[a].at[my_chip], outs[a].at[my_chip], local_sems.at[a]) for a in range(n)]
        sends = [copy(a, j, 2 * px + py, my_chip) for a in range(n) for j, (px, py) in enumerate(chips)]
        arrivals = [copy(a, j, my_chip, 2 * px + py) for a in range(n) for j, (px, py) in enumerate(chips)]
        return own, sends, arrivals

    def start(self, ins, outs, sems):
        own, sends, _ = self._copies(ins, outs, sems)
        for cp in own + sends:
            cp.start()

    def finish(self, ins, outs, sems):
        own, sends, arrivals = self._copies(ins, outs, sems)
        for cp in arrivals:
            cp.wait_recv()
        for cp in sends:
            cp.wait_send()
        for cp in own:
            cp.wait()


def _split_plans(plans, refs_in, refs_out, refs_scr, phase):
    i = o = s = 0
    for p in plans:
        ni, no, ns = len(p.ins), len(p.out_shape), len(p.scratch)
        getattr(p, phase)(refs_in[i:i + ni], refs_out[o:o + no], refs_scr[s:s + ns])
        i, o, s = i + ni, o + no, s + ns


def _plan_results(plans, res):
    out = []
    for p in plans:
        out.append(list(res[:len(p.out_shape)]))
        res = res[len(p.out_shape):]
    return out


def _exchange(plans, name):
    c_in = [a for p in plans for a in p.ins]
    c_out = [s for p in plans for s in p.out_shape]
    c_scr = [s for p in plans for s in p.scratch]

    def body(*refs):
        cin, cout, cscr = refs[:len(c_in)], refs[len(c_in):len(c_in) + len(c_out)], refs[len(c_in) + len(c_out):]
        _split_plans(plans, cin, cout, cscr, "start")
        _split_plans(plans, cin, cout, cscr, "finish")

    res = pl.pallas_call(body, name=name, in_specs=[_any()] * len(c_in), out_specs=[_any()] * len(c_out),
                         out_shape=c_out, scratch_shapes=c_scr)(*c_in)
    return _plan_results(plans, res)


def _call(body, *, name, grid, in_specs, out_specs, out_shape, args, semantics, scratch_shapes=(), plans=()):
    n_in, n_out, n_scr = len(in_specs), len(out_specs), len(scratch_shapes)
    c_in = [a for p in plans for a in p.ins]
    c_out = [s for p in plans for s in p.out_shape]
    c_scr = [s for p in plans for s in p.scratch]

    def wrapped(*refs):
        ins, refs = refs[:n_in], refs[n_in:]
        cin, refs = refs[:len(c_in)], refs[len(c_in):]
        outs, refs = refs[:n_out], refs[n_out:]
        cout, refs = refs[:len(c_out)], refs[len(c_out):]
        scr, cscr = refs[:n_scr], refs[n_scr:]
        if plans:
            ids = [pl.program_id(d) for d in range(len(grid))]
            first = functools.reduce(jnp.logical_and, [i == 0 for i in ids])
            last = functools.reduce(jnp.logical_and, [i == g - 1 for i, g in zip(ids, grid)])
            pl.when(first)(lambda: _split_plans(plans, cin, cout, cscr, "start"))
        body(*ins, *outs, *scr)
        if plans:
            pl.when(last)(lambda: _split_plans(plans, cin, cout, cscr, "finish"))

    res = pl.pallas_call(
        wrapped, name=name, grid=grid, in_specs=list(in_specs) + [_any()] * len(c_in),
        out_specs=list(out_specs) + [_any()] * len(c_out), out_shape=list(out_shape) + c_out,
        scratch_shapes=list(scratch_shapes) + c_scr,
        compiler_params=_params(*(("arbitrary",) * len(grid) if plans else semantics)),
    )(*args, *c_in)
    return list(res[:n_out]), _plan_results(plans, res[n_out:])


def _ffn_fwd(h, mods, gn, wg, wu, wd, row0, seq, tgt=None, plans=()):
    T = h.shape[0]
    tm = _token_tile(seq, 256)
    tps = seq // tm
    n_t = T // tm
    with_loss = tgt is not None

    def body(h_ref, m_ref, g_ref, wg_ref, wu_ref, wd_ref, *rest):
        if with_loss:
            tgt_ref, out_ref, y_ref, gb_ref, ub_ref, loss_ref = rest
        else:
            out_ref, y_ref, gb_ref, ub_ref = rest
        hv = h_ref[...]
        sh = m_ref[0, row0:row0 + 1, :]
        sc = m_ref[0, row0 + 1:row0 + 2, :]
        ga = m_ref[0, row0 + 2:row0 + 3, :]
        xn, _, _ = _rms_mod(hv, g_ref[...], sh, sc)
        xb = xn.astype(BF16)
        acc = jnp.zeros((tm, D), F32)
        for c0 in range(0, FF, FC):
            gg = _dot_nt(xb, wg_ref[c0:c0 + FC, :])
            uu = _dot_nt(xb, wu_ref[c0:c0 + FC, :])
            gg, uu = gg.astype(BF16), uu.astype(BF16)
            gb_ref[:, c0:c0 + FC] = gg
            ub_ref[:, c0:c0 + FC] = uu
            acc = acc + _dot((gg * jax.nn.sigmoid(gg)) * uu, wd_ref[c0:c0 + FC, :])
        y_ref[...] = acc
        hout = hv + (0.5 * ga) * acc
        if with_loss:
            d = hout - tgt_ref[...]
            out_ref[...] = d * (1.0 / D)
            loss_ref[...] = jnp.full((1, 8, 128), 0.5 / D, F32) * jnp.sum(d * d)
        else:
            out_ref[...] = hout

    tok = pl.BlockSpec((tm, D), lambda i: (i, 0))
    tokf = pl.BlockSpec((tm, FF), lambda i: (i, 0))
    in_specs = [tok, pl.BlockSpec((1, 9, D), lambda i: (i // tps, 0, 0)), pl.BlockSpec((1, D), lambda i: (0, 0)),
                _vmem(), _vmem(), _vmem()]
    out_shape = [jax.ShapeDtypeStruct((T, D), F32), jax.ShapeDtypeStruct((T, D), F32),
                 jax.ShapeDtypeStruct((T, FF), BF16), jax.ShapeDtypeStruct((T, FF), BF16)]
    out_specs = [tok, tok, tokf, tokf]
    args = [h, mods, gn, wg, wu, wd]
    if with_loss:
        in_specs.append(tok)
        args.append(tgt)
        out_shape.append(jax.ShapeDtypeStruct((n_t, 8, 128), F32))
        out_specs.append(pl.BlockSpec((1, 8, 128), lambda i: (i, 0, 0)))
    return _call(body, name="ffn_fwd_loss" if with_loss else "ffn_fwd", grid=(n_t,), in_specs=in_specs,
                 out_specs=out_specs, out_shape=out_shape, args=args, semantics=("parallel",), plans=plans)


def _ffn_bwd(h, dhn, y, gb, ub, mods, gn, wg, wu, wd, row0, seq, plans=()):
    T = h.shape[0]
    B = T // seq
    tm = _token_tile(seq, 256)
    tps = seq // tm
    n_t = T // tm

    def body(h_ref, dhn_ref, y_ref, gb_ref, ub_ref, m_ref, g_ref, wg_ref, wu_ref, wd_ref,
             dh_ref, xb_ref, dyb_ref, a_ref, dg_ref, du_ref, mg_ref):
        i = pl.program_id(0)
        hv = h_ref[...]
        dhn = dhn_ref[...]
        sh = m_ref[0, row0:row0 + 1, :]
        sc = m_ref[0, row0 + 1:row0 + 2, :]
        ga = m_ref[0, row0 + 2:row0 + 3, :]
        g = g_ref[...]
        xn, r, inv = _rms_mod(hv, g, sh, sc)
        xb_ref[...] = xn.astype(BF16)
        dyb = ((0.5 * ga) * dhn).astype(BF16)
        dyb_ref[...] = dyb
        dga = 0.5 * jnp.sum(dhn * y_ref[...], axis=0, keepdims=True)
        dxn = jnp.zeros((tm, D), F32)
        for c0 in range(0, FF, FC):
            wgc = wg_ref[c0:c0 + FC, :]
            wuc = wu_ref[c0:c0 + FC, :]
            gg = gb_ref[:, c0:c0 + FC]
            uu = ub_ref[:, c0:c0 + FC]
            sig = jax.nn.sigmoid(gg)
            s = gg * sig
            a_ref[:, c0:c0 + FC] = s * uu
            da = _dot_nt(dyb, wd_ref[c0:c0 + FC, :]).astype(BF16)
            dub = da * s
            dgb = (da * uu) * (sig * (1.0 + gg * (1.0 - sig)))
            dg_ref[:, c0:c0 + FC] = dgb
            du_ref[:, c0:c0 + FC] = dub
            dxn = dxn + _dot(dgb, wgc) + _dot(dub, wuc)
        dh, s_dxn, s_dxr = _rms_mod_bwd(dxn, r, inv, g, sc)
        dh_ref[...] = dhn + dh

        @pl.when(i % tps == 0)
        def _():
            mg_ref[...] = jnp.zeros(mg_ref.shape, F32)

        mg_ref[0, 0:1, :] += s_dxn
        mg_ref[0, 1:2, :] += s_dxr
        mg_ref[0, 2:3, :] += dga

    tok = pl.BlockSpec((tm, D), lambda i: (i, 0))
    tokf = pl.BlockSpec((tm, FF), lambda i: (i, 0))
    return _call(
        body, name="ffn_bwd", grid=(n_t,),
        in_specs=[tok, tok, tok, tokf, tokf, pl.BlockSpec((1, 9, D), lambda i: (i // tps, 0, 0)),
                  pl.BlockSpec((1, D), lambda i: (0, 0)), _vmem(), _vmem(), _vmem()],
        out_specs=[tok, tok, tok, tokf, tokf, tokf, pl.BlockSpec((1, 8, D), lambda i: (i // tps, 0, 0))],
        out_shape=[jax.ShapeDtypeStruct((T, D), F32), jax.ShapeDtypeStruct((T, D), BF16),
                   jax.ShapeDtypeStruct((T, D), BF16), jax.ShapeDtypeStruct((T, FF), BF16),
                   jax.ShapeDtypeStruct((T, FF), BF16), jax.ShapeDtypeStruct((T, FF), BF16),
                   jax.ShapeDtypeStruct((B, 8, D), F32)],
        args=(h, dhn, y, gb, ub, mods, gn, wg, wu, wd), semantics=("arbitrary",), plans=plans)


def _wgrad(a, b, name, plans=()):
    T, da = a.shape
    db = b.shape[1]
    bm = {2816: 1408, 3840: 1280}[da]
    bn = db
    tk = min(1024, T)
    nk = T // tk

    def body(a_ref, b_ref, o_ref):
        @pl.when(pl.program_id(2) == 0)
        def _():
            o_ref[...] = jnp.zeros(o_ref.shape, F32)

        o_ref[...] += _dot_tn(a_ref[...], b_ref[...])

    (out,), plan_outs = _call(
        body, name=name, grid=(da // bm, db // bn, nk),
        in_specs=[pl.BlockSpec((tk, bm), lambda i, j, k: (k, i)), pl.BlockSpec((tk, bn), lambda i, j, k: (k, j))],
        out_specs=[pl.BlockSpec((bm, bn), lambda i, j, k: (i, j))], out_shape=[jax.ShapeDtypeStruct((da, db), F32)],
        args=(a, b), semantics=("parallel", "parallel", "arbitrary"), plans=plans)
    return out, plan_outs


def _inproj_fwd(h, mods, gn, w_in, seq):
    T = h.shape[0]
    tm = _token_tile(seq)
    tps = seq // tm

    def body(h_ref, m_ref, g_ref, w_ref, *outs):
        xn, _, _ = _rms_mod(h_ref[...], g_ref[...], m_ref[0, 3:4, :], m_ref[0, 4:5, :])
        xb = xn.astype(BF16)
        for (_, c0, w), o_ref in zip(PIECES, outs):
            o_ref[...] = _dot_nt(xb, w_ref[c0:c0 + w, :])

    return pl.pallas_call(
        body, name="inproj_fwd", grid=(T // tm,),
        in_specs=[pl.BlockSpec((tm, D), lambda i: (i, 0)), pl.BlockSpec((1, 9, D), lambda i: (i // tps, 0, 0)),
                  pl.BlockSpec((1, D), lambda i: (0, 0)), _vmem()],
        out_specs=[pl.BlockSpec((tm, w), lambda i: (i, 0)) for _, _, w in PIECES],
        out_shape=[jax.ShapeDtypeStruct((T, w), F32) for _, _, w in PIECES],
        compiler_params=_params("parallel"),
    )(h, mods, gn, w_in)


def _inproj_bwd(h, dh_res, dpieces, mods, gn, w_in, seq, plans=()):
    T = h.shape[0]
    B = T // seq
    tm = _token_tile(seq, 256)
    tps = seq // tm

    def body(h_ref, dres_ref, *rest):
        dp_refs = rest[:len(PIECES)]
        m_ref, g_ref, w_ref, dh_ref, xb_ref, dpb_ref, mg_ref = rest[len(PIECES):]
        i = pl.program_id(0)
        g = g_ref[...]
        sc = m_ref[0, 4:5, :]
        xn, r, inv = _rms_mod(h_ref[...], g, m_ref[0, 3:4, :], sc)
        xb_ref[...] = xn.astype(BF16)
        dxn = jnp.zeros((tm, D), F32)
        for (_, c0, w), dp_ref in zip(PIECES, dp_refs):
            dpb = dp_ref[...].astype(BF16)
            dpb_ref[:, c0:c0 + w] = dpb
            dxn = dxn + _dot(dpb, w_ref[c0:c0 + w, :])
        dh, s_dxn, s_dxr = _rms_mod_bwd(dxn, r, inv, g, sc)
        dh_ref[...] = dres_ref[...] + dh

        @pl.when(i % tps == 0)
        def _():
            mg_ref[...] = jnp.zeros(mg_ref.shape, F32)

        mg_ref[0, 0:1, :] += s_dxn
        mg_ref[0, 1:2, :] += s_dxr

    tok = pl.BlockSpec((tm, D), lambda i: (i, 0))
    return _call(
        body, name="inproj_bwd", grid=(T // tm,),
        in_specs=[tok, tok] + [pl.BlockSpec((tm, w), lambda i: (i, 0)) for _, _, w in PIECES]
        + [pl.BlockSpec((1, 9, D), lambda i: (i // tps, 0, 0)), pl.BlockSpec((1, D), lambda i: (0, 0)), _vmem()],
        out_specs=[tok, tok, pl.BlockSpec((tm, IN_COLS), lambda i: (i, 0)),
                   pl.BlockSpec((1, 8, D), lambda i: (i // tps, 0, 0))],
        out_shape=[jax.ShapeDtypeStruct((T, D), F32), jax.ShapeDtypeStruct((T, D), BF16),
                   jax.ShapeDtypeStruct((T, IN_COLS), BF16), jax.ShapeDtypeStruct((B, 8, D), F32)],
        args=(h, dh_res, *dpieces, mods, gn, w_in), semantics=("arbitrary",), plans=plans)


def _seg_mean(x):
    i = lax.broadcasted_iota(jnp.int32, (128, 128), 0) >> 6
    j = lax.broadcasted_iota(jnp.int32, (128, 128), 1) >> 6
    return jnp.dot(x, jnp.where(i == j, 1.0 / HD, 0.0).astype(F32), precision=lax.Precision.HIGHEST,
                   preferred_element_type=F32)


def _head_norm(x, g2):
    inv = lax.rsqrt(_seg_mean(x * x) + EPS)
    y = x * inv
    return y * g2, y, inv


def _head_norm_bwd(dxn, y, inv, g2):
    dy = dxn * g2
    return inv * (dy - y * _seg_mean(dy * y)), jnp.sum(dxn * y, axis=0, keepdims=True)


def _swa_block(q, kk, vv, gq2, gk2, sinks, first, do=None):
    lo = lax.broadcasted_iota(jnp.int32, (1, 128), 1) < HD
    kn, ky, kinv = _head_norm(kk, gk2)

    def operands(x):
        xr = pltpu.roll(x, HD, 1)
        own_lo, own_hi = jnp.where(lo, x, 0.0).astype(BF16), jnp.where(lo, 0.0, x).astype(BF16)
        rolled_lo, rolled_hi = jnp.where(lo, xr, 0.0).astype(BF16), jnp.where(lo, 0.0, xr).astype(BF16)
        return (own_lo, rolled_hi), (rolled_lo, own_hi)

    def restore(parts):
        (own_lo, rolled_hi), (rolled_lo, own_hi) = parts
        return (jnp.where(lo, own_lo, own_hi)
                + pltpu.roll(jnp.where(lo, rolled_lo, rolled_hi), HD, 1))

    k_ops, v_ops = operands(kn), operands(vv)
    k2 = [jnp.concatenate(pair, axis=0) for pair in k_ops]
    def stack(x):
        return jnp.concatenate([x[:, 128 * p:128 * (p + 1)] for p in range(4)], axis=0)

    def unstack(x):
        return jnp.concatenate([x[BLK * p:BLK * (p + 1)] for p in range(4)], axis=1)

    def of_head(x, kh):
        return x[2 * BLK * kh:2 * BLK * (kh + 1)]

    nq = 4 * BLK
    pair = lax.broadcasted_iota(jnp.int32, (nq, 1), 0) >> 7
    row = lax.broadcasted_iota(jnp.int32, (nq, 2 * BLK), 0) & (BLK - 1)
    col = lax.broadcasted_iota(jnp.int32, (nq, 2 * BLK), 1)
    valid = (col <= row + BLK) & (col > row) & (col >= jnp.where(first, BLK, 0))
    scale = HD ** -0.5
    qn, qy, qinv = _head_norm(stack(q), gq2)
    qnb = qn.astype(BF16)
    s2 = jnp.concatenate([_dot_nt(of_head(qnb, kh), k2[kh]) for kh in range(N_KV)], axis=0) * scale
    probs, p_sink = [], []
    for j in range(2):
        s = jnp.where(valid, s2[:, 2 * BLK * j:2 * BLK * (j + 1)], NEG)
        sink = jnp.zeros((nq, 1), F32)
        for p in range(4):
            sink = jnp.where(pair == p, sinks[:, 2 * p + j:2 * p + j + 1], sink)
        m = jnp.maximum(jnp.max(s, axis=-1, keepdims=True), sink)
        e = jnp.exp(s - m)
        e_sink = jnp.exp(sink - m)
        rden = 1.0 / (jnp.sum(e, axis=-1, keepdims=True) + e_sink)
        probs.append(e * rden)
        p_sink.append(e_sink * rden)
    pb = [p.astype(BF16) for p in probs]
    if do is None:
        return unstack(jnp.concatenate(
            [_dot(of_head(pb[0], kh), v_ops[kh][0]) + _dot(of_head(pb[1], kh), v_ops[kh][1]) for kh in range(N_KV)],
            axis=0))
    dob = stack(do).astype(BF16)
    ds, dsinks = [], [None] * 8
    for j in range(2):
        dp = jnp.concatenate([_dot_nt(of_head(dob, kh), v_ops[kh][j]) for kh in range(N_KV)], axis=0)
        t = jnp.sum(dp * probs[j], axis=-1, keepdims=True)
        ds.append(probs[j] * (dp - t) * scale)
        lost = p_sink[j] * t
        for p in range(4):
            dsinks[2 * p + j] = -jnp.sum(lost[BLK * p:BLK * (p + 1)])
    dsb = jnp.concatenate(ds, axis=1).astype(BF16)
    dqn = jnp.concatenate([_dot(of_head(dsb, kh), k2[kh]) for kh in range(N_KV)], axis=0)
    dq, dgq2 = _head_norm_bwd(dqn, qy, qinv, gq2)
    dk2 = [_dot_tn(of_head(dsb, kh), of_head(qnb, kh)) for kh in range(N_KV)]
    dv_ops = [[_dot_tn(of_head(pb[j], kh), of_head(dob, kh)) for j in range(2)] for kh in range(N_KV)]
    dkn = restore(tuple((d[:2 * BLK], d[2 * BLK:]) for d in dk2))
    dvv = restore(tuple(tuple(d) for d in dv_ops))
    dkk, dgk2 = _head_norm_bwd(dkn, ky, kinv, gk2)
    return unstack(dq), dkk, dvv, dgq2, dgk2, dsinks


SWA_GROUP = 2


def _swa_specs(nbs):
    grp = min(SWA_GROUP, nbs)
    rows = grp * BLK

    def tok(w):
        return pl.BlockSpec((rows, w), lambda i: (i, 0))

    halo = pl.BlockSpec((BLK, 128), lambda i: (jnp.maximum(i * grp - 1, 0), 0))
    vec = pl.BlockSpec((1, 128), lambda i: (0, 0))
    sk = pl.BlockSpec((1, 8), lambda i: (0, 0))
    return grp, tok, halo, vec, sk


def _swa_fwd(q, k, v, gq2, gk2, sinks, nbs, plans=()):
    T = q.shape[0]
    grp, tok, halo, vec, sk = _swa_specs(nbs)

    def body(q_ref, kh_ref, kc_ref, vh_ref, vc_ref, gq_ref, gk_ref, sk_ref, o_ref):
        seq_start = ((pl.program_id(0) * grp) % nbs) == 0
        for g in range(grp):
            rows = slice(g * BLK, (g + 1) * BLK)
            prev = slice((g - 1) * BLK, g * BLK)
            kk = jnp.concatenate([kh_ref[...] if g == 0 else kc_ref[prev, :], kc_ref[rows, :]], axis=0)
            vv = jnp.concatenate([vh_ref[...] if g == 0 else vc_ref[prev, :], vc_ref[rows, :]], axis=0)
            o_ref[rows, :] = _swa_block(q_ref[rows, :], kk, vv, gq_ref[...], gk_ref[...], sk_ref[...],
                                        seq_start if g == 0 else False)

    (out,), plan_outs = _call(
        body, name="swa_fwd", grid=(T // (grp * BLK),),
        in_specs=[tok(D_B), halo, tok(128), halo, tok(128), vec, vec, sk], out_specs=[tok(D_B)],
        out_shape=[jax.ShapeDtypeStruct((T, D_B), F32)], args=(q, k, k, v, v, gq2, gk2, sinks),
        semantics=("parallel",), plans=plans)
    return out, plan_outs


def _swa_bwd(q, k, v, gq2, gk2, sinks, do, nbs, plans=()):
    T = q.shape[0]
    grp, tok, halo, vec, sk = _swa_specs(nbs)
    steps = T // (grp * BLK)

    def body(q_ref, kh_ref, kc_ref, vh_ref, vc_ref, gq_ref, gk_ref, sk_ref, do_ref,
             dq_ref, dk_ref, dv_ref, dkh_ref, dvh_ref, dgq_ref, dgk_ref, dsk_ref):
        i = pl.program_id(0)
        seq_start = ((i * grp) % nbs) == 0

        @pl.when(i == 0)
        def _():
            for r in (dgq_ref, dgk_ref, dsk_ref):
                r[...] = jnp.zeros(r.shape, F32)

        res = []
        for g in range(grp):
            rows = slice(g * BLK, (g + 1) * BLK)
            prev = slice((g - 1) * BLK, g * BLK)
            kk = jnp.concatenate([kh_ref[...] if g == 0 else kc_ref[prev, :], kc_ref[rows, :]], axis=0)
            vv = jnp.concatenate([vh_ref[...] if g == 0 else vc_ref[prev, :], vc_ref[rows, :]], axis=0)
            res.append(_swa_block(q_ref[rows, :], kk, vv, gq_ref[...], gk_ref[...], sk_ref[...],
                                  seq_start if g == 0 else False, do=do_ref[rows, :]))
        lane = lax.broadcasted_iota(jnp.int32, (8, 128), 1)
        upd = jnp.zeros((8, 128), F32)
        for g, (dq, dkk, dvv, dgq2, dgk2, dsinks) in enumerate(res):
            rows = slice(g * BLK, (g + 1) * BLK)
            dq_ref[rows, :] = dq
            dk_ref[rows, :] = dkk[BLK:] + res[g + 1][1][:BLK] if g + 1 < grp else dkk[BLK:]
            dv_ref[rows, :] = dvv[BLK:] + res[g + 1][2][:BLK] if g + 1 < grp else dvv[BLK:]
            dgq_ref[...] += dgq2
            dgk_ref[...] += dgk2
            for h, d in enumerate(dsinks):
                upd = upd + jnp.where(lane == h, d, 0.0)
        dkh_ref[0] = res[0][1][:BLK]
        dvh_ref[0] = res[0][2][:BLK]
        dsk_ref[...] += upd

    one = pl.BlockSpec((1, BLK, 128), lambda i: (i, 0, 0))
    halo_shape = jax.ShapeDtypeStruct((steps, BLK, 128), F32)
    return _call(
        body, name="swa_bwd", grid=(steps,),
        in_specs=[tok(D_B), halo, tok(128), halo, tok(128), vec, vec, sk, tok(D_B)],
        out_specs=[tok(D_B), tok(128), tok(128), one, one, vec, vec, pl.BlockSpec((8, 128), lambda i: (0, 0))],
        out_shape=[jax.ShapeDtypeStruct((T, D_B), F32), jax.ShapeDtypeStruct((T, 128), F32),
                   jax.ShapeDtypeStruct((T, 128), F32), halo_shape, halo_shape, jax.ShapeDtypeStruct((1, 128), F32),
                   jax.ShapeDtypeStruct((1, 128), F32), jax.ShapeDtypeStruct((8, 128), F32)],
        args=(q, k, k, v, v, gq2, gk2, sinks, do), semantics=("arbitrary",), plans=plans)


def _swa_add_halo(dk, dk_halo):
    steps = dk_halo.shape[0]
    nxt = jnp.concatenate([dk_halo[1:], jnp.zeros_like(dk_halo[:1])], axis=0)[:, None]
    dk = dk.reshape(steps, -1, BLK, 128)
    return jnp.concatenate([dk[:, :-1], dk[:, -1:] + nxt], axis=1).reshape(-1, 128)


def _sgu_norm(av, g_ln, b_ln):
    t, th = _gelu(av)
    mu = jnp.mean(t, axis=-1, keepdims=True)
    tc = t - mu
    rstd = lax.rsqrt(jnp.mean(tc * tc, axis=-1, keepdims=True) + EPS)
    vhat = tc * rstd
    return vhat * g_ln + b_ln, vhat, rstd, th


def _masked_ws(ws_ref):
    tril = lax.broadcasted_iota(jnp.int32, (BLK, BLK), 0) >= lax.broadcasted_iota(jnp.int32, (BLK, BLK), 1)
    return [jnp.where(tril, ws_ref[g], 0.0).astype(BF16) for g in range(N_GRP)]


def _mix_fwd(au, av, gta, gtb, ob, h, mods, g_ln, b_ln, ws, bsb, wa, wb, wout, seq):
    T = h.shape[0]
    tm = _token_tile(seq)
    tps = seq // tm

    def body(au_ref, av_ref, gta_ref, gtb_ref, ob_ref, h_ref, m_ref, gl_ref, bl_ref, ws_ref, bs_ref,
             wa_ref, wb_ref, wo_ref, out_ref, vvb_s, z_s):
        u, _ = _gelu(au_ref[...])
        vv, _, _, _ = _sgu_norm(av_ref[...], gl_ref[...], bl_ref[...])
        vvb_s[...] = vv.astype(BF16)
        wsm = _masked_ws(ws_ref)
        for c in range(tm // BLK):
            rows = slice(c * BLK, (c + 1) * BLK)
            for g in range(N_GRP):
                cols = slice(g * BLK, (g + 1) * BLK)
                z_s[rows, cols] = _dot(wsm[g], vvb_s[rows, cols]) + bs_ref[g]
        ya = _dot_nt((u * z_s[...]).astype(BF16), wa_ref[...])
        yb = _dot_nt(ob_ref[...].astype(BF16), wb_ref[...])
        merged = jax.nn.sigmoid(gta_ref[...]) * ya + jax.nn.sigmoid(gtb_ref[...]) * yb
        out_ref[...] = h_ref[...] + m_ref[0, 5:6, :] * _dot(merged.astype(BF16), wo_ref[...])

    def tok(w):
        return pl.BlockSpec((tm, w), lambda i: (i, 0))

    def full(shape):
        return pl.BlockSpec(shape, lambda i: (0,) * len(shape))

    return pl.pallas_call(
        body, name="mix_fwd", grid=(T // tm,),
        in_specs=[tok(D_A), tok(D_A), tok(D), tok(D), tok(D_B), tok(D),
                  pl.BlockSpec((1, 9, D), lambda i: (i // tps, 0, 0)), full((1, D_A)), full((1, D_A)),
                  full((N_GRP, BLK, BLK)), full((N_GRP, BLK, BLK)), _vmem(), _vmem(), _vmem()],
        out_specs=tok(D), out_shape=jax.ShapeDtypeStruct((T, D), F32),
        scratch_shapes=[pltpu.VMEM((tm, D_A), BF16), pltpu.VMEM((tm, D_A), F32)],
        compiler_params=_params("parallel"),
    )(au, av, gta, gtb, ob, h, mods, g_ln, b_ln, ws, bsb, wa, wb, wout)


def _mix_bwd(au, av, gta, gtb, ob, dh, mods, g_ln, b_ln, ws, bsb, wa, wb, wout, seq, plans=()):
    T = dh.shape[0]
    B = T // seq
    tm = _token_tile(seq, 256)
    tps = seq // tm

    def body(au_ref, av_ref, gta_ref, gtb_ref, ob_ref, dh_ref, m_ref, gl_ref, bl_ref, ws_ref, bs_ref,
             wa_ref, wb_ref, wo_ref,
             dau_ref, dav_ref, dgta_ref, dgtb_ref, dob_ref, dwo_ref, dwa_ref, dwb_ref, dws_ref, dbs_ref, dln_ref,
             mg_ref, vvb_s, z_s, dz_s, dzb_s, dvv_s):
        i = pl.program_id(0)

        @pl.when(i == 0)
        def _():
            for r in (dwo_ref, dwa_ref, dwb_ref, dws_ref, dbs_ref, dln_ref):
                r[...] = jnp.zeros(r.shape, F32)

        @pl.when(i % tps == 0)
        def _():
            mg_ref[...] = jnp.zeros(mg_ref.shape, F32)

        auv = au_ref[...]
        avv = av_ref[...]
        u, thu = _gelu(auv)
        g_ln = gl_ref[...]
        vv, vhat, rstd, thv = _sgu_norm(avv, g_ln, bl_ref[...])
        vvb_s[...] = vv.astype(BF16)
        wsm = _masked_ws(ws_ref)
        for c in range(tm // BLK):
            rows = slice(c * BLK, (c + 1) * BLK)
            for g in range(N_GRP):
                cols = slice(g * BLK, (g + 1) * BLK)
                z_s[rows, cols] = _dot(wsm[g], vvb_s[rows, cols]) + bs_ref[g]
        z = z_s[...]
        yab = (u * z).astype(BF16)
        obb = ob_ref[...].astype(BF16)
        ya = _dot_nt(yab, wa_ref[...])
        yb = _dot_nt(obb, wb_ref[...])
        sa = jax.nn.sigmoid(gta_ref[...])
        sb = jax.nn.sigmoid(gtb_ref[...])
        mb = (sa * ya + sb * yb).astype(BF16)
        dhv = dh_ref[...]
        mg_ref[0, 0:1, :] += jnp.sum(dhv * _dot(mb, wo_ref[...]), axis=0, keepdims=True)
        dmob = (m_ref[0, 5:6, :] * dhv).astype(BF16)
        dwo_ref[...] += _dot_tn(mb, dmob)
        dmerged = _dot_nt(dmob, wo_ref[...])
        dya = dmerged * sa
        dyb = dmerged * sb
        dgta_ref[...] = dya * ya * (1.0 - sa)
        dgtb_ref[...] = dyb * yb * (1.0 - sb)
        dyab = dya.astype(BF16)
        dybb = dyb.astype(BF16)
        dwa_ref[...] += _dot_tn(dyab, yab)
        dwb_ref[...] += _dot_tn(dybb, obb)
        dob_ref[...] = _dot(dybb, wb_ref[...])
        dyap = _dot(dyab, wa_ref[...])
        dau_ref[...] = (dyap * z) * _gelu_grad(auv, thu)
        dz = dyap * u
        dz_s[...] = dz
        dzb_s[...] = dz.astype(BF16)
        for c in range(tm // BLK):
            rows = slice(c * BLK, (c + 1) * BLK)
            for g in range(N_GRP):
                cols = slice(g * BLK, (g + 1) * BLK)
                dzb = dzb_s[rows, cols]
                dvv_s[rows, cols] = _dot_tn(wsm[g], dzb)
                dws_ref[g] += _dot_nt(dzb, vvb_s[rows, cols])
                dbs_ref[g] += dz_s[rows, cols]
        dvv = dvv_s[...]
        dln_ref[0:1, :] += jnp.sum(dvv * vhat, axis=0, keepdims=True)
        dln_ref[1:2, :] += jnp.sum(dvv, axis=0, keepdims=True)
        dvh = dvv * g_ln
        dt = rstd * (dvh - jnp.mean(dvh, axis=-1, keepdims=True)
                     - vhat * jnp.mean(dvh * vhat, axis=-1, keepdims=True))
        dav_ref[...] = dt * _gelu_grad(avv, thv)

    def tok(w):
        return pl.BlockSpec((tm, w), lambda i: (i, 0))

    def full(shape):
        return pl.BlockSpec(shape, lambda i: (0,) * len(shape))

    return _call(
        body, name="mix_bwd", grid=(T // tm,),
        in_specs=[tok(D_A), tok(D_A), tok(D), tok(D), tok(D_B), tok(D),
                  pl.BlockSpec((1, 9, D), lambda i: (i // tps, 0, 0)), full((1, D_A)), full((1, D_A)),
                  full((N_GRP, BLK, BLK)), full((N_GRP, BLK, BLK)), _vmem(), _vmem(), _vmem()],
        out_specs=[tok(D_A), tok(D_A), tok(D), tok(D), tok(D_B), full((D, D)), full((D, D_A)), full((D, D_B)),
                   full((N_GRP, BLK, BLK)), full((N_GRP, BLK, BLK)), full((8, D_A)),
                   pl.BlockSpec((1, 8, D), lambda i: (i // tps, 0, 0))],
        out_shape=[jax.ShapeDtypeStruct((T, D_A), F32), jax.ShapeDtypeStruct((T, D_A), F32),
                   jax.ShapeDtypeStruct((T, D), F32), jax.ShapeDtypeStruct((T, D), F32),
                   jax.ShapeDtypeStruct((T, D_B), F32), jax.ShapeDtypeStruct((D, D), F32),
                   jax.ShapeDtypeStruct((D, D_A), F32), jax.ShapeDtypeStruct((D, D_B), F32),
                   jax.ShapeDtypeStruct((N_GRP, BLK, BLK), F32), jax.ShapeDtypeStruct((N_GRP, BLK, BLK), F32),
                   jax.ShapeDtypeStruct((8, D_A), F32), jax.ShapeDtypeStruct((B, 8, D), F32)],
        scratch_shapes=[pltpu.VMEM((tm, D_A), BF16), pltpu.VMEM((tm, D_A), F32), pltpu.VMEM((tm, D_A), F32),
                        pltpu.VMEM((tm, D_A), BF16), pltpu.VMEM((tm, D_A), F32)],
        args=(au, av, gta, gtb, ob, dh, mods, g_ln, b_ln, ws, bsb, wa, wb, wout), semantics=("arbitrary",),
        plans=plans)


def _ada_fwd(c_all, w_ada):
    def body(c_ref, w_ref, o_ref):
        cv = c_ref[...]
        o_ref[...] = _dot((cv * jax.nn.sigmoid(cv)).astype(BF16), w_ref[...].astype(BF16))

    out = jax.ShapeDtypeStruct((c_all.shape[0], w_ada.shape[1]), F32)
    return pl.pallas_call(body, name="ada_fwd", grid=(1,), in_specs=[_whole(c_all), _whole(w_ada)],
                          out_specs=_whole(out), out_shape=out, compiler_params=_params("arbitrary"))(c_all, w_ada)


def _ada_update(c_all, dm_cols, w, m, v, plans=()):
    n, cols = c_all.shape[0], w.shape[1]

    def body(c_ref, dm_ref, w_ref, m_ref, v_ref, g_ref, d_ref, nm_ref, nv_ref):
        cv = c_ref[...]
        g = _dot_tn((cv * jax.nn.sigmoid(cv)).astype(BF16), dm_ref[...].astype(BF16))
        g_ref[...] = g
        d_ref[...], nm_ref[...], nv_ref[...] = _adamw(w_ref[...], g, m_ref[...], v_ref[...])

    col = pl.BlockSpec((D, 128), lambda j: (0, j))
    return _call(
        body, name="ada_update", grid=(cols // 128,),
        in_specs=[pl.BlockSpec((n, D), lambda j: (0, 0)), pl.BlockSpec((n, 128), lambda j: (0, j)), col, col, col],
        out_specs=[col] * 4, out_shape=[jax.ShapeDtypeStruct(w.shape, F32)] * 4, args=(c_all, dm_cols, w, m, v),
        semantics=("parallel",), plans=plans)


def _mod_finish(mg1, mg2, mg2g, mg3, mods, g1, g2, g3):
    B = mods.shape[0]

    def body(mg1_ref, mg2_ref, mg2g_ref, mg3_ref, m_ref, g1_ref, g2_ref, g3_ref, dm_ref, dgn_ref):
        dgn_ref[...] = jnp.zeros(dgn_ref.shape, F32)
        for k, (mg, g_ref) in enumerate(((mg1_ref, g1_ref), (mg2_ref, g2_ref), (mg3_ref, g3_ref))):
            for b in range(B):
                s_dxr = mg[b, 1:2, :]
                dm_ref[b, 3 * k:3 * k + 1, :] = mg[b, 0:1, :]
                dm_ref[b, 3 * k + 1:3 * k + 2, :] = g_ref[...] * s_dxr
                dm_ref[b, 3 * k + 2:3 * k + 3, :] = mg2g_ref[b, 0:1, :] if k == 1 else mg[b, 2:3, :]
                dgn_ref[k:k + 1, :] += (1.0 + m_ref[b, 3 * k + 1:3 * k + 2, :]) * s_dxr

    args = (mg1, mg2, mg2g, mg3, mods, g1, g2, g3)
    out_shape = [jax.ShapeDtypeStruct((B, 9, D), F32), jax.ShapeDtypeStruct((8, D), F32)]
    return pl.pallas_call(body, name="mod_finish", grid=(1,), in_specs=[_whole(a) for a in args],
                          out_specs=[_whole(o) for o in out_shape], out_shape=out_shape,
                          compiler_params=_params("arbitrary"))(*args)


def _small_update(gathered, params, ms, vs):
    n = len(SMALL)
    B = gathered[0].shape[1]

    def body(*refs):
        gdm, ggn, gln, gws, gbs, ggq, ggk, gsk, gloss = refs[:9]
        w = dict(zip(SMALL, refs[9:9 + n]))
        m = dict(zip(SMALL, refs[9 + n:9 + 2 * n]))
        v = dict(zip(SMALL, refs[9 + 2 * n:9 + 3 * n]))
        outs = refs[9 + 3 * n:-1]
        out = {name: outs[4 * k:4 * k + 4] for k, name in enumerate(SMALL)}

        def total(ref, idx):
            acc = ref[(0,) + idx]
            for dev in range(1, N_DEV):
                acc = acc + ref[(dev,) + idx]
            return acc

        def finish(name, g, idx=(Ellipsis,)):
            d, nm, nv = _adamw(w[name][idx], g, m[name][idx], v[name][idx])
            for ref, val in zip(out[name], (g, d, nm, nv)):
                ref[idx] = val

        g_bada = total(gdm, (slice(0, 1),))
        for b in range(1, B):
            g_bada = g_bada + total(gdm, (slice(b, b + 1),))
        finish("b_ada", g_bada)
        finish("g_norm1", total(ggn, (slice(0, 1),)))
        finish("g_norm2", total(ggn, (slice(1, 2),)))
        finish("g_norm3", total(ggn, (slice(2, 3),)))
        finish("g_sgu_ln", total(gln, (slice(0, 1),)))
        finish("b_sgu_ln", total(gln, (slice(1, 2),)))
        tril = lax.broadcasted_iota(jnp.int32, (BLK, BLK), 0) >= lax.broadcasted_iota(jnp.int32, (BLK, BLK), 1)
        for g in range(N_GRP):
            finish("w_spatial", jnp.where(tril, total(gws, (g,)), 0.0), (0, g))
            finish("b_spatial", jnp.sum(total(gbs, (g,)).T, axis=0, keepdims=True), (0, slice(g, g + 1)))
        finish("g_q", total(ggq, ()))
        finish("g_k", total(ggk, ()))
        finish("attn_sinks", total(gsk, (slice(0, 1), slice(0, N_KV * Q_PER_KV))))
        refs[-1][...] = total(gloss, ())

    args = list(gathered) + [params[k] for k in SMALL] + [ms[k] for k in SMALL] + [vs[k] for k in SMALL]
    out_shape = []
    for k in SMALL:
        out_shape += [jax.ShapeDtypeStruct(params[k].shape, F32)] * 4
    out_shape.append(jax.ShapeDtypeStruct((8, 128), F32))
    res = pl.pallas_call(body, name="small_update", grid=(1,), in_specs=[_whole(a) for a in args],
                         out_specs=[_whole(o) for o in out_shape], out_shape=out_shape,
                         compiler_params=_params("arbitrary"))(*args)
    return {k: res[4 * i:4 * i + 4] for i, k in enumerate(SMALL)}, res[-1][0, 0]


def _rs_add(p, r, core, name):
    _, rows, width = p.shape

    def body(c_ref, p_ref, r_ref, o_ref):
        o_ref[...] = (p_ref[...] + r_ref[...]).astype(BF16)

    return pl.pallas_call(
        body, name="rs_add_" + name, out_shape=jax.ShapeDtypeStruct((4, rows, width), BF16),
        grid_spec=pltpu.PrefetchScalarGridSpec(
            num_scalar_prefetch=1, grid=(4,),
            in_specs=[pl.BlockSpec((1, rows, width), lambda k, c_ref: (2 * k + c_ref[0], 0, 0)),
                      pl.BlockSpec((1, rows, width), lambda k, c_ref: (k, 0, 0))],
            out_specs=pl.BlockSpec((1, rows, width), lambda k, c_ref: (k, 0, 0))),
        compiler_params=_params("parallel"),
    )(core, p, r)


def _adam_big(w, m, v, r, name, transposed):
    _, rows, cols = w.shape
    br = 256 if rows % 256 == 0 else rows
    pad = -cols % 128

    def body(r_ref, w_ref, m_ref, v_ref, go_ref, d_ref, nm_ref, nv_ref, *scratch):
        g = r_ref[0].astype(F32)
        for k in range(1, 4):
            g = g + r_ref[k].astype(F32)
        if transposed:
            if pad:
                g = jnp.concatenate([g, jnp.zeros((pad, br), F32)], axis=0)
            scratch[0][...] = g.T
            g = scratch[0][:, 0:cols]
        go_ref[0] = g
        d_ref[0], nm_ref[0], nv_ref[0] = _adamw(w_ref[0], g, m_ref[0], v_ref[0])

    shard = pl.BlockSpec((1, br, cols), lambda i: (0, i, 0))
    parts = pl.BlockSpec((4, cols, br), lambda i: (0, 0, i)) if transposed else pl.BlockSpec((4, br, cols),
                                                                                           lambda i: (0, i, 0))
    return pl.pallas_call(body, name="adam_" + name, grid=(rows // br,), in_specs=[parts, shard, shard, shard],
                          out_specs=[shard] * 4, out_shape=[jax.ShapeDtypeStruct(w.shape, F32)] * 4,
                          scratch_shapes=[pltpu.VMEM((br, cols + pad), F32)] if transposed else [],
                          compiler_params=_params("parallel"))(r, w, m, v)


def kernel(x, c, w_ada, b_ada, g_norm1, ffn1_w_gate, ffn1_w_up, ffn1_w_down, g_norm2, w_in, g_sgu_ln, b_sgu_ln, w_spatial, b_spatial, g_q, g_k, attn_sinks, w_branch_a, w_branch_b, w_out, g_norm3, ffn2_w_gate, ffn2_w_up, ffn2_w_down, loss_target, m_w_ada, m_b_ada, m_g_norm1, m_ffn1_w_gate, m_ffn1_w_up, m_ffn1_w_down, m_g_norm2, m_w_in, m_g_sgu_ln, m_b_sgu_ln, m_w_spatial, m_b_spatial, m_g_q, m_g_k, m_attn_sinks, m_w_branch_a, m_w_branch_b, m_w_out, m_g_norm3, m_ffn2_w_gate, m_ffn2_w_up, m_ffn2_w_down, v_w_ada, v_b_ada, v_g_norm1, v_ffn1_w_gate, v_ffn1_w_up, v_ffn1_w_down, v_g_norm2, v_w_in, v_g_sgu_ln, v_b_sgu_ln, v_w_spatial, v_b_spatial, v_g_q, v_g_k, v_attn_sinks, v_w_branch_a, v_w_branch_b, v_w_out, v_g_norm3, v_ffn2_w_gate, v_ffn2_w_up, v_ffn2_w_down):
    names = ("w_ada", "b_ada", "g_norm1", "ffn1_w_gate", "ffn1_w_up", "ffn1_w_down", "g_norm2", "w_in", "g_sgu_ln",
             "b_sgu_ln", "w_spatial", "b_spatial", "g_q", "g_k", "attn_sinks", "w_branch_a", "w_branch_b", "w_out",
             "g_norm3", "ffn2_w_gate", "ffn2_w_up", "ffn2_w_down")
    w = dict(zip(names, (w_ada, b_ada, g_norm1, ffn1_w_gate, ffn1_w_up, ffn1_w_down, g_norm2, w_in, g_sgu_ln,
                         b_sgu_ln, w_spatial, b_spatial, g_q, g_k, attn_sinks, w_branch_a, w_branch_b, w_out, g_norm3,
                         ffn2_w_gate, ffn2_w_up, ffn2_w_down)))
    m = dict(zip(names, (m_w_ada, m_b_ada, m_g_norm1, m_ffn1_w_gate, m_ffn1_w_up, m_ffn1_w_down, m_g_norm2, m_w_in,
                         m_g_sgu_ln, m_b_sgu_ln, m_w_spatial, m_b_spatial, m_g_q, m_g_k, m_attn_sinks, m_w_branch_a,
                         m_w_branch_b, m_w_out, m_g_norm3, m_ffn2_w_gate, m_ffn2_w_up, m_ffn2_w_down)))
    v = dict(zip(names, (v_w_ada, v_b_ada, v_g_norm1, v_ffn1_w_gate, v_ffn1_w_up, v_ffn1_w_down, v_g_norm2, v_w_in,
                         v_g_sgu_ln, v_b_sgu_ln, v_w_spatial, v_b_spatial, v_g_q, v_g_k, v_attn_sinks, v_w_branch_a,
                         v_w_branch_b, v_w_out, v_g_norm3, v_ffn2_w_gate, v_ffn2_w_up, v_ffn2_w_down)))
    B, seq, _ = x.shape
    T = B * seq
    nbs = seq // BLK
    xi, yi, ci = _place()
    me = 4 * xi + 2 * yi + ci
    core = jnp.reshape(ci, (1,)).astype(jnp.int32)
    layout = {name: (tform, n, width) for name, tform, n, width in BIG}
    shard = {name: w[name][0].astype(BF16).T if tform else w[name][0].astype(BF16) for name, tform, _, _ in BIG}
    wts, parts, big_out = {}, {}, {}

    def gather_plan(group):
        return _Gather([shard[k] for k in group])

    def take(group, gathered):
        for k, g in zip(group, gathered):
            wts[k] = g.reshape(N_DEV * layout[k][1], layout[k][2])

    def blocks(name, grad):
        return grad.reshape(N_DEV, layout[name][1], layout[name][2])

    def to_sibling(names, grads):
        return _RsSibling([blocks(k, g) for k, g in zip(names, grads)])

    def add(names, grads, from_sibling):
        for k, g, r in zip(names, grads, from_sibling):
            parts[k] = _rs_add(blocks(k, g), r, core, k)

    def to_chips(names):
        return _RsChips([parts[k] for k in names])

    def update(names, from_chips):
        for k, r in zip(names, from_chips):
            big_out[k] = _adam_big(w[k], m[k], v[k], r, k, transposed=layout[k][0])

    g1, g2, g3, g_ln, b_ln = g_norm1, g_norm2, g_norm3, g_sgu_ln, b_sgu_ln
    gq2, gk2 = jnp.tile(g_q, (1, 2)), jnp.tile(g_k, (1, 2))
    ws = w_spatial[0]
    bsb = jnp.broadcast_to(b_spatial[0][:, :, None], (N_GRP, BLK, BLK))
    xf = x.reshape(T, D)
    tgt = loss_target.reshape(T, D)

    ((c_all,),) = _exchange([_Gather([c])], "gather_c")
    c_all = c_all.reshape(N_DEV * B, D)
    ffn1 = ("ffn1_w_gate", "ffn1_w_up", "ffn1_w_down")
    ffn2 = ("ffn2_w_gate", "ffn2_w_up", "ffn2_w_down")
    ((mods_cols, *gathered),) = _exchange(
        [_Gather([_ada_fwd(c_all, w_ada[0])] + [shard[k] for k in ffn1])], "gather_first")
    take(ffn1, gathered)
    mine = lax.dynamic_slice_in_dim(mods_cols, B * me, B, axis=1)
    mods = (mine.transpose(1, 0, 2).reshape(B, 9 * D) + b_ada).reshape(B, 9, D)

    group = ("w_in", "w_branch_a", "w_branch_b", "w_out", "ffn2_w_gate")
    (h1, y1, gb1, ub1), (gathered,) = _ffn_fwd(xf, mods, g1, *[wts[k] for k in ffn1], 0, seq,
                                               plans=[gather_plan(group)])
    take(group, gathered)
    au, av, q_tok, k_tok, v_tok, gta, gtb = _inproj_fwd(h1, mods, g2, wts["w_in"], seq)
    group = ("ffn2_w_up", "ffn2_w_down")
    ob, (gathered,) = _swa_fwd(q_tok, k_tok, v_tok, gq2, gk2, attn_sinks, nbs, plans=[gather_plan(group)])
    take(group, gathered)
    mixw = (wts["w_branch_a"], wts["w_branch_b"], wts["w_out"])
    h2 = _mix_fwd(au, av, gta, gtb, ob, h1, mods, g_ln, b_ln, ws, bsb, *mixw, seq)
    (dh3, y3, gb3, ub3, loss_parts), _ = _ffn_fwd(h2, mods, g3, *[wts[k] for k in ffn2], 6, seq, tgt=tgt)
    loss_part = jnp.full((8, 128), jnp.sum(loss_parts[:, 0, 0]))

    (dh2, xb, dyb, a, dg, du, mg3), _ = _ffn_bwd(h2, dh3, y3, gb3, ub3, mods, g3, *[wts[k] for k in ffn2], 6, seq)
    d_gate, _ = _wgrad(dg, xb, "wgrad_ffn2_gate")
    d_up, (r,) = _wgrad(du, xb, "wgrad_ffn2_up", plans=[to_sibling(ffn2[:1], [d_gate])])
    add(ffn2[:1], [d_gate], r)
    d_down, (r,) = _wgrad(a, dyb, "wgrad_ffn2_down", plans=[to_sibling(ffn2[1:2], [d_up])])
    add(ffn2[1:2], [d_up], r)

    (dau, dav, dgta, dgtb, dob, d_out, d_a, d_b, dws, dbs, dln, mg2g), (r, from_chips) = _mix_bwd(
        au, av, gta, gtb, ob, dh2, mods, g_ln, b_ln, ws, bsb, *mixw, seq,
        plans=[to_sibling(ffn2[2:], [d_down]), to_chips(ffn2[:2])])
    add(ffn2[2:], [d_down], r)
    update(ffn2[:2], from_chips)
    mixers = ("w_out", "w_branch_a", "w_branch_b")
    (dq, dk, dv, dk_halo, dv_halo, dgq2, dgk2, dsk), (from_chips, r) = _swa_bwd(
        q_tok, k_tok, v_tok, gq2, gk2, attn_sinks, dob, nbs,
        plans=[to_chips(ffn2[2:]), to_sibling(mixers, [d_out, d_a, d_b])])
    update(ffn2[2:], from_chips)
    add(mixers, [d_out, d_a, d_b], r)
    dk, dv = _swa_add_halo(dk, dk_halo), _swa_add_halo(dv, dv_halo)
    dgq, dgk = dgq2[:, :HD] + dgq2[:, HD:], dgk2[:, :HD] + dgk2[:, HD:]
    (dh1, xb2, dpb, mg2), (from_chips, early) = _inproj_bwd(
        h1, dh2, (dau, dav, dq, dk, dv, dgta, dgtb), mods, g2, wts["w_in"], seq,
        plans=[to_chips(mixers), _Gather([dln, dws, dbs, dgq, dgk, dsk, loss_part])])
    update(mixers, from_chips)

    (dx, xb, dyb, a, dg, du, mg1), _ = _ffn_bwd(xf, dh1, y1, gb1, ub1, mods, g1, *[wts[k] for k in ffn1], 0, seq)
    dmods, dgn = _mod_finish(mg1, mg2, mg2g, mg3, mods, g1, g2, g3)
    d_gate, (late,) = _wgrad(dg, xb, "wgrad_ffn1_gate", plans=[_Gather([dmods.reshape(B, 9 * D), dgn])])
    gathered = late + early
    d_up, (r,) = _wgrad(du, xb, "wgrad_ffn1_up", plans=[to_sibling(ffn1[:1], [d_gate])])
    add(ffn1[:1], [d_gate], r)
    d_down, (r, from_chips) = _wgrad(a, dyb, "wgrad_ffn1_down",
                                     plans=[to_sibling(ffn1[1:2], [d_up]), to_chips(ffn1[:1])])
    add(ffn1[1:2], [d_up], r)
    update(ffn1[:1], from_chips)
    d_in, (r, from_chips) = _wgrad(dpb, xb2, "wgrad_w_in",
                                   plans=[to_sibling(ffn1[2:], [d_down]), to_chips(ffn1[1:2])])
    add(ffn1[2:], [d_down], r)
    update(ffn1[1:2], from_chips)
    small_out, loss = _small_update(gathered, w, m, v)
    dm_cols = lax.dynamic_slice_in_dim(gathered[0].reshape(N_DEV * B, 9 * D), (9 * D // N_DEV) * me,
                                       9 * D // N_DEV, axis=1)
    ada_out, (r, from_chips) = _ada_update(c_all, dm_cols, w_ada[0], m_w_ada[0], v_w_ada[0],
                                           plans=[to_sibling(("w_in",), [d_in]), to_chips(ffn1[2:])])
    add(("w_in",), [d_in], r)
    update(ffn1[2:], from_chips)
    (from_chips,) = _exchange([to_chips(("w_in",))], "rs_last")
    update(("w_in",), from_chips)

    def leaf(kind, name):
        if name == "w_ada":
            return ada_out[kind][None]
        if name in SMALL:
            return small_out[name][kind]
        return big_out[name][kind]

    return (loss, dx.reshape(B, seq, D), *[leaf(kind, name) for kind in range(4) for name in names])
```

```python
import functools
import math

import jax
import jax.numpy as jnp
from jax import lax
from jax.experimental import pallas as pl
from jax.experimental.pallas import tpu as pltpu

F32 = jnp.float32
BF16 = jnp.bfloat16
MESH = pl.DeviceIdType.MESH
AXES = ("x", "y", "c")
N_DEV = 8

VMEM_LIMIT = 56 * 1024 * 1024

D = 1024
FF = 2816
FC = 1408
D_A = 512
D_B = 512
HD = 64
N_KV = 2
Q_PER_KV = 4
BLK = 128
N_GRP = 4
IN_COLS = 3840
PIECES = (("au", 0, 512), ("av", 512, 512), ("q", 1024, 512), ("k", 1536, 128), ("v", 1664, 128),
          ("ga", 1792, 1024), ("gb", 2816, 1024))
EPS = 1e-6
NEG = -1e30
GELU_C = math.sqrt(2.0 / math.pi)

ADAM_LR = 0.001
ADAM_B1 = 0.9
ADAM_B2 = 0.999
ADAM_EPS = 1e-08
ADAM_WD = 0.01
ADAM_STEP = 10

NT = (((1,), (1,)), ((), ()))
TN = (((0,), (0,)), ((), ()))

BIG = (("ffn1_w_gate", True, FF // N_DEV, D), ("ffn1_w_up", True, FF // N_DEV, D),
       ("ffn1_w_down", False, FF // N_DEV, D), ("w_in", True, IN_COLS // N_DEV, D),
       ("w_branch_a", True, D // N_DEV, D_A), ("w_branch_b", True, D // N_DEV, D_B), ("w_out", False, D // N_DEV, D),
       ("ffn2_w_gate", True, FF // N_DEV, D), ("ffn2_w_up", True, FF // N_DEV, D),
       ("ffn2_w_down", False, FF // N_DEV, D))
SMALL = ("b_ada", "g_norm1", "g_norm2", "g_sgu_ln", "b_sgu_ln", "w_spatial", "b_spatial", "g_q", "g_k",
         "attn_sinks", "g_norm3")


def _dot(a, b):
    return jnp.dot(a, b, preferred_element_type=F32)


def _dot_nt(a, b):
    return lax.dot_general(a, b, NT, preferred_element_type=F32)


def _dot_tn(a, b):
    return lax.dot_general(a, b, TN, preferred_element_type=F32)


def _vmem():
    return pl.BlockSpec(memory_space=pltpu.VMEM)


def _any():
    return pl.BlockSpec(memory_space=pl.ANY)


def _whole(a):
    return pl.BlockSpec(a.shape, lambda i: (0,) * len(a.shape))


def _rms_mod(h, g, sh, sc):
    inv = lax.rsqrt(jnp.mean(h * h, axis=-1, keepdims=True) + EPS)
    r = h * inv
    return (r * g) * (1.0 + sc) + sh, r, inv


def _rms_mod_bwd(dxn, r, inv, g, sc):
    dr = dxn * (g * (1.0 + sc))
    dh = inv * (dr - r * jnp.mean(dr * r, axis=-1, keepdims=True))
    return dh, jnp.sum(dxn, axis=0, keepdims=True), jnp.sum(dxn * r, axis=0, keepdims=True)


def _gelu(x):
    t = jnp.tanh(GELU_C * (x + 0.044715 * (x * x * x)))
    return 0.5 * x * (1.0 + t), t


def _gelu_grad(x, t):
    return 0.5 * (1.0 + t) + 0.5 * x * (1.0 - t * t) * (GELU_C * (1.0 + 3.0 * 0.044715 * x * x))


def _adamw(w, g, m, v):
    m = ADAM_B1 * m + (1.0 - ADAM_B1) * g
    v = ADAM_B2 * v + (1.0 - ADAM_B2) * (g * g)
    m_hat = m / (1.0 - ADAM_B1 ** ADAM_STEP)
    v_hat = v / (1.0 - ADAM_B2 ** ADAM_STEP)
    delta = -ADAM_LR * (m_hat / (jnp.sqrt(v_hat) + ADAM_EPS) + ADAM_WD * w)
    return delta, m, v


def _token_tile(seq, cap=512):
    return min(cap, seq)


def _params(*semantics):
    return pltpu.CompilerParams(dimension_semantics=semantics, vmem_limit_bytes=VMEM_LIMIT)


def _place():
    return lax.axis_index("x"), lax.axis_index("y"), lax.axis_index("c")


class _Gather:
    def __init__(self, arrays):
        n = len(arrays)
        self.ins = list(arrays)
        self.out_shape = [jax.ShapeDtypeStruct((N_DEV,) + a.shape, a.dtype) for a in arrays]
        self.scratch = [pltpu.SemaphoreType.DMA((n, 7)), pltpu.SemaphoreType.DMA((n, 7)),
                        pltpu.SemaphoreType.DMA((n,))]

    def _copies(self, ins, outs, sems):
        send_sems, recv_sems, local_sems = sems
        n = len(ins)
        x, y, c = _place()
        me, sibling = (x, y, c), (x, y, 1 - c)
        chips = [(1 - x, y), (x, 1 - y), (1 - x, 1 - y)]

        def slot(a, px, py, pc):
            return outs[a].at[4 * px + 2 * py + pc]

        def copy(a, k, block, to, src=None):
            return pltpu.make_async_remote_copy(
                src_ref=slot(a, *block) if src is None else src, dst_ref=slot(a, *block),
                send_sem=send_sems.at[a, k], recv_sem=recv_sems.at[a, k], device_id=to, device_id_type=MESH)

        mine = [pltpu.make_async_copy(ins[a], slot(a, *me), local_sems.at[a]) for a in range(n)]
        first = [copy(a, 0, me, sibling, src=ins[a]) for a in range(n)]
        first += [copy(a, 1 + j, me, (*chip, c), src=ins[a]) for a in range(n) for j, chip in enumerate(chips)]
        landed = [[copy(a, 1 + j, (*chip, c), me) for a in range(n)] for j, chip in enumerate(chips)]
        passed = [[copy(a, 4 + j, (*chip, c), sibling) for a in range(n)] for j, chip in enumerate(chips)]
        from_sibling = [copy(a, 0, sibling, me) for a in range(n)]
        from_sibling += [copy(a, 4 + j, (*chip, 1 - c), me) for a in range(n) for j, chip in enumerate(chips)]
        return mine, first, landed, passed, from_sibling

    def start(self, ins, outs, sems):
        mine, first, _, _, _ = self._copies(ins, outs, sems)
        for cp in mine + first:
            cp.start()

    def finish(self, ins, outs, sems):
        mine, first, landed, passed, from_sibling = self._copies(ins, outs, sems)
        for arrivals, forwards in zip(landed, passed):
            for arrival, forward in zip(arrivals, forwards):
                arrival.wait_recv()
                forward.start()
        for cp in from_sibling:
            cp.wait_recv()
        for cp in first + [f for fs in passed for f in fs]:
            cp.wait_send()
        for cp in mine:
            cp.wait()


class _RsSibling:
    def __init__(self, ps):
        n = len(ps)
        self.ins = list(ps)
        self.out_shape = [jax.ShapeDtypeStruct((4,) + p.shape[1:], p.dtype) for p in ps]
        self.scratch = [pltpu.SemaphoreType.DMA((n, 4)), pltpu.SemaphoreType.DMA((n, 4))]

    def _copies(self, ins, outs, sems):
        send_sems, recv_sems = sems
        x, y, c = _place()
        return [pltpu.make_async_remote_copy(
            src_ref=ins[a].at[2 * q + (1 - c)], dst_ref=outs[a].at[q], send_sem=send_sems.at[a, q],
            recv_sem=recv_sems.at[a, q], device_id=(x, y, 1 - c), device_id_type=MESH)
            for a in range(len(ins)) for q in range(4)]

    def start(self, ins, outs, sems):
        for cp in self._copies(ins, outs, sems):
            cp.start()

    def finish(self, ins, outs, sems):
        for cp in self._copies(ins, outs, sems):
            cp.wait()


class _RsChips:
    def __init__(self, qs):
        n = len(qs)
        self.ins = list(qs)
        self.out_shape = [jax.ShapeDtypeStruct(q.shape, q.dtype) for q in qs]
        self.scratch = [pltpu.SemaphoreType.DMA((n, 3)), pltpu.SemaphoreType.DMA((n, 3)),
                        pltpu.SemaphoreType.DMA((n,))]

    def _copies(self, ins, outs, sems):
        send_sems, recv_sems, local_sems = sems
        n = len(ins)
        x, y, c = _place()
        my_chip = 2 * x + y
        chips = [(1 - x, y), (x, 1 - y), (1 - x, 1 - y)]

        def copy(a, j, src_slot, dst_slot):
            px, py = chips[j]
            return pltpu.make_async_remote_copy(
                src_ref=ins[a].at[src_slot], dst_ref=outs[a].at[dst_slot], send_sem=send_sems.at[a, j],
                recv_sem=recv_sems.at[a, j], device_id=(px, py, c), device_id_type=MESH)

        own = [pltpu.make_async_copy(ins[a].at[my_chip], outs[a].at[my_chip], local_sems.at[a]) for a in range(n)]
        sends = [copy(a, j, 2 * px + py, my_chip) for a in range(n) for j, (px, py) in enumerate(chips)]
        arrivals = [copy(a, j, my_chip, 2 * px + py) for a in range(n) for j, (px, py) in enumerate(chips)]
        return own, sends, arrivals

    def start(self, ins, outs, sems):
        own, sends, _ = self._copies(ins, outs, sems)
        for cp in own + sends:
            cp.start()

    def finish(self, ins, outs, sems):
        own, sends, arrivals = self._copies(ins, outs, sems)
        for cp in arrivals:
            cp.wait_recv()
        for cp in sends:
            cp.wait_send()
        for cp in own:
            cp.wait()


def _split_plans(plans, refs_in, refs_out, refs_scr, phase):
    i = o = s = 0
    for p in plans:
        ni, no, ns = len(p.ins), len(p.out_shape), len(p.scratch)
        getattr(p, phase)(refs_in[i:i + ni], refs_out[o:o + no], refs_scr[s:s + ns])
        i, o, s = i + ni, o + no, s + ns


def _plan_results(plans, res):
    out = []
    for p in plans:
        out.append(list(res[:len(p.out_shape)]))
        res = res[len(p.out_shape):]
    return out


def _exchange(plans, name):
    c_in = [a for p in plans for a in p.ins]
    c_out = [s for p in plans for s in p.out_shape]
    c_scr = [s for p in plans for s in p.scratch]

    def body(*refs):
        cin, cout, cscr = refs[:len(c_in)], refs[len(c_in):len(c_in) + len(c_out)], refs[len(c_in) + len(c_out):]
        _split_plans(plans, cin, cout, cscr, "start")
        _split_plans(plans, cin, cout, cscr, "finish")

    res = pl.pallas_call(body, name=name, in_specs=[_any()] * len(c_in), out_specs=[_any()] * len(c_out),
                         out_shape=c_out, scratch_shapes=c_scr)(*c_in)
    return _plan_results(plans, res)


def _call(body, *, name, grid, in_specs, out_specs, out_shape, args, semantics, scratch_shapes=(), plans=()):
    n_in, n_out, n_scr = len(in_specs), len(out_specs), len(scratch_shapes)
    c_in = [a for p in plans for a in p.ins]
    c_out = [s for p in plans for s in p.out_shape]
    c_scr = [s for p in plans for s in p.scratch]

    def wrapped(*refs):
        ins, refs = refs[:n_in], refs[n_in:]
        cin, refs = refs[:len(c_in)], refs[len(c_in):]
        outs, refs = refs[:n_out], refs[n_out:]
        cout, refs = refs[:len(c_out)], refs[len(c_out):]
        scr, cscr = refs[:n_scr], refs[n_scr:]
        if plans:
            ids = [pl.program_id(d) for d in range(len(grid))]
            first = functools.reduce(jnp.logical_and, [i == 0 for i in ids])
            last = functools.reduce(jnp.logical_and, [i == g - 1 for i, g in zip(ids, grid)])
            pl.when(first)(lambda: _split_plans(plans, cin, cout, cscr, "start"))
        body(*ins, *outs, *scr)
        if plans:
            pl.when(last)(lambda: _split_plans(plans, cin, cout, cscr, "finish"))

    res = pl.pallas_call(
        wrapped, name=name, grid=grid, in_specs=list(in_specs) + [_any()] * len(c_in),
        out_specs=list(out_specs) + [_any()] * len(c_out), out_shape=list(out_shape) + c_out,
        scratch_shapes=list(scratch_shapes) + c_scr,
        compiler_params=_params(*(("arbitrary",) * len(grid) if plans else semantics)),
    )(*args, *c_in)
    return list(res[:n_out]), _plan_results(plans, res[n_out:])


def _ffn_fwd(h, mods, gn, wg, wu, wd, row0, seq, tgt=None, plans=()):
    T = h.shape[0]
    tm = _token_tile(seq, 256)
    tps = seq // tm
    n_t = T // tm
    with_loss = tgt is not None

    def body(h_ref, m_ref, g_ref, wg_ref, wu_ref, wd_ref, *rest):
        if with_loss:
            tgt_ref, out_ref, y_ref, gb_ref, ub_ref, loss_ref = rest
        else:
            out_ref, y_ref, gb_ref, ub_ref = rest
        hv = h_ref[...]
        sh = m_ref[0, row0:row0 + 1, :]
        sc = m_ref[0, row0 + 1:row0 + 2, :]
        ga = m_ref[0, row0 + 2:row0 + 3, :]
        xn, _, _ = _rms_mod(hv, g_ref[...], sh, sc)
        xb = xn.astype(BF16)
        acc = jnp.zeros((tm, D), F32)
        for c0 in range(0, FF, FC):
            gg = _dot_nt(xb, wg_ref[c0:c0 + FC, :])
            uu = _dot_nt(xb, wu_ref[c0:c0 + FC, :])
            gg, uu = gg.astype(BF16), uu.astype(BF16)
            gb_ref[:, c0:c0 + FC] = gg
            ub_ref[:, c0:c0 + FC] = uu
            acc = acc + _dot((gg * jax.nn.sigmoid(gg)) * uu, wd_ref[c0:c0 + FC, :])
        y_ref[...] = acc
        hout = hv + (0.5 * ga) * acc
        if with_loss:
            d = hout - tgt_ref[...]
            out_ref[...] = d * (1.0 / D)
            loss_ref[...] = jnp.full((1, 8, 128), 0.5 / D, F32) * jnp.sum(d * d)
        else:
            out_ref[...] = hout

    tok = pl.BlockSpec((tm, D), lambda i: (i, 0))
    tokf = pl.BlockSpec((tm, FF), lambda i: (i, 0))
    in_specs = [tok, pl.BlockSpec((1, 9, D), lambda i: (i // tps, 0, 0)), pl.BlockSpec((1, D), lambda i: (0, 0)),
                _vmem(), _vmem(), _vmem()]
    out_shape = [jax.ShapeDtypeStruct((T, D), F32), jax.ShapeDtypeStruct((T, D), F32),
                 jax.ShapeDtypeStruct((T, FF), BF16), jax.ShapeDtypeStruct((T, FF), BF16)]
    out_specs = [tok, tok, tokf, tokf]
    args = [h, mods, gn, wg, wu, wd]
    if with_loss:
        in_specs.append(tok)
        args.append(tgt)
        out_shape.append(jax.ShapeDtypeStruct((n_t, 8, 128), F32))
        out_specs.append(pl.BlockSpec((1, 8, 128), lambda i: (i, 0, 0)))
    return _call(body, name="ffn_fwd_loss" if with_loss else "ffn_fwd", grid=(n_t,), in_specs=in_specs,
                 out_specs=out_specs, out_shape=out_shape, args=args, semantics=("parallel",), plans=plans)


def _ffn_bwd(h, dhn, y, gb, ub, mods, gn, wg, wu, wd, row0, seq, plans=()):
    T = h.shape[0]
    B = T // seq
    tm = _token_tile(seq, 256)
    tps = seq // tm
    n_t = T // tm

    def body(h_ref, dhn_ref, y_ref, gb_ref, ub_ref, m_ref, g_ref, wg_ref, wu_ref, wd_ref,
             dh_ref, xb_ref, dyb_ref, a_ref, dg_ref, du_ref, mg_ref):
        i = pl.program_id(0)
        hv = h_ref[...]
        dhn = dhn_ref[...]
        sh = m_ref[0, row0:row0 + 1, :]
        sc = m_ref[0, row0 + 1:row0 + 2, :]
        ga = m_ref[0, row0 + 2:row0 + 3, :]
        g = g_ref[...]
        xn, r, inv = _rms_mod(hv, g, sh, sc)
        xb_ref[...] = xn.astype(BF16)
        dyb = ((0.5 * ga) * dhn).astype(BF16)
        dyb_ref[...] = dyb
        dga = 0.5 * jnp.sum(dhn * y_ref[...], axis=0, keepdims=True)
        dxn = jnp.zeros((tm, D), F32)
        for c0 in range(0, FF, FC):
            wgc = wg_ref[c0:c0 + FC, :]
            wuc = wu_ref[c0:c0 + FC, :]
            gg = gb_ref[:, c0:c0 + FC]
            uu = ub_ref[:, c0:c0 + FC]
            sig = jax.nn.sigmoid(gg)
            s = gg * sig
            a_ref[:, c0:c0 + FC] = s * uu
            da = _dot_nt(dyb, wd_ref[c0:c0 + FC, :]).astype(BF16)
            dub = da * s
            dgb = (da * uu) * (sig * (1.0 + gg * (1.0 - sig)))
            dg_ref[:, c0:c0 + FC] = dgb
            du_ref[:, c0:c0 + FC] = dub
            dxn = dxn + _dot(dgb, wgc) + _dot(dub, wuc)
        dh, s_dxn, s_dxr = _rms_mod_bwd(dxn, r, inv, g, sc)
        dh_ref[...] = dhn + dh

        @pl.when(i % tps == 0)
        def _():
            mg_ref[...] = jnp.zeros(mg_ref.shape, F32)

        mg_ref[0, 0:1, :] += s_dxn
        mg_ref[0, 1:2, :] += s_dxr
        mg_ref[0, 2:3, :] += dga

    tok = pl.BlockSpec((tm, D), lambda i: (i, 0))
    tokf = pl.BlockSpec((tm, FF), lambda i: (i, 0))
    return _call(
        body, name="ffn_bwd", grid=(n_t,),
        in_specs=[tok, tok, tok, tokf, tokf, pl.BlockSpec((1, 9, D), lambda i: (i // tps, 0, 0)),
                  pl.BlockSpec((1, D), lambda i: (0, 0)), _vmem(), _vmem(), _vmem()],
        out_specs=[tok, tok, tok, tokf, tokf, tokf, pl.BlockSpec((1, 8, D), lambda i: (i // tps, 0, 0))],
        out_shape=[jax.ShapeDtypeStruct((T, D), F32), jax.ShapeDtypeStruct((T, D), BF16),
                   jax.ShapeDtypeStruct((T, D), BF16), jax.ShapeDtypeStruct((T, FF), BF16),
                   jax.ShapeDtypeStruct((T, FF), BF16), jax.ShapeDtypeStruct((T, FF), BF16),
                   jax.ShapeDtypeStruct((B, 8, D), F32)],
        args=(h, dhn, y, gb, ub, mods, gn, wg, wu, wd), semantics=("arbitrary",), plans=plans)


def _wgrad(a, b, name, plans=()):
    T, da = a.shape
    db = b.shape[1]
    bm = {2816: 1408, 3840: 1280}[da]
    bn = db
    tk = min(1024, T)
    nk = T // tk

    def body(a_ref, b_ref, o_ref):
        @pl.when(pl.program_id(2) == 0)
        def _():
            o_ref[...] = jnp.zeros(o_ref.shape, F32)

        o_ref[...] += _dot_tn(a_ref[...], b_ref[...])

    (out,), plan_outs = _call(
        body, name=name, grid=(da // bm, db // bn, nk),
        in_specs=[pl.BlockSpec((tk, bm), lambda i, j, k: (k, i)), pl.BlockSpec((tk, bn), lambda i, j, k: (k, j))],
        out_specs=[pl.BlockSpec((bm, bn), lambda i, j, k: (i, j))], out_shape=[jax.ShapeDtypeStruct((da, db), F32)],
        args=(a, b), semantics=("parallel", "parallel", "arbitrary"), plans=plans)
    return out, plan_outs


def _inproj_fwd(h, mods, gn, w_in, seq):
    T = h.shape[0]
    tm = _token_tile(seq)
    tps = seq // tm

    def body(h_ref, m_ref, g_ref, w_ref, *outs):
        xn, _, _ = _rms_mod(h_ref[...], g_ref[...], m_ref[0, 3:4, :], m_ref[0, 4:5, :])
        xb = xn.astype(BF16)
        for (_, c0, w), o_ref in zip(PIECES, outs):
            o_ref[...] = _dot_nt(xb, w_ref[c0:c0 + w, :])

    return pl.pallas_call(
        body, name="inproj_fwd", grid=(T // tm,),
        in_specs=[pl.BlockSpec((tm, D), lambda i: (i, 0)), pl.BlockSpec((1, 9, D), lambda i: (i // tps, 0, 0)),
                  pl.BlockSpec((1, D), lambda i: (0, 0)), _vmem()],
        out_specs=[pl.BlockSpec((tm, w), lambda i: (i, 0)) for _, _, w in PIECES],
        out_shape=[jax.ShapeDtypeStruct((T, w), F32) for _, _, w in PIECES],
        compiler_params=_params("parallel"),
    )(h, mods, gn, w_in)


def _inproj_bwd(h, dh_res, dpieces, mods, gn, w_in, seq, plans=()):
    T = h.shape[0]
    B = T // seq
    tm = _token_tile(seq, 256)
    tps = seq // tm

    def body(h_ref, dres_ref, *rest):
        dp_refs = rest[:len(PIECES)]
        m_ref, g_ref, w_ref, dh_ref, xb_ref, dpb_ref, mg_ref = rest[len(PIECES):]
        i = pl.program_id(0)
        g = g_ref[...]
        sc = m_ref[0, 4:5, :]
        xn, r, inv = _rms_mod(h_ref[...], g, m_ref[0, 3:4, :], sc)
        xb_ref[...] = xn.astype(BF16)
        dxn = jnp.zeros((tm, D), F32)
        for (_, c0, w), dp_ref in zip(PIECES, dp_refs):
            dpb = dp_ref[...].astype(BF16)
            dpb_ref[:, c0:c0 + w] = dpb
            dxn = dxn + _dot(dpb, w_ref[c0:c0 + w, :])
        dh, s_dxn, s_dxr = _rms_mod_bwd(dxn, r, inv, g, sc)
        dh_ref[...] = dres_ref[...] + dh

        @pl.when(i % tps == 0)
        def _():
            mg_ref[...] = jnp.zeros(mg_ref.shape, F32)

        mg_ref[0, 0:1, :] += s_dxn
        mg_ref[0, 1:2, :] += s_dxr

    tok = pl.BlockSpec((tm, D), lambda i: (i, 0))
    return _call(
        body, name="inproj_bwd", grid=(T // tm,),
        in_specs=[tok, tok] + [pl.BlockSpec((tm, w), lambda i: (i, 0)) for _, _, w in PIECES]
        + [pl.BlockSpec((1, 9, D), lambda i: (i // tps, 0, 0)), pl.BlockSpec((1, D), lambda i: (0, 0)), _vmem()],
        out_specs=[tok, tok, pl.BlockSpec((tm, IN_COLS), lambda i: (i, 0)),
                   pl.BlockSpec((1, 8, D), lambda i: (i // tps, 0, 0))],
        out_shape=[jax.ShapeDtypeStruct((T, D), F32), jax.ShapeDtypeStruct((T, D), BF16),
                   jax.ShapeDtypeStruct((T, IN_COLS), BF16), jax.ShapeDtypeStruct((B, 8, D), F32)],
        args=(h, dh_res, *dpieces, mods, gn, w_in), semantics=("arbitrary",), plans=plans)


def _seg_mean(x):
    i = lax.broadcasted_iota(jnp.int32, (128, 128), 0) >> 6
    j = lax.broadcasted_iota(jnp.int32, (128, 128), 1) >> 6
    return jnp.dot(x, jnp.where(i == j, 1.0 / HD, 0.0).astype(F32), precision=lax.Precision.HIGHEST,
                   preferred_element_type=F32)


def _head_norm(x, g2):
    inv = lax.rsqrt(_seg_mean(x * x) + EPS)
    y = x * inv
    return y * g2, y, inv


def _head_norm_bwd(dxn, y, inv, g2):
    dy = dxn * g2
    return inv * (dy - y * _seg_mean(dy * y)), jnp.sum(dxn * y, axis=0, keepdims=True)


def _swa_block(q, kk, vv, gq2, gk2, sinks, first, do=None):
    lo = lax.broadcasted_iota(jnp.int32, (1, 128), 1) < HD
    kn, ky, kinv = _head_norm(kk, gk2)

    def operands(x):
        xr = pltpu.roll(x, HD, 1)
        own_lo, own_hi = jnp.where(lo, x, 0.0).astype(BF16), jnp.where(lo, 0.0, x).astype(BF16)
        rolled_lo, rolled_hi = jnp.where(lo, xr, 0.0).astype(BF16), jnp.where(lo, 0.0, xr).astype(BF16)
        return (own_lo, rolled_hi), (rolled_lo, own_hi)

    def restore(parts):
        (own_lo, rolled_hi), (rolled_lo, own_hi) = parts
        return (jnp.where(lo, own_lo, own_hi)
                + pltpu.roll(jnp.where(lo, rolled_lo, rolled_hi), HD, 1))

    k_ops, v_ops = operands(kn), operands(vv)
    k2 = [jnp.concatenate(pair, axis=0) for pair in k_ops]
    def stack(x):
        return jnp.concatenate([x[:, 128 * p:128 * (p + 1)] for p in range(4)], axis=0)

    def unstack(x):
        return jnp.concatenate([x[BLK * p:BLK * (p + 1)] for p in range(4)], axis=1)

    def of_head(x, kh):
        return x[2 * BLK * kh:2 * BLK * (kh + 1)]

    nq = 4 * BLK
    pair = lax.broadcasted_iota(jnp.int32, (nq, 1), 0) >> 7
    row = lax.broadcasted_iota(jnp.int32, (nq, 2 * BLK), 0) & (BLK - 1)
    col = lax.broadcasted_iota(jnp.int32, (nq, 2 * BLK), 1)
    valid = (col <= row + BLK) & (col > row) & (col >= jnp.where(first, BLK, 0))
    scale = HD ** -0.5
    qn, qy, qinv = _head_norm(stack(q), gq2)
    qnb = qn.astype(BF16)
    s2 = jnp.concatenate([_dot_nt(of_head(qnb, kh), k2[kh]) for kh in range(N_KV)], axis=0) * scale
    probs, p_sink = [], []
    for j in range(2):
        s = jnp.where(valid, s2[:, 2 * BLK * j:2 * BLK * (j + 1)], NEG)
        sink = jnp.zeros((nq, 1), F32)
        for p in range(4):
            sink = jnp.where(pair == p, sinks[:, 2 * p + j:2 * p + j + 1], sink)
        m = jnp.maximum(jnp.max(s, axis=-1, keepdims=True), sink)
        e = jnp.exp(s - m)
        e_sink = jnp.exp(sink - m)
        rden = 1.0 / (jnp.sum(e, axis=-1, keepdims=True) + e_sink)
        probs.append(e * rden)
        p_sink.append(e_sink * rden)
    pb = [p.astype(BF16) for p in probs]
    if do is None:
        return unstack(jnp.concatenate(
            [_dot(of_head(pb[0], kh), v_ops[kh][0]) + _dot(of_head(pb[1], kh), v_ops[kh][1]) for kh in range(N_KV)],
            axis=0))
    dob = stack(do).astype(BF16)
    ds, dsinks = [], [None] * 8
    for j in range(2):
        dp = jnp.concatenate([_dot_nt(of_head(dob, kh), v_ops[kh][j]) for kh in range(N_KV)], axis=0)
        t = jnp.sum(dp * probs[j], axis=-1, keepdims=True)
        ds.append(probs[j] * (dp - t) * scale)
        lost = p_sink[j] * t
        for p in range(4):
            dsinks[2 * p + j] = -jnp.sum(lost[BLK * p:BLK * (p + 1)])
    dsb = jnp.concatenate(ds, axis=1).astype(BF16)
    dqn = jnp.concatenate([_dot(of_head(dsb, kh), k2[kh]) for kh in range(N_KV)], axis=0)
    dq, dgq2 = _head_norm_bwd(dqn, qy, qinv, gq2)
    dk2 = [_dot_tn(of_head(dsb, kh), of_head(qnb, kh)) for kh in range(N_KV)]
    dv_ops = [[_dot_tn(of_head(pb[j], kh), of_head(dob, kh)) for j in range(2)] for kh in range(N_KV)]
    dkn = restore(tuple((d[:2 * BLK], d[2 * BLK:]) for d in dk2))
    dvv = restore(tuple(tuple(d) for d in dv_ops))
    dkk, dgk2 = _head_norm_bwd(dkn, ky, kinv, gk2)
    return unstack(dq), dkk, dvv, dgq2, dgk2, dsinks


SWA_GROUP = 2


def _swa_specs(nbs):
    grp = min(SWA_GROUP, nbs)
    rows = grp * BLK

    def tok(w):
        return pl.BlockSpec((rows, w), lambda i: (i, 0))

    halo = pl.BlockSpec((BLK, 128), lambda i: (jnp.maximum(i * grp - 1, 0), 0))
    vec = pl.BlockSpec((1, 128), lambda i: (0, 0))
    sk = pl.BlockSpec((1, 8), lambda i: (0, 0))
    return grp, tok, halo, vec, sk


def _swa_fwd(q, k, v, gq2, gk2, sinks, nbs, plans=()):
    T = q.shape[0]
    grp, tok, halo, vec, sk = _swa_specs(nbs)

    def body(q_ref, kh_ref, kc_ref, vh_ref, vc_ref, gq_ref, gk_ref, sk_ref, o_ref):
        seq_start = ((pl.program_id(0) * grp) % nbs) == 0
        for g in range(grp):
            rows = slice(g * BLK, (g + 1) * BLK)
            prev = slice((g - 1) * BLK, g * BLK)
            kk = jnp.concatenate([kh_ref[...] if g == 0 else kc_ref[prev, :], kc_ref[rows, :]], axis=0)
            vv = jnp.concatenate([vh_ref[...] if g == 0 else vc_ref[prev, :], vc_ref[rows, :]], axis=0)
            o_ref[rows, :] = _swa_block(q_ref[rows, :], kk, vv, gq_ref[...], gk_ref[...], sk_ref[...],
                                        seq_start if g == 0 else False)

    (out,), plan_outs = _call(
        body, name="swa_fwd", grid=(T // (grp * BLK),),
        in_specs=[tok(D_B), halo, tok(128), halo, tok(128), vec, vec, sk], out_specs=[tok(D_B)],
        out_shape=[jax.ShapeDtypeStruct((T, D_B), F32)], args=(q, k, k, v, v, gq2, gk2, sinks),
        semantics=("parallel",), plans=plans)
    return out, plan_outs


def _swa_bwd(q, k, v, gq2, gk2, sinks, do, nbs, plans=()):
    T = q.shape[0]
    grp, tok, halo, vec, sk = _swa_specs(nbs)
    steps = T // (grp * BLK)

    def body(q_ref, kh_ref, kc_ref, vh_ref, vc_ref, gq_ref, gk_ref, sk_ref, do_ref,
             dq_ref, dk_ref, dv_ref, dkh_ref, dvh_ref, dgq_ref, dgk_ref, dsk_ref):
        i = pl.program_id(0)
        seq_start = ((i * grp) % nbs) == 0

        @pl.when(i == 0)
        def _():
            for r in (dgq_ref, dgk_ref, dsk_ref):
                r[...] = jnp.zeros(r.shape, F32)

        res = []
        for g in range(grp):
            rows = slice(g * BLK, (g + 1) * BLK)
            prev = slice((g - 1) * BLK, g * BLK)
            kk = jnp.concatenate([kh_ref[...] if g == 0 else kc_ref[prev, :], kc_ref[rows, :]], axis=0)
            vv = jnp.concatenate([vh_ref[...] if g == 0 else vc_ref[prev, :], vc_ref[rows, :]], axis=0)
            res.append(_swa_block(q_ref[rows, :], kk, vv, gq_ref[...], gk_ref[...], sk_ref[...],
                                  seq_start if g == 0 else False, do=do_ref[rows, :]))
        lane = lax.broadcasted_iota(jnp.int32, (8, 128), 1)
        upd = jnp.zeros((8, 128), F32)
        for g, (dq, dkk, dvv, dgq2, dgk2, dsinks) in enumerate(res):
            rows = slice(g * BLK, (g + 1) * BLK)
            dq_ref[rows, :] = dq
            dk_ref[rows, :] = dkk[BLK:] + res[g + 1][1][:BLK] if g + 1 < grp else dkk[BLK:]
            dv_ref[rows, :] = dvv[BLK:] + res[g + 1][2][:BLK] if g + 1 < grp else dvv[BLK:]
            dgq_ref[...] += dgq2
            dgk_ref[...] += dgk2
            for h, d in enumerate(dsinks):
                upd = upd + jnp.where(lane == h, d, 0.0)
        dkh_ref[0] = res[0][1][:BLK]
        dvh_ref[0] = res[0][2][:BLK]
        dsk_ref[...] += upd

    one = pl.BlockSpec((1, BLK, 128), lambda i: (i, 0, 0))
    halo_shape = jax.ShapeDtypeStruct((steps, BLK, 128), F32)
    return _call(
        body, name="swa_bwd", grid=(steps,),
        in_specs=[tok(D_B), halo, tok(128), halo, tok(128), vec, vec, sk, tok(D_B)],
        out_specs=[tok(D_B), tok(128), tok(128), one, one, vec, vec, pl.BlockSpec((8, 128), lambda i: (0, 0))],
        out_shape=[jax.ShapeDtypeStruct((T, D_B), F32), jax.ShapeDtypeStruct((T, 128), F32),
                   jax.ShapeDtypeStruct((T, 128), F32), halo_shape, halo_shape, jax.ShapeDtypeStruct((1, 128), F32),
                   jax.ShapeDtypeStruct((1, 128), F32), jax.ShapeDtypeStruct((8, 128), F32)],
        args=(q, k, k, v, v, gq2, gk2, sinks, do), semantics=("arbitrary",), plans=plans)


def _swa_add_halo(dk, dk_halo):
    steps = dk_halo.shape[0]
    nxt = jnp.concatenate([dk_halo[1:], jnp.zeros_like(dk_halo[:1])], axis=0)[:, None]
    dk = dk.reshape(steps, -1, BLK, 128)
    return jnp.concatenate([dk[:, :-1], dk[:, -1:] + nxt], axis=1).reshape(-1, 128)


def _sgu_norm(av, g_ln, b_ln):
    t, th = _gelu(av)
    mu = jnp.mean(t, axis=-1, keepdims=True)
    tc = t - mu
    rstd = lax.rsqrt(jnp.mean(tc * tc, axis=-1, keepdims=True) + EPS)
    vhat = tc * rstd
    return vhat * g_ln + b_ln, vhat, rstd, th


def _masked_ws(ws_ref):
    tril = lax.broadcasted_iota(jnp.int32, (BLK, BLK), 0) >= lax.broadcasted_iota(jnp.int32, (BLK, BLK), 1)
    return [jnp.where(tril, ws_ref[g], 0.0).astype(BF16) for g in range(N_GRP)]


def _mix_fwd(au, av, gta, gtb, ob, h, mods, g_ln, b_ln, ws, bsb, wa, wb, wout, seq):
    T = h.shape[0]
    tm = _token_tile(seq)
    tps = seq // tm

    def body(au_ref, av_ref, gta_ref, gtb_ref, ob_ref, h_ref, m_ref, gl_ref, bl_ref, ws_ref, bs_ref,
             wa_ref, wb_ref, wo_ref, out_ref, vvb_s, z_s):
        u, _ = _gelu(au_ref[...])
        vv, _, _, _ = _sgu_norm(av_ref[...], gl_ref[...], bl_ref[...])
        vvb_s[...] = vv.astype(BF16)
        wsm = _masked_ws(ws_ref)
        for c in range(tm // BLK):
            rows = slice(c * BLK, (c + 1) * BLK)
            for g in range(N_GRP):
                cols = slice(g * BLK, (g + 1) * BLK)
                z_s[rows, cols] = _dot(wsm[g], vvb_s[rows, cols]) + bs_ref[g]
        ya = _dot_nt((u * z_s[...]).astype(BF16), wa_ref[...])
        yb = _dot_nt(ob_ref[...].astype(BF16), wb_ref[...])
        merged = jax.nn.sigmoid(gta_ref[...]) * ya + jax.nn.sigmoid(gtb_ref[...]) * yb
        out_ref[...] = h_ref[...] + m_ref[0, 5:6, :] * _dot(merged.astype(BF16), wo_ref[...])

    def tok(w):
        return pl.BlockSpec((tm, w), lambda i: (i, 0))

    def full(shape):
        return pl.BlockSpec(shape, lambda i: (0,) * len(shape))

    return pl.pallas_call(
        body, name="mix_fwd", grid=(T // tm,),
        in_specs=[tok(D_A), tok(D_A), tok(D), tok(D), tok(D_B), tok(D),
                  pl.BlockSpec((1, 9, D), lambda i: (i // tps, 0, 0)), full((1, D_A)), full((1, D_A)),
                  full((N_GRP, BLK, BLK)), full((N_GRP, BLK, BLK)), _vmem(), _vmem(), _vmem()],
        out_specs=tok(D), out_shape=jax.ShapeDtypeStruct((T, D), F32),
        scratch_shapes=[pltpu.VMEM((tm, D_A), BF16), pltpu.VMEM((tm, D_A), F32)],
        compiler_params=_params("parallel"),
    )(au, av, gta, gtb, ob, h, mods, g_ln, b_ln, ws, bsb, wa, wb, wout)


def _mix_bwd(au, av, gta, gtb, ob, dh, mods, g_ln, b_ln, ws, bsb, wa, wb, wout, seq, plans=()):
    T = dh.shape[0]
    B = T // seq
    tm = _token_tile(seq, 256)
    tps = seq // tm

    def body(au_ref, av_ref, gta_ref, gtb_ref, ob_ref, dh_ref, m_ref, gl_ref, bl_ref, ws_ref, bs_ref,
             wa_ref, wb_ref, wo_ref,
             dau_ref, dav_ref, dgta_ref, dgtb_ref, dob_ref, dwo_ref, dwa_ref, dwb_ref, dws_ref, dbs_ref, dln_ref,
             mg_ref, vvb_s, z_s, dz_s, dzb_s, dvv_s):
        i = pl.program_id(0)

        @pl.when(i == 0)
        def _():
            for r in (dwo_ref, dwa_ref, dwb_ref, dws_ref, dbs_ref, dln_ref):
                r[...] = jnp.zeros(r.shape, F32)

        @pl.when(i % tps == 0)
        def _():
            mg_ref[...] = jnp.zeros(mg_ref.shape, F32)

        auv = au_ref[...]
        avv = av_ref[...]
        u, thu = _gelu(auv)
        g_ln = gl_ref[...]
        vv, vhat, rstd, thv = _sgu_norm(avv, g_ln, bl_ref[...])
        vvb_s[...] = vv.astype(BF16)
        wsm = _masked_ws(ws_ref)
        for c in range(tm // BLK):
            rows = slice(c * BLK, (c + 1) * BLK)
            for g in range(N_GRP):
                cols = slice(g * BLK, (g + 1) * BLK)
                z_s[rows, cols] = _dot(wsm[g], vvb_s[rows, cols]) + bs_ref[g]
        z = z_s[...]
        yab = (u * z).astype(BF16)
        obb = ob_ref[...].astype(BF16)
        ya = _dot_nt(yab, wa_ref[...])
        yb = _dot_nt(obb, wb_ref[...])
        sa = jax.nn.sigmoid(gta_ref[...])
        sb = jax.nn.sigmoid(gtb_ref[...])
        mb = (sa * ya + sb * yb).astype(BF16)
        dhv = dh_ref[...]
        mg_ref[0, 0:1, :] += jnp.sum(dhv * _dot(mb, wo_ref[...]), axis=0, keepdims=True)
        dmob = (m_ref[0, 5:6, :] * dhv).astype(BF16)
        dwo_ref[...] += _dot_tn(mb, dmob)
        dmerged = _dot_nt(dmob, wo_ref[...])
        dya = dmerged * sa
        dyb = dmerged * sb
        dgta_ref[...] = dya * ya * (1.0 - sa)
        dgtb_ref[...] = dyb * yb * (1.0 - sb)
        dyab = dya.astype(BF16)
        dybb = dyb.astype(BF16)
        dwa_ref[...] += _dot_tn(dyab, yab)
        dwb_ref[...] += _dot_tn(dybb, obb)
        dob_ref[...] = _dot(dybb, wb_ref[...])
        dyap = _dot(dyab, wa_ref[...])
        dau_ref[...] = (dyap * z) * _gelu_grad(auv, thu)
        dz = dyap * u
        dz_s[...] = dz
        dzb_s[...] = dz.astype(BF16)
        for c in range(tm // BLK):
            rows = slice(c * BLK, (c + 1) * BLK)
            for g in range(N_GRP):
                cols = slice(g * BLK, (g + 1) * BLK)
                dzb = dzb_s[rows, cols]
                dvv_s[rows, cols] = _dot_tn(wsm[g], dzb)
                dws_ref[g] += _dot_nt(dzb, vvb_s[rows, cols])
                dbs_ref[g] += dz_s[rows, cols]
        dvv = dvv_s[...]
        dln_ref[0:1, :] += jnp.sum(dvv * vhat, axis=0, keepdims=True)
        dln_ref[1:2, :] += jnp.sum(dvv, axis=0, keepdims=True)
        dvh = dvv * g_ln
        dt = rstd * (dvh - jnp.mean(dvh, axis=-1, keepdims=True)
                     - vhat * jnp.mean(dvh * vhat, axis=-1, keepdims=True))
        dav_ref[...] = dt * _gelu_grad(avv, thv)

    def tok(w):
        return pl.BlockSpec((tm, w), lambda i: (i, 0))

    def full(shape):
        return pl.BlockSpec(shape, lambda i: (0,) * len(shape))

    return _call(
        body, name="mix_bwd", grid=(T // tm,),
        in_specs=[tok(D_A), tok(D_A), tok(D), tok(D), tok(D_B), tok(D),
                  pl.BlockSpec((1, 9, D), lambda i: (i // tps, 0, 0)), full((1, D_A)), full((1, D_A)),
                  full((N_GRP, BLK, BLK)), full((N_GRP, BLK, BLK)), _vmem(), _vmem(), _vmem()],
        out_specs=[tok(D_A), tok(D_A), tok(D), tok(D), tok(D_B), full((D, D)), full((D, D_A)), full((D, D_B)),
                   full((N_GRP, BLK, BLK)), full((N_GRP, BLK, BLK)), full((8, D_A)),
                   pl.BlockSpec((1, 8, D), lambda i: (i // tps, 0, 0))],
        out_shape=[jax.ShapeDtypeStruct((T, D_A), F32), jax.ShapeDtypeStruct((T, D_A), F32),
                   jax.ShapeDtypeStruct((T, D), F32), jax.ShapeDtypeStruct((T, D), F32),
                   jax.ShapeDtypeStruct((T, D_B), F32), jax.ShapeDtypeStruct((D, D), F32),
                   jax.ShapeDtypeStruct((D, D_A), F32), jax.ShapeDtypeStruct((D, D_B), F32),
                   jax.ShapeDtypeStruct((N_GRP, BLK, BLK), F32), jax.ShapeDtypeStruct((N_GRP, BLK, BLK), F32),
                   jax.ShapeDtypeStruct((8, D_A), F32), jax.ShapeDtypeStruct((B, 8, D), F32)],
        scratch_shapes=[pltpu.VMEM((tm, D_A), BF16), pltpu.VMEM((tm, D_A), F32), pltpu.VMEM((tm, D_A), F32),
                        pltpu.VMEM((tm, D_A), BF16), pltpu.VMEM((tm, D_A), F32)],
        args=(au, av, gta, gtb, ob, dh, mods, g_ln, b_ln, ws, bsb, wa, wb, wout), semantics=("arbitrary",),
        plans=plans)


def _ada_fwd(c_all, w_ada):
    def body(c_ref, w_ref, o_ref):
        cv = c_ref[...]
        o_ref[...] = _dot((cv * jax.nn.sigmoid(cv)).astype(BF16), w_ref[...].astype(BF16))

    out = jax.ShapeDtypeStruct((c_all.shape[0], w_ada.shape[1]), F32)
    return pl.pallas_call(body, name="ada_fwd", grid=(1,), in_specs=[_whole(c_all), _whole(w_ada)],
                          out_specs=_whole(out), out_shape=out, compiler_params=_params("arbitrary"))(c_all, w_ada)


def _ada_update(c_all, dm_cols, w, m, v, plans=()):
    n, cols = c_all.shape[0], w.shape[1]

    def body(c_ref, dm_ref, w_ref, m_ref, v_ref, g_ref, d_ref, nm_ref, nv_ref):
        cv = c_ref[...]
        g = _dot_tn((cv * jax.nn.sigmoid(cv)).astype(BF16), dm_ref[...].astype(BF16))
        g_ref[...] = g
        d_ref[...], nm_ref[...], nv_ref[...] = _adamw(w_ref[...], g, m_ref[...], v_ref[...])

    col = pl.BlockSpec((D, 128), lambda j: (0, j))
    return _call(
        body, name="ada_update", grid=(cols // 128,),
        in_specs=[pl.BlockSpec((n, D), lambda j: (0, 0)), pl.BlockSpec((n, 128), lambda j: (0, j)), col, col, col],
        out_specs=[col] * 4, out_shape=[jax.ShapeDtypeStruct(w.shape, F32)] * 4, args=(c_all, dm_cols, w, m, v),
        semantics=("parallel",), plans=plans)


def _mod_finish(mg1, mg2, mg2g, mg3, mods, g1, g2, g3):
    B = mods.shape[0]

    def body(mg1_ref, mg2_ref, mg2g_ref, mg3_ref, m_ref, g1_ref, g2_ref, g3_ref, dm_ref, dgn_ref):
        dgn_ref[...] = jnp.zeros(dgn_ref.shape, F32)
        for k, (mg, g_ref) in enumerate(((mg1_ref, g1_ref), (mg2_ref, g2_ref), (mg3_ref, g3_ref))):
            for b in range(B):
                s_dxr = mg[b, 1:2, :]
                dm_ref[b, 3 * k:3 * k + 1, :] = mg[b, 0:1, :]
                dm_ref[b, 3 * k + 1:3 * k + 2, :] = g_ref[...] * s_dxr
                dm_ref[b, 3 * k + 2:3 * k + 3, :] = mg2g_ref[b, 0:1, :] if k == 1 else mg[b, 2:3, :]
                dgn_ref[k:k + 1, :] += (1.0 + m_ref[b, 3 * k + 1:3 * k + 2, :]) * s_dxr

    args = (mg1, mg2, mg2g, mg3, mods, g1, g2, g3)
    out_shape = [jax.ShapeDtypeStruct((B, 9, D), F32), jax.ShapeDtypeStruct((8, D), F32)]
    return pl.pallas_call(body, name="mod_finish", grid=(1,), in_specs=[_whole(a) for a in args],
                          out_specs=[_whole(o) for o in out_shape], out_shape=out_shape,
                          compiler_params=_params("arbitrary"))(*args)


def _small_update(gathered, params, ms, vs):
    n = len(SMALL)
    B = gathered[0].shape[1]

    def body(*refs):
        gdm, ggn, gln, gws, gbs, ggq, ggk, gsk, gloss = refs[:9]
        w = dict(zip(SMALL, refs[9:9 + n]))
        m = dict(zip(SMALL, refs[9 + n:9 + 2 * n]))
        v = dict(zip(SMALL, refs[9 + 2 * n:9 + 3 * n]))
        outs = refs[9 + 3 * n:-1]
        out = {name: outs[4 * k:4 * k + 4] for k, name in enumerate(SMALL)}

        def total(ref, idx):
            acc = ref[(0,) + idx]
            for dev in range(1, N_DEV):
                acc = acc + ref[(dev,) + idx]
            return acc

        def finish(name, g, idx=(Ellipsis,)):
            d, nm, nv = _adamw(w[name][idx], g, m[name][idx], v[name][idx])
            for ref, val in zip(out[name], (g, d, nm, nv)):
                ref[idx] = val

        g_bada = total(gdm, (slice(0, 1),))
        for b in range(1, B):
            g_bada = g_bada + total(gdm, (slice(b, b + 1),))
        finish("b_ada", g_bada)
        finish("g_norm1", total(ggn, (slice(0, 1),)))
        finish("g_norm2", total(ggn, (slice(1, 2),)))
        finish("g_norm3", total(ggn, (slice(2, 3),)))
        finish("g_sgu_ln", total(gln, (slice(0, 1),)))
        finish("b_sgu_ln", total(gln, (slice(1, 2),)))
        tril = lax.broadcasted_iota(jnp.int32, (BLK, BLK), 0) >= lax.broadcasted_iota(jnp.int32, (BLK, BLK), 1)
        for g in range(N_GRP):
            finish("w_spatial", jnp.where(tril, total(gws, (g,)), 0.0), (0, g))
            finish("b_spatial", jnp.sum(total(gbs, (g,)).T, axis=0, keepdims=True), (0, slice(g, g + 1)))
        finish("g_q", total(ggq, ()))
        finish("g_k", total(ggk, ()))
        finish("attn_sinks", total(gsk, (slice(0, 1), slice(0, N_KV * Q_PER_KV))))
        refs[-1][...] = total(gloss, ())

    args = list(gathered) + [params[k] for k in SMALL] + [ms[k] for k in SMALL] + [vs[k] for k in SMALL]
    out_shape = []
    for k in SMALL:
        out_shape += [jax.ShapeDtypeStruct(params[k].shape, F32)] * 4
    out_shape.append(jax.ShapeDtypeStruct((8, 128), F32))
    res = pl.pallas_call(body, name="small_update", grid=(1,), in_specs=[_whole(a) for a in args],
                         out_specs=[_whole(o) for o in out_shape], out_shape=out_shape,
                         compiler_params=_params("arbitrary"))(*args)
    return {k: res[4 * i:4 * i + 4] for i, k in enumerate(SMALL)}, res[-1][0, 0]


def _rs_add(p, r, core, name):
    _, rows, width = p.shape

    def body(c_ref, p_ref, r_ref, o_ref):
        o_ref[...] = (p_ref[...] + r_ref[...]).astype(BF16)

    return pl.pallas_call(
        body, name="rs_add_" + name, out_shape=jax.ShapeDtypeStruct((4, rows, width), BF16),
        grid_spec=pltpu.PrefetchScalarGridSpec(
            num_scalar_prefetch=1, grid=(4,),
            in_specs=[pl.BlockSpec((1, rows, width), lambda k, c_ref: (2 * k + c_ref[0], 0, 0)),
                      pl.BlockSpec((1, rows, width), lambda k, c_ref: (k, 0, 0))],
            out_specs=pl.BlockSpec((1, rows, width), lambda k, c_ref: (k, 0, 0))),
        compiler_params=_params("parallel"),
    )(core, p, r)


def _adam_big(w, m, v, r, name):
    rows, cols = w.shape

    def body(r_ref, w_ref, m_ref, v_ref, go_ref, d_ref, nm_ref, nv_ref):
        g = r_ref[0].astype(F32)
        for k in range(1, 4):
            g = g + r_ref[k].astype(F32)
        go_ref[...] = g
        d_ref[...], nm_ref[...], nv_ref[...] = _adamw(w_ref[...], g, m_ref[...], v_ref[...])

    br = rows // 2
    shard = pl.BlockSpec((br, cols), lambda i: (i, 0))
    return pl.pallas_call(body, name="adam_" + name, grid=(2,),
                          in_specs=[pl.BlockSpec((4, br, cols), lambda i: (0, i, 0)), shard, shard, shard],
                          out_specs=[shard] * 4, out_shape=[jax.ShapeDtypeStruct(w.shape, F32)] * 4,
                          compiler_params=_params("parallel"))(r, w, m, v)


def kernel(x, c, w_ada, b_ada, g_norm1, ffn1_w_gate, ffn1_w_up, ffn1_w_down, g_norm2, w_in, g_sgu_ln, b_sgu_ln, w_spatial, b_spatial, g_q, g_k, attn_sinks, w_branch_a, w_branch_b, w_out, g_norm3, ffn2_w_gate, ffn2_w_up, ffn2_w_down, loss_target, m_w_ada, m_b_ada, m_g_norm1, m_ffn1_w_gate, m_ffn1_w_up, m_ffn1_w_down, m_g_norm2, m_w_in, m_g_sgu_ln, m_b_sgu_ln, m_w_spatial, m_b_spatial, m_g_q, m_g_k, m_attn_sinks, m_w_branch_a, m_w_branch_b, m_w_out, m_g_norm3, m_ffn2_w_gate, m_ffn2_w_up, m_ffn2_w_down, v_w_ada, v_b_ada, v_g_norm1, v_ffn1_w_gate, v_ffn1_w_up, v_ffn1_w_down, v_g_norm2, v_w_in, v_g_sgu_ln, v_b_sgu_ln, v_w_spatial, v_b_spatial, v_g_q, v_g_k, v_attn_sinks, v_w_branch_a, v_w_branch_b, v_w_out, v_g_norm3, v_ffn2_w_gate, v_ffn2_w_up, v_ffn2_w_down):
    names = ("w_ada", "b_ada", "g_norm1", "ffn1_w_gate", "ffn1_w_up", "ffn1_w_down", "g_norm2", "w_in", "g_sgu_ln",
             "b_sgu_ln", "w_spatial", "b_spatial", "g_q", "g_k", "attn_sinks", "w_branch_a", "w_branch_b", "w_out",
             "g_norm3", "ffn2_w_gate", "ffn2_w_up", "ffn2_w_down")
    w = dict(zip(names, (w_ada, b_ada, g_norm1, ffn1_w_gate, ffn1_w_up, ffn1_w_down, g_norm2, w_in, g_sgu_ln,
                         b_sgu_ln, w_spatial, b_spatial, g_q, g_k, attn_sinks, w_branch_a, w_branch_b, w_out, g_norm3,
                         ffn2_w_gate, ffn2_w_up, ffn2_w_down)))
    m = dict(zip(names, (m_w_ada, m_b_ada, m_g_norm1, m_ffn1_w_gate, m_ffn1_w_up, m_ffn1_w_down, m_g_norm2, m_w_in,
                         m_g_sgu_ln, m_b_sgu_ln, m_w_spatial, m_b_spatial, m_g_q, m_g_k, m_attn_sinks, m_w_branch_a,
                         m_w_branch_b, m_w_out, m_g_norm3, m_ffn2_w_gate, m_ffn2_w_up, m_ffn2_w_down)))
    v = dict(zip(names, (v_w_ada, v_b_ada, v_g_norm1, v_ffn1_w_gate, v_ffn1_w_up, v_ffn1_w_down, v_g_norm2, v_w_in,
                         v_g_sgu_ln, v_b_sgu_ln, v_w_spatial, v_b_spatial, v_g_q, v_g_k, v_attn_sinks, v_w_branch_a,
                         v_w_branch_b, v_w_out, v_g_norm3, v_ffn2_w_gate, v_ffn2_w_up, v_ffn2_w_down)))
    B, seq, _ = x.shape
    T = B * seq
    nbs = seq // BLK
    xi, yi, ci = _place()
    me = 4 * xi + 2 * yi + ci
    core = jnp.reshape(ci, (1,)).astype(jnp.int32)
    layout = {name: (tform, n, width) for name, tform, n, width in BIG}
    shard = {name: w[name][0].astype(BF16).T if tform else w[name][0].astype(BF16) for name, tform, _, _ in BIG}
    wts, parts, big_out = {}, {}, {}

    def gather_plan(group):
        return _Gather([shard[k] for k in group])

    def take(group, gathered):
        for k, g in zip(group, gathered):
            wts[k] = g.reshape(N_DEV * layout[k][1], layout[k][2])

    def blocks(name, grad):
        return grad.reshape(N_DEV, layout[name][1], layout[name][2])

    def to_sibling(names, grads):
        return _RsSibling([blocks(k, g) for k, g in zip(names, grads)])

    def add(names, grads, from_sibling):
        for k, g, r in zip(names, grads, from_sibling):
            parts[k] = _rs_add(blocks(k, g), r, core, k)

    def to_chips(names):
        return _RsChips([parts[k] for k in names])

    def update(names, from_chips):
        for k, r in zip(names, from_chips):
            if layout[k][0]:
                big_out[k] = [o.T[None] for o in _adam_big(w[k][0].T, m[k][0].T, v[k][0].T, r, k)]
            else:
                big_out[k] = [o[None] for o in _adam_big(w[k][0], m[k][0], v[k][0], r, k)]

    g1, g2, g3, g_ln, b_ln = g_norm1, g_norm2, g_norm3, g_sgu_ln, b_sgu_ln
    gq2, gk2 = jnp.tile(g_q, (1, 2)), jnp.tile(g_k, (1, 2))
    ws = w_spatial[0]
    bsb = jnp.broadcast_to(b_spatial[0][:, :, None], (N_GRP, BLK, BLK))
    xf = x.reshape(T, D)
    tgt = loss_target.reshape(T, D)

    ((c_all,),) = _exchange([_Gather([c])], "gather_c")
    c_all = c_all.reshape(N_DEV * B, D)
    ffn1 = ("ffn1_w_gate", "ffn1_w_up", "ffn1_w_down")
    ffn2 = ("ffn2_w_gate", "ffn2_w_up", "ffn2_w_down")
    ((mods_cols, *gathered),) = _exchange(
        [_Gather([_ada_fwd(c_all, w_ada[0])] + [shard[k] for k in ffn1])], "gather_first")
    take(ffn1, gathered)
    mine = lax.dynamic_slice_in_dim(mods_cols, B * me, B, axis=1)
    mods = (mine.transpose(1, 0, 2).reshape(B, 9 * D) + b_ada).reshape(B, 9, D)

    group = ("w_in", "w_branch_a", "w_branch_b", "w_out", "ffn2_w_gate")
    (h1, y1, gb1, ub1), (gathered,) = _ffn_fwd(xf, mods, g1, *[wts[k] for k in ffn1], 0, seq,
                                               plans=[gather_plan(group)])
    take(group, gathered)
    au, av, q_tok, k_tok, v_tok, gta, gtb = _inproj_fwd(h1, mods, g2, wts["w_in"], seq)
    group = ("ffn2_w_up", "ffn2_w_down")
    ob, (gathered,) = _swa_fwd(q_tok, k_tok, v_tok, gq2, gk2, attn_sinks, nbs, plans=[gather_plan(group)])
    take(group, gathered)
    mixw = (wts["w_branch_a"], wts["w_branch_b"], wts["w_out"])
    h2 = _mix_fwd(au, av, gta, gtb, ob, h1, mods, g_ln, b_ln, ws, bsb, *mixw, seq)
    (dh3, y3, gb3, ub3, loss_parts), _ = _ffn_fwd(h2, mods, g3, *[wts[k] for k in ffn2], 6, seq, tgt=tgt)
    loss_part = jnp.full((8, 128), jnp.sum(loss_parts[:, 0, 0]))

    (dh2, xb, dyb, a, dg, du, mg3), _ = _ffn_bwd(h2, dh3, y3, gb3, ub3, mods, g3, *[wts[k] for k in ffn2], 6, seq)
    d_gate, _ = _wgrad(dg, xb, "wgrad_ffn2_gate")
    d_up, (r,) = _wgrad(du, xb, "wgrad_ffn2_up", plans=[to_sibling(ffn2[:1], [d_gate])])
    add(ffn2[:1], [d_gate], r)
    d_down, (r,) = _wgrad(a, dyb, "wgrad_ffn2_down", plans=[to_sibling(ffn2[1:2], [d_up])])
    add(ffn2[1:2], [d_up], r)

    (dau, dav, dgta, dgtb, dob, d_out, d_a, d_b, dws, dbs, dln, mg2g), (r, from_chips) = _mix_bwd(
        au, av, gta, gtb, ob, dh2, mods, g_ln, b_ln, ws, bsb, *mixw, seq,
        plans=[to_sibling(ffn2[2:], [d_down]), to_chips(ffn2[:2])])
    add(ffn2[2:], [d_down], r)
    update(ffn2[:2], from_chips)
    mixers = ("w_out", "w_branch_a", "w_branch_b")
    (dq, dk, dv, dk_halo, dv_halo, dgq2, dgk2, dsk), (from_chips, r) = _swa_bwd(
        q_tok, k_tok, v_tok, gq2, gk2, attn_sinks, dob, nbs,
        plans=[to_chips(ffn2[2:]), to_sibling(mixers, [d_out, d_a, d_b])])
    update(ffn2[2:], from_chips)
    add(mixers, [d_out, d_a, d_b], r)
    dk, dv = _swa_add_halo(dk, dk_halo), _swa_add_halo(dv, dv_halo)
    dgq, dgk = dgq2[:, :HD] + dgq2[:, HD:], dgk2[:, :HD] + dgk2[:, HD:]
    (dh1, xb2, dpb, mg2), (from_chips, early) = _inproj_bwd(
        h1, dh2, (dau, dav, dq, dk, dv, dgta, dgtb), mods, g2, wts["w_in"], seq,
        plans=[to_chips(mixers), _Gather([dln, dws, dbs, dgq, dgk, dsk, loss_part])])
    update(mixers, from_chips)

    (dx, xb, dyb, a, dg, du, mg1), _ = _ffn_bwd(xf, dh1, y1, gb1, ub1, mods, g1, *[wts[k] for k in ffn1], 0, seq)
    dmods, dgn = _mod_finish(mg1, mg2, mg2g, mg3, mods, g1, g2, g3)
    d_gate, (late,) = _wgrad(dg, xb, "wgrad_ffn1_gate", plans=[_Gather([dmods.reshape(B, 9 * D), dgn])])
    gathered = late + early
    d_up, (r,) = _wgrad(du, xb, "wgrad_ffn1_up", plans=[to_sibling(ffn1[:1], [d_gate])])
    add(ffn1[:1], [d_gate], r)
    d_in, (r, from_chips) = _wgrad(dpb, xb2, "wgrad_w_in",
                                   plans=[to_sibling(ffn1[1:2], [d_up]), to_chips(ffn1[:1])])
    add(ffn1[1:2], [d_up], r)
    update(ffn1[:1], from_chips)
    d_down, (r, from_chips) = _wgrad(a, dyb, "wgrad_ffn1_down",
                                     plans=[to_sibling(("w_in",), [d_in]), to_chips(ffn1[1:2])])
    add(("w_in",), [d_in], r)
    update(ffn1[1:2], from_chips)
    small_out, loss = _small_update(gathered, w, m, v)
    dm_cols = lax.dynamic_slice_in_dim(gathered[0].reshape(N_DEV * B, 9 * D), (9 * D // N_DEV) * me,
                                       9 * D // N_DEV, axis=1)
    ada_out, (r, from_chips) = _ada_update(c_all, dm_cols, w_ada[0], m_w_ada[0], v_w_ada[0],
                                           plans=[to_sibling(ffn1[2:], [d_down]), to_chips(("w_in",))])
    add(ffn1[2:], [d_down], r)
    update(("w_in",), from_chips)
    (from_chips,) = _exchange([to_chips(ffn1[2:])], "rs_last")
    update(ffn1[2:], from_chips)

    def leaf(kind, name):
        if name == "w_ada":
            return ada_out[kind][None]
        if name in SMALL:
            return small_out[name][kind]
        return big_out[name][kind]

    return (loss, dx.reshape(B, seq, D), *[leaf(kind, name) for kind in range(4) for name in names])
```

```python
import functools
import math

import jax
import jax.numpy as jnp
from jax import lax
from jax.experimental import pallas as pl
from jax.experimental.pallas import tpu as pltpu

F32 = jnp.float32
BF16 = jnp.bfloat16
MESH = pl.DeviceIdType.MESH
AXES = ("x", "y", "c")
N_DEV = 8

VMEM_LIMIT = 56 * 1024 * 1024

D = 1024
FF = 2816
FC = 1408
D_A = 512
D_B = 512
HD = 64
N_KV = 2
Q_PER_KV = 4
BLK = 128
N_GRP = 4
IN_COLS = 3840
PIECES = (("au", 0, 512), ("av", 512, 512), ("q", 1024, 512), ("k", 1536, 128), ("v", 1664, 128),
          ("ga", 1792, 1024), ("gb", 2816, 1024))
EPS = 1e-6
NEG = -1e30
GELU_C = math.sqrt(2.0 / math.pi)

ADAM_LR = 0.001
ADAM_B1 = 0.9
ADAM_B2 = 0.999
ADAM_EPS = 1e-08
ADAM_WD = 0.01
ADAM_STEP = 10

NT = (((1,), (1,)), ((), ()))
TN = (((0,), (0,)), ((), ()))

BIG = (("ffn1_w_gate", True, FF // N_DEV, D), ("ffn1_w_up", True, FF // N_DEV, D),
       ("ffn1_w_down", False, FF // N_DEV, D), ("w_in", True, IN_COLS // N_DEV, D),
       ("w_branch_a", True, D // N_DEV, D_A), ("w_branch_b", True, D // N_DEV, D_B), ("w_out", False, D // N_DEV, D),
       ("ffn2_w_gate", True, FF // N_DEV, D), ("ffn2_w_up", True, FF // N_DEV, D),
       ("ffn2_w_down", False, FF // N_DEV, D))
SMALL = ("b_ada", "g_norm1", "g_norm2", "g_sgu_ln", "b_sgu_ln", "w_spatial", "b_spatial", "g_q", "g_k",
         "attn_sinks", "g_norm3")


def _dot(a, b):
    return jnp.dot(a, b, preferred_element_type=F32)


def _dot_nt(a, b):
    return lax.dot_general(a, b, NT, preferred_element_type=F32)


def _dot_tn(a, b):
    return lax.dot_general(a, b, TN, preferred_element_type=F32)


def _vmem():
    return pl.BlockSpec(memory_space=pltpu.VMEM)


def _any():
    return pl.BlockSpec(memory_space=pl.ANY)


def _whole(a):
    return pl.BlockSpec(a.shape, lambda i: (0,) * len(a.shape))


def _rms_mod(h, g, sh, sc):
    inv = lax.rsqrt(jnp.mean(h * h, axis=-1, keepdims=True) + EPS)
    r = h * inv
    return (r * g) * (1.0 + sc) + sh, r, inv


def _rms_mod_bwd(dxn, r, inv, g, sc):
    dr = dxn * (g * (1.0 + sc))
    dh = inv * (dr - r * jnp.mean(dr * r, axis=-1, keepdims=True))
    return dh, jnp.sum(dxn, axis=0, keepdims=True), jnp.sum(dxn * r, axis=0, keepdims=True)


def _gelu(x):
    t = jnp.tanh(GELU_C * (x + 0.044715 * (x * x * x)))
    return 0.5 * x * (1.0 + t), t


def _gelu_grad(x, t):
    return 0.5 * (1.0 + t) + 0.5 * x * (1.0 - t * t) * (GELU_C * (1.0 + 3.0 * 0.044715 * x * x))


def _adamw(w, g, m, v):
    m = ADAM_B1 * m + (1.0 - ADAM_B1) * g
    v = ADAM_B2 * v + (1.0 - ADAM_B2) * (g * g)
    m_hat = m / (1.0 - ADAM_B1 ** ADAM_STEP)
    v_hat = v / (1.0 - ADAM_B2 ** ADAM_STEP)
    delta = -ADAM_LR * (m_hat / (jnp.sqrt(v_hat) + ADAM_EPS) + ADAM_WD * w)
    return delta, m, v


def _token_tile(seq, cap=512):
    return min(cap, seq)


def _params(*semantics):
    return pltpu.CompilerParams(dimension_semantics=semantics, vmem_limit_bytes=VMEM_LIMIT)


def _place():
    return lax.axis_index("x"), lax.axis_index("y"), lax.axis_index("c")


class _Gather:
    def __init__(self, arrays):
        n = len(arrays)
        self.ins = list(arrays)
        self.out_shape = [jax.ShapeDtypeStruct((N_DEV,) + a.shape, a.dtype) for a in arrays]
        self.scratch = [pltpu.SemaphoreType.DMA((n, 7)), pltpu.SemaphoreType.DMA((n, 7)),
                        pltpu.SemaphoreType.DMA((n,))]

    def _copies(self, ins, outs, sems):
        send_sems, recv_sems, local_sems = sems
        n = len(ins)
        x, y, c = _place()
        me, sibling = (x, y, c), (x, y, 1 - c)
        chips = [(1 - x, y), (x, 1 - y), (1 - x, 1 - y)]

        def slot(a, px, py, pc):
            return outs[a].at[4 * px + 2 * py + pc]

        def copy(a, k, block, to, src=None):
            return pltpu.make_async_remote_copy(
                src_ref=slot(a, *block) if src is None else src, dst_ref=slot(a, *block),
                send_sem=send_sems.at[a, k], recv_sem=recv_sems.at[a, k], device_id=to, device_id_type=MESH)

        mine = [pltpu.make_async_copy(ins[a], slot(a, *me), local_sems.at[a]) for a in range(n)]
        first = [copy(a, 0, me, sibling, src=ins[a]) for a in range(n)]
        first += [copy(a, 1 + j, me, (*chip, c), src=ins[a]) for a in range(n) for j, chip in enumerate(chips)]
        landed = [[copy(a, 1 + j, (*chip, c), me) for a in range(n)] for j, chip in enumerate(chips)]
        passed = [[copy(a, 4 + j, (*chip, c), sibling) for a in range(n)] for j, chip in enumerate(chips)]
        from_sibling = [copy(a, 0, sibling, me) for a in range(n)]
        from_sibling += [copy(a, 4 + j, (*chip, 1 - c), me) for a in range(n) for j, chip in enumerate(chips)]
        return mine, first, landed, passed, from_sibling

    def start(self, ins, outs, sems):
        mine, first, _, _, _ = self._copies(ins, outs, sems)
        for cp in mine + first:
            cp.start()

    def finish(self, ins, outs, sems):
        mine, first, landed, passed, from_sibling = self._copies(ins, outs, sems)
        for arrivals, forwards in zip(landed, passed):
            for arrival, forward in zip(arrivals, forwards):
                arrival.wait_recv()
                forward.start()
        for cp in from_sibling:
            cp.wait_recv()
        for cp in first + [f for fs in passed for f in fs]:
            cp.wait_send()
        for cp in mine:
            cp.wait()


class _RsSibling:
    def __init__(self, ps):
        n = len(ps)
        self.ins = list(ps)
        self.out_shape = [jax.ShapeDtypeStruct((4,) + p.shape[1:], p.dtype) for p in ps]
        self.scratch = [pltpu.SemaphoreType.DMA((n, 4)), pltpu.SemaphoreType.DMA((n, 4))]

    def _copies(self, ins, outs, sems):
        send_sems, recv_sems = sems
        x, y, c = _place()
        return [pltpu.make_async_remote_copy(
            src_ref=ins[a].at[2 * q + (1 - c)], dst_ref=outs[a].at[q], send_sem=send_sems.at[a, q],
            recv_sem=recv_sems.at[a, q], device_id=(x, y, 1 - c), device_id_type=MESH)
            for a in range(len(ins)) for q in range(4)]

    def start(self, ins, outs, sems):
        for cp in self._copies(ins, outs, sems):
            cp.start()

    def finish(self, ins, outs, sems):
        for cp in self._copies(ins, outs, sems):
            cp.wait()


class _RsChips:
    def __init__(self, qs):
        n = len(qs)
        self.ins = list(qs)
        self.out_shape = [jax.ShapeDtypeStruct(q.shape, q.dtype) for q in qs]
        self.scratch = [pltpu.SemaphoreType.DMA((n, 3)), pltpu.SemaphoreType.DMA((n, 3)),
                        pltpu.SemaphoreType.DMA((n,))]

    def _copies(self, ins, outs, sems):
        send_sems, recv_sems, local_sems = sems
        n = len(ins)
        x, y, c = _place()
        my_chip = 2 * x + y
        chips = [(1 - x, y), (x, 1 - y), (1 - x, 1 - y)]

        def copy(a, j, src_slot, dst_slot):
            px, py = chips[j]
            return pltpu.make_async_remote_copy(
                src_ref=ins[a].at[src_slot], dst_ref=outs[a].at[dst_slot], send_sem=send_sems.at[a, j],
                recv_sem=recv_sems.at[a, j], device_id=(px, py, c), device_id_type=MESH)

        own = [pltpu.make_async_copy(ins[a].at[my_chip], outs[a].at[my_chip], local_sems.at[a]) for a in range(n)]
        sends = [copy(a, j, 2 * px + py, my_chip) for a in range(n) for j, (px, py) in enumerate(chips)]
        arrivals = [copy(a, j, my_chip, 2 * px + py) for a in range(n) for j, (px, py) in enumerate(chips)]
        return own, sends, arrivals

    def start(self, ins, outs, sems):
        own, sends, _ = self._copies(ins, outs, sems)
        for cp in own + sends:
            cp.start()

    def finish(self, ins, outs, sems):
        own, sends, arrivals = self._copies(ins, outs, sems)
        for cp in arrivals:
            cp.wait_recv()
        for cp in sends:
            cp.wait_send()
        for cp in own:
            cp.wait()


def _split_plans(plans, refs_in, refs_out, refs_scr, phase):
    i = o = s = 0
    for p in plans:
        ni, no, ns = len(p.ins), len(p.out_shape), len(p.scratch)
        getattr(p, phase)(refs_in[i:i + ni], refs_out[o:o + no], refs_scr[s:s + ns])
        i, o, s = i + ni, o + no, s + ns


def _plan_results(plans, res):
    out = []
    for p in plans:
        out.append(list(res[:len(p.out_shape)]))
        res = res[len(p.out_shape):]
    return out


def _exchange(plans, name):
    c_in = [a for p in plans for a in p.ins]
    c_out = [s for p in plans for s in p.out_shape]
    c_scr = [s for p in plans for s in p.scratch]

    def body(*refs):
        cin, cout, cscr = refs[:len(c_in)], refs[len(c_in):len(c_in) + len(c_out)], refs[len(c_in) + len(c_out):]
        _split_plans(plans, cin, cout, cscr, "start")
        _split_plans(plans, cin, cout, cscr, "finish")

    res = pl.pallas_call(body, name=name, in_specs=[_any()] * len(c_in), out_specs=[_any()] * len(c_out),
                         out_shape=c_out, scratch_shapes=c_scr)(*c_in)
    return _plan_results(plans, res)


def _call(body, *, name, grid, in_specs, out_specs, out_shape, args, semantics, scratch_shapes=(), plans=()):
    n_in, n_out, n_scr = len(in_specs), len(out_specs), len(scratch_shapes)
    c_in = [a for p in plans for a in p.ins]
    c_out = [s for p in plans for s in p.out_shape]
    c_scr = [s for p in plans for s in p.scratch]

    def wrapped(*refs):
        ins, refs = refs[:n_in], refs[n_in:]
        cin, refs = refs[:len(c_in)], refs[len(c_in):]
        outs, refs = refs[:n_out], refs[n_out:]
        cout, refs = refs[:len(c_out)], refs[len(c_out):]
        scr, cscr = refs[:n_scr], refs[n_scr:]
        if plans:
            ids = [pl.program_id(d) for d in range(len(grid))]
            first = functools.reduce(jnp.logical_and, [i == 0 for i in ids])
            last = functools.reduce(jnp.logical_and, [i == g - 1 for i, g in zip(ids, grid)])
            pl.when(first)(lambda: _split_plans(plans, cin, cout, cscr, "start"))
        body(*ins, *outs, *scr)
        if plans:
            pl.when(last)(lambda: _split_plans(plans, cin, cout, cscr, "finish"))

    res = pl.pallas_call(
        wrapped, name=name, grid=grid, in_specs=list(in_specs) + [_any()] * len(c_in),
        out_specs=list(out_specs) + [_any()] * len(c_out), out_shape=list(out_shape) + c_out,
        scratch_shapes=list(scratch_shapes) + c_scr,
        compiler_params=_params(*(("arbitrary",) * len(grid) if plans else semantics)),
    )(*args, *c_in)
    return list(res[:n_out]), _plan_results(plans, res[n_out:])


def _ffn_fwd(h, mods, gn, wg, wu, wd, row0, seq, tgt=None, plans=()):
    T = h.shape[0]
    tm = _token_tile(seq, 256)
    tps = seq // tm
    n_t = T // tm
    with_loss = tgt is not None

    def body(h_ref, m_ref, g_ref, wg_ref, wu_ref, wd_ref, *rest):
        if with_loss:
            tgt_ref, out_ref, y_ref, gb_ref, ub_ref, loss_ref = rest
        else:
            out_ref, y_ref, gb_ref, ub_ref = rest
        hv = h_ref[...]
        sh = m_ref[0, row0:row0 + 1, :]
        sc = m_ref[0, row0 + 1:row0 + 2, :]
        ga = m_ref[0, row0 + 2:row0 + 3, :]
        xn, _, _ = _rms_mod(hv, g_ref[...], sh, sc)
        xb = xn.astype(BF16)
        acc = jnp.zeros((tm, D), F32)
        for c0 in range(0, FF, FC):
            gg = _dot_nt(xb, wg_ref[c0:c0 + FC, :])
            uu = _dot_nt(xb, wu_ref[c0:c0 + FC, :])
            gg, uu = gg.astype(BF16), uu.astype(BF16)
            gb_ref[:, c0:c0 + FC] = gg
            ub_ref[:, c0:c0 + FC] = uu
            acc = acc + _dot((gg * jax.nn.sigmoid(gg)) * uu, wd_ref[c0:c0 + FC, :])
        y_ref[...] = acc
        hout = hv + (0.5 * ga) * acc
        if with_loss:
            d = hout - tgt_ref[...]
            out_ref[...] = d * (1.0 / D)
            loss_ref[...] = jnp.full((1, 8, 128), 0.5 / D, F32) * jnp.sum(d * d)
        else:
            out_ref[...] = hout

    tok = pl.BlockSpec((tm, D), lambda i: (i, 0))
    tokf = pl.BlockSpec((tm, FF), lambda i: (i, 0))
    in_specs = [tok, pl.BlockSpec((1, 9, D), lambda i: (i // tps, 0, 0)), pl.BlockSpec((1, D), lambda i: (0, 0)),
                _vmem(), _vmem(), _vmem()]
    out_shape = [jax.ShapeDtypeStruct((T, D), F32), jax.ShapeDtypeStruct((T, D), F32),
                 jax.ShapeDtypeStruct((T, FF), BF16), jax.ShapeDtypeStruct((T, FF), BF16)]
    out_specs = [tok, tok, tokf, tokf]
    args = [h, mods, gn, wg, wu, wd]
    if with_loss:
        in_specs.append(tok)
        args.append(tgt)
        out_shape.append(jax.ShapeDtypeStruct((n_t, 8, 128), F32))
        out_specs.append(pl.BlockSpec((1, 8, 128), lambda i: (i, 0, 0)))
    return _call(body, name="ffn_fwd_loss" if with_loss else "ffn_fwd", grid=(n_t,), in_specs=in_specs,
                 out_specs=out_specs, out_shape=out_shape, args=args, semantics=("parallel",), plans=plans)


def _ffn_gate_up(h, mods, gn, wg, wu, row0, seq, plans=()):
    T = h.shape[0]
    tm = _token_tile(seq, 256)
    tps = seq // tm

    def body(h_ref, m_ref, g_ref, wg_ref, wu_ref, gb_ref, ub_ref):
        xn, _, _ = _rms_mod(h_ref[...], g_ref[...], m_ref[0, row0:row0 + 1, :], m_ref[0, row0 + 1:row0 + 2, :])
        xb = xn.astype(BF16)
        for c0 in range(0, FF, FC):
            gb_ref[:, c0:c0 + FC] = _dot_nt(xb, wg_ref[c0:c0 + FC, :]).astype(BF16)
            ub_ref[:, c0:c0 + FC] = _dot_nt(xb, wu_ref[c0:c0 + FC, :]).astype(BF16)

    tokf = pl.BlockSpec((tm, FF), lambda i: (i, 0))
    return _call(
        body, name="ffn_gate_up", grid=(T // tm,),
        in_specs=[pl.BlockSpec((tm, D), lambda i: (i, 0)), pl.BlockSpec((1, 9, D), lambda i: (i // tps, 0, 0)),
                  pl.BlockSpec((1, D), lambda i: (0, 0)), _vmem(), _vmem()],
        out_specs=[tokf, tokf], out_shape=[jax.ShapeDtypeStruct((T, FF), BF16)] * 2, args=(h, mods, gn, wg, wu),
        semantics=("parallel",), plans=plans)


def _ffn_down(h, gb, ub, mods, wd, row0, seq, plans=()):
    T = h.shape[0]
    tm = _token_tile(seq, 256)
    tps = seq // tm

    def body(h_ref, gb_ref, ub_ref, m_ref, wd_ref, out_ref, y_ref):
        acc = jnp.zeros((tm, D), F32)
        for c0 in range(0, FF, FC):
            gg = gb_ref[:, c0:c0 + FC]
            acc = acc + _dot((gg * jax.nn.sigmoid(gg)) * ub_ref[:, c0:c0 + FC], wd_ref[c0:c0 + FC, :])
        y_ref[...] = acc
        out_ref[...] = h_ref[...] + (0.5 * m_ref[0, row0 + 2:row0 + 3, :]) * acc

    tok = pl.BlockSpec((tm, D), lambda i: (i, 0))
    tokf = pl.BlockSpec((tm, FF), lambda i: (i, 0))
    return _call(
        body, name="ffn_down", grid=(T // tm,),
        in_specs=[tok, tokf, tokf, pl.BlockSpec((1, 9, D), lambda i: (i // tps, 0, 0)), _vmem()],
        out_specs=[tok, tok], out_shape=[jax.ShapeDtypeStruct((T, D), F32)] * 2, args=(h, gb, ub, mods, wd),
        semantics=("parallel",), plans=plans)


def _ffn_bwd(h, dhn, y, gb, ub, mods, gn, wg, wu, wd, row0, seq, plans=()):
    T = h.shape[0]
    B = T // seq
    tm = _token_tile(seq, 256)
    tps = seq // tm
    n_t = T // tm

    def body(h_ref, dhn_ref, y_ref, gb_ref, ub_ref, m_ref, g_ref, wg_ref, wu_ref, wd_ref,
             dh_ref, xb_ref, dyb_ref, a_ref, dg_ref, du_ref, mg_ref):
        i = pl.program_id(0)
        hv = h_ref[...]
        dhn = dhn_ref[...]
        sh = m_ref[0, row0:row0 + 1, :]
        sc = m_ref[0, row0 + 1:row0 + 2, :]
        ga = m_ref[0, row0 + 2:row0 + 3, :]
        g = g_ref[...]
        xn, r, inv = _rms_mod(hv, g, sh, sc)
        xb_ref[...] = xn.astype(BF16)
        dyb = ((0.5 * ga) * dhn).astype(BF16)
        dyb_ref[...] = dyb
        dga = 0.5 * jnp.sum(dhn * y_ref[...], axis=0, keepdims=True)
        dxn = jnp.zeros((tm, D), F32)
        for c0 in range(0, FF, FC):
            wgc = wg_ref[c0:c0 + FC, :]
            wuc = wu_ref[c0:c0 + FC, :]
            gg = gb_ref[:, c0:c0 + FC]
            uu = ub_ref[:, c0:c0 + FC]
            sig = jax.nn.sigmoid(gg)
            s = gg * sig
            a_ref[:, c0:c0 + FC] = s * uu
            da = _dot_nt(dyb, wd_ref[c0:c0 + FC, :]).astype(BF16)
            dub = da * s
            dgb = (da * uu) * (sig * (1.0 + gg * (1.0 - sig)))
            dg_ref[:, c0:c0 + FC] = dgb
            du_ref[:, c0:c0 + FC] = dub
            dxn = dxn + _dot(dgb, wgc) + _dot(dub, wuc)
        dh, s_dxn, s_dxr = _rms_mod_bwd(dxn, r, inv, g, sc)
        dh_ref[...] = dhn + dh

        @pl.when(i % tps == 0)
        def _():
            mg_ref[...] = jnp.zeros(mg_ref.shape, F32)

        mg_ref[0, 0:1, :] += s_dxn
        mg_ref[0, 1:2, :] += s_dxr
        mg_ref[0, 2:3, :] += dga

    tok = pl.BlockSpec((tm, D), lambda i: (i, 0))
    tokf = pl.BlockSpec((tm, FF), lambda i: (i, 0))
    return _call(
        body, name="ffn_bwd", grid=(n_t,),
        in_specs=[tok, tok, tok, tokf, tokf, pl.BlockSpec((1, 9, D), lambda i: (i // tps, 0, 0)),
                  pl.BlockSpec((1, D), lambda i: (0, 0)), _vmem(), _vmem(), _vmem()],
        out_specs=[tok, tok, tok, tokf, tokf, tokf, pl.BlockSpec((1, 8, D), lambda i: (i // tps, 0, 0))],
        out_shape=[jax.ShapeDtypeStruct((T, D), F32), jax.ShapeDtypeStruct((T, D), BF16),
                   jax.ShapeDtypeStruct((T, D), BF16), jax.ShapeDtypeStruct((T, FF), BF16),
                   jax.ShapeDtypeStruct((T, FF), BF16), jax.ShapeDtypeStruct((T, FF), BF16),
                   jax.ShapeDtypeStruct((B, 8, D), F32)],
        args=(h, dhn, y, gb, ub, mods, gn, wg, wu, wd), semantics=("arbitrary",), plans=plans)


def _wgrad(a, b, name, plans=()):
    T, da = a.shape
    db = b.shape[1]
    bm = {2816: 1408, 3840: 1280}[da]
    bn = db
    tk = min(1024, T)
    nk = T // tk

    def body(a_ref, b_ref, o_ref):
        @pl.when(pl.program_id(2) == 0)
        def _():
            o_ref[...] = jnp.zeros(o_ref.shape, F32)

        o_ref[...] += _dot_tn(a_ref[...], b_ref[...])

    (out,), plan_outs = _call(
        body, name=name, grid=(da // bm, db // bn, nk),
        in_specs=[pl.BlockSpec((tk, bm), lambda i, j, k: (k, i)), pl.BlockSpec((tk, bn), lambda i, j, k: (k, j))],
        out_specs=[pl.BlockSpec((bm, bn), lambda i, j, k: (i, j))], out_shape=[jax.ShapeDtypeStruct((da, db), F32)],
        args=(a, b), semantics=("parallel", "parallel", "arbitrary"), plans=plans)
    return out, plan_outs


def _inproj_fwd(h, mods, gn, w_in, seq, plans=()):
    T = h.shape[0]
    tm = _token_tile(seq)
    tps = seq // tm

    def body(h_ref, m_ref, g_ref, w_ref, *outs):
        xn, _, _ = _rms_mod(h_ref[...], g_ref[...], m_ref[0, 3:4, :], m_ref[0, 4:5, :])
        xb = xn.astype(BF16)
        for (_, c0, w), o_ref in zip(PIECES, outs):
            o_ref[...] = _dot_nt(xb, w_ref[c0:c0 + w, :])

    return _call(
        body, name="inproj_fwd", grid=(T // tm,),
        in_specs=[pl.BlockSpec((tm, D), lambda i: (i, 0)), pl.BlockSpec((1, 9, D), lambda i: (i // tps, 0, 0)),
                  pl.BlockSpec((1, D), lambda i: (0, 0)), _vmem()],
        out_specs=[pl.BlockSpec((tm, w), lambda i: (i, 0)) for _, _, w in PIECES],
        out_shape=[jax.ShapeDtypeStruct((T, w), F32) for _, _, w in PIECES], args=(h, mods, gn, w_in),
        semantics=("parallel",), plans=plans)


def _inproj_bwd(h, dh_res, dpieces, mods, gn, w_in, seq, plans=()):
    T = h.shape[0]
    B = T // seq
    tm = _token_tile(seq, 256)
    tps = seq // tm

    def body(h_ref, dres_ref, *rest):
        dp_refs = rest[:len(PIECES)]
        m_ref, g_ref, w_ref, dh_ref, xb_ref, dpb_ref, mg_ref = rest[len(PIECES):]
        i = pl.program_id(0)
        g = g_ref[...]
        sc = m_ref[0, 4:5, :]
        xn, r, inv = _rms_mod(h_ref[...], g, m_ref[0, 3:4, :], sc)
        xb_ref[...] = xn.astype(BF16)
        dxn = jnp.zeros((tm, D), F32)
        for (_, c0, w), dp_ref in zip(PIECES, dp_refs):
            dpb = dp_ref[...].astype(BF16)
            dpb_ref[:, c0:c0 + w] = dpb
            dxn = dxn + _dot(dpb, w_ref[c0:c0 + w, :])
        dh, s_dxn, s_dxr = _rms_mod_bwd(dxn, r, inv, g, sc)
        dh_ref[...] = dres_ref[...] + dh

        @pl.when(i % tps == 0)
        def _():
            mg_ref[...] = jnp.zeros(mg_ref.shape, F32)

        mg_ref[0, 0:1, :] += s_dxn
        mg_ref[0, 1:2, :] += s_dxr

    tok = pl.BlockSpec((tm, D), lambda i: (i, 0))
    return _call(
        body, name="inproj_bwd", grid=(T // tm,),
        in_specs=[tok, tok] + [pl.BlockSpec((tm, w), lambda i: (i, 0)) for _, _, w in PIECES]
        + [pl.BlockSpec((1, 9, D), lambda i: (i // tps, 0, 0)), pl.BlockSpec((1, D), lambda i: (0, 0)), _vmem()],
        out_specs=[tok, tok, pl.BlockSpec((tm, IN_COLS), lambda i: (i, 0)),
                   pl.BlockSpec((1, 8, D), lambda i: (i // tps, 0, 0))],
        out_shape=[jax.ShapeDtypeStruct((T, D), F32), jax.ShapeDtypeStruct((T, D), BF16),
                   jax.ShapeDtypeStruct((T, IN_COLS), BF16), jax.ShapeDtypeStruct((B, 8, D), F32)],
        args=(h, dh_res, *dpieces, mods, gn, w_in), semantics=("arbitrary",), plans=plans)


def _seg_mean(x):
    i = lax.broadcasted_iota(jnp.int32, (128, 128), 0) >> 6
    j = lax.broadcasted_iota(jnp.int32, (128, 128), 1) >> 6
    ones = jnp.where(i == j, 1.0 / HD, 0.0).astype(BF16)
    hi = x.astype(BF16)
    lo = (x - hi.astype(F32)).astype(BF16)
    return _dot(hi, ones) + _dot(lo, ones)


def _head_norm(x, g2):
    inv = lax.rsqrt(_seg_mean(x * x) + EPS)
    y = x * inv
    return y * g2, y, inv


def _head_norm_bwd(dxn, y, inv, g2):
    dy = dxn * g2
    return inv * (dy - y * _seg_mean(dy * y)), jnp.sum(dxn * y, axis=0, keepdims=True)


def _swa_block(q, kk, vv, gq2, gk2, sinks, first, do=None):
    lo = lax.broadcasted_iota(jnp.int32, (1, 128), 1) < HD
    kn, ky, kinv = _head_norm(kk, gk2)

    def operands(x):
        xr = pltpu.roll(x, HD, 1)
        own_lo, own_hi = jnp.where(lo, x, 0.0).astype(BF16), jnp.where(lo, 0.0, x).astype(BF16)
        rolled_lo, rolled_hi = jnp.where(lo, xr, 0.0).astype(BF16), jnp.where(lo, 0.0, xr).astype(BF16)
        return (own_lo, rolled_hi), (rolled_lo, own_hi)

    def restore(parts):
        (own_lo, rolled_hi), (rolled_lo, own_hi) = parts
        return (jnp.where(lo, own_lo, own_hi)
                + pltpu.roll(jnp.where(lo, rolled_lo, rolled_hi), HD, 1))

    k_ops, v_ops = operands(kn), operands(vv)
    k2 = [jnp.concatenate(pair, axis=0) for pair in k_ops]
    def stack(x):
        return jnp.concatenate([x[:, 128 * p:128 * (p + 1)] for p in range(4)], axis=0)

    def unstack(x):
        return jnp.concatenate([x[BLK * p:BLK * (p + 1)] for p in range(4)], axis=1)

    def of_head(x, kh):
        return x[2 * BLK * kh:2 * BLK * (kh + 1)]

    nq = 4 * BLK
    pair = lax.broadcasted_iota(jnp.int32, (nq, 1), 0) >> 7
    row = lax.broadcasted_iota(jnp.int32, (nq, 2 * BLK), 0) & (BLK - 1)
    col = lax.broadcasted_iota(jnp.int32, (nq, 2 * BLK), 1)
    valid = (col <= row + BLK) & (col > row) & (col >= jnp.where(first, BLK, 0))
    scale = HD ** -0.5
    qn, qy, qinv = _head_norm(stack(q), gq2)
    qnb = qn.astype(BF16)
    s2 = jnp.concatenate([_dot_nt(of_head(qnb, kh), k2[kh]) for kh in range(N_KV)], axis=0) * scale
    probs, p_sink = [], []
    for j in range(2):
        s = jnp.where(valid, s2[:, 2 * BLK * j:2 * BLK * (j + 1)], NEG)
        sink = jnp.zeros((nq, 1), F32)
        for p in range(4):
            sink = jnp.where(pair == p, sinks[:, 2 * p + j:2 * p + j + 1], sink)
        m = jnp.maximum(jnp.max(s, axis=-1, keepdims=True), sink)
        e = jnp.exp(s - m)
        e_sink = jnp.exp(sink - m)
        rden = 1.0 / (jnp.sum(e, axis=-1, keepdims=True) + e_sink)
        probs.append(e * rden)
        p_sink.append(e_sink * rden)
    pb = [p.astype(BF16) for p in probs]
    if do is None:
        return unstack(jnp.concatenate(
            [_dot(of_head(pb[0], kh), v_ops[kh][0]) + _dot(of_head(pb[1], kh), v_ops[kh][1]) for kh in range(N_KV)],
            axis=0))
    dob = stack(do).astype(BF16)
    ds, dsinks = [], [None] * 8
    for j in range(2):
        dp = jnp.concatenate([_dot_nt(of_head(dob, kh), v_ops[kh][j]) for kh in range(N_KV)], axis=0)
        t = jnp.sum(dp * probs[j], axis=-1, keepdims=True)
        ds.append(probs[j] * (dp - t) * scale)
        lost = p_sink[j] * t
        for p in range(4):
            dsinks[2 * p + j] = -jnp.sum(lost[BLK * p:BLK * (p + 1)])
    dsb = jnp.concatenate(ds, axis=1).astype(BF16)
    dqn = jnp.concatenate([_dot(of_head(dsb, kh), k2[kh]) for kh in range(N_KV)], axis=0)
    dq, dgq2 = _head_norm_bwd(dqn, qy, qinv, gq2)
    dk2 = [_dot_tn(of_head(dsb, kh), of_head(qnb, kh)) for kh in range(N_KV)]
    dv_ops = [[_dot_tn(of_head(pb[j], kh), of_head(dob, kh)) for j in range(2)] for kh in range(N_KV)]
    dkn = restore(tuple((d[:2 * BLK], d[2 * BLK:]) for d in dk2))
    dvv = restore(tuple(tuple(d) for d in dv_ops))
    dkk, dgk2 = _head_norm_bwd(dkn, ky, kinv, gk2)
    return unstack(dq), dkk, dvv, dgq2, dgk2, dsinks


SWA_GROUP = 8


def _swa_specs(nbs):
    grp = min(SWA_GROUP, nbs)
    rows = grp * BLK

    def tok(w):
        return pl.BlockSpec((rows, w), lambda i: (i, 0))

    halo = pl.BlockSpec((BLK, 128), lambda i: (jnp.maximum(i * grp - 1, 0), 0))
    vec = pl.BlockSpec((1, 128), lambda i: (0, 0))
    sk = pl.BlockSpec((1, 8), lambda i: (0, 0))
    return grp, tok, halo, vec, sk


def _swa_fwd(q, k, v, gq2, gk2, sinks, nbs, plans=()):
    T = q.shape[0]
    grp, tok, halo, vec, sk = _swa_specs(nbs)

    def body(q_ref, kh_ref, kc_ref, vh_ref, vc_ref, gq_ref, gk_ref, sk_ref, o_ref):
        seq_start = ((pl.program_id(0) * grp) % nbs) == 0
        for g in range(grp):
            rows = slice(g * BLK, (g + 1) * BLK)
            prev = slice((g - 1) * BLK, g * BLK)
            kk = jnp.concatenate([kh_ref[...] if g == 0 else kc_ref[prev, :], kc_ref[rows, :]], axis=0)
            vv = jnp.concatenate([vh_ref[...] if g == 0 else vc_ref[prev, :], vc_ref[rows, :]], axis=0)
            o_ref[rows, :] = _swa_block(q_ref[rows, :], kk, vv, gq_ref[...], gk_ref[...], sk_ref[...],
                                        seq_start if g == 0 else False)

    (out,), plan_outs = _call(
        body, name="swa_fwd", grid=(T // (grp * BLK),),
        in_specs=[tok(D_B), halo, tok(128), halo, tok(128), vec, vec, sk], out_specs=[tok(D_B)],
        out_shape=[jax.ShapeDtypeStruct((T, D_B), F32)], args=(q, k, k, v, v, gq2, gk2, sinks),
        semantics=("parallel",), plans=plans)
    return out, plan_outs


def _swa_bwd(q, k, v, gq2, gk2, sinks, do, nbs, plans=()):
    T = q.shape[0]
    grp, tok, halo, vec, sk = _swa_specs(nbs)
    steps = T // (grp * BLK)

    def body(q_ref, kh_ref, kc_ref, vh_ref, vc_ref, gq_ref, gk_ref, sk_ref, do_ref,
             dq_ref, dk_ref, dv_ref, dkh_ref, dvh_ref, dgq_ref, dgk_ref, dsk_ref):
        i = pl.program_id(0)
        seq_start = ((i * grp) % nbs) == 0

        @pl.when(i == 0)
        def _():
            for r in (dgq_ref, dgk_ref, dsk_ref):
                r[...] = jnp.zeros(r.shape, F32)

        res = []
        for g in range(grp):
            rows = slice(g * BLK, (g + 1) * BLK)
            prev = slice((g - 1) * BLK, g * BLK)
            kk = jnp.concatenate([kh_ref[...] if g == 0 else kc_ref[prev, :], kc_ref[rows, :]], axis=0)
            vv = jnp.concatenate([vh_ref[...] if g == 0 else vc_ref[prev, :], vc_ref[rows, :]], axis=0)
            res.append(_swa_block(q_ref[rows, :], kk, vv, gq_ref[...], gk_ref[...], sk_ref[...],
                                  seq_start if g == 0 else False, do=do_ref[rows, :]))
        lane = lax.broadcasted_iota(jnp.int32, (8, 128), 1)
        upd = jnp.zeros((8, 128), F32)
        for g, (dq, dkk, dvv, dgq2, dgk2, dsinks) in enumerate(res):
            rows = slice(g * BLK, (g + 1) * BLK)
            dq_ref[rows, :] = dq
            dk_ref[rows, :] = dkk[BLK:] + res[g + 1][1][:BLK] if g + 1 < grp else dkk[BLK:]
            dv_ref[rows, :] = dvv[BLK:] + res[g + 1][2][:BLK] if g + 1 < grp else dvv[BLK:]
            dgq_ref[...] += dgq2
            dgk_ref[...] += dgk2
            for h, d in enumerate(dsinks):
                upd = upd + jnp.where(lane == h, d, 0.0)
        dkh_ref[0] = res[0][1][:BLK]
        dvh_ref[0] = res[0][2][:BLK]
        dsk_ref[...] += upd

    one = pl.BlockSpec((1, BLK, 128), lambda i: (i, 0, 0))
    halo_shape = jax.ShapeDtypeStruct((steps, BLK, 128), F32)
    return _call(
        body, name="swa_bwd", grid=(steps,),
        in_specs=[tok(D_B), halo, tok(128), halo, tok(128), vec, vec, sk, tok(D_B)],
        out_specs=[tok(D_B), tok(128), tok(128), one, one, vec, vec, pl.BlockSpec((8, 128), lambda i: (0, 0))],
        out_shape=[jax.ShapeDtypeStruct((T, D_B), F32), jax.ShapeDtypeStruct((T, 128), F32),
                   jax.ShapeDtypeStruct((T, 128), F32), halo_shape, halo_shape, jax.ShapeDtypeStruct((1, 128), F32),
                   jax.ShapeDtypeStruct((1, 128), F32), jax.ShapeDtypeStruct((8, 128), F32)],
        args=(q, k, k, v, v, gq2, gk2, sinks, do), semantics=("arbitrary",), plans=plans)


def _swa_add_halo(dk, dk_halo):
    steps = dk_halo.shape[0]
    nxt = jnp.concatenate([dk_halo[1:], jnp.zeros_like(dk_halo[:1])], axis=0)[:, None]
    dk = dk.reshape(steps, -1, BLK, 128)
    return jnp.concatenate([dk[:, :-1], dk[:, -1:] + nxt], axis=1).reshape(-1, 128)


def _sgu_norm(av, g_ln, b_ln):
    t, th = _gelu(av)
    mu = jnp.mean(t, axis=-1, keepdims=True)
    tc = t - mu
    rstd = lax.rsqrt(jnp.mean(tc * tc, axis=-1, keepdims=True) + EPS)
    vhat = tc * rstd
    return vhat * g_ln + b_ln, vhat, rstd, th


def _masked_ws(ws_ref):
    tril = lax.broadcasted_iota(jnp.int32, (BLK, BLK), 0) >= lax.broadcasted_iota(jnp.int32, (BLK, BLK), 1)
    return [jnp.where(tril, ws_ref[g], 0.0).astype(BF16) for g in range(N_GRP)]


def _mix_fwd(au, av, gta, gtb, ob, h, mods, g_ln, b_ln, ws, bsb, wa, wb, wout, seq):
    T = h.shape[0]
    tm = _token_tile(seq)
    tps = seq // tm

    def body(au_ref, av_ref, gta_ref, gtb_ref, ob_ref, h_ref, m_ref, gl_ref, bl_ref, ws_ref, bs_ref,
             wa_ref, wb_ref, wo_ref, out_ref, vvb_s, z_s):
        u, _ = _gelu(au_ref[...])
        vv, _, _, _ = _sgu_norm(av_ref[...], gl_ref[...], bl_ref[...])
        vvb_s[...] = vv.astype(BF16)
        wsm = _masked_ws(ws_ref)
        for c in range(tm // BLK):
            rows = slice(c * BLK, (c + 1) * BLK)
            for g in range(N_GRP):
                cols = slice(g * BLK, (g + 1) * BLK)
                z_s[rows, cols] = _dot(wsm[g], vvb_s[rows, cols]) + bs_ref[g]
        ya = _dot_nt((u * z_s[...]).astype(BF16), wa_ref[...])
        yb = _dot_nt(ob_ref[...].astype(BF16), wb_ref[...])
        merged = jax.nn.sigmoid(gta_ref[...]) * ya + jax.nn.sigmoid(gtb_ref[...]) * yb
        out_ref[...] = h_ref[...] + m_ref[0, 5:6, :] * _dot(merged.astype(BF16), wo_ref[...])

    def tok(w):
        return pl.BlockSpec((tm, w), lambda i: (i, 0))

    def full(shape):
        return pl.BlockSpec(shape, lambda i: (0,) * len(shape))

    return pl.pallas_call(
        body, name="mix_fwd", grid=(T // tm,),
        in_specs=[tok(D_A), tok(D_A), tok(D), tok(D), tok(D_B), tok(D),
                  pl.BlockSpec((1, 9, D), lambda i: (i // tps, 0, 0)), full((1, D_A)), full((1, D_A)),
                  full((N_GRP, BLK, BLK)), full((N_GRP, BLK, BLK)), _vmem(), _vmem(), _vmem()],
        out_specs=tok(D), out_shape=jax.ShapeDtypeStruct((T, D), F32),
        scratch_shapes=[pltpu.VMEM((tm, D_A), BF16), pltpu.VMEM((tm, D_A), F32)],
        compiler_params=_params("parallel"),
    )(au, av, gta, gtb, ob, h, mods, g_ln, b_ln, ws, bsb, wa, wb, wout)


def _mix_bwd(au, av, gta, gtb, ob, dh, mods, g_ln, b_ln, ws, bsb, wa, wb, wout, seq, plans=()):
    T = dh.shape[0]
    B = T // seq
    tm = _token_tile(seq, 256)
    tps = seq // tm

    def body(au_ref, av_ref, gta_ref, gtb_ref, ob_ref, dh_ref, m_ref, gl_ref, bl_ref, ws_ref, bs_ref,
             wa_ref, wb_ref, wo_ref,
             dau_ref, dav_ref, dgta_ref, dgtb_ref, dob_ref, dwo_ref, dwa_ref, dwb_ref, dws_ref, dbs_ref, dln_ref,
             mg_ref, vvb_s, z_s, dz_s, dzb_s, dvv_s):
        i = pl.program_id(0)

        @pl.when(i == 0)
        def _():
            for r in (dwo_ref, dwa_ref, dwb_ref, dws_ref, dbs_ref, dln_ref):
                r[...] = jnp.zeros(r.shape, F32)

        @pl.when(i % tps == 0)
        def _():
            mg_ref[...] = jnp.zeros(mg_ref.shape, F32)

        auv = au_ref[...]
        avv = av_ref[...]
        u, thu = _gelu(auv)
        g_ln = gl_ref[...]
        vv, vhat, rstd, thv = _sgu_norm(avv, g_ln, bl_ref[...])
        vvb_s[...] = vv.astype(BF16)
        wsm = _masked_ws(ws_ref)
        for c in range(tm // BLK):
            rows = slice(c * BLK, (c + 1) * BLK)
            for g in range(N_GRP):
                cols = slice(g * BLK, (g + 1) * BLK)
                z_s[rows, cols] = _dot(wsm[g], vvb_s[rows, cols]) + bs_ref[g]
        z = z_s[...]
        yab = (u * z).astype(BF16)
        obb = ob_ref[...].astype(BF16)
        ya = _dot_nt(yab, wa_ref[...])
        yb = _dot_nt(obb, wb_ref[...])
        sa = jax.nn.sigmoid(gta_ref[...])
        sb = jax.nn.sigmoid(gtb_ref[...])
        mb = (sa * ya + sb * yb).astype(BF16)
        dhv = dh_ref[...]
        mg_ref[0, 0:1, :] += jnp.sum(dhv * _dot(mb, wo_ref[...]), axis=0, keepdims=True)
        dmob = (m_ref[0, 5:6, :] * dhv).astype(BF16)
        dwo_ref[...] += _dot_tn(mb, dmob)
        dmerged = _dot_nt(dmob, wo_ref[...])
        dya = dmerged * sa
        dyb = dmerged * sb
        dgta_ref[...] = dya * ya * (1.0 - sa)
        dgtb_ref[...] = dyb * yb * (1.0 - sb)
        dyab = dya.astype(BF16)
        dybb = dyb.astype(BF16)
        dwa_ref[...] += _dot_tn(dyab, yab)
        dwb_ref[...] += _dot_tn(dybb, obb)
        dob_ref[...] = _dot(dybb, wb_ref[...])
        dyap = _dot(dyab, wa_ref[...])
        dau_ref[...] = (dyap * z) * _gelu_grad(auv, thu)
        dz = dyap * u
        dz_s[...] = dz
        dzb_s[...] = dz.astype(BF16)
        for c in range(tm // BLK):
            rows = slice(c * BLK, (c + 1) * BLK)
            for g in range(N_GRP):
                cols = slice(g * BLK, (g + 1) * BLK)
                dzb = dzb_s[rows, cols]
                dvv_s[rows, cols] = _dot_tn(wsm[g], dzb)
                dws_ref[g] += _dot_nt(dzb, vvb_s[rows, cols])
                dbs_ref[g] += dz_s[rows, cols]
        dvv = dvv_s[...]
        dln_ref[0:1, :] += jnp.sum(dvv * vhat, axis=0, keepdims=True)
        dln_ref[1:2, :] += jnp.sum(dvv, axis=0, keepdims=True)
        dvh = dvv * g_ln
        dt = rstd * (dvh - jnp.mean(dvh, axis=-1, keepdims=True)
                     - vhat * jnp.mean(dvh * vhat, axis=-1, keepdims=True))
        dav_ref[...] = dt * _gelu_grad(avv, thv)

    def tok(w):
        return pl.BlockSpec((tm, w), lambda i: (i, 0))

    def full(shape):
        return pl.BlockSpec(shape, lambda i: (0,) * len(shape))

    return _call(
        body, name="mix_bwd", grid=(T // tm,),
        in_specs=[tok(D_A), tok(D_A), tok(D), tok(D), tok(D_B), tok(D),
                  pl.BlockSpec((1, 9, D), lambda i: (i // tps, 0, 0)), full((1, D_A)), full((1, D_A)),
                  full((N_GRP, BLK, BLK)), full((N_GRP, BLK, BLK)), _vmem(), _vmem(), _vmem()],
        out_specs=[tok(D_A), tok(D_A), tok(D), tok(D), tok(D_B), full((D, D)), full((D, D_A)), full((D, D_B)),
                   full((N_GRP, BLK, BLK)), full((N_GRP, BLK, BLK)), full((8, D_A)),
                   pl.BlockSpec((1, 8, D), lambda i: (i // tps, 0, 0))],
        out_shape=[jax.ShapeDtypeStruct((T, D_A), F32), jax.ShapeDtypeStruct((T, D_A), F32),
                   jax.ShapeDtypeStruct((T, D), F32), jax.ShapeDtypeStruct((T, D), F32),
                   jax.ShapeDtypeStruct((T, D_B), F32), jax.ShapeDtypeStruct((D, D), F32),
                   jax.ShapeDtypeStruct((D, D_A), F32), jax.ShapeDtypeStruct((D, D_B), F32),
                   jax.ShapeDtypeStruct((N_GRP, BLK, BLK), F32), jax.ShapeDtypeStruct((N_GRP, BLK, BLK), F32),
                   jax.ShapeDtypeStruct((8, D_A), F32), jax.ShapeDtypeStruct((B, 8, D), F32)],
        scratch_shapes=[pltpu.VMEM((tm, D_A), BF16), pltpu.VMEM((tm, D_A), F32), pltpu.VMEM((tm, D_A), F32),
                        pltpu.VMEM((tm, D_A), BF16), pltpu.VMEM((tm, D_A), F32)],
        args=(au, av, gta, gtb, ob, dh, mods, g_ln, b_ln, ws, bsb, wa, wb, wout), semantics=("arbitrary",),
        plans=plans)


def _prologue(c_pad, w_ada, first_shards):
    cols = w_ada.shape[1]
    plan_w, plan_c = _Gather(first_shards), _Gather([c_pad])
    plan_m = _Gather([jax.ShapeDtypeStruct((N_DEV * 8, cols), F32)])
    n_w = len(first_shards)

    def body(c_ref, wada_ref, *rest):
        w_ins, rest = rest[:n_w], rest[n_w:]
        call_ref, mods_ref = rest[:2]
        w_outs, rest = rest[2:2 + n_w], rest[2 + n_w:]
        cvm, part, local_sem = rest[:3]
        sems = rest[3:]
        sems_w, sems_c, sems_m = sems[:3], sems[3:6], sems[6:9]
        plan_w.start(w_ins, w_outs, sems_w)
        plan_c.start([c_ref], [call_ref], sems_c)
        plan_c.finish([c_ref], [call_ref], sems_c)
        load = pltpu.make_async_copy(call_ref, cvm, local_sem)
        load.start()
        load.wait()
        cv = cvm[...].reshape(N_DEV * 8, D)
        part[...] = _dot((cv * jax.nn.sigmoid(cv)).astype(BF16), wada_ref[...].astype(BF16))
        plan_m.start([part], [mods_ref], sems_m)
        plan_m.finish([part], [mods_ref], sems_m)
        plan_w.finish(w_ins, w_outs, sems_w)

    res = pl.pallas_call(
        body, name="prologue", in_specs=[_any(), _vmem()] + [_any()] * n_w,
        out_specs=[_any()] * (2 + n_w), out_shape=plan_c.out_shape + plan_m.out_shape + plan_w.out_shape,
        scratch_shapes=[pltpu.VMEM((N_DEV, 8, D), F32), pltpu.VMEM((N_DEV * 8, cols), F32), pltpu.SemaphoreType.DMA]
        + plan_w.scratch + plan_c.scratch + plan_m.scratch,
        compiler_params=pltpu.CompilerParams(vmem_limit_bytes=VMEM_LIMIT),
    )(c_pad, w_ada, *first_shards)
    return res[0], res[1], list(res[2:])


def _ada_update(c_all, dm_cols, w, m, v, plans=()):
    n, cols = c_all.shape[0], w.shape[1]

    def body(c_ref, dm_ref, w_ref, m_ref, v_ref, g_ref, d_ref, nm_ref, nv_ref):
        cv = c_ref[...]
        g = _dot_tn((cv * jax.nn.sigmoid(cv)).astype(BF16), dm_ref[...].astype(BF16))
        g_ref[...] = g
        d_ref[...], nm_ref[...], nv_ref[...] = _adamw(w_ref[...], g, m_ref[...], v_ref[...])

    col = pl.BlockSpec((D, 128), lambda j: (0, j))
    return _call(
        body, name="ada_update", grid=(cols // 128,),
        in_specs=[pl.BlockSpec((n, D), lambda j: (0, 0)), pl.BlockSpec((n, 128), lambda j: (0, j)), col, col, col],
        out_specs=[col] * 4, out_shape=[jax.ShapeDtypeStruct(w.shape, F32)] * 4, args=(c_all, dm_cols, w, m, v),
        semantics=("parallel",), plans=plans)


def _mod_finish(mg1, mg2, mg2g, mg3, mods, g1, g2, g3):
    B = mods.shape[0]

    def body(mg1_ref, mg2_ref, mg2g_ref, mg3_ref, m_ref, g1_ref, g2_ref, g3_ref, dm_ref, dgn_ref):
        dgn_ref[...] = jnp.zeros(dgn_ref.shape, F32)
        for k, (mg, g_ref) in enumerate(((mg1_ref, g1_ref), (mg2_ref, g2_ref), (mg3_ref, g3_ref))):
            for b in range(B):
                s_dxr = mg[b, 1:2, :]
                dm_ref[b, 3 * k:3 * k + 1, :] = mg[b, 0:1, :]
                dm_ref[b, 3 * k + 1:3 * k + 2, :] = g_ref[...] * s_dxr
                dm_ref[b, 3 * k + 2:3 * k + 3, :] = mg2g_ref[b, 0:1, :] if k == 1 else mg[b, 2:3, :]
                dgn_ref[k:k + 1, :] += (1.0 + m_ref[b, 3 * k + 1:3 * k + 2, :]) * s_dxr

    args = (mg1, mg2, mg2g, mg3, mods, g1, g2, g3)
    out_shape = [jax.ShapeDtypeStruct((B, 9, D), F32), jax.ShapeDtypeStruct((8, D), F32)]
    return pl.pallas_call(body, name="mod_finish", grid=(1,), in_specs=[_whole(a) for a in args],
                          out_specs=[_whole(o) for o in out_shape], out_shape=out_shape,
                          compiler_params=_params("arbitrary"))(*args)


def _small_update(gathered, params, ms, vs):
    n = len(SMALL)
    B = gathered[0].shape[1]

    def body(*refs):
        gdm, ggn, gln, gws, gbs, ggq, ggk, gsk, gloss = refs[:9]
        w = dict(zip(SMALL, refs[9:9 + n]))
        m = dict(zip(SMALL, refs[9 + n:9 + 2 * n]))
        v = dict(zip(SMALL, refs[9 + 2 * n:9 + 3 * n]))
        outs = refs[9 + 3 * n:-1]
        out = {name: outs[4 * k:4 * k + 4] for k, name in enumerate(SMALL)}

        def total(ref, idx):
            acc = ref[(0,) + idx]
            for dev in range(1, N_DEV):
                acc = acc + ref[(dev,) + idx]
            return acc

        def finish(name, g, idx=(Ellipsis,)):
            d, nm, nv = _adamw(w[name][idx], g, m[name][idx], v[name][idx])
            for ref, val in zip(out[name], (g, d, nm, nv)):
                ref[idx] = val

        g_bada = total(gdm, (slice(0, 1),))
        for b in range(1, B):
            g_bada = g_bada + total(gdm, (slice(b, b + 1),))
        finish("b_ada", g_bada)
        finish("g_norm1", total(ggn, (slice(0, 1),)))
        finish("g_norm2", total(ggn, (slice(1, 2),)))
        finish("g_norm3", total(ggn, (slice(2, 3),)))
        finish("g_sgu_ln", total(gln, (slice(0, 1),)))
        finish("b_sgu_ln", total(gln, (slice(1, 2),)))
        tril = lax.broadcasted_iota(jnp.int32, (BLK, BLK), 0) >= lax.broadcasted_iota(jnp.int32, (BLK, BLK), 1)
        for g in range(N_GRP):
            finish("w_spatial", jnp.where(tril, total(gws, (g,)), 0.0), (0, g))
            finish("b_spatial", jnp.sum(total(gbs, (g,)).T, axis=0, keepdims=True), (0, slice(g, g + 1)))
        finish("g_q", total(ggq, ()))
        finish("g_k", total(ggk, ()))
        finish("attn_sinks", total(gsk, (slice(0, 1), slice(0, N_KV * Q_PER_KV))))
        refs[-1][...] = total(gloss, ())

    args = list(gathered) + [params[k] for k in SMALL] + [ms[k] for k in SMALL] + [vs[k] for k in SMALL]
    out_shape = []
    for k in SMALL:
        out_shape += [jax.ShapeDtypeStruct(params[k].shape, F32)] * 4
    out_shape.append(jax.ShapeDtypeStruct((8, 128), F32))
    res = pl.pallas_call(body, name="small_update", grid=(1,), in_specs=[_whole(a) for a in args],
                         out_specs=[_whole(o) for o in out_shape], out_shape=out_shape,
                         compiler_params=_params("arbitrary"))(*args)
    return {k: res[4 * i:4 * i + 4] for i, k in enumerate(SMALL)}, res[-1][0, 0]


def _rs_add(p, r, core, name):
    _, rows, width = p.shape

    def body(c_ref, p_ref, r_ref, o_ref):
        o_ref[...] = (p_ref[...] + r_ref[...]).astype(BF16)

    return pl.pallas_call(
        body, name="rs_add_" + name, out_shape=jax.ShapeDtypeStruct((4, rows, width), BF16),
        grid_spec=pltpu.PrefetchScalarGridSpec(
            num_scalar_prefetch=1, grid=(4,),
            in_specs=[pl.BlockSpec((1, rows, width), lambda k, c_ref: (2 * k + c_ref[0], 0, 0)),
                      pl.BlockSpec((1, rows, width), lambda k, c_ref: (k, 0, 0))],
            out_specs=pl.BlockSpec((1, rows, width), lambda k, c_ref: (k, 0, 0))),
        compiler_params=_params("parallel"),
    )(core, p, r)


def _adam_big(w, m, v, r, name):
    rows, cols = w.shape

    def body(r_ref, w_ref, m_ref, v_ref, go_ref, d_ref, nm_ref, nv_ref):
        g = r_ref[0].astype(F32)
        for k in range(1, 4):
            g = g + r_ref[k].astype(F32)
        go_ref[...] = g
        d_ref[...], nm_ref[...], nv_ref[...] = _adamw(w_ref[...], g, m_ref[...], v_ref[...])

    br = rows // 2
    shard = pl.BlockSpec((br, cols), lambda i: (i, 0))
    return pl.pallas_call(body, name="adam_" + name, grid=(2,),
                          in_specs=[pl.BlockSpec((4, br, cols), lambda i: (0, i, 0)), shard, shard, shard],
                          out_specs=[shard] * 4, out_shape=[jax.ShapeDtypeStruct(w.shape, F32)] * 4,
                          compiler_params=_params("parallel"))(r, w, m, v)


def kernel(x, c, w_ada, b_ada, g_norm1, ffn1_w_gate, ffn1_w_up, ffn1_w_down, g_norm2, w_in, g_sgu_ln, b_sgu_ln, w_spatial, b_spatial, g_q, g_k, attn_sinks, w_branch_a, w_branch_b, w_out, g_norm3, ffn2_w_gate, ffn2_w_up, ffn2_w_down, loss_target, m_w_ada, m_b_ada, m_g_norm1, m_ffn1_w_gate, m_ffn1_w_up, m_ffn1_w_down, m_g_norm2, m_w_in, m_g_sgu_ln, m_b_sgu_ln, m_w_spatial, m_b_spatial, m_g_q, m_g_k, m_attn_sinks, m_w_branch_a, m_w_branch_b, m_w_out, m_g_norm3, m_ffn2_w_gate, m_ffn2_w_up, m_ffn2_w_down, v_w_ada, v_b_ada, v_g_norm1, v_ffn1_w_gate, v_ffn1_w_up, v_ffn1_w_down, v_g_norm2, v_w_in, v_g_sgu_ln, v_b_sgu_ln, v_w_spatial, v_b_spatial, v_g_q, v_g_k, v_attn_sinks, v_w_branch_a, v_w_branch_b, v_w_out, v_g_norm3, v_ffn2_w_gate, v_ffn2_w_up, v_ffn2_w_down):
    names = ("w_ada", "b_ada", "g_norm1", "ffn1_w_gate", "ffn1_w_up", "ffn1_w_down", "g_norm2", "w_in", "g_sgu_ln",
             "b_sgu_ln", "w_spatial", "b_spatial", "g_q", "g_k", "attn_sinks", "w_branch_a", "w_branch_b", "w_out",
             "g_norm3", "ffn2_w_gate", "ffn2_w_up", "ffn2_w_down")
    w = dict(zip(names, (w_ada, b_ada, g_norm1, ffn1_w_gate, ffn1_w_up, ffn1_w_down, g_norm2, w_in, g_sgu_ln,
                         b_sgu_ln, w_spatial, b_spatial, g_q, g_k, attn_sinks, w_branch_a, w_branch_b, w_out, g_norm3,
                         ffn2_w_gate, ffn2_w_up, ffn2_w_down)))
    m = dict(zip(names, (m_w_ada, m_b_ada, m_g_norm1, m_ffn1_w_gate, m_ffn1_w_up, m_ffn1_w_down, m_g_norm2, m_w_in,
                         m_g_sgu_ln, m_b_sgu_ln, m_w_spatial, m_b_spatial, m_g_q, m_g_k, m_attn_sinks, m_w_branch_a,
                         m_w_branch_b, m_w_out, m_g_norm3, m_ffn2_w_gate, m_ffn2_w_up, m_ffn2_w_down)))
    v = dict(zip(names, (v_w_ada, v_b_ada, v_g_norm1, v_ffn1_w_gate, v_ffn1_w_up, v_ffn1_w_down, v_g_norm2, v_w_in,
                         v_g_sgu_ln, v_b_sgu_ln, v_w_spatial, v_b_spatial, v_g_q, v_g_k, v_attn_sinks, v_w_branch_a,
                         v_w_branch_b, v_w_out, v_g_norm3, v_ffn2_w_gate, v_ffn2_w_up, v_ffn2_w_down)))
    B, seq, _ = x.shape
    T = B * seq
    nbs = seq // BLK
    xi, yi, ci = _place()
    me = 4 * xi + 2 * yi + ci
    core = jnp.reshape(ci, (1,)).astype(jnp.int32)
    layout = {name: (tform, n, width) for name, tform, n, width in BIG}
    shard = {name: w[name][0].astype(BF16).T if tform else w[name][0].astype(BF16) for name, tform, _, _ in BIG}
    wts, parts, big_out = {}, {}, {}

    def gather_plan(group):
        return _Gather([shard[k] for k in group])

    def take(group, gathered):
        for k, g in zip(group, gathered):
            wts[k] = g.reshape(N_DEV * layout[k][1], layout[k][2])

    def blocks(name, grad):
        return grad.reshape(N_DEV, layout[name][1], layout[name][2])

    def to_sibling(names, grads):
        return _RsSibling([blocks(k, g) for k, g in zip(names, grads)])

    def add(names, grads, from_sibling):
        for k, g, r in zip(names, grads, from_sibling):
            parts[k] = _rs_add(blocks(k, g), r, core, k)

    def to_chips(names):
        return _RsChips([parts[k] for k in names])

    def update(names, from_chips):
        for k, r in zip(names, from_chips):
            if layout[k][0]:
                big_out[k] = [o.T[None] for o in _adam_big(w[k][0].T, m[k][0].T, v[k][0].T, r, k)]
            else:
                big_out[k] = [o[None] for o in _adam_big(w[k][0], m[k][0], v[k][0], r, k)]

    g1, g2, g3, g_ln, b_ln = g_norm1, g_norm2, g_norm3, g_sgu_ln, b_sgu_ln
    gq2, gk2 = jnp.tile(g_q, (1, 2)), jnp.tile(g_k, (1, 2))
    ws = w_spatial[0]
    bsb = jnp.broadcast_to(b_spatial[0][:, :, None], (N_GRP, BLK, BLK))
    xf = x.reshape(T, D)
    tgt = loss_target.reshape(T, D)

    ffn1 = ("ffn1_w_gate", "ffn1_w_up", "ffn1_w_down")
    ffn2 = ("ffn2_w_gate", "ffn2_w_up", "ffn2_w_down")
    c_pad = jnp.concatenate([c, jnp.zeros((8 - B, D), F32)], axis=0)
    c_all, mods_cols, gathered = _prologue(c_pad, w_ada[0], [shard[k] for k in ffn1[:2]])
    take(ffn1[:2], gathered)
    c_all = c_all[:, :B].reshape(N_DEV * B, D)
    mine = lax.dynamic_slice_in_dim(mods_cols, 8 * me, B, axis=1)
    mods = (mine.transpose(1, 0, 2).reshape(B, 9 * D) + b_ada).reshape(B, 9, D)

    group = ("ffn1_w_down", "w_in")
    (gb1, ub1), (gathered,) = _ffn_gate_up(xf, mods, g1, wts["ffn1_w_gate"], wts["ffn1_w_up"], 0, seq,
                                           plans=[gather_plan(group)])
    take(group, gathered)
    group = ("w_branch_a", "w_branch_b", "w_out")
    (h1, y1), (gathered,) = _ffn_down(xf, gb1, ub1, mods, wts["ffn1_w_down"], 0, seq, plans=[gather_plan(group)])
    take(group, gathered)
    group = ("ffn2_w_gate",)
    (au, av, q_tok, k_tok, v_tok, gta, gtb), (gathered,) = _inproj_fwd(h1, mods, g2, wts["w_in"], seq,
                                                                       plans=[gather_plan(group)])
    take(group, gathered)
    group = ("ffn2_w_up", "ffn2_w_down")
    ob, (gathered,) = _swa_fwd(q_tok, k_tok, v_tok, gq2, gk2, attn_sinks, nbs, plans=[gather_plan(group)])
    take(group, gathered)
    mixw = (wts["w_branch_a"], wts["w_branch_b"], wts["w_out"])
    h2 = _mix_fwd(au, av, gta, gtb, ob, h1, mods, g_ln, b_ln, ws, bsb, *mixw, seq)
    (dh3, y3, gb3, ub3, loss_parts), _ = _ffn_fwd(h2, mods, g3, *[wts[k] for k in ffn2], 6, seq, tgt=tgt)
    loss_part = jnp.full((8, 128), jnp.sum(loss_parts[:, 0, 0]))

    (dh2, xb, dyb, a, dg, du, mg3), _ = _ffn_bwd(h2, dh3, y3, gb3, ub3, mods, g3, *[wts[k] for k in ffn2], 6, seq)
    d_gate, _ = _wgrad(dg, xb, "wgrad_ffn2_gate")
    d_up, (r,) = _wgrad(du, xb, "wgrad_ffn2_up", plans=[to_sibling(ffn2[:1], [d_gate])])
    add(ffn2[:1], [d_gate], r)
    d_down, (r,) = _wgrad(a, dyb, "wgrad_ffn2_down", plans=[to_sibling(ffn2[1:2], [d_up])])
    add(ffn2[1:2], [d_up], r)

    (dau, dav, dgta, dgtb, dob, d_out, d_a, d_b, dws, dbs, dln, mg2g), (r, from_chips) = _mix_bwd(
        au, av, gta, gtb, ob, dh2, mods, g_ln, b_ln, ws, bsb, *mixw, seq,
        plans=[to_sibling(ffn2[2:], [d_down]), to_chips(ffn2[:2])])
    add(ffn2[2:], [d_down], r)
    update(ffn2[:2], from_chips)
    mixers = ("w_out", "w_branch_a", "w_branch_b")
    (dq, dk, dv, dk_halo, dv_halo, dgq2, dgk2, dsk), (from_chips, r) = _swa_bwd(
        q_tok, k_tok, v_tok, gq2, gk2, attn_sinks, dob, nbs,
        plans=[to_chips(ffn2[2:]), to_sibling(mixers, [d_out, d_a, d_b])])
    update(ffn2[2:], from_chips)
    add(mixers, [d_out, d_a, d_b], r)
    dk, dv = _swa_add_halo(dk, dk_halo), _swa_add_halo(dv, dv_halo)
    dgq, dgk = dgq2[:, :HD] + dgq2[:, HD:], dgk2[:, :HD] + dgk2[:, HD:]
    (dh1, xb2, dpb, mg2), (from_chips, early) = _inproj_bwd(
        h1, dh2, (dau, dav, dq, dk, dv, dgta, dgtb), mods, g2, wts["w_in"], seq,
        plans=[to_chips(mixers), _Gather([dln, dws, dbs, dgq, dgk, dsk, loss_part])])
    update(mixers, from_chips)

    (dx, xb, dyb, a, dg, du, mg1), _ = _ffn_bwd(xf, dh1, y1, gb1, ub1, mods, g1, *[wts[k] for k in ffn1], 0, seq)
    dmods, dgn = _mod_finish(mg1, mg2, mg2g, mg3, mods, g1, g2, g3)
    d_gate, (late,) = _wgrad(dg, xb, "wgrad_ffn1_gate", plans=[_Gather([dmods.reshape(B, 9 * D), dgn])])
    gathered = late + early
    d_up, (r,) = _wgrad(du, xb, "wgrad_ffn1_up", plans=[to_sibling(ffn1[:1], [d_gate])])
    add(ffn1[:1], [d_gate], r)
    d_in, (r, from_chips) = _wgrad(dpb, xb2, "wgrad_w_in",
                                   plans=[to_sibling(ffn1[1:2], [d_up]), to_chips(ffn1[:1])])
    add(ffn1[1:2], [d_up], r)
    update(ffn1[:1], from_chips)
    d_down, (r, from_chips) = _wgrad(a, dyb, "wgrad_ffn1_down",
                                     plans=[to_sibling(("w_in",), [d_in]), to_chips(ffn1[1:2])])
    add(("w_in",), [d_in], r)
    update(ffn1[1:2], from_chips)
    small_out, loss = _small_update(gathered, w, m, v)
    dm_cols = lax.dynamic_slice_in_dim(gathered[0].reshape(N_DEV * B, 9 * D), (9 * D // N_DEV) * me,
                                       9 * D // N_DEV, axis=1)
    ada_out, (r, from_chips) = _ada_update(c_all, dm_cols, w_ada[0], m_w_ada[0], v_w_ada[0],
                                           plans=[to_sibling(ffn1[2:], [d_down]), to_chips(("w_in",))])
    add(ffn1[2:], [d_down], r)
    update(("w_in",), from_chips)
    (from_chips,) = _exchange([to_chips(ffn1[2:])], "rs_last")
    update(ffn1[2:], from_chips)

    def leaf(kind, name):
        if name == "w_ada":
            return ada_out[kind][None]
        if name in SMALL:
            return small_out[name][kind]
        return big_out[name][kind]

    return (loss, dx.reshape(B, seq, D), *[leaf(kind, name) for kind in range(4) for name in names])
```

```python
import functools
import math

import jax
import jax.numpy as jnp
from jax import lax
from jax.experimental import pallas as pl
from jax.experimental.pallas import tpu as pltpu

F32 = jnp.float32
BF16 = jnp.bfloat16
MESH = pl.DeviceIdType.MESH
AXES = ("x", "y", "c")
N_DEV = 8

VMEM_LIMIT = 56 * 1024 * 1024

D = 1024
FF = 2816
FC = 1408
D_A = 512
D_B = 512
HD = 64
N_KV = 2
Q_PER_KV = 4
BLK = 128
N_GRP = 4
IN_COLS = 3840
PIECES = (("au", 0, 512), ("av", 512, 512), ("q", 1024, 512), ("k", 1536, 128), ("v", 1664, 128),
          ("ga", 1792, 1024), ("gb", 2816, 1024))
EPS = 1e-6
NEG = -1e30
GELU_C = math.sqrt(2.0 / math.pi)

ADAM_LR = 0.001
ADAM_B1 = 0.9
ADAM_B2 = 0.999
ADAM_EPS = 1e-08
ADAM_WD = 0.01
ADAM_STEP = 10

NT = (((1,), (1,)), ((), ()))
TN = (((0,), (0,)), ((), ()))

BIG = (("ffn1_w_gate", True, FF // N_DEV, D), ("ffn1_w_up", True, FF // N_DEV, D),
       ("ffn1_w_down", False, FF // N_DEV, D), ("w_in", True, IN_COLS // N_DEV, D),
       ("w_branch_a", True, D // N_DEV, D_A), ("w_branch_b", True, D // N_DEV, D_B), ("w_out", False, D // N_DEV, D),
       ("ffn2_w_gate", True, FF // N_DEV, D), ("ffn2_w_up", True, FF // N_DEV, D),
       ("ffn2_w_down", False, FF // N_DEV, D))
SMALL = ("b_ada", "g_norm1", "g_norm2", "g_sgu_ln", "b_sgu_ln", "w_spatial", "b_spatial", "g_q", "g_k",
         "attn_sinks", "g_norm3")


def _dot(a, b):
    return jnp.dot(a, b, preferred_element_type=F32)


def _dot_nt(a, b):
    return lax.dot_general(a, b, NT, preferred_element_type=F32)


def _dot_tn(a, b):
    return lax.dot_general(a, b, TN, preferred_element_type=F32)


def _vmem():
    return pl.BlockSpec(memory_space=pltpu.VMEM)


def _any():
    return pl.BlockSpec(memory_space=pl.ANY)


def _whole(a):
    return pl.BlockSpec(a.shape, lambda i: (0,) * len(a.shape))


def _rms_mod(h, g, sh, sc):
    inv = lax.rsqrt(jnp.mean(h * h, axis=-1, keepdims=True) + EPS)
    r = h * inv
    return (r * g) * (1.0 + sc) + sh, r, inv


def _rms_mod_bwd(dxn, r, inv, g, sc):
    dr = dxn * (g * (1.0 + sc))
    dh = inv * (dr - r * jnp.mean(dr * r, axis=-1, keepdims=True))
    return dh, jnp.sum(dxn, axis=0, keepdims=True), jnp.sum(dxn * r, axis=0, keepdims=True)


def _gelu(x):
    t = jnp.tanh(GELU_C * (x + 0.044715 * (x * x * x)))
    return 0.5 * x * (1.0 + t), t


def _gelu_grad(x, t):
    return 0.5 * (1.0 + t) + 0.5 * x * (1.0 - t * t) * (GELU_C * (1.0 + 3.0 * 0.044715 * x * x))


def _adamw(w, g, m, v):
    m = ADAM_B1 * m + (1.0 - ADAM_B1) * g
    v = ADAM_B2 * v + (1.0 - ADAM_B2) * (g * g)
    m_hat = m / (1.0 - ADAM_B1 ** ADAM_STEP)
    v_hat = v / (1.0 - ADAM_B2 ** ADAM_STEP)
    delta = -ADAM_LR * (m_hat / (jnp.sqrt(v_hat) + ADAM_EPS) + ADAM_WD * w)
    return delta, m, v


def _token_tile(seq, cap=512):
    return min(cap, seq)


def _params(*semantics):
    return pltpu.CompilerParams(dimension_semantics=semantics, vmem_limit_bytes=VMEM_LIMIT)


def _place():
    return lax.axis_index("x"), lax.axis_index("y"), lax.axis_index("c")


class _Gather:
    def __init__(self, arrays):
        n = len(arrays)
        self.ins = list(arrays)
        self.out_shape = [jax.ShapeDtypeStruct((N_DEV,) + a.shape, a.dtype) for a in arrays]
        self.scratch = [pltpu.SemaphoreType.DMA((n, 7)), pltpu.SemaphoreType.DMA((n, 7)),
                        pltpu.SemaphoreType.DMA((n,))]

    def _copies(self, ins, outs, sems):
        send_sems, recv_sems, local_sems = sems
        n = len(ins)
        x, y, c = _place()
        me, sibling = (x, y, c), (x, y, 1 - c)
        chips = [(1 - x, y), (x, 1 - y), (1 - x, 1 - y)]

        def slot(a, px, py, pc):
            return outs[a].at[4 * px + 2 * py + pc]

        def copy(a, k, block, to, src=None):
            return pltpu.make_async_remote_copy(
                src_ref=slot(a, *block) if src is None else src, dst_ref=slot(a, *block),
                send_sem=send_sems.at[a, k], recv_sem=recv_sems.at[a, k], device_id=to, device_id_type=MESH)

        mine = [pltpu.make_async_copy(ins[a], slot(a, *me), local_sems.at[a]) for a in range(n)]
        first = [copy(a, 0, me, sibling, src=ins[a]) for a in range(n)]
        first += [copy(a, 1 + j, me, (*chip, c), src=ins[a]) for a in range(n) for j, chip in enumerate(chips)]
        landed = [[copy(a, 1 + j, (*chip, c), me) for a in range(n)] for j, chip in enumerate(chips)]
        passed = [[copy(a, 4 + j, (*chip, c), sibling) for a in range(n)] for j, chip in enumerate(chips)]
        from_sibling = [copy(a, 0, sibling, me) for a in range(n)]
        from_sibling += [copy(a, 4 + j, (*chip, 1 - c), me) for a in range(n) for j, chip in enumerate(chips)]
        return mine, first, landed, passed, from_sibling

    def start(self, ins, outs, sems):
        mine, first, _, _, _ = self._copies(ins, outs, sems)
        for cp in mine + first:
            cp.start()

    def relay(self, ins, outs, sems):
        _, _, landed, passed, _ = self._copies(ins, outs, sems)
        for arrivals, forwards in zip(landed, passed):
            for arrival, forward in zip(arrivals, forwards):
                arrival.wait_recv()
                forward.start()

    def finish(self, ins, outs, sems):
        mine, first, _, passed, from_sibling = self._copies(ins, outs, sems)
        for cp in from_sibling:
            cp.wait_recv()
        for cp in first + [f for fs in passed for f in fs]:
            cp.wait_send()
        for cp in mine:
            cp.wait()


class _RsSibling:
    def __init__(self, ps):
        n = len(ps)
        self.ins = list(ps)
        self.out_shape = [jax.ShapeDtypeStruct((4,) + p.shape[1:], p.dtype) for p in ps]
        self.scratch = [pltpu.SemaphoreType.DMA((n, 4)), pltpu.SemaphoreType.DMA((n, 4))]

    def _copies(self, ins, outs, sems):
        send_sems, recv_sems = sems
        x, y, c = _place()
        return [pltpu.make_async_remote_copy(
            src_ref=ins[a].at[2 * q + (1 - c)], dst_ref=outs[a].at[q], send_sem=send_sems.at[a, q],
            recv_sem=recv_sems.at[a, q], device_id=(x, y, 1 - c), device_id_type=MESH)
            for a in range(len(ins)) for q in range(4)]

    def start(self, ins, outs, sems):
        for cp in self._copies(ins, outs, sems):
            cp.start()

    def relay(self, ins, outs, sems):
        pass

    def finish(self, ins, outs, sems):
        for cp in self._copies(ins, outs, sems):
            cp.wait()


class _RsChips:
    def __init__(self, qs):
        n = len(qs)
        self.ins = list(qs)
        self.out_shape = [jax.ShapeDtypeStruct(q.shape, q.dtype) for q in qs]
        self.scratch = [pltpu.SemaphoreType.DMA((n, 3)), pltpu.SemaphoreType.DMA((n, 3)),
                        pltpu.SemaphoreType.DMA((n,))]

    def _copies(self, ins, outs, sems):
        send_sems, recv_sems, local_sems = sems
        n = len(ins)
        x, y, c = _place()
        my_chip = 2 * x + y
        chips = [(1 - x, y), (x, 1 - y), (1 - x, 1 - y)]

        def copy(a, j, src_slot, dst_slot):
            px, py = chips[j]
            return pltpu.make_async_remote_copy(
                src_ref=ins[a].at[src_slot], dst_ref=outs[a].at[dst_slot], send_sem=send_sems.at[a, j],
                recv_sem=recv_sems.at[a, j], device_id=(px, py, c), device_id_type=MESH)

        own = [pltpu.make_async_copy(ins[a].at[my_chip], outs[a].at[my_chip], local_sems.at[a]) for a in range(n)]
        sends = [copy(a, j, 2 * px + py, my_chip) for a in range(n) for j, (px, py) in enumerate(chips)]
        arrivals = [copy(a, j, my_chip, 2 * px + py) for a in range(n) for j, (px, py) in enumerate(chips)]
        return own, sends, arrivals

    def start(self, ins, outs, sems):
        own, sends, _ = self._copies(ins, outs, sems)
        for cp in own + sends:
            cp.start()

    def relay(self, ins, outs, sems):
        pass

    def finish(self, ins, outs, sems):
        own, sends, arrivals = self._copies(ins, outs, sems)
        for cp in arrivals:
            cp.wait_recv()
        for cp in sends:
            cp.wait_send()
        for cp in own:
            cp.wait()


def _split_plans(plans, refs_in, refs_out, refs_scr, phase):
    i = o = s = 0
    for p in plans:
        ni, no, ns = len(p.ins), len(p.out_shape), len(p.scratch)
        getattr(p, phase)(refs_in[i:i + ni], refs_out[o:o + no], refs_scr[s:s + ns])
        i, o, s = i + ni, o + no, s + ns


def _plan_results(plans, res):
    out = []
    for p in plans:
        out.append(list(res[:len(p.out_shape)]))
        res = res[len(p.out_shape):]
    return out


def _exchange(plans, name):
    c_in = [a for p in plans for a in p.ins]
    c_out = [s for p in plans for s in p.out_shape]
    c_scr = [s for p in plans for s in p.scratch]

    def body(*refs):
        cin, cout, cscr = refs[:len(c_in)], refs[len(c_in):len(c_in) + len(c_out)], refs[len(c_in) + len(c_out):]
        for phase in ("start", "relay", "finish"):
            _split_plans(plans, cin, cout, cscr, phase)

    res = pl.pallas_call(body, name=name, in_specs=[_any()] * len(c_in), out_specs=[_any()] * len(c_out),
                         out_shape=c_out, scratch_shapes=c_scr)(*c_in)
    return _plan_results(plans, res)


def _call(body, *, name, grid, in_specs, out_specs, out_shape, args, semantics, scratch_shapes=(), plans=()):
    n_in, n_out, n_scr = len(in_specs), len(out_specs), len(scratch_shapes)
    c_in = [a for p in plans for a in p.ins]
    c_out = [s for p in plans for s in p.out_shape]
    c_scr = [s for p in plans for s in p.scratch]
    n_steps = math.prod(grid)

    def wrapped(*refs):
        ins, refs = refs[:n_in], refs[n_in:]
        cin, refs = refs[:len(c_in)], refs[len(c_in):]
        outs, refs = refs[:n_out], refs[n_out:]
        cout, refs = refs[:len(c_out)], refs[len(c_out):]
        scr, cscr = refs[:n_scr], refs[n_scr:]
        if plans:
            step = 0
            for d, g in enumerate(grid):
                step = step * g + pl.program_id(d)
            pl.when(step == 0)(lambda: _split_plans(plans, cin, cout, cscr, "start"))
        body(*ins, *outs, *scr)
        if plans:
            relay_step = max((3 * n_steps) // 4 - 1, 0)
            pl.when(step == relay_step)(lambda: _split_plans(plans, cin, cout, cscr, "relay"))
            pl.when(step == n_steps - 1)(lambda: _split_plans(plans, cin, cout, cscr, "finish"))

    res = pl.pallas_call(
        wrapped, name=name, grid=grid, in_specs=list(in_specs) + [_any()] * len(c_in),
        out_specs=list(out_specs) + [_any()] * len(c_out), out_shape=list(out_shape) + c_out,
        scratch_shapes=list(scratch_shapes) + c_scr,
        compiler_params=_params(*(("arbitrary",) * len(grid) if plans else semantics)),
    )(*args, *c_in)
    return list(res[:n_out]), _plan_results(plans, res[n_out:])


def _ffn_fwd(h, mods, gn, wg, wu, wd, row0, seq, tgt=None, plans=()):
    T = h.shape[0]
    tm = _token_tile(seq, 256)
    tps = seq // tm
    n_t = T // tm
    with_loss = tgt is not None

    def body(h_ref, m_ref, g_ref, wg_ref, wu_ref, wd_ref, *rest):
        if with_loss:
            tgt_ref, out_ref, y_ref, gb_ref, ub_ref, loss_ref = rest
        else:
            out_ref, y_ref, gb_ref, ub_ref = rest
        hv = h_ref[...]
        sh = m_ref[0, row0:row0 + 1, :]
        sc = m_ref[0, row0 + 1:row0 + 2, :]
        ga = m_ref[0, row0 + 2:row0 + 3, :]
        xn, _, _ = _rms_mod(hv, g_ref[...], sh, sc)
        xb = xn.astype(BF16)
        acc = jnp.zeros((tm, D), F32)
        for c0 in range(0, FF, FC):
            gg = _dot_nt(xb, wg_ref[c0:c0 + FC, :])
            uu = _dot_nt(xb, wu_ref[c0:c0 + FC, :])
            gg, uu = gg.astype(BF16), uu.astype(BF16)
            gb_ref[:, c0:c0 + FC] = gg
            ub_ref[:, c0:c0 + FC] = uu
            acc = acc + _dot((gg * jax.nn.sigmoid(gg)) * uu, wd_ref[c0:c0 + FC, :])
        y_ref[...] = acc
        hout = hv + (0.5 * ga) * acc
        if with_loss:
            d = hout - tgt_ref[...]
            out_ref[...] = d * (1.0 / D)
            loss_ref[...] = jnp.full((1, 8, 128), 0.5 / D, F32) * jnp.sum(d * d)
        else:
            out_ref[...] = hout

    tok = pl.BlockSpec((tm, D), lambda i: (i, 0))
    tokf = pl.BlockSpec((tm, FF), lambda i: (i, 0))
    in_specs = [tok, pl.BlockSpec((1, 9, D), lambda i: (i // tps, 0, 0)), pl.BlockSpec((1, D), lambda i: (0, 0)),
                _vmem(), _vmem(), _vmem()]
    out_shape = [jax.ShapeDtypeStruct((T, D), F32), jax.ShapeDtypeStruct((T, D), F32),
                 jax.ShapeDtypeStruct((T, FF), BF16), jax.ShapeDtypeStruct((T, FF), BF16)]
    out_specs = [tok, tok, tokf, tokf]
    args = [h, mods, gn, wg, wu, wd]
    if with_loss:
        in_specs.append(tok)
        args.append(tgt)
        out_shape.append(jax.ShapeDtypeStruct((n_t, 8, 128), F32))
        out_specs.append(pl.BlockSpec((1, 8, 128), lambda i: (i, 0, 0)))
    return _call(body, name="ffn_fwd_loss" if with_loss else "ffn_fwd", grid=(n_t,), in_specs=in_specs,
                 out_specs=out_specs, out_shape=out_shape, args=args, semantics=("parallel",), plans=plans)


def _ffn_gate_up(h, mods, gn, wg, wu, row0, seq, plans=()):
    T = h.shape[0]
    tm = _token_tile(seq, 256)
    tps = seq // tm

    def body(h_ref, m_ref, g_ref, wg_ref, wu_ref, gb_ref, ub_ref):
        xn, _, _ = _rms_mod(h_ref[...], g_ref[...], m_ref[0, row0:row0 + 1, :], m_ref[0, row0 + 1:row0 + 2, :])
        xb = xn.astype(BF16)
        for c0 in range(0, FF, FC):
            gb_ref[:, c0:c0 + FC] = _dot_nt(xb, wg_ref[c0:c0 + FC, :]).astype(BF16)
            ub_ref[:, c0:c0 + FC] = _dot_nt(xb, wu_ref[c0:c0 + FC, :]).astype(BF16)

    tokf = pl.BlockSpec((tm, FF), lambda i: (i, 0))
    return _call(
        body, name="ffn_gate_up", grid=(T // tm,),
        in_specs=[pl.BlockSpec((tm, D), lambda i: (i, 0)), pl.BlockSpec((1, 9, D), lambda i: (i // tps, 0, 0)),
                  pl.BlockSpec((1, D), lambda i: (0, 0)), _vmem(), _vmem()],
        out_specs=[tokf, tokf], out_shape=[jax.ShapeDtypeStruct((T, FF), BF16)] * 2, args=(h, mods, gn, wg, wu),
        semantics=("parallel",), plans=plans)


def _ffn_down(h, gb, ub, mods, wd, row0, seq, plans=()):
    T = h.shape[0]
    tm = _token_tile(seq, 256)
    tps = seq // tm

    def body(h_ref, gb_ref, ub_ref, m_ref, wd_ref, out_ref, y_ref):
        acc = jnp.zeros((tm, D), F32)
        for c0 in range(0, FF, FC):
            gg = gb_ref[:, c0:c0 + FC]
            acc = acc + _dot((gg * jax.nn.sigmoid(gg)) * ub_ref[:, c0:c0 + FC], wd_ref[c0:c0 + FC, :])
        y_ref[...] = acc
        out_ref[...] = h_ref[...] + (0.5 * m_ref[0, row0 + 2:row0 + 3, :]) * acc

    tok = pl.BlockSpec((tm, D), lambda i: (i, 0))
    tokf = pl.BlockSpec((tm, FF), lambda i: (i, 0))
    return _call(
        body, name="ffn_down", grid=(T // tm,),
        in_specs=[tok, tokf, tokf, pl.BlockSpec((1, 9, D), lambda i: (i // tps, 0, 0)), _vmem()],
        out_specs=[tok, tok], out_shape=[jax.ShapeDtypeStruct((T, D), F32)] * 2, args=(h, gb, ub, mods, wd),
        semantics=("parallel",), plans=plans)


def _ffn_bwd(h, dhn, y, gb, ub, mods, gn, wg, wu, wd, row0, seq, plans=()):
    T = h.shape[0]
    B = T // seq
    tm = _token_tile(seq, 256)
    tps = seq // tm
    n_t = T // tm

    def body(h_ref, dhn_ref, y_ref, gb_ref, ub_ref, m_ref, g_ref, wg_ref, wu_ref, wd_ref,
             dh_ref, xb_ref, dyb_ref, a_ref, dg_ref, du_ref, mg_ref):
        i = pl.program_id(0)
        hv = h_ref[...]
        dhn = dhn_ref[...]
        sh = m_ref[0, row0:row0 + 1, :]
        sc = m_ref[0, row0 + 1:row0 + 2, :]
        ga = m_ref[0, row0 + 2:row0 + 3, :]
        g = g_ref[...]
        xn, r, inv = _rms_mod(hv, g, sh, sc)
        xb_ref[...] = xn.astype(BF16)
        dyb = ((0.5 * ga) * dhn).astype(BF16)
        dyb_ref[...] = dyb
        dga = 0.5 * jnp.sum(dhn * y_ref[...], axis=0, keepdims=True)
        dxn = jnp.zeros((tm, D), F32)
        for c0 in range(0, FF, FC):
            wgc = wg_ref[c0:c0 + FC, :]
            wuc = wu_ref[c0:c0 + FC, :]
            gg = gb_ref[:, c0:c0 + FC]
            uu = ub_ref[:, c0:c0 + FC]
            sig = jax.nn.sigmoid(gg)
            s = gg * sig
            a_ref[:, c0:c0 + FC] = s * uu
            da = _dot_nt(dyb, wd_ref[c0:c0 + FC, :]).astype(BF16)
            dub = da * s
            dgb = (da * uu) * (sig * (1.0 + gg * (1.0 - sig)))
            dg_ref[:, c0:c0 + FC] = dgb
            du_ref[:, c0:c0 + FC] = dub
            dxn = dxn + _dot(dgb, wgc) + _dot(dub, wuc)
        dh, s_dxn, s_dxr = _rms_mod_bwd(dxn, r, inv, g, sc)
        dh_ref[...] = dhn + dh

        @pl.when(i % tps == 0)
        def _():
            mg_ref[...] = jnp.zeros(mg_ref.shape, F32)

        mg_ref[0, 0:1, :] += s_dxn
        mg_ref[0, 1:2, :] += s_dxr
        mg_ref[0, 2:3, :] += dga

    tok = pl.BlockSpec((tm, D), lambda i: (i, 0))
    tokf = pl.BlockSpec((tm, FF), lambda i: (i, 0))
    return _call(
        body, name="ffn_bwd", grid=(n_t,),
        in_specs=[tok, tok, tok, tokf, tokf, pl.BlockSpec((1, 9, D), lambda i: (i // tps, 0, 0)),
                  pl.BlockSpec((1, D), lambda i: (0, 0)), _vmem(), _vmem(), _vmem()],
        out_specs=[tok, tok, tok, tokf, tokf, tokf, pl.BlockSpec((1, 8, D), lambda i: (i // tps, 0, 0))],
        out_shape=[jax.ShapeDtypeStruct((T, D), F32), jax.ShapeDtypeStruct((T, D), BF16),
                   jax.ShapeDtypeStruct((T, D), BF16), jax.ShapeDtypeStruct((T, FF), BF16),
                   jax.ShapeDtypeStruct((T, FF), BF16), jax.ShapeDtypeStruct((T, FF), BF16),
                   jax.ShapeDtypeStruct((B, 8, D), F32)],
        args=(h, dhn, y, gb, ub, mods, gn, wg, wu, wd), semantics=("arbitrary",), plans=plans)


def _wgrad(a, b, name, plans=()):
    T, da = a.shape
    db = b.shape[1]
    bm = {2816: 1408, 3840: 1280}[da]
    bn = db
    tk = min(1024, T)
    nk = T // tk

    def body(a_ref, b_ref, o_ref):
        @pl.when(pl.program_id(2) == 0)
        def _():
            o_ref[...] = jnp.zeros(o_ref.shape, F32)

        o_ref[...] += _dot_tn(a_ref[...], b_ref[...])

    (out,), plan_outs = _call(
        body, name=name, grid=(da // bm, db // bn, nk),
        in_specs=[pl.BlockSpec((tk, bm), lambda i, j, k: (k, i)), pl.BlockSpec((tk, bn), lambda i, j, k: (k, j))],
        out_specs=[pl.BlockSpec((bm, bn), lambda i, j, k: (i, j))], out_shape=[jax.ShapeDtypeStruct((da, db), F32)],
        args=(a, b), semantics=("parallel", "parallel", "arbitrary"), plans=plans)
    return out, plan_outs


def _inproj_fwd(h, mods, gn, w_in, seq, plans=()):
    T = h.shape[0]
    tm = _token_tile(seq)
    tps = seq // tm

    def body(h_ref, m_ref, g_ref, w_ref, *outs):
        xn, _, _ = _rms_mod(h_ref[...], g_ref[...], m_ref[0, 3:4, :], m_ref[0, 4:5, :])
        xb = xn.astype(BF16)
        for (_, c0, w), o_ref in zip(PIECES, outs):
            o_ref[...] = _dot_nt(xb, w_ref[c0:c0 + w, :])

    return _call(
        body, name="inproj_fwd", grid=(T // tm,),
        in_specs=[pl.BlockSpec((tm, D), lambda i: (i, 0)), pl.BlockSpec((1, 9, D), lambda i: (i // tps, 0, 0)),
                  pl.BlockSpec((1, D), lambda i: (0, 0)), _vmem()],
        out_specs=[pl.BlockSpec((tm, w), lambda i: (i, 0)) for _, _, w in PIECES],
        out_shape=[jax.ShapeDtypeStruct((T, w), F32) for _, _, w in PIECES], args=(h, mods, gn, w_in),
        semantics=("parallel",), plans=plans)


def _inproj_bwd(h, dh_res, dpieces, mods, gn, w_in, seq, plans=()):
    T = h.shape[0]
    B = T // seq
    tm = _token_tile(seq, 256)
    tps = seq // tm

    def body(h_ref, dres_ref, *rest):
        dp_refs = rest[:len(PIECES)]
        m_ref, g_ref, w_ref, dh_ref, xb_ref, dpb_ref, mg_ref = rest[len(PIECES):]
        i = pl.program_id(0)
        g = g_ref[...]
        sc = m_ref[0, 4:5, :]
        xn, r, inv = _rms_mod(h_ref[...], g, m_ref[0, 3:4, :], sc)
        xb_ref[...] = xn.astype(BF16)
        dxn = jnp.zeros((tm, D), F32)
        for (_, c0, w), dp_ref in zip(PIECES, dp_refs):
            dpb = dp_ref[...].astype(BF16)
            dpb_ref[:, c0:c0 + w] = dpb
            dxn = dxn + _dot(dpb, w_ref[c0:c0 + w, :])
        dh, s_dxn, s_dxr = _rms_mod_bwd(dxn, r, inv, g, sc)
        dh_ref[...] = dres_ref[...] + dh

        @pl.when(i % tps == 0)
        def _():
            mg_ref[...] = jnp.zeros(mg_ref.shape, F32)

        mg_ref[0, 0:1, :] += s_dxn
        mg_ref[0, 1:2, :] += s_dxr

    tok = pl.BlockSpec((tm, D), lambda i: (i, 0))
    return _call(
        body, name="inproj_bwd", grid=(T // tm,),
        in_specs=[tok, tok] + [pl.BlockSpec((tm, w), lambda i: (i, 0)) for _, _, w in PIECES]
        + [pl.BlockSpec((1, 9, D), lambda i: (i // tps, 0, 0)), pl.BlockSpec((1, D), lambda i: (0, 0)), _vmem()],
        out_specs=[tok, tok, pl.BlockSpec((tm, IN_COLS), lambda i: (i, 0)),
                   pl.BlockSpec((1, 8, D), lambda i: (i // tps, 0, 0))],
        out_shape=[jax.ShapeDtypeStruct((T, D), F32), jax.ShapeDtypeStruct((T, D), BF16),
                   jax.ShapeDtypeStruct((T, IN_COLS), BF16), jax.ShapeDtypeStruct((B, 8, D), F32)],
        args=(h, dh_res, *dpieces, mods, gn, w_in), semantics=("arbitrary",), plans=plans)


def _seg_mean(x):
    i = lax.broadcasted_iota(jnp.int32, (128, 128), 0) >> 6
    j = lax.broadcasted_iota(jnp.int32, (128, 128), 1) >> 6
    ones = jnp.where(i == j, 1.0 / HD, 0.0).astype(BF16)
    hi = x.astype(BF16)
    lo = (x - hi.astype(F32)).astype(BF16)
    return _dot(hi, ones) + _dot(lo, ones)


def _head_norm(x, g2):
    inv = lax.rsqrt(_seg_mean(x * x) + EPS)
    y = x * inv
    return y * g2, y, inv


def _head_norm_bwd(dxn, y, inv, g2):
    dy = dxn * g2
    return inv * (dy - y * _seg_mean(dy * y)), jnp.sum(dxn * y, axis=0, keepdims=True)


def _swa_block(q, kk, vv, gq2, gk2, sinks, first, do=None):
    lo = lax.broadcasted_iota(jnp.int32, (1, 128), 1) < HD
    kn, ky, kinv = _head_norm(kk, gk2)

    def operands(x):
        xr = pltpu.roll(x, HD, 1)
        own_lo, own_hi = jnp.where(lo, x, 0.0).astype(BF16), jnp.where(lo, 0.0, x).astype(BF16)
        rolled_lo, rolled_hi = jnp.where(lo, xr, 0.0).astype(BF16), jnp.where(lo, 0.0, xr).astype(BF16)
        return (own_lo, rolled_hi), (rolled_lo, own_hi)

    def restore(parts):
        (own_lo, rolled_hi), (rolled_lo, own_hi) = parts
        return (jnp.where(lo, own_lo, own_hi)
                + pltpu.roll(jnp.where(lo, rolled_lo, rolled_hi), HD, 1))

    k_ops, v_ops = operands(kn), operands(vv)
    k2 = [jnp.concatenate(pair, axis=0) for pair in k_ops]
    def stack(x):
        return jnp.concatenate([x[:, 128 * p:128 * (p + 1)] for p in range(4)], axis=0)

    def unstack(x):
        return jnp.concatenate([x[BLK * p:BLK * (p + 1)] for p in range(4)], axis=1)

    def of_head(x, kh):
        return x[2 * BLK * kh:2 * BLK * (kh + 1)]

    nq = 4 * BLK
    pair = lax.broadcasted_iota(jnp.int32, (nq, 1), 0) >> 7
    row = lax.broadcasted_iota(jnp.int32, (nq, 2 * BLK), 0) & (BLK - 1)
    col = lax.broadcasted_iota(jnp.int32, (nq, 2 * BLK), 1)
    valid = (col <= row + BLK) & (col > row) & (col >= jnp.where(first, BLK, 0))
    scale = HD ** -0.5
    qn, qy, qinv = _head_norm(stack(q), gq2)
    qnb = qn.astype(BF16)
    s2 = jnp.concatenate([_dot_nt(of_head(qnb, kh), k2[kh]) for kh in range(N_KV)], axis=0) * scale
    probs, p_sink = [], []
    for j in range(2):
        s = jnp.where(valid, s2[:, 2 * BLK * j:2 * BLK * (j + 1)], NEG)
        sink = jnp.zeros((nq, 1), F32)
        for p in range(4):
            sink = jnp.where(pair == p, sinks[:, 2 * p + j:2 * p + j + 1], sink)
        m = jnp.maximum(jnp.max(s, axis=-1, keepdims=True), sink)
        e = jnp.exp(s - m)
        e_sink = jnp.exp(sink - m)
        rden = 1.0 / (jnp.sum(e, axis=-1, keepdims=True) + e_sink)
        probs.append(e * rden)
        p_sink.append(e_sink * rden)
    pb = [p.astype(BF16) for p in probs]
    if do is None:
        return unstack(jnp.concatenate(
            [_dot(of_head(pb[0], kh), v_ops[kh][0]) + _dot(of_head(pb[1], kh), v_ops[kh][1]) for kh in range(N_KV)],
            axis=0))
    dob = stack(do).astype(BF16)
    ds, dsinks = [], [None] * 8
    for j in range(2):
        dp = jnp.concatenate([_dot_nt(of_head(dob, kh), v_ops[kh][j]) for kh in range(N_KV)], axis=0)
        t = jnp.sum(dp * probs[j], axis=-1, keepdims=True)
        ds.append(probs[j] * (dp - t) * scale)
        lost = p_sink[j] * t
        for p in range(4):
            dsinks[2 * p + j] = -jnp.sum(lost[BLK * p:BLK * (p + 1)])
    dsb = jnp.concatenate(ds, axis=1).astype(BF16)
    dqn = jnp.concatenate([_dot(of_head(dsb, kh), k2[kh]) for kh in range(N_KV)], axis=0)
    dq, dgq2 = _head_norm_bwd(dqn, qy, qinv, gq2)
    dk2 = [_dot_tn(of_head(dsb, kh), of_head(qnb, kh)) for kh in range(N_KV)]
    dv_ops = [[_dot_tn(of_head(pb[j], kh), of_head(dob, kh)) for j in range(2)] for kh in range(N_KV)]
    dkn = restore(tuple((d[:2 * BLK], d[2 * BLK:]) for d in dk2))
    dvv = restore(tuple(tuple(d) for d in dv_ops))
    dkk, dgk2 = _head_norm_bwd(dkn, ky, kinv, gk2)
    return unstack(dq), dkk, dvv, dgq2, dgk2, dsinks


SWA_GROUP = 8


def _swa_specs(nbs):
    grp = min(SWA_GROUP, nbs)
    rows = grp * BLK

    def tok(w):
        return pl.BlockSpec((rows, w), lambda i: (i, 0))

    halo = pl.BlockSpec((BLK, 128), lambda i: (jnp.maximum(i * grp - 1, 0), 0))
    vec = pl.BlockSpec((1, 128), lambda i: (0, 0))
    sk = pl.BlockSpec((1, 8), lambda i: (0, 0))
    return grp, tok, halo, vec, sk


def _swa_fwd(q, k, v, gq2, gk2, sinks, nbs, plans=()):
    T = q.shape[0]
    grp, tok, halo, vec, sk = _swa_specs(nbs)

    def body(q_ref, kh_ref, kc_ref, vh_ref, vc_ref, gq_ref, gk_ref, sk_ref, o_ref):
        seq_start = ((pl.program_id(0) * grp) % nbs) == 0
        for g in range(grp):
            rows = slice(g * BLK, (g + 1) * BLK)
            prev = slice((g - 1) * BLK, g * BLK)
            kk = jnp.concatenate([kh_ref[...] if g == 0 else kc_ref[prev, :], kc_ref[rows, :]], axis=0)
            vv = jnp.concatenate([vh_ref[...] if g == 0 else vc_ref[prev, :], vc_ref[rows, :]], axis=0)
            o_ref[rows, :] = _swa_block(q_ref[rows, :], kk, vv, gq_ref[...], gk_ref[...], sk_ref[...],
                                        seq_start if g == 0 else False)

    (out,), plan_outs = _call(
        body, name="swa_fwd", grid=(T // (grp * BLK),),
        in_specs=[tok(D_B), halo, tok(128), halo, tok(128), vec, vec, sk], out_specs=[tok(D_B)],
        out_shape=[jax.ShapeDtypeStruct((T, D_B), F32)], args=(q, k, k, v, v, gq2, gk2, sinks),
        semantics=("parallel",), plans=plans)
    return out, plan_outs


def _swa_bwd(q, k, v, gq2, gk2, sinks, do, nbs, plans=()):
    T = q.shape[0]
    grp, tok, halo, vec, sk = _swa_specs(nbs)
    steps = T // (grp * BLK)

    def body(q_ref, kh_ref, kc_ref, vh_ref, vc_ref, gq_ref, gk_ref, sk_ref, do_ref,
             dq_ref, dk_ref, dv_ref, dkh_ref, dvh_ref, dgq_ref, dgk_ref, dsk_ref):
        i = pl.program_id(0)
        seq_start = ((i * grp) % nbs) == 0

        @pl.when(i == 0)
        def _():
            for r in (dgq_ref, dgk_ref, dsk_ref):
                r[...] = jnp.zeros(r.shape, F32)

        res = []
        for g in range(grp):
            rows = slice(g * BLK, (g + 1) * BLK)
            prev = slice((g - 1) * BLK, g * BLK)
            kk = jnp.concatenate([kh_ref[...] if g == 0 else kc_ref[prev, :], kc_ref[rows, :]], axis=0)
            vv = jnp.concatenate([vh_ref[...] if g == 0 else vc_ref[prev, :], vc_ref[rows, :]], axis=0)
            res.append(_swa_block(q_ref[rows, :], kk, vv, gq_ref[...], gk_ref[...], sk_ref[...],
                                  seq_start if g == 0 else False, do=do_ref[rows, :]))
        lane = lax.broadcasted_iota(jnp.int32, (8, 128), 1)
        upd = jnp.zeros((8, 128), F32)
        for g, (dq, dkk, dvv, dgq2, dgk2, dsinks) in enumerate(res):
            rows = slice(g * BLK, (g + 1) * BLK)
            dq_ref[rows, :] = dq
            dk_ref[rows, :] = dkk[BLK:] + res[g + 1][1][:BLK] if g + 1 < grp else dkk[BLK:]
            dv_ref[rows, :] = dvv[BLK:] + res[g + 1][2][:BLK] if g + 1 < grp else dvv[BLK:]
            dgq_ref[...] += dgq2
            dgk_ref[...] += dgk2
            for h, d in enumerate(dsinks):
                upd = upd + jnp.where(lane == h, d, 0.0)
        dkh_ref[0] = res[0][1][:BLK]
        dvh_ref[0] = res[0][2][:BLK]
        dsk_ref[...] += upd

    one = pl.BlockSpec((1, BLK, 128), lambda i: (i, 0, 0))
    halo_shape = jax.ShapeDtypeStruct((steps, BLK, 128), F32)
    return _call(
        body, name="swa_bwd", grid=(steps,),
        in_specs=[tok(D_B), halo, tok(128), halo, tok(128), vec, vec, sk, tok(D_B)],
        out_specs=[tok(D_B), tok(128), tok(128), one, one, vec, vec, pl.BlockSpec((8, 128), lambda i: (0, 0))],
        out_shape=[jax.ShapeDtypeStruct((T, D_B), F32), jax.ShapeDtypeStruct((T, 128), F32),
                   jax.ShapeDtypeStruct((T, 128), F32), halo_shape, halo_shape, jax.ShapeDtypeStruct((1, 128), F32),
                   jax.ShapeDtypeStruct((1, 128), F32), jax.ShapeDtypeStruct((8, 128), F32)],
        args=(q, k, k, v, v, gq2, gk2, sinks, do), semantics=("arbitrary",), plans=plans)


def _swa_add_halo(dk, dk_halo):
    steps = dk_halo.shape[0]
    nxt = jnp.concatenate([dk_halo[1:], jnp.zeros_like(dk_halo[:1])], axis=0)[:, None]
    dk = dk.reshape(steps, -1, BLK, 128)
    return jnp.concatenate([dk[:, :-1], dk[:, -1:] + nxt], axis=1).reshape(-1, 128)


def _sgu_norm(av, g_ln, b_ln):
    t, th = _gelu(av)
    mu = jnp.mean(t, axis=-1, keepdims=True)
    tc = t - mu
    rstd = lax.rsqrt(jnp.mean(tc * tc, axis=-1, keepdims=True) + EPS)
    vhat = tc * rstd
    return vhat * g_ln + b_ln, vhat, rstd, th


def _masked_ws(ws_ref):
    tril = lax.broadcasted_iota(jnp.int32, (BLK, BLK), 0) >= lax.broadcasted_iota(jnp.int32, (BLK, BLK), 1)
    return [jnp.where(tril, ws_ref[g], 0.0).astype(BF16) for g in range(N_GRP)]


def _mix_fwd(au, av, gta, gtb, ob, h, mods, g_ln, b_ln, ws, bsb, wa, wb, wout, seq):
    T = h.shape[0]
    tm = _token_tile(seq)
    tps = seq // tm

    def body(au_ref, av_ref, gta_ref, gtb_ref, ob_ref, h_ref, m_ref, gl_ref, bl_ref, ws_ref, bs_ref,
             wa_ref, wb_ref, wo_ref, out_ref, vvb_s, z_s):
        u, _ = _gelu(au_ref[...])
        vv, _, _, _ = _sgu_norm(av_ref[...], gl_ref[...], bl_ref[...])
        vvb_s[...] = vv.astype(BF16)
        wsm = _masked_ws(ws_ref)
        for c in range(tm // BLK):
            rows = slice(c * BLK, (c + 1) * BLK)
            for g in range(N_GRP):
                cols = slice(g * BLK, (g + 1) * BLK)
                z_s[rows, cols] = _dot(wsm[g], vvb_s[rows, cols]) + bs_ref[g]
        ya = _dot_nt((u * z_s[...]).astype(BF16), wa_ref[...])
        yb = _dot_nt(ob_ref[...].astype(BF16), wb_ref[...])
        merged = jax.nn.sigmoid(gta_ref[...]) * ya + jax.nn.sigmoid(gtb_ref[...]) * yb
        out_ref[...] = h_ref[...] + m_ref[0, 5:6, :] * _dot(merged.astype(BF16), wo_ref[...])

    def tok(w):
        return pl.BlockSpec((tm, w), lambda i: (i, 0))

    def full(shape):
        return pl.BlockSpec(shape, lambda i: (0,) * len(shape))

    return pl.pallas_call(
        body, name="mix_fwd", grid=(T // tm,),
        in_specs=[tok(D_A), tok(D_A), tok(D), tok(D), tok(D_B), tok(D),
                  pl.BlockSpec((1, 9, D), lambda i: (i // tps, 0, 0)), full((1, D_A)), full((1, D_A)),
                  full((N_GRP, BLK, BLK)), full((N_GRP, BLK, BLK)), _vmem(), _vmem(), _vmem()],
        out_specs=tok(D), out_shape=jax.ShapeDtypeStruct((T, D), F32),
        scratch_shapes=[pltpu.VMEM((tm, D_A), BF16), pltpu.VMEM((tm, D_A), F32)],
        compiler_params=_params("parallel"),
    )(au, av, gta, gtb, ob, h, mods, g_ln, b_ln, ws, bsb, wa, wb, wout)


def _mix_bwd(au, av, gta, gtb, ob, dh, mods, g_ln, b_ln, ws, bsb, wa, wb, wout, seq, plans=()):
    T = dh.shape[0]
    B = T // seq
    tm = _token_tile(seq, 256)
    tps = seq // tm

    def body(au_ref, av_ref, gta_ref, gtb_ref, ob_ref, dh_ref, m_ref, gl_ref, bl_ref, ws_ref, bs_ref,
             wa_ref, wb_ref, wo_ref,
             dau_ref, dav_ref, dgta_ref, dgtb_ref, dob_ref, dwo_ref, dwa_ref, dwb_ref, dws_ref, dbs_ref, dln_ref,
             mg_ref, vvb_s, z_s, dz_s, dzb_s, dvv_s):
        i = pl.program_id(0)

        @pl.when(i == 0)
        def _():
            for r in (dwo_ref, dwa_ref, dwb_ref, dws_ref, dbs_ref, dln_ref):
                r[...] = jnp.zeros(r.shape, F32)

        @pl.when(i % tps == 0)
        def _():
            mg_ref[...] = jnp.zeros(mg_ref.shape, F32)

        auv = au_ref[...]
        avv = av_ref[...]
        u, thu = _gelu(auv)
        g_ln = gl_ref[...]
        vv, vhat, rstd, thv = _sgu_norm(avv, g_ln, bl_ref[...])
        vvb_s[...] = vv.astype(BF16)
        wsm = _masked_ws(ws_ref)
        for c in range(tm // BLK):
            rows = slice(c * BLK, (c + 1) * BLK)
            for g in range(N_GRP):
                cols = slice(g * BLK, (g + 1) * BLK)
                z_s[rows, cols] = _dot(wsm[g], vvb_s[rows, cols]) + bs_ref[g]
        z = z_s[...]
        yab = (u * z).astype(BF16)
        obb = ob_ref[...].astype(BF16)
        ya = _dot_nt(yab, wa_ref[...])
        yb = _dot_nt(obb, wb_ref[...])
        sa = jax.nn.sigmoid(gta_ref[...])
        sb = jax.nn.sigmoid(gtb_ref[...])
        mb = (sa * ya + sb * yb).astype(BF16)
        dhv = dh_ref[...]
        mg_ref[0, 0:1, :] += jnp.sum(dhv * _dot(mb, wo_ref[...]), axis=0, keepdims=True)
        dmob = (m_ref[0, 5:6, :] * dhv).astype(BF16)
        dwo_ref[...] += _dot_tn(mb, dmob)
        dmerged = _dot_nt(dmob, wo_ref[...])
        dya = dmerged * sa
        dyb = dmerged * sb
        dgta_ref[...] = dya * ya * (1.0 - sa)
        dgtb_ref[...] = dyb * yb * (1.0 - sb)
        dyab = dya.astype(BF16)
        dybb = dyb.astype(BF16)
        dwa_ref[...] += _dot_tn(dyab, yab)
        dwb_ref[...] += _dot_tn(dybb, obb)
        dob_ref[...] = _dot(dybb, wb_ref[...])
        dyap = _dot(dyab, wa_ref[...])
        dau_ref[...] = (dyap * z) * _gelu_grad(auv, thu)
        dz = dyap * u
        dz_s[...] = dz
        dzb_s[...] = dz.astype(BF16)
        for c in range(tm // BLK):
            rows = slice(c * BLK, (c + 1) * BLK)
            for g in range(N_GRP):
                cols = slice(g * BLK, (g + 1) * BLK)
                dzb = dzb_s[rows, cols]
                dvv_s[rows, cols] = _dot_tn(wsm[g], dzb)
                dws_ref[g] += _dot_nt(dzb, vvb_s[rows, cols])
                dbs_ref[g] += dz_s[rows, cols]
        dvv = dvv_s[...]
        dln_ref[0:1, :] += jnp.sum(dvv * vhat, axis=0, keepdims=True)
        dln_ref[1:2, :] += jnp.sum(dvv, axis=0, keepdims=True)
        dvh = dvv * g_ln
        dt = rstd * (dvh - jnp.mean(dvh, axis=-1, keepdims=True)
                     - vhat * jnp.mean(dvh * vhat, axis=-1, keepdims=True))
        dav_ref[...] = dt * _gelu_grad(avv, thv)

    def tok(w):
        return pl.BlockSpec((tm, w), lambda i: (i, 0))

    def full(shape):
        return pl.BlockSpec(shape, lambda i: (0,) * len(shape))

    return _call(
        body, name="mix_bwd", grid=(T // tm,),
        in_specs=[tok(D_A), tok(D_A), tok(D), tok(D), tok(D_B), tok(D),
                  pl.BlockSpec((1, 9, D), lambda i: (i // tps, 0, 0)), full((1, D_A)), full((1, D_A)),
                  full((N_GRP, BLK, BLK)), full((N_GRP, BLK, BLK)), _vmem(), _vmem(), _vmem()],
        out_specs=[tok(D_A), tok(D_A), tok(D), tok(D), tok(D_B), full((D, D)), full((D, D_A)), full((D, D_B)),
                   full((N_GRP, BLK, BLK)), full((N_GRP, BLK, BLK)), full((8, D_A)),
                   pl.BlockSpec((1, 8, D), lambda i: (i // tps, 0, 0))],
        out_shape=[jax.ShapeDtypeStruct((T, D_A), F32), jax.ShapeDtypeStruct((T, D_A), F32),
                   jax.ShapeDtypeStruct((T, D), F32), jax.ShapeDtypeStruct((T, D), F32),
                   jax.ShapeDtypeStruct((T, D_B), F32), jax.ShapeDtypeStruct((D, D), F32),
                   jax.ShapeDtypeStruct((D, D_A), F32), jax.ShapeDtypeStruct((D, D_B), F32),
                   jax.ShapeDtypeStruct((N_GRP, BLK, BLK), F32), jax.ShapeDtypeStruct((N_GRP, BLK, BLK), F32),
                   jax.ShapeDtypeStruct((8, D_A), F32), jax.ShapeDtypeStruct((B, 8, D), F32)],
        scratch_shapes=[pltpu.VMEM((tm, D_A), BF16), pltpu.VMEM((tm, D_A), F32), pltpu.VMEM((tm, D_A), F32),
                        pltpu.VMEM((tm, D_A), BF16), pltpu.VMEM((tm, D_A), F32)],
        args=(au, av, gta, gtb, ob, dh, mods, g_ln, b_ln, ws, bsb, wa, wb, wout), semantics=("arbitrary",),
        plans=plans)


def _prologue(c_pad, w_ada, first_shards):
    cols = w_ada.shape[1]
    plan_w, plan_c = _Gather(first_shards), _Gather([c_pad])
    plan_m = _Gather([jax.ShapeDtypeStruct((N_DEV * 8, cols), F32)])
    n_w = len(first_shards)

    def body(c_ref, wada_ref, *rest):
        w_ins, rest = rest[:n_w], rest[n_w:]
        call_ref, mods_ref = rest[:2]
        w_outs, rest = rest[2:2 + n_w], rest[2 + n_w:]
        cvm, part, local_sem = rest[:3]
        sems = rest[3:]
        sems_w, sems_c, sems_m = sems[:3], sems[3:6], sems[6:9]
        plan_c.start([c_ref], [call_ref], sems_c)
        plan_c.relay([c_ref], [call_ref], sems_c)
        plan_c.finish([c_ref], [call_ref], sems_c)
        load = pltpu.make_async_copy(call_ref, cvm, local_sem)
        load.start()
        load.wait()
        cv = cvm[...].reshape(N_DEV * 8, D)
        part[...] = _dot((cv * jax.nn.sigmoid(cv)).astype(BF16), wada_ref[...].astype(BF16))
        plan_m.start([part], [mods_ref], sems_m)
        plan_w.start(w_ins, w_outs, sems_w)
        plan_m.relay([part], [mods_ref], sems_m)
        plan_m.finish([part], [mods_ref], sems_m)
        plan_w.relay(w_ins, w_outs, sems_w)
        plan_w.finish(w_ins, w_outs, sems_w)

    res = pl.pallas_call(
        body, name="prologue", in_specs=[_any(), _vmem()] + [_any()] * n_w,
        out_specs=[_any()] * (2 + n_w), out_shape=plan_c.out_shape + plan_m.out_shape + plan_w.out_shape,
        scratch_shapes=[pltpu.VMEM((N_DEV, 8, D), F32), pltpu.VMEM((N_DEV * 8, cols), F32), pltpu.SemaphoreType.DMA]
        + plan_w.scratch + plan_c.scratch + plan_m.scratch,
        compiler_params=pltpu.CompilerParams(vmem_limit_bytes=VMEM_LIMIT),
    )(c_pad, w_ada, *first_shards)
    return res[0], res[1], list(res[2:])


def _ada_update(c_all, dm_cols, w, m, v, plans=()):
    n, cols = c_all.shape[0], w.shape[1]

    def body(c_ref, dm_ref, w_ref, m_ref, v_ref, g_ref, d_ref, nm_ref, nv_ref):
        cv = c_ref[...]
        g = _dot_tn((cv * jax.nn.sigmoid(cv)).astype(BF16), dm_ref[...].astype(BF16))
        g_ref[...] = g
        d_ref[...], nm_ref[...], nv_ref[...] = _adamw(w_ref[...], g, m_ref[...], v_ref[...])

    col = pl.BlockSpec((D, 128), lambda j: (0, j))
    return _call(
        body, name="ada_update", grid=(cols // 128,),
        in_specs=[pl.BlockSpec((n, D), lambda j: (0, 0)), pl.BlockSpec((n, 128), lambda j: (0, j)), col, col, col],
        out_specs=[col] * 4, out_shape=[jax.ShapeDtypeStruct(w.shape, F32)] * 4, args=(c_all, dm_cols, w, m, v),
        semantics=("parallel",), plans=plans)


def _mod_finish(mg1, mg2, mg2g, mg3, mods, g1, g2, g3):
    B = mods.shape[0]

    def body(mg1_ref, mg2_ref, mg2g_ref, mg3_ref, m_ref, g1_ref, g2_ref, g3_ref, dm_ref, dgn_ref):
        dgn_ref[...] = jnp.zeros(dgn_ref.shape, F32)
        for k, (mg, g_ref) in enumerate(((mg1_ref, g1_ref), (mg2_ref, g2_ref), (mg3_ref, g3_ref))):
            for b in range(B):
                s_dxr = mg[b, 1:2, :]
                dm_ref[b, 3 * k:3 * k + 1, :] = mg[b, 0:1, :]
                dm_ref[b, 3 * k + 1:3 * k + 2, :] = g_ref[...] * s_dxr
                dm_ref[b, 3 * k + 2:3 * k + 3, :] = mg2g_ref[b, 0:1, :] if k == 1 else mg[b, 2:3, :]
                dgn_ref[k:k + 1, :] += (1.0 + m_ref[b, 3 * k + 1:3 * k + 2, :]) * s_dxr

    args = (mg1, mg2, mg2g, mg3, mods, g1, g2, g3)
    out_shape = [jax.ShapeDtypeStruct((B, 9, D), F32), jax.ShapeDtypeStruct((8, D), F32)]
    return pl.pallas_call(body, name="mod_finish", grid=(1,), in_specs=[_whole(a) for a in args],
                          out_specs=[_whole(o) for o in out_shape], out_shape=out_shape,
                          compiler_params=_params("arbitrary"))(*args)


def _small_update(gathered, params, ms, vs):
    n = len(SMALL)
    B = gathered[0].shape[1]

    def body(*refs):
        gdm, ggn, gln, gws, gbs, ggq, ggk, gsk, gloss = refs[:9]
        w = dict(zip(SMALL, refs[9:9 + n]))
        m = dict(zip(SMALL, refs[9 + n:9 + 2 * n]))
        v = dict(zip(SMALL, refs[9 + 2 * n:9 + 3 * n]))
        outs = refs[9 + 3 * n:-1]
        out = {name: outs[4 * k:4 * k + 4] for k, name in enumerate(SMALL)}

        def total(ref, idx):
            acc = ref[(0,) + idx]
            for dev in range(1, N_DEV):
                acc = acc + ref[(dev,) + idx]
            return acc

        def finish(name, g, idx=(Ellipsis,)):
            d, nm, nv = _adamw(w[name][idx], g, m[name][idx], v[name][idx])
            for ref, val in zip(out[name], (g, d, nm, nv)):
                ref[idx] = val

        g_bada = total(gdm, (slice(0, 1),))
        for b in range(1, B):
            g_bada = g_bada + total(gdm, (slice(b, b + 1),))
        finish("b_ada", g_bada)
        finish("g_norm1", total(ggn, (slice(0, 1),)))
        finish("g_norm2", total(ggn, (slice(1, 2),)))
        finish("g_norm3", total(ggn, (slice(2, 3),)))
        finish("g_sgu_ln", total(gln, (slice(0, 1),)))
        finish("b_sgu_ln", total(gln, (slice(1, 2),)))
        tril = lax.broadcasted_iota(jnp.int32, (BLK, BLK), 0) >= lax.broadcasted_iota(jnp.int32, (BLK, BLK), 1)
        for g in range(N_GRP):
            finish("w_spatial", jnp.where(tril, total(gws, (g,)), 0.0), (0, g))
            finish("b_spatial", jnp.sum(total(gbs, (g,)).T, axis=0, keepdims=True), (0, slice(g, g + 1)))
        finish("g_q", total(ggq, ()))
        finish("g_k", total(ggk, ()))
        finish("attn_sinks", total(gsk, (slice(0, 1), slice(0, N_KV * Q_PER_KV))))
        refs[-1][...] = total(gloss, ())

    args = list(gathered) + [params[k] for k in SMALL] + [ms[k] for k in SMALL] + [vs[k] for k in SMALL]
    out_shape = []
    for k in SMALL:
        out_shape += [jax.ShapeDtypeStruct(params[k].shape, F32)] * 4
    out_shape.append(jax.ShapeDtypeStruct((8, 128), F32))
    res = pl.pallas_call(body, name="small_update", grid=(1,), in_specs=[_whole(a) for a in args],
                         out_specs=[_whole(o) for o in out_shape], out_shape=out_shape,
                         compiler_params=_params("arbitrary"))(*args)
    return {k: res[4 * i:4 * i + 4] for i, k in enumerate(SMALL)}, res[-1][0, 0]


def _rs_add(p, r, core, name):
    _, rows, width = p.shape

    def body(c_ref, p_ref, r_ref, o_ref):
        o_ref[...] = (p_ref[...] + r_ref[...]).astype(BF16)

    return pl.pallas_call(
        body, name="rs_add_" + name, out_shape=jax.ShapeDtypeStruct((4, rows, width), BF16),
        grid_spec=pltpu.PrefetchScalarGridSpec(
            num_scalar_prefetch=1, grid=(4,),
            in_specs=[pl.BlockSpec((1, rows, width), lambda k, c_ref: (2 * k + c_ref[0], 0, 0)),
                      pl.BlockSpec((1, rows, width), lambda k, c_ref: (k, 0, 0))],
            out_specs=pl.BlockSpec((1, rows, width), lambda k, c_ref: (k, 0, 0))),
        compiler_params=_params("parallel"),
    )(core, p, r)


def _adam_big(w, m, v, r, name):
    rows, cols = w.shape

    def body(r_ref, w_ref, m_ref, v_ref, go_ref, d_ref, nm_ref, nv_ref):
        g = r_ref[0].astype(F32)
        for k in range(1, 4):
            g = g + r_ref[k].astype(F32)
        go_ref[...] = g
        d_ref[...], nm_ref[...], nv_ref[...] = _adamw(w_ref[...], g, m_ref[...], v_ref[...])

    br = rows // 2
    shard = pl.BlockSpec((br, cols), lambda i: (i, 0))
    return pl.pallas_call(body, name="adam_" + name, grid=(2,),
                          in_specs=[pl.BlockSpec((4, br, cols), lambda i: (0, i, 0)), shard, shard, shard],
                          out_specs=[shard] * 4, out_shape=[jax.ShapeDtypeStruct(w.shape, F32)] * 4,
                          compiler_params=_params("parallel"))(r, w, m, v)


def kernel(x, c, w_ada, b_ada, g_norm1, ffn1_w_gate, ffn1_w_up, ffn1_w_down, g_norm2, w_in, g_sgu_ln, b_sgu_ln, w_spatial, b_spatial, g_q, g_k, attn_sinks, w_branch_a, w_branch_b, w_out, g_norm3, ffn2_w_gate, ffn2_w_up, ffn2_w_down, loss_target, m_w_ada, m_b_ada, m_g_norm1, m_ffn1_w_gate, m_ffn1_w_up, m_ffn1_w_down, m_g_norm2, m_w_in, m_g_sgu_ln, m_b_sgu_ln, m_w_spatial, m_b_spatial, m_g_q, m_g_k, m_attn_sinks, m_w_branch_a, m_w_branch_b, m_w_out, m_g_norm3, m_ffn2_w_gate, m_ffn2_w_up, m_ffn2_w_down, v_w_ada, v_b_ada, v_g_norm1, v_ffn1_w_gate, v_ffn1_w_up, v_ffn1_w_down, v_g_norm2, v_w_in, v_g_sgu_ln, v_b_sgu_ln, v_w_spatial, v_b_spatial, v_g_q, v_g_k, v_attn_sinks, v_w_branch_a, v_w_branch_b, v_w_out, v_g_norm3, v_ffn2_w_gate, v_ffn2_w_up, v_ffn2_w_down):
    names = ("w_ada", "b_ada", "g_norm1", "ffn1_w_gate", "ffn1_w_up", "ffn1_w_down", "g_norm2", "w_in", "g_sgu_ln",
             "b_sgu_ln", "w_spatial", "b_spatial", "g_q", "g_k", "attn_sinks", "w_branch_a", "w_branch_b", "w_out",
             "g_norm3", "ffn2_w_gate", "ffn2_w_up", "ffn2_w_down")
    w = dict(zip(names, (w_ada, b_ada, g_norm1, ffn1_w_gate, ffn1_w_up, ffn1_w_down, g_norm2, w_in, g_sgu_ln,
                         b_sgu_ln, w_spatial, b_spatial, g_q, g_k, attn_sinks, w_branch_a, w_branch_b, w_out, g_norm3,
                         ffn2_w_gate, ffn2_w_up, ffn2_w_down)))
    m = dict(zip(names, (m_w_ada, m_b_ada, m_g_norm1, m_ffn1_w_gate, m_ffn1_w_up, m_ffn1_w_down, m_g_norm2, m_w_in,
                         m_g_sgu_ln, m_b_sgu_ln, m_w_spatial, m_b_spatial, m_g_q, m_g_k, m_attn_sinks, m_w_branch_a,
                         m_w_branch_b, m_w_out, m_g_norm3, m_ffn2_w_gate, m_ffn2_w_up, m_ffn2_w_down)))
    v = dict(zip(names, (v_w_ada, v_b_ada, v_g_norm1, v_ffn1_w_gate, v_ffn1_w_up, v_ffn1_w_down, v_g_norm2, v_w_in,
                         v_g_sgu_ln, v_b_sgu_ln, v_w_spatial, v_b_spatial, v_g_q, v_g_k, v_attn_sinks, v_w_branch_a,
                         v_w_branch_b, v_w_out, v_g_norm3, v_ffn2_w_gate, v_ffn2_w_up, v_ffn2_w_down)))
    B, seq, _ = x.shape
    T = B * seq
    nbs = seq // BLK
    xi, yi, ci = _place()
    me = 4 * xi + 2 * yi + ci
    core = jnp.reshape(ci, (1,)).astype(jnp.int32)
    layout = {name: (tform, n, width) for name, tform, n, width in BIG}
    shard = {name: w[name][0].astype(BF16).T if tform else w[name][0].astype(BF16) for name, tform, _, _ in BIG}
    wts, parts, big_out = {}, {}, {}

    def gather_plan(group):
        return _Gather([shard[k] for k in group])

    def take(group, gathered):
        for k, g in zip(group, gathered):
            wts[k] = g.reshape(N_DEV * layout[k][1], layout[k][2])

    def blocks(name, grad):
        return grad.reshape(N_DEV, layout[name][1], layout[name][2])

    def to_sibling(names, grads):
        return _RsSibling([blocks(k, g) for k, g in zip(names, grads)])

    def add(names, grads, from_sibling):
        for k, g, r in zip(names, grads, from_sibling):
            parts[k] = _rs_add(blocks(k, g), r, core, k)

    def to_chips(names):
        return _RsChips([parts[k] for k in names])

    def update(names, from_chips):
        for k, r in zip(names, from_chips):
            if layout[k][0]:
                big_out[k] = [o.T[None] for o in _adam_big(w[k][0].T, m[k][0].T, v[k][0].T, r, k)]
            else:
                big_out[k] = [o[None] for o in _adam_big(w[k][0], m[k][0], v[k][0], r, k)]

    g1, g2, g3, g_ln, b_ln = g_norm1, g_norm2, g_norm3, g_sgu_ln, b_sgu_ln
    gq2, gk2 = jnp.tile(g_q, (1, 2)), jnp.tile(g_k, (1, 2))
    ws = w_spatial[0]
    bsb = jnp.broadcast_to(b_spatial[0][:, :, None], (N_GRP, BLK, BLK))
    xf = x.reshape(T, D)
    tgt = loss_target.reshape(T, D)

    ffn1 = ("ffn1_w_gate", "ffn1_w_up", "ffn1_w_down")
    ffn2 = ("ffn2_w_gate", "ffn2_w_up", "ffn2_w_down")
    c_pad = jnp.concatenate([c, jnp.zeros((8 - B, D), F32)], axis=0)
    c_all, mods_cols, gathered = _prologue(c_pad, w_ada[0], [shard[k] for k in ffn1[:2]])
    take(ffn1[:2], gathered)
    c_all = c_all[:, :B].reshape(N_DEV * B, D)
    mine = lax.dynamic_slice_in_dim(mods_cols, 8 * me, B, axis=1)
    mods = (mine.transpose(1, 0, 2).reshape(B, 9 * D) + b_ada).reshape(B, 9, D)

    group = ("ffn1_w_down", "w_in")
    (gb1, ub1), (gathered,) = _ffn_gate_up(xf, mods, g1, wts["ffn1_w_gate"], wts["ffn1_w_up"], 0, seq,
                                           plans=[gather_plan(group)])
    take(group, gathered)
    group = ("w_branch_a", "w_branch_b", "w_out")
    (h1, y1), (gathered,) = _ffn_down(xf, gb1, ub1, mods, wts["ffn1_w_down"], 0, seq, plans=[gather_plan(group)])
    take(group, gathered)
    group = ("ffn2_w_gate",)
    (au, av, q_tok, k_tok, v_tok, gta, gtb), (gathered,) = _inproj_fwd(h1, mods, g2, wts["w_in"], seq,
                                                                       plans=[gather_plan(group)])
    take(group, gathered)
    group = ("ffn2_w_up", "ffn2_w_down")
    ob, (gathered,) = _swa_fwd(q_tok, k_tok, v_tok, gq2, gk2, attn_sinks, nbs, plans=[gather_plan(group)])
    take(group, gathered)
    mixw = (wts["w_branch_a"], wts["w_branch_b"], wts["w_out"])
    h2 = _mix_fwd(au, av, gta, gtb, ob, h1, mods, g_ln, b_ln, ws, bsb, *mixw, seq)
    (dh3, y3, gb3, ub3, loss_parts), _ = _ffn_fwd(h2, mods, g3, *[wts[k] for k in ffn2], 6, seq, tgt=tgt)
    loss_part = jnp.full((8, 128), jnp.sum(loss_parts[:, 0, 0]))

    (dh2, xb, dyb, a, dg, du, mg3), _ = _ffn_bwd(h2, dh3, y3, gb3, ub3, mods, g3, *[wts[k] for k in ffn2], 6, seq)
    d_gate, _ = _wgrad(dg, xb, "wgrad_ffn2_gate")
    d_up, (r,) = _wgrad(du, xb, "wgrad_ffn2_up", plans=[to_sibling(ffn2[:1], [d_gate])])
    add(ffn2[:1], [d_gate], r)
    d_down, (r,) = _wgrad(a, dyb, "wgrad_ffn2_down", plans=[to_sibling(ffn2[1:2], [d_up])])
    add(ffn2[1:2], [d_up], r)

    (dau, dav, dgta, dgtb, dob, d_out, d_a, d_b, dws, dbs, dln, mg2g), (r, from_chips) = _mix_bwd(
        au, av, gta, gtb, ob, dh2, mods, g_ln, b_ln, ws, bsb, *mixw, seq,
        plans=[to_sibling(ffn2[2:], [d_down]), to_chips(ffn2[:2])])
    add(ffn2[2:], [d_down], r)
    update(ffn2[:2], from_chips)
    mixers = ("w_out", "w_branch_a", "w_branch_b")
    (dq, dk, dv, dk_halo, dv_halo, dgq2, dgk2, dsk), (from_chips, r) = _swa_bwd(
        q_tok, k_tok, v_tok, gq2, gk2, attn_sinks, dob, nbs,
        plans=[to_chips(ffn2[2:]), to_sibling(mixers, [d_out, d_a, d_b])])
    update(ffn2[2:], from_chips)
    add(mixers, [d_out, d_a, d_b], r)
    dk, dv = _swa_add_halo(dk, dk_halo), _swa_add_halo(dv, dv_halo)
    dgq, dgk = dgq2[:, :HD] + dgq2[:, HD:], dgk2[:, :HD] + dgk2[:, HD:]
    (dh1, xb2, dpb, mg2), (from_chips, early) = _inproj_bwd(
        h1, dh2, (dau, dav, dq, dk, dv, dgta, dgtb), mods, g2, wts["w_in"], seq,
        plans=[to_chips(mixers), _Gather([dln, dws, dbs, dgq, dgk, dsk, loss_part])])
    update(mixers, from_chips)

    (dx, xb, dyb, a, dg, du, mg1), _ = _ffn_bwd(xf, dh1, y1, gb1, ub1, mods, g1, *[wts[k] for k in ffn1], 0, seq)
    dmods, dgn = _mod_finish(mg1, mg2, mg2g, mg3, mods, g1, g2, g3)
    d_gate, (late,) = _wgrad(dg, xb, "wgrad_ffn1_gate", plans=[_Gather([dmods.reshape(B, 9 * D), dgn])])
    gathered = late + early
    d_up, (r,) = _wgrad(du, xb, "wgrad_ffn1_up", plans=[to_sibling(ffn1[:1], [d_gate])])
    add(ffn1[:1], [d_gate], r)
    d_in, (r, from_chips) = _wgrad(dpb, xb2, "wgrad_w_in",
                                   plans=[to_sibling(ffn1[1:2], [d_up]), to_chips(ffn1[:1])])
    add(ffn1[1:2], [d_up], r)
    update(ffn1[:1], from_chips)
    d_down, (r, from_chips) = _wgrad(a, dyb, "wgrad_ffn1_down",
                                     plans=[to_sibling(("w_in",), [d_in]), to_chips(ffn1[1:2])])
    add(("w_in",), [d_in], r)
    update(ffn1[1:2], from_chips)
    small_out, loss = _small_update(gathered, w, m, v)
    dm_cols = lax.dynamic_slice_in_dim(gathered[0].reshape(N_DEV * B, 9 * D), (9 * D // N_DEV) * me,
                                       9 * D // N_DEV, axis=1)
    ada_out, (r, from_chips) = _ada_update(c_all, dm_cols, w_ada[0], m_w_ada[0], v_w_ada[0],
                                           plans=[to_sibling(ffn1[2:], [d_down]), to_chips(("w_in",))])
    add(ffn1[2:], [d_down], r)
    update(("w_in",), from_chips)
    (from_chips,) = _exchange([to_chips(ffn1[2:])], "rs_last")
    update(ffn1[2:], from_chips)

    def leaf(kind, name):
        if name == "w_ada":
            return ada_out[kind][None]
        if name in SMALL:
            return small_out[name][kind]
        return big_out[name][kind]

    return (loss, dx.reshape(B, seq, D), *[leaf(kind, name) for kind in range(4) for name in names])
```

```python
import functools
import math

import jax
import jax.numpy as jnp
from jax import lax
from jax.experimental import pallas as pl
from jax.experimental.pallas import tpu as pltpu

F32 = jnp.float32
BF16 = jnp.bfloat16
MESH = pl.DeviceIdType.MESH
AXES = ("x", "y", "c")
N_DEV = 8

VMEM_LIMIT = 56 * 1024 * 1024

D = 1024
FF = 2816
FC = 1408
D_A = 512
D_B = 512
HD = 64
N_KV = 2
Q_PER_KV = 4
BLK = 128
N_GRP = 4
IN_COLS = 3840
PIECES = (("au", 0, 512), ("av", 512, 512), ("q", 1024, 512), ("k", 1536, 128), ("v", 1664, 128),
          ("ga", 1792, 1024), ("gb", 2816, 1024))
EPS = 1e-6
NEG = -1e30
GELU_C = math.sqrt(2.0 / math.pi)

ADAM_LR = 0.001
ADAM_B1 = 0.9
ADAM_B2 = 0.999
ADAM_EPS = 1e-08
ADAM_WD = 0.01
ADAM_STEP = 10

NT = (((1,), (1,)), ((), ()))
TN = (((0,), (0,)), ((), ()))

BIG = (("ffn1_w_gate", True, FF // N_DEV, D), ("ffn1_w_up", True, FF // N_DEV, D),
       ("ffn1_w_down", False, FF // N_DEV, D), ("w_in", True, IN_COLS // N_DEV, D),
       ("w_branch_a", True, D // N_DEV, D_A), ("w_branch_b", True, D // N_DEV, D_B), ("w_out", False, D // N_DEV, D),
       ("ffn2_w_gate", True, FF // N_DEV, D), ("ffn2_w_up", True, FF // N_DEV, D),
       ("ffn2_w_down", False, FF // N_DEV, D))
SMALL = ("b_ada", "g_norm1", "g_norm2", "g_sgu_ln", "b_sgu_ln", "w_spatial", "b_spatial", "g_q", "g_k",
         "attn_sinks", "g_norm3")


def _dot(a, b):
    return jnp.dot(a, b, preferred_element_type=F32)


def _dot_nt(a, b):
    return lax.dot_general(a, b, NT, preferred_element_type=F32)


def _dot_tn(a, b):
    return lax.dot_general(a, b, TN, preferred_element_type=F32)


def _vmem():
    return pl.BlockSpec(memory_space=pltpu.VMEM)


def _any():
    return pl.BlockSpec(memory_space=pl.ANY)


def _whole(a):
    return pl.BlockSpec(a.shape, lambda i: (0,) * len(a.shape))


def _rms_mod(h, g, sh, sc):
    inv = lax.rsqrt(jnp.mean(h * h, axis=-1, keepdims=True) + EPS)
    r = h * inv
    return (r * g) * (1.0 + sc) + sh, r, inv


def _rms_mod_bwd(dxn, r, inv, g, sc):
    dr = dxn * (g * (1.0 + sc))
    dh = inv * (dr - r * jnp.mean(dr * r, axis=-1, keepdims=True))
    return dh, jnp.sum(dxn, axis=0, keepdims=True), jnp.sum(dxn * r, axis=0, keepdims=True)


def _gelu(x):
    t = jnp.tanh(GELU_C * (x + 0.044715 * (x * x * x)))
    return 0.5 * x * (1.0 + t), t


def _gelu_grad(x, t):
    return 0.5 * (1.0 + t) + 0.5 * x * (1.0 - t * t) * (GELU_C * (1.0 + 3.0 * 0.044715 * x * x))


def _adamw(w, g, m, v):
    m = ADAM_B1 * m + (1.0 - ADAM_B1) * g
    v = ADAM_B2 * v + (1.0 - ADAM_B2) * (g * g)
    m_hat = m / (1.0 - ADAM_B1 ** ADAM_STEP)
    v_hat = v / (1.0 - ADAM_B2 ** ADAM_STEP)
    delta = -ADAM_LR * (m_hat / (jnp.sqrt(v_hat) + ADAM_EPS) + ADAM_WD * w)
    return delta, m, v


def _token_tile(seq, cap=512):
    return min(cap, seq)


def _params(*semantics):
    return pltpu.CompilerParams(dimension_semantics=semantics, vmem_limit_bytes=VMEM_LIMIT)


def _place():
    return lax.axis_index("x"), lax.axis_index("y"), lax.axis_index("c")


class _Gather:
    def __init__(self, arrays):
        n = len(arrays)
        self.ins = list(arrays)
        self.out_shape = [jax.ShapeDtypeStruct((N_DEV,) + a.shape, a.dtype) for a in arrays]
        self.scratch = [pltpu.SemaphoreType.DMA((n, 7)), pltpu.SemaphoreType.DMA((n, 7)),
                        pltpu.SemaphoreType.DMA((n,))]

    def _copies(self, ins, outs, sems):
        send_sems, recv_sems, local_sems = sems
        n = len(ins)
        x, y, c = _place()
        me, sibling = (x, y, c), (x, y, 1 - c)
        chips = [(1 - x, y), (x, 1 - y), (1 - x, 1 - y)]

        def slot(a, px, py, pc):
            return outs[a].at[4 * px + 2 * py + pc]

        def copy(a, k, block, to, src=None):
            return pltpu.make_async_remote_copy(
                src_ref=slot(a, *block) if src is None else src, dst_ref=slot(a, *block),
                send_sem=send_sems.at[a, k], recv_sem=recv_sems.at[a, k], device_id=to, device_id_type=MESH)

        mine = [pltpu.make_async_copy(ins[a], slot(a, *me), local_sems.at[a]) for a in range(n)]
        first = [copy(a, 0, me, sibling, src=ins[a]) for a in range(n)]
        first += [copy(a, 1 + j, me, (*chip, c), src=ins[a]) for a in range(n) for j, chip in enumerate(chips)]
        landed = [[copy(a, 1 + j, (*chip, c), me) for a in range(n)] for j, chip in enumerate(chips)]
        passed = [[copy(a, 4 + j, (*chip, c), sibling) for a in range(n)] for j, chip in enumerate(chips)]
        from_sibling = [copy(a, 0, sibling, me) for a in range(n)]
        from_sibling += [copy(a, 4 + j, (*chip, 1 - c), me) for a in range(n) for j, chip in enumerate(chips)]
        return mine, first, landed, passed, from_sibling

    def start(self, ins, outs, sems):
        mine, first, _, _, _ = self._copies(ins, outs, sems)
        for cp in mine + first:
            cp.start()

    def relay(self, ins, outs, sems):
        _, _, landed, passed, _ = self._copies(ins, outs, sems)
        for arrivals, forwards in zip(landed, passed):
            for arrival, forward in zip(arrivals, forwards):
                arrival.wait_recv()
                forward.start()

    def finish(self, ins, outs, sems):
        mine, first, _, passed, from_sibling = self._copies(ins, outs, sems)
        for cp in from_sibling:
            cp.wait_recv()
        for cp in first + [f for fs in passed for f in fs]:
            cp.wait_send()
        for cp in mine:
            cp.wait()


class _RsSibling:
    def __init__(self, ps):
        n = len(ps)
        self.ins = list(ps)
        self.out_shape = [jax.ShapeDtypeStruct((4,) + p.shape[1:], p.dtype) for p in ps]
        self.scratch = [pltpu.SemaphoreType.DMA((n, 4)), pltpu.SemaphoreType.DMA((n, 4))]

    def _copies(self, ins, outs, sems):
        send_sems, recv_sems = sems
        x, y, c = _place()
        return [pltpu.make_async_remote_copy(
            src_ref=ins[a].at[2 * q + (1 - c)], dst_ref=outs[a].at[q], send_sem=send_sems.at[a, q],
            recv_sem=recv_sems.at[a, q], device_id=(x, y, 1 - c), device_id_type=MESH)
            for a in range(len(ins)) for q in range(4)]

    def start(self, ins, outs, sems):
        for cp in self._copies(ins, outs, sems):
            cp.start()

    def relay(self, ins, outs, sems):
        pass

    def finish(self, ins, outs, sems):
        for cp in self._copies(ins, outs, sems):
            cp.wait()


class _RsChips:
    def __init__(self, qs):
        n = len(qs)
        self.ins = list(qs)
        self.out_shape = [jax.ShapeDtypeStruct(q.shape, q.dtype) for q in qs]
        self.scratch = [pltpu.SemaphoreType.DMA((n, 3)), pltpu.SemaphoreType.DMA((n, 3)),
                        pltpu.SemaphoreType.DMA((n,))]

    def _copies(self, ins, outs, sems):
        send_sems, recv_sems, local_sems = sems
        n = len(ins)
        x, y, c = _place()
        my_chip = 2 * x + y
        chips = [(1 - x, y), (x, 1 - y), (1 - x, 1 - y)]

        def copy(a, j, src_slot, dst_slot):
            px, py = chips[j]
            return pltpu.make_async_remote_copy(
                src_ref=ins[a].at[src_slot], dst_ref=outs[a].at[dst_slot], send_sem=send_sems.at[a, j],
                recv_sem=recv_sems.at[a, j], device_id=(px, py, c), device_id_type=MESH)

        own = [pltpu.make_async_copy(ins[a].at[my_chip], outs[a].at[my_chip], local_sems.at[a]) for a in range(n)]
        sends = [copy(a, j, 2 * px + py, my_chip) for a in range(n) for j, (px, py) in enumerate(chips)]
        arrivals = [copy(a, j, my_chip, 2 * px + py) for a in range(n) for j, (px, py) in enumerate(chips)]
        return own, sends, arrivals

    def start(self, ins, outs, sems):
        own, sends, _ = self._copies(ins, outs, sems)
        for cp in own + sends:
            cp.start()

    def relay(self, ins, outs, sems):
        pass

    def finish(self, ins, outs, sems):
        own, sends, arrivals = self._copies(ins, outs, sems)
        for cp in arrivals:
            cp.wait_recv()
        for cp in sends:
            cp.wait_send()
        for cp in own:
            cp.wait()


def _split_plans(plans, refs_in, refs_out, refs_scr, phase):
    i = o = s = 0
    for p in plans:
        ni, no, ns = len(p.ins), len(p.out_shape), len(p.scratch)
        getattr(p, phase)(refs_in[i:i + ni], refs_out[o:o + no], refs_scr[s:s + ns])
        i, o, s = i + ni, o + no, s + ns


def _plan_results(plans, res):
    out = []
    for p in plans:
        out.append(list(res[:len(p.out_shape)]))
        res = res[len(p.out_shape):]
    return out


def _exchange(plans, name):
    c_in = [a for p in plans for a in p.ins]
    c_out = [s for p in plans for s in p.out_shape]
    c_scr = [s for p in plans for s in p.scratch]

    def body(*refs):
        cin, cout, cscr = refs[:len(c_in)], refs[len(c_in):len(c_in) + len(c_out)], refs[len(c_in) + len(c_out):]
        for phase in ("start", "relay", "finish"):
            _split_plans(plans, cin, cout, cscr, phase)

    res = pl.pallas_call(body, name=name, in_specs=[_any()] * len(c_in), out_specs=[_any()] * len(c_out),
                         out_shape=c_out, scratch_shapes=c_scr)(*c_in)
    return _plan_results(plans, res)


def _call(body, *, name, grid, in_specs, out_specs, out_shape, args, semantics, scratch_shapes=(), plans=()):
    n_in, n_out, n_scr = len(in_specs), len(out_specs), len(scratch_shapes)
    c_in = [a for p in plans for a in p.ins]
    c_out = [s for p in plans for s in p.out_shape]
    c_scr = [s for p in plans for s in p.scratch]
    n_steps = math.prod(grid)

    def wrapped(*refs):
        ins, refs = refs[:n_in], refs[n_in:]
        cin, refs = refs[:len(c_in)], refs[len(c_in):]
        outs, refs = refs[:n_out], refs[n_out:]
        cout, refs = refs[:len(c_out)], refs[len(c_out):]
        scr, cscr = refs[:n_scr], refs[n_scr:]
        if plans:
            step = 0
            for d, g in enumerate(grid):
                step = step * g + pl.program_id(d)
            pl.when(step == 0)(lambda: _split_plans(plans, cin, cout, cscr, "start"))
        body(*ins, *outs, *scr)
        if plans:
            @pl.when(step == n_steps - 1)
            def _():
                _split_plans(plans, cin, cout, cscr, "relay")
                _split_plans(plans, cin, cout, cscr, "finish")

    res = pl.pallas_call(
        wrapped, name=name, grid=grid, in_specs=list(in_specs) + [_any()] * len(c_in),
        out_specs=list(out_specs) + [_any()] * len(c_out), out_shape=list(out_shape) + c_out,
        scratch_shapes=list(scratch_shapes) + c_scr,
        compiler_params=_params(*(("arbitrary",) * len(grid) if plans else semantics)),
    )(*args, *c_in)
    return list(res[:n_out]), _plan_results(plans, res[n_out:])


def _ffn_fwd(h, mods, gn, wg, wu, wd, row0, seq, tgt=None, plans=()):
    T = h.shape[0]
    tm = _token_tile(seq, 256)
    tps = seq // tm
    n_t = T // tm
    with_loss = tgt is not None

    def body(h_ref, m_ref, g_ref, wg_ref, wu_ref, wd_ref, *rest):
        if with_loss:
            tgt_ref, out_ref, y_ref, gb_ref, ub_ref, loss_ref = rest
        else:
            out_ref, y_ref, gb_ref, ub_ref = rest
        hv = h_ref[...]
        sh = m_ref[0, row0:row0 + 1, :]
        sc = m_ref[0, row0 + 1:row0 + 2, :]
        ga = m_ref[0, row0 + 2:row0 + 3, :]
        xn, _, _ = _rms_mod(hv, g_ref[...], sh, sc)
        xb = xn.astype(BF16)
        acc = jnp.zeros((tm, D), F32)
        for c0 in range(0, FF, FC):
            gg = _dot_nt(xb, wg_ref[c0:c0 + FC, :])
            uu = _dot_nt(xb, wu_ref[c0:c0 + FC, :])
            gg, uu = gg.astype(BF16), uu.astype(BF16)
            gb_ref[:, c0:c0 + FC] = gg
            ub_ref[:, c0:c0 + FC] = uu
            acc = acc + _dot((gg * jax.nn.sigmoid(gg)) * uu, wd_ref[c0:c0 + FC, :])
        y_ref[...] = acc
        hout = hv + (0.5 * ga) * acc
        if with_loss:
            d = hout - tgt_ref[...]
            out_ref[...] = d * (1.0 / D)
            loss_ref[...] = jnp.full((1, 8, 128), 0.5 / D, F32) * jnp.sum(d * d)
        else:
            out_ref[...] = hout

    tok = pl.BlockSpec((tm, D), lambda i: (i, 0))
    tokf = pl.BlockSpec((tm, FF), lambda i: (i, 0))
    in_specs = [tok, pl.BlockSpec((1, 9, D), lambda i: (i // tps, 0, 0)), pl.BlockSpec((1, D), lambda i: (0, 0)),
                _vmem(), _vmem(), _vmem()]
    out_shape = [jax.ShapeDtypeStruct((T, D), F32), jax.ShapeDtypeStruct((T, D), F32),
                 jax.ShapeDtypeStruct((T, FF), BF16), jax.ShapeDtypeStruct((T, FF), BF16)]
    out_specs = [tok, tok, tokf, tokf]
    args = [h, mods, gn, wg, wu, wd]
    if with_loss:
        in_specs.append(tok)
        args.append(tgt)
        out_shape.append(jax.ShapeDtypeStruct((n_t, 8, 128), F32))
        out_specs.append(pl.BlockSpec((1, 8, 128), lambda i: (i, 0, 0)))
    return _call(body, name="ffn_fwd_loss" if with_loss else "ffn_fwd", grid=(n_t,), in_specs=in_specs,
                 out_specs=out_specs, out_shape=out_shape, args=args, semantics=("parallel",), plans=plans)


def _ffn_gate_up(h, mods, gn, wg, wu, row0, seq, plans=()):
    T = h.shape[0]
    tm = _token_tile(seq, 512)
    tps = seq // tm

    def body(h_ref, m_ref, g_ref, wg_ref, wu_ref, gb_ref, ub_ref):
        xn, _, _ = _rms_mod(h_ref[...], g_ref[...], m_ref[0, row0:row0 + 1, :], m_ref[0, row0 + 1:row0 + 2, :])
        xb = xn.astype(BF16)
        for c0 in range(0, FF, FC):
            gb_ref[:, c0:c0 + FC] = _dot_nt(xb, wg_ref[c0:c0 + FC, :]).astype(BF16)
            ub_ref[:, c0:c0 + FC] = _dot_nt(xb, wu_ref[c0:c0 + FC, :]).astype(BF16)

    tokf = pl.BlockSpec((tm, FF), lambda i: (i, 0))
    return _call(
        body, name="ffn_gate_up", grid=(T // tm,),
        in_specs=[pl.BlockSpec((tm, D), lambda i: (i, 0)), pl.BlockSpec((1, 9, D), lambda i: (i // tps, 0, 0)),
                  pl.BlockSpec((1, D), lambda i: (0, 0)), _vmem(), _vmem()],
        out_specs=[tokf, tokf], out_shape=[jax.ShapeDtypeStruct((T, FF), BF16)] * 2, args=(h, mods, gn, wg, wu),
        semantics=("parallel",), plans=plans)


def _ffn_down(h, gb, ub, mods, wd, row0, seq, plans=()):
    T = h.shape[0]
    tm = _token_tile(seq, 512)
    tps = seq // tm

    def body(h_ref, gb_ref, ub_ref, m_ref, wd_ref, out_ref, y_ref):
        acc = jnp.zeros((tm, D), F32)
        for c0 in range(0, FF, FC):
            gg = gb_ref[:, c0:c0 + FC]
            acc = acc + _dot((gg * jax.nn.sigmoid(gg)) * ub_ref[:, c0:c0 + FC], wd_ref[c0:c0 + FC, :])
        y_ref[...] = acc
        out_ref[...] = h_ref[...] + (0.5 * m_ref[0, row0 + 2:row0 + 3, :]) * acc

    tok = pl.BlockSpec((tm, D), lambda i: (i, 0))
    tokf = pl.BlockSpec((tm, FF), lambda i: (i, 0))
    return _call(
        body, name="ffn_down", grid=(T // tm,),
        in_specs=[tok, tokf, tokf, pl.BlockSpec((1, 9, D), lambda i: (i // tps, 0, 0)), _vmem()],
        out_specs=[tok, tok], out_shape=[jax.ShapeDtypeStruct((T, D), F32)] * 2, args=(h, gb, ub, mods, wd),
        semantics=("parallel",), plans=plans)


def _ffn_bwd(h, dhn, y, gb, ub, mods, gn, wg, wu, wd, row0, seq, plans=()):
    T = h.shape[0]
    B = T // seq
    tm = _token_tile(seq, 256)
    tps = seq // tm
    n_t = T // tm

    def body(h_ref, dhn_ref, y_ref, gb_ref, ub_ref, m_ref, g_ref, wg_ref, wu_ref, wd_ref,
             dh_ref, xb_ref, dyb_ref, a_ref, dg_ref, du_ref, mg_ref):
        i = pl.program_id(0)
        hv = h_ref[...]
        dhn = dhn_ref[...]
        sh = m_ref[0, row0:row0 + 1, :]
        sc = m_ref[0, row0 + 1:row0 + 2, :]
        ga = m_ref[0, row0 + 2:row0 + 3, :]
        g = g_ref[...]
        xn, r, inv = _rms_mod(hv, g, sh, sc)
        xb_ref[...] = xn.astype(BF16)
        dyb = ((0.5 * ga) * dhn).astype(BF16)
        dyb_ref[...] = dyb
        dga = 0.5 * jnp.sum(dhn * y_ref[...], axis=0, keepdims=True)
        dxn = jnp.zeros((tm, D), F32)
        for c0 in range(0, FF, FC):
            wgc = wg_ref[c0:c0 + FC, :]
            wuc = wu_ref[c0:c0 + FC, :]
            gg = gb_ref[:, c0:c0 + FC]
            uu = ub_ref[:, c0:c0 + FC]
            sig = jax.nn.sigmoid(gg)
            s = gg * sig
            a_ref[:, c0:c0 + FC] = s * uu
            da = _dot_nt(dyb, wd_ref[c0:c0 + FC, :]).astype(BF16)
            dub = da * s
            dgb = (da * uu) * (sig * (1.0 + gg * (1.0 - sig)))
            dg_ref[:, c0:c0 + FC] = dgb
            du_ref[:, c0:c0 + FC] = dub
            dxn = dxn + _dot(dgb, wgc) + _dot(dub, wuc)
        dh, s_dxn, s_dxr = _rms_mod_bwd(dxn, r, inv, g, sc)
        dh_ref[...] = dhn + dh

        @pl.when(i % tps == 0)
        def _():
            mg_ref[...] = jnp.zeros(mg_ref.shape, F32)

        mg_ref[0, 0:1, :] += s_dxn
        mg_ref[0, 1:2, :] += s_dxr
        mg_ref[0, 2:3, :] += dga

    tok = pl.BlockSpec((tm, D), lambda i: (i, 0))
    tokf = pl.BlockSpec((tm, FF), lambda i: (i, 0))
    return _call(
        body, name="ffn_bwd", grid=(n_t,),
        in_specs=[tok, tok, tok, tokf, tokf, pl.BlockSpec((1, 9, D), lambda i: (i // tps, 0, 0)),
                  pl.BlockSpec((1, D), lambda i: (0, 0)), _vmem(), _vmem(), _vmem()],
        out_specs=[tok, tok, tok, tokf, tokf, tokf, pl.BlockSpec((1, 8, D), lambda i: (i // tps, 0, 0))],
        out_shape=[jax.ShapeDtypeStruct((T, D), F32), jax.ShapeDtypeStruct((T, D), BF16),
                   jax.ShapeDtypeStruct((T, D), BF16), jax.ShapeDtypeStruct((T, FF), BF16),
                   jax.ShapeDtypeStruct((T, FF), BF16), jax.ShapeDtypeStruct((T, FF), BF16),
                   jax.ShapeDtypeStruct((B, 8, D), F32)],
        args=(h, dhn, y, gb, ub, mods, gn, wg, wu, wd), semantics=("arbitrary",), plans=plans)


def _wgrad(a, b, name, plans=()):
    T, da = a.shape
    db = b.shape[1]
    bm = {2816: 1408, 3840: 1280}[da]
    bn = db
    tk = min(1024, T)
    nk = T // tk

    def body(a_ref, b_ref, o_ref):
        @pl.when(pl.program_id(2) == 0)
        def _():
            o_ref[...] = jnp.zeros(o_ref.shape, F32)

        o_ref[...] += _dot_tn(a_ref[...], b_ref[...])

    (out,), plan_outs = _call(
        body, name=name, grid=(da // bm, db // bn, nk),
        in_specs=[pl.BlockSpec((tk, bm), lambda i, j, k: (k, i)), pl.BlockSpec((tk, bn), lambda i, j, k: (k, j))],
        out_specs=[pl.BlockSpec((bm, bn), lambda i, j, k: (i, j))], out_shape=[jax.ShapeDtypeStruct((da, db), F32)],
        args=(a, b), semantics=("parallel", "parallel", "arbitrary"), plans=plans)
    return out, plan_outs


def _inproj_fwd(h, mods, gn, w_in, seq, plans=()):
    T = h.shape[0]
    tm = _token_tile(seq)
    tps = seq // tm

    def body(h_ref, m_ref, g_ref, w_ref, *outs):
        xn, _, _ = _rms_mod(h_ref[...], g_ref[...], m_ref[0, 3:4, :], m_ref[0, 4:5, :])
        xb = xn.astype(BF16)
        for (_, c0, w), o_ref in zip(PIECES, outs):
            o_ref[...] = _dot_nt(xb, w_ref[c0:c0 + w, :])

    return _call(
        body, name="inproj_fwd", grid=(T // tm,),
        in_specs=[pl.BlockSpec((tm, D), lambda i: (i, 0)), pl.BlockSpec((1, 9, D), lambda i: (i // tps, 0, 0)),
                  pl.BlockSpec((1, D), lambda i: (0, 0)), _vmem()],
        out_specs=[pl.BlockSpec((tm, w), lambda i: (i, 0)) for _, _, w in PIECES],
        out_shape=[jax.ShapeDtypeStruct((T, w), F32) for _, _, w in PIECES], args=(h, mods, gn, w_in),
        semantics=("parallel",), plans=plans)


def _inproj_bwd(h, dh_res, dpieces, mods, gn, w_in, seq, plans=()):
    T = h.shape[0]
    B = T // seq
    tm = _token_tile(seq, 256)
    tps = seq // tm

    def body(h_ref, dres_ref, *rest):
        dp_refs = rest[:len(PIECES)]
        m_ref, g_ref, w_ref, dh_ref, xb_ref, dpb_ref, mg_ref = rest[len(PIECES):]
        i = pl.program_id(0)
        g = g_ref[...]
        sc = m_ref[0, 4:5, :]
        xn, r, inv = _rms_mod(h_ref[...], g, m_ref[0, 3:4, :], sc)
        xb_ref[...] = xn.astype(BF16)
        dxn = jnp.zeros((tm, D), F32)
        for (_, c0, w), dp_ref in zip(PIECES, dp_refs):
            dpb = dp_ref[...].astype(BF16)
            dpb_ref[:, c0:c0 + w] = dpb
            dxn = dxn + _dot(dpb, w_ref[c0:c0 + w, :])
        dh, s_dxn, s_dxr = _rms_mod_bwd(dxn, r, inv, g, sc)
        dh_ref[...] = dres_ref[...] + dh

        @pl.when(i % tps == 0)
        def _():
            mg_ref[...] = jnp.zeros(mg_ref.shape, F32)

        mg_ref[0, 0:1, :] += s_dxn
        mg_ref[0, 1:2, :] += s_dxr

    tok = pl.BlockSpec((tm, D), lambda i: (i, 0))
    return _call(
        body, name="inproj_bwd", grid=(T // tm,),
        in_specs=[tok, tok] + [pl.BlockSpec((tm, w), lambda i: (i, 0)) for _, _, w in PIECES]
        + [pl.BlockSpec((1, 9, D), lambda i: (i // tps, 0, 0)), pl.BlockSpec((1, D), lambda i: (0, 0)), _vmem()],
        out_specs=[tok, tok, pl.BlockSpec((tm, IN_COLS), lambda i: (i, 0)),
                   pl.BlockSpec((1, 8, D), lambda i: (i // tps, 0, 0))],
        out_shape=[jax.ShapeDtypeStruct((T, D), F32), jax.ShapeDtypeStruct((T, D), BF16),
                   jax.ShapeDtypeStruct((T, IN_COLS), BF16), jax.ShapeDtypeStruct((B, 8, D), F32)],
        args=(h, dh_res, *dpieces, mods, gn, w_in), semantics=("arbitrary",), plans=plans)


def _seg_mean(x):
    i = lax.broadcasted_iota(jnp.int32, (128, 128), 0) >> 6
    j = lax.broadcasted_iota(jnp.int32, (128, 128), 1) >> 6
    ones = jnp.where(i == j, 1.0 / HD, 0.0).astype(BF16)
    hi = x.astype(BF16)
    lo = (x - hi.astype(F32)).astype(BF16)
    return _dot(hi, ones) + _dot(lo, ones)


def _head_norm(x, g2):
    inv = lax.rsqrt(_seg_mean(x * x) + EPS)
    y = x * inv
    return y * g2, y, inv


def _head_norm_bwd(dxn, y, inv, g2):
    dy = dxn * g2
    return inv * (dy - y * _seg_mean(dy * y)), jnp.sum(dxn * y, axis=0, keepdims=True)


def _swa_block(q, kk, vv, gq2, gk2, sinks, first, do=None):
    lo = lax.broadcasted_iota(jnp.int32, (1, 128), 1) < HD
    kn, ky, kinv = _head_norm(kk, gk2)

    def operands(x):
        xr = pltpu.roll(x, HD, 1)
        own_lo, own_hi = jnp.where(lo, x, 0.0).astype(BF16), jnp.where(lo, 0.0, x).astype(BF16)
        rolled_lo, rolled_hi = jnp.where(lo, xr, 0.0).astype(BF16), jnp.where(lo, 0.0, xr).astype(BF16)
        return (own_lo, rolled_hi), (rolled_lo, own_hi)

    def restore(parts):
        (own_lo, rolled_hi), (rolled_lo, own_hi) = parts
        return (jnp.where(lo, own_lo, own_hi)
                + pltpu.roll(jnp.where(lo, rolled_lo, rolled_hi), HD, 1))

    k_ops, v_ops = operands(kn), operands(vv)
    k2 = [jnp.concatenate(pair, axis=0) for pair in k_ops]
    def stack(x):
        return jnp.concatenate([x[:, 128 * p:128 * (p + 1)] for p in range(4)], axis=0)

    def unstack(x):
        return jnp.concatenate([x[BLK * p:BLK * (p + 1)] for p in range(4)], axis=1)

    def of_head(x, kh):
        return x[2 * BLK * kh:2 * BLK * (kh + 1)]

    nq = 4 * BLK
    pair = lax.broadcasted_iota(jnp.int32, (nq, 1), 0) >> 7
    row = lax.broadcasted_iota(jnp.int32, (nq, 2 * BLK), 0) & (BLK - 1)
    col = lax.broadcasted_iota(jnp.int32, (nq, 2 * BLK), 1)
    valid = (col <= row + BLK) & (col > row) & (col >= jnp.where(first, BLK, 0))
    scale = HD ** -0.5
    qn, qy, qinv = _head_norm(stack(q), gq2)
    qnb = qn.astype(BF16)
    s2 = jnp.concatenate([_dot_nt(of_head(qnb, kh), k2[kh]) for kh in range(N_KV)], axis=0) * scale
    probs, p_sink = [], []
    for j in range(2):
        s = jnp.where(valid, s2[:, 2 * BLK * j:2 * BLK * (j + 1)], NEG)
        sink = jnp.zeros((nq, 1), F32)
        for p in range(4):
            sink = jnp.where(pair == p, sinks[:, 2 * p + j:2 * p + j + 1], sink)
        m = jnp.maximum(jnp.max(s, axis=-1, keepdims=True), sink)
        e = jnp.exp(s - m)
        e_sink = jnp.exp(sink - m)
        rden = 1.0 / (jnp.sum(e, axis=-1, keepdims=True) + e_sink)
        probs.append(e * rden)
        p_sink.append(e_sink * rden)
    pb = [p.astype(BF16) for p in probs]
    if do is None:
        return unstack(jnp.concatenate(
            [_dot(of_head(pb[0], kh), v_ops[kh][0]) + _dot(of_head(pb[1], kh), v_ops[kh][1]) for kh in range(N_KV)],
            axis=0))
    dob = stack(do).astype(BF16)
    ds, dsinks = [], [None] * 8
    for j in range(2):
        dp = jnp.concatenate([_dot_nt(of_head(dob, kh), v_ops[kh][j]) for kh in range(N_KV)], axis=0)
        t = jnp.sum(dp * probs[j], axis=-1, keepdims=True)
        ds.append(probs[j] * (dp - t) * scale)
        lost = p_sink[j] * t
        for p in range(4):
            dsinks[2 * p + j] = -jnp.sum(lost[BLK * p:BLK * (p + 1)])
    dsb = jnp.concatenate(ds, axis=1).astype(BF16)
    dqn = jnp.concatenate([_dot(of_head(dsb, kh), k2[kh]) for kh in range(N_KV)], axis=0)
    dq, dgq2 = _head_norm_bwd(dqn, qy, qinv, gq2)
    dk2 = [_dot_tn(of_head(dsb, kh), of_head(qnb, kh)) for kh in range(N_KV)]
    dv_ops = [[_dot_tn(of_head(pb[j], kh), of_head(dob, kh)) for j in range(2)] for kh in range(N_KV)]
    dkn = restore(tuple((d[:2 * BLK], d[2 * BLK:]) for d in dk2))
    dvv = restore(tuple(tuple(d) for d in dv_ops))
    dkk, dgk2 = _head_norm_bwd(dkn, ky, kinv, gk2)
    return unstack(dq), dkk, dvv, dgq2, dgk2, dsinks


SWA_GROUP = 8


def _swa_specs(nbs):
    grp = min(SWA_GROUP, nbs)
    rows = grp * BLK

    def tok(w):
        return pl.BlockSpec((rows, w), lambda i: (i, 0))

    halo = pl.BlockSpec((BLK, 128), lambda i: (jnp.maximum(i * grp - 1, 0), 0))
    vec = pl.BlockSpec((1, 128), lambda i: (0, 0))
    sk = pl.BlockSpec((1, 8), lambda i: (0, 0))
    return grp, tok, halo, vec, sk


def _swa_fwd(q, k, v, gq2, gk2, sinks, nbs, plans=()):
    T = q.shape[0]
    grp, tok, halo, vec, sk = _swa_specs(nbs)

    def body(q_ref, kh_ref, kc_ref, vh_ref, vc_ref, gq_ref, gk_ref, sk_ref, o_ref):
        seq_start = ((pl.program_id(0) * grp) % nbs) == 0
        for g in range(grp):
            rows = slice(g * BLK, (g + 1) * BLK)
            prev = slice((g - 1) * BLK, g * BLK)
            kk = jnp.concatenate([kh_ref[...] if g == 0 else kc_ref[prev, :], kc_ref[rows, :]], axis=0)
            vv = jnp.concatenate([vh_ref[...] if g == 0 else vc_ref[prev, :], vc_ref[rows, :]], axis=0)
            o_ref[rows, :] = _swa_block(q_ref[rows, :], kk, vv, gq_ref[...], gk_ref[...], sk_ref[...],
                                        seq_start if g == 0 else False)

    (out,), plan_outs = _call(
        body, name="swa_fwd", grid=(T // (grp * BLK),),
        in_specs=[tok(D_B), halo, tok(128), halo, tok(128), vec, vec, sk], out_specs=[tok(D_B)],
        out_shape=[jax.ShapeDtypeStruct((T, D_B), F32)], args=(q, k, k, v, v, gq2, gk2, sinks),
        semantics=("parallel",), plans=plans)
    return out, plan_outs


def _swa_bwd(q, k, v, gq2, gk2, sinks, do, nbs, plans=()):
    T = q.shape[0]
    grp, tok, halo, vec, sk = _swa_specs(nbs)
    steps = T // (grp * BLK)

    def body(q_ref, kh_ref, kc_ref, vh_ref, vc_ref, gq_ref, gk_ref, sk_ref, do_ref,
             dq_ref, dk_ref, dv_ref, dkh_ref, dvh_ref, dgq_ref, dgk_ref, dsk_ref):
        i = pl.program_id(0)
        seq_start = ((i * grp) % nbs) == 0

        @pl.when(i == 0)
        def _():
            for r in (dgq_ref, dgk_ref, dsk_ref):
                r[...] = jnp.zeros(r.shape, F32)

        res = []
        for g in range(grp):
            rows = slice(g * BLK, (g + 1) * BLK)
            prev = slice((g - 1) * BLK, g * BLK)
            kk = jnp.concatenate([kh_ref[...] if g == 0 else kc_ref[prev, :], kc_ref[rows, :]], axis=0)
            vv = jnp.concatenate([vh_ref[...] if g == 0 else vc_ref[prev, :], vc_ref[rows, :]], axis=0)
            res.append(_swa_block(q_ref[rows, :], kk, vv, gq_ref[...], gk_ref[...], sk_ref[...],
                                  seq_start if g == 0 else False, do=do_ref[rows, :]))
        lane = lax.broadcasted_iota(jnp.int32, (8, 128), 1)
        upd = jnp.zeros((8, 128), F32)
        for g, (dq, dkk, dvv, dgq2, dgk2, dsinks) in enumerate(res):
            rows = slice(g * BLK, (g + 1) * BLK)
            dq_ref[rows, :] = dq
            dk_ref[rows, :] = dkk[BLK:] + res[g + 1][1][:BLK] if g + 1 < grp else dkk[BLK:]
            dv_ref[rows, :] = dvv[BLK:] + res[g + 1][2][:BLK] if g + 1 < grp else dvv[BLK:]
            dgq_ref[...] += dgq2
            dgk_ref[...] += dgk2
            for h, d in enumerate(dsinks):
                upd = upd + jnp.where(lane == h, d, 0.0)
        dkh_ref[0] = res[0][1][:BLK]
        dvh_ref[0] = res[0][2][:BLK]
        dsk_ref[...] += upd

    one = pl.BlockSpec((1, BLK, 128), lambda i: (i, 0, 0))
    halo_shape = jax.ShapeDtypeStruct((steps, BLK, 128), F32)
    return _call(
        body, name="swa_bwd", grid=(steps,),
        in_specs=[tok(D_B), halo, tok(128), halo, tok(128), vec, vec, sk, tok(D_B)],
        out_specs=[tok(D_B), tok(128), tok(128), one, one, vec, vec, pl.BlockSpec((8, 128), lambda i: (0, 0))],
        out_shape=[jax.ShapeDtypeStruct((T, D_B), F32), jax.ShapeDtypeStruct((T, 128), F32),
                   jax.ShapeDtypeStruct((T, 128), F32), halo_shape, halo_shape, jax.ShapeDtypeStruct((1, 128), F32),
                   jax.ShapeDtypeStruct((1, 128), F32), jax.ShapeDtypeStruct((8, 128), F32)],
        args=(q, k, k, v, v, gq2, gk2, sinks, do), semantics=("arbitrary",), plans=plans)


def _swa_add_halo(dk, dk_halo):
    steps = dk_halo.shape[0]
    nxt = jnp.concatenate([dk_halo[1:], jnp.zeros_like(dk_halo[:1])], axis=0)[:, None]
    dk = dk.reshape(steps, -1, BLK, 128)
    return jnp.concatenate([dk[:, :-1], dk[:, -1:] + nxt], axis=1).reshape(-1, 128)


def _sgu_norm(av, g_ln, b_ln):
    t, th = _gelu(av)
    mu = jnp.mean(t, axis=-1, keepdims=True)
    tc = t - mu
    rstd = lax.rsqrt(jnp.mean(tc * tc, axis=-1, keepdims=True) + EPS)
    vhat = tc * rstd
    return vhat * g_ln + b_ln, vhat, rstd, th


def _masked_ws(ws_ref):
    tril = lax.broadcasted_iota(jnp.int32, (BLK, BLK), 0) >= lax.broadcasted_iota(jnp.int32, (BLK, BLK), 1)
    return [jnp.where(tril, ws_ref[g], 0.0).astype(BF16) for g in range(N_GRP)]


def _mix_fwd(au, av, gta, gtb, ob, h, mods, g_ln, b_ln, ws, bsb, wa, wb, wout, seq):
    T = h.shape[0]
    tm = _token_tile(seq)
    tps = seq // tm

    def body(au_ref, av_ref, gta_ref, gtb_ref, ob_ref, h_ref, m_ref, gl_ref, bl_ref, ws_ref, bs_ref,
             wa_ref, wb_ref, wo_ref, out_ref, vvb_s, z_s):
        u, _ = _gelu(au_ref[...])
        vv, _, _, _ = _sgu_norm(av_ref[...], gl_ref[...], bl_ref[...])
        vvb_s[...] = vv.astype(BF16)
        wsm = _masked_ws(ws_ref)
        for c in range(tm // BLK):
            rows = slice(c * BLK, (c + 1) * BLK)
            for g in range(N_GRP):
                cols = slice(g * BLK, (g + 1) * BLK)
                z_s[rows, cols] = _dot(wsm[g], vvb_s[rows, cols]) + bs_ref[g]
        ya = _dot_nt((u * z_s[...]).astype(BF16), wa_ref[...])
        yb = _dot_nt(ob_ref[...].astype(BF16), wb_ref[...])
        merged = jax.nn.sigmoid(gta_ref[...]) * ya + jax.nn.sigmoid(gtb_ref[...]) * yb
        out_ref[...] = h_ref[...] + m_ref[0, 5:6, :] * _dot(merged.astype(BF16), wo_ref[...])

    def tok(w):
        return pl.BlockSpec((tm, w), lambda i: (i, 0))

    def full(shape):
        return pl.BlockSpec(shape, lambda i: (0,) * len(shape))

    return pl.pallas_call(
        body, name="mix_fwd", grid=(T // tm,),
        in_specs=[tok(D_A), tok(D_A), tok(D), tok(D), tok(D_B), tok(D),
                  pl.BlockSpec((1, 9, D), lambda i: (i // tps, 0, 0)), full((1, D_A)), full((1, D_A)),
                  full((N_GRP, BLK, BLK)), full((N_GRP, BLK, BLK)), _vmem(), _vmem(), _vmem()],
        out_specs=tok(D), out_shape=jax.ShapeDtypeStruct((T, D), F32),
        scratch_shapes=[pltpu.VMEM((tm, D_A), BF16), pltpu.VMEM((tm, D_A), F32)],
        compiler_params=_params("parallel"),
    )(au, av, gta, gtb, ob, h, mods, g_ln, b_ln, ws, bsb, wa, wb, wout)


def _mix_bwd(au, av, gta, gtb, ob, dh, mods, g_ln, b_ln, ws, bsb, wa, wb, wout, seq, plans=()):
    T = dh.shape[0]
    B = T // seq
    tm = _token_tile(seq, 256)
    tps = seq // tm

    def body(au_ref, av_ref, gta_ref, gtb_ref, ob_ref, dh_ref, m_ref, gl_ref, bl_ref, ws_ref, bs_ref,
             wa_ref, wb_ref, wo_ref,
             dau_ref, dav_ref, dgta_ref, dgtb_ref, dob_ref, dwo_ref, dwa_ref, dwb_ref, dws_ref, dbs_ref, dln_ref,
             mg_ref, vvb_s, z_s, dz_s, dzb_s, dvv_s):
        i = pl.program_id(0)

        @pl.when(i == 0)
        def _():
            for r in (dwo_ref, dwa_ref, dwb_ref, dws_ref, dbs_ref, dln_ref):
                r[...] = jnp.zeros(r.shape, F32)

        @pl.when(i % tps == 0)
        def _():
            mg_ref[...] = jnp.zeros(mg_ref.shape, F32)

        auv = au_ref[...]
        avv = av_ref[...]
        u, thu = _gelu(auv)
        g_ln = gl_ref[...]
        vv, vhat, rstd, thv = _sgu_norm(avv, g_ln, bl_ref[...])
        vvb_s[...] = vv.astype(BF16)
        wsm = _masked_ws(ws_ref)
        for c in range(tm // BLK):
            rows = slice(c * BLK, (c + 1) * BLK)
            for g in range(N_GRP):
                cols = slice(g * BLK, (g + 1) * BLK)
                z_s[rows, cols] = _dot(wsm[g], vvb_s[rows, cols]) + bs_ref[g]
        z = z_s[...]
        yab = (u * z).astype(BF16)
        obb = ob_ref[...].astype(BF16)
        ya = _dot_nt(yab, wa_ref[...])
        yb = _dot_nt(obb, wb_ref[...])
        sa = jax.nn.sigmoid(gta_ref[...])
        sb = jax.nn.sigmoid(gtb_ref[...])
        mb = (sa * ya + sb * yb).astype(BF16)
        dhv = dh_ref[...]
        mg_ref[0, 0:1, :] += jnp.sum(dhv * _dot(mb, wo_ref[...]), axis=0, keepdims=True)
        dmob = (m_ref[0, 5:6, :] * dhv).astype(BF16)
        dwo_ref[...] += _dot_tn(mb, dmob)
        dmerged = _dot_nt(dmob, wo_ref[...])
        dya = dmerged * sa
        dyb = dmerged * sb
        dgta_ref[...] = dya * ya * (1.0 - sa)
        dgtb_ref[...] = dyb * yb * (1.0 - sb)
        dyab = dya.astype(BF16)
        dybb = dyb.astype(BF16)
        dwa_ref[...] += _dot_tn(dyab, yab)
        dwb_ref[...] += _dot_tn(dybb, obb)
        dob_ref[...] = _dot(dybb, wb_ref[...])
        dyap = _dot(dyab, wa_ref[...])
        dau_ref[...] = (dyap * z) * _gelu_grad(auv, thu)
        dz = dyap * u
        dz_s[...] = dz
        dzb_s[...] = dz.astype(BF16)
        for c in range(tm // BLK):
            rows = slice(c * BLK, (c + 1) * BLK)
            for g in range(N_GRP):
                cols = slice(g * BLK, (g + 1) * BLK)
                dzb = dzb_s[rows, cols]
                dvv_s[rows, cols] = _dot_tn(wsm[g], dzb)
                dws_ref[g] += _dot_nt(dzb, vvb_s[rows, cols])
                dbs_ref[g] += dz_s[rows, cols]
        dvv = dvv_s[...]
        dln_ref[0:1, :] += jnp.sum(dvv * vhat, axis=0, keepdims=True)
        dln_ref[1:2, :] += jnp.sum(dvv, axis=0, keepdims=True)
        dvh = dvv * g_ln
        dt = rstd * (dvh - jnp.mean(dvh, axis=-1, keepdims=True)
                     - vhat * jnp.mean(dvh * vhat, axis=-1, keepdims=True))
        dav_ref[...] = dt * _gelu_grad(avv, thv)

    def tok(w):
        return pl.BlockSpec((tm, w), lambda i: (i, 0))

    def full(shape):
        return pl.BlockSpec(shape, lambda i: (0,) * len(shape))

    return _call(
        body, name="mix_bwd", grid=(T // tm,),
        in_specs=[tok(D_A), tok(D_A), tok(D), tok(D), tok(D_B), tok(D),
                  pl.BlockSpec((1, 9, D), lambda i: (i // tps, 0, 0)), full((1, D_A)), full((1, D_A)),
                  full((N_GRP, BLK, BLK)), full((N_GRP, BLK, BLK)), _vmem(), _vmem(), _vmem()],
        out_specs=[tok(D_A), tok(D_A), tok(D), tok(D), tok(D_B), full((D, D)), full((D, D_A)), full((D, D_B)),
                   full((N_GRP, BLK, BLK)), full((N_GRP, BLK, BLK)), full((8, D_A)),
                   pl.BlockSpec((1, 8, D), lambda i: (i // tps, 0, 0))],
        out_shape=[jax.ShapeDtypeStruct((T, D_A), F32), jax.ShapeDtypeStruct((T, D_A), F32),
                   jax.ShapeDtypeStruct((T, D), F32), jax.ShapeDtypeStruct((T, D), F32),
                   jax.ShapeDtypeStruct((T, D_B), F32), jax.ShapeDtypeStruct((D, D), F32),
                   jax.ShapeDtypeStruct((D, D_A), F32), jax.ShapeDtypeStruct((D, D_B), F32),
                   jax.ShapeDtypeStruct((N_GRP, BLK, BLK), F32), jax.ShapeDtypeStruct((N_GRP, BLK, BLK), F32),
                   jax.ShapeDtypeStruct((8, D_A), F32), jax.ShapeDtypeStruct((B, 8, D), F32)],
        scratch_shapes=[pltpu.VMEM((tm, D_A), BF16), pltpu.VMEM((tm, D_A), F32), pltpu.VMEM((tm, D_A), F32),
                        pltpu.VMEM((tm, D_A), BF16), pltpu.VMEM((tm, D_A), F32)],
        args=(au, av, gta, gtb, ob, dh, mods, g_ln, b_ln, ws, bsb, wa, wb, wout), semantics=("arbitrary",),
        plans=plans)


def _prologue(c_pad, w_ada, first_shards):
    cols = w_ada.shape[1]
    plan_w, plan_c = _Gather(first_shards), _Gather([c_pad])
    plan_m = _Gather([jax.ShapeDtypeStruct((N_DEV * 8, cols), F32)])
    n_w = len(first_shards)

    def body(c_ref, wada_ref, *rest):
        w_ins, rest = rest[:n_w], rest[n_w:]
        call_ref, mods_ref = rest[:2]
        w_outs, rest = rest[2:2 + n_w], rest[2 + n_w:]
        cvm, part, local_sem = rest[:3]
        sems = rest[3:]
        sems_w, sems_c, sems_m = sems[:3], sems[3:6], sems[6:9]
        plan_c.start([c_ref], [call_ref], sems_c)
        plan_c.relay([c_ref], [call_ref], sems_c)
        plan_c.finish([c_ref], [call_ref], sems_c)
        load = pltpu.make_async_copy(call_ref, cvm, local_sem)
        load.start()
        load.wait()
        cv = cvm[...].reshape(N_DEV * 8, D)
        part[...] = _dot((cv * jax.nn.sigmoid(cv)).astype(BF16), wada_ref[...].astype(BF16))
        plan_m.start([part], [mods_ref], sems_m)
        plan_w.start(w_ins, w_outs, sems_w)
        plan_m.relay([part], [mods_ref], sems_m)
        plan_m.finish([part], [mods_ref], sems_m)
        plan_w.relay(w_ins, w_outs, sems_w)
        plan_w.finish(w_ins, w_outs, sems_w)

    res = pl.pallas_call(
        body, name="prologue", in_specs=[_any(), _vmem()] + [_any()] * n_w,
        out_specs=[_any()] * (2 + n_w), out_shape=plan_c.out_shape + plan_m.out_shape + plan_w.out_shape,
        scratch_shapes=[pltpu.VMEM((N_DEV, 8, D), F32), pltpu.VMEM((N_DEV * 8, cols), F32), pltpu.SemaphoreType.DMA]
        + plan_w.scratch + plan_c.scratch + plan_m.scratch,
        compiler_params=pltpu.CompilerParams(vmem_limit_bytes=VMEM_LIMIT),
    )(c_pad, w_ada, *first_shards)
    return res[0], res[1], list(res[2:])


def _ada_update(c_all, dm_cols, w, m, v, plans=()):
    n, cols = c_all.shape[0], w.shape[1]

    def body(c_ref, dm_ref, w_ref, m_ref, v_ref, g_ref, d_ref, nm_ref, nv_ref):
        cv = c_ref[...]
        g = _dot_tn((cv * jax.nn.sigmoid(cv)).astype(BF16), dm_ref[...].astype(BF16))
        g_ref[...] = g
        d_ref[...], nm_ref[...], nv_ref[...] = _adamw(w_ref[...], g, m_ref[...], v_ref[...])

    col = pl.BlockSpec((D, 128), lambda j: (0, j))
    return _call(
        body, name="ada_update", grid=(cols // 128,),
        in_specs=[pl.BlockSpec((n, D), lambda j: (0, 0)), pl.BlockSpec((n, 128), lambda j: (0, j)), col, col, col],
        out_specs=[col] * 4, out_shape=[jax.ShapeDtypeStruct(w.shape, F32)] * 4, args=(c_all, dm_cols, w, m, v),
        semantics=("parallel",), plans=plans)


def _mod_finish(mg1, mg2, mg2g, mg3, mods, g1, g2, g3):
    B = mods.shape[0]

    def body(mg1_ref, mg2_ref, mg2g_ref, mg3_ref, m_ref, g1_ref, g2_ref, g3_ref, dm_ref, dgn_ref):
        dgn_ref[...] = jnp.zeros(dgn_ref.shape, F32)
        for k, (mg, g_ref) in enumerate(((mg1_ref, g1_ref), (mg2_ref, g2_ref), (mg3_ref, g3_ref))):
            for b in range(B):
                s_dxr = mg[b, 1:2, :]
                dm_ref[b, 3 * k:3 * k + 1, :] = mg[b, 0:1, :]
                dm_ref[b, 3 * k + 1:3 * k + 2, :] = g_ref[...] * s_dxr
                dm_ref[b, 3 * k + 2:3 * k + 3, :] = mg2g_ref[b, 0:1, :] if k == 1 else mg[b, 2:3, :]
                dgn_ref[k:k + 1, :] += (1.0 + m_ref[b, 3 * k + 1:3 * k + 2, :]) * s_dxr

    args = (mg1, mg2, mg2g, mg3, mods, g1, g2, g3)
    out_shape = [jax.ShapeDtypeStruct((B, 9, D), F32), jax.ShapeDtypeStruct((8, D), F32)]
    return pl.pallas_call(body, name="mod_finish", grid=(1,), in_specs=[_whole(a) for a in args],
                          out_specs=[_whole(o) for o in out_shape], out_shape=out_shape,
                          compiler_params=_params("arbitrary"))(*args)


def _small_update(gathered, params, ms, vs):
    n = len(SMALL)
    B = gathered[0].shape[1]

    def body(*refs):
        gdm, ggn, gln, gws, gbs, ggq, ggk, gsk, gloss = refs[:9]
        w = dict(zip(SMALL, refs[9:9 + n]))
        m = dict(zip(SMALL, refs[9 + n:9 + 2 * n]))
        v = dict(zip(SMALL, refs[9 + 2 * n:9 + 3 * n]))
        outs = refs[9 + 3 * n:-1]
        out = {name: outs[4 * k:4 * k + 4] for k, name in enumerate(SMALL)}

        def total(ref, idx):
            acc = ref[(0,) + idx]
            for dev in range(1, N_DEV):
                acc = acc + ref[(dev,) + idx]
            return acc

        def finish(name, g, idx=(Ellipsis,)):
            d, nm, nv = _adamw(w[name][idx], g, m[name][idx], v[name][idx])
            for ref, val in zip(out[name], (g, d, nm, nv)):
                ref[idx] = val

        g_bada = total(gdm, (slice(0, 1),))
        for b in range(1, B):
            g_bada = g_bada + total(gdm, (slice(b, b + 1),))
        finish("b_ada", g_bada)
        finish("g_norm1", total(ggn, (slice(0, 1),)))
        finish("g_norm2", total(ggn, (slice(1, 2),)))
        finish("g_norm3", total(ggn, (slice(2, 3),)))
        finish("g_sgu_ln", total(gln, (slice(0, 1),)))
        finish("b_sgu_ln", total(gln, (slice(1, 2),)))
        tril = lax.broadcasted_iota(jnp.int32, (BLK, BLK), 0) >= lax.broadcasted_iota(jnp.int32, (BLK, BLK), 1)
        for g in range(N_GRP):
            finish("w_spatial", jnp.where(tril, total(gws, (g,)), 0.0), (0, g))
            finish("b_spatial", jnp.sum(total(gbs, (g,)).T, axis=0, keepdims=True), (0, slice(g, g + 1)))
        finish("g_q", total(ggq, ()))
        finish("g_k", total(ggk, ()))
        finish("attn_sinks", total(gsk, (slice(0, 1), slice(0, N_KV * Q_PER_KV))))
        refs[-1][...] = total(gloss, ())

    args = list(gathered) + [params[k] for k in SMALL] + [ms[k] for k in SMALL] + [vs[k] for k in SMALL]
    out_shape = []
    for k in SMALL:
        out_shape += [jax.ShapeDtypeStruct(params[k].shape, F32)] * 4
    out_shape.append(jax.ShapeDtypeStruct((8, 128), F32))
    res = pl.pallas_call(body, name="small_update", grid=(1,), in_specs=[_whole(a) for a in args],
                         out_specs=[_whole(o) for o in out_shape], out_shape=out_shape,
                         compiler_params=_params("arbitrary"))(*args)
    return {k: res[4 * i:4 * i + 4] for i, k in enumerate(SMALL)}, res[-1][0, 0]


def _rs_add(p, r, core, name):
    _, rows, width = p.shape

    def body(c_ref, p_ref, r_ref, o_ref):
        o_ref[...] = (p_ref[...] + r_ref[...]).astype(BF16)

    return pl.pallas_call(
        body, name="rs_add_" + name, out_shape=jax.ShapeDtypeStruct((4, rows, width), BF16),
        grid_spec=pltpu.PrefetchScalarGridSpec(
            num_scalar_prefetch=1, grid=(4,),
            in_specs=[pl.BlockSpec((1, rows, width), lambda k, c_ref: (2 * k + c_ref[0], 0, 0)),
                      pl.BlockSpec((1, rows, width), lambda k, c_ref: (k, 0, 0))],
            out_specs=pl.BlockSpec((1, rows, width), lambda k, c_ref: (k, 0, 0))),
        compiler_params=_params("parallel"),
    )(core, p, r)


def _adam_big(w, m, v, r, name):
    rows, cols = w.shape

    def body(r_ref, w_ref, m_ref, v_ref, go_ref, d_ref, nm_ref, nv_ref):
        g = r_ref[0].astype(F32)
        for k in range(1, 4):
            g = g + r_ref[k].astype(F32)
        go_ref[...] = g
        d_ref[...], nm_ref[...], nv_ref[...] = _adamw(w_ref[...], g, m_ref[...], v_ref[...])

    br = rows // 2
    shard = pl.BlockSpec((br, cols), lambda i: (i, 0))
    return pl.pallas_call(body, name="adam_" + name, grid=(2,),
                          in_specs=[pl.BlockSpec((4, br, cols), lambda i: (0, i, 0)), shard, shard, shard],
                          out_specs=[shard] * 4, out_shape=[jax.ShapeDtypeStruct(w.shape, F32)] * 4,
                          compiler_params=_params("parallel"))(r, w, m, v)


def kernel(x, c, w_ada, b_ada, g_norm1, ffn1_w_gate, ffn1_w_up, ffn1_w_down, g_norm2, w_in, g_sgu_ln, b_sgu_ln, w_spatial, b_spatial, g_q, g_k, attn_sinks, w_branch_a, w_branch_b, w_out, g_norm3, ffn2_w_gate, ffn2_w_up, ffn2_w_down, loss_target, m_w_ada, m_b_ada, m_g_norm1, m_ffn1_w_gate, m_ffn1_w_up, m_ffn1_w_down, m_g_norm2, m_w_in, m_g_sgu_ln, m_b_sgu_ln, m_w_spatial, m_b_spatial, m_g_q, m_g_k, m_attn_sinks, m_w_branch_a, m_w_branch_b, m_w_out, m_g_norm3, m_ffn2_w_gate, m_ffn2_w_up, m_ffn2_w_down, v_w_ada, v_b_ada, v_g_norm1, v_ffn1_w_gate, v_ffn1_w_up, v_ffn1_w_down, v_g_norm2, v_w_in, v_g_sgu_ln, v_b_sgu_ln, v_w_spatial, v_b_spatial, v_g_q, v_g_k, v_attn_sinks, v_w_branch_a, v_w_branch_b, v_w_out, v_g_norm3, v_ffn2_w_gate, v_ffn2_w_up, v_ffn2_w_down):
    names = ("w_ada", "b_ada", "g_norm1", "ffn1_w_gate", "ffn1_w_up", "ffn1_w_down", "g_norm2", "w_in", "g_sgu_ln",
             "b_sgu_ln", "w_spatial", "b_spatial", "g_q", "g_k", "attn_sinks", "w_branch_a", "w_branch_b", "w_out",
             "g_norm3", "ffn2_w_gate", "ffn2_w_up", "ffn2_w_down")
    w = dict(zip(names, (w_ada, b_ada, g_norm1, ffn1_w_gate, ffn1_w_up, ffn1_w_down, g_norm2, w_in, g_sgu_ln,
                         b_sgu_ln, w_spatial, b_spatial, g_q, g_k, attn_sinks, w_branch_a, w_branch_b, w_out, g_norm3,
                         ffn2_w_gate, ffn2_w_up, ffn2_w_down)))
    m = dict(zip(names, (m_w_ada, m_b_ada, m_g_norm1, m_ffn1_w_gate, m_ffn1_w_up, m_ffn1_w_down, m_g_norm2, m_w_in,
                         m_g_sgu_ln, m_b_sgu_ln, m_w_spatial, m_b_spatial, m_g_q, m_g_k, m_attn_sinks, m_w_branch_a,
                         m_w_branch_b, m_w_out, m_g_norm3, m_ffn2_w_gate, m_ffn2_w_up, m_ffn2_w_down)))
    v = dict(zip(names, (v_w_ada, v_b_ada, v_g_norm1, v_ffn1_w_gate, v_ffn1_w_up, v_ffn1_w_down, v_g_norm2, v_w_in,
                         v_g_sgu_ln, v_b_sgu_ln, v_w_spatial, v_b_spatial, v_g_q, v_g_k, v_attn_sinks, v_w_branch_a,
                         v_w_branch_b, v_w_out, v_g_norm3, v_ffn2_w_gate, v_ffn2_w_up, v_ffn2_w_down)))
    B, seq, _ = x.shape
    T = B * seq
    nbs = seq // BLK
    xi, yi, ci = _place()
    me = 4 * xi + 2 * yi + ci
    core = jnp.reshape(ci, (1,)).astype(jnp.int32)
    layout = {name: (tform, n, width) for name, tform, n, width in BIG}
    shard = {name: w[name][0].astype(BF16).T if tform else w[name][0].astype(BF16) for name, tform, _, _ in BIG}
    wts, parts, big_out = {}, {}, {}

    def gather_plan(group):
        return _Gather([shard[k] for k in group])

    def take(group, gathered):
        for k, g in zip(group, gathered):
            wts[k] = g.reshape(N_DEV * layout[k][1], layout[k][2])

    def blocks(name, grad):
        return grad.reshape(N_DEV, layout[name][1], layout[name][2])

    def to_sibling(names, grads):
        return _RsSibling([blocks(k, g) for k, g in zip(names, grads)])

    def add(names, grads, from_sibling):
        for k, g, r in zip(names, grads, from_sibling):
            parts[k] = _rs_add(blocks(k, g), r, core, k)

    def to_chips(names):
        return _RsChips([parts[k] for k in names])

    def update(names, from_chips):
        for k, r in zip(names, from_chips):
            if layout[k][0]:
                big_out[k] = [o.T[None] for o in _adam_big(w[k][0].T, m[k][0].T, v[k][0].T, r, k)]
            else:
                big_out[k] = [o[None] for o in _adam_big(w[k][0], m[k][0], v[k][0], r, k)]

    g1, g2, g3, g_ln, b_ln = g_norm1, g_norm2, g_norm3, g_sgu_ln, b_sgu_ln
    gq2, gk2 = jnp.tile(g_q, (1, 2)), jnp.tile(g_k, (1, 2))
    ws = w_spatial[0]
    bsb = jnp.broadcast_to(b_spatial[0][:, :, None], (N_GRP, BLK, BLK))
    xf = x.reshape(T, D)
    tgt = loss_target.reshape(T, D)

    ffn1 = ("ffn1_w_gate", "ffn1_w_up", "ffn1_w_down")
    ffn2 = ("ffn2_w_gate", "ffn2_w_up", "ffn2_w_down")
    c_pad = jnp.concatenate([c, jnp.zeros((8 - B, D), F32)], axis=0)
    c_all, mods_cols, gathered = _prologue(c_pad, w_ada[0], [shard[k] for k in ffn1[:2]])
    take(ffn1[:2], gathered)
    c_all = c_all[:, :B].reshape(N_DEV * B, D)
    mine = lax.dynamic_slice_in_dim(mods_cols, 8 * me, B, axis=1)
    mods = (mine.transpose(1, 0, 2).reshape(B, 9 * D) + b_ada).reshape(B, 9, D)

    group = ("ffn1_w_down", "w_in")
    (gb1, ub1), (gathered,) = _ffn_gate_up(xf, mods, g1, wts["ffn1_w_gate"], wts["ffn1_w_up"], 0, seq,
                                           plans=[gather_plan(group)])
    take(group, gathered)
    group = ("w_branch_a", "w_branch_b", "w_out")
    (h1, y1), (gathered,) = _ffn_down(xf, gb1, ub1, mods, wts["ffn1_w_down"], 0, seq, plans=[gather_plan(group)])
    take(group, gathered)
    group = ("ffn2_w_gate",)
    (au, av, q_tok, k_tok, v_tok, gta, gtb), (gathered,) = _inproj_fwd(h1, mods, g2, wts["w_in"], seq,
                                                                       plans=[gather_plan(group)])
    take(group, gathered)
    group = ("ffn2_w_up", "ffn2_w_down")
    ob, (gathered,) = _swa_fwd(q_tok, k_tok, v_tok, gq2, gk2, attn_sinks, nbs, plans=[gather_plan(group)])
    take(group, gathered)
    mixw = (wts["w_branch_a"], wts["w_branch_b"], wts["w_out"])
    h2 = _mix_fwd(au, av, gta, gtb, ob, h1, mods, g_ln, b_ln, ws, bsb, *mixw, seq)
    (dh3, y3, gb3, ub3, loss_parts), _ = _ffn_fwd(h2, mods, g3, *[wts[k] for k in ffn2], 6, seq, tgt=tgt)
    loss_part = jnp.full((8, 128), jnp.sum(loss_parts[:, 0, 0]))

    (dh2, xb, dyb, a, dg, du, mg3), _ = _ffn_bwd(h2, dh3, y3, gb3, ub3, mods, g3, *[wts[k] for k in ffn2], 6, seq)
    d_gate, _ = _wgrad(dg, xb, "wgrad_ffn2_gate")
    d_up, (r,) = _wgrad(du, xb, "wgrad_ffn2_up", plans=[to_sibling(ffn2[:1], [d_gate])])
    add(ffn2[:1], [d_gate], r)
    d_down, (r,) = _wgrad(a, dyb, "wgrad_ffn2_down", plans=[to_sibling(ffn2[1:2], [d_up])])
    add(ffn2[1:2], [d_up], r)

    (dau, dav, dgta, dgtb, dob, d_out, d_a, d_b, dws, dbs, dln, mg2g), (r, from_chips) = _mix_bwd(
        au, av, gta, gtb, ob, dh2, mods, g_ln, b_ln, ws, bsb, *mixw, seq,
        plans=[to_sibling(ffn2[2:], [d_down]), to_chips(ffn2[:2])])
    add(ffn2[2:], [d_down], r)
    update(ffn2[:2], from_chips)
    mixers = ("w_out", "w_branch_a", "w_branch_b")
    (dq, dk, dv, dk_halo, dv_halo, dgq2, dgk2, dsk), (from_chips, r) = _swa_bwd(
        q_tok, k_tok, v_tok, gq2, gk2, attn_sinks, dob, nbs,
        plans=[to_chips(ffn2[2:]), to_sibling(mixers, [d_out, d_a, d_b])])
    update(ffn2[2:], from_chips)
    add(mixers, [d_out, d_a, d_b], r)
    dk, dv = _swa_add_halo(dk, dk_halo), _swa_add_halo(dv, dv_halo)
    dgq, dgk = dgq2[:, :HD] + dgq2[:, HD:], dgk2[:, :HD] + dgk2[:, HD:]
    (dh1, xb2, dpb, mg2), (from_chips, early) = _inproj_bwd(
        h1, dh2, (dau, dav, dq, dk, dv, dgta, dgtb), mods, g2, wts["w_in"], seq,
        plans=[to_chips(mixers), _Gather([dln, dws, dbs, dgq, dgk, dsk, loss_part])])
    update(mixers, from_chips)

    (dx, xb, dyb, a, dg, du, mg1), _ = _ffn_bwd(xf, dh1, y1, gb1, ub1, mods, g1, *[wts[k] for k in ffn1], 0, seq)
    dmods, dgn = _mod_finish(mg1, mg2, mg2g, mg3, mods, g1, g2, g3)
    d_gate, (late,) = _wgrad(dg, xb, "wgrad_ffn1_gate", plans=[_Gather([dmods.reshape(B, 9 * D), dgn])])
    gathered = late + early
    d_up, (r,) = _wgrad(du, xb, "wgrad_ffn1_up", plans=[to_sibling(ffn1[:1], [d_gate])])
    add(ffn1[:1], [d_gate], r)
    d_in, (r, from_chips) = _wgrad(dpb, xb2, "wgrad_w_in",
                                   plans=[to_sibling(ffn1[1:2], [d_up]), to_chips(ffn1[:1])])
    add(ffn1[1:2], [d_up], r)
    update(ffn1[:1], from_chips)
    d_down, (r, from_chips) = _wgrad(a, dyb, "wgrad_ffn1_down",
                                     plans=[to_sibling(("w_in",), [d_in]), to_chips(ffn1[1:2])])
    add(("w_in",), [d_in], r)
    update(ffn1[1:2], from_chips)
    small_out, loss = _small_update(gathered, w, m, v)
    dm_cols = lax.dynamic_slice_in_dim(gathered[0].reshape(N_DEV * B, 9 * D), (9 * D // N_DEV) * me,
                                       9 * D // N_DEV, axis=1)
    ada_out, (r, from_chips) = _ada_update(c_all, dm_cols, w_ada[0], m_w_ada[0], v_w_ada[0],
                                           plans=[to_sibling(ffn1[2:], [d_down]), to_chips(("w_in",))])
    add(ffn1[2:], [d_down], r)
    update(("w_in",), from_chips)
    (from_chips,) = _exchange([to_chips(ffn1[2:])], "rs_last")
    update(ffn1[2:], from_chips)

    def leaf(kind, name):
        if name == "w_ada":
            return ada_out[kind][None]
        if name in SMALL:
            return small_out[name][kind]
        return big_out[name][kind]

    return (loss, dx.reshape(B, seq, D), *[leaf(kind, name) for kind in range(4) for name in names])
```

```python
import functools
import math

import jax
import jax.numpy as jnp
from jax import lax
from jax.experimental import pallas as pl
from jax.experimental.pallas import tpu as pltpu

F32 = jnp.float32
BF16 = jnp.bfloat16
MESH = pl.DeviceIdType.MESH
AXES = ("x", "y", "c")
N_DEV = 8

VMEM_LIMIT = 56 * 1024 * 1024

D = 1024
FF = 2816
FC = 1408
D_A = 512
D_B = 512
HD = 64
N_KV = 2
Q_PER_KV = 4
BLK = 128
N_GRP = 4
IN_COLS = 3840
PIECES = (("au", 0, 512), ("av", 512, 512), ("q", 1024, 512), ("k", 1536, 128), ("v", 1664, 128),
          ("ga", 1792, 1024), ("gb", 2816, 1024))
EPS = 1e-6
NEG = -1e30
GELU_C = math.sqrt(2.0 / math.pi)

ADAM_LR = 0.001
ADAM_B1 = 0.9
ADAM_B2 = 0.999
ADAM_EPS = 1e-08
ADAM_WD = 0.01
ADAM_STEP = 10

NT = (((1,), (1,)), ((), ()))
TN = (((0,), (0,)), ((), ()))

BIG = (("ffn1_w_gate", True, FF // N_DEV, D), ("ffn1_w_up", True, FF // N_DEV, D),
       ("ffn1_w_down", False, FF // N_DEV, D), ("w_in", True, IN_COLS // N_DEV, D),
       ("w_branch_a", True, D // N_DEV, D_A), ("w_branch_b", True, D // N_DEV, D_B), ("w_out", False, D // N_DEV, D),
       ("ffn2_w_gate", True, FF // N_DEV, D), ("ffn2_w_up", True, FF // N_DEV, D),
       ("ffn2_w_down", False, FF // N_DEV, D))
SMALL = ("b_ada", "g_norm1", "g_norm2", "g_sgu_ln", "b_sgu_ln", "w_spatial", "b_spatial", "g_q", "g_k",
         "attn_sinks", "g_norm3")


def _dot(a, b):
    return jnp.dot(a, b, preferred_element_type=F32)


def _dot_nt(a, b):
    return lax.dot_general(a, b, NT, preferred_element_type=F32)


def _dot_tn(a, b):
    return lax.dot_general(a, b, TN, preferred_element_type=F32)


def _vmem():
    return pl.BlockSpec(memory_space=pltpu.VMEM)


def _any():
    return pl.BlockSpec(memory_space=pl.ANY)


def _whole(a):
    return pl.BlockSpec(a.shape, lambda i: (0,) * len(a.shape))


def _rms_mod(h, g, sh, sc):
    inv = lax.rsqrt(jnp.mean(h * h, axis=-1, keepdims=True) + EPS)
    r = h * inv
    return (r * g) * (1.0 + sc) + sh, r, inv


def _rms_mod_bwd(dxn, r, inv, g, sc):
    dr = dxn * (g * (1.0 + sc))
    dh = inv * (dr - r * jnp.mean(dr * r, axis=-1, keepdims=True))
    return dh, jnp.sum(dxn, axis=0, keepdims=True), jnp.sum(dxn * r, axis=0, keepdims=True)


def _gelu(x):
    t = jnp.tanh(GELU_C * (x + 0.044715 * (x * x * x)))
    return 0.5 * x * (1.0 + t), t


def _gelu_grad(x, t):
    return 0.5 * (1.0 + t) + 0.5 * x * (1.0 - t * t) * (GELU_C * (1.0 + 3.0 * 0.044715 * x * x))


def _adamw(w, g, m, v):
    m = ADAM_B1 * m + (1.0 - ADAM_B1) * g
    v = ADAM_B2 * v + (1.0 - ADAM_B2) * (g * g)
    m_hat = m / (1.0 - ADAM_B1 ** ADAM_STEP)
    v_hat = v / (1.0 - ADAM_B2 ** ADAM_STEP)
    delta = -ADAM_LR * (m_hat / (jnp.sqrt(v_hat) + ADAM_EPS) + ADAM_WD * w)
    return delta, m, v


def _token_tile(seq, cap=512):
    return min(cap, seq)


def _params(*semantics):
    return pltpu.CompilerParams(dimension_semantics=semantics, vmem_limit_bytes=VMEM_LIMIT)


def _place():
    return lax.axis_index("x"), lax.axis_index("y"), lax.axis_index("c")


class _Gather:
    def __init__(self, arrays):
        n = len(arrays)
        self.ins = list(arrays)
        self.out_shape = [jax.ShapeDtypeStruct((N_DEV,) + a.shape, a.dtype) for a in arrays]
        self.scratch = [pltpu.SemaphoreType.DMA((n, 7)), pltpu.SemaphoreType.DMA((n, 7)),
                        pltpu.SemaphoreType.DMA((n,))]

    def _copies(self, ins, outs, sems):
        send_sems, recv_sems, local_sems = sems
        n = len(ins)
        x, y, c = _place()
        me, sibling = (x, y, c), (x, y, 1 - c)
        chips = [(1 - x, y), (x, 1 - y), (1 - x, 1 - y)]

        def slot(a, px, py, pc):
            return outs[a].at[4 * px + 2 * py + pc]

        def copy(a, k, block, to, src=None):
            return pltpu.make_async_remote_copy(
                src_ref=slot(a, *block) if src is None else src, dst_ref=slot(a, *block),
                send_sem=send_sems.at[a, k], recv_sem=recv_sems.at[a, k], device_id=to, device_id_type=MESH)

        mine = [pltpu.make_async_copy(ins[a], slot(a, *me), local_sems.at[a]) for a in range(n)]
        first = [copy(a, 0, me, sibling, src=ins[a]) for a in range(n)]
        first += [copy(a, 1 + j, me, (*chip, c), src=ins[a]) for a in range(n) for j, chip in enumerate(chips)]
        landed = [[copy(a, 1 + j, (*chip, c), me) for a in range(n)] for j, chip in enumerate(chips)]
        passed = [[copy(a, 4 + j, (*chip, c), sibling) for a in range(n)] for j, chip in enumerate(chips)]
        from_sibling = [copy(a, 0, sibling, me) for a in range(n)]
        from_sibling += [copy(a, 4 + j, (*chip, 1 - c), me) for a in range(n) for j, chip in enumerate(chips)]
        return mine, first, landed, passed, from_sibling

    def start(self, ins, outs, sems):
        mine, first, _, _, _ = self._copies(ins, outs, sems)
        for cp in mine + first:
            cp.start()

    def relay(self, ins, outs, sems):
        _, _, landed, passed, _ = self._copies(ins, outs, sems)
        for arrivals, forwards in zip(landed, passed):
            for arrival, forward in zip(arrivals, forwards):
                arrival.wait_recv()
                forward.start()

    def finish(self, ins, outs, sems):
        mine, first, _, passed, from_sibling = self._copies(ins, outs, sems)
        for cp in from_sibling:
            cp.wait_recv()
        for cp in first + [f for fs in passed for f in fs]:
            cp.wait_send()
        for cp in mine:
            cp.wait()


class _RsSibling:
    def __init__(self, ps):
        n = len(ps)
        self.ins = list(ps)
        self.out_shape = [jax.ShapeDtypeStruct((4,) + p.shape[1:], p.dtype) for p in ps]
        self.scratch = [pltpu.SemaphoreType.DMA((n, 4)), pltpu.SemaphoreType.DMA((n, 4))]

    def _copies(self, ins, outs, sems):
        send_sems, recv_sems = sems
        x, y, c = _place()
        return [pltpu.make_async_remote_copy(
            src_ref=ins[a].at[2 * q + (1 - c)], dst_ref=outs[a].at[q], send_sem=send_sems.at[a, q],
            recv_sem=recv_sems.at[a, q], device_id=(x, y, 1 - c), device_id_type=MESH)
            for a in range(len(ins)) for q in range(4)]

    def start(self, ins, outs, sems):
        for cp in self._copies(ins, outs, sems):
            cp.start()

    def relay(self, ins, outs, sems):
        pass

    def finish(self, ins, outs, sems):
        for cp in self._copies(ins, outs, sems):
            cp.wait()


class _RsChips:
    def __init__(self, qs):
        n = len(qs)
        self.ins = list(qs)
        self.out_shape = [jax.ShapeDtypeStruct(q.shape, q.dtype) for q in qs]
        self.scratch = [pltpu.SemaphoreType.DMA((n, 3)), pltpu.SemaphoreType.DMA((n, 3)),
                        pltpu.SemaphoreType.DMA((n,))]

    def _copies(self, ins, outs, sems):
        send_sems, recv_sems, local_sems = sems
        n = len(ins)
        x, y, c = _place()
        my_chip = 2 * x + y
        chips = [(1 - x, y), (x, 1 - y), (1 - x, 1 - y)]

        def copy(a, j, src_slot, dst_slot):
            px, py = chips[j]
            return pltpu.make_async_remote_copy(
                src_ref=ins[a].at[src_slot], dst_ref=outs[a].at[dst_slot], send_sem=send_sems.at[a, j],
                recv_sem=recv_sems.at[a, j], device_id=(px, py, c), device_id_type=MESH)

        own = [pltpu.make_async_copy(ins[a].at[my_chip], outs[a].at[my_chip], local_sems.at[a]) for a in range(n)]
        sends = [copy(a, j, 2 * px + py, my_chip) for a in range(n) for j, (px, py) in enumerate(chips)]
        arrivals = [copy(a, j, my_chip, 2 * px + py) for a in range(n) for j, (px, py) in enumerate(chips)]
        return own, sends, arrivals

    def start(self, ins, outs, sems):
        own, sends, _ = self._copies(ins, outs, sems)
        for cp in own + sends:
            cp.start()

    def relay(self, ins, outs, sems):
        pass

    def finish(self, ins, outs, sems):
        own, sends, arrivals = self._copies(ins, outs, sems)
        for cp in arrivals:
            cp.wait_recv()
        for cp in sends:
            cp.wait_send()
        for cp in own:
            cp.wait()


def _split_plans(plans, refs_in, refs_out, refs_scr, phase):
    i = o = s = 0
    for p in plans:
        ni, no, ns = len(p.ins), len(p.out_shape), len(p.scratch)
        getattr(p, phase)(refs_in[i:i + ni], refs_out[o:o + no], refs_scr[s:s + ns])
        i, o, s = i + ni, o + no, s + ns


def _plan_results(plans, res):
    out = []
    for p in plans:
        out.append(list(res[:len(p.out_shape)]))
        res = res[len(p.out_shape):]
    return out


def _exchange(plans, name):
    c_in = [a for p in plans for a in p.ins]
    c_out = [s for p in plans for s in p.out_shape]
    c_scr = [s for p in plans for s in p.scratch]

    def body(*refs):
        cin, cout, cscr = refs[:len(c_in)], refs[len(c_in):len(c_in) + len(c_out)], refs[len(c_in) + len(c_out):]
        for phase in ("start", "relay", "finish"):
            _split_plans(plans, cin, cout, cscr, phase)

    res = pl.pallas_call(body, name=name, in_specs=[_any()] * len(c_in), out_specs=[_any()] * len(c_out),
                         out_shape=c_out, scratch_shapes=c_scr)(*c_in)
    return _plan_results(plans, res)


def _call(body, *, name, grid, in_specs, out_specs, out_shape, args, semantics, scratch_shapes=(), plans=()):
    n_in, n_out, n_scr = len(in_specs), len(out_specs), len(scratch_shapes)
    c_in = [a for p in plans for a in p.ins]
    c_out = [s for p in plans for s in p.out_shape]
    c_scr = [s for p in plans for s in p.scratch]
    n_steps = math.prod(grid)

    def wrapped(*refs):
        ins, refs = refs[:n_in], refs[n_in:]
        cin, refs = refs[:len(c_in)], refs[len(c_in):]
        outs, refs = refs[:n_out], refs[n_out:]
        cout, refs = refs[:len(c_out)], refs[len(c_out):]
        scr, cscr = refs[:n_scr], refs[n_scr:]
        if plans:
            step = 0
            for d, g in enumerate(grid):
                step = step * g + pl.program_id(d)
            pl.when(step == 0)(lambda: _split_plans(plans, cin, cout, cscr, "start"))
        body(*ins, *outs, *scr)
        if plans:
            pl.when(step == max(n_steps - 2, 0))(lambda: _split_plans(plans, cin, cout, cscr, "relay"))
            pl.when(step == n_steps - 1)(lambda: _split_plans(plans, cin, cout, cscr, "finish"))

    res = pl.pallas_call(
        wrapped, name=name, grid=grid, in_specs=list(in_specs) + [_any()] * len(c_in),
        out_specs=list(out_specs) + [_any()] * len(c_out), out_shape=list(out_shape) + c_out,
        scratch_shapes=list(scratch_shapes) + c_scr,
        compiler_params=_params(*(("arbitrary",) * len(grid) if plans else semantics)),
    )(*args, *c_in)
    return list(res[:n_out]), _plan_results(plans, res[n_out:])


def _ffn_fwd(h, mods, gn, wg, wu, wd, row0, seq, tgt=None, plans=()):
    T = h.shape[0]
    tm = _token_tile(seq, 256)
    tps = seq // tm
    n_t = T // tm
    with_loss = tgt is not None

    def body(h_ref, m_ref, g_ref, wg_ref, wu_ref, wd_ref, *rest):
        if with_loss:
            tgt_ref, out_ref, y_ref, gb_ref, ub_ref, loss_ref = rest
        else:
            out_ref, y_ref, gb_ref, ub_ref = rest
        hv = h_ref[...]
        sh = m_ref[0, row0:row0 + 1, :]
        sc = m_ref[0, row0 + 1:row0 + 2, :]
        ga = m_ref[0, row0 + 2:row0 + 3, :]
        xn, _, _ = _rms_mod(hv, g_ref[...], sh, sc)
        xb = xn.astype(BF16)
        acc = jnp.zeros((tm, D), F32)
        for c0 in range(0, FF, FC):
            gg = _dot_nt(xb, wg_ref[c0:c0 + FC, :])
            uu = _dot_nt(xb, wu_ref[c0:c0 + FC, :])
            gg, uu = gg.astype(BF16), uu.astype(BF16)
            gb_ref[:, c0:c0 + FC] = gg
            ub_ref[:, c0:c0 + FC] = uu
            acc = acc + _dot((gg * jax.nn.sigmoid(gg)) * uu, wd_ref[c0:c0 + FC, :])
        y_ref[...] = acc
        hout = hv + (0.5 * ga) * acc
        if with_loss:
            d = hout - tgt_ref[...]
            out_ref[...] = d * (1.0 / D)
            loss_ref[...] = jnp.full((1, 8, 128), 0.5 / D, F32) * jnp.sum(d * d)
        else:
            out_ref[...] = hout

    tok = pl.BlockSpec((tm, D), lambda i: (i, 0))
    tokf = pl.BlockSpec((tm, FF), lambda i: (i, 0))
    in_specs = [tok, pl.BlockSpec((1, 9, D), lambda i: (i // tps, 0, 0)), pl.BlockSpec((1, D), lambda i: (0, 0)),
                _vmem(), _vmem(), _vmem()]
    out_shape = [jax.ShapeDtypeStruct((T, D), F32), jax.ShapeDtypeStruct((T, D), F32),
                 jax.ShapeDtypeStruct((T, FF), BF16), jax.ShapeDtypeStruct((T, FF), BF16)]
    out_specs = [tok, tok, tokf, tokf]
    args = [h, mods, gn, wg, wu, wd]
    if with_loss:
        in_specs.append(tok)
        args.append(tgt)
        out_shape.append(jax.ShapeDtypeStruct((n_t, 8, 128), F32))
        out_specs.append(pl.BlockSpec((1, 8, 128), lambda i: (i, 0, 0)))
    return _call(body, name="ffn_fwd_loss" if with_loss else "ffn_fwd", grid=(n_t,), in_specs=in_specs,
                 out_specs=out_specs, out_shape=out_shape, args=args, semantics=("parallel",), plans=plans)


def _ffn_gate_up(h, mods, gn, wg, wu, row0, seq, plans=()):
    T = h.shape[0]
    tm = _token_tile(seq, 512)
    tps = seq // tm

    def body(h_ref, m_ref, g_ref, wg_ref, wu_ref, gb_ref, ub_ref):
        xn, _, _ = _rms_mod(h_ref[...], g_ref[...], m_ref[0, row0:row0 + 1, :], m_ref[0, row0 + 1:row0 + 2, :])
        xb = xn.astype(BF16)
        for c0 in range(0, FF, FC):
            gb_ref[:, c0:c0 + FC] = _dot_nt(xb, wg_ref[c0:c0 + FC, :]).astype(BF16)
            ub_ref[:, c0:c0 + FC] = _dot_nt(xb, wu_ref[c0:c0 + FC, :]).astype(BF16)

    tokf = pl.BlockSpec((tm, FF), lambda i: (i, 0))
    return _call(
        body, name="ffn_gate_up", grid=(T // tm,),
        in_specs=[pl.BlockSpec((tm, D), lambda i: (i, 0)), pl.BlockSpec((1, 9, D), lambda i: (i // tps, 0, 0)),
                  pl.BlockSpec((1, D), lambda i: (0, 0)), _vmem(), _vmem()],
        out_specs=[tokf, tokf], out_shape=[jax.ShapeDtypeStruct((T, FF), BF16)] * 2, args=(h, mods, gn, wg, wu),
        semantics=("parallel",), plans=plans)


def _ffn_down(h, gb, ub, mods, wd, row0, seq, plans=()):
    T = h.shape[0]
    tm = _token_tile(seq, 512)
    tps = seq // tm

    def body(h_ref, gb_ref, ub_ref, m_ref, wd_ref, out_ref, y_ref):
        acc = jnp.zeros((tm, D), F32)
        for c0 in range(0, FF, FC):
            gg = gb_ref[:, c0:c0 + FC]
            acc = acc + _dot((gg * jax.nn.sigmoid(gg)) * ub_ref[:, c0:c0 + FC], wd_ref[c0:c0 + FC, :])
        y_ref[...] = acc
        out_ref[...] = h_ref[...] + (0.5 * m_ref[0, row0 + 2:row0 + 3, :]) * acc

    tok = pl.BlockSpec((tm, D), lambda i: (i, 0))
    tokf = pl.BlockSpec((tm, FF), lambda i: (i, 0))
    return _call(
        body, name="ffn_down", grid=(T // tm,),
        in_specs=[tok, tokf, tokf, pl.BlockSpec((1, 9, D), lambda i: (i // tps, 0, 0)), _vmem()],
        out_specs=[tok, tok], out_shape=[jax.ShapeDtypeStruct((T, D), F32)] * 2, args=(h, gb, ub, mods, wd),
        semantics=("parallel",), plans=plans)


def _ffn_bwd(h, dhn, y, gb, ub, mods, gn, wg, wu, wd, row0, seq, plans=()):
    T = h.shape[0]
    B = T // seq
    tm = _token_tile(seq, 256)
    tps = seq // tm
    n_t = T // tm

    def body(h_ref, dhn_ref, y_ref, gb_ref, ub_ref, m_ref, g_ref, wg_ref, wu_ref, wd_ref,
             dh_ref, xb_ref, dyb_ref, a_ref, dg_ref, du_ref, mg_ref):
        i = pl.program_id(0)
        hv = h_ref[...]
        dhn = dhn_ref[...]
        sh = m_ref[0, row0:row0 + 1, :]
        sc = m_ref[0, row0 + 1:row0 + 2, :]
        ga = m_ref[0, row0 + 2:row0 + 3, :]
        g = g_ref[...]
        xn, r, inv = _rms_mod(hv, g, sh, sc)
        xb_ref[...] = xn.astype(BF16)
        dyb = ((0.5 * ga) * dhn).astype(BF16)
        dyb_ref[...] = dyb
        dga = 0.5 * jnp.sum(dhn * y_ref[...], axis=0, keepdims=True)
        dxn = jnp.zeros((tm, D), F32)
        for c0 in range(0, FF, FC):
            wgc = wg_ref[c0:c0 + FC, :]
            wuc = wu_ref[c0:c0 + FC, :]
            gg = gb_ref[:, c0:c0 + FC]
            uu = ub_ref[:, c0:c0 + FC]
            sig = jax.nn.sigmoid(gg)
            s = gg * sig
            a_ref[:, c0:c0 + FC] = s * uu
            da = _dot_nt(dyb, wd_ref[c0:c0 + FC, :]).astype(BF16)
            dub = da * s
            dgb = (da * uu) * (sig * (1.0 + gg * (1.0 - sig)))
            dg_ref[:, c0:c0 + FC] = dgb
            du_ref[:, c0:c0 + FC] = dub
            dxn = dxn + _dot(dgb, wgc) + _dot(dub, wuc)
        dh, s_dxn, s_dxr = _rms_mod_bwd(dxn, r, inv, g, sc)
        dh_ref[...] = dhn + dh

        @pl.when(i % tps == 0)
        def _():
            mg_ref[...] = jnp.zeros(mg_ref.shape, F32)

        mg_ref[0, 0:1, :] += s_dxn
        mg_ref[0, 1:2, :] += s_dxr
        mg_ref[0, 2:3, :] += dga

    tok = pl.BlockSpec((tm, D), lambda i: (i, 0))
    tokf = pl.BlockSpec((tm, FF), lambda i: (i, 0))
    return _call(
        body, name="ffn_bwd", grid=(n_t,),
        in_specs=[tok, tok, tok, tokf, tokf, pl.BlockSpec((1, 9, D), lambda i: (i // tps, 0, 0)),
                  pl.BlockSpec((1, D), lambda i: (0, 0)), _vmem(), _vmem(), _vmem()],
        out_specs=[tok, tok, tok, tokf, tokf, tokf, pl.BlockSpec((1, 8, D), lambda i: (i // tps, 0, 0))],
        out_shape=[jax.ShapeDtypeStruct((T, D), F32), jax.ShapeDtypeStruct((T, D), BF16),
                   jax.ShapeDtypeStruct((T, D), BF16), jax.ShapeDtypeStruct((T, FF), BF16),
                   jax.ShapeDtypeStruct((T, FF), BF16), jax.ShapeDtypeStruct((T, FF), BF16),
                   jax.ShapeDtypeStruct((B, 8, D), F32)],
        args=(h, dhn, y, gb, ub, mods, gn, wg, wu, wd), semantics=("arbitrary",), plans=plans)


def _wgrad(a, b, name, plans=()):
    T, da = a.shape
    db = b.shape[1]
    bm = {2816: 1408, 3840: 1280}[da]
    bn = db
    tk = min(1024, T)
    nk = T // tk

    def body(a_ref, b_ref, o_ref):
        @pl.when(pl.program_id(2) == 0)
        def _():
            o_ref[...] = jnp.zeros(o_ref.shape, F32)

        o_ref[...] += _dot_tn(a_ref[...], b_ref[...])

    (out,), plan_outs = _call(
        body, name=name, grid=(da // bm, db // bn, nk),
        in_specs=[pl.BlockSpec((tk, bm), lambda i, j, k: (k, i)), pl.BlockSpec((tk, bn), lambda i, j, k: (k, j))],
        out_specs=[pl.BlockSpec((bm, bn), lambda i, j, k: (i, j))], out_shape=[jax.ShapeDtypeStruct((da, db), F32)],
        args=(a, b), semantics=("parallel", "parallel", "arbitrary"), plans=plans)
    return out, plan_outs


def _inproj_fwd(h, mods, gn, w_in, seq, plans=()):
    T = h.shape[0]
    tm = _token_tile(seq)
    tps = seq // tm

    def body(h_ref, m_ref, g_ref, w_ref, *outs):
        xn, _, _ = _rms_mod(h_ref[...], g_ref[...], m_ref[0, 3:4, :], m_ref[0, 4:5, :])
        xb = xn.astype(BF16)
        for (_, c0, w), o_ref in zip(PIECES, outs):
            o_ref[...] = _dot_nt(xb, w_ref[c0:c0 + w, :])

    return _call(
        body, name="inproj_fwd", grid=(T // tm,),
        in_specs=[pl.BlockSpec((tm, D), lambda i: (i, 0)), pl.BlockSpec((1, 9, D), lambda i: (i // tps, 0, 0)),
                  pl.BlockSpec((1, D), lambda i: (0, 0)), _vmem()],
        out_specs=[pl.BlockSpec((tm, w), lambda i: (i, 0)) for _, _, w in PIECES],
        out_shape=[jax.ShapeDtypeStruct((T, w), F32) for _, _, w in PIECES], args=(h, mods, gn, w_in),
        semantics=("parallel",), plans=plans)


def _inproj_bwd(h, dh_res, dpieces, mods, gn, w_in, seq, plans=()):
    T = h.shape[0]
    B = T // seq
    tm = _token_tile(seq, 256)
    tps = seq // tm

    def body(h_ref, dres_ref, *rest):
        dp_refs = rest[:len(PIECES)]
        m_ref, g_ref, w_ref, dh_ref, xb_ref, dpb_ref, mg_ref = rest[len(PIECES):]
        i = pl.program_id(0)
        g = g_ref[...]
        sc = m_ref[0, 4:5, :]
        xn, r, inv = _rms_mod(h_ref[...], g, m_ref[0, 3:4, :], sc)
        xb_ref[...] = xn.astype(BF16)
        dxn = jnp.zeros((tm, D), F32)
        for (_, c0, w), dp_ref in zip(PIECES, dp_refs):
            dpb = dp_ref[...].astype(BF16)
            dpb_ref[:, c0:c0 + w] = dpb
            dxn = dxn + _dot(dpb, w_ref[c0:c0 + w, :])
        dh, s_dxn, s_dxr = _rms_mod_bwd(dxn, r, inv, g, sc)
        dh_ref[...] = dres_ref[...] + dh

        @pl.when(i % tps == 0)
        def _():
            mg_ref[...] = jnp.zeros(mg_ref.shape, F32)

        mg_ref[0, 0:1, :] += s_dxn
        mg_ref[0, 1:2, :] += s_dxr

    tok = pl.BlockSpec((tm, D), lambda i: (i, 0))
    return _call(
        body, name="inproj_bwd", grid=(T // tm,),
        in_specs=[tok, tok] + [pl.BlockSpec((tm, w), lambda i: (i, 0)) for _, _, w in PIECES]
        + [pl.BlockSpec((1, 9, D), lambda i: (i // tps, 0, 0)), pl.BlockSpec((1, D), lambda i: (0, 0)), _vmem()],
        out_specs=[tok, tok, pl.BlockSpec((tm, IN_COLS), lambda i: (i, 0)),
                   pl.BlockSpec((1, 8, D), lambda i: (i // tps, 0, 0))],
        out_shape=[jax.ShapeDtypeStruct((T, D), F32), jax.ShapeDtypeStruct((T, D), BF16),
                   jax.ShapeDtypeStruct((T, IN_COLS), BF16), jax.ShapeDtypeStruct((B, 8, D), F32)],
        args=(h, dh_res, *dpieces, mods, gn, w_in), semantics=("arbitrary",), plans=plans)


def _seg_mean(x):
    i = lax.broadcasted_iota(jnp.int32, (128, 128), 0) >> 6
    j = lax.broadcasted_iota(jnp.int32, (128, 128), 1) >> 6
    ones = jnp.where(i == j, 1.0 / HD, 0.0).astype(BF16)
    hi = x.astype(BF16)
    lo = (x - hi.astype(F32)).astype(BF16)
    return _dot(hi, ones) + _dot(lo, ones)


def _head_norm(x, g2):
    inv = lax.rsqrt(_seg_mean(x * x) + EPS)
    y = x * inv
    return y * g2, y, inv


def _head_norm_bwd(dxn, y, inv, g2):
    dy = dxn * g2
    return inv * (dy - y * _seg_mean(dy * y)), jnp.sum(dxn * y, axis=0, keepdims=True)


def _swa_block(q, kk, vv, gq2, gk2, sinks, first, do=None):
    lo = lax.broadcasted_iota(jnp.int32, (1, 128), 1) < HD
    kn, ky, kinv = _head_norm(kk, gk2)

    def operands(x):
        xr = pltpu.roll(x, HD, 1)
        own_lo, own_hi = jnp.where(lo, x, 0.0).astype(BF16), jnp.where(lo, 0.0, x).astype(BF16)
        rolled_lo, rolled_hi = jnp.where(lo, xr, 0.0).astype(BF16), jnp.where(lo, 0.0, xr).astype(BF16)
        return (own_lo, rolled_hi), (rolled_lo, own_hi)

    def restore(parts):
        (own_lo, rolled_hi), (rolled_lo, own_hi) = parts
        return (jnp.where(lo, own_lo, own_hi)
                + pltpu.roll(jnp.where(lo, rolled_lo, rolled_hi), HD, 1))

    k_ops, v_ops = operands(kn), operands(vv)
    k2 = [jnp.concatenate(pair, axis=0) for pair in k_ops]
    def stack(x):
        return jnp.concatenate([x[:, 128 * p:128 * (p + 1)] for p in range(4)], axis=0)

    def unstack(x):
        return jnp.concatenate([x[BLK * p:BLK * (p + 1)] for p in range(4)], axis=1)

    def of_head(x, kh):
        return x[2 * BLK * kh:2 * BLK * (kh + 1)]

    nq = 4 * BLK
    pair = lax.broadcasted_iota(jnp.int32, (nq, 1), 0) >> 7
    row = lax.broadcasted_iota(jnp.int32, (nq, 2 * BLK), 0) & (BLK - 1)
    col = lax.broadcasted_iota(jnp.int32, (nq, 2 * BLK), 1)
    valid = (col <= row + BLK) & (col > row) & (col >= jnp.where(first, BLK, 0))
    scale = HD ** -0.5
    qn, qy, qinv = _head_norm(stack(q), gq2)
    qnb = qn.astype(BF16)
    s2 = jnp.concatenate([_dot_nt(of_head(qnb, kh), k2[kh]) for kh in range(N_KV)], axis=0) * scale
    probs, p_sink = [], []
    for j in range(2):
        s = jnp.where(valid, s2[:, 2 * BLK * j:2 * BLK * (j + 1)], NEG)
        sink = jnp.zeros((nq, 1), F32)
        for p in range(4):
            sink = jnp.where(pair == p, sinks[:, 2 * p + j:2 * p + j + 1], sink)
        m = jnp.maximum(jnp.max(s, axis=-1, keepdims=True), sink)
        e = jnp.exp(s - m)
        e_sink = jnp.exp(sink - m)
        rden = 1.0 / (jnp.sum(e, axis=-1, keepdims=True) + e_sink)
        probs.append(e * rden)
        p_sink.append(e_sink * rden)
    pb = [p.astype(BF16) for p in probs]
    if do is None:
        return unstack(jnp.concatenate(
            [_dot(of_head(pb[0], kh), v_ops[kh][0]) + _dot(of_head(pb[1], kh), v_ops[kh][1]) for kh in range(N_KV)],
            axis=0))
    dob = stack(do).astype(BF16)
    ds, dsinks = [], [None] * 8
    for j in range(2):
        dp = jnp.concatenate([_dot_nt(of_head(dob, kh), v_ops[kh][j]) for kh in range(N_KV)], axis=0)
        t = jnp.sum(dp * probs[j], axis=-1, keepdims=True)
        ds.append(probs[j] * (dp - t) * scale)
        lost = p_sink[j] * t
        for p in range(4):
            dsinks[2 * p + j] = -jnp.sum(lost[BLK * p:BLK * (p + 1)])
    dsb = jnp.concatenate(ds, axis=1).astype(BF16)
    dqn = jnp.concatenate([_dot(of_head(dsb, kh), k2[kh]) for kh in range(N_KV)], axis=0)
    dq, dgq2 = _head_norm_bwd(dqn, qy, qinv, gq2)
    dk2 = [_dot_tn(of_head(dsb, kh), of_head(qnb, kh)) for kh in range(N_KV)]
    dv_ops = [[_dot_tn(of_head(pb[j], kh), of_head(dob, kh)) for j in range(2)] for kh in range(N_KV)]
    dkn = restore(tuple((d[:2 * BLK], d[2 * BLK:]) for d in dk2))
    dvv = restore(tuple(tuple(d) for d in dv_ops))
    dkk, dgk2 = _head_norm_bwd(dkn, ky, kinv, gk2)
    return unstack(dq), dkk, dvv, dgq2, dgk2, dsinks


SWA_GROUP = 8


def _swa_specs(nbs):
    grp = min(SWA_GROUP, nbs)
    rows = grp * BLK

    def tok(w):
        return pl.BlockSpec((rows, w), lambda i: (i, 0))

    halo = pl.BlockSpec((BLK, 128), lambda i: (jnp.maximum(i * grp - 1, 0), 0))
    vec = pl.BlockSpec((1, 128), lambda i: (0, 0))
    sk = pl.BlockSpec((1, 8), lambda i: (0, 0))
    return grp, tok, halo, vec, sk


def _swa_fwd(q, k, v, gq2, gk2, sinks, nbs, plans=()):
    T = q.shape[0]
    grp, tok, halo, vec, sk = _swa_specs(nbs)

    def body(q_ref, kh_ref, kc_ref, vh_ref, vc_ref, gq_ref, gk_ref, sk_ref, o_ref):
        seq_start = ((pl.program_id(0) * grp) % nbs) == 0
        for g in range(grp):
            rows = slice(g * BLK, (g + 1) * BLK)
            prev = slice((g - 1) * BLK, g * BLK)
            kk = jnp.concatenate([kh_ref[...] if g == 0 else kc_ref[prev, :], kc_ref[rows, :]], axis=0)
            vv = jnp.concatenate([vh_ref[...] if g == 0 else vc_ref[prev, :], vc_ref[rows, :]], axis=0)
            o_ref[rows, :] = _swa_block(q_ref[rows, :], kk, vv, gq_ref[...], gk_ref[...], sk_ref[...],
                                        seq_start if g == 0 else False)

    (out,), plan_outs = _call(
        body, name="swa_fwd", grid=(T // (grp * BLK),),
        in_specs=[tok(D_B), halo, tok(128), halo, tok(128), vec, vec, sk], out_specs=[tok(D_B)],
        out_shape=[jax.ShapeDtypeStruct((T, D_B), F32)], args=(q, k, k, v, v, gq2, gk2, sinks),
        semantics=("parallel",), plans=plans)
    return out, plan_outs


def _swa_bwd(q, k, v, gq2, gk2, sinks, do, nbs, plans=()):
    T = q.shape[0]
    grp, tok, halo, vec, sk = _swa_specs(nbs)
    steps = T // (grp * BLK)

    def body(q_ref, kh_ref, kc_ref, vh_ref, vc_ref, gq_ref, gk_ref, sk_ref, do_ref,
             dq_ref, dk_ref, dv_ref, dkh_ref, dvh_ref, dgq_ref, dgk_ref, dsk_ref):
        i = pl.program_id(0)
        seq_start = ((i * grp) % nbs) == 0

        @pl.when(i == 0)
        def _():
            for r in (dgq_ref, dgk_ref, dsk_ref):
                r[...] = jnp.zeros(r.shape, F32)

        res = []
        for g in range(grp):
            rows = slice(g * BLK, (g + 1) * BLK)
            prev = slice((g - 1) * BLK, g * BLK)
            kk = jnp.concatenate([kh_ref[...] if g == 0 else kc_ref[prev, :], kc_ref[rows, :]], axis=0)
            vv = jnp.concatenate([vh_ref[...] if g == 0 else vc_ref[prev, :], vc_ref[rows, :]], axis=0)
            res.append(_swa_block(q_ref[rows, :], kk, vv, gq_ref[...], gk_ref[...], sk_ref[...],
                                  seq_start if g == 0 else False, do=do_ref[rows, :]))
        lane = lax.broadcasted_iota(jnp.int32, (8, 128), 1)
        upd = jnp.zeros((8, 128), F32)
        for g, (dq, dkk, dvv, dgq2, dgk2, dsinks) in enumerate(res):
            rows = slice(g * BLK, (g + 1) * BLK)
            dq_ref[rows, :] = dq
            dk_ref[rows, :] = dkk[BLK:] + res[g + 1][1][:BLK] if g + 1 < grp else dkk[BLK:]
            dv_ref[rows, :] = dvv[BLK:] + res[g + 1][2][:BLK] if g + 1 < grp else dvv[BLK:]
            dgq_ref[...] += dgq2
            dgk_ref[...] += dgk2
            for h, d in enumerate(dsinks):
                upd = upd + jnp.where(lane == h, d, 0.0)
        dkh_ref[0] = res[0][1][:BLK]
        dvh_ref[0] = res[0][2][:BLK]
        dsk_ref[...] += upd

    one = pl.BlockSpec((1, BLK, 128), lambda i: (i, 0, 0))
    halo_shape = jax.ShapeDtypeStruct((steps, BLK, 128), F32)
    return _call(
        body, name="swa_bwd", grid=(steps,),
        in_specs=[tok(D_B), halo, tok(128), halo, tok(128), vec, vec, sk, tok(D_B)],
        out_specs=[tok(D_B), tok(128), tok(128), one, one, vec, vec, pl.BlockSpec((8, 128), lambda i: (0, 0))],
        out_shape=[jax.ShapeDtypeStruct((T, D_B), F32), jax.ShapeDtypeStruct((T, 128), F32),
                   jax.ShapeDtypeStruct((T, 128), F32), halo_shape, halo_shape, jax.ShapeDtypeStruct((1, 128), F32),
                   jax.ShapeDtypeStruct((1, 128), F32), jax.ShapeDtypeStruct((8, 128), F32)],
        args=(q, k, k, v, v, gq2, gk2, sinks, do), semantics=("arbitrary",), plans=plans)


def _swa_add_halo(dk, dk_halo):
    steps = dk_halo.shape[0]
    nxt = jnp.concatenate([dk_halo[1:], jnp.zeros_like(dk_halo[:1])], axis=0)[:, None]
    dk = dk.reshape(steps, -1, BLK, 128)
    return jnp.concatenate([dk[:, :-1], dk[:, -1:] + nxt], axis=1).reshape(-1, 128)


def _sgu_norm(av, g_ln, b_ln):
    t, th = _gelu(av)
    mu = jnp.mean(t, axis=-1, keepdims=True)
    tc = t - mu
    rstd = lax.rsqrt(jnp.mean(tc * tc, axis=-1, keepdims=True) + EPS)
    vhat = tc * rstd
    return vhat * g_ln + b_ln, vhat, rstd, th


def _masked_ws(ws_ref):
    tril = lax.broadcasted_iota(jnp.int32, (BLK, BLK), 0) >= lax.broadcasted_iota(jnp.int32, (BLK, BLK), 1)
    return [jnp.where(tril, ws_ref[g], 0.0).astype(BF16) for g in range(N_GRP)]


def _mix_fwd(au, av, gta, gtb, ob, h, mods, g_ln, b_ln, ws, bsb, wa, wb, wout, seq, plans=()):
    T = h.shape[0]
    tm = _token_tile(seq)
    tps = seq // tm

    def body(au_ref, av_ref, gta_ref, gtb_ref, ob_ref, h_ref, m_ref, gl_ref, bl_ref, ws_ref, bs_ref,
             wa_ref, wb_ref, wo_ref, out_ref, vvb_s, z_s):
        u, _ = _gelu(au_ref[...])
        vv, _, _, _ = _sgu_norm(av_ref[...], gl_ref[...], bl_ref[...])
        vvb_s[...] = vv.astype(BF16)
        wsm = _masked_ws(ws_ref)
        for c in range(tm // BLK):
            rows = slice(c * BLK, (c + 1) * BLK)
            for g in range(N_GRP):
                cols = slice(g * BLK, (g + 1) * BLK)
                z_s[rows, cols] = _dot(wsm[g], vvb_s[rows, cols]) + bs_ref[g]
        ya = _dot_nt((u * z_s[...]).astype(BF16), wa_ref[...])
        yb = _dot_nt(ob_ref[...].astype(BF16), wb_ref[...])
        merged = jax.nn.sigmoid(gta_ref[...]) * ya + jax.nn.sigmoid(gtb_ref[...]) * yb
        out_ref[...] = h_ref[...] + m_ref[0, 5:6, :] * _dot(merged.astype(BF16), wo_ref[...])

    def tok(w):
        return pl.BlockSpec((tm, w), lambda i: (i, 0))

    def full(shape):
        return pl.BlockSpec(shape, lambda i: (0,) * len(shape))

    (out,), plan_outs = _call(
        body, name="mix_fwd", grid=(T // tm,),
        in_specs=[tok(D_A), tok(D_A), tok(D), tok(D), tok(D_B), tok(D),
                  pl.BlockSpec((1, 9, D), lambda i: (i // tps, 0, 0)), full((1, D_A)), full((1, D_A)),
                  full((N_GRP, BLK, BLK)), full((N_GRP, BLK, BLK)), _vmem(), _vmem(), _vmem()],
        out_specs=[tok(D)], out_shape=[jax.ShapeDtypeStruct((T, D), F32)],
        scratch_shapes=[pltpu.VMEM((tm, D_A), BF16), pltpu.VMEM((tm, D_A), F32)],
        args=(au, av, gta, gtb, ob, h, mods, g_ln, b_ln, ws, bsb, wa, wb, wout), semantics=("parallel",),
        plans=plans)
    return out, plan_outs


def _mix_bwd(au, av, gta, gtb, ob, dh, mods, g_ln, b_ln, ws, bsb, wa, wb, wout, seq, plans=()):
    T = dh.shape[0]
    B = T // seq
    tm = _token_tile(seq, 256)
    tps = seq // tm

    def body(au_ref, av_ref, gta_ref, gtb_ref, ob_ref, dh_ref, m_ref, gl_ref, bl_ref, ws_ref, bs_ref,
             wa_ref, wb_ref, wo_ref,
             dau_ref, dav_ref, dgta_ref, dgtb_ref, dob_ref, dwo_ref, dwa_ref, dwb_ref, dws_ref, dbs_ref, dln_ref,
             mg_ref, vvb_s, z_s, dz_s, dzb_s, dvv_s):
        i = pl.program_id(0)

        @pl.when(i == 0)
        def _():
            for r in (dwo_ref, dwa_ref, dwb_ref, dws_ref, dbs_ref, dln_ref):
                r[...] = jnp.zeros(r.shape, F32)

        @pl.when(i % tps == 0)
        def _():
            mg_ref[...] = jnp.zeros(mg_ref.shape, F32)

        auv = au_ref[...]
        avv = av_ref[...]
        u, thu = _gelu(auv)
        g_ln = gl_ref[...]
        vv, vhat, rstd, thv = _sgu_norm(avv, g_ln, bl_ref[...])
        vvb_s[...] = vv.astype(BF16)
        wsm = _masked_ws(ws_ref)
        for c in range(tm // BLK):
            rows = slice(c * BLK, (c + 1) * BLK)
            for g in range(N_GRP):
                cols = slice(g * BLK, (g + 1) * BLK)
                z_s[rows, cols] = _dot(wsm[g], vvb_s[rows, cols]) + bs_ref[g]
        z = z_s[...]
        yab = (u * z).astype(BF16)
        obb = ob_ref[...].astype(BF16)
        ya = _dot_nt(yab, wa_ref[...])
        yb = _dot_nt(obb, wb_ref[...])
        sa = jax.nn.sigmoid(gta_ref[...])
        sb = jax.nn.sigmoid(gtb_ref[...])
        mb = (sa * ya + sb * yb).astype(BF16)
        dhv = dh_ref[...]
        mg_ref[0, 0:1, :] += jnp.sum(dhv * _dot(mb, wo_ref[...]), axis=0, keepdims=True)
        dmob = (m_ref[0, 5:6, :] * dhv).astype(BF16)
        dwo_ref[...] += _dot_tn(mb, dmob)
        dmerged = _dot_nt(dmob, wo_ref[...])
        dya = dmerged * sa
        dyb = dmerged * sb
        dgta_ref[...] = dya * ya * (1.0 - sa)
        dgtb_ref[...] = dyb * yb * (1.0 - sb)
        dyab = dya.astype(BF16)
        dybb = dyb.astype(BF16)
        dwa_ref[...] += _dot_tn(dyab, yab)
        dwb_ref[...] += _dot_tn(dybb, obb)
        dob_ref[...] = _dot(dybb, wb_ref[...])
        dyap = _dot(dyab, wa_ref[...])
        dau_ref[...] = (dyap * z) * _gelu_grad(auv, thu)
        dz = dyap * u
        dz_s[...] = dz
        dzb_s[...] = dz.astype(BF16)
        for c in range(tm // BLK):
            rows = slice(c * BLK, (c + 1) * BLK)
            for g in range(N_GRP):
                cols = slice(g * BLK, (g + 1) * BLK)
                dzb = dzb_s[rows, cols]
                dvv_s[rows, cols] = _dot_tn(wsm[g], dzb)
                dws_ref[g] += _dot_nt(dzb, vvb_s[rows, cols])
                dbs_ref[g] += dz_s[rows, cols]
        dvv = dvv_s[...]
        dln_ref[0:1, :] += jnp.sum(dvv * vhat, axis=0, keepdims=True)
        dln_ref[1:2, :] += jnp.sum(dvv, axis=0, keepdims=True)
        dvh = dvv * g_ln
        dt = rstd * (dvh - jnp.mean(dvh, axis=-1, keepdims=True)
                     - vhat * jnp.mean(dvh * vhat, axis=-1, keepdims=True))
        dav_ref[...] = dt * _gelu_grad(avv, thv)

    def tok(w):
        return pl.BlockSpec((tm, w), lambda i: (i, 0))

    def full(shape):
        return pl.BlockSpec(shape, lambda i: (0,) * len(shape))

    return _call(
        body, name="mix_bwd", grid=(T // tm,),
        in_specs=[tok(D_A), tok(D_A), tok(D), tok(D), tok(D_B), tok(D),
                  pl.BlockSpec((1, 9, D), lambda i: (i // tps, 0, 0)), full((1, D_A)), full((1, D_A)),
                  full((N_GRP, BLK, BLK)), full((N_GRP, BLK, BLK)), _vmem(), _vmem(), _vmem()],
        out_specs=[tok(D_A), tok(D_A), tok(D), tok(D), tok(D_B), full((D, D)), full((D, D_A)), full((D, D_B)),
                   full((N_GRP, BLK, BLK)), full((N_GRP, BLK, BLK)), full((8, D_A)),
                   pl.BlockSpec((1, 8, D), lambda i: (i // tps, 0, 0))],
        out_shape=[jax.ShapeDtypeStruct((T, D_A), F32), jax.ShapeDtypeStruct((T, D_A), F32),
                   jax.ShapeDtypeStruct((T, D), F32), jax.ShapeDtypeStruct((T, D), F32),
                   jax.ShapeDtypeStruct((T, D_B), F32), jax.ShapeDtypeStruct((D, D), F32),
                   jax.ShapeDtypeStruct((D, D_A), F32), jax.ShapeDtypeStruct((D, D_B), F32),
                   jax.ShapeDtypeStruct((N_GRP, BLK, BLK), F32), jax.ShapeDtypeStruct((N_GRP, BLK, BLK), F32),
                   jax.ShapeDtypeStruct((8, D_A), F32), jax.ShapeDtypeStruct((B, 8, D), F32)],
        scratch_shapes=[pltpu.VMEM((tm, D_A), BF16), pltpu.VMEM((tm, D_A), F32), pltpu.VMEM((tm, D_A), F32),
                        pltpu.VMEM((tm, D_A), BF16), pltpu.VMEM((tm, D_A), F32)],
        args=(au, av, gta, gtb, ob, dh, mods, g_ln, b_ln, ws, bsb, wa, wb, wout), semantics=("arbitrary",),
        plans=plans)


def _prologue(c_pad, w_ada, first_shards):
    cols = w_ada.shape[1]
    plan_w, plan_c = _Gather(first_shards), _Gather([c_pad])
    plan_m = _Gather([jax.ShapeDtypeStruct((N_DEV * 8, cols), F32)])
    n_w = len(first_shards)

    def body(c_ref, wada_ref, *rest):
        w_ins, rest = rest[:n_w], rest[n_w:]
        call_ref, mods_ref = rest[:2]
        w_outs, rest = rest[2:2 + n_w], rest[2 + n_w:]
        cvm, part, local_sem = rest[:3]
        sems = rest[3:]
        sems_w, sems_c, sems_m = sems[:3], sems[3:6], sems[6:9]
        plan_c.start([c_ref], [call_ref], sems_c)
        plan_c.relay([c_ref], [call_ref], sems_c)
        plan_c.finish([c_ref], [call_ref], sems_c)
        load = pltpu.make_async_copy(call_ref, cvm, local_sem)
        load.start()
        load.wait()
        cv = cvm[...].reshape(N_DEV * 8, D)
        part[...] = _dot((cv * jax.nn.sigmoid(cv)).astype(BF16), wada_ref[...].astype(BF16))
        plan_m.start([part], [mods_ref], sems_m)
        plan_w.start(w_ins, w_outs, sems_w)
        plan_m.relay([part], [mods_ref], sems_m)
        plan_m.finish([part], [mods_ref], sems_m)
        plan_w.relay(w_ins, w_outs, sems_w)
        plan_w.finish(w_ins, w_outs, sems_w)

    res = pl.pallas_call(
        body, name="prologue", in_specs=[_any(), _vmem()] + [_any()] * n_w,
        out_specs=[_any()] * (2 + n_w), out_shape=plan_c.out_shape + plan_m.out_shape + plan_w.out_shape,
        scratch_shapes=[pltpu.VMEM((N_DEV, 8, D), F32), pltpu.VMEM((N_DEV * 8, cols), F32), pltpu.SemaphoreType.DMA]
        + plan_w.scratch + plan_c.scratch + plan_m.scratch,
        compiler_params=pltpu.CompilerParams(vmem_limit_bytes=VMEM_LIMIT),
    )(c_pad, w_ada, *first_shards)
    return res[0], res[1], list(res[2:])


def _ada_update(c_all, dm_cols, w, m, v, plans=()):
    n, cols = c_all.shape[0], w.shape[1]

    def body(c_ref, dm_ref, w_ref, m_ref, v_ref, g_ref, d_ref, nm_ref, nv_ref):
        cv = c_ref[...]
        g = _dot_tn((cv * jax.nn.sigmoid(cv)).astype(BF16), dm_ref[...].astype(BF16))
        g_ref[...] = g
        d_ref[...], nm_ref[...], nv_ref[...] = _adamw(w_ref[...], g, m_ref[...], v_ref[...])

    col = pl.BlockSpec((D, 128), lambda j: (0, j))
    return _call(
        body, name="ada_update", grid=(cols // 128,),
        in_specs=[pl.BlockSpec((n, D), lambda j: (0, 0)), pl.BlockSpec((n, 128), lambda j: (0, j)), col, col, col],
        out_specs=[col] * 4, out_shape=[jax.ShapeDtypeStruct(w.shape, F32)] * 4, args=(c_all, dm_cols, w, m, v),
        semantics=("parallel",), plans=plans)


def _mod_finish(mg1, mg2, mg2g, mg3, mods, g1, g2, g3):
    B = mods.shape[0]

    def body(mg1_ref, mg2_ref, mg2g_ref, mg3_ref, m_ref, g1_ref, g2_ref, g3_ref, dm_ref, dgn_ref):
        dgn_ref[...] = jnp.zeros(dgn_ref.shape, F32)
        for k, (mg, g_ref) in enumerate(((mg1_ref, g1_ref), (mg2_ref, g2_ref), (mg3_ref, g3_ref))):
            for b in range(B):
                s_dxr = mg[b, 1:2, :]
                dm_ref[b, 3 * k:3 * k + 1, :] = mg[b, 0:1, :]
                dm_ref[b, 3 * k + 1:3 * k + 2, :] = g_ref[...] * s_dxr
                dm_ref[b, 3 * k + 2:3 * k + 3, :] = mg2g_ref[b, 0:1, :] if k == 1 else mg[b, 2:3, :]
                dgn_ref[k:k + 1, :] += (1.0 + m_ref[b, 3 * k + 1:3 * k + 2, :]) * s_dxr

    args = (mg1, mg2, mg2g, mg3, mods, g1, g2, g3)
    out_shape = [jax.ShapeDtypeStruct((B, 9, D), F32), jax.ShapeDtypeStruct((8, D), F32)]
    return pl.pallas_call(body, name="mod_finish", grid=(1,), in_specs=[_whole(a) for a in args],
                          out_specs=[_whole(o) for o in out_shape], out_shape=out_shape,
                          compiler_params=_params("arbitrary"))(*args)


def _small_update(gathered, params, ms, vs):
    n = len(SMALL)
    B = gathered[0].shape[1]

    def body(*refs):
        gdm, ggn, gln, gws, gbs, ggq, ggk, gsk, gloss = refs[:9]
        w = dict(zip(SMALL, refs[9:9 + n]))
        m = dict(zip(SMALL, refs[9 + n:9 + 2 * n]))
        v = dict(zip(SMALL, refs[9 + 2 * n:9 + 3 * n]))
        outs = refs[9 + 3 * n:-1]
        out = {name: outs[4 * k:4 * k + 4] for k, name in enumerate(SMALL)}

        def total(ref, idx):
            acc = ref[(0,) + idx]
            for dev in range(1, N_DEV):
                acc = acc + ref[(dev,) + idx]
            return acc

        def finish(name, g, idx=(Ellipsis,)):
            d, nm, nv = _adamw(w[name][idx], g, m[name][idx], v[name][idx])
            for ref, val in zip(out[name], (g, d, nm, nv)):
                ref[idx] = val

        g_bada = total(gdm, (slice(0, 1),))
        for b in range(1, B):
            g_bada = g_bada + total(gdm, (slice(b, b + 1),))
        finish("b_ada", g_bada)
        finish("g_norm1", total(ggn, (slice(0, 1),)))
        finish("g_norm2", total(ggn, (slice(1, 2),)))
        finish("g_norm3", total(ggn, (slice(2, 3),)))
        finish("g_sgu_ln", total(gln, (slice(0, 1),)))
        finish("b_sgu_ln", total(gln, (slice(1, 2),)))
        tril = lax.broadcasted_iota(jnp.int32, (BLK, BLK), 0) >= lax.broadcasted_iota(jnp.int32, (BLK, BLK), 1)
        for g in range(N_GRP):
            finish("w_spatial", jnp.where(tril, total(gws, (g,)), 0.0), (0, g))
            finish("b_spatial", jnp.sum(total(gbs, (g,)).T, axis=0, keepdims=True), (0, slice(g, g + 1)))
        finish("g_q", total(ggq, ()))
        finish("g_k", total(ggk, ()))
        finish("attn_sinks", total(gsk, (slice(0, 1), slice(0, N_KV * Q_PER_KV))))
        refs[-1][...] = total(gloss, ())

    args = list(gathered) + [params[k] for k in SMALL] + [ms[k] for k in SMALL] + [vs[k] for k in SMALL]
    out_shape = []
    for k in SMALL:
        out_shape += [jax.ShapeDtypeStruct(params[k].shape, F32)] * 4
    out_shape.append(jax.ShapeDtypeStruct((8, 128), F32))
    res = pl.pallas_call(body, name="small_update", grid=(1,), in_specs=[_whole(a) for a in args],
                         out_specs=[_whole(o) for o in out_shape], out_shape=out_shape,
                         compiler_params=_params("arbitrary"))(*args)
    return {k: res[4 * i:4 * i + 4] for i, k in enumerate(SMALL)}, res[-1][0, 0]


def _rs_add(p, r, core, name):
    _, rows, width = p.shape

    def body(c_ref, p_ref, r_ref, o_ref):
        o_ref[...] = (p_ref[...] + r_ref[...]).astype(BF16)

    return pl.pallas_call(
        body, name="rs_add_" + name, out_shape=jax.ShapeDtypeStruct((4, rows, width), BF16),
        grid_spec=pltpu.PrefetchScalarGridSpec(
            num_scalar_prefetch=1, grid=(4,),
            in_specs=[pl.BlockSpec((1, rows, width), lambda k, c_ref: (2 * k + c_ref[0], 0, 0)),
                      pl.BlockSpec((1, rows, width), lambda k, c_ref: (k, 0, 0))],
            out_specs=pl.BlockSpec((1, rows, width), lambda k, c_ref: (k, 0, 0))),
        compiler_params=_params("parallel"),
    )(core, p, r)


def _adam_big(w, m, v, r, name):
    rows, cols = w.shape

    def body(r_ref, w_ref, m_ref, v_ref, go_ref, d_ref, nm_ref, nv_ref):
        g = r_ref[0].astype(F32)
        for k in range(1, 4):
            g = g + r_ref[k].astype(F32)
        go_ref[...] = g
        d_ref[...], nm_ref[...], nv_ref[...] = _adamw(w_ref[...], g, m_ref[...], v_ref[...])

    br = rows // 2
    shard = pl.BlockSpec((br, cols), lambda i: (i, 0))
    return pl.pallas_call(body, name="adam_" + name, grid=(2,),
                          in_specs=[pl.BlockSpec((4, br, cols), lambda i: (0, i, 0)), shard, shard, shard],
                          out_specs=[shard] * 4, out_shape=[jax.ShapeDtypeStruct(w.shape, F32)] * 4,
                          compiler_params=_params("parallel"))(r, w, m, v)


def kernel(x, c, w_ada, b_ada, g_norm1, ffn1_w_gate, ffn1_w_up, ffn1_w_down, g_norm2, w_in, g_sgu_ln, b_sgu_ln, w_spatial, b_spatial, g_q, g_k, attn_sinks, w_branch_a, w_branch_b, w_out, g_norm3, ffn2_w_gate, ffn2_w_up, ffn2_w_down, loss_target, m_w_ada, m_b_ada, m_g_norm1, m_ffn1_w_gate, m_ffn1_w_up, m_ffn1_w_down, m_g_norm2, m_w_in, m_g_sgu_ln, m_b_sgu_ln, m_w_spatial, m_b_spatial, m_g_q, m_g_k, m_attn_sinks, m_w_branch_a, m_w_branch_b, m_w_out, m_g_norm3, m_ffn2_w_gate, m_ffn2_w_up, m_ffn2_w_down, v_w_ada, v_b_ada, v_g_norm1, v_ffn1_w_gate, v_ffn1_w_up, v_ffn1_w_down, v_g_norm2, v_w_in, v_g_sgu_ln, v_b_sgu_ln, v_w_spatial, v_b_spatial, v_g_q, v_g_k, v_attn_sinks, v_w_branch_a, v_w_branch_b, v_w_out, v_g_norm3, v_ffn2_w_gate, v_ffn2_w_up, v_ffn2_w_down):
    names = ("w_ada", "b_ada", "g_norm1", "ffn1_w_gate", "ffn1_w_up", "ffn1_w_down", "g_norm2", "w_in", "g_sgu_ln",
             "b_sgu_ln", "w_spatial", "b_spatial", "g_q", "g_k", "attn_sinks", "w_branch_a", "w_branch_b", "w_out",
             "g_norm3", "ffn2_w_gate", "ffn2_w_up", "ffn2_w_down")
    w = dict(zip(names, (w_ada, b_ada, g_norm1, ffn1_w_gate, ffn1_w_up, ffn1_w_down, g_norm2, w_in, g_sgu_ln,
                         b_sgu_ln, w_spatial, b_spatial, g_q, g_k, attn_sinks, w_branch_a, w_branch_b, w_out, g_norm3,
                         ffn2_w_gate, ffn2_w_up, ffn2_w_down)))
    m = dict(zip(names, (m_w_ada, m_b_ada, m_g_norm1, m_ffn1_w_gate, m_ffn1_w_up, m_ffn1_w_down, m_g_norm2, m_w_in,
                         m_g_sgu_ln, m_b_sgu_ln, m_w_spatial, m_b_spatial, m_g_q, m_g_k, m_attn_sinks, m_w_branch_a,
                         m_w_branch_b, m_w_out, m_g_norm3, m_ffn2_w_gate, m_ffn2_w_up, m_ffn2_w_down)))
    v = dict(zip(names, (v_w_ada, v_b_ada, v_g_norm1, v_ffn1_w_gate, v_ffn1_w_up, v_ffn1_w_down, v_g_norm2, v_w_in,
                         v_g_sgu_ln, v_b_sgu_ln, v_w_spatial, v_b_spatial, v_g_q, v_g_k, v_attn_sinks, v_w_branch_a,
                         v_w_branch_b, v_w_out, v_g_norm3, v_ffn2_w_gate, v_ffn2_w_up, v_ffn2_w_down)))
    B, seq, _ = x.shape
    T = B * seq
    nbs = seq // BLK
    xi, yi, ci = _place()
    me = 4 * xi + 2 * yi + ci
    core = jnp.reshape(ci, (1,)).astype(jnp.int32)
    layout = {name: (tform, n, width) for name, tform, n, width in BIG}
    shard = {name: w[name][0].astype(BF16).T if tform else w[name][0].astype(BF16) for name, tform, _, _ in BIG}
    wts, parts, big_out = {}, {}, {}

    def gather_plan(group):
        return _Gather([shard[k] for k in group])

    def take(group, gathered):
        for k, g in zip(group, gathered):
            wts[k] = g.reshape(N_DEV * layout[k][1], layout[k][2])

    def blocks(name, grad):
        return grad.reshape(N_DEV, layout[name][1], layout[name][2])

    def to_sibling(names, grads):
        return _RsSibling([blocks(k, g) for k, g in zip(names, grads)])

    def add(names, grads, from_sibling):
        for k, g, r in zip(names, grads, from_sibling):
            parts[k] = _rs_add(blocks(k, g), r, core, k)

    def to_chips(names):
        return _RsChips([parts[k] for k in names])

    def update(names, from_chips):
        for k, r in zip(names, from_chips):
            if layout[k][0]:
                big_out[k] = [o.T[None] for o in _adam_big(w[k][0].T, m[k][0].T, v[k][0].T, r, k)]
            else:
                big_out[k] = [o[None] for o in _adam_big(w[k][0], m[k][0], v[k][0], r, k)]

    g1, g2, g3, g_ln, b_ln = g_norm1, g_norm2, g_norm3, g_sgu_ln, b_sgu_ln
    gq2, gk2 = jnp.tile(g_q, (1, 2)), jnp.tile(g_k, (1, 2))
    ws = w_spatial[0]
    bsb = jnp.broadcast_to(b_spatial[0][:, :, None], (N_GRP, BLK, BLK))
    xf = x.reshape(T, D)
    tgt = loss_target.reshape(T, D)

    ffn1 = ("ffn1_w_gate", "ffn1_w_up", "ffn1_w_down")
    ffn2 = ("ffn2_w_gate", "ffn2_w_up", "ffn2_w_down")
    c_pad = jnp.concatenate([c, jnp.zeros((8 - B, D), F32)], axis=0)
    c_all, mods_cols, gathered = _prologue(c_pad, w_ada[0], [shard[k] for k in ffn1[:2]])
    take(ffn1[:2], gathered)
    c_all = c_all[:, :B].reshape(N_DEV * B, D)
    mine = lax.dynamic_slice_in_dim(mods_cols, 8 * me, B, axis=1)
    mods = (mine.transpose(1, 0, 2).reshape(B, 9 * D) + b_ada).reshape(B, 9, D)

    group = ("ffn1_w_down", "w_in")
    (gb1, ub1), (gathered,) = _ffn_gate_up(xf, mods, g1, wts["ffn1_w_gate"], wts["ffn1_w_up"], 0, seq,
                                           plans=[gather_plan(group)])
    take(group, gathered)
    group = ("w_branch_a", "w_branch_b", "w_out")
    (h1, y1), (gathered,) = _ffn_down(xf, gb1, ub1, mods, wts["ffn1_w_down"], 0, seq, plans=[gather_plan(group)])
    take(group, gathered)
    group = ("ffn2_w_gate",)
    (au, av, q_tok, k_tok, v_tok, gta, gtb), (gathered,) = _inproj_fwd(h1, mods, g2, wts["w_in"], seq,
                                                                       plans=[gather_plan(group)])
    take(group, gathered)
    group = ("ffn2_w_up",)
    ob, (gathered,) = _swa_fwd(q_tok, k_tok, v_tok, gq2, gk2, attn_sinks, nbs, plans=[gather_plan(group)])
    take(group, gathered)
    mixw = (wts["w_branch_a"], wts["w_branch_b"], wts["w_out"])
    group = ("ffn2_w_down",)
    h2, (gathered,) = _mix_fwd(au, av, gta, gtb, ob, h1, mods, g_ln, b_ln, ws, bsb, *mixw, seq,
                               plans=[gather_plan(group)])
    take(group, gathered)
    (dh3, y3, gb3, ub3, loss_parts), _ = _ffn_fwd(h2, mods, g3, *[wts[k] for k in ffn2], 6, seq, tgt=tgt)
    loss_part = jnp.full((8, 128), jnp.sum(loss_parts[:, 0, 0]))

    (dh2, xb, dyb, a, dg, du, mg3), _ = _ffn_bwd(h2, dh3, y3, gb3, ub3, mods, g3, *[wts[k] for k in ffn2], 6, seq)
    d_gate, _ = _wgrad(dg, xb, "wgrad_ffn2_gate")
    d_up, (r,) = _wgrad(du, xb, "wgrad_ffn2_up", plans=[to_sibling(ffn2[:1], [d_gate])])
    add(ffn2[:1], [d_gate], r)
    d_down, (r,) = _wgrad(a, dyb, "wgrad_ffn2_down", plans=[to_sibling(ffn2[1:2], [d_up])])
    add(ffn2[1:2], [d_up], r)

    (dau, dav, dgta, dgtb, dob, d_out, d_a, d_b, dws, dbs, dln, mg2g), (r, from_chips) = _mix_bwd(
        au, av, gta, gtb, ob, dh2, mods, g_ln, b_ln, ws, bsb, *mixw, seq,
        plans=[to_sibling(ffn2[2:], [d_down]), to_chips(ffn2[:2])])
    add(ffn2[2:], [d_down], r)
    update(ffn2[:2], from_chips)
    mixers = ("w_out", "w_branch_a", "w_branch_b")
    (dq, dk, dv, dk_halo, dv_halo, dgq2, dgk2, dsk), (from_chips, r) = _swa_bwd(
        q_tok, k_tok, v_tok, gq2, gk2, attn_sinks, dob, nbs,
        plans=[to_chips(ffn2[2:]), to_sibling(mixers, [d_out, d_a, d_b])])
    update(ffn2[2:], from_chips)
    add(mixers, [d_out, d_a, d_b], r)
    dk, dv = _swa_add_halo(dk, dk_halo), _swa_add_halo(dv, dv_halo)
    dgq, dgk = dgq2[:, :HD] + dgq2[:, HD:], dgk2[:, :HD] + dgk2[:, HD:]
    (dh1, xb2, dpb, mg2), (from_chips, early) = _inproj_bwd(
        h1, dh2, (dau, dav, dq, dk, dv, dgta, dgtb), mods, g2, wts["w_in"], seq,
        plans=[to_chips(mixers), _Gather([dln, dws, dbs, dgq, dgk, dsk, loss_part])])
    update(mixers, from_chips)

    (dx, xb, dyb, a, dg, du, mg1), _ = _ffn_bwd(xf, dh1, y1, gb1, ub1, mods, g1, *[wts[k] for k in ffn1], 0, seq)
    dmods, dgn = _mod_finish(mg1, mg2, mg2g, mg3, mods, g1, g2, g3)
    d_gate, (late,) = _wgrad(dg, xb, "wgrad_ffn1_gate", plans=[_Gather([dmods.reshape(B, 9 * D), dgn])])
    gathered = late + early
    d_up, (r,) = _wgrad(du, xb, "wgrad_ffn1_up", plans=[to_sibling(ffn1[:1], [d_gate])])
    add(ffn1[:1], [d_gate], r)
    d_in, (r, from_chips) = _wgrad(dpb, xb2, "wgrad_w_in",
                                   plans=[to_sibling(ffn1[1:2], [d_up]), to_chips(ffn1[:1])])
    add(ffn1[1:2], [d_up], r)
    update(ffn1[:1], from_chips)
    d_down, (r, from_chips) = _wgrad(a, dyb, "wgrad_ffn1_down",
                                     plans=[to_sibling(("w_in",), [d_in]), to_chips(ffn1[1:2])])
    add(("w_in",), [d_in], r)
    update(ffn1[1:2], from_chips)
    small_out, loss = _small_update(gathered, w, m, v)
    dm_cols = lax.dynamic_slice_in_dim(gathered[0].reshape(N_DEV * B, 9 * D), (9 * D // N_DEV) * me,
                                       9 * D // N_DEV, axis=1)
    ada_out, (r, from_chips) = _ada_update(c_all, dm_cols, w_ada[0], m_w_ada[0], v_w_ada[0],
                                           plans=[to_sibling(ffn1[2:], [d_down]), to_chips(("w_in",))])
    add(ffn1[2:], [d_down], r)
    update(("w_in",), from_chips)
    (from_chips,) = _exchange([to_chips(ffn1[2:])], "rs_last")
    update(ffn1[2:], from_chips)

    def leaf(kind, name):
        if name == "w_ada":
            return ada_out[kind][None]
        if name in SMALL:
            return small_out[name][kind]
        return big_out[name][kind]

    return (loss, dx.reshape(B, seq, D), *[leaf(kind, name) for kind in range(4) for name in names])
```

```python
import functools
import math

import jax
import jax.numpy as jnp
from jax import lax
from jax.experimental import pallas as pl
from jax.experimental.pallas import tpu as pltpu

F32 = jnp.float32
BF16 = jnp.bfloat16
MESH = pl.DeviceIdType.MESH
AXES = ("x", "y", "c")
N_DEV = 8

VMEM_LIMIT = 56 * 1024 * 1024

D = 1024
FF = 2816
FC = 1408
D_A = 512
D_B = 512
HD = 64
N_KV = 2
Q_PER_KV = 4
BLK = 128
N_GRP = 4
IN_COLS = 3840
PIECES = (("au", 0, 512), ("av", 512, 512), ("q", 1024, 512), ("k", 1536, 128), ("v", 1664, 128),
          ("ga", 1792, 1024), ("gb", 2816, 1024))
EPS = 1e-6
NEG = -1e30
GELU_C = math.sqrt(2.0 / math.pi)

ADAM_LR = 0.001
ADAM_B1 = 0.9
ADAM_B2 = 0.999
ADAM_EPS = 1e-08
ADAM_WD = 0.01
ADAM_STEP = 10

NT = (((1,), (1,)), ((), ()))
TN = (((0,), (0,)), ((), ()))

BIG = (("ffn1_w_gate", True, FF // N_DEV, D), ("ffn1_w_up", True, FF // N_DEV, D),
       ("ffn1_w_down", False, FF // N_DEV, D), ("w_in", True, IN_COLS // N_DEV, D),
       ("w_branch_a", True, D // N_DEV, D_A), ("w_branch_b", True, D // N_DEV, D_B), ("w_out", False, D // N_DEV, D),
       ("ffn2_w_gate", True, FF // N_DEV, D), ("ffn2_w_up", True, FF // N_DEV, D),
       ("ffn2_w_down", False, FF // N_DEV, D))
SMALL = ("b_ada", "g_norm1", "g_norm2", "g_sgu_ln", "b_sgu_ln", "w_spatial", "b_spatial", "g_q", "g_k",
         "attn_sinks", "g_norm3")


def _dot(a, b):
    return jnp.dot(a, b, preferred_element_type=F32)


def _dot_nt(a, b):
    return lax.dot_general(a, b, NT, preferred_element_type=F32)


def _dot_tn(a, b):
    return lax.dot_general(a, b, TN, preferred_element_type=F32)


def _vmem():
    return pl.BlockSpec(memory_space=pltpu.VMEM)


def _any():
    return pl.BlockSpec(memory_space=pl.ANY)


def _whole(a):
    return pl.BlockSpec(a.shape, lambda i: (0,) * len(a.shape))


def _rms_mod(h, g, sh, sc):
    inv = lax.rsqrt(jnp.mean(h * h, axis=-1, keepdims=True) + EPS)
    r = h * inv
    return (r * g) * (1.0 + sc) + sh, r, inv


def _rms_mod_bwd(dxn, r, inv, g, sc):
    dr = dxn * (g * (1.0 + sc))
    dh = inv * (dr - r * jnp.mean(dr * r, axis=-1, keepdims=True))
    return dh, jnp.sum(dxn, axis=0, keepdims=True), jnp.sum(dxn * r, axis=0, keepdims=True)


def _gelu(x):
    t = jnp.tanh(GELU_C * (x + 0.044715 * (x * x * x)))
    return 0.5 * x * (1.0 + t), t


def _gelu_grad(x, t):
    return 0.5 * (1.0 + t) + 0.5 * x * (1.0 - t * t) * (GELU_C * (1.0 + 3.0 * 0.044715 * x * x))


def _adamw(w, g, m, v):
    m = ADAM_B1 * m + (1.0 - ADAM_B1) * g
    v = ADAM_B2 * v + (1.0 - ADAM_B2) * (g * g)
    m_hat = m / (1.0 - ADAM_B1 ** ADAM_STEP)
    v_hat = v / (1.0 - ADAM_B2 ** ADAM_STEP)
    delta = -ADAM_LR * (m_hat / (jnp.sqrt(v_hat) + ADAM_EPS) + ADAM_WD * w)
    return delta, m, v


def _token_tile(seq, cap=512):
    return min(cap, seq)


def _params(*semantics):
    return pltpu.CompilerParams(dimension_semantics=semantics, vmem_limit_bytes=VMEM_LIMIT)


def _place():
    return lax.axis_index("x"), lax.axis_index("y"), lax.axis_index("c")


class _Gather:
    def __init__(self, arrays):
        n = len(arrays)
        self.ins = list(arrays)
        self.out_shape = [jax.ShapeDtypeStruct((N_DEV,) + a.shape, a.dtype) for a in arrays]
        self.scratch = [pltpu.SemaphoreType.DMA((n, 7)), pltpu.SemaphoreType.DMA((n, 7)),
                        pltpu.SemaphoreType.DMA((n,))]

    def _copies(self, ins, outs, sems):
        send_sems, recv_sems, local_sems = sems
        n = len(ins)
        x, y, c = _place()
        me, sibling = (x, y, c), (x, y, 1 - c)
        chips = [(1 - x, y), (x, 1 - y), (1 - x, 1 - y)]

        def slot(a, px, py, pc):
            return outs[a].at[4 * px + 2 * py + pc]

        def copy(a, k, block, to, src=None):
            return pltpu.make_async_remote_copy(
                src_ref=slot(a, *block) if src is None else src, dst_ref=slot(a, *block),
                send_sem=send_sems.at[a, k], recv_sem=recv_sems.at[a, k], device_id=to, device_id_type=MESH)

        mine = [pltpu.make_async_copy(ins[a], slot(a, *me), local_sems.at[a]) for a in range(n)]
        first = [copy(a, 0, me, sibling, src=ins[a]) for a in range(n)]
        first += [copy(a, 1 + j, me, (*chip, c), src=ins[a]) for a in range(n) for j, chip in enumerate(chips)]
        landed = [[copy(a, 1 + j, (*chip, c), me) for a in range(n)] for j, chip in enumerate(chips)]
        passed = [[copy(a, 4 + j, (*chip, c), sibling) for a in range(n)] for j, chip in enumerate(chips)]
        from_sibling = [copy(a, 0, sibling, me) for a in range(n)]
        from_sibling += [copy(a, 4 + j, (*chip, 1 - c), me) for a in range(n) for j, chip in enumerate(chips)]
        return mine, first, landed, passed, from_sibling

    def start(self, ins, outs, sems):
        mine, first, _, _, _ = self._copies(ins, outs, sems)
        for cp in mine + first:
            cp.start()

    def relay(self, ins, outs, sems):
        _, _, landed, passed, _ = self._copies(ins, outs, sems)
        for arrivals, forwards in zip(landed, passed):
            for arrival, forward in zip(arrivals, forwards):
                arrival.wait_recv()
                forward.start()

    def finish(self, ins, outs, sems):
        mine, first, _, passed, from_sibling = self._copies(ins, outs, sems)
        for cp in from_sibling:
            cp.wait_recv()
        for cp in first + [f for fs in passed for f in fs]:
            cp.wait_send()
        for cp in mine:
            cp.wait()


class _GatherDirect:
    def __init__(self, arrays):
        n = len(arrays)
        self.ins = list(arrays)
        self.out_shape = [jax.ShapeDtypeStruct((N_DEV,) + a.shape, a.dtype) for a in arrays]
        self.scratch = [pltpu.SemaphoreType.DMA((n, 7)), pltpu.SemaphoreType.DMA((n, 7)),
                        pltpu.SemaphoreType.DMA((n,))]

    def _copies(self, ins, outs, sems):
        send_sems, recv_sems, local_sems = sems
        n = len(ins)
        x, y, c = _place()
        peers = [(x ^ (k >> 2 & 1), y ^ (k >> 1 & 1), c ^ (k & 1)) for k in range(1, N_DEV)]

        def slot(a, px, py, pc):
            return outs[a].at[4 * px + 2 * py + pc]

        def copy(a, k, block, to, src=None):
            return pltpu.make_async_remote_copy(
                src_ref=slot(a, *block) if src is None else src, dst_ref=slot(a, *block),
                send_sem=send_sems.at[a, k], recv_sem=recv_sems.at[a, k], device_id=to, device_id_type=MESH)

        mine = [pltpu.make_async_copy(ins[a], slot(a, x, y, c), local_sems.at[a]) for a in range(n)]
        sends = [copy(a, k, (x, y, c), peer, src=ins[a]) for a in range(n) for k, peer in enumerate(peers)]
        arrivals = [copy(a, k, peer, (x, y, c)) for a in range(n) for k, peer in enumerate(peers)]
        return mine, sends, arrivals

    def start(self, ins, outs, sems):
        mine, sends, _ = self._copies(ins, outs, sems)
        for cp in mine + sends:
            cp.start()

    def relay(self, ins, outs, sems):
        pass

    def finish(self, ins, outs, sems):
        mine, sends, arrivals = self._copies(ins, outs, sems)
        for cp in arrivals:
            cp.wait_recv()
        for cp in sends:
            cp.wait_send()
        for cp in mine:
            cp.wait()


class _RsSibling:
    def __init__(self, ps):
        n = len(ps)
        self.ins = list(ps)
        self.out_shape = [jax.ShapeDtypeStruct((4,) + p.shape[1:], p.dtype) for p in ps]
        self.scratch = [pltpu.SemaphoreType.DMA((n, 4)), pltpu.SemaphoreType.DMA((n, 4))]

    def _copies(self, ins, outs, sems):
        send_sems, recv_sems = sems
        x, y, c = _place()
        return [pltpu.make_async_remote_copy(
            src_ref=ins[a].at[2 * q + (1 - c)], dst_ref=outs[a].at[q], send_sem=send_sems.at[a, q],
            recv_sem=recv_sems.at[a, q], device_id=(x, y, 1 - c), device_id_type=MESH)
            for a in range(len(ins)) for q in range(4)]

    def start(self, ins, outs, sems):
        for cp in self._copies(ins, outs, sems):
            cp.start()

    def relay(self, ins, outs, sems):
        pass

    def finish(self, ins, outs, sems):
        for cp in self._copies(ins, outs, sems):
            cp.wait()


class _RsChips:
    def __init__(self, qs):
        n = len(qs)
        self.ins = list(qs)
        self.out_shape = [jax.ShapeDtypeStruct(q.shape, q.dtype) for q in qs]
        self.scratch = [pltpu.SemaphoreType.DMA((n, 3)), pltpu.SemaphoreType.DMA((n, 3)),
                        pltpu.SemaphoreType.DMA((n,))]

    def _copies(self, ins, outs, sems):
        send_sems, recv_sems, local_sems = sems
        n = len(ins)
        x, y, c = _place()
        my_chip = 2 * x + y
        chips = [(1 - x, y), (x, 1 - y), (1 - x, 1 - y)]

        def copy(a, j, src_slot, dst_slot):
            px, py = chips[j]
            return pltpu.make_async_remote_copy(
                src_ref=ins[a].at[src_slot], dst_ref=outs[a].at[dst_slot], send_sem=send_sems.at[a, j],
                recv_sem=recv_sems.at[a, j], device_id=(px, py, c), device_id_type=MESH)

        own = [pltpu.make_async_copy(ins[a].at[my_chip], outs[a].at[my_chip], local_sems.at[a]) for a in range(n)]
        sends = [copy(a, j, 2 * px + py, my_chip) for a in range(n) for j, (px, py) in enumerate(chips)]
        arrivals = [copy(a, j, my_chip, 2 * px + py) for a in range(n) for j, (px, py) in enumerate(chips)]
        return own, sends, arrivals

    def start(self, ins, outs, sems):
        own, sends, _ = self._copies(ins, outs, sems)
        for cp in own + sends:
            cp.start()

    def relay(self, ins, outs, sems):
        pass

    def finish(self, ins, outs, sems):
        own, sends, arrivals = self._copies(ins, outs, sems)
        for cp in arrivals:
            cp.wait_recv()
        for cp in sends:
            cp.wait_send()
        for cp in own:
            cp.wait()


def _split_plans(plans, refs_in, refs_out, refs_scr, phase):
    i = o = s = 0
    for p in plans:
        ni, no, ns = len(p.ins), len(p.out_shape), len(p.scratch)
        getattr(p, phase)(refs_in[i:i + ni], refs_out[o:o + no], refs_scr[s:s + ns])
        i, o, s = i + ni, o + no, s + ns


def _plan_results(plans, res):
    out = []
    for p in plans:
        out.append(list(res[:len(p.out_shape)]))
        res = res[len(p.out_shape):]
    return out


def _exchange(plans, name):
    c_in = [a for p in plans for a in p.ins]
    c_out = [s for p in plans for s in p.out_shape]
    c_scr = [s for p in plans for s in p.scratch]

    def body(*refs):
        cin, cout, cscr = refs[:len(c_in)], refs[len(c_in):len(c_in) + len(c_out)], refs[len(c_in) + len(c_out):]
        for phase in ("start", "relay", "finish"):
            _split_plans(plans, cin, cout, cscr, phase)

    res = pl.pallas_call(body, name=name, in_specs=[_any()] * len(c_in), out_specs=[_any()] * len(c_out),
                         out_shape=c_out, scratch_shapes=c_scr)(*c_in)
    return _plan_results(plans, res)


def _call(body, *, name, grid, in_specs, out_specs, out_shape, args, semantics, scratch_shapes=(), plans=()):
    n_in, n_out, n_scr = len(in_specs), len(out_specs), len(scratch_shapes)
    c_in = [a for p in plans for a in p.ins]
    c_out = [s for p in plans for s in p.out_shape]
    c_scr = [s for p in plans for s in p.scratch]
    n_steps = math.prod(grid)

    def wrapped(*refs):
        ins, refs = refs[:n_in], refs[n_in:]
        cin, refs = refs[:len(c_in)], refs[len(c_in):]
        outs, refs = refs[:n_out], refs[n_out:]
        cout, refs = refs[:len(c_out)], refs[len(c_out):]
        scr, cscr = refs[:n_scr], refs[n_scr:]
        if plans:
            step = 0
            for d, g in enumerate(grid):
                step = step * g + pl.program_id(d)
            pl.when(step == 0)(lambda: _split_plans(plans, cin, cout, cscr, "start"))
        body(*ins, *outs, *scr)
        if plans:
            pl.when(step == max(n_steps - 2, 0))(lambda: _split_plans(plans, cin, cout, cscr, "relay"))
            pl.when(step == n_steps - 1)(lambda: _split_plans(plans, cin, cout, cscr, "finish"))

    res = pl.pallas_call(
        wrapped, name=name, grid=grid, in_specs=list(in_specs) + [_any()] * len(c_in),
        out_specs=list(out_specs) + [_any()] * len(c_out), out_shape=list(out_shape) + c_out,
        scratch_shapes=list(scratch_shapes) + c_scr,
        compiler_params=_params(*(("arbitrary",) * len(grid) if plans else semantics)),
    )(*args, *c_in)
    return list(res[:n_out]), _plan_results(plans, res[n_out:])


def _ffn_fwd(h, mods, gn, wg, wu, wd, row0, seq, tgt=None, plans=()):
    T = h.shape[0]
    tm = _token_tile(seq, 256)
    tps = seq // tm
    n_t = T // tm
    with_loss = tgt is not None

    def body(h_ref, m_ref, g_ref, wg_ref, wu_ref, wd_ref, *rest):
        if with_loss:
            tgt_ref, out_ref, y_ref, gb_ref, ub_ref, loss_ref = rest
        else:
            out_ref, y_ref, gb_ref, ub_ref = rest
        hv = h_ref[...]
        sh = m_ref[0, row0:row0 + 1, :]
        sc = m_ref[0, row0 + 1:row0 + 2, :]
        ga = m_ref[0, row0 + 2:row0 + 3, :]
        xn, _, _ = _rms_mod(hv, g_ref[...], sh, sc)
        xb = xn.astype(BF16)
        acc = jnp.zeros((tm, D), F32)
        for c0 in range(0, FF, FC):
            gg = _dot_nt(xb, wg_ref[c0:c0 + FC, :])
            uu = _dot_nt(xb, wu_ref[c0:c0 + FC, :])
            gg, uu = gg.astype(BF16), uu.astype(BF16)
            gb_ref[:, c0:c0 + FC] = gg
            ub_ref[:, c0:c0 + FC] = uu
            acc = acc + _dot((gg * jax.nn.sigmoid(gg)) * uu, wd_ref[c0:c0 + FC, :])
        y_ref[...] = acc
        hout = hv + (0.5 * ga) * acc
        if with_loss:
            d = hout - tgt_ref[...]
            out_ref[...] = d * (1.0 / D)
            loss_ref[...] = jnp.full((1, 8, 128), 0.5 / D, F32) * jnp.sum(d * d)
        else:
            out_ref[...] = hout

    tok = pl.BlockSpec((tm, D), lambda i: (i, 0))
    tokf = pl.BlockSpec((tm, FF), lambda i: (i, 0))
    in_specs = [tok, pl.BlockSpec((1, 9, D), lambda i: (i // tps, 0, 0)), pl.BlockSpec((1, D), lambda i: (0, 0)),
                _vmem(), _vmem(), _vmem()]
    out_shape = [jax.ShapeDtypeStruct((T, D), F32), jax.ShapeDtypeStruct((T, D), F32),
                 jax.ShapeDtypeStruct((T, FF), BF16), jax.ShapeDtypeStruct((T, FF), BF16)]
    out_specs = [tok, tok, tokf, tokf]
    args = [h, mods, gn, wg, wu, wd]
    if with_loss:
        in_specs.append(tok)
        args.append(tgt)
        out_shape.append(jax.ShapeDtypeStruct((n_t, 8, 128), F32))
        out_specs.append(pl.BlockSpec((1, 8, 128), lambda i: (i, 0, 0)))
    return _call(body, name="ffn_fwd_loss" if with_loss else "ffn_fwd", grid=(n_t,), in_specs=in_specs,
                 out_specs=out_specs, out_shape=out_shape, args=args, semantics=("parallel",), plans=plans)


def _ffn_gate_up(h, mods, gn, wg, wu, row0, seq, plans=()):
    T = h.shape[0]
    tm = _token_tile(seq, 512)
    tps = seq // tm

    def body(h_ref, m_ref, g_ref, wg_ref, wu_ref, gb_ref, ub_ref):
        xn, _, _ = _rms_mod(h_ref[...], g_ref[...], m_ref[0, row0:row0 + 1, :], m_ref[0, row0 + 1:row0 + 2, :])
        xb = xn.astype(BF16)
        for c0 in range(0, FF, FC):
            gb_ref[:, c0:c0 + FC] = _dot_nt(xb, wg_ref[c0:c0 + FC, :]).astype(BF16)
            ub_ref[:, c0:c0 + FC] = _dot_nt(xb, wu_ref[c0:c0 + FC, :]).astype(BF16)

    tokf = pl.BlockSpec((tm, FF), lambda i: (i, 0))
    return _call(
        body, name="ffn_gate_up", grid=(T // tm,),
        in_specs=[pl.BlockSpec((tm, D), lambda i: (i, 0)), pl.BlockSpec((1, 9, D), lambda i: (i // tps, 0, 0)),
                  pl.BlockSpec((1, D), lambda i: (0, 0)), _vmem(), _vmem()],
        out_specs=[tokf, tokf], out_shape=[jax.ShapeDtypeStruct((T, FF), BF16)] * 2, args=(h, mods, gn, wg, wu),
        semantics=("parallel",), plans=plans)


def _ffn_down(h, gb, ub, mods, wd, row0, seq, plans=()):
    T = h.shape[0]
    tm = _token_tile(seq, 512)
    tps = seq // tm

    def body(h_ref, gb_ref, ub_ref, m_ref, wd_ref, out_ref, y_ref):
        acc = jnp.zeros((tm, D), F32)
        for c0 in range(0, FF, FC):
            gg = gb_ref[:, c0:c0 + FC]
            acc = acc + _dot((gg * jax.nn.sigmoid(gg)) * ub_ref[:, c0:c0 + FC], wd_ref[c0:c0 + FC, :])
        y_ref[...] = acc
        out_ref[...] = h_ref[...] + (0.5 * m_ref[0, row0 + 2:row0 + 3, :]) * acc

    tok = pl.BlockSpec((tm, D), lambda i: (i, 0))
    tokf = pl.BlockSpec((tm, FF), lambda i: (i, 0))
    return _call(
        body, name="ffn_down", grid=(T // tm,),
        in_specs=[tok, tokf, tokf, pl.BlockSpec((1, 9, D), lambda i: (i // tps, 0, 0)), _vmem()],
        out_specs=[tok, tok], out_shape=[jax.ShapeDtypeStruct((T, D), F32)] * 2, args=(h, gb, ub, mods, wd),
        semantics=("parallel",), plans=plans)


def _ffn_bwd(h, dhn, y, gb, ub, mods, gn, wg, wu, wd, row0, seq, plans=()):
    T = h.shape[0]
    B = T // seq
    tm = _token_tile(seq, 256)
    tps = seq // tm
    n_t = T // tm

    def body(h_ref, dhn_ref, y_ref, gb_ref, ub_ref, m_ref, g_ref, wg_ref, wu_ref, wd_ref,
             dh_ref, xb_ref, dyb_ref, a_ref, dg_ref, du_ref, mg_ref):
        i = pl.program_id(0)
        hv = h_ref[...]
        dhn = dhn_ref[...]
        sh = m_ref[0, row0:row0 + 1, :]
        sc = m_ref[0, row0 + 1:row0 + 2, :]
        ga = m_ref[0, row0 + 2:row0 + 3, :]
        g = g_ref[...]
        xn, r, inv = _rms_mod(hv, g, sh, sc)
        xb_ref[...] = xn.astype(BF16)
        dyb = ((0.5 * ga) * dhn).astype(BF16)
        dyb_ref[...] = dyb
        dga = 0.5 * jnp.sum(dhn * y_ref[...], axis=0, keepdims=True)
        dxn = jnp.zeros((tm, D), F32)
        for c0 in range(0, FF, FC):
            wgc = wg_ref[c0:c0 + FC, :]
            wuc = wu_ref[c0:c0 + FC, :]
            gg = gb_ref[:, c0:c0 + FC]
            uu = ub_ref[:, c0:c0 + FC]
            sig = jax.nn.sigmoid(gg)
            s = gg * sig
            a_ref[:, c0:c0 + FC] = s * uu
            da = _dot_nt(dyb, wd_ref[c0:c0 + FC, :]).astype(BF16)
            dub = da * s
            dgb = (da * uu) * (sig * (1.0 + gg * (1.0 - sig)))
            dg_ref[:, c0:c0 + FC] = dgb
            du_ref[:, c0:c0 + FC] = dub
            dxn = dxn + _dot(dgb, wgc) + _dot(dub, wuc)
        dh, s_dxn, s_dxr = _rms_mod_bwd(dxn, r, inv, g, sc)
        dh_ref[...] = dhn + dh

        @pl.when(i % tps == 0)
        def _():
            mg_ref[...] = jnp.zeros(mg_ref.shape, F32)

        mg_ref[0, 0:1, :] += s_dxn
        mg_ref[0, 1:2, :] += s_dxr
        mg_ref[0, 2:3, :] += dga

    tok = pl.BlockSpec((tm, D), lambda i: (i, 0))
    tokf = pl.BlockSpec((tm, FF), lambda i: (i, 0))
    return _call(
        body, name="ffn_bwd", grid=(n_t,),
        in_specs=[tok, tok, tok, tokf, tokf, pl.BlockSpec((1, 9, D), lambda i: (i // tps, 0, 0)),
                  pl.BlockSpec((1, D), lambda i: (0, 0)), _vmem(), _vmem(), _vmem()],
        out_specs=[tok, tok, tok, tokf, tokf, tokf, pl.BlockSpec((1, 8, D), lambda i: (i // tps, 0, 0))],
        out_shape=[jax.ShapeDtypeStruct((T, D), F32), jax.ShapeDtypeStruct((T, D), BF16),
                   jax.ShapeDtypeStruct((T, D), BF16), jax.ShapeDtypeStruct((T, FF), BF16),
                   jax.ShapeDtypeStruct((T, FF), BF16), jax.ShapeDtypeStruct((T, FF), BF16),
                   jax.ShapeDtypeStruct((B, 8, D), F32)],
        args=(h, dhn, y, gb, ub, mods, gn, wg, wu, wd), semantics=("arbitrary",), plans=plans)


def _wgrad(a, b, name, plans=()):
    T, da = a.shape
    db = b.shape[1]
    bm = {2816: 1408, 3840: 1280}[da]
    bn = db
    tk = min(1024, T)
    nk = T // tk

    def body(a_ref, b_ref, o_ref):
        @pl.when(pl.program_id(2) == 0)
        def _():
            o_ref[...] = jnp.zeros(o_ref.shape, F32)

        o_ref[...] += _dot_tn(a_ref[...], b_ref[...])

    (out,), plan_outs = _call(
        body, name=name, grid=(da // bm, db // bn, nk),
        in_specs=[pl.BlockSpec((tk, bm), lambda i, j, k: (k, i)), pl.BlockSpec((tk, bn), lambda i, j, k: (k, j))],
        out_specs=[pl.BlockSpec((bm, bn), lambda i, j, k: (i, j))], out_shape=[jax.ShapeDtypeStruct((da, db), F32)],
        args=(a, b), semantics=("parallel", "parallel", "arbitrary"), plans=plans)
    return out, plan_outs


def _inproj_fwd(h, mods, gn, w_in, seq, plans=()):
    T = h.shape[0]
    tm = _token_tile(seq)
    tps = seq // tm

    def body(h_ref, m_ref, g_ref, w_ref, *outs):
        xn, _, _ = _rms_mod(h_ref[...], g_ref[...], m_ref[0, 3:4, :], m_ref[0, 4:5, :])
        xb = xn.astype(BF16)
        for (_, c0, w), o_ref in zip(PIECES, outs):
            o_ref[...] = _dot_nt(xb, w_ref[c0:c0 + w, :])

    return _call(
        body, name="inproj_fwd", grid=(T // tm,),
        in_specs=[pl.BlockSpec((tm, D), lambda i: (i, 0)), pl.BlockSpec((1, 9, D), lambda i: (i // tps, 0, 0)),
                  pl.BlockSpec((1, D), lambda i: (0, 0)), _vmem()],
        out_specs=[pl.BlockSpec((tm, w), lambda i: (i, 0)) for _, _, w in PIECES],
        out_shape=[jax.ShapeDtypeStruct((T, w), F32) for _, _, w in PIECES], args=(h, mods, gn, w_in),
        semantics=("parallel",), plans=plans)


def _inproj_bwd(h, dh_res, dpieces, mods, gn, w_in, seq, plans=()):
    T = h.shape[0]
    B = T // seq
    tm = _token_tile(seq, 256)
    tps = seq // tm

    def body(h_ref, dres_ref, *rest):
        dp_refs = rest[:len(PIECES)]
        m_ref, g_ref, w_ref, dh_ref, xb_ref, dpb_ref, mg_ref = rest[len(PIECES):]
        i = pl.program_id(0)
        g = g_ref[...]
        sc = m_ref[0, 4:5, :]
        xn, r, inv = _rms_mod(h_ref[...], g, m_ref[0, 3:4, :], sc)
        xb_ref[...] = xn.astype(BF16)
        dxn = jnp.zeros((tm, D), F32)
        for (_, c0, w), dp_ref in zip(PIECES, dp_refs):
            dpb = dp_ref[...].astype(BF16)
            dpb_ref[:, c0:c0 + w] = dpb
            dxn = dxn + _dot(dpb, w_ref[c0:c0 + w, :])
        dh, s_dxn, s_dxr = _rms_mod_bwd(dxn, r, inv, g, sc)
        dh_ref[...] = dres_ref[...] + dh

        @pl.when(i % tps == 0)
        def _():
            mg_ref[...] = jnp.zeros(mg_ref.shape, F32)

        mg_ref[0, 0:1, :] += s_dxn
        mg_ref[0, 1:2, :] += s_dxr

    tok = pl.BlockSpec((tm, D), lambda i: (i, 0))
    return _call(
        body, name="inproj_bwd", grid=(T // tm,),
        in_specs=[tok, tok] + [pl.BlockSpec((tm, w), lambda i: (i, 0)) for _, _, w in PIECES]
        + [pl.BlockSpec((1, 9, D), lambda i: (i // tps, 0, 0)), pl.BlockSpec((1, D), lambda i: (0, 0)), _vmem()],
        out_specs=[tok, tok, pl.BlockSpec((tm, IN_COLS), lambda i: (i, 0)),
                   pl.BlockSpec((1, 8, D), lambda i: (i // tps, 0, 0))],
        out_shape=[jax.ShapeDtypeStruct((T, D), F32), jax.ShapeDtypeStruct((T, D), BF16),
                   jax.ShapeDtypeStruct((T, IN_COLS), BF16), jax.ShapeDtypeStruct((B, 8, D), F32)],
        args=(h, dh_res, *dpieces, mods, gn, w_in), semantics=("arbitrary",), plans=plans)


def _seg_mean(x):
    i = lax.broadcasted_iota(jnp.int32, (128, 128), 0) >> 6
    j = lax.broadcasted_iota(jnp.int32, (128, 128), 1) >> 6
    ones = jnp.where(i == j, 1.0 / HD, 0.0).astype(BF16)
    hi = x.astype(BF16)
    lo = (x - hi.astype(F32)).astype(BF16)
    return _dot(hi, ones) + _dot(lo, ones)


def _head_norm(x, g2):
    inv = lax.rsqrt(_seg_mean(x * x) + EPS)
    y = x * inv
    return y * g2, y, inv


def _head_norm_bwd(dxn, y, inv, g2):
    dy = dxn * g2
    return inv * (dy - y * _seg_mean(dy * y)), jnp.sum(dxn * y, axis=0, keepdims=True)


def _swa_block(q, kk, vv, gq2, gk2, sinks, first, do=None):
    lo = lax.broadcasted_iota(jnp.int32, (1, 128), 1) < HD
    kn, ky, kinv = _head_norm(kk, gk2)

    def operands(x):
        xr = pltpu.roll(x, HD, 1)
        own_lo, own_hi = jnp.where(lo, x, 0.0).astype(BF16), jnp.where(lo, 0.0, x).astype(BF16)
        rolled_lo, rolled_hi = jnp.where(lo, xr, 0.0).astype(BF16), jnp.where(lo, 0.0, xr).astype(BF16)
        return (own_lo, rolled_hi), (rolled_lo, own_hi)

    def restore(parts):
        (own_lo, rolled_hi), (rolled_lo, own_hi) = parts
        return (jnp.where(lo, own_lo, own_hi)
                + pltpu.roll(jnp.where(lo, rolled_lo, rolled_hi), HD, 1))

    k_ops, v_ops = operands(kn), operands(vv)
    k2 = [jnp.concatenate(pair, axis=0) for pair in k_ops]
    def stack(x):
        return jnp.concatenate([x[:, 128 * p:128 * (p + 1)] for p in range(4)], axis=0)

    def unstack(x):
        return jnp.concatenate([x[BLK * p:BLK * (p + 1)] for p in range(4)], axis=1)

    def of_head(x, kh):
        return x[2 * BLK * kh:2 * BLK * (kh + 1)]

    nq = 4 * BLK
    pair = lax.broadcasted_iota(jnp.int32, (nq, 1), 0) >> 7
    row = lax.broadcasted_iota(jnp.int32, (nq, 2 * BLK), 0) & (BLK - 1)
    col = lax.broadcasted_iota(jnp.int32, (nq, 2 * BLK), 1)
    valid = (col <= row + BLK) & (col > row) & (col >= jnp.where(first, BLK, 0))
    scale = HD ** -0.5
    qn, qy, qinv = _head_norm(stack(q), gq2)
    qnb = qn.astype(BF16)
    s2 = jnp.concatenate([_dot_nt(of_head(qnb, kh), k2[kh]) for kh in range(N_KV)], axis=0) * scale
    probs, p_sink = [], []
    for j in range(2):
        s = jnp.where(valid, s2[:, 2 * BLK * j:2 * BLK * (j + 1)], NEG)
        sink = jnp.zeros((nq, 1), F32)
        for p in range(4):
            sink = jnp.where(pair == p, sinks[:, 2 * p + j:2 * p + j + 1], sink)
        m = jnp.maximum(jnp.max(s, axis=-1, keepdims=True), sink)
        e = jnp.exp(s - m)
        e_sink = jnp.exp(sink - m)
        rden = 1.0 / (jnp.sum(e, axis=-1, keepdims=True) + e_sink)
        probs.append(e * rden)
        p_sink.append(e_sink * rden)
    pb = [p.astype(BF16) for p in probs]
    if do is None:
        return unstack(jnp.concatenate(
            [_dot(of_head(pb[0], kh), v_ops[kh][0]) + _dot(of_head(pb[1], kh), v_ops[kh][1]) for kh in range(N_KV)],
            axis=0))
    dob = stack(do).astype(BF16)
    ds, dsinks = [], [None] * 8
    for j in range(2):
        dp = jnp.concatenate([_dot_nt(of_head(dob, kh), v_ops[kh][j]) for kh in range(N_KV)], axis=0)
        t = jnp.sum(dp * probs[j], axis=-1, keepdims=True)
        ds.append(probs[j] * (dp - t) * scale)
        lost = p_sink[j] * t
        for p in range(4):
            dsinks[2 * p + j] = -jnp.sum(lost[BLK * p:BLK * (p + 1)])
    dsb = jnp.concatenate(ds, axis=1).astype(BF16)
    dqn = jnp.concatenate([_dot(of_head(dsb, kh), k2[kh]) for kh in range(N_KV)], axis=0)
    dq, dgq2 = _head_norm_bwd(dqn, qy, qinv, gq2)
    dk2 = [_dot_tn(of_head(dsb, kh), of_head(qnb, kh)) for kh in range(N_KV)]
    dv_ops = [[_dot_tn(of_head(pb[j], kh), of_head(dob, kh)) for j in range(2)] for kh in range(N_KV)]
    dkn = restore(tuple((d[:2 * BLK], d[2 * BLK:]) for d in dk2))
    dvv = restore(tuple(tuple(d) for d in dv_ops))
    dkk, dgk2 = _head_norm_bwd(dkn, ky, kinv, gk2)
    return unstack(dq), dkk, dvv, dgq2, dgk2, dsinks


SWA_GROUP = 8


def _swa_specs(nbs):
    grp = min(SWA_GROUP, nbs)
    rows = grp * BLK

    def tok(w):
        return pl.BlockSpec((rows, w), lambda i: (i, 0))

    halo = pl.BlockSpec((BLK, 128), lambda i: (jnp.maximum(i * grp - 1, 0), 0))
    vec = pl.BlockSpec((1, 128), lambda i: (0, 0))
    sk = pl.BlockSpec((1, 8), lambda i: (0, 0))
    return grp, tok, halo, vec, sk


def _swa_fwd(q, k, v, gq2, gk2, sinks, nbs, plans=()):
    T = q.shape[0]
    grp, tok, halo, vec, sk = _swa_specs(nbs)

    def body(q_ref, kh_ref, kc_ref, vh_ref, vc_ref, gq_ref, gk_ref, sk_ref, o_ref):
        seq_start = ((pl.program_id(0) * grp) % nbs) == 0
        for g in range(grp):
            rows = slice(g * BLK, (g + 1) * BLK)
            prev = slice((g - 1) * BLK, g * BLK)
            kk = jnp.concatenate([kh_ref[...] if g == 0 else kc_ref[prev, :], kc_ref[rows, :]], axis=0)
            vv = jnp.concatenate([vh_ref[...] if g == 0 else vc_ref[prev, :], vc_ref[rows, :]], axis=0)
            o_ref[rows, :] = _swa_block(q_ref[rows, :], kk, vv, gq_ref[...], gk_ref[...], sk_ref[...],
                                        seq_start if g == 0 else False)

    (out,), plan_outs = _call(
        body, name="swa_fwd", grid=(T // (grp * BLK),),
        in_specs=[tok(D_B), halo, tok(128), halo, tok(128), vec, vec, sk], out_specs=[tok(D_B)],
        out_shape=[jax.ShapeDtypeStruct((T, D_B), F32)], args=(q, k, k, v, v, gq2, gk2, sinks),
        semantics=("parallel",), plans=plans)
    return out, plan_outs


def _swa_bwd(q, k, v, gq2, gk2, sinks, do, nbs, plans=()):
    T = q.shape[0]
    grp, tok, halo, vec, sk = _swa_specs(nbs)
    steps = T // (grp * BLK)

    def body(q_ref, kh_ref, kc_ref, vh_ref, vc_ref, gq_ref, gk_ref, sk_ref, do_ref,
             dq_ref, dk_ref, dv_ref, dkh_ref, dvh_ref, dgq_ref, dgk_ref, dsk_ref):
        i = pl.program_id(0)
        seq_start = ((i * grp) % nbs) == 0

        @pl.when(i == 0)
        def _():
            for r in (dgq_ref, dgk_ref, dsk_ref):
                r[...] = jnp.zeros(r.shape, F32)

        res = []
        for g in range(grp):
            rows = slice(g * BLK, (g + 1) * BLK)
            prev = slice((g - 1) * BLK, g * BLK)
            kk = jnp.concatenate([kh_ref[...] if g == 0 else kc_ref[prev, :], kc_ref[rows, :]], axis=0)
            vv = jnp.concatenate([vh_ref[...] if g == 0 else vc_ref[prev, :], vc_ref[rows, :]], axis=0)
            res.append(_swa_block(q_ref[rows, :], kk, vv, gq_ref[...], gk_ref[...], sk_ref[...],
                                  seq_start if g == 0 else False, do=do_ref[rows, :]))
        lane = lax.broadcasted_iota(jnp.int32, (8, 128), 1)
        upd = jnp.zeros((8, 128), F32)
        for g, (dq, dkk, dvv, dgq2, dgk2, dsinks) in enumerate(res):
            rows = slice(g * BLK, (g + 1) * BLK)
            dq_ref[rows, :] = dq
            dk_ref[rows, :] = dkk[BLK:] + res[g + 1][1][:BLK] if g + 1 < grp else dkk[BLK:]
            dv_ref[rows, :] = dvv[BLK:] + res[g + 1][2][:BLK] if g + 1 < grp else dvv[BLK:]
            dgq_ref[...] += dgq2
            dgk_ref[...] += dgk2
            for h, d in enumerate(dsinks):
                upd = upd + jnp.where(lane == h, d, 0.0)
        dkh_ref[0] = res[0][1][:BLK]
        dvh_ref[0] = res[0][2][:BLK]
        dsk_ref[...] += upd

    one = pl.BlockSpec((1, BLK, 128), lambda i: (i, 0, 0))
    halo_shape = jax.ShapeDtypeStruct((steps, BLK, 128), F32)
    return _call(
        body, name="swa_bwd", grid=(steps,),
        in_specs=[tok(D_B), halo, tok(128), halo, tok(128), vec, vec, sk, tok(D_B)],
        out_specs=[tok(D_B), tok(128), tok(128), one, one, vec, vec, pl.BlockSpec((8, 128), lambda i: (0, 0))],
        out_shape=[jax.ShapeDtypeStruct((T, D_B), F32), jax.ShapeDtypeStruct((T, 128), F32),
                   jax.ShapeDtypeStruct((T, 128), F32), halo_shape, halo_shape, jax.ShapeDtypeStruct((1, 128), F32),
                   jax.ShapeDtypeStruct((1, 128), F32), jax.ShapeDtypeStruct((8, 128), F32)],
        args=(q, k, k, v, v, gq2, gk2, sinks, do), semantics=("arbitrary",), plans=plans)


def _swa_add_halo(dk, dk_halo):
    steps = dk_halo.shape[0]
    nxt = jnp.concatenate([dk_halo[1:], jnp.zeros_like(dk_halo[:1])], axis=0)[:, None]
    dk = dk.reshape(steps, -1, BLK, 128)
    return jnp.concatenate([dk[:, :-1], dk[:, -1:] + nxt], axis=1).reshape(-1, 128)


def _sgu_norm(av, g_ln, b_ln):
    t, th = _gelu(av)
    mu = jnp.mean(t, axis=-1, keepdims=True)
    tc = t - mu
    rstd = lax.rsqrt(jnp.mean(tc * tc, axis=-1, keepdims=True) + EPS)
    vhat = tc * rstd
    return vhat * g_ln + b_ln, vhat, rstd, th


def _masked_ws(ws_ref):
    tril = lax.broadcasted_iota(jnp.int32, (BLK, BLK), 0) >= lax.broadcasted_iota(jnp.int32, (BLK, BLK), 1)
    return [jnp.where(tril, ws_ref[g], 0.0).astype(BF16) for g in range(N_GRP)]


def _mix_fwd(au, av, gta, gtb, ob, h, mods, g_ln, b_ln, ws, bsb, wa, wb, wout, seq, plans=()):
    T = h.shape[0]
    tm = _token_tile(seq)
    tps = seq // tm

    def body(au_ref, av_ref, gta_ref, gtb_ref, ob_ref, h_ref, m_ref, gl_ref, bl_ref, ws_ref, bs_ref,
             wa_ref, wb_ref, wo_ref, out_ref, vvb_s, z_s):
        u, _ = _gelu(au_ref[...])
        vv, _, _, _ = _sgu_norm(av_ref[...], gl_ref[...], bl_ref[...])
        vvb_s[...] = vv.astype(BF16)
        wsm = _masked_ws(ws_ref)
        for c in range(tm // BLK):
            rows = slice(c * BLK, (c + 1) * BLK)
            for g in range(N_GRP):
                cols = slice(g * BLK, (g + 1) * BLK)
                z_s[rows, cols] = _dot(wsm[g], vvb_s[rows, cols]) + bs_ref[g]
        ya = _dot_nt((u * z_s[...]).astype(BF16), wa_ref[...])
        yb = _dot_nt(ob_ref[...].astype(BF16), wb_ref[...])
        merged = jax.nn.sigmoid(gta_ref[...]) * ya + jax.nn.sigmoid(gtb_ref[...]) * yb
        out_ref[...] = h_ref[...] + m_ref[0, 5:6, :] * _dot(merged.astype(BF16), wo_ref[...])

    def tok(w):
        return pl.BlockSpec((tm, w), lambda i: (i, 0))

    def full(shape):
        return pl.BlockSpec(shape, lambda i: (0,) * len(shape))

    (out,), plan_outs = _call(
        body, name="mix_fwd", grid=(T // tm,),
        in_specs=[tok(D_A), tok(D_A), tok(D), tok(D), tok(D_B), tok(D),
                  pl.BlockSpec((1, 9, D), lambda i: (i // tps, 0, 0)), full((1, D_A)), full((1, D_A)),
                  full((N_GRP, BLK, BLK)), full((N_GRP, BLK, BLK)), _vmem(), _vmem(), _vmem()],
        out_specs=[tok(D)], out_shape=[jax.ShapeDtypeStruct((T, D), F32)],
        scratch_shapes=[pltpu.VMEM((tm, D_A), BF16), pltpu.VMEM((tm, D_A), F32)],
        args=(au, av, gta, gtb, ob, h, mods, g_ln, b_ln, ws, bsb, wa, wb, wout), semantics=("parallel",),
        plans=plans)
    return out, plan_outs


def _mix_bwd(au, av, gta, gtb, ob, dh, mods, g_ln, b_ln, ws, bsb, wa, wb, wout, seq, plans=()):
    T = dh.shape[0]
    B = T // seq
    tm = _token_tile(seq, 256)
    tps = seq // tm

    def body(au_ref, av_ref, gta_ref, gtb_ref, ob_ref, dh_ref, m_ref, gl_ref, bl_ref, ws_ref, bs_ref,
             wa_ref, wb_ref, wo_ref,
             dau_ref, dav_ref, dgta_ref, dgtb_ref, dob_ref, dwo_ref, dwa_ref, dwb_ref, dws_ref, dbs_ref, dln_ref,
             mg_ref, vvb_s, z_s, dz_s, dzb_s, dvv_s):
        i = pl.program_id(0)

        @pl.when(i == 0)
        def _():
            for r in (dwo_ref, dwa_ref, dwb_ref, dws_ref, dbs_ref, dln_ref):
                r[...] = jnp.zeros(r.shape, F32)

        @pl.when(i % tps == 0)
        def _():
            mg_ref[...] = jnp.zeros(mg_ref.shape, F32)

        auv = au_ref[...]
        avv = av_ref[...]
        u, thu = _gelu(auv)
        g_ln = gl_ref[...]
        vv, vhat, rstd, thv = _sgu_norm(avv, g_ln, bl_ref[...])
        vvb_s[...] = vv.astype(BF16)
        wsm = _masked_ws(ws_ref)
        for c in range(tm // BLK):
            rows = slice(c * BLK, (c + 1) * BLK)
            for g in range(N_GRP):
                cols = slice(g * BLK, (g + 1) * BLK)
                z_s[rows, cols] = _dot(wsm[g], vvb_s[rows, cols]) + bs_ref[g]
        z = z_s[...]
        yab = (u * z).astype(BF16)
        obb = ob_ref[...].astype(BF16)
        ya = _dot_nt(yab, wa_ref[...])
        yb = _dot_nt(obb, wb_ref[...])
        sa = jax.nn.sigmoid(gta_ref[...])
        sb = jax.nn.sigmoid(gtb_ref[...])
        mb = (sa * ya + sb * yb).astype(BF16)
        dhv = dh_ref[...]
        mg_ref[0, 0:1, :] += jnp.sum(dhv * _dot(mb, wo_ref[...]), axis=0, keepdims=True)
        dmob = (m_ref[0, 5:6, :] * dhv).astype(BF16)
        dwo_ref[...] += _dot_tn(mb, dmob)
        dmerged = _dot_nt(dmob, wo_ref[...])
        dya = dmerged * sa
        dyb = dmerged * sb
        dgta_ref[...] = dya * ya * (1.0 - sa)
        dgtb_ref[...] = dyb * yb * (1.0 - sb)
        dyab = dya.astype(BF16)
        dybb = dyb.astype(BF16)
        dwa_ref[...] += _dot_tn(dyab, yab)
        dwb_ref[...] += _dot_tn(dybb, obb)
        dob_ref[...] = _dot(dybb, wb_ref[...])
        dyap = _dot(dyab, wa_ref[...])
        dau_ref[...] = (dyap * z) * _gelu_grad(auv, thu)
        dz = dyap * u
        dz_s[...] = dz
        dzb_s[...] = dz.astype(BF16)
        for c in range(tm // BLK):
            rows = slice(c * BLK, (c + 1) * BLK)
            for g in range(N_GRP):
                cols = slice(g * BLK, (g + 1) * BLK)
                dzb = dzb_s[rows, cols]
                dvv_s[rows, cols] = _dot_tn(wsm[g], dzb)
                dws_ref[g] += _dot_nt(dzb, vvb_s[rows, cols])
                dbs_ref[g] += dz_s[rows, cols]
        dvv = dvv_s[...]
        dln_ref[0:1, :] += jnp.sum(dvv * vhat, axis=0, keepdims=True)
        dln_ref[1:2, :] += jnp.sum(dvv, axis=0, keepdims=True)
        dvh = dvv * g_ln
        dt = rstd * (dvh - jnp.mean(dvh, axis=-1, keepdims=True)
                     - vhat * jnp.mean(dvh * vhat, axis=-1, keepdims=True))
        dav_ref[...] = dt * _gelu_grad(avv, thv)

    def tok(w):
        return pl.BlockSpec((tm, w), lambda i: (i, 0))

    def full(shape):
        return pl.BlockSpec(shape, lambda i: (0,) * len(shape))

    return _call(
        body, name="mix_bwd", grid=(T // tm,),
        in_specs=[tok(D_A), tok(D_A), tok(D), tok(D), tok(D_B), tok(D),
                  pl.BlockSpec((1, 9, D), lambda i: (i // tps, 0, 0)), full((1, D_A)), full((1, D_A)),
                  full((N_GRP, BLK, BLK)), full((N_GRP, BLK, BLK)), _vmem(), _vmem(), _vmem()],
        out_specs=[tok(D_A), tok(D_A), tok(D), tok(D), tok(D_B), full((D, D)), full((D, D_A)), full((D, D_B)),
                   full((N_GRP, BLK, BLK)), full((N_GRP, BLK, BLK)), full((8, D_A)),
                   pl.BlockSpec((1, 8, D), lambda i: (i // tps, 0, 0))],
        out_shape=[jax.ShapeDtypeStruct((T, D_A), F32), jax.ShapeDtypeStruct((T, D_A), F32),
                   jax.ShapeDtypeStruct((T, D), F32), jax.ShapeDtypeStruct((T, D), F32),
                   jax.ShapeDtypeStruct((T, D_B), F32), jax.ShapeDtypeStruct((D, D), F32),
                   jax.ShapeDtypeStruct((D, D_A), F32), jax.ShapeDtypeStruct((D, D_B), F32),
                   jax.ShapeDtypeStruct((N_GRP, BLK, BLK), F32), jax.ShapeDtypeStruct((N_GRP, BLK, BLK), F32),
                   jax.ShapeDtypeStruct((8, D_A), F32), jax.ShapeDtypeStruct((B, 8, D), F32)],
        scratch_shapes=[pltpu.VMEM((tm, D_A), BF16), pltpu.VMEM((tm, D_A), F32), pltpu.VMEM((tm, D_A), F32),
                        pltpu.VMEM((tm, D_A), BF16), pltpu.VMEM((tm, D_A), F32)],
        args=(au, av, gta, gtb, ob, dh, mods, g_ln, b_ln, ws, bsb, wa, wb, wout), semantics=("arbitrary",),
        plans=plans)


def _prologue(c_pad, w_ada, first_shards):
    cols = w_ada.shape[1]
    plan_w, plan_c = _Gather(first_shards), _GatherDirect([c_pad])
    plan_m = _GatherDirect([jax.ShapeDtypeStruct((N_DEV * 8, cols), F32)])
    n_w = len(first_shards)

    def body(c_ref, wada_ref, *rest):
        w_ins, rest = rest[:n_w], rest[n_w:]
        call_ref, mods_ref = rest[:2]
        w_outs, rest = rest[2:2 + n_w], rest[2 + n_w:]
        cvm, part, local_sem = rest[:3]
        sems = rest[3:]
        sems_w, sems_c, sems_m = sems[:3], sems[3:6], sems[6:9]
        plan_c.start([c_ref], [call_ref], sems_c)
        plan_c.relay([c_ref], [call_ref], sems_c)
        plan_c.finish([c_ref], [call_ref], sems_c)
        load = pltpu.make_async_copy(call_ref, cvm, local_sem)
        load.start()
        load.wait()
        cv = cvm[...].reshape(N_DEV * 8, D)
        part[...] = _dot((cv * jax.nn.sigmoid(cv)).astype(BF16), wada_ref[...].astype(BF16))
        plan_m.start([part], [mods_ref], sems_m)
        plan_w.start(w_ins, w_outs, sems_w)
        plan_m.relay([part], [mods_ref], sems_m)
        plan_m.finish([part], [mods_ref], sems_m)
        plan_w.relay(w_ins, w_outs, sems_w)
        plan_w.finish(w_ins, w_outs, sems_w)

    res = pl.pallas_call(
        body, name="prologue", in_specs=[_any(), _vmem()] + [_any()] * n_w,
        out_specs=[_any()] * (2 + n_w), out_shape=plan_c.out_shape + plan_m.out_shape + plan_w.out_shape,
        scratch_shapes=[pltpu.VMEM((N_DEV, 8, D), F32), pltpu.VMEM((N_DEV * 8, cols), F32), pltpu.SemaphoreType.DMA]
        + plan_w.scratch + plan_c.scratch + plan_m.scratch,
        compiler_params=pltpu.CompilerParams(vmem_limit_bytes=VMEM_LIMIT),
    )(c_pad, w_ada, *first_shards)
    return res[0], res[1], list(res[2:])


def _ada_update(c_all, dm_cols, w, m, v, plans=()):
    n, cols = c_all.shape[0], w.shape[1]

    def body(c_ref, dm_ref, w_ref, m_ref, v_ref, g_ref, d_ref, nm_ref, nv_ref):
        cv = c_ref[...]
        g = _dot_tn((cv * jax.nn.sigmoid(cv)).astype(BF16), dm_ref[...].astype(BF16))
        g_ref[...] = g
        d_ref[...], nm_ref[...], nv_ref[...] = _adamw(w_ref[...], g, m_ref[...], v_ref[...])

    col = pl.BlockSpec((D, 128), lambda j: (0, j))
    return _call(
        body, name="ada_update", grid=(cols // 128,),
        in_specs=[pl.BlockSpec((n, D), lambda j: (0, 0)), pl.BlockSpec((n, 128), lambda j: (0, j)), col, col, col],
        out_specs=[col] * 4, out_shape=[jax.ShapeDtypeStruct(w.shape, F32)] * 4, args=(c_all, dm_cols, w, m, v),
        semantics=("parallel",), plans=plans)


def _mod_finish(mg1, mg2, mg2g, mg3, mods, g1, g2, g3):
    B = mods.shape[0]

    def body(mg1_ref, mg2_ref, mg2g_ref, mg3_ref, m_ref, g1_ref, g2_ref, g3_ref, dm_ref, dgn_ref):
        dgn_ref[...] = jnp.zeros(dgn_ref.shape, F32)
        for k, (mg, g_ref) in enumerate(((mg1_ref, g1_ref), (mg2_ref, g2_ref), (mg3_ref, g3_ref))):
            for b in range(B):
                s_dxr = mg[b, 1:2, :]
                dm_ref[b, 3 * k:3 * k + 1, :] = mg[b, 0:1, :]
                dm_ref[b, 3 * k + 1:3 * k + 2, :] = g_ref[...] * s_dxr
                dm_ref[b, 3 * k + 2:3 * k + 3, :] = mg2g_ref[b, 0:1, :] if k == 1 else mg[b, 2:3, :]
                dgn_ref[k:k + 1, :] += (1.0 + m_ref[b, 3 * k + 1:3 * k + 2, :]) * s_dxr

    args = (mg1, mg2, mg2g, mg3, mods, g1, g2, g3)
    out_shape = [jax.ShapeDtypeStruct((B, 9, D), F32), jax.ShapeDtypeStruct((8, D), F32)]
    return pl.pallas_call(body, name="mod_finish", grid=(1,), in_specs=[_whole(a) for a in args],
                          out_specs=[_whole(o) for o in out_shape], out_shape=out_shape,
                          compiler_params=_params("arbitrary"))(*args)


def _small_update(gathered, params, ms, vs):
    n = len(SMALL)
    B = gathered[0].shape[1]

    def body(*refs):
        gdm, ggn, gln, gws, gbs, ggq, ggk, gsk, gloss = refs[:9]
        w = dict(zip(SMALL, refs[9:9 + n]))
        m = dict(zip(SMALL, refs[9 + n:9 + 2 * n]))
        v = dict(zip(SMALL, refs[9 + 2 * n:9 + 3 * n]))
        outs = refs[9 + 3 * n:-1]
        out = {name: outs[4 * k:4 * k + 4] for k, name in enumerate(SMALL)}

        def total(ref, idx):
            acc = ref[(0,) + idx]
            for dev in range(1, N_DEV):
                acc = acc + ref[(dev,) + idx]
            return acc

        def finish(name, g, idx=(Ellipsis,)):
            d, nm, nv = _adamw(w[name][idx], g, m[name][idx], v[name][idx])
            for ref, val in zip(out[name], (g, d, nm, nv)):
                ref[idx] = val

        g_bada = total(gdm, (slice(0, 1),))
        for b in range(1, B):
            g_bada = g_bada + total(gdm, (slice(b, b + 1),))
        finish("b_ada", g_bada)
        finish("g_norm1", total(ggn, (slice(0, 1),)))
        finish("g_norm2", total(ggn, (slice(1, 2),)))
        finish("g_norm3", total(ggn, (slice(2, 3),)))
        finish("g_sgu_ln", total(gln, (slice(0, 1),)))
        finish("b_sgu_ln", total(gln, (slice(1, 2),)))
        tril = lax.broadcasted_iota(jnp.int32, (BLK, BLK), 0) >= lax.broadcasted_iota(jnp.int32, (BLK, BLK), 1)
        for g in range(N_GRP):
            finish("w_spatial", jnp.where(tril, total(gws, (g,)), 0.0), (0, g))
            finish("b_spatial", jnp.sum(total(gbs, (g,)).T, axis=0, keepdims=True), (0, slice(g, g + 1)))
        finish("g_q", total(ggq, ()))
        finish("g_k", total(ggk, ()))
        finish("attn_sinks", total(gsk, (slice(0, 1), slice(0, N_KV * Q_PER_KV))))
        refs[-1][...] = total(gloss, ())

    args = list(gathered) + [params[k] for k in SMALL] + [ms[k] for k in SMALL] + [vs[k] for k in SMALL]
    out_shape = []
    for k in SMALL:
        out_shape += [jax.ShapeDtypeStruct(params[k].shape, F32)] * 4
    out_shape.append(jax.ShapeDtypeStruct((8, 128), F32))
    res = pl.pallas_call(body, name="small_update", grid=(1,), in_specs=[_whole(a) for a in args],
                         out_specs=[_whole(o) for o in out_shape], out_shape=out_shape,
                         compiler_params=_params("arbitrary"))(*args)
    return {k: res[4 * i:4 * i + 4] for i, k in enumerate(SMALL)}, res[-1][0, 0]


def _rs_add(p, r, core, name):
    _, rows, width = p.shape

    def body(c_ref, p_ref, r_ref, o_ref):
        o_ref[...] = (p_ref[...] + r_ref[...]).astype(BF16)

    return pl.pallas_call(
        body, name="rs_add_" + name, out_shape=jax.ShapeDtypeStruct((4, rows, width), BF16),
        grid_spec=pltpu.PrefetchScalarGridSpec(
            num_scalar_prefetch=1, grid=(4,),
            in_specs=[pl.BlockSpec((1, rows, width), lambda k, c_ref: (2 * k + c_ref[0], 0, 0)),
                      pl.BlockSpec((1, rows, width), lambda k, c_ref: (k, 0, 0))],
            out_specs=pl.BlockSpec((1, rows, width), lambda k, c_ref: (k, 0, 0))),
        compiler_params=_params("parallel"),
    )(core, p, r)


def _adam_big(w, m, v, r, name):
    rows, cols = w.shape

    def body(r_ref, w_ref, m_ref, v_ref, go_ref, d_ref, nm_ref, nv_ref):
        g = r_ref[0].astype(F32)
        for k in range(1, 4):
            g = g + r_ref[k].astype(F32)
        go_ref[...] = g
        d_ref[...], nm_ref[...], nv_ref[...] = _adamw(w_ref[...], g, m_ref[...], v_ref[...])

    br = rows // 2
    shard = pl.BlockSpec((br, cols), lambda i: (i, 0))
    return pl.pallas_call(body, name="adam_" + name, grid=(2,),
                          in_specs=[pl.BlockSpec((4, br, cols), lambda i: (0, i, 0)), shard, shard, shard],
                          out_specs=[shard] * 4, out_shape=[jax.ShapeDtypeStruct(w.shape, F32)] * 4,
                          compiler_params=_params("parallel"))(r, w, m, v)


def kernel(x, c, w_ada, b_ada, g_norm1, ffn1_w_gate, ffn1_w_up, ffn1_w_down, g_norm2, w_in, g_sgu_ln, b_sgu_ln, w_spatial, b_spatial, g_q, g_k, attn_sinks, w_branch_a, w_branch_b, w_out, g_norm3, ffn2_w_gate, ffn2_w_up, ffn2_w_down, loss_target, m_w_ada, m_b_ada, m_g_norm1, m_ffn1_w_gate, m_ffn1_w_up, m_ffn1_w_down, m_g_norm2, m_w_in, m_g_sgu_ln, m_b_sgu_ln, m_w_spatial, m_b_spatial, m_g_q, m_g_k, m_attn_sinks, m_w_branch_a, m_w_branch_b, m_w_out, m_g_norm3, m_ffn2_w_gate, m_ffn2_w_up, m_ffn2_w_down, v_w_ada, v_b_ada, v_g_norm1, v_ffn1_w_gate, v_ffn1_w_up, v_ffn1_w_down, v_g_norm2, v_w_in, v_g_sgu_ln, v_b_sgu_ln, v_w_spatial, v_b_spatial, v_g_q, v_g_k, v_attn_sinks, v_w_branch_a, v_w_branch_b, v_w_out, v_g_norm3, v_ffn2_w_gate, v_ffn2_w_up, v_ffn2_w_down):
    names = ("w_ada", "b_ada", "g_norm1", "ffn1_w_gate", "ffn1_w_up", "ffn1_w_down", "g_norm2", "w_in", "g_sgu_ln",
             "b_sgu_ln", "w_spatial", "b_spatial", "g_q", "g_k", "attn_sinks", "w_branch_a", "w_branch_b", "w_out",
             "g_norm3", "ffn2_w_gate", "ffn2_w_up", "ffn2_w_down")
    w = dict(zip(names, (w_ada, b_ada, g_norm1, ffn1_w_gate, ffn1_w_up, ffn1_w_down, g_norm2, w_in, g_sgu_ln,
                         b_sgu_ln, w_spatial, b_spatial, g_q, g_k, attn_sinks, w_branch_a, w_branch_b, w_out, g_norm3,
                         ffn2_w_gate, ffn2_w_up, ffn2_w_down)))
    m = dict(zip(names, (m_w_ada, m_b_ada, m_g_norm1, m_ffn1_w_gate, m_ffn1_w_up, m_ffn1_w_down, m_g_norm2, m_w_in,
                         m_g_sgu_ln, m_b_sgu_ln, m_w_spatial, m_b_spatial, m_g_q, m_g_k, m_attn_sinks, m_w_branch_a,
                         m_w_branch_b, m_w_out, m_g_norm3, m_ffn2_w_gate, m_ffn2_w_up, m_ffn2_w_down)))
    v = dict(zip(names, (v_w_ada, v_b_ada, v_g_norm1, v_ffn1_w_gate, v_ffn1_w_up, v_ffn1_w_down, v_g_norm2, v_w_in,
                         v_g_sgu_ln, v_b_sgu_ln, v_w_spatial, v_b_spatial, v_g_q, v_g_k, v_attn_sinks, v_w_branch_a,
                         v_w_branch_b, v_w_out, v_g_norm3, v_ffn2_w_gate, v_ffn2_w_up, v_ffn2_w_down)))
    B, seq, _ = x.shape
    T = B * seq
    nbs = seq // BLK
    xi, yi, ci = _place()
    me = 4 * xi + 2 * yi + ci
    core = jnp.reshape(ci, (1,)).astype(jnp.int32)
    layout = {name: (tform, n, width) for name, tform, n, width in BIG}
    shard = {name: w[name][0].astype(BF16).T if tform else w[name][0].astype(BF16) for name, tform, _, _ in BIG}
    wts, parts, big_out = {}, {}, {}

    def gather_plan(group):
        return _Gather([shard[k] for k in group])

    def take(group, gathered):
        for k, g in zip(group, gathered):
            wts[k] = g.reshape(N_DEV * layout[k][1], layout[k][2])

    def blocks(name, grad):
        return grad.reshape(N_DEV, layout[name][1], layout[name][2])

    def to_sibling(names, grads):
        return _RsSibling([blocks(k, g) for k, g in zip(names, grads)])

    def add(names, grads, from_sibling):
        for k, g, r in zip(names, grads, from_sibling):
            parts[k] = _rs_add(blocks(k, g), r, core, k)

    def to_chips(names):
        return _RsChips([parts[k] for k in names])

    def update(names, from_chips):
        for k, r in zip(names, from_chips):
            if layout[k][0]:
                big_out[k] = [o.T[None] for o in _adam_big(w[k][0].T, m[k][0].T, v[k][0].T, r, k)]
            else:
                big_out[k] = [o[None] for o in _adam_big(w[k][0], m[k][0], v[k][0], r, k)]

    g1, g2, g3, g_ln, b_ln = g_norm1, g_norm2, g_norm3, g_sgu_ln, b_sgu_ln
    gq2, gk2 = jnp.tile(g_q, (1, 2)), jnp.tile(g_k, (1, 2))
    ws = w_spatial[0]
    bsb = jnp.broadcast_to(b_spatial[0][:, :, None], (N_GRP, BLK, BLK))
    xf = x.reshape(T, D)
    tgt = loss_target.reshape(T, D)

    ffn1 = ("ffn1_w_gate", "ffn1_w_up", "ffn1_w_down")
    ffn2 = ("ffn2_w_gate", "ffn2_w_up", "ffn2_w_down")
    c_pad = jnp.concatenate([c, jnp.zeros((8 - B, D), F32)], axis=0)
    c_all, mods_cols, gathered = _prologue(c_pad, w_ada[0], [shard[k] for k in ffn1[:2]])
    take(ffn1[:2], gathered)
    c_all = c_all[:, :B].reshape(N_DEV * B, D)
    mine = lax.dynamic_slice_in_dim(mods_cols, 8 * me, B, axis=1)
    mods = (mine.transpose(1, 0, 2).reshape(B, 9 * D) + b_ada).reshape(B, 9, D)

    group = ("ffn1_w_down", "w_branch_a", "w_branch_b", "w_out")
    (gb1, ub1), (gathered,) = _ffn_gate_up(xf, mods, g1, wts["ffn1_w_gate"], wts["ffn1_w_up"], 0, seq,
                                           plans=[gather_plan(group)])
    take(group, gathered)
    group = ("w_in",)
    (h1, y1), (gathered,) = _ffn_down(xf, gb1, ub1, mods, wts["ffn1_w_down"], 0, seq, plans=[gather_plan(group)])
    take(group, gathered)
    group = ("ffn2_w_gate",)
    (au, av, q_tok, k_tok, v_tok, gta, gtb), (gathered,) = _inproj_fwd(h1, mods, g2, wts["w_in"], seq,
                                                                       plans=[gather_plan(group)])
    take(group, gathered)
    group = ("ffn2_w_up",)
    ob, (gathered,) = _swa_fwd(q_tok, k_tok, v_tok, gq2, gk2, attn_sinks, nbs, plans=[gather_plan(group)])
    take(group, gathered)
    mixw = (wts["w_branch_a"], wts["w_branch_b"], wts["w_out"])
    group = ("ffn2_w_down",)
    h2, (gathered,) = _mix_fwd(au, av, gta, gtb, ob, h1, mods, g_ln, b_ln, ws, bsb, *mixw, seq,
                               plans=[gather_plan(group)])
    take(group, gathered)
    (dh3, y3, gb3, ub3, loss_parts), _ = _ffn_fwd(h2, mods, g3, *[wts[k] for k in ffn2], 6, seq, tgt=tgt)
    loss_part = jnp.full((8, 128), jnp.sum(loss_parts[:, 0, 0]))

    (dh2, xb, dyb, a, dg, du, mg3), _ = _ffn_bwd(h2, dh3, y3, gb3, ub3, mods, g3, *[wts[k] for k in ffn2], 6, seq)
    d_gate, _ = _wgrad(dg, xb, "wgrad_ffn2_gate")
    d_up, (r,) = _wgrad(du, xb, "wgrad_ffn2_up", plans=[to_sibling(ffn2[:1], [d_gate])])
    add(ffn2[:1], [d_gate], r)
    d_down, (r,) = _wgrad(a, dyb, "wgrad_ffn2_down", plans=[to_sibling(ffn2[1:2], [d_up])])
    add(ffn2[1:2], [d_up], r)

    (dau, dav, dgta, dgtb, dob, d_out, d_a, d_b, dws, dbs, dln, mg2g), (r, from_chips) = _mix_bwd(
        au, av, gta, gtb, ob, dh2, mods, g_ln, b_ln, ws, bsb, *mixw, seq,
        plans=[to_sibling(ffn2[2:], [d_down]), to_chips(ffn2[:2])])
    add(ffn2[2:], [d_down], r)
    update(ffn2[:2], from_chips)
    mixers = ("w_out", "w_branch_a", "w_branch_b")
    (dq, dk, dv, dk_halo, dv_halo, dgq2, dgk2, dsk), (from_chips, r) = _swa_bwd(
        q_tok, k_tok, v_tok, gq2, gk2, attn_sinks, dob, nbs,
        plans=[to_chips(ffn2[2:]), to_sibling(mixers, [d_out, d_a, d_b])])
    update(ffn2[2:], from_chips)
    add(mixers, [d_out, d_a, d_b], r)
    dk, dv = _swa_add_halo(dk, dk_halo), _swa_add_halo(dv, dv_halo)
    dgq, dgk = dgq2[:, :HD] + dgq2[:, HD:], dgk2[:, :HD] + dgk2[:, HD:]
    (dh1, xb2, dpb, mg2), (from_chips, early) = _inproj_bwd(
        h1, dh2, (dau, dav, dq, dk, dv, dgta, dgtb), mods, g2, wts["w_in"], seq,
        plans=[to_chips(mixers), _Gather([dln, dws, dbs, dgq, dgk, dsk, loss_part])])
    update(mixers, from_chips)

    (dx, xb, dyb, a, dg, du, mg1), _ = _ffn_bwd(xf, dh1, y1, gb1, ub1, mods, g1, *[wts[k] for k in ffn1], 0, seq)
    dmods, dgn = _mod_finish(mg1, mg2, mg2g, mg3, mods, g1, g2, g3)
    d_gate, (late,) = _wgrad(dg, xb, "wgrad_ffn1_gate", plans=[_GatherDirect([dmods.reshape(B, 9 * D), dgn])])
    gathered = late + early
    d_up, (r,) = _wgrad(du, xb, "wgrad_ffn1_up", plans=[to_sibling(ffn1[:1], [d_gate])])
    add(ffn1[:1], [d_gate], r)
    d_in, (r, from_chips) = _wgrad(dpb, xb2, "wgrad_w_in",
                                   plans=[to_sibling(ffn1[1:2], [d_up]), to_chips(ffn1[:1])])
    add(ffn1[1:2], [d_up], r)
    update(ffn1[:1], from_chips)
    d_down, (r, from_chips) = _wgrad(a, dyb, "wgrad_ffn1_down",
                                     plans=[to_sibling(("w_in",), [d_in]), to_chips(ffn1[1:2])])
    add(("w_in",), [d_in], r)
    update(ffn1[1:2], from_chips)
    small_out, loss = _small_update(gathered, w, m, v)
    dm_cols = lax.dynamic_slice_in_dim(gathered[0].reshape(N_DEV * B, 9 * D), (9 * D // N_DEV) * me,
                                       9 * D // N_DEV, axis=1)
    ada_out, (r, from_chips) = _ada_update(c_all, dm_cols, w_ada[0], m_w_ada[0], v_w_ada[0],
                                           plans=[to_sibling(ffn1[2:], [d_down]), to_chips(("w_in",))])
    add(ffn1[2:], [d_down], r)
    update(("w_in",), from_chips)
    (from_chips,) = _exchange([to_chips(ffn1[2:])], "rs_last")
    update(ffn1[2:], from_chips)

    def leaf(kind, name):
        if name == "w_ada":
            return ada_out[kind][None]
        if name in SMALL:
            return small_out[name][kind]
        return big_out[name][kind]

    return (loss, dx.reshape(B, seq, D), *[leaf(kind, name) for kind in range(4) for name in names])
```

```python
import functools
import math

import jax
import jax.numpy as jnp
from jax import lax
from jax.experimental import pallas as pl
from jax.experimental.pallas import tpu as pltpu

F32 = jnp.float32
BF16 = jnp.bfloat16
MESH = pl.DeviceIdType.MESH
AXES = ("x", "y", "c")
N_DEV = 8

VMEM_LIMIT = 56 * 1024 * 1024

D = 1024
FF = 2816
FC = 1408
D_A = 512
D_B = 512
HD = 64
N_KV = 2
Q_PER_KV = 4
BLK = 128
N_GRP = 4
IN_COLS = 3840
PIECES = (("au", 0, 512), ("av", 512, 512), ("q", 1024, 512), ("k", 1536, 128), ("v", 1664, 128),
          ("ga", 1792, 1024), ("gb", 2816, 1024))
EPS = 1e-6
NEG = -1e30
GELU_C = math.sqrt(2.0 / math.pi)

ADAM_LR = 0.001
ADAM_B1 = 0.9
ADAM_B2 = 0.999
ADAM_EPS = 1e-08
ADAM_WD = 0.01
ADAM_STEP = 10

NT = (((1,), (1,)), ((), ()))
TN = (((0,), (0,)), ((), ()))

BIG = (("ffn1_w_gate", True, FF // N_DEV, D), ("ffn1_w_up", True, FF // N_DEV, D),
       ("ffn1_w_down", False, FF // N_DEV, D), ("w_in", True, IN_COLS // N_DEV, D),
       ("w_branch_a", True, D // N_DEV, D_A), ("w_branch_b", True, D // N_DEV, D_B), ("w_out", False, D // N_DEV, D),
       ("ffn2_w_gate", True, FF // N_DEV, D), ("ffn2_w_up", True, FF // N_DEV, D),
       ("ffn2_w_down", False, FF // N_DEV, D))
SMALL = ("b_ada", "g_norm1", "g_norm2", "g_sgu_ln", "b_sgu_ln", "w_spatial", "b_spatial", "g_q", "g_k",
         "attn_sinks", "g_norm3")


def _dot(a, b):
    return jnp.dot(a, b, preferred_element_type=F32)


def _dot_nt(a, b):
    return lax.dot_general(a, b, NT, preferred_element_type=F32)


def _dot_tn(a, b):
    return lax.dot_general(a, b, TN, preferred_element_type=F32)


def _vmem():
    return pl.BlockSpec(memory_space=pltpu.VMEM)


def _any():
    return pl.BlockSpec(memory_space=pl.ANY)


def _whole(a):
    return pl.BlockSpec(a.shape, lambda i: (0,) * len(a.shape))


def _rms_mod(h, g, sh, sc):
    inv = lax.rsqrt(jnp.mean(h * h, axis=-1, keepdims=True) + EPS)
    r = h * inv
    return (r * g) * (1.0 + sc) + sh, r, inv


def _rms_mod_bwd(dxn, r, inv, g, sc):
    dr = dxn * (g * (1.0 + sc))
    dh = inv * (dr - r * jnp.mean(dr * r, axis=-1, keepdims=True))
    return dh, jnp.sum(dxn, axis=0, keepdims=True), jnp.sum(dxn * r, axis=0, keepdims=True)


def _gelu(x):
    t = jnp.tanh(GELU_C * (x + 0.044715 * (x * x * x)))
    return 0.5 * x * (1.0 + t), t


def _gelu_grad(x, t):
    return 0.5 * (1.0 + t) + 0.5 * x * (1.0 - t * t) * (GELU_C * (1.0 + 3.0 * 0.044715 * x * x))


def _adamw(w, g, m, v):
    m = ADAM_B1 * m + (1.0 - ADAM_B1) * g
    v = ADAM_B2 * v + (1.0 - ADAM_B2) * (g * g)
    m_hat = m / (1.0 - ADAM_B1 ** ADAM_STEP)
    v_hat = v / (1.0 - ADAM_B2 ** ADAM_STEP)
    delta = -ADAM_LR * (m_hat / (jnp.sqrt(v_hat) + ADAM_EPS) + ADAM_WD * w)
    return delta, m, v


def _token_tile(seq, cap=512):
    return min(cap, seq)


def _params(*semantics):
    return pltpu.CompilerParams(dimension_semantics=semantics, vmem_limit_bytes=VMEM_LIMIT)


def _place():
    return lax.axis_index("x"), lax.axis_index("y"), lax.axis_index("c")


class _Gather:
    def __init__(self, arrays):
        n = len(arrays)
        self.ins = list(arrays)
        self.out_shape = [jax.ShapeDtypeStruct((N_DEV,) + a.shape, a.dtype) for a in arrays]
        self.scratch = [pltpu.SemaphoreType.DMA((n, 7)), pltpu.SemaphoreType.DMA((n, 7)),
                        pltpu.SemaphoreType.DMA((n,))]

    def _copies(self, ins, outs, sems):
        send_sems, recv_sems, local_sems = sems
        n = len(ins)
        x, y, c = _place()
        me, sibling = (x, y, c), (x, y, 1 - c)
        chips = [(1 - x, y), (x, 1 - y), (1 - x, 1 - y)]

        def slot(a, px, py, pc):
            return outs[a].at[4 * px + 2 * py + pc]

        def copy(a, k, block, to, src=None):
            return pltpu.make_async_remote_copy(
                src_ref=slot(a, *block) if src is None else src, dst_ref=slot(a, *block),
                send_sem=send_sems.at[a, k], recv_sem=recv_sems.at[a, k], device_id=to, device_id_type=MESH)

        mine = [pltpu.make_async_copy(ins[a], slot(a, *me), local_sems.at[a]) for a in range(n)]
        first = [copy(a, 0, me, sibling, src=ins[a]) for a in range(n)]
        first += [copy(a, 1 + j, me, (*chip, c), src=ins[a]) for a in range(n) for j, chip in enumerate(chips)]
        landed = [[copy(a, 1 + j, (*chip, c), me) for a in range(n)] for j, chip in enumerate(chips)]
        passed = [[copy(a, 4 + j, (*chip, c), sibling) for a in range(n)] for j, chip in enumerate(chips)]
        from_sibling = [copy(a, 0, sibling, me) for a in range(n)]
        from_sibling += [copy(a, 4 + j, (*chip, 1 - c), me) for a in range(n) for j, chip in enumerate(chips)]
        return mine, first, landed, passed, from_sibling

    def start(self, ins, outs, sems):
        mine, first, _, _, _ = self._copies(ins, outs, sems)
        for cp in mine + first:
            cp.start()

    def relay(self, ins, outs, sems):
        _, _, landed, passed, _ = self._copies(ins, outs, sems)
        for arrivals, forwards in zip(landed, passed):
            for arrival, forward in zip(arrivals, forwards):
                arrival.wait_recv()
                forward.start()

    def finish(self, ins, outs, sems):
        mine, first, _, passed, from_sibling = self._copies(ins, outs, sems)
        for cp in from_sibling:
            cp.wait_recv()
        for cp in first + [f for fs in passed for f in fs]:
            cp.wait_send()
        for cp in mine:
            cp.wait()


class _GatherRelayed:
    def __init__(self, arrays):
        n = len(arrays)
        self.ins = list(arrays)
        self.out_shape = [jax.ShapeDtypeStruct((N_DEV,) + a.shape, a.dtype) for a in arrays]
        self.scratch = [pltpu.SemaphoreType.DMA((n, 7)), pltpu.SemaphoreType.DMA((n, 7)),
                        pltpu.SemaphoreType.DMA((n,))]

    def _copies(self, ins, outs, sems):
        send_sems, recv_sems, local_sems = sems
        n = len(ins)
        x, y, c = _place()
        me, sibling = (x, y, c), (x, y, 1 - c)
        south = c == 0
        via = (jnp.where(south, 1 - x, x), jnp.where(south, y, 1 - y))
        onward = (jnp.where(south, x, 1 - x), jnp.where(south, 1 - y, y))
        via_sem = jnp.where(south, 1, 2)

        def slot(a, px, py, pc):
            return outs[a].at[4 * px + 2 * py + pc]

        def copy(a, k, block, to, src=None):
            return pltpu.make_async_remote_copy(
                src_ref=slot(a, *block) if src is None else src, dst_ref=slot(a, *block),
                send_sem=send_sems.at[a, k], recv_sem=recv_sems.at[a, k], device_id=to, device_id_type=MESH)

        chips = [(1 - x, y), (x, 1 - y), (1 - x, 1 - y)]
        mine = [pltpu.make_async_copy(ins[a], slot(a, *me), local_sems.at[a]) for a in range(n)]
        first = [copy(a, 0, me, sibling, src=ins[a]) for a in range(n)]
        first += [copy(a, 1 + j, me, (*chips[j], c), src=ins[a]) for a in range(n) for j in range(2)]
        to_pass = [copy(a, via_sem, (*via, c), me) for a in range(n)]
        passed_on = [copy(a, 3, (*via, c), (*onward, c)) for a in range(n)]
        landed = [[copy(a, 1 + j, (*chips[j], c), me) for a in range(n)] for j in range(3)]
        to_sibling = [[copy(a, 4 + j, (*chips[j], c), sibling) for a in range(n)] for j in range(3)]
        from_sibling = [copy(a, 0, sibling, me) for a in range(n)]
        from_sibling += [copy(a, 4 + j, (*chips[j], 1 - c), me) for a in range(n) for j in range(3)]
        return mine, first, to_pass, passed_on, landed, to_sibling, from_sibling

    def start(self, ins, outs, sems):
        mine, first, _, _, _, _, _ = self._copies(ins, outs, sems)
        for cp in mine + first:
            cp.start()

    def relay(self, ins, outs, sems):
        _, _, to_pass, passed_on, landed, to_sibling, _ = self._copies(ins, outs, sems)
        for arrival, onward in zip(to_pass, passed_on):
            arrival.wait_recv()
            onward.start()
        x, y, c = _place()
        for j in range(3):
            for a, (arrival, forward) in enumerate(zip(landed[j], to_sibling[j])):
                if j < 2:
                    pl.when((c == 0) != (j == 0))(arrival.wait_recv)
                else:
                    arrival.wait_recv()
                forward.start()

    def finish(self, ins, outs, sems):
        mine, first, _, passed_on, _, to_sibling, from_sibling = self._copies(ins, outs, sems)
        for cp in from_sibling:
            cp.wait_recv()
        for cp in first + passed_on + [f for fs in to_sibling for f in fs]:
            cp.wait_send()
        for cp in mine:
            cp.wait()


class _GatherDirect:
    def __init__(self, arrays):
        n = len(arrays)
        self.ins = list(arrays)
        self.out_shape = [jax.ShapeDtypeStruct((N_DEV,) + a.shape, a.dtype) for a in arrays]
        self.scratch = [pltpu.SemaphoreType.DMA((n, 7)), pltpu.SemaphoreType.DMA((n, 7)),
                        pltpu.SemaphoreType.DMA((n,))]

    def _copies(self, ins, outs, sems):
        send_sems, recv_sems, local_sems = sems
        n = len(ins)
        x, y, c = _place()
        peers = [(x ^ (k >> 2 & 1), y ^ (k >> 1 & 1), c ^ (k & 1)) for k in range(1, N_DEV)]

        def slot(a, px, py, pc):
            return outs[a].at[4 * px + 2 * py + pc]

        def copy(a, k, block, to, src=None):
            return pltpu.make_async_remote_copy(
                src_ref=slot(a, *block) if src is None else src, dst_ref=slot(a, *block),
                send_sem=send_sems.at[a, k], recv_sem=recv_sems.at[a, k], device_id=to, device_id_type=MESH)

        mine = [pltpu.make_async_copy(ins[a], slot(a, x, y, c), local_sems.at[a]) for a in range(n)]
        sends = [copy(a, k, (x, y, c), peer, src=ins[a]) for a in range(n) for k, peer in enumerate(peers)]
        arrivals = [copy(a, k, peer, (x, y, c)) for a in range(n) for k, peer in enumerate(peers)]
        return mine, sends, arrivals

    def start(self, ins, outs, sems):
        mine, sends, _ = self._copies(ins, outs, sems)
        for cp in mine + sends:
            cp.start()

    def relay(self, ins, outs, sems):
        pass

    def finish(self, ins, outs, sems):
        mine, sends, arrivals = self._copies(ins, outs, sems)
        for cp in arrivals:
            cp.wait_recv()
        for cp in sends:
            cp.wait_send()
        for cp in mine:
            cp.wait()


class _RsSibling:
    def __init__(self, ps):
        n = len(ps)
        self.ins = list(ps)
        self.out_shape = [jax.ShapeDtypeStruct((4,) + p.shape[1:], p.dtype) for p in ps]
        self.scratch = [pltpu.SemaphoreType.DMA((n, 4)), pltpu.SemaphoreType.DMA((n, 4))]

    def _copies(self, ins, outs, sems):
        send_sems, recv_sems = sems
        x, y, c = _place()
        return [pltpu.make_async_remote_copy(
            src_ref=ins[a].at[2 * q + (1 - c)], dst_ref=outs[a].at[q], send_sem=send_sems.at[a, q],
            recv_sem=recv_sems.at[a, q], device_id=(x, y, 1 - c), device_id_type=MESH)
            for a in range(len(ins)) for q in range(4)]

    def start(self, ins, outs, sems):
        for cp in self._copies(ins, outs, sems):
            cp.start()

    def relay(self, ins, outs, sems):
        pass

    def finish(self, ins, outs, sems):
        for cp in self._copies(ins, outs, sems):
            cp.wait()


class _RsChips:
    def __init__(self, qs):
        n = len(qs)
        self.ins = list(qs)
        self.out_shape = [jax.ShapeDtypeStruct(q.shape, q.dtype) for q in qs]
        self.scratch = [pltpu.SemaphoreType.DMA((n, 3)), pltpu.SemaphoreType.DMA((n, 3)),
                        pltpu.SemaphoreType.DMA((n,))]

    def _copies(self, ins, outs, sems):
        send_sems, recv_sems, local_sems = sems
        n = len(ins)
        x, y, c = _place()
        my_chip = 2 * x + y
        chips = [(1 - x, y), (x, 1 - y), (1 - x, 1 - y)]

        def copy(a, j, src_slot, dst_slot):
            px, py = chips[j]
            return pltpu.make_async_remote_copy(
                src_ref=ins[a].at[src_slot], dst_ref=outs[a].at[dst_slot], send_sem=send_sems.at[a, j],
                recv_sem=recv_sems.at[a, j], device_id=(px, py, c), device_id_type=MESH)

        own = [pltpu.make_async_copy(ins[a].at[my_chip], outs[a].at[my_chip], local_sems.at[a]) for a in range(n)]
        sends = [copy(a, j, 2 * px + py, my_chip) for a in range(n) for j, (px, py) in enumerate(chips)]
        arrivals = [copy(a, j, my_chip, 2 * px + py) for a in range(n) for j, (px, py) in enumerate(chips)]
        return own, sends, arrivals

    def start(self, ins, outs, sems):
        own, sends, _ = self._copies(ins, outs, sems)
        for cp in own + sends:
            cp.start()

    def relay(self, ins, outs, sems):
        pass

    def finish(self, ins, outs, sems):
        own, sends, arrivals = self._copies(ins, outs, sems)
        for cp in arrivals:
            cp.wait_recv()
        for cp in sends:
            cp.wait_send()
        for cp in own:
            cp.wait()


def _split_plans(plans, refs_in, refs_out, refs_scr, phase):
    i = o = s = 0
    for p in plans:
        ni, no, ns = len(p.ins), len(p.out_shape), len(p.scratch)
        getattr(p, phase)(refs_in[i:i + ni], refs_out[o:o + no], refs_scr[s:s + ns])
        i, o, s = i + ni, o + no, s + ns


def _plan_results(plans, res):
    out = []
    for p in plans:
        out.append(list(res[:len(p.out_shape)]))
        res = res[len(p.out_shape):]
    return out


def _exchange(plans, name):
    c_in = [a for p in plans for a in p.ins]
    c_out = [s for p in plans for s in p.out_shape]
    c_scr = [s for p in plans for s in p.scratch]

    def body(*refs):
        cin, cout, cscr = refs[:len(c_in)], refs[len(c_in):len(c_in) + len(c_out)], refs[len(c_in) + len(c_out):]
        for phase in ("start", "relay", "finish"):
            _split_plans(plans, cin, cout, cscr, phase)

    res = pl.pallas_call(body, name=name, in_specs=[_any()] * len(c_in), out_specs=[_any()] * len(c_out),
                         out_shape=c_out, scratch_shapes=c_scr)(*c_in)
    return _plan_results(plans, res)


def _call(body, *, name, grid, in_specs, out_specs, out_shape, args, semantics, scratch_shapes=(), plans=()):
    n_in, n_out, n_scr = len(in_specs), len(out_specs), len(scratch_shapes)
    c_in = [a for p in plans for a in p.ins]
    c_out = [s for p in plans for s in p.out_shape]
    c_scr = [s for p in plans for s in p.scratch]
    n_steps = math.prod(grid)

    def wrapped(*refs):
        ins, refs = refs[:n_in], refs[n_in:]
        cin, refs = refs[:len(c_in)], refs[len(c_in):]
        outs, refs = refs[:n_out], refs[n_out:]
        cout, refs = refs[:len(c_out)], refs[len(c_out):]
        scr, cscr = refs[:n_scr], refs[n_scr:]
        if plans:
            step = 0
            for d, g in enumerate(grid):
                step = step * g + pl.program_id(d)
            pl.when(step == 0)(lambda: _split_plans(plans, cin, cout, cscr, "start"))
        body(*ins, *outs, *scr)
        if plans:
            pl.when(step == max(n_steps - 2, 0))(lambda: _split_plans(plans, cin, cout, cscr, "relay"))
            pl.when(step == n_steps - 1)(lambda: _split_plans(plans, cin, cout, cscr, "finish"))

    res = pl.pallas_call(
        wrapped, name=name, grid=grid, in_specs=list(in_specs) + [_any()] * len(c_in),
        out_specs=list(out_specs) + [_any()] * len(c_out), out_shape=list(out_shape) + c_out,
        scratch_shapes=list(scratch_shapes) + c_scr,
        compiler_params=_params(*(("arbitrary",) * len(grid) if plans else semantics)),
    )(*args, *c_in)
    return list(res[:n_out]), _plan_results(plans, res[n_out:])


def _ffn_fwd(h, mods, gn, wg, wu, wd, row0, seq, tgt=None, plans=()):
    T = h.shape[0]
    tm = _token_tile(seq, 256)
    tps = seq // tm
    n_t = T // tm
    with_loss = tgt is not None

    def body(h_ref, m_ref, g_ref, wg_ref, wu_ref, wd_ref, *rest):
        if with_loss:
            tgt_ref, out_ref, y_ref, gb_ref, ub_ref, loss_ref = rest
        else:
            out_ref, y_ref, gb_ref, ub_ref = rest
        hv = h_ref[...]
        sh = m_ref[0, row0:row0 + 1, :]
        sc = m_ref[0, row0 + 1:row0 + 2, :]
        ga = m_ref[0, row0 + 2:row0 + 3, :]
        xn, _, _ = _rms_mod(hv, g_ref[...], sh, sc)
        xb = xn.astype(BF16)
        acc = jnp.zeros((tm, D), F32)
        for c0 in range(0, FF, FC):
            gg = _dot_nt(xb, wg_ref[c0:c0 + FC, :])
            uu = _dot_nt(xb, wu_ref[c0:c0 + FC, :])
            gg, uu = gg.astype(BF16), uu.astype(BF16)
            gb_ref[:, c0:c0 + FC] = gg
            ub_ref[:, c0:c0 + FC] = uu
            acc = acc + _dot((gg * jax.nn.sigmoid(gg)) * uu, wd_ref[c0:c0 + FC, :])
        y_ref[...] = acc
        hout = hv + (0.5 * ga) * acc
        if with_loss:
            d = hout - tgt_ref[...]
            out_ref[...] = d * (1.0 / D)
            loss_ref[...] = jnp.full((1, 8, 128), 0.5 / D, F32) * jnp.sum(d * d)
        else:
            out_ref[...] = hout

    tok = pl.BlockSpec((tm, D), lambda i: (i, 0))
    tokf = pl.BlockSpec((tm, FF), lambda i: (i, 0))
    in_specs = [tok, pl.BlockSpec((1, 9, D), lambda i: (i // tps, 0, 0)), pl.BlockSpec((1, D), lambda i: (0, 0)),
                _vmem(), _vmem(), _vmem()]
    out_shape = [jax.ShapeDtypeStruct((T, D), F32), jax.ShapeDtypeStruct((T, D), F32),
                 jax.ShapeDtypeStruct((T, FF), BF16), jax.ShapeDtypeStruct((T, FF), BF16)]
    out_specs = [tok, tok, tokf, tokf]
    args = [h, mods, gn, wg, wu, wd]
    if with_loss:
        in_specs.append(tok)
        args.append(tgt)
        out_shape.append(jax.ShapeDtypeStruct((n_t, 8, 128), F32))
        out_specs.append(pl.BlockSpec((1, 8, 128), lambda i: (i, 0, 0)))
    return _call(body, name="ffn_fwd_loss" if with_loss else "ffn_fwd", grid=(n_t,), in_specs=in_specs,
                 out_specs=out_specs, out_shape=out_shape, args=args, semantics=("parallel",), plans=plans)


def _ffn_gate_up(h, mods, gn, wg, wu, row0, seq, plans=()):
    T = h.shape[0]
    tm = _token_tile(seq, 512)
    tps = seq // tm

    def body(h_ref, m_ref, g_ref, wg_ref, wu_ref, gb_ref, ub_ref):
        xn, _, _ = _rms_mod(h_ref[...], g_ref[...], m_ref[0, row0:row0 + 1, :], m_ref[0, row0 + 1:row0 + 2, :])
        xb = xn.astype(BF16)
        for c0 in range(0, FF, FC):
            gb_ref[:, c0:c0 + FC] = _dot_nt(xb, wg_ref[c0:c0 + FC, :]).astype(BF16)
            ub_ref[:, c0:c0 + FC] = _dot_nt(xb, wu_ref[c0:c0 + FC, :]).astype(BF16)

    tokf = pl.BlockSpec((tm, FF), lambda i: (i, 0))
    return _call(
        body, name="ffn_gate_up", grid=(T // tm,),
        in_specs=[pl.BlockSpec((tm, D), lambda i: (i, 0)), pl.BlockSpec((1, 9, D), lambda i: (i // tps, 0, 0)),
                  pl.BlockSpec((1, D), lambda i: (0, 0)), _vmem(), _vmem()],
        out_specs=[tokf, tokf], out_shape=[jax.ShapeDtypeStruct((T, FF), BF16)] * 2, args=(h, mods, gn, wg, wu),
        semantics=("parallel",), plans=plans)


def _ffn_down(h, gb, ub, mods, wd, row0, seq, plans=()):
    T = h.shape[0]
    tm = _token_tile(seq, 512)
    tps = seq // tm

    def body(h_ref, gb_ref, ub_ref, m_ref, wd_ref, out_ref, y_ref):
        acc = jnp.zeros((tm, D), F32)
        for c0 in range(0, FF, FC):
            gg = gb_ref[:, c0:c0 + FC]
            acc = acc + _dot((gg * jax.nn.sigmoid(gg)) * ub_ref[:, c0:c0 + FC], wd_ref[c0:c0 + FC, :])
        y_ref[...] = acc
        out_ref[...] = h_ref[...] + (0.5 * m_ref[0, row0 + 2:row0 + 3, :]) * acc

    tok = pl.BlockSpec((tm, D), lambda i: (i, 0))
    tokf = pl.BlockSpec((tm, FF), lambda i: (i, 0))
    return _call(
        body, name="ffn_down", grid=(T // tm,),
        in_specs=[tok, tokf, tokf, pl.BlockSpec((1, 9, D), lambda i: (i // tps, 0, 0)), _vmem()],
        out_specs=[tok, tok], out_shape=[jax.ShapeDtypeStruct((T, D), F32)] * 2, args=(h, gb, ub, mods, wd),
        semantics=("parallel",), plans=plans)


def _ffn_bwd(h, dhn, y, gb, ub, mods, gn, wg, wu, wd, row0, seq, plans=()):
    T = h.shape[0]
    B = T // seq
    tm = _token_tile(seq, 256)
    tps = seq // tm
    n_t = T // tm

    def body(h_ref, dhn_ref, y_ref, gb_ref, ub_ref, m_ref, g_ref, wg_ref, wu_ref, wd_ref,
             dh_ref, xb_ref, dyb_ref, a_ref, dg_ref, du_ref, mg_ref):
        i = pl.program_id(0)
        hv = h_ref[...]
        dhn = dhn_ref[...]
        sh = m_ref[0, row0:row0 + 1, :]
        sc = m_ref[0, row0 + 1:row0 + 2, :]
        ga = m_ref[0, row0 + 2:row0 + 3, :]
        g = g_ref[...]
        xn, r, inv = _rms_mod(hv, g, sh, sc)
        xb_ref[...] = xn.astype(BF16)
        dyb = ((0.5 * ga) * dhn).astype(BF16)
        dyb_ref[...] = dyb
        dga = 0.5 * jnp.sum(dhn * y_ref[...], axis=0, keepdims=True)
        dxn = jnp.zeros((tm, D), F32)
        for c0 in range(0, FF, FC):
            wgc = wg_ref[c0:c0 + FC, :]
            wuc = wu_ref[c0:c0 + FC, :]
            gg = gb_ref[:, c0:c0 + FC]
            uu = ub_ref[:, c0:c0 + FC]
            sig = jax.nn.sigmoid(gg)
            s = gg * sig
            a_ref[:, c0:c0 + FC] = s * uu
            da = _dot_nt(dyb, wd_ref[c0:c0 + FC, :]).astype(BF16)
            dub = da * s
            dgb = (da * uu) * (sig * (1.0 + gg * (1.0 - sig)))
            dg_ref[:, c0:c0 + FC] = dgb
            du_ref[:, c0:c0 + FC] = dub
            dxn = dxn + _dot(dgb, wgc) + _dot(dub, wuc)
        dh, s_dxn, s_dxr = _rms_mod_bwd(dxn, r, inv, g, sc)
        dh_ref[...] = dhn + dh

        @pl.when(i % tps == 0)
        def _():
            mg_ref[...] = jnp.zeros(mg_ref.shape, F32)

        mg_ref[0, 0:1, :] += s_dxn
        mg_ref[0, 1:2, :] += s_dxr
        mg_ref[0, 2:3, :] += dga

    tok = pl.BlockSpec((tm, D), lambda i: (i, 0))
    tokf = pl.BlockSpec((tm, FF), lambda i: (i, 0))
    return _call(
        body, name="ffn_bwd", grid=(n_t,),
        in_specs=[tok, tok, tok, tokf, tokf, pl.BlockSpec((1, 9, D), lambda i: (i // tps, 0, 0)),
                  pl.BlockSpec((1, D), lambda i: (0, 0)), _vmem(), _vmem(), _vmem()],
        out_specs=[tok, tok, tok, tokf, tokf, tokf, pl.BlockSpec((1, 8, D), lambda i: (i // tps, 0, 0))],
        out_shape=[jax.ShapeDtypeStruct((T, D), F32), jax.ShapeDtypeStruct((T, D), BF16),
                   jax.ShapeDtypeStruct((T, D), BF16), jax.ShapeDtypeStruct((T, FF), BF16),
                   jax.ShapeDtypeStruct((T, FF), BF16), jax.ShapeDtypeStruct((T, FF), BF16),
                   jax.ShapeDtypeStruct((B, 8, D), F32)],
        args=(h, dhn, y, gb, ub, mods, gn, wg, wu, wd), semantics=("arbitrary",), plans=plans)


def _wgrad(a, b, name, plans=()):
    T, da = a.shape
    db = b.shape[1]
    bm = {2816: 1408, 3840: 1280}[da]
    bn = db
    tk = min(1024, T)
    nk = T // tk

    def body(a_ref, b_ref, o_ref):
        @pl.when(pl.program_id(2) == 0)
        def _():
            o_ref[...] = jnp.zeros(o_ref.shape, F32)

        o_ref[...] += _dot_tn(a_ref[...], b_ref[...])

    (out,), plan_outs = _call(
        body, name=name, grid=(da // bm, db // bn, nk),
        in_specs=[pl.BlockSpec((tk, bm), lambda i, j, k: (k, i)), pl.BlockSpec((tk, bn), lambda i, j, k: (k, j))],
        out_specs=[pl.BlockSpec((bm, bn), lambda i, j, k: (i, j))], out_shape=[jax.ShapeDtypeStruct((da, db), F32)],
        args=(a, b), semantics=("parallel", "parallel", "arbitrary"), plans=plans)
    return out, plan_outs


def _inproj_fwd(h, mods, gn, w_in, seq, plans=()):
    T = h.shape[0]
    tm = _token_tile(seq)
    tps = seq // tm

    def body(h_ref, m_ref, g_ref, w_ref, *outs):
        xn, _, _ = _rms_mod(h_ref[...], g_ref[...], m_ref[0, 3:4, :], m_ref[0, 4:5, :])
        xb = xn.astype(BF16)
        for (_, c0, w), o_ref in zip(PIECES, outs):
            o_ref[...] = _dot_nt(xb, w_ref[c0:c0 + w, :])

    return _call(
        body, name="inproj_fwd", grid=(T // tm,),
        in_specs=[pl.BlockSpec((tm, D), lambda i: (i, 0)), pl.BlockSpec((1, 9, D), lambda i: (i // tps, 0, 0)),
                  pl.BlockSpec((1, D), lambda i: (0, 0)), _vmem()],
        out_specs=[pl.BlockSpec((tm, w), lambda i: (i, 0)) for _, _, w in PIECES],
        out_shape=[jax.ShapeDtypeStruct((T, w), F32) for _, _, w in PIECES], args=(h, mods, gn, w_in),
        semantics=("parallel",), plans=plans)


def _inproj_bwd(h, dh_res, dpieces, mods, gn, w_in, seq, plans=()):
    T = h.shape[0]
    B = T // seq
    tm = _token_tile(seq, 256)
    tps = seq // tm

    def body(h_ref, dres_ref, *rest):
        dp_refs = rest[:len(PIECES)]
        m_ref, g_ref, w_ref, dh_ref, xb_ref, dpb_ref, mg_ref = rest[len(PIECES):]
        i = pl.program_id(0)
        g = g_ref[...]
        sc = m_ref[0, 4:5, :]
        xn, r, inv = _rms_mod(h_ref[...], g, m_ref[0, 3:4, :], sc)
        xb_ref[...] = xn.astype(BF16)
        dxn = jnp.zeros((tm, D), F32)
        for (_, c0, w), dp_ref in zip(PIECES, dp_refs):
            dpb = dp_ref[...].astype(BF16)
            dpb_ref[:, c0:c0 + w] = dpb
            dxn = dxn + _dot(dpb, w_ref[c0:c0 + w, :])
        dh, s_dxn, s_dxr = _rms_mod_bwd(dxn, r, inv, g, sc)
        dh_ref[...] = dres_ref[...] + dh

        @pl.when(i % tps == 0)
        def _():
            mg_ref[...] = jnp.zeros(mg_ref.shape, F32)

        mg_ref[0, 0:1, :] += s_dxn
        mg_ref[0, 1:2, :] += s_dxr

    tok = pl.BlockSpec((tm, D), lambda i: (i, 0))
    return _call(
        body, name="inproj_bwd", grid=(T // tm,),
        in_specs=[tok, tok] + [pl.BlockSpec((tm, w), lambda i: (i, 0)) for _, _, w in PIECES]
        + [pl.BlockSpec((1, 9, D), lambda i: (i // tps, 0, 0)), pl.BlockSpec((1, D), lambda i: (0, 0)), _vmem()],
        out_specs=[tok, tok, pl.BlockSpec((tm, IN_COLS), lambda i: (i, 0)),
                   pl.BlockSpec((1, 8, D), lambda i: (i // tps, 0, 0))],
        out_shape=[jax.ShapeDtypeStruct((T, D), F32), jax.ShapeDtypeStruct((T, D), BF16),
                   jax.ShapeDtypeStruct((T, IN_COLS), BF16), jax.ShapeDtypeStruct((B, 8, D), F32)],
        args=(h, dh_res, *dpieces, mods, gn, w_in), semantics=("arbitrary",), plans=plans)


def _seg_mean(x):
    i = lax.broadcasted_iota(jnp.int32, (128, 128), 0) >> 6
    j = lax.broadcasted_iota(jnp.int32, (128, 128), 1) >> 6
    ones = jnp.where(i == j, 1.0 / HD, 0.0).astype(BF16)
    hi = x.astype(BF16)
    lo = (x - hi.astype(F32)).astype(BF16)
    return _dot(hi, ones) + _dot(lo, ones)


def _head_norm(x, g2):
    inv = lax.rsqrt(_seg_mean(x * x) + EPS)
    y = x * inv
    return y * g2, y, inv


def _head_norm_bwd(dxn, y, inv, g2):
    dy = dxn * g2
    return inv * (dy - y * _seg_mean(dy * y)), jnp.sum(dxn * y, axis=0, keepdims=True)


def _swa_block(q, kk, vv, gq2, gk2, sinks, first, do=None):
    lo = lax.broadcasted_iota(jnp.int32, (1, 128), 1) < HD
    kn, ky, kinv = _head_norm(kk, gk2)

    def operands(x):
        xr = pltpu.roll(x, HD, 1)
        own_lo, own_hi = jnp.where(lo, x, 0.0).astype(BF16), jnp.where(lo, 0.0, x).astype(BF16)
        rolled_lo, rolled_hi = jnp.where(lo, xr, 0.0).astype(BF16), jnp.where(lo, 0.0, xr).astype(BF16)
        return (own_lo, rolled_hi), (rolled_lo, own_hi)

    def restore(parts):
        (own_lo, rolled_hi), (rolled_lo, own_hi) = parts
        return (jnp.where(lo, own_lo, own_hi)
                + pltpu.roll(jnp.where(lo, rolled_lo, rolled_hi), HD, 1))

    k_ops, v_ops = operands(kn), operands(vv)
    k2 = [jnp.concatenate(pair, axis=0) for pair in k_ops]
    def stack(x):
        return jnp.concatenate([x[:, 128 * p:128 * (p + 1)] for p in range(4)], axis=0)

    def unstack(x):
        return jnp.concatenate([x[BLK * p:BLK * (p + 1)] for p in range(4)], axis=1)

    def of_head(x, kh):
        return x[2 * BLK * kh:2 * BLK * (kh + 1)]

    nq = 4 * BLK
    pair = lax.broadcasted_iota(jnp.int32, (nq, 1), 0) >> 7
    row = lax.broadcasted_iota(jnp.int32, (nq, 2 * BLK), 0) & (BLK - 1)
    col = lax.broadcasted_iota(jnp.int32, (nq, 2 * BLK), 1)
    valid = (col <= row + BLK) & (col > row) & (col >= jnp.where(first, BLK, 0))
    scale = HD ** -0.5
    qn, qy, qinv = _head_norm(stack(q), gq2)
    qnb = qn.astype(BF16)
    s2 = jnp.concatenate([_dot_nt(of_head(qnb, kh), k2[kh]) for kh in range(N_KV)], axis=0) * scale
    probs, p_sink = [], []
    for j in range(2):
        s = jnp.where(valid, s2[:, 2 * BLK * j:2 * BLK * (j + 1)], NEG)
        sink = jnp.zeros((nq, 1), F32)
        for p in range(4):
            sink = jnp.where(pair == p, sinks[:, 2 * p + j:2 * p + j + 1], sink)
        m = jnp.maximum(jnp.max(s, axis=-1, keepdims=True), sink)
        e = jnp.exp(s - m)
        e_sink = jnp.exp(sink - m)
        rden = 1.0 / (jnp.sum(e, axis=-1, keepdims=True) + e_sink)
        probs.append(e * rden)
        p_sink.append(e_sink * rden)
    pb = [p.astype(BF16) for p in probs]
    if do is None:
        return unstack(jnp.concatenate(
            [_dot(of_head(pb[0], kh), v_ops[kh][0]) + _dot(of_head(pb[1], kh), v_ops[kh][1]) for kh in range(N_KV)],
            axis=0))
    dob = stack(do).astype(BF16)
    ds, dsinks = [], [None] * 8
    for j in range(2):
        dp = jnp.concatenate([_dot_nt(of_head(dob, kh), v_ops[kh][j]) for kh in range(N_KV)], axis=0)
        t = jnp.sum(dp * probs[j], axis=-1, keepdims=True)
        ds.append(probs[j] * (dp - t) * scale)
        lost = p_sink[j] * t
        for p in range(4):
            dsinks[2 * p + j] = -jnp.sum(lost[BLK * p:BLK * (p + 1)])
    dsb = jnp.concatenate(ds, axis=1).astype(BF16)
    dqn = jnp.concatenate([_dot(of_head(dsb, kh), k2[kh]) for kh in range(N_KV)], axis=0)
    dq, dgq2 = _head_norm_bwd(dqn, qy, qinv, gq2)
    dk2 = [_dot_tn(of_head(dsb, kh), of_head(qnb, kh)) for kh in range(N_KV)]
    dv_ops = [[_dot_tn(of_head(pb[j], kh), of_head(dob, kh)) for j in range(2)] for kh in range(N_KV)]
    dkn = restore(tuple((d[:2 * BLK], d[2 * BLK:]) for d in dk2))
    dvv = restore(tuple(tuple(d) for d in dv_ops))
    dkk, dgk2 = _head_norm_bwd(dkn, ky, kinv, gk2)
    return unstack(dq), dkk, dvv, dgq2, dgk2, dsinks


SWA_GROUP = 8


def _swa_specs(nbs):
    grp = min(SWA_GROUP, nbs)
    rows = grp * BLK

    def tok(w):
        return pl.BlockSpec((rows, w), lambda i: (i, 0))

    halo = pl.BlockSpec((BLK, 128), lambda i: (jnp.maximum(i * grp - 1, 0), 0))
    vec = pl.BlockSpec((1, 128), lambda i: (0, 0))
    sk = pl.BlockSpec((1, 8), lambda i: (0, 0))
    return grp, tok, halo, vec, sk


def _swa_fwd(q, k, v, gq2, gk2, sinks, nbs, plans=()):
    T = q.shape[0]
    grp, tok, halo, vec, sk = _swa_specs(nbs)

    def body(q_ref, kh_ref, kc_ref, vh_ref, vc_ref, gq_ref, gk_ref, sk_ref, o_ref):
        seq_start = ((pl.program_id(0) * grp) % nbs) == 0
        for g in range(grp):
            rows = slice(g * BLK, (g + 1) * BLK)
            prev = slice((g - 1) * BLK, g * BLK)
            kk = jnp.concatenate([kh_ref[...] if g == 0 else kc_ref[prev, :], kc_ref[rows, :]], axis=0)
            vv = jnp.concatenate([vh_ref[...] if g == 0 else vc_ref[prev, :], vc_ref[rows, :]], axis=0)
            o_ref[rows, :] = _swa_block(q_ref[rows, :], kk, vv, gq_ref[...], gk_ref[...], sk_ref[...],
                                        seq_start if g == 0 else False)

    (out,), plan_outs = _call(
        body, name="swa_fwd", grid=(T // (grp * BLK),),
        in_specs=[tok(D_B), halo, tok(128), halo, tok(128), vec, vec, sk], out_specs=[tok(D_B)],
        out_shape=[jax.ShapeDtypeStruct((T, D_B), F32)], args=(q, k, k, v, v, gq2, gk2, sinks),
        semantics=("parallel",), plans=plans)
    return out, plan_outs


def _swa_bwd(q, k, v, gq2, gk2, sinks, do, nbs, plans=()):
    T = q.shape[0]
    grp, tok, halo, vec, sk = _swa_specs(nbs)
    steps = T // (grp * BLK)

    def body(q_ref, kh_ref, kc_ref, vh_ref, vc_ref, gq_ref, gk_ref, sk_ref, do_ref,
             dq_ref, dk_ref, dv_ref, dkh_ref, dvh_ref, dgq_ref, dgk_ref, dsk_ref):
        i = pl.program_id(0)
        seq_start = ((i * grp) % nbs) == 0

        @pl.when(i == 0)
        def _():
            for r in (dgq_ref, dgk_ref, dsk_ref):
                r[...] = jnp.zeros(r.shape, F32)

        res = []
        for g in range(grp):
            rows = slice(g * BLK, (g + 1) * BLK)
            prev = slice((g - 1) * BLK, g * BLK)
            kk = jnp.concatenate([kh_ref[...] if g == 0 else kc_ref[prev, :], kc_ref[rows, :]], axis=0)
            vv = jnp.concatenate([vh_ref[...] if g == 0 else vc_ref[prev, :], vc_ref[rows, :]], axis=0)
            res.append(_swa_block(q_ref[rows, :], kk, vv, gq_ref[...], gk_ref[...], sk_ref[...],
                                  seq_start if g == 0 else False, do=do_ref[rows, :]))
        lane = lax.broadcasted_iota(jnp.int32, (8, 128), 1)
        upd = jnp.zeros((8, 128), F32)
        for g, (dq, dkk, dvv, dgq2, dgk2, dsinks) in enumerate(res):
            rows = slice(g * BLK, (g + 1) * BLK)
            dq_ref[rows, :] = dq
            dk_ref[rows, :] = dkk[BLK:] + res[g + 1][1][:BLK] if g + 1 < grp else dkk[BLK:]
            dv_ref[rows, :] = dvv[BLK:] + res[g + 1][2][:BLK] if g + 1 < grp else dvv[BLK:]
            dgq_ref[...] += dgq2
            dgk_ref[...] += dgk2
            for h, d in enumerate(dsinks):
                upd = upd + jnp.where(lane == h, d, 0.0)
        dkh_ref[0] = res[0][1][:BLK]
        dvh_ref[0] = res[0][2][:BLK]
        dsk_ref[...] += upd

    one = pl.BlockSpec((1, BLK, 128), lambda i: (i, 0, 0))
    halo_shape = jax.ShapeDtypeStruct((steps, BLK, 128), F32)
    return _call(
        body, name="swa_bwd", grid=(steps,),
        in_specs=[tok(D_B), halo, tok(128), halo, tok(128), vec, vec, sk, tok(D_B)],
        out_specs=[tok(D_B), tok(128), tok(128), one, one, vec, vec, pl.BlockSpec((8, 128), lambda i: (0, 0))],
        out_shape=[jax.ShapeDtypeStruct((T, D_B), F32), jax.ShapeDtypeStruct((T, 128), F32),
                   jax.ShapeDtypeStruct((T, 128), F32), halo_shape, halo_shape, jax.ShapeDtypeStruct((1, 128), F32),
                   jax.ShapeDtypeStruct((1, 128), F32), jax.ShapeDtypeStruct((8, 128), F32)],
        args=(q, k, k, v, v, gq2, gk2, sinks, do), semantics=("arbitrary",), plans=plans)


def _swa_add_halo(dk, dk_halo):
    steps = dk_halo.shape[0]
    nxt = jnp.concatenate([dk_halo[1:], jnp.zeros_like(dk_halo[:1])], axis=0)[:, None]
    dk = dk.reshape(steps, -1, BLK, 128)
    return jnp.concatenate([dk[:, :-1], dk[:, -1:] + nxt], axis=1).reshape(-1, 128)


def _sgu_norm(av, g_ln, b_ln):
    t, th = _gelu(av)
    mu = jnp.mean(t, axis=-1, keepdims=True)
    tc = t - mu
    rstd = lax.rsqrt(jnp.mean(tc * tc, axis=-1, keepdims=True) + EPS)
    vhat = tc * rstd
    return vhat * g_ln + b_ln, vhat, rstd, th


def _masked_ws(ws_ref):
    tril = lax.broadcasted_iota(jnp.int32, (BLK, BLK), 0) >= lax.broadcasted_iota(jnp.int32, (BLK, BLK), 1)
    return [jnp.where(tril, ws_ref[g], 0.0).astype(BF16) for g in range(N_GRP)]


def _mix_fwd(au, av, gta, gtb, ob, h, mods, g_ln, b_ln, ws, bsb, wa, wb, wout, seq, plans=()):
    T = h.shape[0]
    tm = _token_tile(seq)
    tps = seq // tm

    def body(au_ref, av_ref, gta_ref, gtb_ref, ob_ref, h_ref, m_ref, gl_ref, bl_ref, ws_ref, bs_ref,
             wa_ref, wb_ref, wo_ref, out_ref, vvb_s, z_s):
        u, _ = _gelu(au_ref[...])
        vv, _, _, _ = _sgu_norm(av_ref[...], gl_ref[...], bl_ref[...])
        vvb_s[...] = vv.astype(BF16)
        wsm = _masked_ws(ws_ref)
        for c in range(tm // BLK):
            rows = slice(c * BLK, (c + 1) * BLK)
            for g in range(N_GRP):
                cols = slice(g * BLK, (g + 1) * BLK)
                z_s[rows, cols] = _dot(wsm[g], vvb_s[rows, cols]) + bs_ref[g]
        ya = _dot_nt((u * z_s[...]).astype(BF16), wa_ref[...])
        yb = _dot_nt(ob_ref[...].astype(BF16), wb_ref[...])
        merged = jax.nn.sigmoid(gta_ref[...]) * ya + jax.nn.sigmoid(gtb_ref[...]) * yb
        out_ref[...] = h_ref[...] + m_ref[0, 5:6, :] * _dot(merged.astype(BF16), wo_ref[...])

    def tok(w):
        return pl.BlockSpec((tm, w), lambda i: (i, 0))

    def full(shape):
        return pl.BlockSpec(shape, lambda i: (0,) * len(shape))

    (out,), plan_outs = _call(
        body, name="mix_fwd", grid=(T // tm,),
        in_specs=[tok(D_A), tok(D_A), tok(D), tok(D), tok(D_B), tok(D),
                  pl.BlockSpec((1, 9, D), lambda i: (i // tps, 0, 0)), full((1, D_A)), full((1, D_A)),
                  full((N_GRP, BLK, BLK)), full((N_GRP, BLK, BLK)), _vmem(), _vmem(), _vmem()],
        out_specs=[tok(D)], out_shape=[jax.ShapeDtypeStruct((T, D), F32)],
        scratch_shapes=[pltpu.VMEM((tm, D_A), BF16), pltpu.VMEM((tm, D_A), F32)],
        args=(au, av, gta, gtb, ob, h, mods, g_ln, b_ln, ws, bsb, wa, wb, wout), semantics=("parallel",),
        plans=plans)
    return out, plan_outs


def _mix_bwd(au, av, gta, gtb, ob, dh, mods, g_ln, b_ln, ws, bsb, wa, wb, wout, seq, plans=()):
    T = dh.shape[0]
    B = T // seq
    tm = _token_tile(seq, 256)
    tps = seq // tm

    def body(au_ref, av_ref, gta_ref, gtb_ref, ob_ref, dh_ref, m_ref, gl_ref, bl_ref, ws_ref, bs_ref,
             wa_ref, wb_ref, wo_ref,
             dau_ref, dav_ref, dgta_ref, dgtb_ref, dob_ref, dwo_ref, dwa_ref, dwb_ref, dws_ref, dbs_ref, dln_ref,
             mg_ref, vvb_s, z_s, dz_s, dzb_s, dvv_s):
        i = pl.program_id(0)

        @pl.when(i == 0)
        def _():
            for r in (dwo_ref, dwa_ref, dwb_ref, dws_ref, dbs_ref, dln_ref):
                r[...] = jnp.zeros(r.shape, F32)

        @pl.when(i % tps == 0)
        def _():
            mg_ref[...] = jnp.zeros(mg_ref.shape, F32)

        auv = au_ref[...]
        avv = av_ref[...]
        u, thu = _gelu(auv)
        g_ln = gl_ref[...]
        vv, vhat, rstd, thv = _sgu_norm(avv, g_ln, bl_ref[...])
        vvb_s[...] = vv.astype(BF16)
        wsm = _masked_ws(ws_ref)
        for c in range(tm // BLK):
            rows = slice(c * BLK, (c + 1) * BLK)
            for g in range(N_GRP):
                cols = slice(g * BLK, (g + 1) * BLK)
                z_s[rows, cols] = _dot(wsm[g], vvb_s[rows, cols]) + bs_ref[g]
        z = z_s[...]
        yab = (u * z).astype(BF16)
        obb = ob_ref[...].astype(BF16)
        ya = _dot_nt(yab, wa_ref[...])
        yb = _dot_nt(obb, wb_ref[...])
        sa = jax.nn.sigmoid(gta_ref[...])
        sb = jax.nn.sigmoid(gtb_ref[...])
        mb = (sa * ya + sb * yb).astype(BF16)
        dhv = dh_ref[...]
        mg_ref[0, 0:1, :] += jnp.sum(dhv * _dot(mb, wo_ref[...]), axis=0, keepdims=True)
        dmob = (m_ref[0, 5:6, :] * dhv).astype(BF16)
        dwo_ref[...] += _dot_tn(mb, dmob)
        dmerged = _dot_nt(dmob, wo_ref[...])
        dya = dmerged * sa
        dyb = dmerged * sb
        dgta_ref[...] = dya * ya * (1.0 - sa)
        dgtb_ref[...] = dyb * yb * (1.0 - sb)
        dyab = dya.astype(BF16)
        dybb = dyb.astype(BF16)
        dwa_ref[...] += _dot_tn(dyab, yab)
        dwb_ref[...] += _dot_tn(dybb, obb)
        dob_ref[...] = _dot(dybb, wb_ref[...])
        dyap = _dot(dyab, wa_ref[...])
        dau_ref[...] = (dyap * z) * _gelu_grad(auv, thu)
        dz = dyap * u
        dz_s[...] = dz
        dzb_s[...] = dz.astype(BF16)
        for c in range(tm // BLK):
            rows = slice(c * BLK, (c + 1) * BLK)
            for g in range(N_GRP):
                cols = slice(g * BLK, (g + 1) * BLK)
                dzb = dzb_s[rows, cols]
                dvv_s[rows, cols] = _dot_tn(wsm[g], dzb)
                dws_ref[g] += _dot_nt(dzb, vvb_s[rows, cols])
                dbs_ref[g] += dz_s[rows, cols]
        dvv = dvv_s[...]
        dln_ref[0:1, :] += jnp.sum(dvv * vhat, axis=0, keepdims=True)
        dln_ref[1:2, :] += jnp.sum(dvv, axis=0, keepdims=True)
        dvh = dvv * g_ln
        dt = rstd * (dvh - jnp.mean(dvh, axis=-1, keepdims=True)
                     - vhat * jnp.mean(dvh * vhat, axis=-1, keepdims=True))
        dav_ref[...] = dt * _gelu_grad(avv, thv)

    def tok(w):
        return pl.BlockSpec((tm, w), lambda i: (i, 0))

    def full(shape):
        return pl.BlockSpec(shape, lambda i: (0,) * len(shape))

    return _call(
        body, name="mix_bwd", grid=(T // tm,),
        in_specs=[tok(D_A), tok(D_A), tok(D), tok(D), tok(D_B), tok(D),
                  pl.BlockSpec((1, 9, D), lambda i: (i // tps, 0, 0)), full((1, D_A)), full((1, D_A)),
                  full((N_GRP, BLK, BLK)), full((N_GRP, BLK, BLK)), _vmem(), _vmem(), _vmem()],
        out_specs=[tok(D_A), tok(D_A), tok(D), tok(D), tok(D_B), full((D, D)), full((D, D_A)), full((D, D_B)),
                   full((N_GRP, BLK, BLK)), full((N_GRP, BLK, BLK)), full((8, D_A)),
                   pl.BlockSpec((1, 8, D), lambda i: (i // tps, 0, 0))],
        out_shape=[jax.ShapeDtypeStruct((T, D_A), F32), jax.ShapeDtypeStruct((T, D_A), F32),
                   jax.ShapeDtypeStruct((T, D), F32), jax.ShapeDtypeStruct((T, D), F32),
                   jax.ShapeDtypeStruct((T, D_B), F32), jax.ShapeDtypeStruct((D, D), F32),
                   jax.ShapeDtypeStruct((D, D_A), F32), jax.ShapeDtypeStruct((D, D_B), F32),
                   jax.ShapeDtypeStruct((N_GRP, BLK, BLK), F32), jax.ShapeDtypeStruct((N_GRP, BLK, BLK), F32),
                   jax.ShapeDtypeStruct((8, D_A), F32), jax.ShapeDtypeStruct((B, 8, D), F32)],
        scratch_shapes=[pltpu.VMEM((tm, D_A), BF16), pltpu.VMEM((tm, D_A), F32), pltpu.VMEM((tm, D_A), F32),
                        pltpu.VMEM((tm, D_A), BF16), pltpu.VMEM((tm, D_A), F32)],
        args=(au, av, gta, gtb, ob, dh, mods, g_ln, b_ln, ws, bsb, wa, wb, wout), semantics=("arbitrary",),
        plans=plans)


def _prologue(c_pad, w_ada, first_shards):
    cols = w_ada.shape[1]
    plan_w, plan_c = _GatherRelayed(first_shards), _GatherDirect([c_pad])
    plan_m = _Gather([jax.ShapeDtypeStruct((N_DEV * 8, cols), F32)])
    n_w = len(first_shards)

    def body(c_ref, wada_ref, *rest):
        w_ins, rest = rest[:n_w], rest[n_w:]
        call_ref, mods_ref = rest[:2]
        w_outs, rest = rest[2:2 + n_w], rest[2 + n_w:]
        cvm, part, local_sem = rest[:3]
        sems = rest[3:]
        sems_w, sems_c, sems_m = sems[:3], sems[3:6], sems[6:9]
        plan_c.start([c_ref], [call_ref], sems_c)
        plan_c.relay([c_ref], [call_ref], sems_c)
        plan_c.finish([c_ref], [call_ref], sems_c)
        load = pltpu.make_async_copy(call_ref, cvm, local_sem)
        load.start()
        load.wait()
        cv = cvm[...].reshape(N_DEV * 8, D)
        part[...] = _dot((cv * jax.nn.sigmoid(cv)).astype(BF16), wada_ref[...].astype(BF16))
        plan_m.start([part], [mods_ref], sems_m)
        plan_w.start(w_ins, w_outs, sems_w)
        plan_m.relay([part], [mods_ref], sems_m)
        plan_m.finish([part], [mods_ref], sems_m)
        plan_w.relay(w_ins, w_outs, sems_w)
        plan_w.finish(w_ins, w_outs, sems_w)

    res = pl.pallas_call(
        body, name="prologue", in_specs=[_any(), _vmem()] + [_any()] * n_w,
        out_specs=[_any()] * (2 + n_w), out_shape=plan_c.out_shape + plan_m.out_shape + plan_w.out_shape,
        scratch_shapes=[pltpu.VMEM((N_DEV, 8, D), F32), pltpu.VMEM((N_DEV * 8, cols), F32), pltpu.SemaphoreType.DMA]
        + plan_w.scratch + plan_c.scratch + plan_m.scratch,
        compiler_params=pltpu.CompilerParams(vmem_limit_bytes=VMEM_LIMIT),
    )(c_pad, w_ada, *first_shards)
    return res[0], res[1], list(res[2:])


def _ada_update(c_all, dm_cols, w, m, v, plans=()):
    n, cols = c_all.shape[0], w.shape[1]

    def body(c_ref, dm_ref, w_ref, m_ref, v_ref, g_ref, d_ref, nm_ref, nv_ref):
        cv = c_ref[...]
        g = _dot_tn((cv * jax.nn.sigmoid(cv)).astype(BF16), dm_ref[...].astype(BF16))
        g_ref[...] = g
        d_ref[...], nm_ref[...], nv_ref[...] = _adamw(w_ref[...], g, m_ref[...], v_ref[...])

    col = pl.BlockSpec((D, 128), lambda j: (0, j))
    return _call(
        body, name="ada_update", grid=(cols // 128,),
        in_specs=[pl.BlockSpec((n, D), lambda j: (0, 0)), pl.BlockSpec((n, 128), lambda j: (0, j)), col, col, col],
        out_specs=[col] * 4, out_shape=[jax.ShapeDtypeStruct(w.shape, F32)] * 4, args=(c_all, dm_cols, w, m, v),
        semantics=("parallel",), plans=plans)


def _mod_finish(mg1, mg2, mg2g, mg3, mods, g1, g2, g3):
    B = mods.shape[0]

    def body(mg1_ref, mg2_ref, mg2g_ref, mg3_ref, m_ref, g1_ref, g2_ref, g3_ref, dm_ref, dgn_ref):
        dgn_ref[...] = jnp.zeros(dgn_ref.shape, F32)
        for k, (mg, g_ref) in enumerate(((mg1_ref, g1_ref), (mg2_ref, g2_ref), (mg3_ref, g3_ref))):
            for b in range(B):
                s_dxr = mg[b, 1:2, :]
                dm_ref[b, 3 * k:3 * k + 1, :] = mg[b, 0:1, :]
                dm_ref[b, 3 * k + 1:3 * k + 2, :] = g_ref[...] * s_dxr
                dm_ref[b, 3 * k + 2:3 * k + 3, :] = mg2g_ref[b, 0:1, :] if k == 1 else mg[b, 2:3, :]
                dgn_ref[k:k + 1, :] += (1.0 + m_ref[b, 3 * k + 1:3 * k + 2, :]) * s_dxr

    args = (mg1, mg2, mg2g, mg3, mods, g1, g2, g3)
    out_shape = [jax.ShapeDtypeStruct((B, 9, D), F32), jax.ShapeDtypeStruct((8, D), F32)]
    return pl.pallas_call(body, name="mod_finish", grid=(1,), in_specs=[_whole(a) for a in args],
                          out_specs=[_whole(o) for o in out_shape], out_shape=out_shape,
                          compiler_params=_params("arbitrary"))(*args)


def _small_update(gathered, params, ms, vs):
    n = len(SMALL)
    B = gathered[0].shape[1]

    def body(*refs):
        gdm, ggn, gln, gws, gbs, ggq, ggk, gsk, gloss = refs[:9]
        w = dict(zip(SMALL, refs[9:9 + n]))
        m = dict(zip(SMALL, refs[9 + n:9 + 2 * n]))
        v = dict(zip(SMALL, refs[9 + 2 * n:9 + 3 * n]))
        outs = refs[9 + 3 * n:-1]
        out = {name: outs[4 * k:4 * k + 4] for k, name in enumerate(SMALL)}

        def total(ref, idx):
            acc = ref[(0,) + idx]
            for dev in range(1, N_DEV):
                acc = acc + ref[(dev,) + idx]
            return acc

        def finish(name, g, idx=(Ellipsis,)):
            d, nm, nv = _adamw(w[name][idx], g, m[name][idx], v[name][idx])
            for ref, val in zip(out[name], (g, d, nm, nv)):
                ref[idx] = val

        g_bada = total(gdm, (slice(0, 1),))
        for b in range(1, B):
            g_bada = g_bada + total(gdm, (slice(b, b + 1),))
        finish("b_ada", g_bada)
        finish("g_norm1", total(ggn, (slice(0, 1),)))
        finish("g_norm2", total(ggn, (slice(1, 2),)))
        finish("g_norm3", total(ggn, (slice(2, 3),)))
        finish("g_sgu_ln", total(gln, (slice(0, 1),)))
        finish("b_sgu_ln", total(gln, (slice(1, 2),)))
        tril = lax.broadcasted_iota(jnp.int32, (BLK, BLK), 0) >= lax.broadcasted_iota(jnp.int32, (BLK, BLK), 1)
        for g in range(N_GRP):
            finish("w_spatial", jnp.where(tril, total(gws, (g,)), 0.0), (0, g))
            finish("b_spatial", jnp.sum(total(gbs, (g,)).T, axis=0, keepdims=True), (0, slice(g, g + 1)))
        finish("g_q", total(ggq, ()))
        finish("g_k", total(ggk, ()))
        finish("attn_sinks", total(gsk, (slice(0, 1), slice(0, N_KV * Q_PER_KV))))
        refs[-1][...] = total(gloss, ())

    args = list(gathered) + [params[k] for k in SMALL] + [ms[k] for k in SMALL] + [vs[k] for k in SMALL]
    out_shape = []
    for k in SMALL:
        out_shape += [jax.ShapeDtypeStruct(params[k].shape, F32)] * 4
    out_shape.append(jax.ShapeDtypeStruct((8, 128), F32))
    res = pl.pallas_call(body, name="small_update", grid=(1,), in_specs=[_whole(a) for a in args],
                         out_specs=[_whole(o) for o in out_shape], out_shape=out_shape,
                         compiler_params=_params("arbitrary"))(*args)
    return {k: res[4 * i:4 * i + 4] for i, k in enumerate(SMALL)}, res[-1][0, 0]


def _rs_add(p, r, core, name):
    _, rows, width = p.shape

    def body(c_ref, p_ref, r_ref, o_ref):
        o_ref[...] = (p_ref[...] + r_ref[...]).astype(BF16)

    return pl.pallas_call(
        body, name="rs_add_" + name, out_shape=jax.ShapeDtypeStruct((4, rows, width), BF16),
        grid_spec=pltpu.PrefetchScalarGridSpec(
            num_scalar_prefetch=1, grid=(4,),
            in_specs=[pl.BlockSpec((1, rows, width), lambda k, c_ref: (2 * k + c_ref[0], 0, 0)),
                      pl.BlockSpec((1, rows, width), lambda k, c_ref: (k, 0, 0))],
            out_specs=pl.BlockSpec((1, rows, width), lambda k, c_ref: (k, 0, 0))),
        compiler_params=_params("parallel"),
    )(core, p, r)


def _adam_big(w, m, v, r, name):
    rows, cols = w.shape

    def body(r_ref, w_ref, m_ref, v_ref, go_ref, d_ref, nm_ref, nv_ref):
        g = r_ref[0].astype(F32)
        for k in range(1, 4):
            g = g + r_ref[k].astype(F32)
        go_ref[...] = g
        d_ref[...], nm_ref[...], nv_ref[...] = _adamw(w_ref[...], g, m_ref[...], v_ref[...])

    br = rows // 2
    shard = pl.BlockSpec((br, cols), lambda i: (i, 0))
    return pl.pallas_call(body, name="adam_" + name, grid=(2,),
                          in_specs=[pl.BlockSpec((4, br, cols), lambda i: (0, i, 0)), shard, shard, shard],
                          out_specs=[shard] * 4, out_shape=[jax.ShapeDtypeStruct(w.shape, F32)] * 4,
                          compiler_params=_params("parallel"))(r, w, m, v)


def kernel(x, c, w_ada, b_ada, g_norm1, ffn1_w_gate, ffn1_w_up, ffn1_w_down, g_norm2, w_in, g_sgu_ln, b_sgu_ln, w_spatial, b_spatial, g_q, g_k, attn_sinks, w_branch_a, w_branch_b, w_out, g_norm3, ffn2_w_gate, ffn2_w_up, ffn2_w_down, loss_target, m_w_ada, m_b_ada, m_g_norm1, m_ffn1_w_gate, m_ffn1_w_up, m_ffn1_w_down, m_g_norm2, m_w_in, m_g_sgu_ln, m_b_sgu_ln, m_w_spatial, m_b_spatial, m_g_q, m_g_k, m_attn_sinks, m_w_branch_a, m_w_branch_b, m_w_out, m_g_norm3, m_ffn2_w_gate, m_ffn2_w_up, m_ffn2_w_down, v_w_ada, v_b_ada, v_g_norm1, v_ffn1_w_gate, v_ffn1_w_up, v_ffn1_w_down, v_g_norm2, v_w_in, v_g_sgu_ln, v_b_sgu_ln, v_w_spatial, v_b_spatial, v_g_q, v_g_k, v_attn_sinks, v_w_branch_a, v_w_branch_b, v_w_out, v_g_norm3, v_ffn2_w_gate, v_ffn2_w_up, v_ffn2_w_down):
    names = ("w_ada", "b_ada", "g_norm1", "ffn1_w_gate", "ffn1_w_up", "ffn1_w_down", "g_norm2", "w_in", "g_sgu_ln",
             "b_sgu_ln", "w_spatial", "b_spatial", "g_q", "g_k", "attn_sinks", "w_branch_a", "w_branch_b", "w_out",
             "g_norm3", "ffn2_w_gate", "ffn2_w_up", "ffn2_w_down")
    w = dict(zip(names, (w_ada, b_ada, g_norm1, ffn1_w_gate, ffn1_w_up, ffn1_w_down, g_norm2, w_in, g_sgu_ln,
                         b_sgu_ln, w_spatial, b_spatial, g_q, g_k, attn_sinks, w_branch_a, w_branch_b, w_out, g_norm3,
                         ffn2_w_gate, ffn2_w_up, ffn2_w_down)))
    m = dict(zip(names, (m_w_ada, m_b_ada, m_g_norm1, m_ffn1_w_gate, m_ffn1_w_up, m_ffn1_w_down, m_g_norm2, m_w_in,
                         m_g_sgu_ln, m_b_sgu_ln, m_w_spatial, m_b_spatial, m_g_q, m_g_k, m_attn_sinks, m_w_branch_a,
                         m_w_branch_b, m_w_out, m_g_norm3, m_ffn2_w_gate, m_ffn2_w_up, m_ffn2_w_down)))
    v = dict(zip(names, (v_w_ada, v_b_ada, v_g_norm1, v_ffn1_w_gate, v_ffn1_w_up, v_ffn1_w_down, v_g_norm2, v_w_in,
                         v_g_sgu_ln, v_b_sgu_ln, v_w_spatial, v_b_spatial, v_g_q, v_g_k, v_attn_sinks, v_w_branch_a,
                         v_w_branch_b, v_w_out, v_g_norm3, v_ffn2_w_gate, v_ffn2_w_up, v_ffn2_w_down)))
    B, seq, _ = x.shape
    T = B * seq
    nbs = seq // BLK
    xi, yi, ci = _place()
    me = 4 * xi + 2 * yi + ci
    core = jnp.reshape(ci, (1,)).astype(jnp.int32)
    layout = {name: (tform, n, width) for name, tform, n, width in BIG}
    shard = {name: w[name][0].astype(BF16).T if tform else w[name][0].astype(BF16) for name, tform, _, _ in BIG}
    wts, parts, big_out = {}, {}, {}

    def gather_plan(group):
        return _Gather([shard[k] for k in group])

    def take(group, gathered):
        for k, g in zip(group, gathered):
            wts[k] = g.reshape(N_DEV * layout[k][1], layout[k][2])

    def blocks(name, grad):
        return grad.reshape(N_DEV, layout[name][1], layout[name][2])

    def to_sibling(names, grads):
        return _RsSibling([blocks(k, g) for k, g in zip(names, grads)])

    def add(names, grads, from_sibling):
        for k, g, r in zip(names, grads, from_sibling):
            parts[k] = _rs_add(blocks(k, g), r, core, k)

    def to_chips(names):
        return _RsChips([parts[k] for k in names])

    def update(names, from_chips):
        for k, r in zip(names, from_chips):
            if layout[k][0]:
                big_out[k] = [o.T[None] for o in _adam_big(w[k][0].T, m[k][0].T, v[k][0].T, r, k)]
            else:
                big_out[k] = [o[None] for o in _adam_big(w[k][0], m[k][0], v[k][0], r, k)]

    g1, g2, g3, g_ln, b_ln = g_norm1, g_norm2, g_norm3, g_sgu_ln, b_sgu_ln
    gq2, gk2 = jnp.tile(g_q, (1, 2)), jnp.tile(g_k, (1, 2))
    ws = w_spatial[0]
    bsb = jnp.broadcast_to(b_spatial[0][:, :, None], (N_GRP, BLK, BLK))
    xf = x.reshape(T, D)
    tgt = loss_target.reshape(T, D)

    ffn1 = ("ffn1_w_gate", "ffn1_w_up", "ffn1_w_down")
    ffn2 = ("ffn2_w_gate", "ffn2_w_up", "ffn2_w_down")
    c_pad = jnp.concatenate([c, jnp.zeros((8 - B, D), F32)], axis=0)
    c_all, mods_cols, gathered = _prologue(c_pad, w_ada[0], [shard[k] for k in ffn1[:2]])
    take(ffn1[:2], gathered)
    c_all = c_all[:, :B].reshape(N_DEV * B, D)
    mine = lax.dynamic_slice_in_dim(mods_cols, 8 * me, B, axis=1)
    mods = (mine.transpose(1, 0, 2).reshape(B, 9 * D) + b_ada).reshape(B, 9, D)

    group = ("ffn1_w_down", "w_branch_a", "w_branch_b", "w_out")
    (gb1, ub1), (gathered,) = _ffn_gate_up(xf, mods, g1, wts["ffn1_w_gate"], wts["ffn1_w_up"], 0, seq,
                                           plans=[gather_plan(group)])
    take(group, gathered)
    group = ("w_in",)
    (h1, y1), (gathered,) = _ffn_down(xf, gb1, ub1, mods, wts["ffn1_w_down"], 0, seq, plans=[gather_plan(group)])
    take(group, gathered)
    group = ("ffn2_w_gate",)
    (au, av, q_tok, k_tok, v_tok, gta, gtb), (gathered,) = _inproj_fwd(h1, mods, g2, wts["w_in"], seq,
                                                                       plans=[gather_plan(group)])
    take(group, gathered)
    group = ("ffn2_w_up",)
    ob, (gathered,) = _swa_fwd(q_tok, k_tok, v_tok, gq2, gk2, attn_sinks, nbs, plans=[gather_plan(group)])
    take(group, gathered)
    mixw = (wts["w_branch_a"], wts["w_branch_b"], wts["w_out"])
    group = ("ffn2_w_down",)
    h2, (gathered,) = _mix_fwd(au, av, gta, gtb, ob, h1, mods, g_ln, b_ln, ws, bsb, *mixw, seq,
                               plans=[gather_plan(group)])
    take(group, gathered)
    (dh3, y3, gb3, ub3, loss_parts), _ = _ffn_fwd(h2, mods, g3, *[wts[k] for k in ffn2], 6, seq, tgt=tgt)
    loss_part = jnp.full((8, 128), jnp.sum(loss_parts[:, 0, 0]))

    (dh2, xb, dyb, a, dg, du, mg3), _ = _ffn_bwd(h2, dh3, y3, gb3, ub3, mods, g3, *[wts[k] for k in ffn2], 6, seq)
    d_gate, _ = _wgrad(dg, xb, "wgrad_ffn2_gate")
    d_up, (r,) = _wgrad(du, xb, "wgrad_ffn2_up", plans=[to_sibling(ffn2[:1], [d_gate])])
    add(ffn2[:1], [d_gate], r)
    d_down, (r,) = _wgrad(a, dyb, "wgrad_ffn2_down", plans=[to_sibling(ffn2[1:2], [d_up])])
    add(ffn2[1:2], [d_up], r)

    (dau, dav, dgta, dgtb, dob, d_out, d_a, d_b, dws, dbs, dln, mg2g), (r, from_chips) = _mix_bwd(
        au, av, gta, gtb, ob, dh2, mods, g_ln, b_ln, ws, bsb, *mixw, seq,
        plans=[to_sibling(ffn2[2:], [d_down]), to_chips(ffn2[:2])])
    add(ffn2[2:], [d_down], r)
    update(ffn2[:2], from_chips)
    mixers = ("w_out", "w_branch_a", "w_branch_b")
    (dq, dk, dv, dk_halo, dv_halo, dgq2, dgk2, dsk), (from_chips, r) = _swa_bwd(
        q_tok, k_tok, v_tok, gq2, gk2, attn_sinks, dob, nbs,
        plans=[to_chips(ffn2[2:]), to_sibling(mixers, [d_out, d_a, d_b])])
    update(ffn2[2:], from_chips)
    add(mixers, [d_out, d_a, d_b], r)
    dk, dv = _swa_add_halo(dk, dk_halo), _swa_add_halo(dv, dv_halo)
    dgq, dgk = dgq2[:, :HD] + dgq2[:, HD:], dgk2[:, :HD] + dgk2[:, HD:]
    (dh1, xb2, dpb, mg2), (from_chips, early) = _inproj_bwd(
        h1, dh2, (dau, dav, dq, dk, dv, dgta, dgtb), mods, g2, wts["w_in"], seq,
        plans=[to_chips(mixers), _Gather([dln, dws, dbs, dgq, dgk, dsk, loss_part])])
    update(mixers, from_chips)

    (dx, xb, dyb, a, dg, du, mg1), _ = _ffn_bwd(xf, dh1, y1, gb1, ub1, mods, g1, *[wts[k] for k in ffn1], 0, seq)
    dmods, dgn = _mod_finish(mg1, mg2, mg2g, mg3, mods, g1, g2, g3)
    d_gate, (late,) = _wgrad(dg, xb, "wgrad_ffn1_gate", plans=[_GatherDirect([dmods.reshape(B, 9 * D), dgn])])
    gathered = late + early
    d_up, (r,) = _wgrad(du, xb, "wgrad_ffn1_up", plans=[to_sibling(ffn1[:1], [d_gate])])
    add(ffn1[:1], [d_gate], r)
    d_in, (r, from_chips) = _wgrad(dpb, xb2, "wgrad_w_in",
                                   plans=[to_sibling(ffn1[1:2], [d_up]), to_chips(ffn1[:1])])
    add(ffn1[1:2], [d_up], r)
    update(ffn1[:1], from_chips)
    d_down, (r, from_chips) = _wgrad(a, dyb, "wgrad_ffn1_down",
                                     plans=[to_sibling(("w_in",), [d_in]), to_chips(ffn1[1:2])])
    add(("w_in",), [d_in], r)
    update(ffn1[1:2], from_chips)
    small_out, loss = _small_update(gathered, w, m, v)
    dm_cols = lax.dynamic_slice_in_dim(gathered[0].reshape(N_DEV * B, 9 * D), (9 * D // N_DEV) * me,
                                       9 * D // N_DEV, axis=1)
    ada_out, (r, from_chips) = _ada_update(c_all, dm_cols, w_ada[0], m_w_ada[0], v_w_ada[0],
                                           plans=[to_sibling(ffn1[2:], [d_down]), to_chips(("w_in",))])
    add(ffn1[2:], [d_down], r)
    update(("w_in",), from_chips)
    (from_chips,) = _exchange([to_chips(ffn1[2:])], "rs_last")
    update(ffn1[2:], from_chips)

    def leaf(kind, name):
        if name == "w_ada":
            return ada_out[kind][None]
        if name in SMALL:
            return small_out[name][kind]
        return big_out[name][kind]

    return (loss, dx.reshape(B, seq, D), *[leaf(kind, name) for kind in range(4) for name in names])
```

```python
import functools
import math

import jax
import jax.numpy as jnp
from jax import lax
from jax.experimental import pallas as pl
from jax.experimental.pallas import tpu as pltpu

F32 = jnp.float32
BF16 = jnp.bfloat16
MESH = pl.DeviceIdType.MESH
AXES = ("x", "y", "c")
N_DEV = 8

VMEM_LIMIT = 56 * 1024 * 1024

D = 1024
FF = 2816
FC = 1408
D_A = 512
D_B = 512
HD = 64
N_KV = 2
Q_PER_KV = 4
BLK = 128
N_GRP = 4
IN_COLS = 3840
PIECES = (("au", 0, 512), ("av", 512, 512), ("q", 1024, 512), ("k", 1536, 128), ("v", 1664, 128),
          ("ga", 1792, 1024), ("gb", 2816, 1024))
EPS = 1e-6
NEG = -1e30
GELU_C = math.sqrt(2.0 / math.pi)

ADAM_LR = 0.001
ADAM_B1 = 0.9
ADAM_B2 = 0.999
ADAM_EPS = 1e-08
ADAM_WD = 0.01
ADAM_STEP = 10

NT = (((1,), (1,)), ((), ()))
TN = (((0,), (0,)), ((), ()))

BIG = (("ffn1_w_gate", True, FF // N_DEV, D), ("ffn1_w_up", True, FF // N_DEV, D),
       ("ffn1_w_down", False, FF // N_DEV, D), ("w_in", True, IN_COLS // N_DEV, D),
       ("w_branch_a", True, D // N_DEV, D_A), ("w_branch_b", True, D // N_DEV, D_B), ("w_out", False, D // N_DEV, D),
       ("ffn2_w_gate", True, FF // N_DEV, D), ("ffn2_w_up", True, FF // N_DEV, D),
       ("ffn2_w_down", False, FF // N_DEV, D))
SMALL = ("b_ada", "g_norm1", "g_norm2", "g_sgu_ln", "b_sgu_ln", "w_spatial", "b_spatial", "g_q", "g_k",
         "attn_sinks", "g_norm3")


def _dot(a, b):
    return jnp.dot(a, b, preferred_element_type=F32)


def _dot_nt(a, b):
    return lax.dot_general(a, b, NT, preferred_element_type=F32)


def _dot_tn(a, b):
    return lax.dot_general(a, b, TN, preferred_element_type=F32)


def _vmem():
    return pl.BlockSpec(memory_space=pltpu.VMEM)


def _any():
    return pl.BlockSpec(memory_space=pl.ANY)


def _whole(a):
    return pl.BlockSpec(a.shape, lambda i: (0,) * len(a.shape))


def _rms_mod(h, g, sh, sc):
    inv = lax.rsqrt(jnp.mean(h * h, axis=-1, keepdims=True) + EPS)
    r = h * inv
    return (r * g) * (1.0 + sc) + sh, r, inv


def _rms_mod_bwd(dxn, r, inv, g, sc):
    dr = dxn * (g * (1.0 + sc))
    dh = inv * (dr - r * jnp.mean(dr * r, axis=-1, keepdims=True))
    return dh, jnp.sum(dxn, axis=0, keepdims=True), jnp.sum(dxn * r, axis=0, keepdims=True)


def _gelu(x):
    t = jnp.tanh(GELU_C * (x + 0.044715 * (x * x * x)))
    return 0.5 * x * (1.0 + t), t


def _gelu_grad(x, t):
    return 0.5 * (1.0 + t) + 0.5 * x * (1.0 - t * t) * (GELU_C * (1.0 + 3.0 * 0.044715 * x * x))


def _adamw(w, g, m, v):
    m = ADAM_B1 * m + (1.0 - ADAM_B1) * g
    v = ADAM_B2 * v + (1.0 - ADAM_B2) * (g * g)
    m_hat = m / (1.0 - ADAM_B1 ** ADAM_STEP)
    v_hat = v / (1.0 - ADAM_B2 ** ADAM_STEP)
    delta = -ADAM_LR * (m_hat / (jnp.sqrt(v_hat) + ADAM_EPS) + ADAM_WD * w)
    return delta, m, v


def _token_tile(seq, cap=512):
    return min(cap, seq)


def _params(*semantics):
    return pltpu.CompilerParams(dimension_semantics=semantics, vmem_limit_bytes=VMEM_LIMIT)


def _place():
    return lax.axis_index("x"), lax.axis_index("y"), lax.axis_index("c")


class _Gather:
    def __init__(self, arrays):
        n = len(arrays)
        self.ins = list(arrays)
        self.out_shape = [jax.ShapeDtypeStruct((N_DEV,) + a.shape, a.dtype) for a in arrays]
        self.scratch = [pltpu.SemaphoreType.DMA((n, 7)), pltpu.SemaphoreType.DMA((n, 7)),
                        pltpu.SemaphoreType.DMA((n,))]

    def _copies(self, ins, outs, sems):
        send_sems, recv_sems, local_sems = sems
        n = len(ins)
        x, y, c = _place()
        me, sibling = (x, y, c), (x, y, 1 - c)
        chips = [(1 - x, y), (x, 1 - y), (1 - x, 1 - y)]

        def slot(a, px, py, pc):
            return outs[a].at[4 * px + 2 * py + pc]

        def copy(a, k, block, to, src=None):
            return pltpu.make_async_remote_copy(
                src_ref=slot(a, *block) if src is None else src, dst_ref=slot(a, *block),
                send_sem=send_sems.at[a, k], recv_sem=recv_sems.at[a, k], device_id=to, device_id_type=MESH)

        mine = [pltpu.make_async_copy(ins[a], slot(a, *me), local_sems.at[a]) for a in range(n)]
        first = [copy(a, 0, me, sibling, src=ins[a]) for a in range(n)]
        first += [copy(a, 1 + j, me, (*chip, c), src=ins[a]) for a in range(n) for j, chip in enumerate(chips)]
        landed = [[copy(a, 1 + j, (*chip, c), me) for a in range(n)] for j, chip in enumerate(chips)]
        passed = [[copy(a, 4 + j, (*chip, c), sibling) for a in range(n)] for j, chip in enumerate(chips)]
        from_sibling = [copy(a, 0, sibling, me) for a in range(n)]
        from_sibling += [copy(a, 4 + j, (*chip, 1 - c), me) for a in range(n) for j, chip in enumerate(chips)]
        return mine, first, landed, passed, from_sibling

    def start(self, ins, outs, sems):
        mine, first, _, _, _ = self._copies(ins, outs, sems)
        for cp in mine + first:
            cp.start()

    def relay(self, ins, outs, sems):
        _, _, landed, passed, _ = self._copies(ins, outs, sems)
        for arrivals, forwards in zip(landed, passed):
            for arrival, forward in zip(arrivals, forwards):
                arrival.wait_recv()
                forward.start()

    def finish(self, ins, outs, sems):
        mine, first, _, passed, from_sibling = self._copies(ins, outs, sems)
        for cp in from_sibling:
            cp.wait_recv()
        for cp in first + [f for fs in passed for f in fs]:
            cp.wait_send()
        for cp in mine:
            cp.wait()


class _GatherRelayed:
    def __init__(self, arrays):
        n = len(arrays)
        self.ins = list(arrays)
        self.out_shape = [jax.ShapeDtypeStruct((N_DEV,) + a.shape, a.dtype) for a in arrays]
        self.scratch = [pltpu.SemaphoreType.DMA((n, 7)), pltpu.SemaphoreType.DMA((n, 7)),
                        pltpu.SemaphoreType.DMA((n,))]

    def _copies(self, ins, outs, sems):
        send_sems, recv_sems, local_sems = sems
        n = len(ins)
        x, y, c = _place()
        me, sibling = (x, y, c), (x, y, 1 - c)
        south = c == 0
        via = (jnp.where(south, 1 - x, x), jnp.where(south, y, 1 - y))
        onward = (jnp.where(south, x, 1 - x), jnp.where(south, 1 - y, y))
        via_sem = jnp.where(south, 1, 2)

        def slot(a, px, py, pc):
            return outs[a].at[4 * px + 2 * py + pc]

        def copy(a, k, block, to, src=None):
            return pltpu.make_async_remote_copy(
                src_ref=slot(a, *block) if src is None else src, dst_ref=slot(a, *block),
                send_sem=send_sems.at[a, k], recv_sem=recv_sems.at[a, k], device_id=to, device_id_type=MESH)

        chips = [(1 - x, y), (x, 1 - y), (1 - x, 1 - y)]
        mine = [pltpu.make_async_copy(ins[a], slot(a, *me), local_sems.at[a]) for a in range(n)]
        first = [copy(a, 0, me, sibling, src=ins[a]) for a in range(n)]
        first += [copy(a, 1 + j, me, (*chips[j], c), src=ins[a]) for a in range(n) for j in range(2)]
        to_pass = [copy(a, via_sem, (*via, c), me) for a in range(n)]
        passed_on = [copy(a, 3, (*via, c), (*onward, c)) for a in range(n)]
        landed = [[copy(a, 1 + j, (*chips[j], c), me) for a in range(n)] for j in range(3)]
        to_sibling = [[copy(a, 4 + j, (*chips[j], c), sibling) for a in range(n)] for j in range(3)]
        from_sibling = [copy(a, 0, sibling, me) for a in range(n)]
        from_sibling += [copy(a, 4 + j, (*chips[j], 1 - c), me) for a in range(n) for j in range(3)]
        return mine, first, to_pass, passed_on, landed, to_sibling, from_sibling

    def start(self, ins, outs, sems):
        mine, first, _, _, _, _, _ = self._copies(ins, outs, sems)
        for cp in mine + first:
            cp.start()

    def relay(self, ins, outs, sems):
        _, _, to_pass, passed_on, landed, to_sibling, _ = self._copies(ins, outs, sems)
        for arrival, onward in zip(to_pass, passed_on):
            arrival.wait_recv()
            onward.start()
        x, y, c = _place()
        for j in range(3):
            for a, (arrival, forward) in enumerate(zip(landed[j], to_sibling[j])):
                if j < 2:
                    pl.when((c == 0) != (j == 0))(arrival.wait_recv)
                else:
                    arrival.wait_recv()
                forward.start()

    def finish(self, ins, outs, sems):
        mine, first, _, passed_on, _, to_sibling, from_sibling = self._copies(ins, outs, sems)
        for cp in from_sibling:
            cp.wait_recv()
        for cp in first + passed_on + [f for fs in to_sibling for f in fs]:
            cp.wait_send()
        for cp in mine:
            cp.wait()


class _GatherDirect:
    def __init__(self, arrays):
        n = len(arrays)
        self.ins = list(arrays)
        self.out_shape = [jax.ShapeDtypeStruct((N_DEV,) + a.shape, a.dtype) for a in arrays]
        self.scratch = [pltpu.SemaphoreType.DMA((n, 7)), pltpu.SemaphoreType.DMA((n, 7)),
                        pltpu.SemaphoreType.DMA((n,))]

    def _copies(self, ins, outs, sems):
        send_sems, recv_sems, local_sems = sems
        n = len(ins)
        x, y, c = _place()
        peers = [(x ^ (k >> 2 & 1), y ^ (k >> 1 & 1), c ^ (k & 1)) for k in range(1, N_DEV)]

        def slot(a, px, py, pc):
            return outs[a].at[4 * px + 2 * py + pc]

        def copy(a, k, block, to, src=None):
            return pltpu.make_async_remote_copy(
                src_ref=slot(a, *block) if src is None else src, dst_ref=slot(a, *block),
                send_sem=send_sems.at[a, k], recv_sem=recv_sems.at[a, k], device_id=to, device_id_type=MESH)

        mine = [pltpu.make_async_copy(ins[a], slot(a, x, y, c), local_sems.at[a]) for a in range(n)]
        sends = [copy(a, k, (x, y, c), peer, src=ins[a]) for a in range(n) for k, peer in enumerate(peers)]
        arrivals = [copy(a, k, peer, (x, y, c)) for a in range(n) for k, peer in enumerate(peers)]
        return mine, sends, arrivals

    def start(self, ins, outs, sems):
        mine, sends, _ = self._copies(ins, outs, sems)
        for cp in mine + sends:
            cp.start()

    def relay(self, ins, outs, sems):
        pass

    def finish(self, ins, outs, sems):
        mine, sends, arrivals = self._copies(ins, outs, sems)
        for cp in arrivals:
            cp.wait_recv()
        for cp in sends:
            cp.wait_send()
        for cp in mine:
            cp.wait()


class _RsSibling:
    def __init__(self, ps):
        n = len(ps)
        self.ins = list(ps)
        self.out_shape = [jax.ShapeDtypeStruct((4,) + p.shape[1:], p.dtype) for p in ps]
        self.scratch = [pltpu.SemaphoreType.DMA((n, 4)), pltpu.SemaphoreType.DMA((n, 4))]

    def _copies(self, ins, outs, sems):
        send_sems, recv_sems = sems
        x, y, c = _place()
        return [pltpu.make_async_remote_copy(
            src_ref=ins[a].at[2 * q + (1 - c)], dst_ref=outs[a].at[q], send_sem=send_sems.at[a, q],
            recv_sem=recv_sems.at[a, q], device_id=(x, y, 1 - c), device_id_type=MESH)
            for a in range(len(ins)) for q in range(4)]

    def start(self, ins, outs, sems):
        for cp in self._copies(ins, outs, sems):
            cp.start()

    def relay(self, ins, outs, sems):
        pass

    def finish(self, ins, outs, sems):
        for cp in self._copies(ins, outs, sems):
            cp.wait()


class _RsChips:
    def __init__(self, qs):
        n = len(qs)
        self.ins = list(qs)
        self.out_shape = ([jax.ShapeDtypeStruct(q.shape, q.dtype) for q in qs]
                          + [jax.ShapeDtypeStruct(q.shape[1:], q.dtype) for q in qs])
        self.scratch = [pltpu.SemaphoreType.DMA((n, 4)), pltpu.SemaphoreType.DMA((n, 4)),
                        pltpu.SemaphoreType.DMA((n,))]

    def _copies(self, ins, outs, sems):
        send_sems, recv_sems, local_sems = sems
        n = len(ins)
        slots, stage = outs[:n], outs[n:]
        x, y, c = _place()
        south = c == 0
        my_chip, diagonal = 2 * x + y, 2 * (1 - x) + (1 - y)
        via = (jnp.where(south, 1 - x, x), jnp.where(south, y, 1 - y))
        onward = (jnp.where(south, x, 1 - x), jnp.where(south, 1 - y, y))
        via_chip = 2 * via[0] + via[1]

        def copy(a, k, src, dst, to):
            return pltpu.make_async_remote_copy(
                src_ref=src, dst_ref=dst, send_sem=send_sems.at[a, k], recv_sem=recv_sems.at[a, k],
                device_id=(*to, c), device_id_type=MESH)

        neighbours = [(1 - x, y), (x, 1 - y)]
        own = [pltpu.make_async_copy(ins[a].at[my_chip], slots[a].at[my_chip], local_sems.at[a]) for a in range(n)]
        sends = [copy(a, j, ins[a].at[2 * px + py], slots[a].at[my_chip], (px, py))
                 for a in range(n) for j, (px, py) in enumerate(neighbours)]
        sends += [copy(a, 2, ins[a].at[diagonal], stage[a], via) for a in range(n)]
        staged = [copy(a, 2, ins[a].at[diagonal], stage[a], via) for a in range(n)]
        passed_on = [copy(a, 3, stage[a], slots[a].at[via_chip], onward) for a in range(n)]
        arrivals = [copy(a, j, ins[a].at[my_chip], slots[a].at[2 * px + py], (px, py))
                    for a in range(n) for j, (px, py) in enumerate(neighbours)]
        arrivals += [copy(a, 3, stage[a], slots[a].at[diagonal], onward) for a in range(n)]
        return own, sends, staged, passed_on, arrivals

    def start(self, ins, outs, sems):
        own, sends, _, _, _ = self._copies(ins, outs, sems)
        for cp in own + sends:
            cp.start()

    def relay(self, ins, outs, sems):
        _, _, staged, passed_on, _ = self._copies(ins, outs, sems)
        for arrival, onward in zip(staged, passed_on):
            arrival.wait_recv()
            onward.start()

    def finish(self, ins, outs, sems):
        own, sends, _, passed_on, arrivals = self._copies(ins, outs, sems)
        for cp in arrivals:
            cp.wait_recv()
        for cp in sends + passed_on:
            cp.wait_send()
        for cp in own:
            cp.wait()


def _split_plans(plans, refs_in, refs_out, refs_scr, phase):
    i = o = s = 0
    for p in plans:
        ni, no, ns = len(p.ins), len(p.out_shape), len(p.scratch)
        getattr(p, phase)(refs_in[i:i + ni], refs_out[o:o + no], refs_scr[s:s + ns])
        i, o, s = i + ni, o + no, s + ns


def _plan_results(plans, res):
    out = []
    for p in plans:
        out.append(list(res[:len(p.out_shape)]))
        res = res[len(p.out_shape):]
    return out


def _exchange(plans, name):
    c_in = [a for p in plans for a in p.ins]
    c_out = [s for p in plans for s in p.out_shape]
    c_scr = [s for p in plans for s in p.scratch]

    def body(*refs):
        cin, cout, cscr = refs[:len(c_in)], refs[len(c_in):len(c_in) + len(c_out)], refs[len(c_in) + len(c_out):]
        for phase in ("start", "relay", "finish"):
            _split_plans(plans, cin, cout, cscr, phase)

    res = pl.pallas_call(body, name=name, in_specs=[_any()] * len(c_in), out_specs=[_any()] * len(c_out),
                         out_shape=c_out, scratch_shapes=c_scr)(*c_in)
    return _plan_results(plans, res)


def _call(body, *, name, grid, in_specs, out_specs, out_shape, args, semantics, scratch_shapes=(), plans=()):
    n_in, n_out, n_scr = len(in_specs), len(out_specs), len(scratch_shapes)
    c_in = [a for p in plans for a in p.ins]
    c_out = [s for p in plans for s in p.out_shape]
    c_scr = [s for p in plans for s in p.scratch]
    n_steps = math.prod(grid)

    def wrapped(*refs):
        ins, refs = refs[:n_in], refs[n_in:]
        cin, refs = refs[:len(c_in)], refs[len(c_in):]
        outs, refs = refs[:n_out], refs[n_out:]
        cout, refs = refs[:len(c_out)], refs[len(c_out):]
        scr, cscr = refs[:n_scr], refs[n_scr:]
        if plans:
            step = 0
            for d, g in enumerate(grid):
                step = step * g + pl.program_id(d)
            pl.when(step == 0)(lambda: _split_plans(plans, cin, cout, cscr, "start"))
        body(*ins, *outs, *scr)
        if plans:
            pl.when(step == max(n_steps - 2, 0))(lambda: _split_plans(plans, cin, cout, cscr, "relay"))
            pl.when(step == n_steps - 1)(lambda: _split_plans(plans, cin, cout, cscr, "finish"))

    res = pl.pallas_call(
        wrapped, name=name, grid=grid, in_specs=list(in_specs) + [_any()] * len(c_in),
        out_specs=list(out_specs) + [_any()] * len(c_out), out_shape=list(out_shape) + c_out,
        scratch_shapes=list(scratch_shapes) + c_scr,
        compiler_params=_params(*(("arbitrary",) * len(grid) if plans else semantics)),
    )(*args, *c_in)
    return list(res[:n_out]), _plan_results(plans, res[n_out:])


def _ffn_fwd(h, mods, gn, wg, wu, wd, row0, seq, tgt=None, plans=()):
    T = h.shape[0]
    tm = _token_tile(seq, 256)
    tps = seq // tm
    n_t = T // tm
    with_loss = tgt is not None

    def body(h_ref, m_ref, g_ref, wg_ref, wu_ref, wd_ref, *rest):
        if with_loss:
            tgt_ref, out_ref, y_ref, gb_ref, ub_ref, loss_ref = rest
        else:
            out_ref, y_ref, gb_ref, ub_ref = rest
        hv = h_ref[...]
        sh = m_ref[0, row0:row0 + 1, :]
        sc = m_ref[0, row0 + 1:row0 + 2, :]
        ga = m_ref[0, row0 + 2:row0 + 3, :]
        xn, _, _ = _rms_mod(hv, g_ref[...], sh, sc)
        xb = xn.astype(BF16)
        acc = jnp.zeros((tm, D), F32)
        for c0 in range(0, FF, FC):
            gg = _dot_nt(xb, wg_ref[c0:c0 + FC, :])
            uu = _dot_nt(xb, wu_ref[c0:c0 + FC, :])
            gg, uu = gg.astype(BF16), uu.astype(BF16)
            gb_ref[:, c0:c0 + FC] = gg
            ub_ref[:, c0:c0 + FC] = uu
            acc = acc + _dot((gg * jax.nn.sigmoid(gg)) * uu, wd_ref[c0:c0 + FC, :])
        y_ref[...] = acc
        hout = hv + (0.5 * ga) * acc
        if with_loss:
            d = hout - tgt_ref[...]
            out_ref[...] = d * (1.0 / D)
            loss_ref[...] = jnp.full((1, 8, 128), 0.5 / D, F32) * jnp.sum(d * d)
        else:
            out_ref[...] = hout

    tok = pl.BlockSpec((tm, D), lambda i: (i, 0))
    tokf = pl.BlockSpec((tm, FF), lambda i: (i, 0))
    in_specs = [tok, pl.BlockSpec((1, 9, D), lambda i: (i // tps, 0, 0)), pl.BlockSpec((1, D), lambda i: (0, 0)),
                _vmem(), _vmem(), _vmem()]
    out_shape = [jax.ShapeDtypeStruct((T, D), F32), jax.ShapeDtypeStruct((T, D), F32),
                 jax.ShapeDtypeStruct((T, FF), BF16), jax.ShapeDtypeStruct((T, FF), BF16)]
    out_specs = [tok, tok, tokf, tokf]
    args = [h, mods, gn, wg, wu, wd]
    if with_loss:
        in_specs.append(tok)
        args.append(tgt)
        out_shape.append(jax.ShapeDtypeStruct((n_t, 8, 128), F32))
        out_specs.append(pl.BlockSpec((1, 8, 128), lambda i: (i, 0, 0)))
    return _call(body, name="ffn_fwd_loss" if with_loss else "ffn_fwd", grid=(n_t,), in_specs=in_specs,
                 out_specs=out_specs, out_shape=out_shape, args=args, semantics=("parallel",), plans=plans)


def _ffn_gate_up(h, mods, gn, wg, wu, row0, seq, plans=()):
    T = h.shape[0]
    tm = _token_tile(seq, 512)
    tps = seq // tm

    def body(h_ref, m_ref, g_ref, wg_ref, wu_ref, gb_ref, ub_ref):
        xn, _, _ = _rms_mod(h_ref[...], g_ref[...], m_ref[0, row0:row0 + 1, :], m_ref[0, row0 + 1:row0 + 2, :])
        xb = xn.astype(BF16)
        for c0 in range(0, FF, FC):
            gb_ref[:, c0:c0 + FC] = _dot_nt(xb, wg_ref[c0:c0 + FC, :]).astype(BF16)
            ub_ref[:, c0:c0 + FC] = _dot_nt(xb, wu_ref[c0:c0 + FC, :]).astype(BF16)

    tokf = pl.BlockSpec((tm, FF), lambda i: (i, 0))
    return _call(
        body, name="ffn_gate_up", grid=(T // tm,),
        in_specs=[pl.BlockSpec((tm, D), lambda i: (i, 0)), pl.BlockSpec((1, 9, D), lambda i: (i // tps, 0, 0)),
                  pl.BlockSpec((1, D), lambda i: (0, 0)), _vmem(), _vmem()],
        out_specs=[tokf, tokf], out_shape=[jax.ShapeDtypeStruct((T, FF), BF16)] * 2, args=(h, mods, gn, wg, wu),
        semantics=("parallel",), plans=plans)


def _ffn_down(h, gb, ub, mods, wd, row0, seq, plans=()):
    T = h.shape[0]
    tm = _token_tile(seq, 512)
    tps = seq // tm

    def body(h_ref, gb_ref, ub_ref, m_ref, wd_ref, out_ref, y_ref):
        acc = jnp.zeros((tm, D), F32)
        for c0 in range(0, FF, FC):
            gg = gb_ref[:, c0:c0 + FC]
            acc = acc + _dot((gg * jax.nn.sigmoid(gg)) * ub_ref[:, c0:c0 + FC], wd_ref[c0:c0 + FC, :])
        y_ref[...] = acc
        out_ref[...] = h_ref[...] + (0.5 * m_ref[0, row0 + 2:row0 + 3, :]) * acc

    tok = pl.BlockSpec((tm, D), lambda i: (i, 0))
    tokf = pl.BlockSpec((tm, FF), lambda i: (i, 0))
    return _call(
        body, name="ffn_down", grid=(T // tm,),
        in_specs=[tok, tokf, tokf, pl.BlockSpec((1, 9, D), lambda i: (i // tps, 0, 0)), _vmem()],
        out_specs=[tok, tok], out_shape=[jax.ShapeDtypeStruct((T, D), F32)] * 2, args=(h, gb, ub, mods, wd),
        semantics=("parallel",), plans=plans)


def _ffn_bwd(h, dhn, y, gb, ub, mods, gn, wg, wu, wd, row0, seq, plans=()):
    T = h.shape[0]
    B = T // seq
    tm = _token_tile(seq, 256)
    tps = seq // tm
    n_t = T // tm

    def body(h_ref, dhn_ref, y_ref, gb_ref, ub_ref, m_ref, g_ref, wg_ref, wu_ref, wd_ref,
             dh_ref, xb_ref, dyb_ref, a_ref, dg_ref, du_ref, mg_ref):
        i = pl.program_id(0)
        hv = h_ref[...]
        dhn = dhn_ref[...]
        sh = m_ref[0, row0:row0 + 1, :]
        sc = m_ref[0, row0 + 1:row0 + 2, :]
        ga = m_ref[0, row0 + 2:row0 + 3, :]
        g = g_ref[...]
        xn, r, inv = _rms_mod(hv, g, sh, sc)
        xb_ref[...] = xn.astype(BF16)
        dyb = ((0.5 * ga) * dhn).astype(BF16)
        dyb_ref[...] = dyb
        dga = 0.5 * jnp.sum(dhn * y_ref[...], axis=0, keepdims=True)
        dxn = jnp.zeros((tm, D), F32)
        for c0 in range(0, FF, FC):
            wgc = wg_ref[c0:c0 + FC, :]
            wuc = wu_ref[c0:c0 + FC, :]
            gg = gb_ref[:, c0:c0 + FC]
            uu = ub_ref[:, c0:c0 + FC]
            sig = jax.nn.sigmoid(gg)
            s = gg * sig
            a_ref[:, c0:c0 + FC] = s * uu
            da = _dot_nt(dyb, wd_ref[c0:c0 + FC, :]).astype(BF16)
            dub = da * s
            dgb = (da * uu) * (sig * (1.0 + gg * (1.0 - sig)))
            dg_ref[:, c0:c0 + FC] = dgb
            du_ref[:, c0:c0 + FC] = dub
            dxn = dxn + _dot(dgb, wgc) + _dot(dub, wuc)
        dh, s_dxn, s_dxr = _rms_mod_bwd(dxn, r, inv, g, sc)
        dh_ref[...] = dhn + dh

        @pl.when(i % tps == 0)
        def _():
            mg_ref[...] = jnp.zeros(mg_ref.shape, F32)

        mg_ref[0, 0:1, :] += s_dxn
        mg_ref[0, 1:2, :] += s_dxr
        mg_ref[0, 2:3, :] += dga

    tok = pl.BlockSpec((tm, D), lambda i: (i, 0))
    tokf = pl.BlockSpec((tm, FF), lambda i: (i, 0))
    return _call(
        body, name="ffn_bwd", grid=(n_t,),
        in_specs=[tok, tok, tok, tokf, tokf, pl.BlockSpec((1, 9, D), lambda i: (i // tps, 0, 0)),
                  pl.BlockSpec((1, D), lambda i: (0, 0)), _vmem(), _vmem(), _vmem()],
        out_specs=[tok, tok, tok, tokf, tokf, tokf, pl.BlockSpec((1, 8, D), lambda i: (i // tps, 0, 0))],
        out_shape=[jax.ShapeDtypeStruct((T, D), F32), jax.ShapeDtypeStruct((T, D), BF16),
                   jax.ShapeDtypeStruct((T, D), BF16), jax.ShapeDtypeStruct((T, FF), BF16),
                   jax.ShapeDtypeStruct((T, FF), BF16), jax.ShapeDtypeStruct((T, FF), BF16),
                   jax.ShapeDtypeStruct((B, 8, D), F32)],
        args=(h, dhn, y, gb, ub, mods, gn, wg, wu, wd), semantics=("arbitrary",), plans=plans)


def _wgrad(a, b, name, plans=()):
    T, da = a.shape
    db = b.shape[1]
    bm = {2816: 1408, 3840: 1280}[da]
    bn = db
    tk = min(1024, T)
    nk = T // tk

    def body(a_ref, b_ref, o_ref):
        @pl.when(pl.program_id(2) == 0)
        def _():
            o_ref[...] = jnp.zeros(o_ref.shape, F32)

        o_ref[...] += _dot_tn(a_ref[...], b_ref[...])

    (out,), plan_outs = _call(
        body, name=name, grid=(da // bm, db // bn, nk),
        in_specs=[pl.BlockSpec((tk, bm), lambda i, j, k: (k, i)), pl.BlockSpec((tk, bn), lambda i, j, k: (k, j))],
        out_specs=[pl.BlockSpec((bm, bn), lambda i, j, k: (i, j))], out_shape=[jax.ShapeDtypeStruct((da, db), F32)],
        args=(a, b), semantics=("parallel", "parallel", "arbitrary"), plans=plans)
    return out, plan_outs


def _inproj_fwd(h, mods, gn, w_in, seq, plans=()):
    T = h.shape[0]
    tm = _token_tile(seq)
    tps = seq // tm

    def body(h_ref, m_ref, g_ref, w_ref, *outs):
        xn, _, _ = _rms_mod(h_ref[...], g_ref[...], m_ref[0, 3:4, :], m_ref[0, 4:5, :])
        xb = xn.astype(BF16)
        for (_, c0, w), o_ref in zip(PIECES, outs):
            o_ref[...] = _dot_nt(xb, w_ref[c0:c0 + w, :])

    return _call(
        body, name="inproj_fwd", grid=(T // tm,),
        in_specs=[pl.BlockSpec((tm, D), lambda i: (i, 0)), pl.BlockSpec((1, 9, D), lambda i: (i // tps, 0, 0)),
                  pl.BlockSpec((1, D), lambda i: (0, 0)), _vmem()],
        out_specs=[pl.BlockSpec((tm, w), lambda i: (i, 0)) for _, _, w in PIECES],
        out_shape=[jax.ShapeDtypeStruct((T, w), F32) for _, _, w in PIECES], args=(h, mods, gn, w_in),
        semantics=("parallel",), plans=plans)


def _inproj_bwd(h, dh_res, dpieces, mods, gn, w_in, seq, plans=()):
    T = h.shape[0]
    B = T // seq
    tm = _token_tile(seq, 256)
    tps = seq // tm

    def body(h_ref, dres_ref, *rest):
        dp_refs = rest[:len(PIECES)]
        m_ref, g_ref, w_ref, dh_ref, xb_ref, dpb_ref, mg_ref = rest[len(PIECES):]
        i = pl.program_id(0)
        g = g_ref[...]
        sc = m_ref[0, 4:5, :]
        xn, r, inv = _rms_mod(h_ref[...], g, m_ref[0, 3:4, :], sc)
        xb_ref[...] = xn.astype(BF16)
        dxn = jnp.zeros((tm, D), F32)
        for (_, c0, w), dp_ref in zip(PIECES, dp_refs):
            dpb = dp_ref[...].astype(BF16)
            dpb_ref[:, c0:c0 + w] = dpb
            dxn = dxn + _dot(dpb, w_ref[c0:c0 + w, :])
        dh, s_dxn, s_dxr = _rms_mod_bwd(dxn, r, inv, g, sc)
        dh_ref[...] = dres_ref[...] + dh

        @pl.when(i % tps == 0)
        def _():
            mg_ref[...] = jnp.zeros(mg_ref.shape, F32)

        mg_ref[0, 0:1, :] += s_dxn
        mg_ref[0, 1:2, :] += s_dxr

    tok = pl.BlockSpec((tm, D), lambda i: (i, 0))
    return _call(
        body, name="inproj_bwd", grid=(T // tm,),
        in_specs=[tok, tok] + [pl.BlockSpec((tm, w), lambda i: (i, 0)) for _, _, w in PIECES]
        + [pl.BlockSpec((1, 9, D), lambda i: (i // tps, 0, 0)), pl.BlockSpec((1, D), lambda i: (0, 0)), _vmem()],
        out_specs=[tok, tok, pl.BlockSpec((tm, IN_COLS), lambda i: (i, 0)),
                   pl.BlockSpec((1, 8, D), lambda i: (i // tps, 0, 0))],
        out_shape=[jax.ShapeDtypeStruct((T, D), F32), jax.ShapeDtypeStruct((T, D), BF16),
                   jax.ShapeDtypeStruct((T, IN_COLS), BF16), jax.ShapeDtypeStruct((B, 8, D), F32)],
        args=(h, dh_res, *dpieces, mods, gn, w_in), semantics=("arbitrary",), plans=plans)


def _seg_mean(x):
    i = lax.broadcasted_iota(jnp.int32, (128, 128), 0) >> 6
    j = lax.broadcasted_iota(jnp.int32, (128, 128), 1) >> 6
    ones = jnp.where(i == j, 1.0 / HD, 0.0).astype(BF16)
    hi = x.astype(BF16)
    lo = (x - hi.astype(F32)).astype(BF16)
    return _dot(hi, ones) + _dot(lo, ones)


def _head_norm(x, g2):
    inv = lax.rsqrt(_seg_mean(x * x) + EPS)
    y = x * inv
    return y * g2, y, inv


def _head_norm_bwd(dxn, y, inv, g2):
    dy = dxn * g2
    return inv * (dy - y * _seg_mean(dy * y)), jnp.sum(dxn * y, axis=0, keepdims=True)


def _swa_block(q, kk, vv, gq2, gk2, sinks, first, do=None):
    lo = lax.broadcasted_iota(jnp.int32, (1, 128), 1) < HD
    kn, ky, kinv = _head_norm(kk, gk2)

    def operands(x):
        xr = pltpu.roll(x, HD, 1)
        own_lo, own_hi = jnp.where(lo, x, 0.0).astype(BF16), jnp.where(lo, 0.0, x).astype(BF16)
        rolled_lo, rolled_hi = jnp.where(lo, xr, 0.0).astype(BF16), jnp.where(lo, 0.0, xr).astype(BF16)
        return (own_lo, rolled_hi), (rolled_lo, own_hi)

    def restore(parts):
        (own_lo, rolled_hi), (rolled_lo, own_hi) = parts
        return (jnp.where(lo, own_lo, own_hi)
                + pltpu.roll(jnp.where(lo, rolled_lo, rolled_hi), HD, 1))

    k_ops, v_ops = operands(kn), operands(vv)
    k2 = [jnp.concatenate(pair, axis=0) for pair in k_ops]
    def stack(x):
        return jnp.concatenate([x[:, 128 * p:128 * (p + 1)] for p in range(4)], axis=0)

    def unstack(x):
        return jnp.concatenate([x[BLK * p:BLK * (p + 1)] for p in range(4)], axis=1)

    def of_head(x, kh):
        return x[2 * BLK * kh:2 * BLK * (kh + 1)]

    nq = 4 * BLK
    pair = lax.broadcasted_iota(jnp.int32, (nq, 1), 0) >> 7
    row = lax.broadcasted_iota(jnp.int32, (nq, 2 * BLK), 0) & (BLK - 1)
    col = lax.broadcasted_iota(jnp.int32, (nq, 2 * BLK), 1)
    valid = (col <= row + BLK) & (col > row) & (col >= jnp.where(first, BLK, 0))
    scale = HD ** -0.5
    qn, qy, qinv = _head_norm(stack(q), gq2)
    qnb = qn.astype(BF16)
    s2 = jnp.concatenate([_dot_nt(of_head(qnb, kh), k2[kh]) for kh in range(N_KV)], axis=0) * scale
    probs, p_sink = [], []
    for j in range(2):
        s = jnp.where(valid, s2[:, 2 * BLK * j:2 * BLK * (j + 1)], NEG)
        sink = jnp.zeros((nq, 1), F32)
        for p in range(4):
            sink = jnp.where(pair == p, sinks[:, 2 * p + j:2 * p + j + 1], sink)
        m = jnp.maximum(jnp.max(s, axis=-1, keepdims=True), sink)
        e = jnp.exp(s - m)
        e_sink = jnp.exp(sink - m)
        rden = 1.0 / (jnp.sum(e, axis=-1, keepdims=True) + e_sink)
        probs.append(e * rden)
        p_sink.append(e_sink * rden)
    pb = [p.astype(BF16) for p in probs]
    if do is None:
        return unstack(jnp.concatenate(
            [_dot(of_head(pb[0], kh), v_ops[kh][0]) + _dot(of_head(pb[1], kh), v_ops[kh][1]) for kh in range(N_KV)],
            axis=0))
    dob = stack(do).astype(BF16)
    ds, dsinks = [], [None] * 8
    for j in range(2):
        dp = jnp.concatenate([_dot_nt(of_head(dob, kh), v_ops[kh][j]) for kh in range(N_KV)], axis=0)
        t = jnp.sum(dp * probs[j], axis=-1, keepdims=True)
        ds.append(probs[j] * (dp - t) * scale)
        lost = p_sink[j] * t
        for p in range(4):
            dsinks[2 * p + j] = -jnp.sum(lost[BLK * p:BLK * (p + 1)])
    dsb = jnp.concatenate(ds, axis=1).astype(BF16)
    dqn = jnp.concatenate([_dot(of_head(dsb, kh), k2[kh]) for kh in range(N_KV)], axis=0)
    dq, dgq2 = _head_norm_bwd(dqn, qy, qinv, gq2)
    dk2 = [_dot_tn(of_head(dsb, kh), of_head(qnb, kh)) for kh in range(N_KV)]
    dv_ops = [[_dot_tn(of_head(pb[j], kh), of_head(dob, kh)) for j in range(2)] for kh in range(N_KV)]
    dkn = restore(tuple((d[:2 * BLK], d[2 * BLK:]) for d in dk2))
    dvv = restore(tuple(tuple(d) for d in dv_ops))
    dkk, dgk2 = _head_norm_bwd(dkn, ky, kinv, gk2)
    return unstack(dq), dkk, dvv, dgq2, dgk2, dsinks


SWA_GROUP = 8


def _swa_specs(nbs):
    grp = min(SWA_GROUP, nbs)
    rows = grp * BLK

    def tok(w):
        return pl.BlockSpec((rows, w), lambda i: (i, 0))

    halo = pl.BlockSpec((BLK, 128), lambda i: (jnp.maximum(i * grp - 1, 0), 0))
    vec = pl.BlockSpec((1, 128), lambda i: (0, 0))
    sk = pl.BlockSpec((1, 8), lambda i: (0, 0))
    return grp, tok, halo, vec, sk


def _swa_fwd(q, k, v, gq2, gk2, sinks, nbs, plans=()):
    T = q.shape[0]
    grp, tok, halo, vec, sk = _swa_specs(nbs)

    def body(q_ref, kh_ref, kc_ref, vh_ref, vc_ref, gq_ref, gk_ref, sk_ref, o_ref):
        seq_start = ((pl.program_id(0) * grp) % nbs) == 0
        for g in range(grp):
            rows = slice(g * BLK, (g + 1) * BLK)
            prev = slice((g - 1) * BLK, g * BLK)
            kk = jnp.concatenate([kh_ref[...] if g == 0 else kc_ref[prev, :], kc_ref[rows, :]], axis=0)
            vv = jnp.concatenate([vh_ref[...] if g == 0 else vc_ref[prev, :], vc_ref[rows, :]], axis=0)
            o_ref[rows, :] = _swa_block(q_ref[rows, :], kk, vv, gq_ref[...], gk_ref[...], sk_ref[...],
                                        seq_start if g == 0 else False)

    (out,), plan_outs = _call(
        body, name="swa_fwd", grid=(T // (grp * BLK),),
        in_specs=[tok(D_B), halo, tok(128), halo, tok(128), vec, vec, sk], out_specs=[tok(D_B)],
        out_shape=[jax.ShapeDtypeStruct((T, D_B), F32)], args=(q, k, k, v, v, gq2, gk2, sinks),
        semantics=("parallel",), plans=plans)
    return out, plan_outs


def _swa_bwd(q, k, v, gq2, gk2, sinks, do, nbs, plans=()):
    T = q.shape[0]
    grp, tok, halo, vec, sk = _swa_specs(nbs)
    steps = T // (grp * BLK)

    def body(q_ref, kh_ref, kc_ref, vh_ref, vc_ref, gq_ref, gk_ref, sk_ref, do_ref,
             dq_ref, dk_ref, dv_ref, dkh_ref, dvh_ref, dgq_ref, dgk_ref, dsk_ref):
        i = pl.program_id(0)
        seq_start = ((i * grp) % nbs) == 0

        @pl.when(i == 0)
        def _():
            for r in (dgq_ref, dgk_ref, dsk_ref):
                r[...] = jnp.zeros(r.shape, F32)

        res = []
        for g in range(grp):
            rows = slice(g * BLK, (g + 1) * BLK)
            prev = slice((g - 1) * BLK, g * BLK)
            kk = jnp.concatenate([kh_ref[...] if g == 0 else kc_ref[prev, :], kc_ref[rows, :]], axis=0)
            vv = jnp.concatenate([vh_ref[...] if g == 0 else vc_ref[prev, :], vc_ref[rows, :]], axis=0)
            res.append(_swa_block(q_ref[rows, :], kk, vv, gq_ref[...], gk_ref[...], sk_ref[...],
                                  seq_start if g == 0 else False, do=do_ref[rows, :]))
        lane = lax.broadcasted_iota(jnp.int32, (8, 128), 1)
        upd = jnp.zeros((8, 128), F32)
        for g, (dq, dkk, dvv, dgq2, dgk2, dsinks) in enumerate(res):
            rows = slice(g * BLK, (g + 1) * BLK)
            dq_ref[rows, :] = dq
            dk_ref[rows, :] = dkk[BLK:] + res[g + 1][1][:BLK] if g + 1 < grp else dkk[BLK:]
            dv_ref[rows, :] = dvv[BLK:] + res[g + 1][2][:BLK] if g + 1 < grp else dvv[BLK:]
            dgq_ref[...] += dgq2
            dgk_ref[...] += dgk2
            for h, d in enumerate(dsinks):
                upd = upd + jnp.where(lane == h, d, 0.0)
        dkh_ref[0] = res[0][1][:BLK]
        dvh_ref[0] = res[0][2][:BLK]
        dsk_ref[...] += upd

    one = pl.BlockSpec((1, BLK, 128), lambda i: (i, 0, 0))
    halo_shape = jax.ShapeDtypeStruct((steps, BLK, 128), F32)
    return _call(
        body, name="swa_bwd", grid=(steps,),
        in_specs=[tok(D_B), halo, tok(128), halo, tok(128), vec, vec, sk, tok(D_B)],
        out_specs=[tok(D_B), tok(128), tok(128), one, one, vec, vec, pl.BlockSpec((8, 128), lambda i: (0, 0))],
        out_shape=[jax.ShapeDtypeStruct((T, D_B), F32), jax.ShapeDtypeStruct((T, 128), F32),
                   jax.ShapeDtypeStruct((T, 128), F32), halo_shape, halo_shape, jax.ShapeDtypeStruct((1, 128), F32),
                   jax.ShapeDtypeStruct((1, 128), F32), jax.ShapeDtypeStruct((8, 128), F32)],
        args=(q, k, k, v, v, gq2, gk2, sinks, do), semantics=("arbitrary",), plans=plans)


def _swa_add_halo(dk, dk_halo):
    steps = dk_halo.shape[0]
    nxt = jnp.concatenate([dk_halo[1:], jnp.zeros_like(dk_halo[:1])], axis=0)[:, None]
    dk = dk.reshape(steps, -1, BLK, 128)
    return jnp.concatenate([dk[:, :-1], dk[:, -1:] + nxt], axis=1).reshape(-1, 128)


def _sgu_norm(av, g_ln, b_ln):
    t, th = _gelu(av)
    mu = jnp.mean(t, axis=-1, keepdims=True)
    tc = t - mu
    rstd = lax.rsqrt(jnp.mean(tc * tc, axis=-1, keepdims=True) + EPS)
    vhat = tc * rstd
    return vhat * g_ln + b_ln, vhat, rstd, th


def _masked_ws(ws_ref):
    tril = lax.broadcasted_iota(jnp.int32, (BLK, BLK), 0) >= lax.broadcasted_iota(jnp.int32, (BLK, BLK), 1)
    return [jnp.where(tril, ws_ref[g], 0.0).astype(BF16) for g in range(N_GRP)]


def _mix_fwd(au, av, gta, gtb, ob, h, mods, g_ln, b_ln, ws, bsb, wa, wb, wout, seq, plans=()):
    T = h.shape[0]
    tm = _token_tile(seq)
    tps = seq // tm

    def body(au_ref, av_ref, gta_ref, gtb_ref, ob_ref, h_ref, m_ref, gl_ref, bl_ref, ws_ref, bs_ref,
             wa_ref, wb_ref, wo_ref, out_ref, vvb_s, z_s):
        u, _ = _gelu(au_ref[...])
        vv, _, _, _ = _sgu_norm(av_ref[...], gl_ref[...], bl_ref[...])
        vvb_s[...] = vv.astype(BF16)
        wsm = _masked_ws(ws_ref)
        for c in range(tm // BLK):
            rows = slice(c * BLK, (c + 1) * BLK)
            for g in range(N_GRP):
                cols = slice(g * BLK, (g + 1) * BLK)
                z_s[rows, cols] = _dot(wsm[g], vvb_s[rows, cols]) + bs_ref[g]
        ya = _dot_nt((u * z_s[...]).astype(BF16), wa_ref[...])
        yb = _dot_nt(ob_ref[...].astype(BF16), wb_ref[...])
        merged = jax.nn.sigmoid(gta_ref[...]) * ya + jax.nn.sigmoid(gtb_ref[...]) * yb
        out_ref[...] = h_ref[...] + m_ref[0, 5:6, :] * _dot(merged.astype(BF16), wo_ref[...])

    def tok(w):
        return pl.BlockSpec((tm, w), lambda i: (i, 0))

    def full(shape):
        return pl.BlockSpec(shape, lambda i: (0,) * len(shape))

    (out,), plan_outs = _call(
        body, name="mix_fwd", grid=(T // tm,),
        in_specs=[tok(D_A), tok(D_A), tok(D), tok(D), tok(D_B), tok(D),
                  pl.BlockSpec((1, 9, D), lambda i: (i // tps, 0, 0)), full((1, D_A)), full((1, D_A)),
                  full((N_GRP, BLK, BLK)), full((N_GRP, BLK, BLK)), _vmem(), _vmem(), _vmem()],
        out_specs=[tok(D)], out_shape=[jax.ShapeDtypeStruct((T, D), F32)],
        scratch_shapes=[pltpu.VMEM((tm, D_A), BF16), pltpu.VMEM((tm, D_A), F32)],
        args=(au, av, gta, gtb, ob, h, mods, g_ln, b_ln, ws, bsb, wa, wb, wout), semantics=("parallel",),
        plans=plans)
    return out, plan_outs


def _mix_bwd(au, av, gta, gtb, ob, dh, mods, g_ln, b_ln, ws, bsb, wa, wb, wout, seq, plans=()):
    T = dh.shape[0]
    B = T // seq
    tm = _token_tile(seq, 256)
    tps = seq // tm

    def body(au_ref, av_ref, gta_ref, gtb_ref, ob_ref, dh_ref, m_ref, gl_ref, bl_ref, ws_ref, bs_ref,
             wa_ref, wb_ref, wo_ref,
             dau_ref, dav_ref, dgta_ref, dgtb_ref, dob_ref, dwo_ref, dwa_ref, dwb_ref, dws_ref, dbs_ref, dln_ref,
             mg_ref, vvb_s, z_s, dz_s, dzb_s, dvv_s):
        i = pl.program_id(0)

        @pl.when(i == 0)
        def _():
            for r in (dwo_ref, dwa_ref, dwb_ref, dws_ref, dbs_ref, dln_ref):
                r[...] = jnp.zeros(r.shape, F32)

        @pl.when(i % tps == 0)
        def _():
            mg_ref[...] = jnp.zeros(mg_ref.shape, F32)

        auv = au_ref[...]
        avv = av_ref[...]
        u, thu = _gelu(auv)
        g_ln = gl_ref[...]
        vv, vhat, rstd, thv = _sgu_norm(avv, g_ln, bl_ref[...])
        vvb_s[...] = vv.astype(BF16)
        wsm = _masked_ws(ws_ref)
        for c in range(tm // BLK):
            rows = slice(c * BLK, (c + 1) * BLK)
            for g in range(N_GRP):
                cols = slice(g * BLK, (g + 1) * BLK)
                z_s[rows, cols] = _dot(wsm[g], vvb_s[rows, cols]) + bs_ref[g]
        z = z_s[...]
        yab = (u * z).astype(BF16)
        obb = ob_ref[...].astype(BF16)
        ya = _dot_nt(yab, wa_ref[...])
        yb = _dot_nt(obb, wb_ref[...])
        sa = jax.nn.sigmoid(gta_ref[...])
        sb = jax.nn.sigmoid(gtb_ref[...])
        mb = (sa * ya + sb * yb).astype(BF16)
        dhv = dh_ref[...]
        mg_ref[0, 0:1, :] += jnp.sum(dhv * _dot(mb, wo_ref[...]), axis=0, keepdims=True)
        dmob = (m_ref[0, 5:6, :] * dhv).astype(BF16)
        dwo_ref[...] += _dot_tn(mb, dmob)
        dmerged = _dot_nt(dmob, wo_ref[...])
        dya = dmerged * sa
        dyb = dmerged * sb
        dgta_ref[...] = dya * ya * (1.0 - sa)
        dgtb_ref[...] = dyb * yb * (1.0 - sb)
        dyab = dya.astype(BF16)
        dybb = dyb.astype(BF16)
        dwa_ref[...] += _dot_tn(dyab, yab)
        dwb_ref[...] += _dot_tn(dybb, obb)
        dob_ref[...] = _dot(dybb, wb_ref[...])
        dyap = _dot(dyab, wa_ref[...])
        dau_ref[...] = (dyap * z) * _gelu_grad(auv, thu)
        dz = dyap * u
        dz_s[...] = dz
        dzb_s[...] = dz.astype(BF16)
        for c in range(tm // BLK):
            rows = slice(c * BLK, (c + 1) * BLK)
            for g in range(N_GRP):
                cols = slice(g * BLK, (g + 1) * BLK)
                dzb = dzb_s[rows, cols]
                dvv_s[rows, cols] = _dot_tn(wsm[g], dzb)
                dws_ref[g] += _dot_nt(dzb, vvb_s[rows, cols])
                dbs_ref[g] += dz_s[rows, cols]
        dvv = dvv_s[...]
        dln_ref[0:1, :] += jnp.sum(dvv * vhat, axis=0, keepdims=True)
        dln_ref[1:2, :] += jnp.sum(dvv, axis=0, keepdims=True)
        dvh = dvv * g_ln
        dt = rstd * (dvh - jnp.mean(dvh, axis=-1, keepdims=True)
                     - vhat * jnp.mean(dvh * vhat, axis=-1, keepdims=True))
        dav_ref[...] = dt * _gelu_grad(avv, thv)

    def tok(w):
        return pl.BlockSpec((tm, w), lambda i: (i, 0))

    def full(shape):
        return pl.BlockSpec(shape, lambda i: (0,) * len(shape))

    return _call(
        body, name="mix_bwd", grid=(T // tm,),
        in_specs=[tok(D_A), tok(D_A), tok(D), tok(D), tok(D_B), tok(D),
                  pl.BlockSpec((1, 9, D), lambda i: (i // tps, 0, 0)), full((1, D_A)), full((1, D_A)),
                  full((N_GRP, BLK, BLK)), full((N_GRP, BLK, BLK)), _vmem(), _vmem(), _vmem()],
        out_specs=[tok(D_A), tok(D_A), tok(D), tok(D), tok(D_B), full((D, D)), full((D, D_A)), full((D, D_B)),
                   full((N_GRP, BLK, BLK)), full((N_GRP, BLK, BLK)), full((8, D_A)),
                   pl.BlockSpec((1, 8, D), lambda i: (i // tps, 0, 0))],
        out_shape=[jax.ShapeDtypeStruct((T, D_A), F32), jax.ShapeDtypeStruct((T, D_A), F32),
                   jax.ShapeDtypeStruct((T, D), F32), jax.ShapeDtypeStruct((T, D), F32),
                   jax.ShapeDtypeStruct((T, D_B), F32), jax.ShapeDtypeStruct((D, D), F32),
                   jax.ShapeDtypeStruct((D, D_A), F32), jax.ShapeDtypeStruct((D, D_B), F32),
                   jax.ShapeDtypeStruct((N_GRP, BLK, BLK), F32), jax.ShapeDtypeStruct((N_GRP, BLK, BLK), F32),
                   jax.ShapeDtypeStruct((8, D_A), F32), jax.ShapeDtypeStruct((B, 8, D), F32)],
        scratch_shapes=[pltpu.VMEM((tm, D_A), BF16), pltpu.VMEM((tm, D_A), F32), pltpu.VMEM((tm, D_A), F32),
                        pltpu.VMEM((tm, D_A), BF16), pltpu.VMEM((tm, D_A), F32)],
        args=(au, av, gta, gtb, ob, dh, mods, g_ln, b_ln, ws, bsb, wa, wb, wout), semantics=("arbitrary",),
        plans=plans)


def _prologue(c_pad, w_ada, first_shards):
    cols = w_ada.shape[1]
    plan_w, plan_c = _GatherRelayed(first_shards), _GatherDirect([c_pad])
    plan_m = _Gather([jax.ShapeDtypeStruct((N_DEV * 8, cols), F32)])
    n_w = len(first_shards)

    def body(c_ref, wada_ref, *rest):
        w_ins, rest = rest[:n_w], rest[n_w:]
        call_ref, mods_ref = rest[:2]
        w_outs, rest = rest[2:2 + n_w], rest[2 + n_w:]
        cvm, part, local_sem = rest[:3]
        sems = rest[3:]
        sems_w, sems_c, sems_m = sems[:3], sems[3:6], sems[6:9]
        plan_c.start([c_ref], [call_ref], sems_c)
        plan_c.relay([c_ref], [call_ref], sems_c)
        plan_c.finish([c_ref], [call_ref], sems_c)
        load = pltpu.make_async_copy(call_ref, cvm, local_sem)
        load.start()
        load.wait()
        cv = cvm[...].reshape(N_DEV * 8, D)
        part[...] = _dot((cv * jax.nn.sigmoid(cv)).astype(BF16), wada_ref[...].astype(BF16))
        plan_m.start([part], [mods_ref], sems_m)
        plan_w.start(w_ins, w_outs, sems_w)
        plan_m.relay([part], [mods_ref], sems_m)
        plan_m.finish([part], [mods_ref], sems_m)
        plan_w.relay(w_ins, w_outs, sems_w)
        plan_w.finish(w_ins, w_outs, sems_w)

    res = pl.pallas_call(
        body, name="prologue", in_specs=[_any(), _vmem()] + [_any()] * n_w,
        out_specs=[_any()] * (2 + n_w), out_shape=plan_c.out_shape + plan_m.out_shape + plan_w.out_shape,
        scratch_shapes=[pltpu.VMEM((N_DEV, 8, D), F32), pltpu.VMEM((N_DEV * 8, cols), F32), pltpu.SemaphoreType.DMA]
        + plan_w.scratch + plan_c.scratch + plan_m.scratch,
        compiler_params=pltpu.CompilerParams(vmem_limit_bytes=VMEM_LIMIT),
    )(c_pad, w_ada, *first_shards)
    return res[0], res[1], list(res[2:])


def _ada_update(c_all, dm_cols, w, m, v, plans=()):
    n, cols = c_all.shape[0], w.shape[1]

    def body(c_ref, dm_ref, w_ref, m_ref, v_ref, g_ref, d_ref, nm_ref, nv_ref):
        cv = c_ref[...]
        g = _dot_tn((cv * jax.nn.sigmoid(cv)).astype(BF16), dm_ref[...].astype(BF16))
        g_ref[...] = g
        d_ref[...], nm_ref[...], nv_ref[...] = _adamw(w_ref[...], g, m_ref[...], v_ref[...])

    col = pl.BlockSpec((D, 128), lambda j: (0, j))
    return _call(
        body, name="ada_update", grid=(cols // 128,),
        in_specs=[pl.BlockSpec((n, D), lambda j: (0, 0)), pl.BlockSpec((n, 128), lambda j: (0, j)), col, col, col],
        out_specs=[col] * 4, out_shape=[jax.ShapeDtypeStruct(w.shape, F32)] * 4, args=(c_all, dm_cols, w, m, v),
        semantics=("parallel",), plans=plans)


def _mod_finish(mg1, mg2, mg2g, mg3, mods, g1, g2, g3):
    B = mods.shape[0]

    def body(mg1_ref, mg2_ref, mg2g_ref, mg3_ref, m_ref, g1_ref, g2_ref, g3_ref, dm_ref, dgn_ref):
        dgn_ref[...] = jnp.zeros(dgn_ref.shape, F32)
        for k, (mg, g_ref) in enumerate(((mg1_ref, g1_ref), (mg2_ref, g2_ref), (mg3_ref, g3_ref))):
            for b in range(B):
                s_dxr = mg[b, 1:2, :]
                dm_ref[b, 3 * k:3 * k + 1, :] = mg[b, 0:1, :]
                dm_ref[b, 3 * k + 1:3 * k + 2, :] = g_ref[...] * s_dxr
                dm_ref[b, 3 * k + 2:3 * k + 3, :] = mg2g_ref[b, 0:1, :] if k == 1 else mg[b, 2:3, :]
                dgn_ref[k:k + 1, :] += (1.0 + m_ref[b, 3 * k + 1:3 * k + 2, :]) * s_dxr

    args = (mg1, mg2, mg2g, mg3, mods, g1, g2, g3)
    out_shape = [jax.ShapeDtypeStruct((B, 9, D), F32), jax.ShapeDtypeStruct((8, D), F32)]
    return pl.pallas_call(body, name="mod_finish", grid=(1,), in_specs=[_whole(a) for a in args],
                          out_specs=[_whole(o) for o in out_shape], out_shape=out_shape,
                          compiler_params=_params("arbitrary"))(*args)


def _small_update(gathered, params, ms, vs):
    n = len(SMALL)
    B = gathered[0].shape[1]

    def body(*refs):
        gdm, ggn, gln, gws, gbs, ggq, ggk, gsk, gloss = refs[:9]
        w = dict(zip(SMALL, refs[9:9 + n]))
        m = dict(zip(SMALL, refs[9 + n:9 + 2 * n]))
        v = dict(zip(SMALL, refs[9 + 2 * n:9 + 3 * n]))
        outs = refs[9 + 3 * n:-1]
        out = {name: outs[4 * k:4 * k + 4] for k, name in enumerate(SMALL)}

        def total(ref, idx):
            acc = ref[(0,) + idx]
            for dev in range(1, N_DEV):
                acc = acc + ref[(dev,) + idx]
            return acc

        def finish(name, g, idx=(Ellipsis,)):
            d, nm, nv = _adamw(w[name][idx], g, m[name][idx], v[name][idx])
            for ref, val in zip(out[name], (g, d, nm, nv)):
                ref[idx] = val

        g_bada = total(gdm, (slice(0, 1),))
        for b in range(1, B):
            g_bada = g_bada + total(gdm, (slice(b, b + 1),))
        finish("b_ada", g_bada)
        finish("g_norm1", total(ggn, (slice(0, 1),)))
        finish("g_norm2", total(ggn, (slice(1, 2),)))
        finish("g_norm3", total(ggn, (slice(2, 3),)))
        finish("g_sgu_ln", total(gln, (slice(0, 1),)))
        finish("b_sgu_ln", total(gln, (slice(1, 2),)))
        tril = lax.broadcasted_iota(jnp.int32, (BLK, BLK), 0) >= lax.broadcasted_iota(jnp.int32, (BLK, BLK), 1)
        for g in range(N_GRP):
            finish("w_spatial", jnp.where(tril, total(gws, (g,)), 0.0), (0, g))
            finish("b_spatial", jnp.sum(total(gbs, (g,)).T, axis=0, keepdims=True), (0, slice(g, g + 1)))
        finish("g_q", total(ggq, ()))
        finish("g_k", total(ggk, ()))
        finish("attn_sinks", total(gsk, (slice(0, 1), slice(0, N_KV * Q_PER_KV))))
        refs[-1][...] = total(gloss, ())

    args = list(gathered) + [params[k] for k in SMALL] + [ms[k] for k in SMALL] + [vs[k] for k in SMALL]
    out_shape = []
    for k in SMALL:
        out_shape += [jax.ShapeDtypeStruct(params[k].shape, F32)] * 4
    out_shape.append(jax.ShapeDtypeStruct((8, 128), F32))
    res = pl.pallas_call(body, name="small_update", grid=(1,), in_specs=[_whole(a) for a in args],
                         out_specs=[_whole(o) for o in out_shape], out_shape=out_shape,
                         compiler_params=_params("arbitrary"))(*args)
    return {k: res[4 * i:4 * i + 4] for i, k in enumerate(SMALL)}, res[-1][0, 0]


def _rs_add(p, r, core, name):
    _, rows, width = p.shape

    def body(c_ref, p_ref, r_ref, o_ref):
        o_ref[...] = (p_ref[...] + r_ref[...]).astype(BF16)

    return pl.pallas_call(
        body, name="rs_add_" + name, out_shape=jax.ShapeDtypeStruct((4, rows, width), BF16),
        grid_spec=pltpu.PrefetchScalarGridSpec(
            num_scalar_prefetch=1, grid=(4,),
            in_specs=[pl.BlockSpec((1, rows, width), lambda k, c_ref: (2 * k + c_ref[0], 0, 0)),
                      pl.BlockSpec((1, rows, width), lambda k, c_ref: (k, 0, 0))],
            out_specs=pl.BlockSpec((1, rows, width), lambda k, c_ref: (k, 0, 0))),
        compiler_params=_params("parallel"),
    )(core, p, r)


def _adam_big(w, m, v, r, name):
    rows, cols = w.shape

    def body(r_ref, w_ref, m_ref, v_ref, go_ref, d_ref, nm_ref, nv_ref):
        g = r_ref[0].astype(F32)
        for k in range(1, 4):
            g = g + r_ref[k].astype(F32)
        go_ref[...] = g
        d_ref[...], nm_ref[...], nv_ref[...] = _adamw(w_ref[...], g, m_ref[...], v_ref[...])

    br = rows // 2
    shard = pl.BlockSpec((br, cols), lambda i: (i, 0))
    return pl.pallas_call(body, name="adam_" + name, grid=(2,),
                          in_specs=[pl.BlockSpec((4, br, cols), lambda i: (0, i, 0)), shard, shard, shard],
                          out_specs=[shard] * 4, out_shape=[jax.ShapeDtypeStruct(w.shape, F32)] * 4,
                          compiler_params=_params("parallel"))(r, w, m, v)


def kernel(x, c, w_ada, b_ada, g_norm1, ffn1_w_gate, ffn1_w_up, ffn1_w_down, g_norm2, w_in, g_sgu_ln, b_sgu_ln, w_spatial, b_spatial, g_q, g_k, attn_sinks, w_branch_a, w_branch_b, w_out, g_norm3, ffn2_w_gate, ffn2_w_up, ffn2_w_down, loss_target, m_w_ada, m_b_ada, m_g_norm1, m_ffn1_w_gate, m_ffn1_w_up, m_ffn1_w_down, m_g_norm2, m_w_in, m_g_sgu_ln, m_b_sgu_ln, m_w_spatial, m_b_spatial, m_g_q, m_g_k, m_attn_sinks, m_w_branch_a, m_w_branch_b, m_w_out, m_g_norm3, m_ffn2_w_gate, m_ffn2_w_up, m_ffn2_w_down, v_w_ada, v_b_ada, v_g_norm1, v_ffn1_w_gate, v_ffn1_w_up, v_ffn1_w_down, v_g_norm2, v_w_in, v_g_sgu_ln, v_b_sgu_ln, v_w_spatial, v_b_spatial, v_g_q, v_g_k, v_attn_sinks, v_w_branch_a, v_w_branch_b, v_w_out, v_g_norm3, v_ffn2_w_gate, v_ffn2_w_up, v_ffn2_w_down):
    names = ("w_ada", "b_ada", "g_norm1", "ffn1_w_gate", "ffn1_w_up", "ffn1_w_down", "g_norm2", "w_in", "g_sgu_ln",
             "b_sgu_ln", "w_spatial", "b_spatial", "g_q", "g_k", "attn_sinks", "w_branch_a", "w_branch_b", "w_out",
             "g_norm3", "ffn2_w_gate", "ffn2_w_up", "ffn2_w_down")
    w = dict(zip(names, (w_ada, b_ada, g_norm1, ffn1_w_gate, ffn1_w_up, ffn1_w_down, g_norm2, w_in, g_sgu_ln,
                         b_sgu_ln, w_spatial, b_spatial, g_q, g_k, attn_sinks, w_branch_a, w_branch_b, w_out, g_norm3,
                         ffn2_w_gate, ffn2_w_up, ffn2_w_down)))
    m = dict(zip(names, (m_w_ada, m_b_ada, m_g_norm1, m_ffn1_w_gate, m_ffn1_w_up, m_ffn1_w_down, m_g_norm2, m_w_in,
                         m_g_sgu_ln, m_b_sgu_ln, m_w_spatial, m_b_spatial, m_g_q, m_g_k, m_attn_sinks, m_w_branch_a,
                         m_w_branch_b, m_w_out, m_g_norm3, m_ffn2_w_gate, m_ffn2_w_up, m_ffn2_w_down)))
    v = dict(zip(names, (v_w_ada, v_b_ada, v_g_norm1, v_ffn1_w_gate, v_ffn1_w_up, v_ffn1_w_down, v_g_norm2, v_w_in,
                         v_g_sgu_ln, v_b_sgu_ln, v_w_spatial, v_b_spatial, v_g_q, v_g_k, v_attn_sinks, v_w_branch_a,
                         v_w_branch_b, v_w_out, v_g_norm3, v_ffn2_w_gate, v_ffn2_w_up, v_ffn2_w_down)))
    B, seq, _ = x.shape
    T = B * seq
    nbs = seq // BLK
    xi, yi, ci = _place()
    me = 4 * xi + 2 * yi + ci
    core = jnp.reshape(ci, (1,)).astype(jnp.int32)
    layout = {name: (tform, n, width) for name, tform, n, width in BIG}
    shard = {name: w[name][0].astype(BF16).T if tform else w[name][0].astype(BF16) for name, tform, _, _ in BIG}
    wts, parts, big_out = {}, {}, {}

    def gather_plan(group):
        return _GatherRelayed([shard[k] for k in group])

    def take(group, gathered):
        for k, g in zip(group, gathered):
            wts[k] = g.reshape(N_DEV * layout[k][1], layout[k][2])

    def blocks(name, grad):
        return grad.reshape(N_DEV, layout[name][1], layout[name][2])

    def to_sibling(names, grads):
        return _RsSibling([blocks(k, g) for k, g in zip(names, grads)])

    def add(names, grads, from_sibling):
        for k, g, r in zip(names, grads, from_sibling):
            parts[k] = _rs_add(blocks(k, g), r, core, k)

    def to_chips(names):
        return _RsChips([parts[k] for k in names])

    def update(names, from_chips):
        for k, r in zip(names, from_chips):
            if layout[k][0]:
                big_out[k] = [o.T[None] for o in _adam_big(w[k][0].T, m[k][0].T, v[k][0].T, r, k)]
            else:
                big_out[k] = [o[None] for o in _adam_big(w[k][0], m[k][0], v[k][0], r, k)]

    g1, g2, g3, g_ln, b_ln = g_norm1, g_norm2, g_norm3, g_sgu_ln, b_sgu_ln
    gq2, gk2 = jnp.tile(g_q, (1, 2)), jnp.tile(g_k, (1, 2))
    ws = w_spatial[0]
    bsb = jnp.broadcast_to(b_spatial[0][:, :, None], (N_GRP, BLK, BLK))
    xf = x.reshape(T, D)
    tgt = loss_target.reshape(T, D)

    ffn1 = ("ffn1_w_gate", "ffn1_w_up", "ffn1_w_down")
    ffn2 = ("ffn2_w_gate", "ffn2_w_up", "ffn2_w_down")
    c_pad = jnp.concatenate([c, jnp.zeros((8 - B, D), F32)], axis=0)
    c_all, mods_cols, gathered = _prologue(c_pad, w_ada[0], [shard[k] for k in ffn1[:2]])
    take(ffn1[:2], gathered)
    c_all = c_all[:, :B].reshape(N_DEV * B, D)
    mine = lax.dynamic_slice_in_dim(mods_cols, 8 * me, B, axis=1)
    mods = (mine.transpose(1, 0, 2).reshape(B, 9 * D) + b_ada).reshape(B, 9, D)

    group = ("ffn1_w_down", "w_branch_a", "w_branch_b", "w_out")
    (gb1, ub1), (gathered,) = _ffn_gate_up(xf, mods, g1, wts["ffn1_w_gate"], wts["ffn1_w_up"], 0, seq,
                                           plans=[gather_plan(group)])
    take(group, gathered)
    group = ("w_in",)
    (h1, y1), (gathered,) = _ffn_down(xf, gb1, ub1, mods, wts["ffn1_w_down"], 0, seq, plans=[gather_plan(group)])
    take(group, gathered)
    group = ("ffn2_w_gate",)
    (au, av, q_tok, k_tok, v_tok, gta, gtb), (gathered,) = _inproj_fwd(h1, mods, g2, wts["w_in"], seq,
                                                                       plans=[gather_plan(group)])
    take(group, gathered)
    group = ("ffn2_w_up",)
    ob, (gathered,) = _swa_fwd(q_tok, k_tok, v_tok, gq2, gk2, attn_sinks, nbs, plans=[gather_plan(group)])
    take(group, gathered)
    mixw = (wts["w_branch_a"], wts["w_branch_b"], wts["w_out"])
    group = ("ffn2_w_down",)
    h2, (gathered,) = _mix_fwd(au, av, gta, gtb, ob, h1, mods, g_ln, b_ln, ws, bsb, *mixw, seq,
                               plans=[gather_plan(group)])
    take(group, gathered)
    (dh3, y3, gb3, ub3, loss_parts), _ = _ffn_fwd(h2, mods, g3, *[wts[k] for k in ffn2], 6, seq, tgt=tgt)
    loss_part = jnp.full((8, 128), jnp.sum(loss_parts[:, 0, 0]))

    (dh2, xb, dyb, a, dg, du, mg3), _ = _ffn_bwd(h2, dh3, y3, gb3, ub3, mods, g3, *[wts[k] for k in ffn2], 6, seq)
    d_gate, _ = _wgrad(dg, xb, "wgrad_ffn2_gate")
    d_up, (r,) = _wgrad(du, xb, "wgrad_ffn2_up", plans=[to_sibling(ffn2[:1], [d_gate])])
    add(ffn2[:1], [d_gate], r)
    d_down, (r,) = _wgrad(a, dyb, "wgrad_ffn2_down", plans=[to_sibling(ffn2[1:2], [d_up])])
    add(ffn2[1:2], [d_up], r)

    (dau, dav, dgta, dgtb, dob, d_out, d_a, d_b, dws, dbs, dln, mg2g), (r, from_chips) = _mix_bwd(
        au, av, gta, gtb, ob, dh2, mods, g_ln, b_ln, ws, bsb, *mixw, seq,
        plans=[to_sibling(ffn2[2:], [d_down]), to_chips(ffn2[:2])])
    add(ffn2[2:], [d_down], r)
    update(ffn2[:2], from_chips)
    mixers = ("w_out", "w_branch_a", "w_branch_b")
    (dq, dk, dv, dk_halo, dv_halo, dgq2, dgk2, dsk), (from_chips, r) = _swa_bwd(
        q_tok, k_tok, v_tok, gq2, gk2, attn_sinks, dob, nbs,
        plans=[to_chips(ffn2[2:]), to_sibling(mixers, [d_out, d_a, d_b])])
    update(ffn2[2:], from_chips)
    add(mixers, [d_out, d_a, d_b], r)
    dk, dv = _swa_add_halo(dk, dk_halo), _swa_add_halo(dv, dv_halo)
    dgq, dgk = dgq2[:, :HD] + dgq2[:, HD:], dgk2[:, :HD] + dgk2[:, HD:]
    (dh1, xb2, dpb, mg2), (from_chips, early) = _inproj_bwd(
        h1, dh2, (dau, dav, dq, dk, dv, dgta, dgtb), mods, g2, wts["w_in"], seq,
        plans=[to_chips(mixers), _Gather([dln, dws, dbs, dgq, dgk, dsk, loss_part])])
    update(mixers, from_chips)

    (dx, xb, dyb, a, dg, du, mg1), _ = _ffn_bwd(xf, dh1, y1, gb1, ub1, mods, g1, *[wts[k] for k in ffn1], 0, seq)
    dmods, dgn = _mod_finish(mg1, mg2, mg2g, mg3, mods, g1, g2, g3)
    d_gate, (late,) = _wgrad(dg, xb, "wgrad_ffn1_gate", plans=[_GatherDirect([dmods.reshape(B, 9 * D), dgn])])
    gathered = late + early
    d_up, (r,) = _wgrad(du, xb, "wgrad_ffn1_up", plans=[to_sibling(ffn1[:1], [d_gate])])
    add(ffn1[:1], [d_gate], r)
    d_in, (r, from_chips) = _wgrad(dpb, xb2, "wgrad_w_in",
                                   plans=[to_sibling(ffn1[1:2], [d_up]), to_chips(ffn1[:1])])
    add(ffn1[1:2], [d_up], r)
    update(ffn1[:1], from_chips)
    d_down, (r, from_chips) = _wgrad(a, dyb, "wgrad_ffn1_down",
                                     plans=[to_sibling(("w_in",), [d_in]), to_chips(ffn1[1:2])])
    add(("w_in",), [d_in], r)
    update(ffn1[1:2], from_chips)
    small_out, loss = _small_update(gathered, w, m, v)
    dm_cols = lax.dynamic_slice_in_dim(gathered[0].reshape(N_DEV * B, 9 * D), (9 * D // N_DEV) * me,
                                       9 * D // N_DEV, axis=1)
    ada_out, (r, from_chips) = _ada_update(c_all, dm_cols, w_ada[0], m_w_ada[0], v_w_ada[0],
                                           plans=[to_sibling(ffn1[2:], [d_down]), to_chips(("w_in",))])
    add(ffn1[2:], [d_down], r)
    update(("w_in",), from_chips)
    (from_chips,) = _exchange([to_chips(ffn1[2:])], "rs_last")
    update(ffn1[2:], from_chips)

    def leaf(kind, name):
        if name == "w_ada":
            return ada_out[kind][None]
        if name in SMALL:
            return small_out[name][kind]
        return big_out[name][kind]

    return (loss, dx.reshape(B, seq, D), *[leaf(kind, name) for kind in range(4) for name in names])
```

```python
import functools
import math

import jax
import jax.numpy as jnp
from jax import lax
from jax.experimental import pallas as pl
from jax.experimental.pallas import tpu as pltpu

F32 = jnp.float32
BF16 = jnp.bfloat16
MESH = pl.DeviceIdType.MESH
AXES = ("x", "y", "c")
N_DEV = 8

VMEM_LIMIT = 56 * 1024 * 1024

D = 1024
FF = 2816
FC = 1408
D_A = 512
D_B = 512
HD = 64
N_KV = 2
Q_PER_KV = 4
BLK = 128
N_GRP = 4
IN_COLS = 3840
PIECES = (("au", 0, 512), ("av", 512, 512), ("q", 1024, 512), ("k", 1536, 128), ("v", 1664, 128),
          ("ga", 1792, 1024), ("gb", 2816, 1024))
EPS = 1e-6
NEG = -1e30
GELU_C = math.sqrt(2.0 / math.pi)

ADAM_LR = 0.001
ADAM_B1 = 0.9
ADAM_B2 = 0.999
ADAM_EPS = 1e-08
ADAM_WD = 0.01
ADAM_STEP = 10

NT = (((1,), (1,)), ((), ()))
TN = (((0,), (0,)), ((), ()))

BIG = (("ffn1_w_gate", True, FF // N_DEV, D), ("ffn1_w_up", True, FF // N_DEV, D),
       ("ffn1_w_down", False, FF // N_DEV, D), ("w_in", True, IN_COLS // N_DEV, D),
       ("w_branch_a", True, D // N_DEV, D_A), ("w_branch_b", True, D // N_DEV, D_B), ("w_out", False, D // N_DEV, D),
       ("ffn2_w_gate", True, FF // N_DEV, D), ("ffn2_w_up", True, FF // N_DEV, D),
       ("ffn2_w_down", False, FF // N_DEV, D))
SMALL = ("b_ada", "g_norm1", "g_norm2", "g_sgu_ln", "b_sgu_ln", "w_spatial", "b_spatial", "g_q", "g_k",
         "attn_sinks", "g_norm3")


def _dot(a, b):
    return jnp.dot(a, b, preferred_element_type=F32)


def _dot_nt(a, b):
    return lax.dot_general(a, b, NT, preferred_element_type=F32)


def _dot_tn(a, b):
    return lax.dot_general(a, b, TN, preferred_element_type=F32)


def _vmem():
    return pl.BlockSpec(memory_space=pltpu.VMEM)


def _any():
    return pl.BlockSpec(memory_space=pl.ANY)


def _whole(a):
    return pl.BlockSpec(a.shape, lambda i: (0,) * len(a.shape))


def _rms_mod(h, g, sh, sc):
    inv = lax.rsqrt(jnp.mean(h * h, axis=-1, keepdims=True) + EPS)
    r = h * inv
    return (r * g) * (1.0 + sc) + sh, r, inv


def _rms_mod_bwd(dxn, r, inv, g, sc):
    dr = dxn * (g * (1.0 + sc))
    dh = inv * (dr - r * jnp.mean(dr * r, axis=-1, keepdims=True))
    return dh, jnp.sum(dxn, axis=0, keepdims=True), jnp.sum(dxn * r, axis=0, keepdims=True)


def _gelu(x):
    t = jnp.tanh(GELU_C * (x + 0.044715 * (x * x * x)))
    return 0.5 * x * (1.0 + t), t


def _gelu_grad(x, t):
    return 0.5 * (1.0 + t) + 0.5 * x * (1.0 - t * t) * (GELU_C * (1.0 + 3.0 * 0.044715 * x * x))


def _adamw(w, g, m, v):
    m = ADAM_B1 * m + (1.0 - ADAM_B1) * g
    v = ADAM_B2 * v + (1.0 - ADAM_B2) * (g * g)
    m_hat = m / (1.0 - ADAM_B1 ** ADAM_STEP)
    v_hat = v / (1.0 - ADAM_B2 ** ADAM_STEP)
    delta = -ADAM_LR * (m_hat / (jnp.sqrt(v_hat) + ADAM_EPS) + ADAM_WD * w)
    return delta, m, v


def _token_tile(seq, cap=512):
    return min(cap, seq)


def _params(*semantics):
    return pltpu.CompilerParams(dimension_semantics=semantics, vmem_limit_bytes=VMEM_LIMIT)


def _place():
    return lax.axis_index("x"), lax.axis_index("y"), lax.axis_index("c")


class _Gather:
    def __init__(self, arrays):
        n = len(arrays)
        self.ins = list(arrays)
        self.out_shape = [jax.ShapeDtypeStruct((N_DEV,) + a.shape, a.dtype) for a in arrays]
        self.scratch = [pltpu.SemaphoreType.DMA((n, 7)), pltpu.SemaphoreType.DMA((n, 7)),
                        pltpu.SemaphoreType.DMA((n,))]

    def _copies(self, ins, outs, sems):
        send_sems, recv_sems, local_sems = sems
        n = len(ins)
        x, y, c = _place()
        me, sibling = (x, y, c), (x, y, 1 - c)
        chips = [(1 - x, y), (x, 1 - y), (1 - x, 1 - y)]

        def slot(a, px, py, pc):
            return outs[a].at[4 * px + 2 * py + pc]

        def copy(a, k, block, to, src=None):
            return pltpu.make_async_remote_copy(
                src_ref=slot(a, *block) if src is None else src, dst_ref=slot(a, *block),
                send_sem=send_sems.at[a, k], recv_sem=recv_sems.at[a, k], device_id=to, device_id_type=MESH)

        mine = [pltpu.make_async_copy(ins[a], slot(a, *me), local_sems.at[a]) for a in range(n)]
        first = [copy(a, 0, me, sibling, src=ins[a]) for a in range(n)]
        first += [copy(a, 1 + j, me, (*chip, c), src=ins[a]) for a in range(n) for j, chip in enumerate(chips)]
        landed = [[copy(a, 1 + j, (*chip, c), me) for a in range(n)] for j, chip in enumerate(chips)]
        passed = [[copy(a, 4 + j, (*chip, c), sibling) for a in range(n)] for j, chip in enumerate(chips)]
        from_sibling = [copy(a, 0, sibling, me) for a in range(n)]
        from_sibling += [copy(a, 4 + j, (*chip, 1 - c), me) for a in range(n) for j, chip in enumerate(chips)]
        return mine, first, landed, passed, from_sibling

    def start(self, ins, outs, sems):
        mine, first, _, _, _ = self._copies(ins, outs, sems)
        for cp in mine + first:
            cp.start()

    def relay(self, ins, outs, sems):
        _, _, landed, passed, _ = self._copies(ins, outs, sems)
        for arrivals, forwards in zip(landed, passed):
            for arrival, forward in zip(arrivals, forwards):
                arrival.wait_recv()
                forward.start()

    def finish(self, ins, outs, sems):
        mine, first, _, passed, from_sibling = self._copies(ins, outs, sems)
        for cp in from_sibling:
            cp.wait_recv()
        for cp in first + [f for fs in passed for f in fs]:
            cp.wait_send()
        for cp in mine:
            cp.wait()


class _GatherRelayed:
    def __init__(self, arrays):
        n = len(arrays)
        self.ins = list(arrays)
        self.out_shape = [jax.ShapeDtypeStruct((N_DEV,) + a.shape, a.dtype) for a in arrays]
        self.scratch = [pltpu.SemaphoreType.DMA((n, 7)), pltpu.SemaphoreType.DMA((n, 7)),
                        pltpu.SemaphoreType.DMA((n,))]

    def _copies(self, ins, outs, sems):
        send_sems, recv_sems, local_sems = sems
        n = len(ins)
        x, y, c = _place()
        me, sibling = (x, y, c), (x, y, 1 - c)
        south = c == 0
        via = (jnp.where(south, 1 - x, x), jnp.where(south, y, 1 - y))
        onward = (jnp.where(south, x, 1 - x), jnp.where(south, 1 - y, y))
        via_sem = jnp.where(south, 1, 2)

        def slot(a, px, py, pc):
            return outs[a].at[4 * px + 2 * py + pc]

        def copy(a, k, block, to, src=None):
            return pltpu.make_async_remote_copy(
                src_ref=slot(a, *block) if src is None else src, dst_ref=slot(a, *block),
                send_sem=send_sems.at[a, k], recv_sem=recv_sems.at[a, k], device_id=to, device_id_type=MESH)

        chips = [(1 - x, y), (x, 1 - y), (1 - x, 1 - y)]
        mine = [pltpu.make_async_copy(ins[a], slot(a, *me), local_sems.at[a]) for a in range(n)]
        first = [copy(a, 0, me, sibling, src=ins[a]) for a in range(n)]
        first += [copy(a, 1 + j, me, (*chips[j], c), src=ins[a]) for a in range(n) for j in range(2)]
        to_pass = [copy(a, via_sem, (*via, c), me) for a in range(n)]
        passed_on = [copy(a, 3, (*via, c), (*onward, c)) for a in range(n)]
        landed = [[copy(a, 1 + j, (*chips[j], c), me) for a in range(n)] for j in range(3)]
        to_sibling = [[copy(a, 4 + j, (*chips[j], c), sibling) for a in range(n)] for j in range(3)]
        from_sibling = [copy(a, 0, sibling, me) for a in range(n)]
        from_sibling += [copy(a, 4 + j, (*chips[j], 1 - c), me) for a in range(n) for j in range(3)]
        return mine, first, to_pass, passed_on, landed, to_sibling, from_sibling

    def start(self, ins, outs, sems):
        mine, first, _, _, _, _, _ = self._copies(ins, outs, sems)
        for cp in mine + first:
            cp.start()

    def relay(self, ins, outs, sems):
        _, _, to_pass, passed_on, landed, to_sibling, _ = self._copies(ins, outs, sems)
        for arrival, onward in zip(to_pass, passed_on):
            arrival.wait_recv()
            onward.start()
        x, y, c = _place()
        for j in range(3):
            for a, (arrival, forward) in enumerate(zip(landed[j], to_sibling[j])):
                if j < 2:
                    pl.when((c == 0) != (j == 0))(arrival.wait_recv)
                else:
                    arrival.wait_recv()
                forward.start()

    def finish(self, ins, outs, sems):
        mine, first, _, passed_on, _, to_sibling, from_sibling = self._copies(ins, outs, sems)
        for cp in from_sibling:
            cp.wait_recv()
        for cp in first + passed_on + [f for fs in to_sibling for f in fs]:
            cp.wait_send()
        for cp in mine:
            cp.wait()


class _GatherDirect:
    def __init__(self, arrays):
        n = len(arrays)
        self.ins = list(arrays)
        self.out_shape = [jax.ShapeDtypeStruct((N_DEV,) + a.shape, a.dtype) for a in arrays]
        self.scratch = [pltpu.SemaphoreType.DMA((n, 7)), pltpu.SemaphoreType.DMA((n, 7)),
                        pltpu.SemaphoreType.DMA((n,))]

    def _copies(self, ins, outs, sems):
        send_sems, recv_sems, local_sems = sems
        n = len(ins)
        x, y, c = _place()
        peers = [(x ^ (k >> 2 & 1), y ^ (k >> 1 & 1), c ^ (k & 1)) for k in range(1, N_DEV)]

        def slot(a, px, py, pc):
            return outs[a].at[4 * px + 2 * py + pc]

        def copy(a, k, block, to, src=None):
            return pltpu.make_async_remote_copy(
                src_ref=slot(a, *block) if src is None else src, dst_ref=slot(a, *block),
                send_sem=send_sems.at[a, k], recv_sem=recv_sems.at[a, k], device_id=to, device_id_type=MESH)

        mine = [pltpu.make_async_copy(ins[a], slot(a, x, y, c), local_sems.at[a]) for a in range(n)]
        sends = [copy(a, k, (x, y, c), peer, src=ins[a]) for a in range(n) for k, peer in enumerate(peers)]
        arrivals = [copy(a, k, peer, (x, y, c)) for a in range(n) for k, peer in enumerate(peers)]
        return mine, sends, arrivals

    def start(self, ins, outs, sems):
        mine, sends, _ = self._copies(ins, outs, sems)
        for cp in mine + sends:
            cp.start()

    def relay(self, ins, outs, sems):
        pass

    def finish(self, ins, outs, sems):
        mine, sends, arrivals = self._copies(ins, outs, sems)
        for cp in arrivals:
            cp.wait_recv()
        for cp in sends:
            cp.wait_send()
        for cp in mine:
            cp.wait()


class _RsSibling:
    def __init__(self, ps):
        n = len(ps)
        self.ins = list(ps)
        self.out_shape = [jax.ShapeDtypeStruct((4,) + p.shape[1:], p.dtype) for p in ps]
        self.scratch = [pltpu.SemaphoreType.DMA((n, 4)), pltpu.SemaphoreType.DMA((n, 4))]

    def _copies(self, ins, outs, sems):
        send_sems, recv_sems = sems
        x, y, c = _place()
        return [pltpu.make_async_remote_copy(
            src_ref=ins[a].at[2 * q + (1 - c)], dst_ref=outs[a].at[q], send_sem=send_sems.at[a, q],
            recv_sem=recv_sems.at[a, q], device_id=(x, y, 1 - c), device_id_type=MESH)
            for a in range(len(ins)) for q in range(4)]

    def start(self, ins, outs, sems):
        for cp in self._copies(ins, outs, sems):
            cp.start()

    def relay(self, ins, outs, sems):
        pass

    def finish(self, ins, outs, sems):
        for cp in self._copies(ins, outs, sems):
            cp.wait()


class _RsChips:
    def __init__(self, qs):
        n = len(qs)
        self.ins = list(qs)
        self.out_shape = [jax.ShapeDtypeStruct(q.shape, q.dtype) for q in qs]
        self.scratch = [pltpu.SemaphoreType.DMA((n, 3)), pltpu.SemaphoreType.DMA((n, 3)),
                        pltpu.SemaphoreType.DMA((n,))]

    def _copies(self, ins, outs, sems):
        send_sems, recv_sems, local_sems = sems
        n = len(ins)
        x, y, c = _place()
        my_chip = 2 * x + y
        chips = [(1 - x, y), (x, 1 - y), (1 - x, 1 - y)]

        def copy(a, j, src_slot, dst_slot):
            px, py = chips[j]
            return pltpu.make_async_remote_copy(
                src_ref=ins[a].at[src_slot], dst_ref=outs[a].at[dst_slot], send_sem=send_sems.at[a, j],
                recv_sem=recv_sems.at[a, j], device_id=(px, py, c), device_id_type=MESH)

        own = [pltpu.make_async_copy(ins[a].at[my_chip], outs[a].at[my_chip], local_sems.at[a]) for a in range(n)]
        sends = [copy(a, j, 2 * px + py, my_chip) for a in range(n) for j, (px, py) in enumerate(chips)]
        arrivals = [copy(a, j, my_chip, 2 * px + py) for a in range(n) for j, (px, py) in enumerate(chips)]
        return own, sends, arrivals

    def start(self, ins, outs, sems):
        own, sends, _ = self._copies(ins, outs, sems)
        for cp in own + sends:
            cp.start()

    def relay(self, ins, outs, sems):
        pass

    def finish(self, ins, outs, sems):
        own, sends, arrivals = self._copies(ins, outs, sems)
        for cp in arrivals:
            cp.wait_recv()
        for cp in sends:
            cp.wait_send()
        for cp in own:
            cp.wait()


def _split_plans(plans, refs_in, refs_out, refs_scr, phase):
    i = o = s = 0
    for p in plans:
        ni, no, ns = len(p.ins), len(p.out_shape), len(p.scratch)
        getattr(p, phase)(refs_in[i:i + ni], refs_out[o:o + no], refs_scr[s:s + ns])
        i, o, s = i + ni, o + no, s + ns


def _plan_results(plans, res):
    out = []
    for p in plans:
        out.append(list(res[:len(p.out_shape)]))
        res = res[len(p.out_shape):]
    return out


def _exchange(plans, name):
    c_in = [a for p in plans for a in p.ins]
    c_out = [s for p in plans for s in p.out_shape]
    c_scr = [s for p in plans for s in p.scratch]

    def body(*refs):
        cin, cout, cscr = refs[:len(c_in)], refs[len(c_in):len(c_in) + len(c_out)], refs[len(c_in) + len(c_out):]
        for phase in ("start", "relay", "finish"):
            _split_plans(plans, cin, cout, cscr, phase)

    res = pl.pallas_call(body, name=name, in_specs=[_any()] * len(c_in), out_specs=[_any()] * len(c_out),
                         out_shape=c_out, scratch_shapes=c_scr)(*c_in)
    return _plan_results(plans, res)


def _call(body, *, name, grid, in_specs, out_specs, out_shape, args, semantics, scratch_shapes=(), plans=()):
    n_in, n_out, n_scr = len(in_specs), len(out_specs), len(scratch_shapes)
    c_in = [a for p in plans for a in p.ins]
    c_out = [s for p in plans for s in p.out_shape]
    c_scr = [s for p in plans for s in p.scratch]
    n_steps = math.prod(grid)

    def wrapped(*refs):
        ins, refs = refs[:n_in], refs[n_in:]
        cin, refs = refs[:len(c_in)], refs[len(c_in):]
        outs, refs = refs[:n_out], refs[n_out:]
        cout, refs = refs[:len(c_out)], refs[len(c_out):]
        scr, cscr = refs[:n_scr], refs[n_scr:]
        if plans:
            step = 0
            for d, g in enumerate(grid):
                step = step * g + pl.program_id(d)
            pl.when(step == 0)(lambda: _split_plans(plans, cin, cout, cscr, "start"))
        body(*ins, *outs, *scr)
        if plans:
            pl.when(step == max(n_steps - 2, 0))(lambda: _split_plans(plans, cin, cout, cscr, "relay"))
            pl.when(step == n_steps - 1)(lambda: _split_plans(plans, cin, cout, cscr, "finish"))

    res = pl.pallas_call(
        wrapped, name=name, grid=grid, in_specs=list(in_specs) + [_any()] * len(c_in),
        out_specs=list(out_specs) + [_any()] * len(c_out), out_shape=list(out_shape) + c_out,
        scratch_shapes=list(scratch_shapes) + c_scr,
        compiler_params=_params(*(("arbitrary",) * len(grid) if plans else semantics)),
    )(*args, *c_in)
    return list(res[:n_out]), _plan_results(plans, res[n_out:])


def _ffn_fwd(h, mods, gn, wg, wu, wd, row0, seq, tgt=None, plans=()):
    T = h.shape[0]
    tm = _token_tile(seq, 256)
    tps = seq // tm
    n_t = T // tm
    with_loss = tgt is not None

    def body(h_ref, m_ref, g_ref, wg_ref, wu_ref, wd_ref, *rest):
        if with_loss:
            tgt_ref, out_ref, y_ref, gb_ref, ub_ref, loss_ref = rest
        else:
            out_ref, y_ref, gb_ref, ub_ref = rest
        hv = h_ref[...]
        sh = m_ref[0, row0:row0 + 1, :]
        sc = m_ref[0, row0 + 1:row0 + 2, :]
        ga = m_ref[0, row0 + 2:row0 + 3, :]
        xn, _, _ = _rms_mod(hv, g_ref[...], sh, sc)
        xb = xn.astype(BF16)
        acc = jnp.zeros((tm, D), F32)
        for c0 in range(0, FF, FC):
            gg = _dot_nt(xb, wg_ref[c0:c0 + FC, :])
            uu = _dot_nt(xb, wu_ref[c0:c0 + FC, :])
            gg, uu = gg.astype(BF16), uu.astype(BF16)
            gb_ref[:, c0:c0 + FC] = gg
            ub_ref[:, c0:c0 + FC] = uu
            acc = acc + _dot((gg * jax.nn.sigmoid(gg)) * uu, wd_ref[c0:c0 + FC, :])
        y_ref[...] = acc
        hout = hv + (0.5 * ga) * acc
        if with_loss:
            d = hout - tgt_ref[...]
            out_ref[...] = d * (1.0 / D)
            loss_ref[...] = jnp.full((1, 8, 128), 0.5 / D, F32) * jnp.sum(d * d)
        else:
            out_ref[...] = hout

    tok = pl.BlockSpec((tm, D), lambda i: (i, 0))
    tokf = pl.BlockSpec((tm, FF), lambda i: (i, 0))
    in_specs = [tok, pl.BlockSpec((1, 9, D), lambda i: (i // tps, 0, 0)), pl.BlockSpec((1, D), lambda i: (0, 0)),
                _vmem(), _vmem(), _vmem()]
    out_shape = [jax.ShapeDtypeStruct((T, D), F32), jax.ShapeDtypeStruct((T, D), F32),
                 jax.ShapeDtypeStruct((T, FF), BF16), jax.ShapeDtypeStruct((T, FF), BF16)]
    out_specs = [tok, tok, tokf, tokf]
    args = [h, mods, gn, wg, wu, wd]
    if with_loss:
        in_specs.append(tok)
        args.append(tgt)
        out_shape.append(jax.ShapeDtypeStruct((n_t, 8, 128), F32))
        out_specs.append(pl.BlockSpec((1, 8, 128), lambda i: (i, 0, 0)))
    return _call(body, name="ffn_fwd_loss" if with_loss else "ffn_fwd", grid=(n_t,), in_specs=in_specs,
                 out_specs=out_specs, out_shape=out_shape, args=args, semantics=("parallel",), plans=plans)


def _ffn_gate_up(h, mods, gn, wg, wu, row0, seq, plans=()):
    T = h.shape[0]
    tm = _token_tile(seq, 512)
    tps = seq // tm

    def body(h_ref, m_ref, g_ref, wg_ref, wu_ref, gb_ref, ub_ref):
        xn, _, _ = _rms_mod(h_ref[...], g_ref[...], m_ref[0, row0:row0 + 1, :], m_ref[0, row0 + 1:row0 + 2, :])
        xb = xn.astype(BF16)
        for c0 in range(0, FF, FC):
            gb_ref[:, c0:c0 + FC] = _dot_nt(xb, wg_ref[c0:c0 + FC, :]).astype(BF16)
            ub_ref[:, c0:c0 + FC] = _dot_nt(xb, wu_ref[c0:c0 + FC, :]).astype(BF16)

    tokf = pl.BlockSpec((tm, FF), lambda i: (i, 0))
    return _call(
        body, name="ffn_gate_up", grid=(T // tm,),
        in_specs=[pl.BlockSpec((tm, D), lambda i: (i, 0)), pl.BlockSpec((1, 9, D), lambda i: (i // tps, 0, 0)),
                  pl.BlockSpec((1, D), lambda i: (0, 0)), _vmem(), _vmem()],
        out_specs=[tokf, tokf], out_shape=[jax.ShapeDtypeStruct((T, FF), BF16)] * 2, args=(h, mods, gn, wg, wu),
        semantics=("parallel",), plans=plans)


def _ffn_down(h, gb, ub, mods, wd, row0, seq, plans=()):
    T = h.shape[0]
    tm = _token_tile(seq, 512)
    tps = seq // tm

    def body(h_ref, gb_ref, ub_ref, m_ref, wd_ref, out_ref, y_ref):
        acc = jnp.zeros((tm, D), F32)
        for c0 in range(0, FF, FC):
            gg = gb_ref[:, c0:c0 + FC]
            acc = acc + _dot((gg * jax.nn.sigmoid(gg)) * ub_ref[:, c0:c0 + FC], wd_ref[c0:c0 + FC, :])
        y_ref[...] = acc
        out_ref[...] = h_ref[...] + (0.5 * m_ref[0, row0 + 2:row0 + 3, :]) * acc

    tok = pl.BlockSpec((tm, D), lambda i: (i, 0))
    tokf = pl.BlockSpec((tm, FF), lambda i: (i, 0))
    return _call(
        body, name="ffn_down", grid=(T // tm,),
        in_specs=[tok, tokf, tokf, pl.BlockSpec((1, 9, D), lambda i: (i // tps, 0, 0)), _vmem()],
        out_specs=[tok, tok], out_shape=[jax.ShapeDtypeStruct((T, D), F32)] * 2, args=(h, gb, ub, mods, wd),
        semantics=("parallel",), plans=plans)


def _ffn_bwd(h, dhn, y, gb, ub, mods, gn, wg, wu, wd, row0, seq, plans=()):
    T = h.shape[0]
    B = T // seq
    tm = _token_tile(seq, 256)
    tps = seq // tm
    n_t = T // tm

    def body(h_ref, dhn_ref, y_ref, gb_ref, ub_ref, m_ref, g_ref, wg_ref, wu_ref, wd_ref,
             dh_ref, xb_ref, dyb_ref, a_ref, dg_ref, du_ref, mg_ref):
        i = pl.program_id(0)
        hv = h_ref[...]
        dhn = dhn_ref[...]
        sh = m_ref[0, row0:row0 + 1, :]
        sc = m_ref[0, row0 + 1:row0 + 2, :]
        ga = m_ref[0, row0 + 2:row0 + 3, :]
        g = g_ref[...]
        xn, r, inv = _rms_mod(hv, g, sh, sc)
        xb_ref[...] = xn.astype(BF16)
        dyb = ((0.5 * ga) * dhn).astype(BF16)
        dyb_ref[...] = dyb
        dga = 0.5 * jnp.sum(dhn * y_ref[...], axis=0, keepdims=True)
        dxn = jnp.zeros((tm, D), F32)
        for c0 in range(0, FF, FC):
            wgc = wg_ref[c0:c0 + FC, :]
            wuc = wu_ref[c0:c0 + FC, :]
            gg = gb_ref[:, c0:c0 + FC]
            uu = ub_ref[:, c0:c0 + FC]
            sig = jax.nn.sigmoid(gg)
            s = gg * sig
            a_ref[:, c0:c0 + FC] = s * uu
            da = _dot_nt(dyb, wd_ref[c0:c0 + FC, :]).astype(BF16)
            dub = da * s
            dgb = (da * uu) * (sig * (1.0 + gg * (1.0 - sig)))
            dg_ref[:, c0:c0 + FC] = dgb
            du_ref[:, c0:c0 + FC] = dub
            dxn = dxn + _dot(dgb, wgc) + _dot(dub, wuc)
        dh, s_dxn, s_dxr = _rms_mod_bwd(dxn, r, inv, g, sc)
        dh_ref[...] = dhn + dh

        @pl.when(i % tps == 0)
        def _():
            mg_ref[...] = jnp.zeros(mg_ref.shape, F32)

        mg_ref[0, 0:1, :] += s_dxn
        mg_ref[0, 1:2, :] += s_dxr
        mg_ref[0, 2:3, :] += dga

    tok = pl.BlockSpec((tm, D), lambda i: (i, 0))
    tokf = pl.BlockSpec((tm, FF), lambda i: (i, 0))
    return _call(
        body, name="ffn_bwd", grid=(n_t,),
        in_specs=[tok, tok, tok, tokf, tokf, pl.BlockSpec((1, 9, D), lambda i: (i // tps, 0, 0)),
                  pl.BlockSpec((1, D), lambda i: (0, 0)), _vmem(), _vmem(), _vmem()],
        out_specs=[tok, tok, tok, tokf, tokf, tokf, pl.BlockSpec((1, 8, D), lambda i: (i // tps, 0, 0))],
        out_shape=[jax.ShapeDtypeStruct((T, D), F32), jax.ShapeDtypeStruct((T, D), BF16),
                   jax.ShapeDtypeStruct((T, D), BF16), jax.ShapeDtypeStruct((T, FF), BF16),
                   jax.ShapeDtypeStruct((T, FF), BF16), jax.ShapeDtypeStruct((T, FF), BF16),
                   jax.ShapeDtypeStruct((B, 8, D), F32)],
        args=(h, dhn, y, gb, ub, mods, gn, wg, wu, wd), semantics=("arbitrary",), plans=plans)


def _wgrad(a, b, name, plans=()):
    T, da = a.shape
    db = b.shape[1]
    bm = {2816: 1408, 3840: 1280}[da]
    bn = db
    tk = min(1024, T)
    nk = T // tk

    def body(a_ref, b_ref, o_ref):
        @pl.when(pl.program_id(2) == 0)
        def _():
            o_ref[...] = jnp.zeros(o_ref.shape, F32)

        o_ref[...] += _dot_tn(a_ref[...], b_ref[...])

    (out,), plan_outs = _call(
        body, name=name, grid=(da // bm, db // bn, nk),
        in_specs=[pl.BlockSpec((tk, bm), lambda i, j, k: (k, i)), pl.BlockSpec((tk, bn), lambda i, j, k: (k, j))],
        out_specs=[pl.BlockSpec((bm, bn), lambda i, j, k: (i, j))], out_shape=[jax.ShapeDtypeStruct((da, db), F32)],
        args=(a, b), semantics=("parallel", "parallel", "arbitrary"), plans=plans)
    return out, plan_outs


def _inproj_fwd(h, mods, gn, w_in, seq, plans=()):
    T = h.shape[0]
    tm = _token_tile(seq)
    tps = seq // tm

    def body(h_ref, m_ref, g_ref, w_ref, *outs):
        xn, _, _ = _rms_mod(h_ref[...], g_ref[...], m_ref[0, 3:4, :], m_ref[0, 4:5, :])
        xb = xn.astype(BF16)
        for (_, c0, w), o_ref in zip(PIECES, outs):
            o_ref[...] = _dot_nt(xb, w_ref[c0:c0 + w, :])

    return _call(
        body, name="inproj_fwd", grid=(T // tm,),
        in_specs=[pl.BlockSpec((tm, D), lambda i: (i, 0)), pl.BlockSpec((1, 9, D), lambda i: (i // tps, 0, 0)),
                  pl.BlockSpec((1, D), lambda i: (0, 0)), _vmem()],
        out_specs=[pl.BlockSpec((tm, w), lambda i: (i, 0)) for _, _, w in PIECES],
        out_shape=[jax.ShapeDtypeStruct((T, w), F32) for _, _, w in PIECES], args=(h, mods, gn, w_in),
        semantics=("parallel",), plans=plans)


def _inproj_bwd(h, dh_res, dpieces, mods, gn, w_in, seq, plans=()):
    T = h.shape[0]
    B = T // seq
    tm = _token_tile(seq, 256)
    tps = seq // tm

    def body(h_ref, dres_ref, *rest):
        dp_refs = rest[:len(PIECES)]
        m_ref, g_ref, w_ref, dh_ref, xb_ref, dpb_ref, mg_ref = rest[len(PIECES):]
        i = pl.program_id(0)
        g = g_ref[...]
        sc = m_ref[0, 4:5, :]
        xn, r, inv = _rms_mod(h_ref[...], g, m_ref[0, 3:4, :], sc)
        xb_ref[...] = xn.astype(BF16)
        dxn = jnp.zeros((tm, D), F32)
        for (_, c0, w), dp_ref in zip(PIECES, dp_refs):
            dpb = dp_ref[...].astype(BF16)
            dpb_ref[:, c0:c0 + w] = dpb
            dxn = dxn + _dot(dpb, w_ref[c0:c0 + w, :])
        dh, s_dxn, s_dxr = _rms_mod_bwd(dxn, r, inv, g, sc)
        dh_ref[...] = dres_ref[...] + dh

        @pl.when(i % tps == 0)
        def _():
            mg_ref[...] = jnp.zeros(mg_ref.shape, F32)

        mg_ref[0, 0:1, :] += s_dxn
        mg_ref[0, 1:2, :] += s_dxr

    tok = pl.BlockSpec((tm, D), lambda i: (i, 0))
    return _call(
        body, name="inproj_bwd", grid=(T // tm,),
        in_specs=[tok, tok] + [pl.BlockSpec((tm, w), lambda i: (i, 0)) for _, _, w in PIECES]
        + [pl.BlockSpec((1, 9, D), lambda i: (i // tps, 0, 0)), pl.BlockSpec((1, D), lambda i: (0, 0)), _vmem()],
        out_specs=[tok, tok, pl.BlockSpec((tm, IN_COLS), lambda i: (i, 0)),
                   pl.BlockSpec((1, 8, D), lambda i: (i // tps, 0, 0))],
        out_shape=[jax.ShapeDtypeStruct((T, D), F32), jax.ShapeDtypeStruct((T, D), BF16),
                   jax.ShapeDtypeStruct((T, IN_COLS), BF16), jax.ShapeDtypeStruct((B, 8, D), F32)],
        args=(h, dh_res, *dpieces, mods, gn, w_in), semantics=("arbitrary",), plans=plans)


def _seg_mean(x):
    i = lax.broadcasted_iota(jnp.int32, (128, 128), 0) >> 6
    j = lax.broadcasted_iota(jnp.int32, (128, 128), 1) >> 6
    ones = jnp.where(i == j, 1.0 / HD, 0.0).astype(BF16)
    hi = x.astype(BF16)
    lo = (x - hi.astype(F32)).astype(BF16)
    return _dot(hi, ones) + _dot(lo, ones)


def _head_norm(x, g2):
    inv = lax.rsqrt(_seg_mean(x * x) + EPS)
    y = x * inv
    return y * g2, y, inv


def _head_norm_bwd(dxn, y, inv, g2):
    dy = dxn * g2
    return inv * (dy - y * _seg_mean(dy * y)), jnp.sum(dxn * y, axis=0, keepdims=True)


def _swa_block(q, kk, vv, gq2, gk2, sinks, first, do=None):
    lo = lax.broadcasted_iota(jnp.int32, (1, 128), 1) < HD
    kn, ky, kinv = _head_norm(kk, gk2)

    def operands(x):
        xr = pltpu.roll(x, HD, 1)
        own_lo, own_hi = jnp.where(lo, x, 0.0).astype(BF16), jnp.where(lo, 0.0, x).astype(BF16)
        rolled_lo, rolled_hi = jnp.where(lo, xr, 0.0).astype(BF16), jnp.where(lo, 0.0, xr).astype(BF16)
        return (own_lo, rolled_hi), (rolled_lo, own_hi)

    def restore(parts):
        (own_lo, rolled_hi), (rolled_lo, own_hi) = parts
        return (jnp.where(lo, own_lo, own_hi)
                + pltpu.roll(jnp.where(lo, rolled_lo, rolled_hi), HD, 1))

    k_ops, v_ops = operands(kn), operands(vv)
    k2 = [jnp.concatenate(pair, axis=0) for pair in k_ops]
    def stack(x):
        return jnp.concatenate([x[:, 128 * p:128 * (p + 1)] for p in range(4)], axis=0)

    def unstack(x):
        return jnp.concatenate([x[BLK * p:BLK * (p + 1)] for p in range(4)], axis=1)

    def of_head(x, kh):
        return x[2 * BLK * kh:2 * BLK * (kh + 1)]

    nq = 4 * BLK
    pair = lax.broadcasted_iota(jnp.int32, (nq, 1), 0) >> 7
    row = lax.broadcasted_iota(jnp.int32, (nq, 2 * BLK), 0) & (BLK - 1)
    col = lax.broadcasted_iota(jnp.int32, (nq, 2 * BLK), 1)
    valid = (col <= row + BLK) & (col > row) & (col >= jnp.where(first, BLK, 0))
    scale = HD ** -0.5
    qn, qy, qinv = _head_norm(stack(q), gq2)
    qnb = qn.astype(BF16)
    s2 = jnp.concatenate([_dot_nt(of_head(qnb, kh), k2[kh]) for kh in range(N_KV)], axis=0) * scale
    probs, p_sink = [], []
    for j in range(2):
        s = jnp.where(valid, s2[:, 2 * BLK * j:2 * BLK * (j + 1)], NEG)
        sink = jnp.zeros((nq, 1), F32)
        for p in range(4):
            sink = jnp.where(pair == p, sinks[:, 2 * p + j:2 * p + j + 1], sink)
        m = jnp.maximum(jnp.max(s, axis=-1, keepdims=True), sink)
        e = jnp.exp(s - m)
        e_sink = jnp.exp(sink - m)
        rden = 1.0 / (jnp.sum(e, axis=-1, keepdims=True) + e_sink)
        probs.append(e * rden)
        p_sink.append(e_sink * rden)
    pb = [p.astype(BF16) for p in probs]
    if do is None:
        return unstack(jnp.concatenate(
            [_dot(of_head(pb[0], kh), v_ops[kh][0]) + _dot(of_head(pb[1], kh), v_ops[kh][1]) for kh in range(N_KV)],
            axis=0))
    dob = stack(do).astype(BF16)
    ds, dsinks = [], [None] * 8
    for j in range(2):
        dp = jnp.concatenate([_dot_nt(of_head(dob, kh), v_ops[kh][j]) for kh in range(N_KV)], axis=0)
        t = jnp.sum(dp * probs[j], axis=-1, keepdims=True)
        ds.append(probs[j] * (dp - t) * scale)
        lost = p_sink[j] * t
        for p in range(4):
            dsinks[2 * p + j] = -jnp.sum(lost[BLK * p:BLK * (p + 1)])
    dsb = jnp.concatenate(ds, axis=1).astype(BF16)
    dqn = jnp.concatenate([_dot(of_head(dsb, kh), k2[kh]) for kh in range(N_KV)], axis=0)
    dq, dgq2 = _head_norm_bwd(dqn, qy, qinv, gq2)
    dk2 = [_dot_tn(of_head(dsb, kh), of_head(qnb, kh)) for kh in range(N_KV)]
    dv_ops = [[_dot_tn(of_head(pb[j], kh), of_head(dob, kh)) for j in range(2)] for kh in range(N_KV)]
    dkn = restore(tuple((d[:2 * BLK], d[2 * BLK:]) for d in dk2))
    dvv = restore(tuple(tuple(d) for d in dv_ops))
    dkk, dgk2 = _head_norm_bwd(dkn, ky, kinv, gk2)
    return unstack(dq), dkk, dvv, dgq2, dgk2, dsinks


SWA_GROUP = 8


def _swa_specs(nbs):
    grp = min(SWA_GROUP, nbs)
    rows = grp * BLK

    def tok(w):
        return pl.BlockSpec((rows, w), lambda i: (i, 0))

    halo = pl.BlockSpec((BLK, 128), lambda i: (jnp.maximum(i * grp - 1, 0), 0))
    vec = pl.BlockSpec((1, 128), lambda i: (0, 0))
    sk = pl.BlockSpec((1, 8), lambda i: (0, 0))
    return grp, tok, halo, vec, sk


def _swa_fwd(q, k, v, gq2, gk2, sinks, nbs, plans=()):
    T = q.shape[0]
    grp, tok, halo, vec, sk = _swa_specs(nbs)

    def body(q_ref, kh_ref, kc_ref, vh_ref, vc_ref, gq_ref, gk_ref, sk_ref, o_ref):
        seq_start = ((pl.program_id(0) * grp) % nbs) == 0
        for g in range(grp):
            rows = slice(g * BLK, (g + 1) * BLK)
            prev = slice((g - 1) * BLK, g * BLK)
            kk = jnp.concatenate([kh_ref[...] if g == 0 else kc_ref[prev, :], kc_ref[rows, :]], axis=0)
            vv = jnp.concatenate([vh_ref[...] if g == 0 else vc_ref[prev, :], vc_ref[rows, :]], axis=0)
            o_ref[rows, :] = _swa_block(q_ref[rows, :], kk, vv, gq_ref[...], gk_ref[...], sk_ref[...],
                                        seq_start if g == 0 else False)

    (out,), plan_outs = _call(
        body, name="swa_fwd", grid=(T // (grp * BLK),),
        in_specs=[tok(D_B), halo, tok(128), halo, tok(128), vec, vec, sk], out_specs=[tok(D_B)],
        out_shape=[jax.ShapeDtypeStruct((T, D_B), F32)], args=(q, k, k, v, v, gq2, gk2, sinks),
        semantics=("parallel",), plans=plans)
    return out, plan_outs


def _swa_bwd(q, k, v, gq2, gk2, sinks, do, nbs, plans=()):
    T = q.shape[0]
    grp, tok, halo, vec, sk = _swa_specs(nbs)
    steps = T // (grp * BLK)

    def body(q_ref, kh_ref, kc_ref, vh_ref, vc_ref, gq_ref, gk_ref, sk_ref, do_ref,
             dq_ref, dk_ref, dv_ref, dkh_ref, dvh_ref, dgq_ref, dgk_ref, dsk_ref):
        i = pl.program_id(0)
        seq_start = ((i * grp) % nbs) == 0

        @pl.when(i == 0)
        def _():
            for r in (dgq_ref, dgk_ref, dsk_ref):
                r[...] = jnp.zeros(r.shape, F32)

        res = []
        for g in range(grp):
            rows = slice(g * BLK, (g + 1) * BLK)
            prev = slice((g - 1) * BLK, g * BLK)
            kk = jnp.concatenate([kh_ref[...] if g == 0 else kc_ref[prev, :], kc_ref[rows, :]], axis=0)
            vv = jnp.concatenate([vh_ref[...] if g == 0 else vc_ref[prev, :], vc_ref[rows, :]], axis=0)
            res.append(_swa_block(q_ref[rows, :], kk, vv, gq_ref[...], gk_ref[...], sk_ref[...],
                                  seq_start if g == 0 else False, do=do_ref[rows, :]))
        lane = lax.broadcasted_iota(jnp.int32, (8, 128), 1)
        upd = jnp.zeros((8, 128), F32)
        for g, (dq, dkk, dvv, dgq2, dgk2, dsinks) in enumerate(res):
            rows = slice(g * BLK, (g + 1) * BLK)
            dq_ref[rows, :] = dq
            dk_ref[rows, :] = dkk[BLK:] + res[g + 1][1][:BLK] if g + 1 < grp else dkk[BLK:]
            dv_ref[rows, :] = dvv[BLK:] + res[g + 1][2][:BLK] if g + 1 < grp else dvv[BLK:]
            dgq_ref[...] += dgq2
            dgk_ref[...] += dgk2
            for h, d in enumerate(dsinks):
                upd = upd + jnp.where(lane == h, d, 0.0)
        dkh_ref[0] = res[0][1][:BLK]
        dvh_ref[0] = res[0][2][:BLK]
        dsk_ref[...] += upd

    one = pl.BlockSpec((1, BLK, 128), lambda i: (i, 0, 0))
    halo_shape = jax.ShapeDtypeStruct((steps, BLK, 128), F32)
    return _call(
        body, name="swa_bwd", grid=(steps,),
        in_specs=[tok(D_B), halo, tok(128), halo, tok(128), vec, vec, sk, tok(D_B)],
        out_specs=[tok(D_B), tok(128), tok(128), one, one, vec, vec, pl.BlockSpec((8, 128), lambda i: (0, 0))],
        out_shape=[jax.ShapeDtypeStruct((T, D_B), F32), jax.ShapeDtypeStruct((T, 128), F32),
                   jax.ShapeDtypeStruct((T, 128), F32), halo_shape, halo_shape, jax.ShapeDtypeStruct((1, 128), F32),
                   jax.ShapeDtypeStruct((1, 128), F32), jax.ShapeDtypeStruct((8, 128), F32)],
        args=(q, k, k, v, v, gq2, gk2, sinks, do), semantics=("arbitrary",), plans=plans)


def _swa_add_halo(dk, dk_halo):
    steps = dk_halo.shape[0]
    nxt = jnp.concatenate([dk_halo[1:], jnp.zeros_like(dk_halo[:1])], axis=0)[:, None]
    dk = dk.reshape(steps, -1, BLK, 128)
    return jnp.concatenate([dk[:, :-1], dk[:, -1:] + nxt], axis=1).reshape(-1, 128)


def _sgu_norm(av, g_ln, b_ln):
    t, th = _gelu(av)
    mu = jnp.mean(t, axis=-1, keepdims=True)
    tc = t - mu
    rstd = lax.rsqrt(jnp.mean(tc * tc, axis=-1, keepdims=True) + EPS)
    vhat = tc * rstd
    return vhat * g_ln + b_ln, vhat, rstd, th


def _masked_ws(ws_ref):
    tril = lax.broadcasted_iota(jnp.int32, (BLK, BLK), 0) >= lax.broadcasted_iota(jnp.int32, (BLK, BLK), 1)
    return [jnp.where(tril, ws_ref[g], 0.0).astype(BF16) for g in range(N_GRP)]


def _mix_fwd(au, av, gta, gtb, ob, h, mods, g_ln, b_ln, ws, bsb, wa, wb, wout, seq, plans=()):
    T = h.shape[0]
    tm = _token_tile(seq)
    tps = seq // tm

    def body(au_ref, av_ref, gta_ref, gtb_ref, ob_ref, h_ref, m_ref, gl_ref, bl_ref, ws_ref, bs_ref,
             wa_ref, wb_ref, wo_ref, out_ref, vvb_s, z_s):
        u, _ = _gelu(au_ref[...])
        vv, _, _, _ = _sgu_norm(av_ref[...], gl_ref[...], bl_ref[...])
        vvb_s[...] = vv.astype(BF16)
        wsm = _masked_ws(ws_ref)
        for c in range(tm // BLK):
            rows = slice(c * BLK, (c + 1) * BLK)
            for g in range(N_GRP):
                cols = slice(g * BLK, (g + 1) * BLK)
                z_s[rows, cols] = _dot(wsm[g], vvb_s[rows, cols]) + bs_ref[g]
        ya = _dot_nt((u * z_s[...]).astype(BF16), wa_ref[...])
        yb = _dot_nt(ob_ref[...].astype(BF16), wb_ref[...])
        merged = jax.nn.sigmoid(gta_ref[...]) * ya + jax.nn.sigmoid(gtb_ref[...]) * yb
        out_ref[...] = h_ref[...] + m_ref[0, 5:6, :] * _dot(merged.astype(BF16), wo_ref[...])

    def tok(w):
        return pl.BlockSpec((tm, w), lambda i: (i, 0))

    def full(shape):
        return pl.BlockSpec(shape, lambda i: (0,) * len(shape))

    (out,), plan_outs = _call(
        body, name="mix_fwd", grid=(T // tm,),
        in_specs=[tok(D_A), tok(D_A), tok(D), tok(D), tok(D_B), tok(D),
                  pl.BlockSpec((1, 9, D), lambda i: (i // tps, 0, 0)), full((1, D_A)), full((1, D_A)),
                  full((N_GRP, BLK, BLK)), full((N_GRP, BLK, BLK)), _vmem(), _vmem(), _vmem()],
        out_specs=[tok(D)], out_shape=[jax.ShapeDtypeStruct((T, D), F32)],
        scratch_shapes=[pltpu.VMEM((tm, D_A), BF16), pltpu.VMEM((tm, D_A), F32)],
        args=(au, av, gta, gtb, ob, h, mods, g_ln, b_ln, ws, bsb, wa, wb, wout), semantics=("parallel",),
        plans=plans)
    return out, plan_outs


def _mix_bwd(au, av, gta, gtb, ob, dh, mods, g_ln, b_ln, ws, bsb, wa, wb, wout, seq, plans=()):
    T = dh.shape[0]
    B = T // seq
    tm = _token_tile(seq, 256)
    tps = seq // tm

    def body(au_ref, av_ref, gta_ref, gtb_ref, ob_ref, dh_ref, m_ref, gl_ref, bl_ref, ws_ref, bs_ref,
             wa_ref, wb_ref, wo_ref,
             dau_ref, dav_ref, dgta_ref, dgtb_ref, dob_ref, dwo_ref, dwa_ref, dwb_ref, dws_ref, dbs_ref, dln_ref,
             mg_ref, vvb_s, z_s, dz_s, dzb_s, dvv_s):
        i = pl.program_id(0)

        @pl.when(i == 0)
        def _():
            for r in (dwo_ref, dwa_ref, dwb_ref, dws_ref, dbs_ref, dln_ref):
                r[...] = jnp.zeros(r.shape, F32)

        @pl.when(i % tps == 0)
        def _():
            mg_ref[...] = jnp.zeros(mg_ref.shape, F32)

        auv = au_ref[...]
        avv = av_ref[...]
        u, thu = _gelu(auv)
        g_ln = gl_ref[...]
        vv, vhat, rstd, thv = _sgu_norm(avv, g_ln, bl_ref[...])
        vvb_s[...] = vv.astype(BF16)
        wsm = _masked_ws(ws_ref)
        for c in range(tm // BLK):
            rows = slice(c * BLK, (c + 1) * BLK)
            for g in range(N_GRP):
                cols = slice(g * BLK, (g + 1) * BLK)
                z_s[rows, cols] = _dot(wsm[g], vvb_s[rows, cols]) + bs_ref[g]
        z = z_s[...]
        yab = (u * z).astype(BF16)
        obb = ob_ref[...].astype(BF16)
        ya = _dot_nt(yab, wa_ref[...])
        yb = _dot_nt(obb, wb_ref[...])
        sa = jax.nn.sigmoid(gta_ref[...])
        sb = jax.nn.sigmoid(gtb_ref[...])
        mb = (sa * ya + sb * yb).astype(BF16)
        dhv = dh_ref[...]
        mg_ref[0, 0:1, :] += jnp.sum(dhv * _dot(mb, wo_ref[...]), axis=0, keepdims=True)
        dmob = (m_ref[0, 5:6, :] * dhv).astype(BF16)
        dwo_ref[...] += _dot_tn(mb, dmob)
        dmerged = _dot_nt(dmob, wo_ref[...])
        dya = dmerged * sa
        dyb = dmerged * sb
        dgta_ref[...] = dya * ya * (1.0 - sa)
        dgtb_ref[...] = dyb * yb * (1.0 - sb)
        dyab = dya.astype(BF16)
        dybb = dyb.astype(BF16)
        dwa_ref[...] += _dot_tn(dyab, yab)
        dwb_ref[...] += _dot_tn(dybb, obb)
        dob_ref[...] = _dot(dybb, wb_ref[...])
        dyap = _dot(dyab, wa_ref[...])
        dau_ref[...] = (dyap * z) * _gelu_grad(auv, thu)
        dz = dyap * u
        dz_s[...] = dz
        dzb_s[...] = dz.astype(BF16)
        for c in range(tm // BLK):
            rows = slice(c * BLK, (c + 1) * BLK)
            for g in range(N_GRP):
                cols = slice(g * BLK, (g + 1) * BLK)
                dzb = dzb_s[rows, cols]
                dvv_s[rows, cols] = _dot_tn(wsm[g], dzb)
                dws_ref[g] += _dot_nt(dzb, vvb_s[rows, cols])
                dbs_ref[g] += dz_s[rows, cols]
        dvv = dvv_s[...]
        dln_ref[0:1, :] += jnp.sum(dvv * vhat, axis=0, keepdims=True)
        dln_ref[1:2, :] += jnp.sum(dvv, axis=0, keepdims=True)
        dvh = dvv * g_ln
        dt = rstd * (dvh - jnp.mean(dvh, axis=-1, keepdims=True)
                     - vhat * jnp.mean(dvh * vhat, axis=-1, keepdims=True))
        dav_ref[...] = dt * _gelu_grad(avv, thv)

    def tok(w):
        return pl.BlockSpec((tm, w), lambda i: (i, 0))

    def full(shape):
        return pl.BlockSpec(shape, lambda i: (0,) * len(shape))

    return _call(
        body, name="mix_bwd", grid=(T // tm,),
        in_specs=[tok(D_A), tok(D_A), tok(D), tok(D), tok(D_B), tok(D),
                  pl.BlockSpec((1, 9, D), lambda i: (i // tps, 0, 0)), full((1, D_A)), full((1, D_A)),
                  full((N_GRP, BLK, BLK)), full((N_GRP, BLK, BLK)), _vmem(), _vmem(), _vmem()],
        out_specs=[tok(D_A), tok(D_A), tok(D), tok(D), tok(D_B), full((D, D)), full((D, D_A)), full((D, D_B)),
                   full((N_GRP, BLK, BLK)), full((N_GRP, BLK, BLK)), full((8, D_A)),
                   pl.BlockSpec((1, 8, D), lambda i: (i // tps, 0, 0))],
        out_shape=[jax.ShapeDtypeStruct((T, D_A), F32), jax.ShapeDtypeStruct((T, D_A), F32),
                   jax.ShapeDtypeStruct((T, D), F32), jax.ShapeDtypeStruct((T, D), F32),
                   jax.ShapeDtypeStruct((T, D_B), F32), jax.ShapeDtypeStruct((D, D), F32),
                   jax.ShapeDtypeStruct((D, D_A), F32), jax.ShapeDtypeStruct((D, D_B), F32),
                   jax.ShapeDtypeStruct((N_GRP, BLK, BLK), F32), jax.ShapeDtypeStruct((N_GRP, BLK, BLK), F32),
                   jax.ShapeDtypeStruct((8, D_A), F32), jax.ShapeDtypeStruct((B, 8, D), F32)],
        scratch_shapes=[pltpu.VMEM((tm, D_A), BF16), pltpu.VMEM((tm, D_A), F32), pltpu.VMEM((tm, D_A), F32),
                        pltpu.VMEM((tm, D_A), BF16), pltpu.VMEM((tm, D_A), F32)],
        args=(au, av, gta, gtb, ob, dh, mods, g_ln, b_ln, ws, bsb, wa, wb, wout), semantics=("arbitrary",),
        plans=plans)


def _prologue(c_pad, w_ada, first_shards):
    cols = w_ada.shape[1]
    plan_w, plan_c = _GatherRelayed(first_shards), _GatherDirect([c_pad])
    plan_m = _Gather([jax.ShapeDtypeStruct((N_DEV * 8, cols), F32)])
    n_w = len(first_shards)

    def body(c_ref, wada_ref, *rest):
        w_ins, rest = rest[:n_w], rest[n_w:]
        call_ref, mods_ref = rest[:2]
        w_outs, rest = rest[2:2 + n_w], rest[2 + n_w:]
        cvm, part, local_sem = rest[:3]
        sems = rest[3:]
        sems_w, sems_c, sems_m = sems[:3], sems[3:6], sems[6:9]
        plan_c.start([c_ref], [call_ref], sems_c)
        plan_c.relay([c_ref], [call_ref], sems_c)
        plan_c.finish([c_ref], [call_ref], sems_c)
        load = pltpu.make_async_copy(call_ref, cvm, local_sem)
        load.start()
        load.wait()
        cv = cvm[...].reshape(N_DEV * 8, D)
        part[...] = _dot((cv * jax.nn.sigmoid(cv)).astype(BF16), wada_ref[...].astype(BF16))
        plan_m.start([part], [mods_ref], sems_m)
        plan_w.start(w_ins, w_outs, sems_w)
        plan_m.relay([part], [mods_ref], sems_m)
        plan_m.finish([part], [mods_ref], sems_m)
        plan_w.relay(w_ins, w_outs, sems_w)
        plan_w.finish(w_ins, w_outs, sems_w)

    res = pl.pallas_call(
        body, name="prologue", in_specs=[_any(), _vmem()] + [_any()] * n_w,
        out_specs=[_any()] * (2 + n_w), out_shape=plan_c.out_shape + plan_m.out_shape + plan_w.out_shape,
        scratch_shapes=[pltpu.VMEM((N_DEV, 8, D), F32), pltpu.VMEM((N_DEV * 8, cols), F32), pltpu.SemaphoreType.DMA]
        + plan_w.scratch + plan_c.scratch + plan_m.scratch,
        compiler_params=pltpu.CompilerParams(vmem_limit_bytes=VMEM_LIMIT),
    )(c_pad, w_ada, *first_shards)
    return res[0], res[1], list(res[2:])


def _ada_update(c_all, dm_cols, w, m, v, plans=()):
    n, cols = c_all.shape[0], w.shape[1]

    def body(c_ref, dm_ref, w_ref, m_ref, v_ref, g_ref, d_ref, nm_ref, nv_ref):
        cv = c_ref[...]
        g = _dot_tn((cv * jax.nn.sigmoid(cv)).astype(BF16), dm_ref[...].astype(BF16))
        g_ref[...] = g
        d_ref[...], nm_ref[...], nv_ref[...] = _adamw(w_ref[...], g, m_ref[...], v_ref[...])

    col = pl.BlockSpec((D, 128), lambda j: (0, j))
    return _call(
        body, name="ada_update", grid=(cols // 128,),
        in_specs=[pl.BlockSpec((n, D), lambda j: (0, 0)), pl.BlockSpec((n, 128), lambda j: (0, j)), col, col, col],
        out_specs=[col] * 4, out_shape=[jax.ShapeDtypeStruct(w.shape, F32)] * 4, args=(c_all, dm_cols, w, m, v),
        semantics=("parallel",), plans=plans)


def _mod_finish(mg1, mg2, mg2g, mg3, mods, g1, g2, g3):
    B = mods.shape[0]

    def body(mg1_ref, mg2_ref, mg2g_ref, mg3_ref, m_ref, g1_ref, g2_ref, g3_ref, dm_ref, dgn_ref):
        dgn_ref[...] = jnp.zeros(dgn_ref.shape, F32)
        for k, (mg, g_ref) in enumerate(((mg1_ref, g1_ref), (mg2_ref, g2_ref), (mg3_ref, g3_ref))):
            for b in range(B):
                s_dxr = mg[b, 1:2, :]
                dm_ref[b, 3 * k:3 * k + 1, :] = mg[b, 0:1, :]
                dm_ref[b, 3 * k + 1:3 * k + 2, :] = g_ref[...] * s_dxr
                dm_ref[b, 3 * k + 2:3 * k + 3, :] = mg2g_ref[b, 0:1, :] if k == 1 else mg[b, 2:3, :]
                dgn_ref[k:k + 1, :] += (1.0 + m_ref[b, 3 * k + 1:3 * k + 2, :]) * s_dxr

    args = (mg1, mg2, mg2g, mg3, mods, g1, g2, g3)
    out_shape = [jax.ShapeDtypeStruct((B, 9, D), F32), jax.ShapeDtypeStruct((8, D), F32)]
    return pl.pallas_call(body, name="mod_finish", grid=(1,), in_specs=[_whole(a) for a in args],
                          out_specs=[_whole(o) for o in out_shape], out_shape=out_shape,
                          compiler_params=_params("arbitrary"))(*args)


def _small_update(gathered, params, ms, vs):
    n = len(SMALL)
    B = gathered[0].shape[1]

    def body(*refs):
        gdm, ggn, gln, gws, gbs, ggq, ggk, gsk, gloss = refs[:9]
        w = dict(zip(SMALL, refs[9:9 + n]))
        m = dict(zip(SMALL, refs[9 + n:9 + 2 * n]))
        v = dict(zip(SMALL, refs[9 + 2 * n:9 + 3 * n]))
        outs = refs[9 + 3 * n:-1]
        out = {name: outs[4 * k:4 * k + 4] for k, name in enumerate(SMALL)}

        def total(ref, idx):
            acc = ref[(0,) + idx]
            for dev in range(1, N_DEV):
                acc = acc + ref[(dev,) + idx]
            return acc

        def finish(name, g, idx=(Ellipsis,)):
            d, nm, nv = _adamw(w[name][idx], g, m[name][idx], v[name][idx])
            for ref, val in zip(out[name], (g, d, nm, nv)):
                ref[idx] = val

        g_bada = total(gdm, (slice(0, 1),))
        for b in range(1, B):
            g_bada = g_bada + total(gdm, (slice(b, b + 1),))
        finish("b_ada", g_bada)
        finish("g_norm1", total(ggn, (slice(0, 1),)))
        finish("g_norm2", total(ggn, (slice(1, 2),)))
        finish("g_norm3", total(ggn, (slice(2, 3),)))
        finish("g_sgu_ln", total(gln, (slice(0, 1),)))
        finish("b_sgu_ln", total(gln, (slice(1, 2),)))
        tril = lax.broadcasted_iota(jnp.int32, (BLK, BLK), 0) >= lax.broadcasted_iota(jnp.int32, (BLK, BLK), 1)
        for g in range(N_GRP):
            finish("w_spatial", jnp.where(tril, total(gws, (g,)), 0.0), (0, g))
            finish("b_spatial", jnp.sum(total(gbs, (g,)).T, axis=0, keepdims=True), (0, slice(g, g + 1)))
        finish("g_q", total(ggq, ()))
        finish("g_k", total(ggk, ()))
        finish("attn_sinks", total(gsk, (slice(0, 1), slice(0, N_KV * Q_PER_KV))))
        refs[-1][...] = total(gloss, ())

    args = list(gathered) + [params[k] for k in SMALL] + [ms[k] for k in SMALL] + [vs[k] for k in SMALL]
    out_shape = []
    for k in SMALL:
        out_shape += [jax.ShapeDtypeStruct(params[k].shape, F32)] * 4
    out_shape.append(jax.ShapeDtypeStruct((8, 128), F32))
    res = pl.pallas_call(body, name="small_update", grid=(1,), in_specs=[_whole(a) for a in args],
                         out_specs=[_whole(o) for o in out_shape], out_shape=out_shape,
                         compiler_params=_params("arbitrary"))(*args)
    return {k: res[4 * i:4 * i + 4] for i, k in enumerate(SMALL)}, res[-1][0, 0]


def _rs_add(p, r, core, name):
    _, rows, width = p.shape

    def body(c_ref, p_ref, r_ref, o_ref):
        o_ref[...] = (p_ref[...] + r_ref[...]).astype(BF16)

    return pl.pallas_call(
        body, name="rs_add_" + name, out_shape=jax.ShapeDtypeStruct((4, rows, width), BF16),
        grid_spec=pltpu.PrefetchScalarGridSpec(
            num_scalar_prefetch=1, grid=(4,),
            in_specs=[pl.BlockSpec((1, rows, width), lambda k, c_ref: (2 * k + c_ref[0], 0, 0)),
                      pl.BlockSpec((1, rows, width), lambda k, c_ref: (k, 0, 0))],
            out_specs=pl.BlockSpec((1, rows, width), lambda k, c_ref: (k, 0, 0))),
        compiler_params=_params("parallel"),
    )(core, p, r)


def _adam_big(w, m, v, r, name):
    rows, cols = w.shape

    def body(r_ref, w_ref, m_ref, v_ref, go_ref, d_ref, nm_ref, nv_ref):
        g = r_ref[0].astype(F32)
        for k in range(1, 4):
            g = g + r_ref[k].astype(F32)
        go_ref[...] = g
        d_ref[...], nm_ref[...], nv_ref[...] = _adamw(w_ref[...], g, m_ref[...], v_ref[...])

    br = rows // 2
    shard = pl.BlockSpec((br, cols), lambda i: (i, 0))
    return pl.pallas_call(body, name="adam_" + name, grid=(2,),
                          in_specs=[pl.BlockSpec((4, br, cols), lambda i: (0, i, 0)), shard, shard, shard],
                          out_specs=[shard] * 4, out_shape=[jax.ShapeDtypeStruct(w.shape, F32)] * 4,
                          compiler_params=_params("parallel"))(r, w, m, v)


def kernel(x, c, w_ada, b_ada, g_norm1, ffn1_w_gate, ffn1_w_up, ffn1_w_down, g_norm2, w_in, g_sgu_ln, b_sgu_ln, w_spatial, b_spatial, g_q, g_k, attn_sinks, w_branch_a, w_branch_b, w_out, g_norm3, ffn2_w_gate, ffn2_w_up, ffn2_w_down, loss_target, m_w_ada, m_b_ada, m_g_norm1, m_ffn1_w_gate, m_ffn1_w_up, m_ffn1_w_down, m_g_norm2, m_w_in, m_g_sgu_ln, m_b_sgu_ln, m_w_spatial, m_b_spatial, m_g_q, m_g_k, m_attn_sinks, m_w_branch_a, m_w_branch_b, m_w_out, m_g_norm3, m_ffn2_w_gate, m_ffn2_w_up, m_ffn2_w_down, v_w_ada, v_b_ada, v_g_norm1, v_ffn1_w_gate, v_ffn1_w_up, v_ffn1_w_down, v_g_norm2, v_w_in, v_g_sgu_ln, v_b_sgu_ln, v_w_spatial, v_b_spatial, v_g_q, v_g_k, v_attn_sinks, v_w_branch_a, v_w_branch_b, v_w_out, v_g_norm3, v_ffn2_w_gate, v_ffn2_w_up, v_ffn2_w_down):
    names = ("w_ada", "b_ada", "g_norm1", "ffn1_w_gate", "ffn1_w_up", "ffn1_w_down", "g_norm2", "w_in", "g_sgu_ln",
             "b_sgu_ln", "w_spatial", "b_spatial", "g_q", "g_k", "attn_sinks", "w_branch_a", "w_branch_b", "w_out",
             "g_norm3", "ffn2_w_gate", "ffn2_w_up", "ffn2_w_down")
    w = dict(zip(names, (w_ada, b_ada, g_norm1, ffn1_w_gate, ffn1_w_up, ffn1_w_down, g_norm2, w_in, g_sgu_ln,
                         b_sgu_ln, w_spatial, b_spatial, g_q, g_k, attn_sinks, w_branch_a, w_branch_b, w_out, g_norm3,
                         ffn2_w_gate, ffn2_w_up, ffn2_w_down)))
    m = dict(zip(names, (m_w_ada, m_b_ada, m_g_norm1, m_ffn1_w_gate, m_ffn1_w_up, m_ffn1_w_down, m_g_norm2, m_w_in,
                         m_g_sgu_ln, m_b_sgu_ln, m_w_spatial, m_b_spatial, m_g_q, m_g_k, m_attn_sinks, m_w_branch_a,
                         m_w_branch_b, m_w_out, m_g_norm3, m_ffn2_w_gate, m_ffn2_w_up, m_ffn2_w_down)))
    v = dict(zip(names, (v_w_ada, v_b_ada, v_g_norm1, v_ffn1_w_gate, v_ffn1_w_up, v_ffn1_w_down, v_g_norm2, v_w_in,
                         v_g_sgu_ln, v_b_sgu_ln, v_w_spatial, v_b_spatial, v_g_q, v_g_k, v_attn_sinks, v_w_branch_a,
                         v_w_branch_b, v_w_out, v_g_norm3, v_ffn2_w_gate, v_ffn2_w_up, v_ffn2_w_down)))
    B, seq, _ = x.shape
    T = B * seq
    nbs = seq // BLK
    xi, yi, ci = _place()
    me = 4 * xi + 2 * yi + ci
    core = jnp.reshape(ci, (1,)).astype(jnp.int32)
    layout = {name: (tform, n, width) for name, tform, n, width in BIG}
    shard = {name: w[name][0].astype(BF16).T if tform else w[name][0].astype(BF16) for name, tform, _, _ in BIG}
    wts, parts, big_out = {}, {}, {}

    def gather_plan(group):
        return _Gather([shard[k] for k in group])

    def take(group, gathered):
        for k, g in zip(group, gathered):
            wts[k] = g.reshape(N_DEV * layout[k][1], layout[k][2])

    def blocks(name, grad):
        return grad.reshape(N_DEV, layout[name][1], layout[name][2])

    def to_sibling(names, grads):
        return _RsSibling([blocks(k, g) for k, g in zip(names, grads)])

    def add(names, grads, from_sibling):
        for k, g, r in zip(names, grads, from_sibling):
            parts[k] = _rs_add(blocks(k, g), r, core, k)

    def to_chips(names):
        return _RsChips([parts[k] for k in names])

    def update(names, from_chips):
        for k, r in zip(names, from_chips):
            if layout[k][0]:
                big_out[k] = [o.T[None] for o in _adam_big(w[k][0].T, m[k][0].T, v[k][0].T, r, k)]
            else:
                big_out[k] = [o[None] for o in _adam_big(w[k][0], m[k][0], v[k][0], r, k)]

    g1, g2, g3, g_ln, b_ln = g_norm1, g_norm2, g_norm3, g_sgu_ln, b_sgu_ln
    gq2, gk2 = jnp.tile(g_q, (1, 2)), jnp.tile(g_k, (1, 2))
    ws = w_spatial[0]
    bsb = jnp.broadcast_to(b_spatial[0][:, :, None], (N_GRP, BLK, BLK))
    xf = x.reshape(T, D)
    tgt = loss_target.reshape(T, D)

    ffn1 = ("ffn1_w_gate", "ffn1_w_up", "ffn1_w_down")
    ffn2 = ("ffn2_w_gate", "ffn2_w_up", "ffn2_w_down")
    c_pad = jnp.concatenate([c, jnp.zeros((8 - B, D), F32)], axis=0)
    c_all, mods_cols, gathered = _prologue(c_pad, w_ada[0], [shard[k] for k in ffn1[:2]])
    take(ffn1[:2], gathered)
    c_all = c_all[:, :B].reshape(N_DEV * B, D)
    mine = lax.dynamic_slice_in_dim(mods_cols, 8 * me, B, axis=1)
    mods = (mine.transpose(1, 0, 2).reshape(B, 9 * D) + b_ada).reshape(B, 9, D)

    group = ("ffn1_w_down", "w_branch_a", "w_branch_b", "w_out")
    (gb1, ub1), (gathered,) = _ffn_gate_up(xf, mods, g1, wts["ffn1_w_gate"], wts["ffn1_w_up"], 0, seq,
                                           plans=[gather_plan(group)])
    take(group, gathered)
    group = ("w_in",)
    (h1, y1), (gathered,) = _ffn_down(xf, gb1, ub1, mods, wts["ffn1_w_down"], 0, seq, plans=[gather_plan(group)])
    take(group, gathered)
    group = ("ffn2_w_gate",)
    (au, av, q_tok, k_tok, v_tok, gta, gtb), (gathered,) = _inproj_fwd(h1, mods, g2, wts["w_in"], seq,
                                                                       plans=[gather_plan(group)])
    take(group, gathered)
    group = ("ffn2_w_up",)
    ob, (gathered,) = _swa_fwd(q_tok, k_tok, v_tok, gq2, gk2, attn_sinks, nbs, plans=[gather_plan(group)])
    take(group, gathered)
    mixw = (wts["w_branch_a"], wts["w_branch_b"], wts["w_out"])
    group = ("ffn2_w_down",)
    h2, (gathered,) = _mix_fwd(au, av, gta, gtb, ob, h1, mods, g_ln, b_ln, ws, bsb, *mixw, seq,
                               plans=[gather_plan(group)])
    take(group, gathered)
    (dh3, y3, gb3, ub3, loss_parts), _ = _ffn_fwd(h2, mods, g3, *[wts[k] for k in ffn2], 6, seq, tgt=tgt)
    loss_part = jnp.full((8, 128), jnp.sum(loss_parts[:, 0, 0]))

    (dh2, xb, dyb, a, dg, du, mg3), _ = _ffn_bwd(h2, dh3, y3, gb3, ub3, mods, g3, *[wts[k] for k in ffn2], 6, seq)
    d_gate, _ = _wgrad(dg, xb, "wgrad_ffn2_gate")
    d_up, (r,) = _wgrad(du, xb, "wgrad_ffn2_up", plans=[to_sibling(ffn2[:1], [d_gate])])
    add(ffn2[:1], [d_gate], r)
    d_down, (r,) = _wgrad(a, dyb, "wgrad_ffn2_down", plans=[to_sibling(ffn2[1:2], [d_up])])
    add(ffn2[1:2], [d_up], r)

    (dau, dav, dgta, dgtb, dob, d_out, d_a, d_b, dws, dbs, dln, mg2g), (r, from_chips) = _mix_bwd(
        au, av, gta, gtb, ob, dh2, mods, g_ln, b_ln, ws, bsb, *mixw, seq,
        plans=[to_sibling(ffn2[2:], [d_down]), to_chips(ffn2[:2])])
    add(ffn2[2:], [d_down], r)
    update(ffn2[:2], from_chips)
    mixers = ("w_out", "w_branch_a", "w_branch_b")
    (dq, dk, dv, dk_halo, dv_halo, dgq2, dgk2, dsk), (from_chips, r) = _swa_bwd(
        q_tok, k_tok, v_tok, gq2, gk2, attn_sinks, dob, nbs,
        plans=[to_chips(ffn2[2:]), to_sibling(mixers, [d_out, d_a, d_b])])
    update(ffn2[2:], from_chips)
    add(mixers, [d_out, d_a, d_b], r)
    dk, dv = _swa_add_halo(dk, dk_halo), _swa_add_halo(dv, dv_halo)
    dgq, dgk = dgq2[:, :HD] + dgq2[:, HD:], dgk2[:, :HD] + dgk2[:, HD:]
    (dh1, xb2, dpb, mg2), (from_chips, early) = _inproj_bwd(
        h1, dh2, (dau, dav, dq, dk, dv, dgta, dgtb), mods, g2, wts["w_in"], seq,
        plans=[to_chips(mixers), _Gather([dln, dws, dbs, dgq, dgk, dsk, loss_part])])
    update(mixers, from_chips)

    d_in, _ = _wgrad(dpb, xb2, "wgrad_w_in")

    (dx, xb, dyb, a, dg, du, mg1), _ = _ffn_bwd(xf, dh1, y1, gb1, ub1, mods, g1, *[wts[k] for k in ffn1], 0, seq)
    dmods, dgn = _mod_finish(mg1, mg2, mg2g, mg3, mods, g1, g2, g3)
    d_gate, (r, late) = _wgrad(dg, xb, "wgrad_ffn1_gate",
                               plans=[to_sibling(("w_in",), [d_in]), _GatherDirect([dmods.reshape(B, 9 * D), dgn])])
    add(("w_in",), [d_in], r)
    gathered = late + early
    d_up, (r, from_chips) = _wgrad(du, xb, "wgrad_ffn1_up",
                                   plans=[to_sibling(ffn1[:1], [d_gate]), to_chips(("w_in",))])
    add(ffn1[:1], [d_gate], r)
    update(("w_in",), from_chips)
    d_down, (r, from_chips) = _wgrad(a, dyb, "wgrad_ffn1_down",
                                     plans=[to_sibling(ffn1[1:2], [d_up]), to_chips(ffn1[:1])])
    add(ffn1[1:2], [d_up], r)
    update(ffn1[:1], from_chips)
    small_out, loss = _small_update(gathered, w, m, v)
    dm_cols = lax.dynamic_slice_in_dim(gathered[0].reshape(N_DEV * B, 9 * D), (9 * D // N_DEV) * me,
                                       9 * D // N_DEV, axis=1)
    ada_out, (r, from_chips) = _ada_update(c_all, dm_cols, w_ada[0], m_w_ada[0], v_w_ada[0],
                                           plans=[to_sibling(ffn1[2:], [d_down]), to_chips(ffn1[1:2])])
    add(ffn1[2:], [d_down], r)
    update(ffn1[1:2], from_chips)
    (from_chips,) = _exchange([to_chips(ffn1[2:])], "rs_last")
    update(ffn1[2:], from_chips)

    def leaf(kind, name):
        if name == "w_ada":
            return ada_out[kind][None]
        if name in SMALL:
            return small_out[name][kind]
        return big_out[name][kind]

    return (loss, dx.reshape(B, seq, D), *[leaf(kind, name) for kind in range(4) for name in names])
```

```python
import math

import jax
import jax.numpy as jnp
from jax import lax
from jax.experimental import pallas as pl
from jax.experimental.pallas import tpu as pltpu

F32 = jnp.float32
BF16 = jnp.bfloat16
MESH = pl.DeviceIdType.MESH
N_DEV = 8

VMEM_LIMIT = 56 * 1024 * 1024

D = 1024
FF = 2816
FC = 2816
D_A = 512
D_B = 512
HD = 64
N_KV = 2
Q_PER_KV = 4
BLK = 128
N_GRP = 4
IN_COLS = 3840
PIECES = (("au", 0, 512), ("av", 512, 512), ("q", 1024, 512), ("k", 1536, 128), ("v", 1664, 128),
          ("ga", 1792, 1024), ("gb", 2816, 1024))
EPS = 1e-6
NEG = -1e30
GELU_C = math.sqrt(2.0 / math.pi)

ADAM_LR = 0.001
ADAM_B1 = 0.9
ADAM_B2 = 0.999
ADAM_EPS = 1e-08
ADAM_WD = 0.01
ADAM_STEP = 10

NT = (((1,), (1,)), ((), ()))
TN = (((0,), (0,)), ((), ()))

BIG = (("ffn1_w_gate", True, FF // N_DEV, D), ("ffn1_w_up", True, FF // N_DEV, D),
       ("ffn1_w_down", False, FF // N_DEV, D), ("w_in", True, IN_COLS // N_DEV, D),
       ("w_branch_a", True, D // N_DEV, D_A), ("w_branch_b", True, D // N_DEV, D_B), ("w_out", False, D // N_DEV, D),
       ("ffn2_w_gate", True, FF // N_DEV, D), ("ffn2_w_up", True, FF // N_DEV, D),
       ("ffn2_w_down", False, FF // N_DEV, D))
SMALL = ("b_ada", "g_norm1", "g_norm2", "g_sgu_ln", "b_sgu_ln", "w_spatial", "b_spatial", "g_q", "g_k",
         "attn_sinks", "g_norm3")


def _dot(a, b):
    return jnp.dot(a, b, preferred_element_type=F32)


def _dot_nt(a, b):
    return lax.dot_general(a, b, NT, preferred_element_type=F32)


def _dot_tn(a, b):
    return lax.dot_general(a, b, TN, preferred_element_type=F32)


def _vmem():
    return pl.BlockSpec(memory_space=pltpu.VMEM)


def _any():
    return pl.BlockSpec(memory_space=pl.ANY)


def _whole(a):
    return pl.BlockSpec(a.shape, lambda i: (0,) * len(a.shape))


def _rms_mod(h, g, sh, sc):
    inv = lax.rsqrt(jnp.mean(h * h, axis=-1, keepdims=True) + EPS)
    r = h * inv
    return (r * g) * (1.0 + sc) + sh, r, inv


def _rms_mod_bwd(dxn, r, inv, g, sc):
    dr = dxn * (g * (1.0 + sc))
    dh = inv * (dr - r * jnp.mean(dr * r, axis=-1, keepdims=True))
    return dh, jnp.sum(dxn, axis=0, keepdims=True), jnp.sum(dxn * r, axis=0, keepdims=True)


def _gelu(x):
    t = jnp.tanh(GELU_C * (x + 0.044715 * (x * x * x)))
    return 0.5 * x * (1.0 + t), t


def _gelu_grad(x, t):
    return 0.5 * (1.0 + t) + 0.5 * x * (1.0 - t * t) * (GELU_C * (1.0 + 3.0 * 0.044715 * x * x))


def _adamw(w, g, m, v):
    m = ADAM_B1 * m + (1.0 - ADAM_B1) * g
    v = ADAM_B2 * v + (1.0 - ADAM_B2) * (g * g)
    m_hat = m / (1.0 - ADAM_B1 ** ADAM_STEP)
    v_hat = v / (1.0 - ADAM_B2 ** ADAM_STEP)
    delta = -ADAM_LR * (m_hat / (jnp.sqrt(v_hat) + ADAM_EPS) + ADAM_WD * w)
    return delta, m, v


def _token_tile(seq, cap=512):
    return min(cap, seq)


def _params(*semantics):
    return pltpu.CompilerParams(dimension_semantics=semantics, vmem_limit_bytes=VMEM_LIMIT)


def _place():
    return lax.axis_index("x"), lax.axis_index("y"), lax.axis_index("c")


class _Gather:
    def __init__(self, arrays):
        n = len(arrays)
        self.ins = list(arrays)
        self.out_shape = [jax.ShapeDtypeStruct((N_DEV,) + a.shape, a.dtype) for a in arrays]
        self.scratch = [pltpu.SemaphoreType.DMA((n, 7)), pltpu.SemaphoreType.DMA((n, 7)),
                        pltpu.SemaphoreType.DMA((n,))]

    def _copies(self, ins, outs, sems):
        send_sems, recv_sems, local_sems = sems
        n = len(ins)
        x, y, c = _place()
        me, sibling = (x, y, c), (x, y, 1 - c)
        chips = [(1 - x, y), (x, 1 - y), (1 - x, 1 - y)]

        def slot(a, px, py, pc):
            return outs[a].at[4 * px + 2 * py + pc]

        def copy(a, k, block, to, src=None):
            return pltpu.make_async_remote_copy(
                src_ref=slot(a, *block) if src is None else src, dst_ref=slot(a, *block),
                send_sem=send_sems.at[a, k], recv_sem=recv_sems.at[a, k], device_id=to, device_id_type=MESH)

        mine = [pltpu.make_async_copy(ins[a], slot(a, *me), local_sems.at[a]) for a in range(n)]
        first = [copy(a, 0, me, sibling, src=ins[a]) for a in range(n)]
        first += [copy(a, 1 + j, me, (*chip, c), src=ins[a]) for a in range(n) for j, chip in enumerate(chips)]
        landed = [[copy(a, 1 + j, (*chip, c), me) for a in range(n)] for j, chip in enumerate(chips)]
        passed = [[copy(a, 4 + j, (*chip, c), sibling) for a in range(n)] for j, chip in enumerate(chips)]
        from_sibling = [copy(a, 0, sibling, me) for a in range(n)]
        from_sibling += [copy(a, 4 + j, (*chip, 1 - c), me) for a in range(n) for j, chip in enumerate(chips)]
        return mine, first, landed, passed, from_sibling

    def start(self, ins, outs, sems):
        mine, first, _, _, _ = self._copies(ins, outs, sems)
        for cp in mine + first:
            cp.start()

    def relay(self, ins, outs, sems):
        _, _, landed, passed, _ = self._copies(ins, outs, sems)
        for arrivals, forwards in zip(landed, passed):
            for arrival, forward in zip(arrivals, forwards):
                arrival.wait_recv()
                forward.start()

    def finish(self, ins, outs, sems):
        mine, first, _, passed, from_sibling = self._copies(ins, outs, sems)
        for cp in from_sibling:
            cp.wait_recv()
        for cp in first + [f for fs in passed for f in fs]:
            cp.wait_send()
        for cp in mine:
            cp.wait()


class _GatherRelayed:
    def __init__(self, arrays):
        n = len(arrays)
        self.ins = list(arrays)
        self.out_shape = [jax.ShapeDtypeStruct((N_DEV,) + a.shape, a.dtype) for a in arrays]
        self.scratch = [pltpu.SemaphoreType.DMA((n, 7)), pltpu.SemaphoreType.DMA((n, 7)),
                        pltpu.SemaphoreType.DMA((n,))]

    def _copies(self, ins, outs, sems):
        send_sems, recv_sems, local_sems = sems
        n = len(ins)
        x, y, c = _place()
        me, sibling = (x, y, c), (x, y, 1 - c)
        south = c == 0
        via = (jnp.where(south, 1 - x, x), jnp.where(south, y, 1 - y))
        onward = (jnp.where(south, x, 1 - x), jnp.where(south, 1 - y, y))
        via_sem = jnp.where(south, 1, 2)

        def slot(a, px, py, pc):
            return outs[a].at[4 * px + 2 * py + pc]

        def copy(a, k, block, to, src=None):
            return pltpu.make_async_remote_copy(
                src_ref=slot(a, *block) if src is None else src, dst_ref=slot(a, *block),
                send_sem=send_sems.at[a, k], recv_sem=recv_sems.at[a, k], device_id=to, device_id_type=MESH)

        chips = [(1 - x, y), (x, 1 - y), (1 - x, 1 - y)]
        mine = [pltpu.make_async_copy(ins[a], slot(a, *me), local_sems.at[a]) for a in range(n)]
        first = [copy(a, 0, me, sibling, src=ins[a]) for a in range(n)]
        first += [copy(a, 1 + j, me, (*chips[j], c), src=ins[a]) for a in range(n) for j in range(2)]
        to_pass = [copy(a, via_sem, (*via, c), me) for a in range(n)]
        passed_on = [copy(a, 3, (*via, c), (*onward, c)) for a in range(n)]
        landed = [[copy(a, 1 + j, (*chips[j], c), me) for a in range(n)] for j in range(3)]
        to_sibling = [[copy(a, 4 + j, (*chips[j], c), sibling) for a in range(n)] for j in range(3)]
        from_sibling = [copy(a, 0, sibling, me) for a in range(n)]
        from_sibling += [copy(a, 4 + j, (*chips[j], 1 - c), me) for a in range(n) for j in range(3)]
        return mine, first, to_pass, passed_on, landed, to_sibling, from_sibling

    def start(self, ins, outs, sems):
        mine, first, _, _, _, _, _ = self._copies(ins, outs, sems)
        for cp in mine + first:
            cp.start()

    def relay(self, ins, outs, sems):
        _, _, to_pass, passed_on, landed, to_sibling, _ = self._copies(ins, outs, sems)
        for arrival, onward in zip(to_pass, passed_on):
            arrival.wait_recv()
            onward.start()
        x, y, c = _place()
        for j in range(3):
            for a, (arrival, forward) in enumerate(zip(landed[j], to_sibling[j])):
                if j < 2:
                    pl.when((c == 0) != (j == 0))(arrival.wait_recv)
                else:
                    arrival.wait_recv()
                forward.start()

    def finish(self, ins, outs, sems):
        mine, first, _, passed_on, _, to_sibling, from_sibling = self._copies(ins, outs, sems)
        for cp in from_sibling:
            cp.wait_recv()
        for cp in first + passed_on + [f for fs in to_sibling for f in fs]:
            cp.wait_send()
        for cp in mine:
            cp.wait()


class _GatherDirect:
    def __init__(self, arrays):
        n = len(arrays)
        self.ins = list(arrays)
        self.out_shape = [jax.ShapeDtypeStruct((N_DEV,) + a.shape, a.dtype) for a in arrays]
        self.scratch = [pltpu.SemaphoreType.DMA((n, 7)), pltpu.SemaphoreType.DMA((n, 7)),
                        pltpu.SemaphoreType.DMA((n,))]

    def _copies(self, ins, outs, sems):
        send_sems, recv_sems, local_sems = sems
        n = len(ins)
        x, y, c = _place()
        peers = [(x ^ (k >> 2 & 1), y ^ (k >> 1 & 1), c ^ (k & 1)) for k in range(1, N_DEV)]

        def slot(a, px, py, pc):
            return outs[a].at[4 * px + 2 * py + pc]

        def copy(a, k, block, to, src=None):
            return pltpu.make_async_remote_copy(
                src_ref=slot(a, *block) if src is None else src, dst_ref=slot(a, *block),
                send_sem=send_sems.at[a, k], recv_sem=recv_sems.at[a, k], device_id=to, device_id_type=MESH)

        mine = [pltpu.make_async_copy(ins[a], slot(a, x, y, c), local_sems.at[a]) for a in range(n)]
        sends = [copy(a, k, (x, y, c), peer, src=ins[a]) for a in range(n) for k, peer in enumerate(peers)]
        arrivals = [copy(a, k, peer, (x, y, c)) for a in range(n) for k, peer in enumerate(peers)]
        return mine, sends, arrivals

    def start(self, ins, outs, sems):
        mine, sends, _ = self._copies(ins, outs, sems)
        for cp in mine + sends:
            cp.start()

    def relay(self, ins, outs, sems):
        pass

    def finish(self, ins, outs, sems):
        mine, sends, arrivals = self._copies(ins, outs, sems)
        for cp in arrivals:
            cp.wait_recv()
        for cp in sends:
            cp.wait_send()
        for cp in mine:
            cp.wait()


class _RsSibling:
    def __init__(self, ps):
        n = len(ps)
        self.ins = list(ps)
        self.out_shape = [jax.ShapeDtypeStruct((4,) + p.shape[1:], p.dtype) for p in ps]
        self.scratch = [pltpu.SemaphoreType.DMA((n, 4)), pltpu.SemaphoreType.DMA((n, 4))]

    def _copies(self, ins, outs, sems):
        send_sems, recv_sems = sems
        x, y, c = _place()
        return [pltpu.make_async_remote_copy(
            src_ref=ins[a].at[2 * q + (1 - c)], dst_ref=outs[a].at[q], send_sem=send_sems.at[a, q],
            recv_sem=recv_sems.at[a, q], device_id=(x, y, 1 - c), device_id_type=MESH)
            for a in range(len(ins)) for q in range(4)]

    def start(self, ins, outs, sems):
        for cp in self._copies(ins, outs, sems):
            cp.start()

    def relay(self, ins, outs, sems):
        pass

    def finish(self, ins, outs, sems):
        for cp in self._copies(ins, outs, sems):
            cp.wait()


class _RsChips:
    def __init__(self, qs):
        n = len(qs)
        self.ins = list(qs)
        self.out_shape = [jax.ShapeDtypeStruct(q.shape, q.dtype) for q in qs]
        self.scratch = [pltpu.SemaphoreType.DMA((n, 3)), pltpu.SemaphoreType.DMA((n, 3)),
                        pltpu.SemaphoreType.DMA((n,))]

    def _copies(self, ins, outs, sems):
        send_sems, recv_sems, local_sems = sems
        n = len(ins)
        x, y, c = _place()
        my_chip = 2 * x + y
        chips = [(1 - x, y), (x, 1 - y), (1 - x, 1 - y)]

        def copy(a, j, src_slot, dst_slot):
            px, py = chips[j]
            return pltpu.make_async_remote_copy(
                src_ref=ins[a].at[src_slot], dst_ref=outs[a].at[dst_slot], send_sem=send_sems.at[a, j],
                recv_sem=recv_sems.at[a, j], device_id=(px, py, c), device_id_type=MESH)

        own = [pltpu.make_async_copy(ins[a].at[my_chip], outs[a].at[my_chip], local_sems.at[a]) for a in range(n)]
        sends = [copy(a, j, 2 * px + py, my_chip) for a in range(n) for j, (px, py) in enumerate(chips)]
        arrivals = [copy(a, j, my_chip, 2 * px + py) for a in range(n) for j, (px, py) in enumerate(chips)]
        return own, sends, arrivals

    def start(self, ins, outs, sems):
        own, sends, _ = self._copies(ins, outs, sems)
        for cp in own + sends:
            cp.start()

    def relay(self, ins, outs, sems):
        pass

    def finish(self, ins, outs, sems):
        own, sends, arrivals = self._copies(ins, outs, sems)
        for cp in arrivals:
            cp.wait_recv()
        for cp in sends:
            cp.wait_send()
        for cp in own:
            cp.wait()


def _split_plans(plans, refs_in, refs_out, refs_scr, phase):
    i = o = s = 0
    for p in plans:
        ni, no, ns = len(p.ins), len(p.out_shape), len(p.scratch)
        getattr(p, phase)(refs_in[i:i + ni], refs_out[o:o + no], refs_scr[s:s + ns])
        i, o, s = i + ni, o + no, s + ns


def _plan_results(plans, res):
    out = []
    for p in plans:
        out.append(list(res[:len(p.out_shape)]))
        res = res[len(p.out_shape):]
    return out


def _exchange(plans, name):
    c_in = [a for p in plans for a in p.ins]
    c_out = [s for p in plans for s in p.out_shape]
    c_scr = [s for p in plans for s in p.scratch]

    def body(*refs):
        cin, cout, cscr = refs[:len(c_in)], refs[len(c_in):len(c_in) + len(c_out)], refs[len(c_in) + len(c_out):]
        for phase in ("start", "relay", "finish"):
            _split_plans(plans, cin, cout, cscr, phase)

    res = pl.pallas_call(body, name=name, in_specs=[_any()] * len(c_in), out_specs=[_any()] * len(c_out),
                         out_shape=c_out, scratch_shapes=c_scr)(*c_in)
    return _plan_results(plans, res)


def _call(body, *, name, grid, in_specs, out_specs, out_shape, args, semantics, scratch_shapes=(), plans=()):
    n_in, n_out, n_scr = len(in_specs), len(out_specs), len(scratch_shapes)
    c_in = [a for p in plans for a in p.ins]
    c_out = [s for p in plans for s in p.out_shape]
    c_scr = [s for p in plans for s in p.scratch]
    n_steps = math.prod(grid)

    def wrapped(*refs):
        ins, refs = refs[:n_in], refs[n_in:]
        cin, refs = refs[:len(c_in)], refs[len(c_in):]
        outs, refs = refs[:n_out], refs[n_out:]
        cout, refs = refs[:len(c_out)], refs[len(c_out):]
        scr, cscr = refs[:n_scr], refs[n_scr:]
        if plans:
            step = 0
            for d, g in enumerate(grid):
                step = step * g + pl.program_id(d)
            pl.when(step == 0)(lambda: _split_plans(plans, cin, cout, cscr, "start"))
        body(*ins, *outs, *scr)
        if plans:
            pl.when(step == max(n_steps - 2, 0))(lambda: _split_plans(plans, cin, cout, cscr, "relay"))
            pl.when(step == n_steps - 1)(lambda: _split_plans(plans, cin, cout, cscr, "finish"))

    res = pl.pallas_call(
        wrapped, name=name, grid=grid, in_specs=list(in_specs) + [_any()] * len(c_in),
        out_specs=list(out_specs) + [_any()] * len(c_out), out_shape=list(out_shape) + c_out,
        scratch_shapes=list(scratch_shapes) + c_scr,
        compiler_params=_params(*(("arbitrary",) * len(grid) if plans else semantics)),
    )(*args, *c_in)
    return list(res[:n_out]), _plan_results(plans, res[n_out:])


def _ffn_fwd_loss(h, mods, gn, wg, wu, wd, row0, seq, tgt):
    T = h.shape[0]
    tm = _token_tile(seq, 256)
    tps = seq // tm
    n_t = T // tm

    def body(h_ref, m_ref, g_ref, wg_ref, wu_ref, wd_ref, tgt_ref, out_ref, y_ref, gb_ref, ub_ref, loss_ref):
        hv = h_ref[...]
        sh = m_ref[0, row0:row0 + 1, :]
        sc = m_ref[0, row0 + 1:row0 + 2, :]
        ga = m_ref[0, row0 + 2:row0 + 3, :]
        xn, _, _ = _rms_mod(hv, g_ref[...], sh, sc)
        xb = xn.astype(BF16)
        acc = jnp.zeros((tm, D), F32)
        for c0 in range(0, FF, FC):
            gg = _dot_nt(xb, wg_ref[c0:c0 + FC, :])
            uu = _dot_nt(xb, wu_ref[c0:c0 + FC, :])
            gg, uu = gg.astype(BF16), uu.astype(BF16)
            gb_ref[:, c0:c0 + FC] = gg
            ub_ref[:, c0:c0 + FC] = uu
            acc = acc + _dot((gg * jax.nn.sigmoid(gg)) * uu, wd_ref[c0:c0 + FC, :])
        y_ref[...] = acc
        d = hv + (0.5 * ga) * acc - tgt_ref[...]
        out_ref[...] = d * (1.0 / D)
        loss_ref[...] = jnp.full((1, 8, 128), 0.5 / D, F32) * jnp.sum(d * d)

    tok = pl.BlockSpec((tm, D), lambda i: (i, 0))
    tokf = pl.BlockSpec((tm, FF), lambda i: (i, 0))
    outs, _ = _call(
        body, name="ffn_fwd_loss", grid=(n_t,),
        in_specs=[tok, pl.BlockSpec((1, 9, D), lambda i: (i // tps, 0, 0)), pl.BlockSpec((1, D), lambda i: (0, 0)),
                  _vmem(), _vmem(), _vmem(), tok],
        out_specs=[tok, tok, tokf, tokf, pl.BlockSpec((1, 8, 128), lambda i: (i, 0, 0))],
        out_shape=[jax.ShapeDtypeStruct((T, D), F32), jax.ShapeDtypeStruct((T, D), F32),
                   jax.ShapeDtypeStruct((T, FF), BF16), jax.ShapeDtypeStruct((T, FF), BF16),
                   jax.ShapeDtypeStruct((n_t, 8, 128), F32)],
        args=(h, mods, gn, wg, wu, wd, tgt), semantics=("parallel",))
    return outs


def _ffn_gate_up(h, mods, gn, wg, wu, row0, seq, plans=()):
    T = h.shape[0]
    tm = _token_tile(seq, 512)
    tps = seq // tm

    def body(h_ref, m_ref, g_ref, wg_ref, wu_ref, gb_ref, ub_ref):
        xn, _, _ = _rms_mod(h_ref[...], g_ref[...], m_ref[0, row0:row0 + 1, :], m_ref[0, row0 + 1:row0 + 2, :])
        xb = xn.astype(BF16)
        for c0 in range(0, FF, FC):
            gb_ref[:, c0:c0 + FC] = _dot_nt(xb, wg_ref[c0:c0 + FC, :]).astype(BF16)
            ub_ref[:, c0:c0 + FC] = _dot_nt(xb, wu_ref[c0:c0 + FC, :]).astype(BF16)

    tokf = pl.BlockSpec((tm, FF), lambda i: (i, 0))
    return _call(
        body, name="ffn_gate_up", grid=(T // tm,),
        in_specs=[pl.BlockSpec((tm, D), lambda i: (i, 0)), pl.BlockSpec((1, 9, D), lambda i: (i // tps, 0, 0)),
                  pl.BlockSpec((1, D), lambda i: (0, 0)), _vmem(), _vmem()],
        out_specs=[tokf, tokf], out_shape=[jax.ShapeDtypeStruct((T, FF), BF16)] * 2, args=(h, mods, gn, wg, wu),
        semantics=("parallel",), plans=plans)


def _ffn_down(h, gb, ub, mods, wd, row0, seq, plans=()):
    T = h.shape[0]
    tm = _token_tile(seq, 512)
    tps = seq // tm

    def body(h_ref, gb_ref, ub_ref, m_ref, wd_ref, out_ref, y_ref):
        acc = jnp.zeros((tm, D), F32)
        for c0 in range(0, FF, FC):
            gg = gb_ref[:, c0:c0 + FC]
            acc = acc + _dot((gg * jax.nn.sigmoid(gg)) * ub_ref[:, c0:c0 + FC], wd_ref[c0:c0 + FC, :])
        y_ref[...] = acc
        out_ref[...] = h_ref[...] + (0.5 * m_ref[0, row0 + 2:row0 + 3, :]) * acc

    tok = pl.BlockSpec((tm, D), lambda i: (i, 0))
    tokf = pl.BlockSpec((tm, FF), lambda i: (i, 0))
    return _call(
        body, name="ffn_down", grid=(T // tm,),
        in_specs=[tok, tokf, tokf, pl.BlockSpec((1, 9, D), lambda i: (i // tps, 0, 0)), _vmem()],
        out_specs=[tok, tok], out_shape=[jax.ShapeDtypeStruct((T, D), F32)] * 2, args=(h, gb, ub, mods, wd),
        semantics=("parallel",), plans=plans)


def _ffn_bwd(h, dhn, y, gb, ub, mods, gn, wg, wu, wd, row0, seq, plans=()):
    T = h.shape[0]
    B = T // seq
    tm = _token_tile(seq, 256)
    tps = seq // tm
    n_t = T // tm

    def body(h_ref, dhn_ref, y_ref, gb_ref, ub_ref, m_ref, g_ref, wg_ref, wu_ref, wd_ref,
             dh_ref, xb_ref, dyb_ref, a_ref, dg_ref, du_ref, mg_ref):
        i = pl.program_id(0)
        hv = h_ref[...]
        dhn = dhn_ref[...]
        sh = m_ref[0, row0:row0 + 1, :]
        sc = m_ref[0, row0 + 1:row0 + 2, :]
        ga = m_ref[0, row0 + 2:row0 + 3, :]
        g = g_ref[...]
        xn, r, inv = _rms_mod(hv, g, sh, sc)
        xb_ref[...] = xn.astype(BF16)
        dyb = ((0.5 * ga) * dhn).astype(BF16)
        dyb_ref[...] = dyb
        dga = 0.5 * jnp.sum(dhn * y_ref[...], axis=0, keepdims=True)
        dxn = jnp.zeros((tm, D), F32)
        for c0 in range(0, FF, FC):
            wgc = wg_ref[c0:c0 + FC, :]
            wuc = wu_ref[c0:c0 + FC, :]
            gg = gb_ref[:, c0:c0 + FC]
            uu = ub_ref[:, c0:c0 + FC]
            sig = jax.nn.sigmoid(gg)
            s = gg * sig
            a_ref[:, c0:c0 + FC] = s * uu
            da = _dot_nt(dyb, wd_ref[c0:c0 + FC, :]).astype(BF16)
            dub = da * s
            dgb = (da * uu) * (sig * (1.0 + gg * (1.0 - sig)))
            dg_ref[:, c0:c0 + FC] = dgb
            du_ref[:, c0:c0 + FC] = dub
            dxn = dxn + _dot(dgb, wgc) + _dot(dub, wuc)
        dh, s_dxn, s_dxr = _rms_mod_bwd(dxn, r, inv, g, sc)
        dh_ref[...] = dhn + dh

        @pl.when(i % tps == 0)
        def _():
            mg_ref[...] = jnp.zeros(mg_ref.shape, F32)

        mg_ref[0, 0:1, :] += s_dxn
        mg_ref[0, 1:2, :] += s_dxr
        mg_ref[0, 2:3, :] += dga

    tok = pl.BlockSpec((tm, D), lambda i: (i, 0))
    tokf = pl.BlockSpec((tm, FF), lambda i: (i, 0))
    return _call(
        body, name="ffn_bwd", grid=(n_t,),
        in_specs=[tok, tok, tok, tokf, tokf, pl.BlockSpec((1, 9, D), lambda i: (i // tps, 0, 0)),
                  pl.BlockSpec((1, D), lambda i: (0, 0)), _vmem(), _vmem(), _vmem()],
        out_specs=[tok, tok, tok, tokf, tokf, tokf, pl.BlockSpec((1, 8, D), lambda i: (i // tps, 0, 0))],
        out_shape=[jax.ShapeDtypeStruct((T, D), F32), jax.ShapeDtypeStruct((T, D), BF16),
                   jax.ShapeDtypeStruct((T, D), BF16), jax.ShapeDtypeStruct((T, FF), BF16),
                   jax.ShapeDtypeStruct((T, FF), BF16), jax.ShapeDtypeStruct((T, FF), BF16),
                   jax.ShapeDtypeStruct((B, 8, D), F32)],
        args=(h, dhn, y, gb, ub, mods, gn, wg, wu, wd), semantics=("arbitrary",), plans=plans)


def _wgrad(a, b, name, plans=()):
    T, da = a.shape
    db = b.shape[1]
    bm = {2816: 1408, 3840: 1280}[da]
    bn = db
    tk = min(2048, T)
    nk = T // tk

    def body(a_ref, b_ref, o_ref):
        @pl.when(pl.program_id(2) == 0)
        def _():
            o_ref[...] = jnp.zeros(o_ref.shape, F32)

        o_ref[...] += _dot_tn(a_ref[...], b_ref[...])

    (out,), plan_outs = _call(
        body, name=name, grid=(da // bm, db // bn, nk),
        in_specs=[pl.BlockSpec((tk, bm), lambda i, j, k: (k, i)), pl.BlockSpec((tk, bn), lambda i, j, k: (k, j))],
        out_specs=[pl.BlockSpec((bm, bn), lambda i, j, k: (i, j))], out_shape=[jax.ShapeDtypeStruct((da, db), F32)],
        args=(a, b), semantics=("parallel", "parallel", "arbitrary"), plans=plans)
    return out, plan_outs


def _inproj_fwd(h, mods, gn, w_in, seq, plans=()):
    T = h.shape[0]
    tm = _token_tile(seq)
    tps = seq // tm

    def body(h_ref, m_ref, g_ref, w_ref, *outs):
        xn, _, _ = _rms_mod(h_ref[...], g_ref[...], m_ref[0, 3:4, :], m_ref[0, 4:5, :])
        xb = xn.astype(BF16)
        for (_, c0, w), o_ref in zip(PIECES, outs):
            o_ref[...] = _dot_nt(xb, w_ref[c0:c0 + w, :])

    return _call(
        body, name="inproj_fwd", grid=(T // tm,),
        in_specs=[pl.BlockSpec((tm, D), lambda i: (i, 0)), pl.BlockSpec((1, 9, D), lambda i: (i // tps, 0, 0)),
                  pl.BlockSpec((1, D), lambda i: (0, 0)), _vmem()],
        out_specs=[pl.BlockSpec((tm, w), lambda i: (i, 0)) for _, _, w in PIECES],
        out_shape=[jax.ShapeDtypeStruct((T, w), F32) for _, _, w in PIECES], args=(h, mods, gn, w_in),
        semantics=("parallel",), plans=plans)


def _inproj_bwd(h, dh_res, dpieces, mods, gn, w_in, seq, plans=()):
    T = h.shape[0]
    B = T // seq
    tm = _token_tile(seq, 256)
    tps = seq // tm

    def body(h_ref, dres_ref, *rest):
        dp_refs = rest[:len(PIECES)]
        m_ref, g_ref, w_ref, dh_ref, xb_ref, dpb_ref, mg_ref = rest[len(PIECES):]
        i = pl.program_id(0)
        g = g_ref[...]
        sc = m_ref[0, 4:5, :]
        xn, r, inv = _rms_mod(h_ref[...], g, m_ref[0, 3:4, :], sc)
        xb_ref[...] = xn.astype(BF16)
        dxn = jnp.zeros((tm, D), F32)
        for (_, c0, w), dp_ref in zip(PIECES, dp_refs):
            dpb = dp_ref[...].astype(BF16)
            dpb_ref[:, c0:c0 + w] = dpb
            dxn = dxn + _dot(dpb, w_ref[c0:c0 + w, :])
        dh, s_dxn, s_dxr = _rms_mod_bwd(dxn, r, inv, g, sc)
        dh_ref[...] = dres_ref[...] + dh

        @pl.when(i % tps == 0)
        def _():
            mg_ref[...] = jnp.zeros(mg_ref.shape, F32)

        mg_ref[0, 0:1, :] += s_dxn
        mg_ref[0, 1:2, :] += s_dxr

    tok = pl.BlockSpec((tm, D), lambda i: (i, 0))
    return _call(
        body, name="inproj_bwd", grid=(T // tm,),
        in_specs=[tok, tok] + [pl.BlockSpec((tm, w), lambda i: (i, 0)) for _, _, w in PIECES]
        + [pl.BlockSpec((1, 9, D), lambda i: (i // tps, 0, 0)), pl.BlockSpec((1, D), lambda i: (0, 0)), _vmem()],
        out_specs=[tok, tok, pl.BlockSpec((tm, IN_COLS), lambda i: (i, 0)),
                   pl.BlockSpec((1, 8, D), lambda i: (i // tps, 0, 0))],
        out_shape=[jax.ShapeDtypeStruct((T, D), F32), jax.ShapeDtypeStruct((T, D), BF16),
                   jax.ShapeDtypeStruct((T, IN_COLS), BF16), jax.ShapeDtypeStruct((B, 8, D), F32)],
        args=(h, dh_res, *dpieces, mods, gn, w_in), semantics=("arbitrary",), plans=plans)


def _seg_mean(x):
    i = lax.broadcasted_iota(jnp.int32, (128, 128), 0) >> 6
    j = lax.broadcasted_iota(jnp.int32, (128, 128), 1) >> 6
    ones = jnp.where(i == j, 1.0 / HD, 0.0).astype(BF16)
    hi = x.astype(BF16)
    lo = (x - hi.astype(F32)).astype(BF16)
    return _dot(hi, ones) + _dot(lo, ones)


def _head_norm(x, g2):
    inv = lax.rsqrt(_seg_mean(x * x) + EPS)
    y = x * inv
    return y * g2, y, inv


def _head_norm_bwd(dxn, y, inv, g2):
    dy = dxn * g2
    return inv * (dy - y * _seg_mean(dy * y)), jnp.sum(dxn * y, axis=0, keepdims=True)


def _swa_block(q, kk, vv, gq2, gk2, sinks, first, do=None):
    lo = lax.broadcasted_iota(jnp.int32, (1, 128), 1) < HD
    kn, ky, kinv = _head_norm(kk, gk2)

    def operands(x):
        xr = pltpu.roll(x, HD, 1)
        own_lo, own_hi = jnp.where(lo, x, 0.0).astype(BF16), jnp.where(lo, 0.0, x).astype(BF16)
        rolled_lo, rolled_hi = jnp.where(lo, xr, 0.0).astype(BF16), jnp.where(lo, 0.0, xr).astype(BF16)
        return (own_lo, rolled_hi), (rolled_lo, own_hi)

    def restore(parts):
        (own_lo, rolled_hi), (rolled_lo, own_hi) = parts
        return (jnp.where(lo, own_lo, own_hi)
                + pltpu.roll(jnp.where(lo, rolled_lo, rolled_hi), HD, 1))

    k_ops, v_ops = operands(kn), operands(vv)
    k2 = [jnp.concatenate(pair, axis=0) for pair in k_ops]
    def stack(x):
        return jnp.concatenate([x[:, 128 * p:128 * (p + 1)] for p in range(4)], axis=0)

    def unstack(x):
        return jnp.concatenate([x[BLK * p:BLK * (p + 1)] for p in range(4)], axis=1)

    def of_head(x, kh):
        return x[2 * BLK * kh:2 * BLK * (kh + 1)]

    nq = 4 * BLK
    pair = lax.broadcasted_iota(jnp.int32, (nq, 1), 0) >> 7
    row = lax.broadcasted_iota(jnp.int32, (nq, 2 * BLK), 0) & (BLK - 1)
    col = lax.broadcasted_iota(jnp.int32, (nq, 2 * BLK), 1)
    valid = (col <= row + BLK) & (col > row) & (col >= jnp.where(first, BLK, 0))
    scale = HD ** -0.5
    qn, qy, qinv = _head_norm(stack(q), gq2)
    qnb = qn.astype(BF16)
    s2 = jnp.concatenate([_dot_nt(of_head(qnb, kh), k2[kh]) for kh in range(N_KV)], axis=0) * scale
    probs, p_sink = [], []
    for j in range(2):
        s = jnp.where(valid, s2[:, 2 * BLK * j:2 * BLK * (j + 1)], NEG)
        sink = jnp.zeros((nq, 1), F32)
        for p in range(4):
            sink = jnp.where(pair == p, sinks[:, 2 * p + j:2 * p + j + 1], sink)
        m = jnp.maximum(jnp.max(s, axis=-1, keepdims=True), sink)
        e = jnp.exp(s - m)
        e_sink = jnp.exp(sink - m)
        rden = 1.0 / (jnp.sum(e, axis=-1, keepdims=True) + e_sink)
        probs.append(e * rden)
        p_sink.append(e_sink * rden)
    pb = [p.astype(BF16) for p in probs]
    if do is None:
        return unstack(jnp.concatenate(
            [_dot(of_head(pb[0], kh), v_ops[kh][0]) + _dot(of_head(pb[1], kh), v_ops[kh][1]) for kh in range(N_KV)],
            axis=0))
    dob = stack(do).astype(BF16)
    ds, dsinks = [], [None] * 8
    for j in range(2):
        dp = jnp.concatenate([_dot_nt(of_head(dob, kh), v_ops[kh][j]) for kh in range(N_KV)], axis=0)
        t = jnp.sum(dp * probs[j], axis=-1, keepdims=True)
        ds.append(probs[j] * (dp - t) * scale)
        lost = p_sink[j] * t
        for p in range(4):
            dsinks[2 * p + j] = -jnp.sum(lost[BLK * p:BLK * (p + 1)])
    dsb = jnp.concatenate(ds, axis=1).astype(BF16)
    dqn = jnp.concatenate([_dot(of_head(dsb, kh), k2[kh]) for kh in range(N_KV)], axis=0)
    dq, dgq2 = _head_norm_bwd(dqn, qy, qinv, gq2)
    dk2 = [_dot_tn(of_head(dsb, kh), of_head(qnb, kh)) for kh in range(N_KV)]
    dv_ops = [[_dot_tn(of_head(pb[j], kh), of_head(dob, kh)) for j in range(2)] for kh in range(N_KV)]
    dkn = restore(tuple((d[:2 * BLK], d[2 * BLK:]) for d in dk2))
    dvv = restore(tuple(tuple(d) for d in dv_ops))
    dkk, dgk2 = _head_norm_bwd(dkn, ky, kinv, gk2)
    return unstack(dq), dkk, dvv, dgq2, dgk2, dsinks


SWA_GROUP = 8


def _swa_specs(nbs):
    grp = min(SWA_GROUP, nbs)
    rows = grp * BLK

    def tok(w):
        return pl.BlockSpec((rows, w), lambda i: (i, 0))

    halo = pl.BlockSpec((BLK, 128), lambda i: (jnp.maximum(i * grp - 1, 0), 0))
    vec = pl.BlockSpec((1, 128), lambda i: (0, 0))
    sk = pl.BlockSpec((1, 8), lambda i: (0, 0))
    return grp, tok, halo, vec, sk


def _swa_fwd(q, k, v, gq2, gk2, sinks, nbs, plans=()):
    T = q.shape[0]
    grp, tok, halo, vec, sk = _swa_specs(nbs)

    def body(q_ref, kh_ref, kc_ref, vh_ref, vc_ref, gq_ref, gk_ref, sk_ref, o_ref):
        seq_start = ((pl.program_id(0) * grp) % nbs) == 0
        for g in range(grp):
            rows = slice(g * BLK, (g + 1) * BLK)
            prev = slice((g - 1) * BLK, g * BLK)
            kk = jnp.concatenate([kh_ref[...] if g == 0 else kc_ref[prev, :], kc_ref[rows, :]], axis=0)
            vv = jnp.concatenate([vh_ref[...] if g == 0 else vc_ref[prev, :], vc_ref[rows, :]], axis=0)
            o_ref[rows, :] = _swa_block(q_ref[rows, :], kk, vv, gq_ref[...], gk_ref[...], sk_ref[...],
                                        seq_start if g == 0 else False)

    (out,), plan_outs = _call(
        body, name="swa_fwd", grid=(T // (grp * BLK),),
        in_specs=[tok(D_B), halo, tok(128), halo, tok(128), vec, vec, sk], out_specs=[tok(D_B)],
        out_shape=[jax.ShapeDtypeStruct((T, D_B), F32)], args=(q, k, k, v, v, gq2, gk2, sinks),
        semantics=("parallel",), plans=plans)
    return out, plan_outs


def _swa_bwd(q, k, v, gq2, gk2, sinks, do, nbs, plans=()):
    T = q.shape[0]
    grp, tok, halo, vec, sk = _swa_specs(nbs)
    steps = T // (grp * BLK)

    def body(q_ref, kh_ref, kc_ref, vh_ref, vc_ref, gq_ref, gk_ref, sk_ref, do_ref,
             dq_ref, dk_ref, dv_ref, dkh_ref, dvh_ref, dgq_ref, dgk_ref, dsk_ref):
        i = pl.program_id(0)
        seq_start = ((i * grp) % nbs) == 0

        @pl.when(i == 0)
        def _():
            for r in (dgq_ref, dgk_ref, dsk_ref):
                r[...] = jnp.zeros(r.shape, F32)

        res = []
        for g in range(grp):
            rows = slice(g * BLK, (g + 1) * BLK)
            prev = slice((g - 1) * BLK, g * BLK)
            kk = jnp.concatenate([kh_ref[...] if g == 0 else kc_ref[prev, :], kc_ref[rows, :]], axis=0)
            vv = jnp.concatenate([vh_ref[...] if g == 0 else vc_ref[prev, :], vc_ref[rows, :]], axis=0)
            res.append(_swa_block(q_ref[rows, :], kk, vv, gq_ref[...], gk_ref[...], sk_ref[...],
                                  seq_start if g == 0 else False, do=do_ref[rows, :]))
        lane = lax.broadcasted_iota(jnp.int32, (8, 128), 1)
        upd = jnp.zeros((8, 128), F32)
        for g, (dq, dkk, dvv, dgq2, dgk2, dsinks) in enumerate(res):
            rows = slice(g * BLK, (g + 1) * BLK)
            dq_ref[rows, :] = dq
            dk_ref[rows, :] = dkk[BLK:] + res[g + 1][1][:BLK] if g + 1 < grp else dkk[BLK:]
            dv_ref[rows, :] = dvv[BLK:] + res[g + 1][2][:BLK] if g + 1 < grp else dvv[BLK:]
            dgq_ref[...] += dgq2
            dgk_ref[...] += dgk2
            for h, d in enumerate(dsinks):
                upd = upd + jnp.where(lane == h, d, 0.0)
        dkh_ref[0] = res[0][1][:BLK]
        dvh_ref[0] = res[0][2][:BLK]
        dsk_ref[...] += upd

    one = pl.BlockSpec((1, BLK, 128), lambda i: (i, 0, 0))
    halo_shape = jax.ShapeDtypeStruct((steps, BLK, 128), F32)
    return _call(
        body, name="swa_bwd", grid=(steps,),
        in_specs=[tok(D_B), halo, tok(128), halo, tok(128), vec, vec, sk, tok(D_B)],
        out_specs=[tok(D_B), tok(128), tok(128), one, one, vec, vec, pl.BlockSpec((8, 128), lambda i: (0, 0))],
        out_shape=[jax.ShapeDtypeStruct((T, D_B), F32), jax.ShapeDtypeStruct((T, 128), F32),
                   jax.ShapeDtypeStruct((T, 128), F32), halo_shape, halo_shape, jax.ShapeDtypeStruct((1, 128), F32),
                   jax.ShapeDtypeStruct((1, 128), F32), jax.ShapeDtypeStruct((8, 128), F32)],
        args=(q, k, k, v, v, gq2, gk2, sinks, do), semantics=("arbitrary",), plans=plans)


def _swa_add_halo(dk, dk_halo):
    steps = dk_halo.shape[0]
    nxt = jnp.concatenate([dk_halo[1:], jnp.zeros_like(dk_halo[:1])], axis=0)[:, None]
    dk = dk.reshape(steps, -1, BLK, 128)
    return jnp.concatenate([dk[:, :-1], dk[:, -1:] + nxt], axis=1).reshape(-1, 128)


def _sgu_norm(av, g_ln, b_ln):
    t, th = _gelu(av)
    mu = jnp.mean(t, axis=-1, keepdims=True)
    tc = t - mu
    rstd = lax.rsqrt(jnp.mean(tc * tc, axis=-1, keepdims=True) + EPS)
    vhat = tc * rstd
    return vhat * g_ln + b_ln, vhat, rstd, th


def _masked_ws(ws_ref):
    tril = lax.broadcasted_iota(jnp.int32, (BLK, BLK), 0) >= lax.broadcasted_iota(jnp.int32, (BLK, BLK), 1)
    return [jnp.where(tril, ws_ref[g], 0.0).astype(BF16) for g in range(N_GRP)]


def _mix_fwd(au, av, gta, gtb, ob, h, mods, g_ln, b_ln, ws, bsb, wa, wb, wout, seq, plans=()):
    T = h.shape[0]
    tm = _token_tile(seq)
    tps = seq // tm

    def body(au_ref, av_ref, gta_ref, gtb_ref, ob_ref, h_ref, m_ref, gl_ref, bl_ref, ws_ref, bs_ref,
             wa_ref, wb_ref, wo_ref, out_ref, vvb_s, z_s):
        u, _ = _gelu(au_ref[...])
        vv, _, _, _ = _sgu_norm(av_ref[...], gl_ref[...], bl_ref[...])
        vvb_s[...] = vv.astype(BF16)
        wsm = _masked_ws(ws_ref)
        for c in range(tm // BLK):
            rows = slice(c * BLK, (c + 1) * BLK)
            for g in range(N_GRP):
                cols = slice(g * BLK, (g + 1) * BLK)
                z_s[rows, cols] = _dot(wsm[g], vvb_s[rows, cols]) + bs_ref[g]
        ya = _dot_nt((u * z_s[...]).astype(BF16), wa_ref[...])
        yb = _dot_nt(ob_ref[...].astype(BF16), wb_ref[...])
        merged = jax.nn.sigmoid(gta_ref[...]) * ya + jax.nn.sigmoid(gtb_ref[...]) * yb
        out_ref[...] = h_ref[...] + m_ref[0, 5:6, :] * _dot(merged.astype(BF16), wo_ref[...])

    def tok(w):
        return pl.BlockSpec((tm, w), lambda i: (i, 0))

    def full(shape):
        return pl.BlockSpec(shape, lambda i: (0,) * len(shape))

    (out,), plan_outs = _call(
        body, name="mix_fwd", grid=(T // tm,),
        in_specs=[tok(D_A), tok(D_A), tok(D), tok(D), tok(D_B), tok(D),
                  pl.BlockSpec((1, 9, D), lambda i: (i // tps, 0, 0)), full((1, D_A)), full((1, D_A)),
                  full((N_GRP, BLK, BLK)), full((N_GRP, BLK, BLK)), _vmem(), _vmem(), _vmem()],
        out_specs=[tok(D)], out_shape=[jax.ShapeDtypeStruct((T, D), F32)],
        scratch_shapes=[pltpu.VMEM((tm, D_A), BF16), pltpu.VMEM((tm, D_A), F32)],
        args=(au, av, gta, gtb, ob, h, mods, g_ln, b_ln, ws, bsb, wa, wb, wout), semantics=("parallel",),
        plans=plans)
    return out, plan_outs


def _mix_bwd(au, av, gta, gtb, ob, dh, mods, g_ln, b_ln, ws, bsb, wa, wb, wout, seq, plans=()):
    T = dh.shape[0]
    B = T // seq
    tm = _token_tile(seq, 256)
    tps = seq // tm

    def body(au_ref, av_ref, gta_ref, gtb_ref, ob_ref, dh_ref, m_ref, gl_ref, bl_ref, ws_ref, bs_ref,
             wa_ref, wb_ref, wo_ref,
             dau_ref, dav_ref, dgta_ref, dgtb_ref, dob_ref, dwo_ref, dwa_ref, dwb_ref, dws_ref, dbs_ref, dln_ref,
             mg_ref, vvb_s, z_s, dz_s, dzb_s, dvv_s):
        i = pl.program_id(0)

        @pl.when(i == 0)
        def _():
            for r in (dwo_ref, dwa_ref, dwb_ref, dws_ref, dbs_ref, dln_ref):
                r[...] = jnp.zeros(r.shape, F32)

        @pl.when(i % tps == 0)
        def _():
            mg_ref[...] = jnp.zeros(mg_ref.shape, F32)

        auv = au_ref[...]
        avv = av_ref[...]
        u, thu = _gelu(auv)
        g_ln = gl_ref[...]
        vv, vhat, rstd, thv = _sgu_norm(avv, g_ln, bl_ref[...])
        vvb_s[...] = vv.astype(BF16)
        wsm = _masked_ws(ws_ref)
        for c in range(tm // BLK):
            rows = slice(c * BLK, (c + 1) * BLK)
            for g in range(N_GRP):
                cols = slice(g * BLK, (g + 1) * BLK)
                z_s[rows, cols] = _dot(wsm[g], vvb_s[rows, cols]) + bs_ref[g]
        z = z_s[...]
        yab = (u * z).astype(BF16)
        obb = ob_ref[...].astype(BF16)
        ya = _dot_nt(yab, wa_ref[...])
        yb = _dot_nt(obb, wb_ref[...])
        sa = jax.nn.sigmoid(gta_ref[...])
        sb = jax.nn.sigmoid(gtb_ref[...])
        mb = (sa * ya + sb * yb).astype(BF16)
        dhv = dh_ref[...]
        mg_ref[0, 0:1, :] += jnp.sum(dhv * _dot(mb, wo_ref[...]), axis=0, keepdims=True)
        dmob = (m_ref[0, 5:6, :] * dhv).astype(BF16)
        dwo_ref[...] += _dot_tn(mb, dmob)
        dmerged = _dot_nt(dmob, wo_ref[...])
        dya = dmerged * sa
        dyb = dmerged * sb
        dgta_ref[...] = dya * ya * (1.0 - sa)
        dgtb_ref[...] = dyb * yb * (1.0 - sb)
        dyab = dya.astype(BF16)
        dybb = dyb.astype(BF16)
        dwa_ref[...] += _dot_tn(dyab, yab)
        dwb_ref[...] += _dot_tn(dybb, obb)
        dob_ref[...] = _dot(dybb, wb_ref[...])
        dyap = _dot(dyab, wa_ref[...])
        dau_ref[...] = (dyap * z) * _gelu_grad(auv, thu)
        dz = dyap * u
        dz_s[...] = dz
        dzb_s[...] = dz.astype(BF16)
        for c in range(tm // BLK):
            rows = slice(c * BLK, (c + 1) * BLK)
            for g in range(N_GRP):
                cols = slice(g * BLK, (g + 1) * BLK)
                dzb = dzb_s[rows, cols]
                dvv_s[rows, cols] = _dot_tn(wsm[g], dzb)
                dws_ref[g] += _dot_nt(dzb, vvb_s[rows, cols])
                dbs_ref[g] += dz_s[rows, cols]
        dvv = dvv_s[...]
        dln_ref[0:1, :] += jnp.sum(dvv * vhat, axis=0, keepdims=True)
        dln_ref[1:2, :] += jnp.sum(dvv, axis=0, keepdims=True)
        dvh = dvv * g_ln
        dt = rstd * (dvh - jnp.mean(dvh, axis=-1, keepdims=True)
                     - vhat * jnp.mean(dvh * vhat, axis=-1, keepdims=True))
        dav_ref[...] = dt * _gelu_grad(avv, thv)

    def tok(w):
        return pl.BlockSpec((tm, w), lambda i: (i, 0))

    def full(shape):
        return pl.BlockSpec(shape, lambda i: (0,) * len(shape))

    return _call(
        body, name="mix_bwd", grid=(T // tm,),
        in_specs=[tok(D_A), tok(D_A), tok(D), tok(D), tok(D_B), tok(D),
                  pl.BlockSpec((1, 9, D), lambda i: (i // tps, 0, 0)), full((1, D_A)), full((1, D_A)),
                  full((N_GRP, BLK, BLK)), full((N_GRP, BLK, BLK)), _vmem(), _vmem(), _vmem()],
        out_specs=[tok(D_A), tok(D_A), tok(D), tok(D), tok(D_B), full((D, D)), full((D, D_A)), full((D, D_B)),
                   full((N_GRP, BLK, BLK)), full((N_GRP, BLK, BLK)), full((8, D_A)),
                   pl.BlockSpec((1, 8, D), lambda i: (i // tps, 0, 0))],
        out_shape=[jax.ShapeDtypeStruct((T, D_A), F32), jax.ShapeDtypeStruct((T, D_A), F32),
                   jax.ShapeDtypeStruct((T, D), F32), jax.ShapeDtypeStruct((T, D), F32),
                   jax.ShapeDtypeStruct((T, D_B), F32), jax.ShapeDtypeStruct((D, D), F32),
                   jax.ShapeDtypeStruct((D, D_A), F32), jax.ShapeDtypeStruct((D, D_B), F32),
                   jax.ShapeDtypeStruct((N_GRP, BLK, BLK), F32), jax.ShapeDtypeStruct((N_GRP, BLK, BLK), F32),
                   jax.ShapeDtypeStruct((8, D_A), F32), jax.ShapeDtypeStruct((B, 8, D), F32)],
        scratch_shapes=[pltpu.VMEM((tm, D_A), BF16), pltpu.VMEM((tm, D_A), F32), pltpu.VMEM((tm, D_A), F32),
                        pltpu.VMEM((tm, D_A), BF16), pltpu.VMEM((tm, D_A), F32)],
        args=(au, av, gta, gtb, ob, dh, mods, g_ln, b_ln, ws, bsb, wa, wb, wout), semantics=("arbitrary",),
        plans=plans)


def _prologue(c_pad, w_ada, first_shards):
    cols = w_ada.shape[1]
    plan_w, plan_c = _GatherRelayed(first_shards), _GatherDirect([c_pad])
    plan_m = _Gather([jax.ShapeDtypeStruct((N_DEV * 8, cols), F32)])
    n_w = len(first_shards)

    def body(c_ref, wada_ref, *rest):
        w_ins, rest = rest[:n_w], rest[n_w:]
        call_ref, mods_ref = rest[:2]
        w_outs, rest = rest[2:2 + n_w], rest[2 + n_w:]
        cvm, part, local_sem = rest[:3]
        sems = rest[3:]
        sems_w, sems_c, sems_m = sems[:3], sems[3:6], sems[6:9]
        plan_c.start([c_ref], [call_ref], sems_c)
        plan_c.relay([c_ref], [call_ref], sems_c)
        plan_c.finish([c_ref], [call_ref], sems_c)
        load = pltpu.make_async_copy(call_ref, cvm, local_sem)
        load.start()
        load.wait()
        cv = cvm[...].reshape(N_DEV * 8, D)
        part[...] = _dot((cv * jax.nn.sigmoid(cv)).astype(BF16), wada_ref[...].astype(BF16))
        plan_m.start([part], [mods_ref], sems_m)
        plan_w.start(w_ins, w_outs, sems_w)
        plan_m.relay([part], [mods_ref], sems_m)
        plan_m.finish([part], [mods_ref], sems_m)
        plan_w.relay(w_ins, w_outs, sems_w)
        plan_w.finish(w_ins, w_outs, sems_w)

    res = pl.pallas_call(
        body, name="prologue", in_specs=[_any(), _vmem()] + [_any()] * n_w,
        out_specs=[_any()] * (2 + n_w), out_shape=plan_c.out_shape + plan_m.out_shape + plan_w.out_shape,
        scratch_shapes=[pltpu.VMEM((N_DEV, 8, D), F32), pltpu.VMEM((N_DEV * 8, cols), F32), pltpu.SemaphoreType.DMA]
        + plan_w.scratch + plan_c.scratch + plan_m.scratch,
        compiler_params=pltpu.CompilerParams(vmem_limit_bytes=VMEM_LIMIT),
    )(c_pad, w_ada, *first_shards)
    return res[0], res[1], list(res[2:])


def _ada_update(c_all, dm_cols, w, m, v, plans=()):
    n, cols = c_all.shape[0], w.shape[1]

    def body(c_ref, dm_ref, w_ref, m_ref, v_ref, g_ref, d_ref, nm_ref, nv_ref):
        cv = c_ref[...]
        g = _dot_tn((cv * jax.nn.sigmoid(cv)).astype(BF16), dm_ref[...].astype(BF16))
        g_ref[...] = g
        d_ref[...], nm_ref[...], nv_ref[...] = _adamw(w_ref[...], g, m_ref[...], v_ref[...])

    col = pl.BlockSpec((D, 128), lambda j: (0, j))
    return _call(
        body, name="ada_update", grid=(cols // 128,),
        in_specs=[pl.BlockSpec((n, D), lambda j: (0, 0)), pl.BlockSpec((n, 128), lambda j: (0, j)), col, col, col],
        out_specs=[col] * 4, out_shape=[jax.ShapeDtypeStruct(w.shape, F32)] * 4, args=(c_all, dm_cols, w, m, v),
        semantics=("parallel",), plans=plans)


def _mod_finish(mg1, mg2, mg2g, mg3, mods, g1, g2, g3):
    B = mods.shape[0]

    def body(mg1_ref, mg2_ref, mg2g_ref, mg3_ref, m_ref, g1_ref, g2_ref, g3_ref, dm_ref, dgn_ref):
        dgn_ref[...] = jnp.zeros(dgn_ref.shape, F32)
        for k, (mg, g_ref) in enumerate(((mg1_ref, g1_ref), (mg2_ref, g2_ref), (mg3_ref, g3_ref))):
            for b in range(B):
                s_dxr = mg[b, 1:2, :]
                dm_ref[b, 3 * k:3 * k + 1, :] = mg[b, 0:1, :]
                dm_ref[b, 3 * k + 1:3 * k + 2, :] = g_ref[...] * s_dxr
                dm_ref[b, 3 * k + 2:3 * k + 3, :] = mg2g_ref[b, 0:1, :] if k == 1 else mg[b, 2:3, :]
                dgn_ref[k:k + 1, :] += (1.0 + m_ref[b, 3 * k + 1:3 * k + 2, :]) * s_dxr

    args = (mg1, mg2, mg2g, mg3, mods, g1, g2, g3)
    out_shape = [jax.ShapeDtypeStruct((B, 9, D), F32), jax.ShapeDtypeStruct((8, D), F32)]
    return pl.pallas_call(body, name="mod_finish", grid=(1,), in_specs=[_whole(a) for a in args],
                          out_specs=[_whole(o) for o in out_shape], out_shape=out_shape,
                          compiler_params=_params("arbitrary"))(*args)


def _small_update(gathered, params, ms, vs):
    n = len(SMALL)
    B = gathered[0].shape[1]

    def body(*refs):
        gdm, ggn, gln, gws, gbs, ggq, ggk, gsk, gloss = refs[:9]
        w = dict(zip(SMALL, refs[9:9 + n]))
        m = dict(zip(SMALL, refs[9 + n:9 + 2 * n]))
        v = dict(zip(SMALL, refs[9 + 2 * n:9 + 3 * n]))
        outs = refs[9 + 3 * n:-1]
        out = {name: outs[4 * k:4 * k + 4] for k, name in enumerate(SMALL)}

        def total(ref, idx):
            acc = ref[(0,) + idx]
            for dev in range(1, N_DEV):
                acc = acc + ref[(dev,) + idx]
            return acc

        def finish(name, g, idx=(Ellipsis,)):
            d, nm, nv = _adamw(w[name][idx], g, m[name][idx], v[name][idx])
            for ref, val in zip(out[name], (g, d, nm, nv)):
                ref[idx] = val

        g_bada = total(gdm, (slice(0, 1),))
        for b in range(1, B):
            g_bada = g_bada + total(gdm, (slice(b, b + 1),))
        finish("b_ada", g_bada)
        finish("g_norm1", total(ggn, (slice(0, 1),)))
        finish("g_norm2", total(ggn, (slice(1, 2),)))
        finish("g_norm3", total(ggn, (slice(2, 3),)))
        finish("g_sgu_ln", total(gln, (slice(0, 1),)))
        finish("b_sgu_ln", total(gln, (slice(1, 2),)))
        tril = lax.broadcasted_iota(jnp.int32, (BLK, BLK), 0) >= lax.broadcasted_iota(jnp.int32, (BLK, BLK), 1)
        for g in range(N_GRP):
            finish("w_spatial", jnp.where(tril, total(gws, (g,)), 0.0), (0, g))
            finish("b_spatial", jnp.sum(total(gbs, (g,)).T, axis=0, keepdims=True), (0, slice(g, g + 1)))
        finish("g_q", total(ggq, ()))
        finish("g_k", total(ggk, ()))
        finish("attn_sinks", total(gsk, (slice(0, 1), slice(0, N_KV * Q_PER_KV))))
        refs[-1][...] = total(gloss, ())

    args = list(gathered) + [params[k] for k in SMALL] + [ms[k] for k in SMALL] + [vs[k] for k in SMALL]
    out_shape = []
    for k in SMALL:
        out_shape += [jax.ShapeDtypeStruct(params[k].shape, F32)] * 4
    out_shape.append(jax.ShapeDtypeStruct((8, 128), F32))
    res = pl.pallas_call(body, name="small_update", grid=(1,), in_specs=[_whole(a) for a in args],
                         out_specs=[_whole(o) for o in out_shape], out_shape=out_shape,
                         compiler_params=_params("arbitrary"))(*args)
    return {k: res[4 * i:4 * i + 4] for i, k in enumerate(SMALL)}, res[-1][0, 0]


def _rs_add(p, r, core, name):
    _, rows, width = p.shape

    def body(c_ref, p_ref, r_ref, o_ref):
        o_ref[...] = (p_ref[...] + r_ref[...]).astype(BF16)

    return pl.pallas_call(
        body, name="rs_add_" + name, out_shape=jax.ShapeDtypeStruct((4, rows, width), BF16),
        grid_spec=pltpu.PrefetchScalarGridSpec(
            num_scalar_prefetch=1, grid=(4,),
            in_specs=[pl.BlockSpec((1, rows, width), lambda k, c_ref: (2 * k + c_ref[0], 0, 0)),
                      pl.BlockSpec((1, rows, width), lambda k, c_ref: (k, 0, 0))],
            out_specs=pl.BlockSpec((1, rows, width), lambda k, c_ref: (k, 0, 0))),
        compiler_params=_params("parallel"),
    )(core, p, r)


def _adam_big(w, m, v, r, name):
    rows, cols = w.shape

    def body(r_ref, w_ref, m_ref, v_ref, go_ref, d_ref, nm_ref, nv_ref):
        g = r_ref[0].astype(F32)
        for k in range(1, 4):
            g = g + r_ref[k].astype(F32)
        go_ref[...] = g
        d_ref[...], nm_ref[...], nv_ref[...] = _adamw(w_ref[...], g, m_ref[...], v_ref[...])

    br = rows // 2
    shard = pl.BlockSpec((br, cols), lambda i: (i, 0))
    return pl.pallas_call(body, name="adam_" + name, grid=(2,),
                          in_specs=[pl.BlockSpec((4, br, cols), lambda i: (0, i, 0)), shard, shard, shard],
                          out_specs=[shard] * 4, out_shape=[jax.ShapeDtypeStruct(w.shape, F32)] * 4,
                          compiler_params=_params("parallel"))(r, w, m, v)


def kernel(x, c, w_ada, b_ada, g_norm1, ffn1_w_gate, ffn1_w_up, ffn1_w_down, g_norm2, w_in, g_sgu_ln, b_sgu_ln, w_spatial, b_spatial, g_q, g_k, attn_sinks, w_branch_a, w_branch_b, w_out, g_norm3, ffn2_w_gate, ffn2_w_up, ffn2_w_down, loss_target, m_w_ada, m_b_ada, m_g_norm1, m_ffn1_w_gate, m_ffn1_w_up, m_ffn1_w_down, m_g_norm2, m_w_in, m_g_sgu_ln, m_b_sgu_ln, m_w_spatial, m_b_spatial, m_g_q, m_g_k, m_attn_sinks, m_w_branch_a, m_w_branch_b, m_w_out, m_g_norm3, m_ffn2_w_gate, m_ffn2_w_up, m_ffn2_w_down, v_w_ada, v_b_ada, v_g_norm1, v_ffn1_w_gate, v_ffn1_w_up, v_ffn1_w_down, v_g_norm2, v_w_in, v_g_sgu_ln, v_b_sgu_ln, v_w_spatial, v_b_spatial, v_g_q, v_g_k, v_attn_sinks, v_w_branch_a, v_w_branch_b, v_w_out, v_g_norm3, v_ffn2_w_gate, v_ffn2_w_up, v_ffn2_w_down):
    names = ("w_ada", "b_ada", "g_norm1", "ffn1_w_gate", "ffn1_w_up", "ffn1_w_down", "g_norm2", "w_in", "g_sgu_ln",
             "b_sgu_ln", "w_spatial", "b_spatial", "g_q", "g_k", "attn_sinks", "w_branch_a", "w_branch_b", "w_out",
             "g_norm3", "ffn2_w_gate", "ffn2_w_up", "ffn2_w_down")
    w = dict(zip(names, (w_ada, b_ada, g_norm1, ffn1_w_gate, ffn1_w_up, ffn1_w_down, g_norm2, w_in, g_sgu_ln,
                         b_sgu_ln, w_spatial, b_spatial, g_q, g_k, attn_sinks, w_branch_a, w_branch_b, w_out, g_norm3,
                         ffn2_w_gate, ffn2_w_up, ffn2_w_down)))
    m = dict(zip(names, (m_w_ada, m_b_ada, m_g_norm1, m_ffn1_w_gate, m_ffn1_w_up, m_ffn1_w_down, m_g_norm2, m_w_in,
                         m_g_sgu_ln, m_b_sgu_ln, m_w_spatial, m_b_spatial, m_g_q, m_g_k, m_attn_sinks, m_w_branch_a,
                         m_w_branch_b, m_w_out, m_g_norm3, m_ffn2_w_gate, m_ffn2_w_up, m_ffn2_w_down)))
    v = dict(zip(names, (v_w_ada, v_b_ada, v_g_norm1, v_ffn1_w_gate, v_ffn1_w_up, v_ffn1_w_down, v_g_norm2, v_w_in,
                         v_g_sgu_ln, v_b_sgu_ln, v_w_spatial, v_b_spatial, v_g_q, v_g_k, v_attn_sinks, v_w_branch_a,
                         v_w_branch_b, v_w_out, v_g_norm3, v_ffn2_w_gate, v_ffn2_w_up, v_ffn2_w_down)))
    B, seq, _ = x.shape
    T = B * seq
    nbs = seq // BLK
    xi, yi, ci = _place()
    me = 4 * xi + 2 * yi + ci
    core = jnp.reshape(ci, (1,)).astype(jnp.int32)
    layout = {name: (tform, n, width) for name, tform, n, width in BIG}
    shard = {name: w[name][0].astype(BF16).T if tform else w[name][0].astype(BF16) for name, tform, _, _ in BIG}
    wts, parts, big_out = {}, {}, {}

    def gather_plan(group):
        return _Gather([shard[k] for k in group])

    def take(group, gathered):
        for k, g in zip(group, gathered):
            wts[k] = g.reshape(N_DEV * layout[k][1], layout[k][2])

    def blocks(name, grad):
        return grad.reshape(N_DEV, layout[name][1], layout[name][2])

    def to_sibling(names, grads):
        return _RsSibling([blocks(k, g) for k, g in zip(names, grads)])

    def add(names, grads, from_sibling):
        for k, g, r in zip(names, grads, from_sibling):
            parts[k] = _rs_add(blocks(k, g), r, core, k)

    def to_chips(names):
        return _RsChips([parts[k] for k in names])

    def update(names, from_chips):
        for k, r in zip(names, from_chips):
            if layout[k][0]:
                big_out[k] = [o.T[None] for o in _adam_big(w[k][0].T, m[k][0].T, v[k][0].T, r, k)]
            else:
                big_out[k] = [o[None] for o in _adam_big(w[k][0], m[k][0], v[k][0], r, k)]

    g1, g2, g3, g_ln, b_ln = g_norm1, g_norm2, g_norm3, g_sgu_ln, b_sgu_ln
    gq2, gk2 = jnp.tile(g_q, (1, 2)), jnp.tile(g_k, (1, 2))
    ws = w_spatial[0]
    bsb = jnp.broadcast_to(b_spatial[0][:, :, None], (N_GRP, BLK, BLK))
    xf = x.reshape(T, D)
    tgt = loss_target.reshape(T, D)

    ffn1 = ("ffn1_w_gate", "ffn1_w_up", "ffn1_w_down")
    ffn2 = ("ffn2_w_gate", "ffn2_w_up", "ffn2_w_down")
    c_pad = jnp.concatenate([c, jnp.zeros((8 - B, D), F32)], axis=0)
    c_all, mods_cols, gathered = _prologue(c_pad, w_ada[0], [shard[k] for k in ffn1[:2]])
    take(ffn1[:2], gathered)
    c_all = c_all[:, :B].reshape(N_DEV * B, D)
    mine = lax.dynamic_slice_in_dim(mods_cols, 8 * me, B, axis=1)
    mods = (mine.transpose(1, 0, 2).reshape(B, 9 * D) + b_ada).reshape(B, 9, D)

    group = ("ffn1_w_down", "w_branch_a", "w_branch_b", "w_out")
    (gb1, ub1), (gathered,) = _ffn_gate_up(xf, mods, g1, wts["ffn1_w_gate"], wts["ffn1_w_up"], 0, seq,
                                           plans=[gather_plan(group)])
    take(group, gathered)
    group = ("w_in",)
    (h1, y1), (gathered,) = _ffn_down(xf, gb1, ub1, mods, wts["ffn1_w_down"], 0, seq, plans=[gather_plan(group)])
    take(group, gathered)
    group = ("ffn2_w_gate",)
    (au, av, q_tok, k_tok, v_tok, gta, gtb), (gathered,) = _inproj_fwd(h1, mods, g2, wts["w_in"], seq,
                                                                       plans=[gather_plan(group)])
    take(group, gathered)
    group = ("ffn2_w_up",)
    ob, (gathered,) = _swa_fwd(q_tok, k_tok, v_tok, gq2, gk2, attn_sinks, nbs, plans=[gather_plan(group)])
    take(group, gathered)
    mixw = (wts["w_branch_a"], wts["w_branch_b"], wts["w_out"])
    group = ("ffn2_w_down",)
    h2, (gathered,) = _mix_fwd(au, av, gta, gtb, ob, h1, mods, g_ln, b_ln, ws, bsb, *mixw, seq,
                               plans=[gather_plan(group)])
    take(group, gathered)
    dh3, y3, gb3, ub3, loss_parts = _ffn_fwd_loss(h2, mods, g3, *[wts[k] for k in ffn2], 6, seq, tgt)
    loss_part = jnp.full((8, 128), jnp.sum(loss_parts[:, 0, 0]))

    (dh2, xb, dyb, a, dg, du, mg3), _ = _ffn_bwd(h2, dh3, y3, gb3, ub3, mods, g3, *[wts[k] for k in ffn2], 6, seq)
    d_gate, _ = _wgrad(dg, xb, "wgrad_ffn2_gate")
    d_up, (r,) = _wgrad(du, xb, "wgrad_ffn2_up", plans=[to_sibling(ffn2[:1], [d_gate])])
    add(ffn2[:1], [d_gate], r)
    d_down, (r,) = _wgrad(a, dyb, "wgrad_ffn2_down", plans=[to_sibling(ffn2[1:2], [d_up])])
    add(ffn2[1:2], [d_up], r)

    (dau, dav, dgta, dgtb, dob, d_out, d_a, d_b, dws, dbs, dln, mg2g), (r, from_chips) = _mix_bwd(
        au, av, gta, gtb, ob, dh2, mods, g_ln, b_ln, ws, bsb, *mixw, seq,
        plans=[to_sibling(ffn2[2:], [d_down]), to_chips(ffn2[:2])])
    add(ffn2[2:], [d_down], r)
    update(ffn2[:2], from_chips)
    mixers = ("w_out", "w_branch_a", "w_branch_b")
    (dq, dk, dv, dk_halo, dv_halo, dgq2, dgk2, dsk), (from_chips, r) = _swa_bwd(
        q_tok, k_tok, v_tok, gq2, gk2, attn_sinks, dob, nbs,
        plans=[to_chips(ffn2[2:]), to_sibling(mixers, [d_out, d_a, d_b])])
    update(ffn2[2:], from_chips)
    add(mixers, [d_out, d_a, d_b], r)
    dk, dv = _swa_add_halo(dk, dk_halo), _swa_add_halo(dv, dv_halo)
    dgq, dgk = dgq2[:, :HD] + dgq2[:, HD:], dgk2[:, :HD] + dgk2[:, HD:]
    (dh1, xb2, dpb, mg2), (from_chips, early) = _inproj_bwd(
        h1, dh2, (dau, dav, dq, dk, dv, dgta, dgtb), mods, g2, wts["w_in"], seq,
        plans=[to_chips(mixers), _Gather([dln, dws, dbs, dgq, dgk, dsk, loss_part])])
    update(mixers, from_chips)

    d_in, _ = _wgrad(dpb, xb2, "wgrad_w_in")

    (dx, xb, dyb, a, dg, du, mg1), _ = _ffn_bwd(xf, dh1, y1, gb1, ub1, mods, g1, *[wts[k] for k in ffn1], 0, seq)
    dmods, dgn = _mod_finish(mg1, mg2, mg2g, mg3, mods, g1, g2, g3)
    d_gate, (r, late) = _wgrad(dg, xb, "wgrad_ffn1_gate",
                               plans=[to_sibling(("w_in",), [d_in]), _GatherDirect([dmods.reshape(B, 9 * D), dgn])])
    add(("w_in",), [d_in], r)
    gathered = late + early
    d_up, (r, from_chips) = _wgrad(du, xb, "wgrad_ffn1_up",
                                   plans=[to_sibling(ffn1[:1], [d_gate]), to_chips(("w_in",))])
    add(ffn1[:1], [d_gate], r)
    update(("w_in",), from_chips)
    d_down, (r, from_chips) = _wgrad(a, dyb, "wgrad_ffn1_down",
                                     plans=[to_sibling(ffn1[1:2], [d_up]), to_chips(ffn1[:1])])
    add(ffn1[1:2], [d_up], r)
    update(ffn1[:1], from_chips)
    small_out, loss = _small_update(gathered, w, m, v)
    dm_cols = lax.dynamic_slice_in_dim(gathered[0].reshape(N_DEV * B, 9 * D), (9 * D // N_DEV) * me,
                                       9 * D // N_DEV, axis=1)
    ada_out, (r, from_chips) = _ada_update(c_all, dm_cols, w_ada[0], m_w_ada[0], v_w_ada[0],
                                           plans=[to_sibling(ffn1[2:], [d_down]), to_chips(ffn1[1:2])])
    add(ffn1[2:], [d_down], r)
    update(ffn1[1:2], from_chips)
    (from_chips,) = _exchange([to_chips(ffn1[2:])], "rs_last")
    update(ffn1[2:], from_chips)

    def leaf(kind, name):
        if name == "w_ada":
            return ada_out[kind][None]
        if name in SMALL:
            return small_out[name][kind]
        return big_out[name][kind]

    return (loss, dx.reshape(B, seq, D), *[leaf(kind, name) for kind in range(4) for name in names])
```

```python
import math

import jax
import jax.numpy as jnp
from jax import lax
from jax.experimental import pallas as pl
from jax.experimental.pallas import tpu as pltpu

F32 = jnp.float32
BF16 = jnp.bfloat16
MESH = pl.DeviceIdType.MESH
N_DEV = 8

VMEM_LIMIT = 56 * 1024 * 1024

D = 1024
FF = 2816
FC = 2816
D_A = 512
D_B = 512
HD = 64
N_KV = 2
Q_PER_KV = 4
BLK = 128
N_GRP = 4
IN_COLS = 3840
PIECES = (("au", 0, 512), ("av", 512, 512), ("q", 1024, 512), ("k", 1536, 128), ("v", 1664, 128),
          ("ga", 1792, 1024), ("gb", 2816, 1024))
EPS = 1e-6
NEG = -1e30
GELU_C = math.sqrt(2.0 / math.pi)

ADAM_LR = 0.001
ADAM_B1 = 0.9
ADAM_B2 = 0.999
ADAM_EPS = 1e-08
ADAM_WD = 0.01
ADAM_STEP = 10

NT = (((1,), (1,)), ((), ()))
TN = (((0,), (0,)), ((), ()))

BIG = (("ffn1_w_gate", True, FF // N_DEV, D), ("ffn1_w_up", True, FF // N_DEV, D),
       ("ffn1_w_down", False, FF // N_DEV, D), ("w_in", True, IN_COLS // N_DEV, D),
       ("w_branch_a", True, D // N_DEV, D_A), ("w_branch_b", True, D // N_DEV, D_B), ("w_out", False, D // N_DEV, D),
       ("ffn2_w_gate", True, FF // N_DEV, D), ("ffn2_w_up", True, FF // N_DEV, D),
       ("ffn2_w_down", False, FF // N_DEV, D))
SMALL = ("b_ada", "g_norm1", "g_norm2", "g_sgu_ln", "b_sgu_ln", "w_spatial", "b_spatial", "g_q", "g_k",
         "attn_sinks", "g_norm3")


def _dot(a, b):
    return jnp.dot(a, b, preferred_element_type=F32)


def _dot_nt(a, b):
    return lax.dot_general(a, b, NT, preferred_element_type=F32)


def _dot_tn(a, b):
    return lax.dot_general(a, b, TN, preferred_element_type=F32)


def _vmem():
    return pl.BlockSpec(memory_space=pltpu.VMEM)


def _any():
    return pl.BlockSpec(memory_space=pl.ANY)


def _whole(a):
    return pl.BlockSpec(a.shape, lambda i: (0,) * len(a.shape))


def _rms_mod(h, g, sh, sc):
    inv = lax.rsqrt(jnp.mean(h * h, axis=-1, keepdims=True) + EPS)
    r = h * inv
    return (r * g) * (1.0 + sc) + sh, r, inv


def _rms_mod_bwd(dxn, r, inv, g, sc):
    dr = dxn * (g * (1.0 + sc))
    dh = inv * (dr - r * jnp.mean(dr * r, axis=-1, keepdims=True))
    return dh, jnp.sum(dxn, axis=0, keepdims=True), jnp.sum(dxn * r, axis=0, keepdims=True)


def _gelu(x):
    t = jnp.tanh(GELU_C * (x + 0.044715 * (x * x * x)))
    return 0.5 * x * (1.0 + t), t


def _gelu_grad(x, t):
    return 0.5 * (1.0 + t) + 0.5 * x * (1.0 - t * t) * (GELU_C * (1.0 + 3.0 * 0.044715 * x * x))


def _adamw(w, g, m, v):
    m = ADAM_B1 * m + (1.0 - ADAM_B1) * g
    v = ADAM_B2 * v + (1.0 - ADAM_B2) * (g * g)
    m_hat = m / (1.0 - ADAM_B1 ** ADAM_STEP)
    v_hat = v / (1.0 - ADAM_B2 ** ADAM_STEP)
    delta = -ADAM_LR * (m_hat / (jnp.sqrt(v_hat) + ADAM_EPS) + ADAM_WD * w)
    return delta, m, v


def _token_tile(seq, cap=512):
    return min(cap, seq)


def _params(*semantics):
    return pltpu.CompilerParams(dimension_semantics=semantics, vmem_limit_bytes=VMEM_LIMIT)


def _place():
    return lax.axis_index("x"), lax.axis_index("y"), lax.axis_index("c")


class _Gather:
    def __init__(self, arrays):
        n = len(arrays)
        self.ins = list(arrays)
        self.out_shape = [jax.ShapeDtypeStruct((N_DEV,) + a.shape, a.dtype) for a in arrays]
        self.scratch = [pltpu.SemaphoreType.DMA((n, 7)), pltpu.SemaphoreType.DMA((n, 7)),
                        pltpu.SemaphoreType.DMA((n,))]

    def _copies(self, ins, outs, sems):
        send_sems, recv_sems, local_sems = sems
        n = len(ins)
        x, y, c = _place()
        me, sibling = (x, y, c), (x, y, 1 - c)
        chips = [(1 - x, y), (x, 1 - y), (1 - x, 1 - y)]

        def slot(a, px, py, pc):
            return outs[a].at[4 * px + 2 * py + pc]

        def copy(a, k, block, to, src=None):
            return pltpu.make_async_remote_copy(
                src_ref=slot(a, *block) if src is None else src, dst_ref=slot(a, *block),
                send_sem=send_sems.at[a, k], recv_sem=recv_sems.at[a, k], device_id=to, device_id_type=MESH)

        mine = [pltpu.make_async_copy(ins[a], slot(a, *me), local_sems.at[a]) for a in range(n)]
        first = [copy(a, 0, me, sibling, src=ins[a]) for a in range(n)]
        first += [copy(a, 1 + j, me, (*chip, c), src=ins[a]) for a in range(n) for j, chip in enumerate(chips)]
        landed = [[copy(a, 1 + j, (*chip, c), me) for a in range(n)] for j, chip in enumerate(chips)]
        passed = [[copy(a, 4 + j, (*chip, c), sibling) for a in range(n)] for j, chip in enumerate(chips)]
        from_sibling = [copy(a, 0, sibling, me) for a in range(n)]
        from_sibling += [copy(a, 4 + j, (*chip, 1 - c), me) for a in range(n) for j, chip in enumerate(chips)]
        return mine, first, landed, passed, from_sibling

    def start(self, ins, outs, sems):
        mine, first, _, _, _ = self._copies(ins, outs, sems)
        for cp in mine + first:
            cp.start()

    def relay(self, ins, outs, sems):
        _, _, landed, passed, _ = self._copies(ins, outs, sems)
        for arrivals, forwards in zip(landed, passed):
            for arrival, forward in zip(arrivals, forwards):
                arrival.wait_recv()
                forward.start()

    def finish(self, ins, outs, sems):
        mine, first, _, passed, from_sibling = self._copies(ins, outs, sems)
        for cp in from_sibling:
            cp.wait_recv()
        for cp in first + [f for fs in passed for f in fs]:
            cp.wait_send()
        for cp in mine:
            cp.wait()


class _GatherRelayed:
    def __init__(self, arrays):
        n = len(arrays)
        self.ins = list(arrays)
        self.out_shape = [jax.ShapeDtypeStruct((N_DEV,) + a.shape, a.dtype) for a in arrays]
        self.scratch = [pltpu.SemaphoreType.DMA((n, 7)), pltpu.SemaphoreType.DMA((n, 7)),
                        pltpu.SemaphoreType.DMA((n,))]

    def _copies(self, ins, outs, sems):
        send_sems, recv_sems, local_sems = sems
        n = len(ins)
        x, y, c = _place()
        me, sibling = (x, y, c), (x, y, 1 - c)
        south = c == 0
        via = (jnp.where(south, 1 - x, x), jnp.where(south, y, 1 - y))
        onward = (jnp.where(south, x, 1 - x), jnp.where(south, 1 - y, y))
        via_sem = jnp.where(south, 1, 2)

        def slot(a, px, py, pc):
            return outs[a].at[4 * px + 2 * py + pc]

        def copy(a, k, block, to, src=None):
            return pltpu.make_async_remote_copy(
                src_ref=slot(a, *block) if src is None else src, dst_ref=slot(a, *block),
                send_sem=send_sems.at[a, k], recv_sem=recv_sems.at[a, k], device_id=to, device_id_type=MESH)

        chips = [(1 - x, y), (x, 1 - y), (1 - x, 1 - y)]
        mine = [pltpu.make_async_copy(ins[a], slot(a, *me), local_sems.at[a]) for a in range(n)]
        first = [copy(a, 0, me, sibling, src=ins[a]) for a in range(n)]
        first += [copy(a, 1 + j, me, (*chips[j], c), src=ins[a]) for a in range(n) for j in range(2)]
        to_pass = [copy(a, via_sem, (*via, c), me) for a in range(n)]
        passed_on = [copy(a, 3, (*via, c), (*onward, c)) for a in range(n)]
        landed = [[copy(a, 1 + j, (*chips[j], c), me) for a in range(n)] for j in range(3)]
        to_sibling = [[copy(a, 4 + j, (*chips[j], c), sibling) for a in range(n)] for j in range(3)]
        from_sibling = [copy(a, 0, sibling, me) for a in range(n)]
        from_sibling += [copy(a, 4 + j, (*chips[j], 1 - c), me) for a in range(n) for j in range(3)]
        return mine, first, to_pass, passed_on, landed, to_sibling, from_sibling

    def start(self, ins, outs, sems):
        mine, first, _, _, _, _, _ = self._copies(ins, outs, sems)
        for cp in mine + first:
            cp.start()

    def relay(self, ins, outs, sems):
        _, _, to_pass, passed_on, landed, to_sibling, _ = self._copies(ins, outs, sems)
        for arrival, onward in zip(to_pass, passed_on):
            arrival.wait_recv()
            onward.start()
        x, y, c = _place()
        for j in range(3):
            for a, (arrival, forward) in enumerate(zip(landed[j], to_sibling[j])):
                if j < 2:
                    pl.when((c == 0) != (j == 0))(arrival.wait_recv)
                else:
                    arrival.wait_recv()
                forward.start()

    def finish(self, ins, outs, sems):
        mine, first, _, passed_on, _, to_sibling, from_sibling = self._copies(ins, outs, sems)
        for cp in from_sibling:
            cp.wait_recv()
        for cp in first + passed_on + [f for fs in to_sibling for f in fs]:
            cp.wait_send()
        for cp in mine:
            cp.wait()


class _GatherDirect:
    def __init__(self, arrays):
        n = len(arrays)
        self.ins = list(arrays)
        self.out_shape = [jax.ShapeDtypeStruct((N_DEV,) + a.shape, a.dtype) for a in arrays]
        self.scratch = [pltpu.SemaphoreType.DMA((n, 7)), pltpu.SemaphoreType.DMA((n, 7)),
                        pltpu.SemaphoreType.DMA((n,))]

    def _copies(self, ins, outs, sems):
        send_sems, recv_sems, local_sems = sems
        n = len(ins)
        x, y, c = _place()
        peers = [(x ^ (k >> 2 & 1), y ^ (k >> 1 & 1), c ^ (k & 1)) for k in range(1, N_DEV)]

        def slot(a, px, py, pc):
            return outs[a].at[4 * px + 2 * py + pc]

        def copy(a, k, block, to, src=None):
            return pltpu.make_async_remote_copy(
                src_ref=slot(a, *block) if src is None else src, dst_ref=slot(a, *block),
                send_sem=send_sems.at[a, k], recv_sem=recv_sems.at[a, k], device_id=to, device_id_type=MESH)

        mine = [pltpu.make_async_copy(ins[a], slot(a, x, y, c), local_sems.at[a]) for a in range(n)]
        sends = [copy(a, k, (x, y, c), peer, src=ins[a]) for a in range(n) for k, peer in enumerate(peers)]
        arrivals = [copy(a, k, peer, (x, y, c)) for a in range(n) for k, peer in enumerate(peers)]
        return mine, sends, arrivals

    def start(self, ins, outs, sems):
        mine, sends, _ = self._copies(ins, outs, sems)
        for cp in mine + sends:
            cp.start()

    def relay(self, ins, outs, sems):
        pass

    def finish(self, ins, outs, sems):
        mine, sends, arrivals = self._copies(ins, outs, sems)
        for cp in arrivals:
            cp.wait_recv()
        for cp in sends:
            cp.wait_send()
        for cp in mine:
            cp.wait()


class _RsSibling:
    def __init__(self, ps):
        n = len(ps)
        self.ins = list(ps)
        self.out_shape = [jax.ShapeDtypeStruct((4,) + p.shape[1:], p.dtype) for p in ps]
        self.scratch = [pltpu.SemaphoreType.DMA((n, 4)), pltpu.SemaphoreType.DMA((n, 4))]

    def _copies(self, ins, outs, sems):
        send_sems, recv_sems = sems
        x, y, c = _place()
        return [pltpu.make_async_remote_copy(
            src_ref=ins[a].at[2 * q + (1 - c)], dst_ref=outs[a].at[q], send_sem=send_sems.at[a, q],
            recv_sem=recv_sems.at[a, q], device_id=(x, y, 1 - c), device_id_type=MESH)
            for a in range(len(ins)) for q in range(4)]

    def start(self, ins, outs, sems):
        for cp in self._copies(ins, outs, sems):
            cp.start()

    def relay(self, ins, outs, sems):
        pass

    def finish(self, ins, outs, sems):
        for cp in self._copies(ins, outs, sems):
            cp.wait()


class _RsChips:
    def __init__(self, qs):
        n = len(qs)
        self.ins = list(qs)
        self.out_shape = [jax.ShapeDtypeStruct(q.shape, q.dtype) for q in qs]
        self.scratch = [pltpu.SemaphoreType.DMA((n, 3)), pltpu.SemaphoreType.DMA((n, 3)),
                        pltpu.SemaphoreType.DMA((n,))]

    def _copies(self, ins, outs, sems):
        send_sems, recv_sems, local_sems = sems
        n = len(ins)
        x, y, c = _place()
        my_chip = 2 * x + y
        chips = [(1 - x, y), (x, 1 - y), (1 - x, 1 - y)]

        def copy(a, j, src_slot, dst_slot):
            px, py = chips[j]
            return pltpu.make_async_remote_copy(
                src_ref=ins[a].at[src_slot], dst_ref=outs[a].at[dst_slot], send_sem=send_sems.at[a, j],
                recv_sem=recv_sems.at[a, j], device_id=(px, py, c), device_id_type=MESH)

        own = [pltpu.make_async_copy(ins[a].at[my_chip], outs[a].at[my_chip], local_sems.at[a]) for a in range(n)]
        sends = [copy(a, j, 2 * px + py, my_chip) for a in range(n) for j, (px, py) in enumerate(chips)]
        arrivals = [copy(a, j, my_chip, 2 * px + py) for a in range(n) for j, (px, py) in enumerate(chips)]
        return own, sends, arrivals

    def start(self, ins, outs, sems):
        own, sends, _ = self._copies(ins, outs, sems)
        for cp in own + sends:
            cp.start()

    def relay(self, ins, outs, sems):
        pass

    def finish(self, ins, outs, sems):
        own, sends, arrivals = self._copies(ins, outs, sems)
        for cp in arrivals:
            cp.wait_recv()
        for cp in sends:
            cp.wait_send()
        for cp in own:
            cp.wait()


def _split_plans(plans, refs_in, refs_out, refs_scr, phase):
    i = o = s = 0
    for p in plans:
        ni, no, ns = len(p.ins), len(p.out_shape), len(p.scratch)
        getattr(p, phase)(refs_in[i:i + ni], refs_out[o:o + no], refs_scr[s:s + ns])
        i, o, s = i + ni, o + no, s + ns


def _plan_results(plans, res):
    out = []
    for p in plans:
        out.append(list(res[:len(p.out_shape)]))
        res = res[len(p.out_shape):]
    return out


def _exchange(plans, name):
    c_in = [a for p in plans for a in p.ins]
    c_out = [s for p in plans for s in p.out_shape]
    c_scr = [s for p in plans for s in p.scratch]

    def body(*refs):
        cin, cout, cscr = refs[:len(c_in)], refs[len(c_in):len(c_in) + len(c_out)], refs[len(c_in) + len(c_out):]
        for phase in ("start", "relay", "finish"):
            _split_plans(plans, cin, cout, cscr, phase)

    res = pl.pallas_call(body, name=name, in_specs=[_any()] * len(c_in), out_specs=[_any()] * len(c_out),
                         out_shape=c_out, scratch_shapes=c_scr)(*c_in)
    return _plan_results(plans, res)


def _call(body, *, name, grid, in_specs, out_specs, out_shape, args, semantics, scratch_shapes=(), plans=()):
    n_in, n_out, n_scr = len(in_specs), len(out_specs), len(scratch_shapes)
    c_in = [a for p in plans for a in p.ins]
    c_out = [s for p in plans for s in p.out_shape]
    c_scr = [s for p in plans for s in p.scratch]
    n_steps = math.prod(grid)

    def wrapped(*refs):
        ins, refs = refs[:n_in], refs[n_in:]
        cin, refs = refs[:len(c_in)], refs[len(c_in):]
        outs, refs = refs[:n_out], refs[n_out:]
        cout, refs = refs[:len(c_out)], refs[len(c_out):]
        scr, cscr = refs[:n_scr], refs[n_scr:]
        if plans:
            step = 0
            for d, g in enumerate(grid):
                step = step * g + pl.program_id(d)
            pl.when(step == 0)(lambda: _split_plans(plans, cin, cout, cscr, "start"))
        body(*ins, *outs, *scr)
        if plans:
            pl.when(step == max(n_steps - 2, 0))(lambda: _split_plans(plans, cin, cout, cscr, "relay"))
            pl.when(step == n_steps - 1)(lambda: _split_plans(plans, cin, cout, cscr, "finish"))

    res = pl.pallas_call(
        wrapped, name=name, grid=grid, in_specs=list(in_specs) + [_any()] * len(c_in),
        out_specs=list(out_specs) + [_any()] * len(c_out), out_shape=list(out_shape) + c_out,
        scratch_shapes=list(scratch_shapes) + c_scr,
        compiler_params=_params(*(("arbitrary",) * len(grid) if plans else semantics)),
    )(*args, *c_in)
    return list(res[:n_out]), _plan_results(plans, res[n_out:])


def _ffn_fwd_loss(h, mods, gn, wg, wu, wd, row0, seq, tgt):
    T = h.shape[0]
    tm = _token_tile(seq, 256)
    tps = seq // tm
    n_t = T // tm

    def body(h_ref, m_ref, g_ref, wg_ref, wu_ref, wd_ref, tgt_ref, out_ref, y_ref, gb_ref, ub_ref, loss_ref):
        hv = h_ref[...]
        sh = m_ref[0, row0:row0 + 1, :]
        sc = m_ref[0, row0 + 1:row0 + 2, :]
        ga = m_ref[0, row0 + 2:row0 + 3, :]
        xn, _, _ = _rms_mod(hv, g_ref[...], sh, sc)
        xb = xn.astype(BF16)
        acc = jnp.zeros((tm, D), F32)
        for c0 in range(0, FF, FC):
            gg = _dot_nt(xb, wg_ref[c0:c0 + FC, :])
            uu = _dot_nt(xb, wu_ref[c0:c0 + FC, :])
            gg, uu = gg.astype(BF16), uu.astype(BF16)
            gb_ref[:, c0:c0 + FC] = gg
            ub_ref[:, c0:c0 + FC] = uu
            acc = acc + _dot((gg * jax.nn.sigmoid(gg)) * uu, wd_ref[c0:c0 + FC, :])
        y_ref[...] = acc
        d = hv + (0.5 * ga) * acc - tgt_ref[...]
        out_ref[...] = d * (1.0 / D)
        loss_ref[...] = jnp.full((1, 8, 128), 0.5 / D, F32) * jnp.sum(d * d)

    tok = pl.BlockSpec((tm, D), lambda i: (i, 0))
    tokf = pl.BlockSpec((tm, FF), lambda i: (i, 0))
    outs, _ = _call(
        body, name="ffn_fwd_loss", grid=(n_t,),
        in_specs=[tok, pl.BlockSpec((1, 9, D), lambda i: (i // tps, 0, 0)), pl.BlockSpec((1, D), lambda i: (0, 0)),
                  _vmem(), _vmem(), _vmem(), tok],
        out_specs=[tok, tok, tokf, tokf, pl.BlockSpec((1, 8, 128), lambda i: (i, 0, 0))],
        out_shape=[jax.ShapeDtypeStruct((T, D), F32), jax.ShapeDtypeStruct((T, D), F32),
                   jax.ShapeDtypeStruct((T, FF), BF16), jax.ShapeDtypeStruct((T, FF), BF16),
                   jax.ShapeDtypeStruct((n_t, 8, 128), F32)],
        args=(h, mods, gn, wg, wu, wd, tgt), semantics=("parallel",))
    return outs


def _ffn_gate_up(h, mods, gn, wg, wu, row0, seq, plans=()):
    T = h.shape[0]
    tm = _token_tile(seq, 512)
    tps = seq // tm

    def body(h_ref, m_ref, g_ref, wg_ref, wu_ref, gb_ref, ub_ref):
        xn, _, _ = _rms_mod(h_ref[...], g_ref[...], m_ref[0, row0:row0 + 1, :], m_ref[0, row0 + 1:row0 + 2, :])
        xb = xn.astype(BF16)
        for c0 in range(0, FF, FC):
            gb_ref[:, c0:c0 + FC] = _dot_nt(xb, wg_ref[c0:c0 + FC, :]).astype(BF16)
            ub_ref[:, c0:c0 + FC] = _dot_nt(xb, wu_ref[c0:c0 + FC, :]).astype(BF16)

    tokf = pl.BlockSpec((tm, FF), lambda i: (i, 0))
    return _call(
        body, name="ffn_gate_up", grid=(T // tm,),
        in_specs=[pl.BlockSpec((tm, D), lambda i: (i, 0)), pl.BlockSpec((1, 9, D), lambda i: (i // tps, 0, 0)),
                  pl.BlockSpec((1, D), lambda i: (0, 0)), _vmem(), _vmem()],
        out_specs=[tokf, tokf], out_shape=[jax.ShapeDtypeStruct((T, FF), BF16)] * 2, args=(h, mods, gn, wg, wu),
        semantics=("parallel",), plans=plans)


def _ffn_down(h, gb, ub, mods, wd, row0, seq, plans=()):
    T = h.shape[0]
    tm = _token_tile(seq, 512)
    tps = seq // tm

    def body(h_ref, gb_ref, ub_ref, m_ref, wd_ref, out_ref, y_ref):
        acc = jnp.zeros((tm, D), F32)
        for c0 in range(0, FF, FC):
            gg = gb_ref[:, c0:c0 + FC]
            acc = acc + _dot((gg * jax.nn.sigmoid(gg)) * ub_ref[:, c0:c0 + FC], wd_ref[c0:c0 + FC, :])
        y_ref[...] = acc
        out_ref[...] = h_ref[...] + (0.5 * m_ref[0, row0 + 2:row0 + 3, :]) * acc

    tok = pl.BlockSpec((tm, D), lambda i: (i, 0))
    tokf = pl.BlockSpec((tm, FF), lambda i: (i, 0))
    return _call(
        body, name="ffn_down", grid=(T // tm,),
        in_specs=[tok, tokf, tokf, pl.BlockSpec((1, 9, D), lambda i: (i // tps, 0, 0)), _vmem()],
        out_specs=[tok, tok], out_shape=[jax.ShapeDtypeStruct((T, D), F32)] * 2, args=(h, gb, ub, mods, wd),
        semantics=("parallel",), plans=plans)


def _ffn_bwd(h, dhn, y, gb, ub, mods, gn, wg, wu, wd, row0, seq, plans=()):
    T = h.shape[0]
    B = T // seq
    tm = _token_tile(seq, 256)
    tps = seq // tm
    n_t = T // tm

    def body(h_ref, dhn_ref, y_ref, gb_ref, ub_ref, m_ref, g_ref, wg_ref, wu_ref, wd_ref,
             dh_ref, xb_ref, dyb_ref, a_ref, dg_ref, du_ref, mg_ref):
        i = pl.program_id(0)
        hv = h_ref[...]
        dhn = dhn_ref[...]
        sh = m_ref[0, row0:row0 + 1, :]
        sc = m_ref[0, row0 + 1:row0 + 2, :]
        ga = m_ref[0, row0 + 2:row0 + 3, :]
        g = g_ref[...]
        xn, r, inv = _rms_mod(hv, g, sh, sc)
        xb_ref[...] = xn.astype(BF16)
        dyb = ((0.5 * ga) * dhn).astype(BF16)
        dyb_ref[...] = dyb
        dga = 0.5 * jnp.sum(dhn * y_ref[...], axis=0, keepdims=True)
        dxn = jnp.zeros((tm, D), F32)
        for c0 in range(0, FF, FC):
            wgc = wg_ref[c0:c0 + FC, :]
            wuc = wu_ref[c0:c0 + FC, :]
            gg = gb_ref[:, c0:c0 + FC]
            uu = ub_ref[:, c0:c0 + FC]
            sig = jax.nn.sigmoid(gg)
            s = gg * sig
            a_ref[:, c0:c0 + FC] = s * uu
            da = _dot_nt(dyb, wd_ref[c0:c0 + FC, :]).astype(BF16)
            dub = da * s
            dgb = (da * uu) * (sig * (1.0 + gg * (1.0 - sig)))
            dg_ref[:, c0:c0 + FC] = dgb
            du_ref[:, c0:c0 + FC] = dub
            dxn = dxn + _dot(dgb, wgc) + _dot(dub, wuc)
        dh, s_dxn, s_dxr = _rms_mod_bwd(dxn, r, inv, g, sc)
        dh_ref[...] = dhn + dh

        @pl.when(i % tps == 0)
        def _():
            mg_ref[...] = jnp.zeros(mg_ref.shape, F32)

        mg_ref[0, 0:1, :] += s_dxn
        mg_ref[0, 1:2, :] += s_dxr
        mg_ref[0, 2:3, :] += dga

    tok = pl.BlockSpec((tm, D), lambda i: (i, 0))
    tokf = pl.BlockSpec((tm, FF), lambda i: (i, 0))
    return _call(
        body, name="ffn_bwd", grid=(n_t,),
        in_specs=[tok, tok, tok, tokf, tokf, pl.BlockSpec((1, 9, D), lambda i: (i // tps, 0, 0)),
                  pl.BlockSpec((1, D), lambda i: (0, 0)), _vmem(), _vmem(), _vmem()],
        out_specs=[tok, tok, tok, tokf, tokf, tokf, pl.BlockSpec((1, 8, D), lambda i: (i // tps, 0, 0))],
        out_shape=[jax.ShapeDtypeStruct((T, D), F32), jax.ShapeDtypeStruct((T, D), BF16),
                   jax.ShapeDtypeStruct((T, D), BF16), jax.ShapeDtypeStruct((T, FF), BF16),
                   jax.ShapeDtypeStruct((T, FF), BF16), jax.ShapeDtypeStruct((T, FF), BF16),
                   jax.ShapeDtypeStruct((B, 8, D), F32)],
        args=(h, dhn, y, gb, ub, mods, gn, wg, wu, wd), semantics=("arbitrary",), plans=plans)


def _wgrad(a, b, name, plans=()):
    T, da = a.shape
    db = b.shape[1]
    bm = {2816: 1408, 3840: 1280}[da]
    bn = db
    tk = min(2048, T)
    nk = T // tk

    def body(a_ref, b_ref, o_ref):
        @pl.when(pl.program_id(2) == 0)
        def _():
            o_ref[...] = jnp.zeros(o_ref.shape, F32)

        o_ref[...] += _dot_tn(a_ref[...], b_ref[...])

    (out,), plan_outs = _call(
        body, name=name, grid=(da // bm, db // bn, nk),
        in_specs=[pl.BlockSpec((tk, bm), lambda i, j, k: (k, i)), pl.BlockSpec((tk, bn), lambda i, j, k: (k, j))],
        out_specs=[pl.BlockSpec((bm, bn), lambda i, j, k: (i, j))], out_shape=[jax.ShapeDtypeStruct((da, db), F32)],
        args=(a, b), semantics=("parallel", "parallel", "arbitrary"), plans=plans)
    return out, plan_outs


def _inproj_fwd(h, mods, gn, w_in, seq, plans=()):
    T = h.shape[0]
    tm = _token_tile(seq)
    tps = seq // tm

    def body(h_ref, m_ref, g_ref, w_ref, *outs):
        xn, _, _ = _rms_mod(h_ref[...], g_ref[...], m_ref[0, 3:4, :], m_ref[0, 4:5, :])
        xb = xn.astype(BF16)
        for (_, c0, w), o_ref in zip(PIECES, outs):
            o_ref[...] = _dot_nt(xb, w_ref[c0:c0 + w, :])

    return _call(
        body, name="inproj_fwd", grid=(T // tm,),
        in_specs=[pl.BlockSpec((tm, D), lambda i: (i, 0)), pl.BlockSpec((1, 9, D), lambda i: (i // tps, 0, 0)),
                  pl.BlockSpec((1, D), lambda i: (0, 0)), _vmem()],
        out_specs=[pl.BlockSpec((tm, w), lambda i: (i, 0)) for _, _, w in PIECES],
        out_shape=[jax.ShapeDtypeStruct((T, w), F32) for _, _, w in PIECES], args=(h, mods, gn, w_in),
        semantics=("parallel",), plans=plans)


def _inproj_bwd(h, dh_res, dpieces, mods, gn, w_in, seq, plans=()):
    T = h.shape[0]
    B = T // seq
    tm = _token_tile(seq, 256)
    tps = seq // tm

    def body(h_ref, dres_ref, *rest):
        dp_refs = rest[:len(PIECES)]
        m_ref, g_ref, w_ref, dh_ref, xb_ref, dpb_ref, mg_ref = rest[len(PIECES):]
        i = pl.program_id(0)
        g = g_ref[...]
        sc = m_ref[0, 4:5, :]
        xn, r, inv = _rms_mod(h_ref[...], g, m_ref[0, 3:4, :], sc)
        xb_ref[...] = xn.astype(BF16)
        dxn = jnp.zeros((tm, D), F32)
        for (_, c0, w), dp_ref in zip(PIECES, dp_refs):
            dpb = dp_ref[...].astype(BF16)
            dpb_ref[:, c0:c0 + w] = dpb
            dxn = dxn + _dot(dpb, w_ref[c0:c0 + w, :])
        dh, s_dxn, s_dxr = _rms_mod_bwd(dxn, r, inv, g, sc)
        dh_ref[...] = dres_ref[...] + dh

        @pl.when(i % tps == 0)
        def _():
            mg_ref[...] = jnp.zeros(mg_ref.shape, F32)

        mg_ref[0, 0:1, :] += s_dxn
        mg_ref[0, 1:2, :] += s_dxr

    tok = pl.BlockSpec((tm, D), lambda i: (i, 0))
    return _call(
        body, name="inproj_bwd", grid=(T // tm,),
        in_specs=[tok, tok] + [pl.BlockSpec((tm, w), lambda i: (i, 0)) for _, _, w in PIECES]
        + [pl.BlockSpec((1, 9, D), lambda i: (i // tps, 0, 0)), pl.BlockSpec((1, D), lambda i: (0, 0)), _vmem()],
        out_specs=[tok, tok, pl.BlockSpec((tm, IN_COLS), lambda i: (i, 0)),
                   pl.BlockSpec((1, 8, D), lambda i: (i // tps, 0, 0))],
        out_shape=[jax.ShapeDtypeStruct((T, D), F32), jax.ShapeDtypeStruct((T, D), BF16),
                   jax.ShapeDtypeStruct((T, IN_COLS), BF16), jax.ShapeDtypeStruct((B, 8, D), F32)],
        args=(h, dh_res, *dpieces, mods, gn, w_in), semantics=("arbitrary",), plans=plans)


def _seg_mean(x):
    i = lax.broadcasted_iota(jnp.int32, (128, 128), 0) >> 6
    j = lax.broadcasted_iota(jnp.int32, (128, 128), 1) >> 6
    ones = jnp.where(i == j, 1.0 / HD, 0.0).astype(BF16)
    hi = x.astype(BF16)
    lo = (x - hi.astype(F32)).astype(BF16)
    return _dot(hi, ones) + _dot(lo, ones)


def _head_norm(x, g2):
    inv = lax.rsqrt(_seg_mean(x * x) + EPS)
    y = x * inv
    return y * g2, y, inv


def _head_norm_bwd(dxn, y, inv, g2):
    dy = dxn * g2
    return inv * (dy - y * _seg_mean(dy * y)), jnp.sum(dxn * y, axis=0, keepdims=True)


def _swa_block(q, kk, vv, gq2, gk2, sinks, first, do=None):
    lo = lax.broadcasted_iota(jnp.int32, (1, 128), 1) < HD
    kn, ky, kinv = _head_norm(kk, gk2)

    def operands(x):
        xr = pltpu.roll(x, HD, 1)
        own_lo, own_hi = jnp.where(lo, x, 0.0).astype(BF16), jnp.where(lo, 0.0, x).astype(BF16)
        rolled_lo, rolled_hi = jnp.where(lo, xr, 0.0).astype(BF16), jnp.where(lo, 0.0, xr).astype(BF16)
        return (own_lo, rolled_hi), (rolled_lo, own_hi)

    def restore(parts):
        (own_lo, rolled_hi), (rolled_lo, own_hi) = parts
        return (jnp.where(lo, own_lo, own_hi)
                + pltpu.roll(jnp.where(lo, rolled_lo, rolled_hi), HD, 1))

    k_ops, v_ops = operands(kn), operands(vv)
    k2 = [jnp.concatenate(pair, axis=0) for pair in k_ops]
    def stack(x):
        return jnp.concatenate([x[:, 128 * p:128 * (p + 1)] for p in range(4)], axis=0)

    def unstack(x):
        return jnp.concatenate([x[BLK * p:BLK * (p + 1)] for p in range(4)], axis=1)

    def of_head(x, kh):
        return x[2 * BLK * kh:2 * BLK * (kh + 1)]

    nq = 4 * BLK
    pair = lax.broadcasted_iota(jnp.int32, (nq, 1), 0) >> 7
    row = lax.broadcasted_iota(jnp.int32, (nq, 2 * BLK), 0) & (BLK - 1)
    col = lax.broadcasted_iota(jnp.int32, (nq, 2 * BLK), 1)
    valid = (col <= row + BLK) & (col > row) & (col >= jnp.where(first, BLK, 0))
    scale = HD ** -0.5
    qn, qy, qinv = _head_norm(stack(q), gq2)
    qnb = qn.astype(BF16)
    s2 = jnp.concatenate([_dot_nt(of_head(qnb, kh), k2[kh]) for kh in range(N_KV)], axis=0) * scale
    probs, p_sink = [], []
    for j in range(2):
        s = jnp.where(valid, s2[:, 2 * BLK * j:2 * BLK * (j + 1)], NEG)
        sink = jnp.zeros((nq, 1), F32)
        for p in range(4):
            sink = jnp.where(pair == p, sinks[:, 2 * p + j:2 * p + j + 1], sink)
        m = jnp.maximum(jnp.max(s, axis=-1, keepdims=True), sink)
        e = jnp.exp(s - m)
        e_sink = jnp.exp(sink - m)
        rden = 1.0 / (jnp.sum(e, axis=-1, keepdims=True) + e_sink)
        probs.append(e * rden)
        p_sink.append(e_sink * rden)
    pb = [p.astype(BF16) for p in probs]
    if do is None:
        return unstack(jnp.concatenate(
            [_dot(of_head(pb[0], kh), v_ops[kh][0]) + _dot(of_head(pb[1], kh), v_ops[kh][1]) for kh in range(N_KV)],
            axis=0))
    dob = stack(do).astype(BF16)
    ds, dsinks = [], [None] * 8
    for j in range(2):
        dp = jnp.concatenate([_dot_nt(of_head(dob, kh), v_ops[kh][j]) for kh in range(N_KV)], axis=0)
        t = jnp.sum(dp * probs[j], axis=-1, keepdims=True)
        ds.append(probs[j] * (dp - t) * scale)
        lost = p_sink[j] * t
        for p in range(4):
            dsinks[2 * p + j] = -jnp.sum(lost[BLK * p:BLK * (p + 1)])
    dsb = jnp.concatenate(ds, axis=1).astype(BF16)
    dqn = jnp.concatenate([_dot(of_head(dsb, kh), k2[kh]) for kh in range(N_KV)], axis=0)
    dq, dgq2 = _head_norm_bwd(dqn, qy, qinv, gq2)
    dk2 = [_dot_tn(of_head(dsb, kh), of_head(qnb, kh)) for kh in range(N_KV)]
    dv_ops = [[_dot_tn(of_head(pb[j], kh), of_head(dob, kh)) for j in range(2)] for kh in range(N_KV)]
    dkn = restore(tuple((d[:2 * BLK], d[2 * BLK:]) for d in dk2))
    dvv = restore(tuple(tuple(d) for d in dv_ops))
    dkk, dgk2 = _head_norm_bwd(dkn, ky, kinv, gk2)
    return unstack(dq), dkk, dvv, dgq2, dgk2, dsinks


SWA_GROUP = 8


def _swa_specs(nbs):
    grp = min(SWA_GROUP, nbs)
    rows = grp * BLK

    def tok(w):
        return pl.BlockSpec((rows, w), lambda i: (i, 0))

    halo = pl.BlockSpec((BLK, 128), lambda i: (jnp.maximum(i * grp - 1, 0), 0))
    vec = pl.BlockSpec((1, 128), lambda i: (0, 0))
    sk = pl.BlockSpec((1, 8), lambda i: (0, 0))
    return grp, tok, halo, vec, sk


def _swa_fwd(q, k, v, gq2, gk2, sinks, nbs, plans=()):
    T = q.shape[0]
    grp, tok, halo, vec, sk = _swa_specs(nbs)

    def body(q_ref, kh_ref, kc_ref, vh_ref, vc_ref, gq_ref, gk_ref, sk_ref, o_ref):
        seq_start = ((pl.program_id(0) * grp) % nbs) == 0
        for g in range(grp):
            rows = slice(g * BLK, (g + 1) * BLK)
            prev = slice((g - 1) * BLK, g * BLK)
            kk = jnp.concatenate([kh_ref[...] if g == 0 else kc_ref[prev, :], kc_ref[rows, :]], axis=0)
            vv = jnp.concatenate([vh_ref[...] if g == 0 else vc_ref[prev, :], vc_ref[rows, :]], axis=0)
            o_ref[rows, :] = _swa_block(q_ref[rows, :], kk, vv, gq_ref[...], gk_ref[...], sk_ref[...],
                                        seq_start if g == 0 else False)

    (out,), plan_outs = _call(
        body, name="swa_fwd", grid=(T // (grp * BLK),),
        in_specs=[tok(D_B), halo, tok(128), halo, tok(128), vec, vec, sk], out_specs=[tok(D_B)],
        out_shape=[jax.ShapeDtypeStruct((T, D_B), F32)], args=(q, k, k, v, v, gq2, gk2, sinks),
        semantics=("parallel",), plans=plans)
    return out, plan_outs


def _swa_bwd(q, k, v, gq2, gk2, sinks, do, nbs, plans=()):
    T = q.shape[0]
    grp, tok, halo, vec, sk = _swa_specs(nbs)
    steps = T // (grp * BLK)

    def body(q_ref, kh_ref, kc_ref, vh_ref, vc_ref, gq_ref, gk_ref, sk_ref, do_ref,
             dq_ref, dk_ref, dv_ref, dkh_ref, dvh_ref, dgq_ref, dgk_ref, dsk_ref):
        i = pl.program_id(0)
        seq_start = ((i * grp) % nbs) == 0

        @pl.when(i == 0)
        def _():
            for r in (dgq_ref, dgk_ref, dsk_ref):
                r[...] = jnp.zeros(r.shape, F32)

        res = []
        for g in range(grp):
            rows = slice(g * BLK, (g + 1) * BLK)
            prev = slice((g - 1) * BLK, g * BLK)
            kk = jnp.concatenate([kh_ref[...] if g == 0 else kc_ref[prev, :], kc_ref[rows, :]], axis=0)
            vv = jnp.concatenate([vh_ref[...] if g == 0 else vc_ref[prev, :], vc_ref[rows, :]], axis=0)
            res.append(_swa_block(q_ref[rows, :], kk, vv, gq_ref[...], gk_ref[...], sk_ref[...],
                                  seq_start if g == 0 else False, do=do_ref[rows, :]))
        lane = lax.broadcasted_iota(jnp.int32, (8, 128), 1)
        upd = jnp.zeros((8, 128), F32)
        for g, (dq, dkk, dvv, dgq2, dgk2, dsinks) in enumerate(res):
            rows = slice(g * BLK, (g + 1) * BLK)
            dq_ref[rows, :] = dq
            dk_ref[rows, :] = dkk[BLK:] + res[g + 1][1][:BLK] if g + 1 < grp else dkk[BLK:]
            dv_ref[rows, :] = dvv[BLK:] + res[g + 1][2][:BLK] if g + 1 < grp else dvv[BLK:]
            dgq_ref[...] += dgq2
            dgk_ref[...] += dgk2
            for h, d in enumerate(dsinks):
                upd = upd + jnp.where(lane == h, d, 0.0)
        dkh_ref[0] = res[0][1][:BLK]
        dvh_ref[0] = res[0][2][:BLK]
        dsk_ref[...] += upd

    one = pl.BlockSpec((1, BLK, 128), lambda i: (i, 0, 0))
    halo_shape = jax.ShapeDtypeStruct((steps, BLK, 128), F32)
    return _call(
        body, name="swa_bwd", grid=(steps,),
        in_specs=[tok(D_B), halo, tok(128), halo, tok(128), vec, vec, sk, tok(D_B)],
        out_specs=[tok(D_B), tok(128), tok(128), one, one, vec, vec, pl.BlockSpec((8, 128), lambda i: (0, 0))],
        out_shape=[jax.ShapeDtypeStruct((T, D_B), F32), jax.ShapeDtypeStruct((T, 128), F32),
                   jax.ShapeDtypeStruct((T, 128), F32), halo_shape, halo_shape, jax.ShapeDtypeStruct((1, 128), F32),
                   jax.ShapeDtypeStruct((1, 128), F32), jax.ShapeDtypeStruct((8, 128), F32)],
        args=(q, k, k, v, v, gq2, gk2, sinks, do), semantics=("arbitrary",), plans=plans)


def _swa_add_halo(dk, dk_halo):
    steps = dk_halo.shape[0]
    nxt = jnp.concatenate([dk_halo[1:], jnp.zeros_like(dk_halo[:1])], axis=0)[:, None]
    dk = dk.reshape(steps, -1, BLK, 128)
    return jnp.concatenate([dk[:, :-1], dk[:, -1:] + nxt], axis=1).reshape(-1, 128)


def _sgu_norm(av, g_ln, b_ln):
    t, th = _gelu(av)
    mu = jnp.mean(t, axis=-1, keepdims=True)
    tc = t - mu
    rstd = lax.rsqrt(jnp.mean(tc * tc, axis=-1, keepdims=True) + EPS)
    vhat = tc * rstd
    return vhat * g_ln + b_ln, vhat, rstd, th


def _masked_ws(ws_ref):
    tril = lax.broadcasted_iota(jnp.int32, (BLK, BLK), 0) >= lax.broadcasted_iota(jnp.int32, (BLK, BLK), 1)
    return [jnp.where(tril, ws_ref[g], 0.0).astype(BF16) for g in range(N_GRP)]


def _mix_fwd(au, av, gta, gtb, ob, h, mods, g_ln, b_ln, ws, bsb, wa, wb, wout, seq, plans=()):
    T = h.shape[0]
    tm = _token_tile(seq)
    tps = seq // tm

    def body(au_ref, av_ref, gta_ref, gtb_ref, ob_ref, h_ref, m_ref, gl_ref, bl_ref, ws_ref, bs_ref,
             wa_ref, wb_ref, wo_ref, out_ref, vvb_s, z_s):
        u, _ = _gelu(au_ref[...])
        vv, _, _, _ = _sgu_norm(av_ref[...], gl_ref[...], bl_ref[...])
        vvb_s[...] = vv.astype(BF16)
        wsm = _masked_ws(ws_ref)
        for c in range(tm // BLK):
            rows = slice(c * BLK, (c + 1) * BLK)
            for g in range(N_GRP):
                cols = slice(g * BLK, (g + 1) * BLK)
                z_s[rows, cols] = _dot(wsm[g], vvb_s[rows, cols]) + bs_ref[g]
        ya = _dot_nt((u * z_s[...]).astype(BF16), wa_ref[...])
        yb = _dot_nt(ob_ref[...].astype(BF16), wb_ref[...])
        merged = jax.nn.sigmoid(gta_ref[...]) * ya + jax.nn.sigmoid(gtb_ref[...]) * yb
        out_ref[...] = h_ref[...] + m_ref[0, 5:6, :] * _dot(merged.astype(BF16), wo_ref[...])

    def tok(w):
        return pl.BlockSpec((tm, w), lambda i: (i, 0))

    def full(shape):
        return pl.BlockSpec(shape, lambda i: (0,) * len(shape))

    (out,), plan_outs = _call(
        body, name="mix_fwd", grid=(T // tm,),
        in_specs=[tok(D_A), tok(D_A), tok(D), tok(D), tok(D_B), tok(D),
                  pl.BlockSpec((1, 9, D), lambda i: (i // tps, 0, 0)), full((1, D_A)), full((1, D_A)),
                  full((N_GRP, BLK, BLK)), full((N_GRP, BLK, BLK)), _vmem(), _vmem(), _vmem()],
        out_specs=[tok(D)], out_shape=[jax.ShapeDtypeStruct((T, D), F32)],
        scratch_shapes=[pltpu.VMEM((tm, D_A), BF16), pltpu.VMEM((tm, D_A), F32)],
        args=(au, av, gta, gtb, ob, h, mods, g_ln, b_ln, ws, bsb, wa, wb, wout), semantics=("parallel",),
        plans=plans)
    return out, plan_outs


def _mix_bwd(au, av, gta, gtb, ob, dh, mods, g_ln, b_ln, ws, bsb, wa, wb, wout, seq, plans=()):
    T = dh.shape[0]
    B = T // seq
    tm = _token_tile(seq, 256)
    tps = seq // tm

    def body(au_ref, av_ref, gta_ref, gtb_ref, ob_ref, dh_ref, m_ref, gl_ref, bl_ref, ws_ref, bs_ref,
             wa_ref, wb_ref, wo_ref,
             dau_ref, dav_ref, dgta_ref, dgtb_ref, dob_ref, dwo_ref, dwa_ref, dwb_ref, dws_ref, dbs_ref, dln_ref,
             mg_ref, vvb_s, z_s, dz_s, dzb_s, dvv_s):
        i = pl.program_id(0)

        @pl.when(i == 0)
        def _():
            for r in (dwo_ref, dwa_ref, dwb_ref, dws_ref, dbs_ref, dln_ref):
                r[...] = jnp.zeros(r.shape, F32)

        @pl.when(i % tps == 0)
        def _():
            mg_ref[...] = jnp.zeros(mg_ref.shape, F32)

        auv = au_ref[...]
        avv = av_ref[...]
        u, thu = _gelu(auv)
        g_ln = gl_ref[...]
        vv, vhat, rstd, thv = _sgu_norm(avv, g_ln, bl_ref[...])
        vvb_s[...] = vv.astype(BF16)
        wsm = _masked_ws(ws_ref)
        for c in range(tm // BLK):
            rows = slice(c * BLK, (c + 1) * BLK)
            for g in range(N_GRP):
                cols = slice(g * BLK, (g + 1) * BLK)
                z_s[rows, cols] = _dot(wsm[g], vvb_s[rows, cols]) + bs_ref[g]
        z = z_s[...]
        yab = (u * z).astype(BF16)
        obb = ob_ref[...].astype(BF16)
        ya = _dot_nt(yab, wa_ref[...])
        yb = _dot_nt(obb, wb_ref[...])
        sa = jax.nn.sigmoid(gta_ref[...])
        sb = jax.nn.sigmoid(gtb_ref[...])
        mb = (sa * ya + sb * yb).astype(BF16)
        dhv = dh_ref[...]
        mg_ref[0, 0:1, :] += jnp.sum(dhv * _dot(mb, wo_ref[...]), axis=0, keepdims=True)
        dmob = (m_ref[0, 5:6, :] * dhv).astype(BF16)
        dwo_ref[...] += _dot_tn(mb, dmob)
        dmerged = _dot_nt(dmob, wo_ref[...])
        dya = dmerged * sa
        dyb = dmerged * sb
        dgta_ref[...] = dya * ya * (1.0 - sa)
        dgtb_ref[...] = dyb * yb * (1.0 - sb)
        dyab = dya.astype(BF16)
        dybb = dyb.astype(BF16)
        dwa_ref[...] += _dot_tn(dyab, yab)
        dwb_ref[...] += _dot_tn(dybb, obb)
        dob_ref[...] = _dot(dybb, wb_ref[...])
        dyap = _dot(dyab, wa_ref[...])
        dau_ref[...] = (dyap * z) * _gelu_grad(auv, thu)
        dz = dyap * u
        dz_s[...] = dz
        dzb_s[...] = dz.astype(BF16)
        for c in range(tm // BLK):
            rows = slice(c * BLK, (c + 1) * BLK)
            for g in range(N_GRP):
                cols = slice(g * BLK, (g + 1) * BLK)
                dzb = dzb_s[rows, cols]
                dvv_s[rows, cols] = _dot_tn(wsm[g], dzb)
                dws_ref[g] += _dot_nt(dzb, vvb_s[rows, cols])
                dbs_ref[g] += dz_s[rows, cols]
        dvv = dvv_s[...]
        dln_ref[0:1, :] += jnp.sum(dvv * vhat, axis=0, keepdims=True)
        dln_ref[1:2, :] += jnp.sum(dvv, axis=0, keepdims=True)
        dvh = dvv * g_ln
        dt = rstd * (dvh - jnp.mean(dvh, axis=-1, keepdims=True)
                     - vhat * jnp.mean(dvh * vhat, axis=-1, keepdims=True))
        dav_ref[...] = dt * _gelu_grad(avv, thv)

    def tok(w):
        return pl.BlockSpec((tm, w), lambda i: (i, 0))

    def full(shape):
        return pl.BlockSpec(shape, lambda i: (0,) * len(shape))

    return _call(
        body, name="mix_bwd", grid=(T // tm,),
        in_specs=[tok(D_A), tok(D_A), tok(D), tok(D), tok(D_B), tok(D),
                  pl.BlockSpec((1, 9, D), lambda i: (i // tps, 0, 0)), full((1, D_A)), full((1, D_A)),
                  full((N_GRP, BLK, BLK)), full((N_GRP, BLK, BLK)), _vmem(), _vmem(), _vmem()],
        out_specs=[tok(D_A), tok(D_A), tok(D), tok(D), tok(D_B), full((D, D)), full((D, D_A)), full((D, D_B)),
                   full((N_GRP, BLK, BLK)), full((N_GRP, BLK, BLK)), full((8, D_A)),
                   pl.BlockSpec((1, 8, D), lambda i: (i // tps, 0, 0))],
        out_shape=[jax.ShapeDtypeStruct((T, D_A), F32), jax.ShapeDtypeStruct((T, D_A), F32),
                   jax.ShapeDtypeStruct((T, D), F32), jax.ShapeDtypeStruct((T, D), F32),
                   jax.ShapeDtypeStruct((T, D_B), F32), jax.ShapeDtypeStruct((D, D), F32),
                   jax.ShapeDtypeStruct((D, D_A), F32), jax.ShapeDtypeStruct((D, D_B), F32),
                   jax.ShapeDtypeStruct((N_GRP, BLK, BLK), F32), jax.ShapeDtypeStruct((N_GRP, BLK, BLK), F32),
                   jax.ShapeDtypeStruct((8, D_A), F32), jax.ShapeDtypeStruct((B, 8, D), F32)],
        scratch_shapes=[pltpu.VMEM((tm, D_A), BF16), pltpu.VMEM((tm, D_A), F32), pltpu.VMEM((tm, D_A), F32),
                        pltpu.VMEM((tm, D_A), BF16), pltpu.VMEM((tm, D_A), F32)],
        args=(au, av, gta, gtb, ob, dh, mods, g_ln, b_ln, ws, bsb, wa, wb, wout), semantics=("arbitrary",),
        plans=plans)


def _prologue(c_pad, w_ada, first_shards):
    cols = w_ada.shape[1]
    plan_w, plan_c = _GatherRelayed(first_shards), _GatherDirect([c_pad])
    plan_m = _Gather([jax.ShapeDtypeStruct((N_DEV * 8, cols), F32)])
    n_w = len(first_shards)

    def body(c_ref, wada_ref, *rest):
        w_ins, rest = rest[:n_w], rest[n_w:]
        call_ref, mods_ref = rest[:2]
        w_outs, rest = rest[2:2 + n_w], rest[2 + n_w:]
        cvm, part, local_sem = rest[:3]
        sems = rest[3:]
        sems_w, sems_c, sems_m = sems[:3], sems[3:6], sems[6:9]
        plan_c.start([c_ref], [call_ref], sems_c)
        plan_c.relay([c_ref], [call_ref], sems_c)
        plan_c.finish([c_ref], [call_ref], sems_c)
        load = pltpu.make_async_copy(call_ref, cvm, local_sem)
        load.start()
        load.wait()
        cv = cvm[...].reshape(N_DEV * 8, D)
        part[...] = _dot((cv * jax.nn.sigmoid(cv)).astype(BF16), wada_ref[...].astype(BF16))
        plan_m.start([part], [mods_ref], sems_m)
        plan_w.start(w_ins, w_outs, sems_w)
        plan_m.relay([part], [mods_ref], sems_m)
        plan_m.finish([part], [mods_ref], sems_m)
        plan_w.relay(w_ins, w_outs, sems_w)
        plan_w.finish(w_ins, w_outs, sems_w)

    res = pl.pallas_call(
        body, name="prologue", in_specs=[_any(), _vmem()] + [_any()] * n_w,
        out_specs=[_any()] * (2 + n_w), out_shape=plan_c.out_shape + plan_m.out_shape + plan_w.out_shape,
        scratch_shapes=[pltpu.VMEM((N_DEV, 8, D), F32), pltpu.VMEM((N_DEV * 8, cols), F32), pltpu.SemaphoreType.DMA]
        + plan_w.scratch + plan_c.scratch + plan_m.scratch,
        compiler_params=pltpu.CompilerParams(vmem_limit_bytes=VMEM_LIMIT),
    )(c_pad, w_ada, *first_shards)
    return res[0], res[1], list(res[2:])


def _ada_update(c_all, dm_cols, w, m, v):
    n, cols = c_all.shape[0], w.shape[1]
    br = 256

    def body(c_ref, dm_ref, w_ref, m_ref, v_ref, g_ref, d_ref, nm_ref, nv_ref):
        cv = c_ref[...]
        g = _dot_tn((cv * jax.nn.sigmoid(cv)).astype(BF16), dm_ref[...].astype(BF16))
        g_ref[...] = g
        d_ref[...], nm_ref[...], nv_ref[...] = _adamw(w_ref[...], g, m_ref[...], v_ref[...])

    rows = pl.BlockSpec((br, cols), lambda i: (i, 0))
    outs, _ = _call(
        body, name="ada_update", grid=(D // br,),
        in_specs=[pl.BlockSpec((n, br), lambda i: (0, i)), pl.BlockSpec((n, cols), lambda i: (0, 0)), rows, rows, rows],
        out_specs=[rows] * 4, out_shape=[jax.ShapeDtypeStruct(w.shape, F32)] * 4, args=(c_all, dm_cols, w, m, v),
        semantics=("parallel",))
    return outs


def _mod_finish(mg1, mg2, mg2g, mg3, mods, g1, g2, g3):
    B = mods.shape[0]

    def body(mg1_ref, mg2_ref, mg2g_ref, mg3_ref, m_ref, g1_ref, g2_ref, g3_ref, dm_ref, dgn_ref):
        dgn_ref[...] = jnp.zeros(dgn_ref.shape, F32)
        for k, (mg, g_ref) in enumerate(((mg1_ref, g1_ref), (mg2_ref, g2_ref), (mg3_ref, g3_ref))):
            for b in range(B):
                s_dxr = mg[b, 1:2, :]
                dm_ref[b, 3 * k:3 * k + 1, :] = mg[b, 0:1, :]
                dm_ref[b, 3 * k + 1:3 * k + 2, :] = g_ref[...] * s_dxr
                dm_ref[b, 3 * k + 2:3 * k + 3, :] = mg2g_ref[b, 0:1, :] if k == 1 else mg[b, 2:3, :]
                dgn_ref[k:k + 1, :] += (1.0 + m_ref[b, 3 * k + 1:3 * k + 2, :]) * s_dxr

    args = (mg1, mg2, mg2g, mg3, mods, g1, g2, g3)
    out_shape = [jax.ShapeDtypeStruct((B, 9, D), F32), jax.ShapeDtypeStruct((8, D), F32)]
    return pl.pallas_call(body, name="mod_finish", grid=(1,), in_specs=[_whole(a) for a in args],
                          out_specs=[_whole(o) for o in out_shape], out_shape=out_shape,
                          compiler_params=_params("arbitrary"))(*args)


def _small_update(gathered, params, ms, vs):
    n = len(SMALL)
    B = gathered[0].shape[1]

    def body(*refs):
        gdm, ggn, gln, gws, gbs, ggq, ggk, gsk, gloss = refs[:9]
        w = dict(zip(SMALL, refs[9:9 + n]))
        m = dict(zip(SMALL, refs[9 + n:9 + 2 * n]))
        v = dict(zip(SMALL, refs[9 + 2 * n:9 + 3 * n]))
        outs = refs[9 + 3 * n:-1]
        out = {name: outs[4 * k:4 * k + 4] for k, name in enumerate(SMALL)}

        def total(ref, idx):
            acc = ref[(0,) + idx]
            for dev in range(1, N_DEV):
                acc = acc + ref[(dev,) + idx]
            return acc

        def finish(name, g, idx=(Ellipsis,)):
            d, nm, nv = _adamw(w[name][idx], g, m[name][idx], v[name][idx])
            for ref, val in zip(out[name], (g, d, nm, nv)):
                ref[idx] = val

        g_bada = total(gdm, (slice(0, 1),))
        for b in range(1, B):
            g_bada = g_bada + total(gdm, (slice(b, b + 1),))
        finish("b_ada", g_bada)
        finish("g_norm1", total(ggn, (slice(0, 1),)))
        finish("g_norm2", total(ggn, (slice(1, 2),)))
        finish("g_norm3", total(ggn, (slice(2, 3),)))
        finish("g_sgu_ln", total(gln, (slice(0, 1),)))
        finish("b_sgu_ln", total(gln, (slice(1, 2),)))
        tril = lax.broadcasted_iota(jnp.int32, (BLK, BLK), 0) >= lax.broadcasted_iota(jnp.int32, (BLK, BLK), 1)
        for g in range(N_GRP):
            finish("w_spatial", jnp.where(tril, total(gws, (g,)), 0.0), (0, g))
            finish("b_spatial", jnp.sum(total(gbs, (g,)).T, axis=0, keepdims=True), (0, slice(g, g + 1)))
        finish("g_q", total(ggq, ()))
        finish("g_k", total(ggk, ()))
        finish("attn_sinks", total(gsk, (slice(0, 1), slice(0, N_KV * Q_PER_KV))))
        refs[-1][...] = total(gloss, ())

    args = list(gathered) + [params[k] for k in SMALL] + [ms[k] for k in SMALL] + [vs[k] for k in SMALL]
    out_shape = []
    for k in SMALL:
        out_shape += [jax.ShapeDtypeStruct(params[k].shape, F32)] * 4
    out_shape.append(jax.ShapeDtypeStruct((8, 128), F32))
    res = pl.pallas_call(body, name="small_update", grid=(1,), in_specs=[_whole(a) for a in args],
                         out_specs=[_whole(o) for o in out_shape], out_shape=out_shape,
                         compiler_params=_params("arbitrary"))(*args)
    return {k: res[4 * i:4 * i + 4] for i, k in enumerate(SMALL)}, res[-1][0, 0]


def _rs_add(p, r, core, name):
    _, rows, width = p.shape

    def body(c_ref, p_ref, r_ref, o_ref):
        o_ref[...] = (p_ref[...] + r_ref[...]).astype(BF16)

    return pl.pallas_call(
        body, name="rs_add_" + name, out_shape=jax.ShapeDtypeStruct((4, rows, width), BF16),
        grid_spec=pltpu.PrefetchScalarGridSpec(
            num_scalar_prefetch=1, grid=(4,),
            in_specs=[pl.BlockSpec((1, rows, width), lambda k, c_ref: (2 * k + c_ref[0], 0, 0)),
                      pl.BlockSpec((1, rows, width), lambda k, c_ref: (k, 0, 0))],
            out_specs=pl.BlockSpec((1, rows, width), lambda k, c_ref: (k, 0, 0))),
        compiler_params=_params("parallel"),
    )(core, p, r)


def _adam_big(w, m, v, r, name):
    rows, cols = w.shape

    def body(r_ref, w_ref, m_ref, v_ref, go_ref, d_ref, nm_ref, nv_ref):
        g = r_ref[0].astype(F32)
        for k in range(1, 4):
            g = g + r_ref[k].astype(F32)
        go_ref[...] = g
        d_ref[...], nm_ref[...], nv_ref[...] = _adamw(w_ref[...], g, m_ref[...], v_ref[...])

    br = rows // 2
    shard = pl.BlockSpec((br, cols), lambda i: (i, 0))
    return pl.pallas_call(body, name="adam_" + name, grid=(2,),
                          in_specs=[pl.BlockSpec((4, br, cols), lambda i: (0, i, 0)), shard, shard, shard],
                          out_specs=[shard] * 4, out_shape=[jax.ShapeDtypeStruct(w.shape, F32)] * 4,
                          compiler_params=_params("parallel"))(r, w, m, v)


def kernel(x, c, w_ada, b_ada, g_norm1, ffn1_w_gate, ffn1_w_up, ffn1_w_down, g_norm2, w_in, g_sgu_ln, b_sgu_ln, w_spatial, b_spatial, g_q, g_k, attn_sinks, w_branch_a, w_branch_b, w_out, g_norm3, ffn2_w_gate, ffn2_w_up, ffn2_w_down, loss_target, m_w_ada, m_b_ada, m_g_norm1, m_ffn1_w_gate, m_ffn1_w_up, m_ffn1_w_down, m_g_norm2, m_w_in, m_g_sgu_ln, m_b_sgu_ln, m_w_spatial, m_b_spatial, m_g_q, m_g_k, m_attn_sinks, m_w_branch_a, m_w_branch_b, m_w_out, m_g_norm3, m_ffn2_w_gate, m_ffn2_w_up, m_ffn2_w_down, v_w_ada, v_b_ada, v_g_norm1, v_ffn1_w_gate, v_ffn1_w_up, v_ffn1_w_down, v_g_norm2, v_w_in, v_g_sgu_ln, v_b_sgu_ln, v_w_spatial, v_b_spatial, v_g_q, v_g_k, v_attn_sinks, v_w_branch_a, v_w_branch_b, v_w_out, v_g_norm3, v_ffn2_w_gate, v_ffn2_w_up, v_ffn2_w_down):
    names = ("w_ada", "b_ada", "g_norm1", "ffn1_w_gate", "ffn1_w_up", "ffn1_w_down", "g_norm2", "w_in", "g_sgu_ln",
             "b_sgu_ln", "w_spatial", "b_spatial", "g_q", "g_k", "attn_sinks", "w_branch_a", "w_branch_b", "w_out",
             "g_norm3", "ffn2_w_gate", "ffn2_w_up", "ffn2_w_down")
    w = dict(zip(names, (w_ada, b_ada, g_norm1, ffn1_w_gate, ffn1_w_up, ffn1_w_down, g_norm2, w_in, g_sgu_ln,
                         b_sgu_ln, w_spatial, b_spatial, g_q, g_k, attn_sinks, w_branch_a, w_branch_b, w_out, g_norm3,
                         ffn2_w_gate, ffn2_w_up, ffn2_w_down)))
    m = dict(zip(names, (m_w_ada, m_b_ada, m_g_norm1, m_ffn1_w_gate, m_ffn1_w_up, m_ffn1_w_down, m_g_norm2, m_w_in,
                         m_g_sgu_ln, m_b_sgu_ln, m_w_spatial, m_b_spatial, m_g_q, m_g_k, m_attn_sinks, m_w_branch_a,
                         m_w_branch_b, m_w_out, m_g_norm3, m_ffn2_w_gate, m_ffn2_w_up, m_ffn2_w_down)))
    v = dict(zip(names, (v_w_ada, v_b_ada, v_g_norm1, v_ffn1_w_gate, v_ffn1_w_up, v_ffn1_w_down, v_g_norm2, v_w_in,
                         v_g_sgu_ln, v_b_sgu_ln, v_w_spatial, v_b_spatial, v_g_q, v_g_k, v_attn_sinks, v_w_branch_a,
                         v_w_branch_b, v_w_out, v_g_norm3, v_ffn2_w_gate, v_ffn2_w_up, v_ffn2_w_down)))
    B, seq, _ = x.shape
    T = B * seq
    nbs = seq // BLK
    xi, yi, ci = _place()
    me = 4 * xi + 2 * yi + ci
    core = jnp.reshape(ci, (1,)).astype(jnp.int32)
    layout = {name: (tform, n, width) for name, tform, n, width in BIG}
    shard = {name: w[name][0].astype(BF16).T if tform else w[name][0].astype(BF16) for name, tform, _, _ in BIG}
    wts, parts, big_out = {}, {}, {}

    def gather_plan(group):
        return _Gather([shard[k] for k in group])

    def take(group, gathered):
        for k, g in zip(group, gathered):
            wts[k] = g.reshape(N_DEV * layout[k][1], layout[k][2])

    def blocks(name, grad):
        return grad.reshape(N_DEV, layout[name][1], layout[name][2])

    def to_sibling(names, grads):
        return _RsSibling([blocks(k, g) for k, g in zip(names, grads)])

    def add(names, grads, from_sibling):
        for k, g, r in zip(names, grads, from_sibling):
            parts[k] = _rs_add(blocks(k, g), r, core, k)

    def to_chips(names):
        return _RsChips([parts[k] for k in names])

    def update(names, from_chips):
        for k, r in zip(names, from_chips):
            if layout[k][0]:
                big_out[k] = [o.T[None] for o in _adam_big(w[k][0].T, m[k][0].T, v[k][0].T, r, k)]
            else:
                big_out[k] = [o[None] for o in _adam_big(w[k][0], m[k][0], v[k][0], r, k)]

    g1, g2, g3, g_ln, b_ln = g_norm1, g_norm2, g_norm3, g_sgu_ln, b_sgu_ln
    gq2, gk2 = jnp.tile(g_q, (1, 2)), jnp.tile(g_k, (1, 2))
    ws = w_spatial[0]
    bsb = jnp.broadcast_to(b_spatial[0][:, :, None], (N_GRP, BLK, BLK))
    xf = x.reshape(T, D)
    tgt = loss_target.reshape(T, D)

    ffn1 = ("ffn1_w_gate", "ffn1_w_up", "ffn1_w_down")
    ffn2 = ("ffn2_w_gate", "ffn2_w_up", "ffn2_w_down")
    c_pad = jnp.concatenate([c, jnp.zeros((8 - B, D), F32)], axis=0)
    c_all, mods_cols, gathered = _prologue(c_pad, w_ada[0], [shard[k] for k in ffn1[:2]])
    take(ffn1[:2], gathered)
    c_all = c_all[:, :B].reshape(N_DEV * B, D)
    mine = lax.dynamic_slice_in_dim(mods_cols, 8 * me, B, axis=1)
    mods = (mine.transpose(1, 0, 2).reshape(B, 9 * D) + b_ada).reshape(B, 9, D)

    group = ("ffn1_w_down", "w_branch_a", "w_branch_b", "w_out")
    (gb1, ub1), (gathered,) = _ffn_gate_up(xf, mods, g1, wts["ffn1_w_gate"], wts["ffn1_w_up"], 0, seq,
                                           plans=[gather_plan(group)])
    take(group, gathered)
    group = ("w_in",)
    (h1, y1), (gathered,) = _ffn_down(xf, gb1, ub1, mods, wts["ffn1_w_down"], 0, seq, plans=[gather_plan(group)])
    take(group, gathered)
    group = ("ffn2_w_gate",)
    (au, av, q_tok, k_tok, v_tok, gta, gtb), (gathered,) = _inproj_fwd(h1, mods, g2, wts["w_in"], seq,
                                                                       plans=[gather_plan(group)])
    take(group, gathered)
    group = ("ffn2_w_up",)
    ob, (gathered,) = _swa_fwd(q_tok, k_tok, v_tok, gq2, gk2, attn_sinks, nbs, plans=[gather_plan(group)])
    take(group, gathered)
    mixw = (wts["w_branch_a"], wts["w_branch_b"], wts["w_out"])
    group = ("ffn2_w_down",)
    h2, (gathered,) = _mix_fwd(au, av, gta, gtb, ob, h1, mods, g_ln, b_ln, ws, bsb, *mixw, seq,
                               plans=[gather_plan(group)])
    take(group, gathered)
    dh3, y3, gb3, ub3, loss_parts = _ffn_fwd_loss(h2, mods, g3, *[wts[k] for k in ffn2], 6, seq, tgt)
    loss_part = jnp.full((8, 128), jnp.sum(loss_parts[:, 0, 0]))

    (dh2, xb, dyb, a, dg, du, mg3), _ = _ffn_bwd(h2, dh3, y3, gb3, ub3, mods, g3, *[wts[k] for k in ffn2], 6, seq)
    d_gate, _ = _wgrad(dg, xb, "wgrad_ffn2_gate")
    d_up, (r,) = _wgrad(du, xb, "wgrad_ffn2_up", plans=[to_sibling(ffn2[:1], [d_gate])])
    add(ffn2[:1], [d_gate], r)
    d_down, (r,) = _wgrad(a, dyb, "wgrad_ffn2_down", plans=[to_sibling(ffn2[1:2], [d_up])])
    add(ffn2[1:2], [d_up], r)

    (dau, dav, dgta, dgtb, dob, d_out, d_a, d_b, dws, dbs, dln, mg2g), (r, from_chips) = _mix_bwd(
        au, av, gta, gtb, ob, dh2, mods, g_ln, b_ln, ws, bsb, *mixw, seq,
        plans=[to_sibling(ffn2[2:], [d_down]), to_chips(ffn2[:2])])
    add(ffn2[2:], [d_down], r)
    update(ffn2[:2], from_chips)
    mixers = ("w_out", "w_branch_a", "w_branch_b")
    (dq, dk, dv, dk_halo, dv_halo, dgq2, dgk2, dsk), (from_chips, r) = _swa_bwd(
        q_tok, k_tok, v_tok, gq2, gk2, attn_sinks, dob, nbs,
        plans=[to_chips(ffn2[2:]), to_sibling(mixers, [d_out, d_a, d_b])])
    update(ffn2[2:], from_chips)
    add(mixers, [d_out, d_a, d_b], r)
    dk, dv = _swa_add_halo(dk, dk_halo), _swa_add_halo(dv, dv_halo)
    dgq, dgk = dgq2[:, :HD] + dgq2[:, HD:], dgk2[:, :HD] + dgk2[:, HD:]
    (dh1, xb2, dpb, mg2), (from_chips, early) = _inproj_bwd(
        h1, dh2, (dau, dav, dq, dk, dv, dgta, dgtb), mods, g2, wts["w_in"], seq,
        plans=[to_chips(mixers), _Gather([dln, dws, dbs, dgq, dgk, dsk, loss_part])])
    update(mixers, from_chips)

    d_in, _ = _wgrad(dpb, xb2, "wgrad_w_in")

    (dx, xb, dyb, a, dg, du, mg1), _ = _ffn_bwd(xf, dh1, y1, gb1, ub1, mods, g1, *[wts[k] for k in ffn1], 0, seq)
    dmods, dgn = _mod_finish(mg1, mg2, mg2g, mg3, mods, g1, g2, g3)
    d_gate, (r, late) = _wgrad(dg, xb, "wgrad_ffn1_gate",
                               plans=[to_sibling(("w_in",), [d_in]), _GatherDirect([dmods.reshape(B, 9 * D), dgn])])
    add(("w_in",), [d_in], r)
    gathered = late + early
    d_up, (r, from_chips) = _wgrad(du, xb, "wgrad_ffn1_up",
                                   plans=[to_sibling(ffn1[:1], [d_gate]), to_chips(("w_in",))])
    add(ffn1[:1], [d_gate], r)
    update(("w_in",), from_chips)
    d_down, (r, from_chips) = _wgrad(a, dyb, "wgrad_ffn1_down",
                                     plans=[to_sibling(ffn1[1:2], [d_up]), to_chips(ffn1[:1])])
    add(ffn1[1:2], [d_up], r)
    update(ffn1[:1], from_chips)
    small_out, loss = _small_update(gathered, w, m, v)
    dm_cols = lax.dynamic_slice_in_dim(gathered[0].reshape(N_DEV * B, 9 * D), (9 * D // N_DEV) * me,
                                       9 * D // N_DEV, axis=1)
    ada_out = _ada_update(c_all, dm_cols, w_ada[0], m_w_ada[0], v_w_ada[0])
    r, from_chips = _exchange([to_sibling(ffn1[2:], [d_down]), to_chips(ffn1[1:2])], "rs_tail")
    add(ffn1[2:], [d_down], r)
    update(ffn1[1:2], from_chips)
    (from_chips,) = _exchange([to_chips(ffn1[2:])], "rs_last")
    update(ffn1[2:], from_chips)

    def leaf(kind, name):
        if name == "w_ada":
            return ada_out[kind][None]
        if name in SMALL:
            return small_out[name][kind]
        return big_out[name][kind]

    return (loss, dx.reshape(B, seq, D), *[leaf(kind, name) for kind in range(4) for name in names])
```

```python
import math

import jax
import jax.numpy as jnp
from jax import lax
from jax.experimental import pallas as pl
from jax.experimental.pallas import tpu as pltpu

F32 = jnp.float32
BF16 = jnp.bfloat16
MESH = pl.DeviceIdType.MESH
N_DEV = 8

VMEM_LIMIT = 56 * 1024 * 1024

D = 1024
FF = 2816
FC = 2816
D_A = 512
D_B = 512
HD = 64
N_KV = 2
Q_PER_KV = 4
BLK = 128
N_GRP = 4
IN_COLS = 3840
PIECES = (("au", 0, 512), ("av", 512, 512), ("q", 1024, 512), ("k", 1536, 128), ("v", 1664, 128),
          ("ga", 1792, 1024), ("gb", 2816, 1024))
EPS = 1e-6
NEG = -1e30
GELU_C = math.sqrt(2.0 / math.pi)

ADAM_LR = 0.001
ADAM_B1 = 0.9
ADAM_B2 = 0.999
ADAM_EPS = 1e-08
ADAM_WD = 0.01
ADAM_STEP = 10

NT = (((1,), (1,)), ((), ()))
TN = (((0,), (0,)), ((), ()))

BIG = (("ffn1_w_gate", True, FF // N_DEV, D), ("ffn1_w_up", True, FF // N_DEV, D),
       ("ffn1_w_down", False, FF // N_DEV, D), ("w_in", True, IN_COLS // N_DEV, D),
       ("w_branch_a", True, D // N_DEV, D_A), ("w_branch_b", True, D // N_DEV, D_B), ("w_out", False, D // N_DEV, D),
       ("ffn2_w_gate", True, FF // N_DEV, D), ("ffn2_w_up", True, FF // N_DEV, D),
       ("ffn2_w_down", False, FF // N_DEV, D))
SMALL = ("b_ada", "g_norm1", "g_norm2", "g_sgu_ln", "b_sgu_ln", "w_spatial", "b_spatial", "g_q", "g_k",
         "attn_sinks", "g_norm3")


def _dot(a, b):
    return jnp.dot(a, b, preferred_element_type=F32)


def _dot_nt(a, b):
    return lax.dot_general(a, b, NT, preferred_element_type=F32)


def _dot_tn(a, b):
    return lax.dot_general(a, b, TN, preferred_element_type=F32)


def _vmem():
    return pl.BlockSpec(memory_space=pltpu.VMEM)


def _any():
    return pl.BlockSpec(memory_space=pl.ANY)


def _whole(a):
    return pl.BlockSpec(a.shape, lambda i: (0,) * len(a.shape))


def _rms_mod(h, g, sh, sc):
    inv = lax.rsqrt(jnp.mean(h * h, axis=-1, keepdims=True) + EPS)
    r = h * inv
    return (r * g) * (1.0 + sc) + sh, r, inv


def _rms_mod_bwd(dxn, r, inv, g, sc):
    dr = dxn * (g * (1.0 + sc))
    dh = inv * (dr - r * jnp.mean(dr * r, axis=-1, keepdims=True))
    return dh, jnp.sum(dxn, axis=0, keepdims=True), jnp.sum(dxn * r, axis=0, keepdims=True)


def _gelu(x):
    t = jnp.tanh(GELU_C * (x + 0.044715 * (x * x * x)))
    return 0.5 * x * (1.0 + t), t


def _gelu_grad(x, t):
    return 0.5 * (1.0 + t) + 0.5 * x * (1.0 - t * t) * (GELU_C * (1.0 + 3.0 * 0.044715 * x * x))


def _adamw(w, g, m, v):
    m = ADAM_B1 * m + (1.0 - ADAM_B1) * g
    v = ADAM_B2 * v + (1.0 - ADAM_B2) * (g * g)
    m_hat = m / (1.0 - ADAM_B1 ** ADAM_STEP)
    v_hat = v / (1.0 - ADAM_B2 ** ADAM_STEP)
    delta = -ADAM_LR * (m_hat / (jnp.sqrt(v_hat) + ADAM_EPS) + ADAM_WD * w)
    return delta, m, v


def _token_tile(seq, cap=512):
    return min(cap, seq)


def _params(*semantics):
    return pltpu.CompilerParams(dimension_semantics=semantics, vmem_limit_bytes=VMEM_LIMIT)


def _place():
    return lax.axis_index("x"), lax.axis_index("y"), lax.axis_index("c")


class _Gather:
    def __init__(self, arrays):
        n = len(arrays)
        self.ins = list(arrays)
        self.out_shape = [jax.ShapeDtypeStruct((N_DEV,) + a.shape, a.dtype) for a in arrays]
        self.scratch = [pltpu.SemaphoreType.DMA((n, 7)), pltpu.SemaphoreType.DMA((n, 7)),
                        pltpu.SemaphoreType.DMA((n,))]

    def _copies(self, ins, outs, sems):
        send_sems, recv_sems, local_sems = sems
        n = len(ins)
        x, y, c = _place()
        me, sibling = (x, y, c), (x, y, 1 - c)
        chips = [(1 - x, y), (x, 1 - y), (1 - x, 1 - y)]

        def slot(a, px, py, pc):
            return outs[a].at[4 * px + 2 * py + pc]

        def copy(a, k, block, to, src=None):
            return pltpu.make_async_remote_copy(
                src_ref=slot(a, *block) if src is None else src, dst_ref=slot(a, *block),
                send_sem=send_sems.at[a, k], recv_sem=recv_sems.at[a, k], device_id=to, device_id_type=MESH)

        mine = [pltpu.make_async_copy(ins[a], slot(a, *me), local_sems.at[a]) for a in range(n)]
        first = [copy(a, 0, me, sibling, src=ins[a]) for a in range(n)]
        first += [copy(a, 1 + j, me, (*chip, c), src=ins[a]) for a in range(n) for j, chip in enumerate(chips)]
        landed = [[copy(a, 1 + j, (*chip, c), me) for a in range(n)] for j, chip in enumerate(chips)]
        passed = [[copy(a, 4 + j, (*chip, c), sibling) for a in range(n)] for j, chip in enumerate(chips)]
        from_sibling = [copy(a, 0, sibling, me) for a in range(n)]
        from_sibling += [copy(a, 4 + j, (*chip, 1 - c), me) for a in range(n) for j, chip in enumerate(chips)]
        return mine, first, landed, passed, from_sibling

    def start(self, ins, outs, sems):
        mine, first, _, _, _ = self._copies(ins, outs, sems)
        for cp in mine + first:
            cp.start()

    def relay(self, ins, outs, sems):
        _, _, landed, passed, _ = self._copies(ins, outs, sems)
        for arrivals, forwards in zip(landed, passed):
            for arrival, forward in zip(arrivals, forwards):
                arrival.wait_recv()
                forward.start()

    def finish(self, ins, outs, sems):
        mine, first, _, passed, from_sibling = self._copies(ins, outs, sems)
        for cp in from_sibling:
            cp.wait_recv()
        for cp in first + [f for fs in passed for f in fs]:
            cp.wait_send()
        for cp in mine:
            cp.wait()


class _GatherRelayed:
    def __init__(self, arrays):
        n = len(arrays)
        self.ins = list(arrays)
        self.out_shape = [jax.ShapeDtypeStruct((N_DEV,) + a.shape, a.dtype) for a in arrays]
        self.scratch = [pltpu.SemaphoreType.DMA((n, 7)), pltpu.SemaphoreType.DMA((n, 7)),
                        pltpu.SemaphoreType.DMA((n,))]

    def _copies(self, ins, outs, sems):
        send_sems, recv_sems, local_sems = sems
        n = len(ins)
        x, y, c = _place()
        me, sibling = (x, y, c), (x, y, 1 - c)
        south = c == 0
        via = (jnp.where(south, 1 - x, x), jnp.where(south, y, 1 - y))
        onward = (jnp.where(south, x, 1 - x), jnp.where(south, 1 - y, y))
        via_sem = jnp.where(south, 1, 2)

        def slot(a, px, py, pc):
            return outs[a].at[4 * px + 2 * py + pc]

        def copy(a, k, block, to, src=None):
            return pltpu.make_async_remote_copy(
                src_ref=slot(a, *block) if src is None else src, dst_ref=slot(a, *block),
                send_sem=send_sems.at[a, k], recv_sem=recv_sems.at[a, k], device_id=to, device_id_type=MESH)

        chips = [(1 - x, y), (x, 1 - y), (1 - x, 1 - y)]
        mine = [pltpu.make_async_copy(ins[a], slot(a, *me), local_sems.at[a]) for a in range(n)]
        first = [copy(a, 0, me, sibling, src=ins[a]) for a in range(n)]
        first += [copy(a, 1 + j, me, (*chips[j], c), src=ins[a]) for a in range(n) for j in range(2)]
        to_pass = [copy(a, via_sem, (*via, c), me) for a in range(n)]
        passed_on = [copy(a, 3, (*via, c), (*onward, c)) for a in range(n)]
        landed = [[copy(a, 1 + j, (*chips[j], c), me) for a in range(n)] for j in range(3)]
        to_sibling = [[copy(a, 4 + j, (*chips[j], c), sibling) for a in range(n)] for j in range(3)]
        from_sibling = [copy(a, 0, sibling, me) for a in range(n)]
        from_sibling += [copy(a, 4 + j, (*chips[j], 1 - c), me) for a in range(n) for j in range(3)]
        return mine, first, to_pass, passed_on, landed, to_sibling, from_sibling

    def start(self, ins, outs, sems):
        mine, first, _, _, _, _, _ = self._copies(ins, outs, sems)
        for cp in mine + first:
            cp.start()

    def relay(self, ins, outs, sems):
        _, _, to_pass, passed_on, landed, to_sibling, _ = self._copies(ins, outs, sems)
        for arrival, onward in zip(to_pass, passed_on):
            arrival.wait_recv()
            onward.start()
        x, y, c = _place()
        for j in range(3):
            for a, (arrival, forward) in enumerate(zip(landed[j], to_sibling[j])):
                if j < 2:
                    pl.when((c == 0) != (j == 0))(arrival.wait_recv)
                else:
                    arrival.wait_recv()
                forward.start()

    def finish(self, ins, outs, sems):
        mine, first, _, passed_on, _, to_sibling, from_sibling = self._copies(ins, outs, sems)
        for cp in from_sibling:
            cp.wait_recv()
        for cp in first + passed_on + [f for fs in to_sibling for f in fs]:
            cp.wait_send()
        for cp in mine:
            cp.wait()


class _GatherDirect:
    def __init__(self, arrays):
        n = len(arrays)
        self.ins = list(arrays)
        self.out_shape = [jax.ShapeDtypeStruct((N_DEV,) + a.shape, a.dtype) for a in arrays]
        self.scratch = [pltpu.SemaphoreType.DMA((n, 7)), pltpu.SemaphoreType.DMA((n, 7)),
                        pltpu.SemaphoreType.DMA((n,))]

    def _copies(self, ins, outs, sems):
        send_sems, recv_sems, local_sems = sems
        n = len(ins)
        x, y, c = _place()
        peers = [(x ^ (k >> 2 & 1), y ^ (k >> 1 & 1), c ^ (k & 1)) for k in range(1, N_DEV)]

        def slot(a, px, py, pc):
            return outs[a].at[4 * px + 2 * py + pc]

        def copy(a, k, block, to, src=None):
            return pltpu.make_async_remote_copy(
                src_ref=slot(a, *block) if src is None else src, dst_ref=slot(a, *block),
                send_sem=send_sems.at[a, k], recv_sem=recv_sems.at[a, k], device_id=to, device_id_type=MESH)

        mine = [pltpu.make_async_copy(ins[a], slot(a, x, y, c), local_sems.at[a]) for a in range(n)]
        sends = [copy(a, k, (x, y, c), peer, src=ins[a]) for a in range(n) for k, peer in enumerate(peers)]
        arrivals = [copy(a, k, peer, (x, y, c)) for a in range(n) for k, peer in enumerate(peers)]
        return mine, sends, arrivals

    def start(self, ins, outs, sems):
        mine, sends, _ = self._copies(ins, outs, sems)
        for cp in mine + sends:
            cp.start()

    def relay(self, ins, outs, sems):
        pass

    def finish(self, ins, outs, sems):
        mine, sends, arrivals = self._copies(ins, outs, sems)
        for cp in arrivals:
            cp.wait_recv()
        for cp in sends:
            cp.wait_send()
        for cp in mine:
            cp.wait()


class _RsSibling:
    def __init__(self, ps):
        n = len(ps)
        self.ins = list(ps)
        self.out_shape = [jax.ShapeDtypeStruct((4,) + p.shape[1:], p.dtype) for p in ps]
        self.scratch = [pltpu.SemaphoreType.DMA((n, 4)), pltpu.SemaphoreType.DMA((n, 4))]

    def _copies(self, ins, outs, sems):
        send_sems, recv_sems = sems
        x, y, c = _place()
        return [pltpu.make_async_remote_copy(
            src_ref=ins[a].at[2 * q + (1 - c)], dst_ref=outs[a].at[q], send_sem=send_sems.at[a, q],
            recv_sem=recv_sems.at[a, q], device_id=(x, y, 1 - c), device_id_type=MESH)
            for a in range(len(ins)) for q in range(4)]

    def start(self, ins, outs, sems):
        for cp in self._copies(ins, outs, sems):
            cp.start()

    def relay(self, ins, outs, sems):
        pass

    def finish(self, ins, outs, sems):
        for cp in self._copies(ins, outs, sems):
            cp.wait()


class _RsChips:
    def __init__(self, qs):
        n = len(qs)
        self.ins = list(qs)
        self.out_shape = [jax.ShapeDtypeStruct(q.shape, q.dtype) for q in qs]
        self.scratch = [pltpu.SemaphoreType.DMA((n, 3)), pltpu.SemaphoreType.DMA((n, 3)),
                        pltpu.SemaphoreType.DMA((n,))]

    def _copies(self, ins, outs, sems):
        send_sems, recv_sems, local_sems = sems
        n = len(ins)
        x, y, c = _place()
        my_chip = 2 * x + y
        chips = [(1 - x, y), (x, 1 - y), (1 - x, 1 - y)]

        def copy(a, j, src_slot, dst_slot):
            px, py = chips[j]
            return pltpu.make_async_remote_copy(
                src_ref=ins[a].at[src_slot], dst_ref=outs[a].at[dst_slot], send_sem=send_sems.at[a, j],
                recv_sem=recv_sems.at[a, j], device_id=(px, py, c), device_id_type=MESH)

        own = [pltpu.make_async_copy(ins[a].at[my_chip], outs[a].at[my_chip], local_sems.at[a]) for a in range(n)]
        sends = [copy(a, j, 2 * px + py, my_chip) for a in range(n) for j, (px, py) in enumerate(chips)]
        arrivals = [copy(a, j, my_chip, 2 * px + py) for a in range(n) for j, (px, py) in enumerate(chips)]
        return own, sends, arrivals

    def start(self, ins, outs, sems):
        own, sends, _ = self._copies(ins, outs, sems)
        for cp in own + sends:
            cp.start()

    def relay(self, ins, outs, sems):
        pass

    def finish(self, ins, outs, sems):
        own, sends, arrivals = self._copies(ins, outs, sems)
        for cp in arrivals:
            cp.wait_recv()
        for cp in sends:
            cp.wait_send()
        for cp in own:
            cp.wait()


def _split_plans(plans, refs_in, refs_out, refs_scr, phase):
    i = o = s = 0
    for p in plans:
        ni, no, ns = len(p.ins), len(p.out_shape), len(p.scratch)
        getattr(p, phase)(refs_in[i:i + ni], refs_out[o:o + no], refs_scr[s:s + ns])
        i, o, s = i + ni, o + no, s + ns


def _plan_results(plans, res):
    out = []
    for p in plans:
        out.append(list(res[:len(p.out_shape)]))
        res = res[len(p.out_shape):]
    return out


def _exchange(plans, name):
    c_in = [a for p in plans for a in p.ins]
    c_out = [s for p in plans for s in p.out_shape]
    c_scr = [s for p in plans for s in p.scratch]

    def body(*refs):
        cin, cout, cscr = refs[:len(c_in)], refs[len(c_in):len(c_in) + len(c_out)], refs[len(c_in) + len(c_out):]
        for phase in ("start", "relay", "finish"):
            _split_plans(plans, cin, cout, cscr, phase)

    res = pl.pallas_call(body, name=name, in_specs=[_any()] * len(c_in), out_specs=[_any()] * len(c_out),
                         out_shape=c_out, scratch_shapes=c_scr)(*c_in)
    return _plan_results(plans, res)


def _call(body, *, name, grid, in_specs, out_specs, out_shape, args, semantics, scratch_shapes=(), plans=()):
    n_in, n_out, n_scr = len(in_specs), len(out_specs), len(scratch_shapes)
    c_in = [a for p in plans for a in p.ins]
    c_out = [s for p in plans for s in p.out_shape]
    c_scr = [s for p in plans for s in p.scratch]
    n_steps = math.prod(grid)

    def wrapped(*refs):
        ins, refs = refs[:n_in], refs[n_in:]
        cin, refs = refs[:len(c_in)], refs[len(c_in):]
        outs, refs = refs[:n_out], refs[n_out:]
        cout, refs = refs[:len(c_out)], refs[len(c_out):]
        scr, cscr = refs[:n_scr], refs[n_scr:]
        if plans:
            step = 0
            for d, g in enumerate(grid):
                step = step * g + pl.program_id(d)
            pl.when(step == 0)(lambda: _split_plans(plans, cin, cout, cscr, "start"))
        body(*ins, *outs, *scr)
        if plans:
            pl.when(step == max(n_steps - 2, 0))(lambda: _split_plans(plans, cin, cout, cscr, "relay"))
            pl.when(step == n_steps - 1)(lambda: _split_plans(plans, cin, cout, cscr, "finish"))

    res = pl.pallas_call(
        wrapped, name=name, grid=grid, in_specs=list(in_specs) + [_any()] * len(c_in),
        out_specs=list(out_specs) + [_any()] * len(c_out), out_shape=list(out_shape) + c_out,
        scratch_shapes=list(scratch_shapes) + c_scr,
        compiler_params=_params(*(("arbitrary",) * len(grid) if plans else semantics)),
    )(*args, *c_in)
    return list(res[:n_out]), _plan_results(plans, res[n_out:])


def _ffn_fwd_loss(h, mods, gn, wg, wu, wd, row0, seq, tgt):
    T = h.shape[0]
    tm = _token_tile(seq, 256)
    tps = seq // tm
    n_t = T // tm

    def body(h_ref, m_ref, g_ref, wg_ref, wu_ref, wd_ref, tgt_ref, out_ref, y_ref, gb_ref, ub_ref, loss_ref):
        hv = h_ref[...]
        sh = m_ref[0, row0:row0 + 1, :]
        sc = m_ref[0, row0 + 1:row0 + 2, :]
        ga = m_ref[0, row0 + 2:row0 + 3, :]
        xn, _, _ = _rms_mod(hv, g_ref[...], sh, sc)
        xb = xn.astype(BF16)
        acc = jnp.zeros((tm, D), F32)
        for c0 in range(0, FF, FC):
            gg = _dot_nt(xb, wg_ref[c0:c0 + FC, :])
            uu = _dot_nt(xb, wu_ref[c0:c0 + FC, :])
            gg, uu = gg.astype(BF16), uu.astype(BF16)
            gb_ref[:, c0:c0 + FC] = gg
            ub_ref[:, c0:c0 + FC] = uu
            acc = acc + _dot((gg * jax.nn.sigmoid(gg)) * uu, wd_ref[c0:c0 + FC, :])
        y_ref[...] = acc
        d = hv + (0.5 * ga) * acc - tgt_ref[...]
        out_ref[...] = d * (1.0 / D)
        loss_ref[...] = jnp.full((1, 8, 128), 0.5 / D, F32) * jnp.sum(d * d)

    tok = pl.BlockSpec((tm, D), lambda i: (i, 0))
    tokf = pl.BlockSpec((tm, FF), lambda i: (i, 0))
    outs, _ = _call(
        body, name="ffn_fwd_loss", grid=(n_t,),
        in_specs=[tok, pl.BlockSpec((1, 9, D), lambda i: (i // tps, 0, 0)), pl.BlockSpec((1, D), lambda i: (0, 0)),
                  _vmem(), _vmem(), _vmem(), tok],
        out_specs=[tok, tok, tokf, tokf, pl.BlockSpec((1, 8, 128), lambda i: (i, 0, 0))],
        out_shape=[jax.ShapeDtypeStruct((T, D), F32), jax.ShapeDtypeStruct((T, D), F32),
                   jax.ShapeDtypeStruct((T, FF), BF16), jax.ShapeDtypeStruct((T, FF), BF16),
                   jax.ShapeDtypeStruct((n_t, 8, 128), F32)],
        args=(h, mods, gn, wg, wu, wd, tgt), semantics=("parallel",))
    return outs


def _ffn_gate_up(h, mods, gn, wg, wu, row0, seq, plans=()):
    T = h.shape[0]
    tm = _token_tile(seq, 512)
    tps = seq // tm

    def body(h_ref, m_ref, g_ref, wg_ref, wu_ref, gb_ref, ub_ref):
        xn, _, _ = _rms_mod(h_ref[...], g_ref[...], m_ref[0, row0:row0 + 1, :], m_ref[0, row0 + 1:row0 + 2, :])
        xb = xn.astype(BF16)
        for c0 in range(0, FF, FC):
            gb_ref[:, c0:c0 + FC] = _dot_nt(xb, wg_ref[c0:c0 + FC, :]).astype(BF16)
            ub_ref[:, c0:c0 + FC] = _dot_nt(xb, wu_ref[c0:c0 + FC, :]).astype(BF16)

    tokf = pl.BlockSpec((tm, FF), lambda i: (i, 0))
    return _call(
        body, name="ffn_gate_up", grid=(T // tm,),
        in_specs=[pl.BlockSpec((tm, D), lambda i: (i, 0)), pl.BlockSpec((1, 9, D), lambda i: (i // tps, 0, 0)),
                  pl.BlockSpec((1, D), lambda i: (0, 0)), _vmem(), _vmem()],
        out_specs=[tokf, tokf], out_shape=[jax.ShapeDtypeStruct((T, FF), BF16)] * 2, args=(h, mods, gn, wg, wu),
        semantics=("parallel",), plans=plans)


def _ffn_down(h, gb, ub, mods, wd, row0, seq, plans=()):
    T = h.shape[0]
    tm = _token_tile(seq, 512)
    tps = seq // tm

    def body(h_ref, gb_ref, ub_ref, m_ref, wd_ref, out_ref, y_ref):
        acc = jnp.zeros((tm, D), F32)
        for c0 in range(0, FF, FC):
            gg = gb_ref[:, c0:c0 + FC]
            acc = acc + _dot((gg * jax.nn.sigmoid(gg)) * ub_ref[:, c0:c0 + FC], wd_ref[c0:c0 + FC, :])
        y_ref[...] = acc
        out_ref[...] = h_ref[...] + (0.5 * m_ref[0, row0 + 2:row0 + 3, :]) * acc

    tok = pl.BlockSpec((tm, D), lambda i: (i, 0))
    tokf = pl.BlockSpec((tm, FF), lambda i: (i, 0))
    return _call(
        body, name="ffn_down", grid=(T // tm,),
        in_specs=[tok, tokf, tokf, pl.BlockSpec((1, 9, D), lambda i: (i // tps, 0, 0)), _vmem()],
        out_specs=[tok, tok], out_shape=[jax.ShapeDtypeStruct((T, D), F32)] * 2, args=(h, gb, ub, mods, wd),
        semantics=("parallel",), plans=plans)


def _ffn_bwd(h, dhn, y, gb, ub, mods, gn, wg, wu, wd, row0, seq, plans=()):
    T = h.shape[0]
    B = T // seq
    tm = _token_tile(seq, 256)
    tps = seq // tm
    n_t = T // tm

    def body(h_ref, dhn_ref, y_ref, gb_ref, ub_ref, m_ref, g_ref, wg_ref, wu_ref, wd_ref,
             dh_ref, xb_ref, dyb_ref, a_ref, dg_ref, du_ref, mg_ref):
        i = pl.program_id(0)
        hv = h_ref[...]
        dhn = dhn_ref[...]
        sh = m_ref[0, row0:row0 + 1, :]
        sc = m_ref[0, row0 + 1:row0 + 2, :]
        ga = m_ref[0, row0 + 2:row0 + 3, :]
        g = g_ref[...]
        xn, r, inv = _rms_mod(hv, g, sh, sc)
        xb_ref[...] = xn.astype(BF16)
        dyb = ((0.5 * ga) * dhn).astype(BF16)
        dyb_ref[...] = dyb
        dga = 0.5 * jnp.sum(dhn * y_ref[...], axis=0, keepdims=True)
        dxn = jnp.zeros((tm, D), F32)
        for c0 in range(0, FF, FC):
            wgc = wg_ref[c0:c0 + FC, :]
            wuc = wu_ref[c0:c0 + FC, :]
            gg = gb_ref[:, c0:c0 + FC]
            uu = ub_ref[:, c0:c0 + FC]
            sig = jax.nn.sigmoid(gg)
            s = gg * sig
            a_ref[:, c0:c0 + FC] = s * uu
            da = _dot_nt(dyb, wd_ref[c0:c0 + FC, :]).astype(BF16)
            dub = da * s
            dgb = (da * uu) * (sig * (1.0 + gg * (1.0 - sig)))
            dg_ref[:, c0:c0 + FC] = dgb
            du_ref[:, c0:c0 + FC] = dub
            dxn = dxn + _dot(dgb, wgc) + _dot(dub, wuc)
        dh, s_dxn, s_dxr = _rms_mod_bwd(dxn, r, inv, g, sc)
        dh_ref[...] = dhn + dh

        @pl.when(i % tps == 0)
        def _():
            mg_ref[...] = jnp.zeros(mg_ref.shape, F32)

        mg_ref[0, 0:1, :] += s_dxn
        mg_ref[0, 1:2, :] += s_dxr
        mg_ref[0, 2:3, :] += dga

    tok = pl.BlockSpec((tm, D), lambda i: (i, 0))
    tokf = pl.BlockSpec((tm, FF), lambda i: (i, 0))
    return _call(
        body, name="ffn_bwd", grid=(n_t,),
        in_specs=[tok, tok, tok, tokf, tokf, pl.BlockSpec((1, 9, D), lambda i: (i // tps, 0, 0)),
                  pl.BlockSpec((1, D), lambda i: (0, 0)), _vmem(), _vmem(), _vmem()],
        out_specs=[tok, tok, tok, tokf, tokf, tokf, pl.BlockSpec((1, 8, D), lambda i: (i // tps, 0, 0))],
        out_shape=[jax.ShapeDtypeStruct((T, D), F32), jax.ShapeDtypeStruct((T, D), BF16),
                   jax.ShapeDtypeStruct((T, D), BF16), jax.ShapeDtypeStruct((T, FF), BF16),
                   jax.ShapeDtypeStruct((T, FF), BF16), jax.ShapeDtypeStruct((T, FF), BF16),
                   jax.ShapeDtypeStruct((B, 8, D), F32)],
        args=(h, dhn, y, gb, ub, mods, gn, wg, wu, wd), semantics=("arbitrary",), plans=plans)


def _wgrad(a, b, name, plans=()):
    T, da = a.shape
    db = b.shape[1]
    bm = {2816: 1408, 3840: 1280}[da]
    bn = db
    tk = min(2048, T)
    nk = T // tk

    def body(a_ref, b_ref, o_ref):
        @pl.when(pl.program_id(2) == 0)
        def _():
            o_ref[...] = jnp.zeros(o_ref.shape, F32)

        o_ref[...] += _dot_tn(a_ref[...], b_ref[...])

    (out,), plan_outs = _call(
        body, name=name, grid=(da // bm, db // bn, nk),
        in_specs=[pl.BlockSpec((tk, bm), lambda i, j, k: (k, i)), pl.BlockSpec((tk, bn), lambda i, j, k: (k, j))],
        out_specs=[pl.BlockSpec((bm, bn), lambda i, j, k: (i, j))], out_shape=[jax.ShapeDtypeStruct((da, db), F32)],
        args=(a, b), semantics=("parallel", "parallel", "arbitrary"), plans=plans)
    return out, plan_outs


def _inproj_fwd(h, mods, gn, w_in, seq, plans=()):
    T = h.shape[0]
    tm = _token_tile(seq)
    tps = seq // tm

    def body(h_ref, m_ref, g_ref, w_ref, *outs):
        xn, _, _ = _rms_mod(h_ref[...], g_ref[...], m_ref[0, 3:4, :], m_ref[0, 4:5, :])
        xb = xn.astype(BF16)
        for (_, c0, w), o_ref in zip(PIECES, outs):
            o_ref[...] = _dot_nt(xb, w_ref[c0:c0 + w, :])

    return _call(
        body, name="inproj_fwd", grid=(T // tm,),
        in_specs=[pl.BlockSpec((tm, D), lambda i: (i, 0)), pl.BlockSpec((1, 9, D), lambda i: (i // tps, 0, 0)),
                  pl.BlockSpec((1, D), lambda i: (0, 0)), _vmem()],
        out_specs=[pl.BlockSpec((tm, w), lambda i: (i, 0)) for _, _, w in PIECES],
        out_shape=[jax.ShapeDtypeStruct((T, w), F32) for _, _, w in PIECES], args=(h, mods, gn, w_in),
        semantics=("parallel",), plans=plans)


def _inproj_bwd(h, dh_res, dpieces, mods, gn, w_in, seq, plans=()):
    T = h.shape[0]
    B = T // seq
    tm = _token_tile(seq, 256)
    tps = seq // tm

    def body(h_ref, dres_ref, *rest):
        dp_refs = rest[:len(PIECES)]
        m_ref, g_ref, w_ref, dh_ref, xb_ref, dpb_ref, mg_ref = rest[len(PIECES):]
        i = pl.program_id(0)
        g = g_ref[...]
        sc = m_ref[0, 4:5, :]
        xn, r, inv = _rms_mod(h_ref[...], g, m_ref[0, 3:4, :], sc)
        xb_ref[...] = xn.astype(BF16)
        dxn = jnp.zeros((tm, D), F32)
        for (_, c0, w), dp_ref in zip(PIECES, dp_refs):
            dpb = dp_ref[...].astype(BF16)
            dpb_ref[:, c0:c0 + w] = dpb
            dxn = dxn + _dot(dpb, w_ref[c0:c0 + w, :])
        dh, s_dxn, s_dxr = _rms_mod_bwd(dxn, r, inv, g, sc)
        dh_ref[...] = dres_ref[...] + dh

        @pl.when(i % tps == 0)
        def _():
            mg_ref[...] = jnp.zeros(mg_ref.shape, F32)

        mg_ref[0, 0:1, :] += s_dxn
        mg_ref[0, 1:2, :] += s_dxr

    tok = pl.BlockSpec((tm, D), lambda i: (i, 0))
    return _call(
        body, name="inproj_bwd", grid=(T // tm,),
        in_specs=[tok, tok] + [pl.BlockSpec((tm, w), lambda i: (i, 0)) for _, _, w in PIECES]
        + [pl.BlockSpec((1, 9, D), lambda i: (i // tps, 0, 0)), pl.BlockSpec((1, D), lambda i: (0, 0)), _vmem()],
        out_specs=[tok, tok, pl.BlockSpec((tm, IN_COLS), lambda i: (i, 0)),
                   pl.BlockSpec((1, 8, D), lambda i: (i // tps, 0, 0))],
        out_shape=[jax.ShapeDtypeStruct((T, D), F32), jax.ShapeDtypeStruct((T, D), BF16),
                   jax.ShapeDtypeStruct((T, IN_COLS), BF16), jax.ShapeDtypeStruct((B, 8, D), F32)],
        args=(h, dh_res, *dpieces, mods, gn, w_in), semantics=("arbitrary",), plans=plans)


def _seg_mean(x):
    i = lax.broadcasted_iota(jnp.int32, (128, 128), 0) >> 6
    j = lax.broadcasted_iota(jnp.int32, (128, 128), 1) >> 6
    ones = jnp.where(i == j, 1.0 / HD, 0.0).astype(BF16)
    hi = x.astype(BF16)
    lo = (x - hi.astype(F32)).astype(BF16)
    return _dot(hi, ones) + _dot(lo, ones)


def _head_norm(x, g2):
    inv = lax.rsqrt(_seg_mean(x * x) + EPS)
    y = x * inv
    return y * g2, y, inv


def _head_norm_bwd(dxn, y, inv, g2):
    dy = dxn * g2
    return inv * (dy - y * _seg_mean(dy * y)), jnp.sum(dxn * y, axis=0, keepdims=True)


def _swa_block(q, kk, vv, gq2, gk2, sinks, first, do=None):
    lo = lax.broadcasted_iota(jnp.int32, (1, 128), 1) < HD
    kn, ky, kinv = _head_norm(kk, gk2)

    def operands(x):
        xr = pltpu.roll(x, HD, 1)
        own_lo, own_hi = jnp.where(lo, x, 0.0).astype(BF16), jnp.where(lo, 0.0, x).astype(BF16)
        rolled_lo, rolled_hi = jnp.where(lo, xr, 0.0).astype(BF16), jnp.where(lo, 0.0, xr).astype(BF16)
        return (own_lo, rolled_hi), (rolled_lo, own_hi)

    def restore(parts):
        (own_lo, rolled_hi), (rolled_lo, own_hi) = parts
        return (jnp.where(lo, own_lo, own_hi)
                + pltpu.roll(jnp.where(lo, rolled_lo, rolled_hi), HD, 1))

    k_ops, v_ops = operands(kn), operands(vv)
    k2 = [jnp.concatenate(pair, axis=0) for pair in k_ops]
    def stack(x):
        return jnp.concatenate([x[:, 128 * p:128 * (p + 1)] for p in range(4)], axis=0)

    def unstack(x):
        return jnp.concatenate([x[BLK * p:BLK * (p + 1)] for p in range(4)], axis=1)

    def of_head(x, kh):
        return x[2 * BLK * kh:2 * BLK * (kh + 1)]

    nq = 4 * BLK
    pair = lax.broadcasted_iota(jnp.int32, (nq, 1), 0) >> 7
    row = lax.broadcasted_iota(jnp.int32, (nq, 2 * BLK), 0) & (BLK - 1)
    col = lax.broadcasted_iota(jnp.int32, (nq, 2 * BLK), 1)
    valid = (col <= row + BLK) & (col > row) & (col >= jnp.where(first, BLK, 0))
    scale = HD ** -0.5
    qn, qy, qinv = _head_norm(stack(q), gq2)
    qnb = qn.astype(BF16)
    s2 = jnp.concatenate([_dot_nt(of_head(qnb, kh), k2[kh]) for kh in range(N_KV)], axis=0) * scale
    probs, p_sink = [], []
    for j in range(2):
        s = jnp.where(valid, s2[:, 2 * BLK * j:2 * BLK * (j + 1)], NEG)
        sink = jnp.zeros((nq, 1), F32)
        for p in range(4):
            sink = jnp.where(pair == p, sinks[:, 2 * p + j:2 * p + j + 1], sink)
        m = jnp.maximum(jnp.max(s, axis=-1, keepdims=True), sink)
        e = jnp.exp(s - m)
        e_sink = jnp.exp(sink - m)
        rden = 1.0 / (jnp.sum(e, axis=-1, keepdims=True) + e_sink)
        probs.append(e * rden)
        p_sink.append(e_sink * rden)
    pb = [p.astype(BF16) for p in probs]
    if do is None:
        return unstack(jnp.concatenate(
            [_dot(of_head(pb[0], kh), v_ops[kh][0]) + _dot(of_head(pb[1], kh), v_ops[kh][1]) for kh in range(N_KV)],
            axis=0))
    dob = stack(do).astype(BF16)
    ds, dsinks = [], [None] * 8
    for j in range(2):
        dp = jnp.concatenate([_dot_nt(of_head(dob, kh), v_ops[kh][j]) for kh in range(N_KV)], axis=0)
        t = jnp.sum(dp * probs[j], axis=-1, keepdims=True)
        ds.append(probs[j] * (dp - t) * scale)
        lost = p_sink[j] * t
        for p in range(4):
            dsinks[2 * p + j] = -jnp.sum(lost[BLK * p:BLK * (p + 1)])
    dsb = jnp.concatenate(ds, axis=1).astype(BF16)
    dqn = jnp.concatenate([_dot(of_head(dsb, kh), k2[kh]) for kh in range(N_KV)], axis=0)
    dq, dgq2 = _head_norm_bwd(dqn, qy, qinv, gq2)
    dk2 = [_dot_tn(of_head(dsb, kh), of_head(qnb, kh)) for kh in range(N_KV)]
    dv_ops = [[_dot_tn(of_head(pb[j], kh), of_head(dob, kh)) for j in range(2)] for kh in range(N_KV)]
    dkn = restore(tuple((d[:2 * BLK], d[2 * BLK:]) for d in dk2))
    dvv = restore(tuple(tuple(d) for d in dv_ops))
    dkk, dgk2 = _head_norm_bwd(dkn, ky, kinv, gk2)
    return unstack(dq), dkk, dvv, dgq2, dgk2, dsinks


SWA_GROUP = 8


def _swa_specs(nbs):
    grp = min(SWA_GROUP, nbs)
    rows = grp * BLK

    def tok(w):
        return pl.BlockSpec((rows, w), lambda i: (i, 0))

    halo = pl.BlockSpec((BLK, 128), lambda i: (jnp.maximum(i * grp - 1, 0), 0))
    vec = pl.BlockSpec((1, 128), lambda i: (0, 0))
    sk = pl.BlockSpec((1, 8), lambda i: (0, 0))
    return grp, tok, halo, vec, sk


def _swa_fwd(q, k, v, gq2, gk2, sinks, nbs, plans=()):
    T = q.shape[0]
    grp, tok, halo, vec, sk = _swa_specs(nbs)

    def body(q_ref, kh_ref, kc_ref, vh_ref, vc_ref, gq_ref, gk_ref, sk_ref, o_ref):
        seq_start = ((pl.program_id(0) * grp) % nbs) == 0
        for g in range(grp):
            rows = slice(g * BLK, (g + 1) * BLK)
            prev = slice((g - 1) * BLK, g * BLK)
            kk = jnp.concatenate([kh_ref[...] if g == 0 else kc_ref[prev, :], kc_ref[rows, :]], axis=0)
            vv = jnp.concatenate([vh_ref[...] if g == 0 else vc_ref[prev, :], vc_ref[rows, :]], axis=0)
            o_ref[rows, :] = _swa_block(q_ref[rows, :], kk, vv, gq_ref[...], gk_ref[...], sk_ref[...],
                                        seq_start if g == 0 else False)

    (out,), plan_outs = _call(
        body, name="swa_fwd", grid=(T // (grp * BLK),),
        in_specs=[tok(D_B), halo, tok(128), halo, tok(128), vec, vec, sk], out_specs=[tok(D_B)],
        out_shape=[jax.ShapeDtypeStruct((T, D_B), F32)], args=(q, k, k, v, v, gq2, gk2, sinks),
        semantics=("parallel",), plans=plans)
    return out, plan_outs


def _swa_bwd(q, k, v, gq2, gk2, sinks, do, nbs, plans=()):
    T = q.shape[0]
    grp, tok, halo, vec, sk = _swa_specs(nbs)
    steps = T // (grp * BLK)

    def body(q_ref, kh_ref, kc_ref, vh_ref, vc_ref, gq_ref, gk_ref, sk_ref, do_ref,
             dq_ref, dk_ref, dv_ref, dkh_ref, dvh_ref, dgq_ref, dgk_ref, dsk_ref):
        i = pl.program_id(0)
        seq_start = ((i * grp) % nbs) == 0

        @pl.when(i == 0)
        def _():
            for r in (dgq_ref, dgk_ref, dsk_ref):
                r[...] = jnp.zeros(r.shape, F32)

        res = []
        for g in range(grp):
            rows = slice(g * BLK, (g + 1) * BLK)
            prev = slice((g - 1) * BLK, g * BLK)
            kk = jnp.concatenate([kh_ref[...] if g == 0 else kc_ref[prev, :], kc_ref[rows, :]], axis=0)
            vv = jnp.concatenate([vh_ref[...] if g == 0 else vc_ref[prev, :], vc_ref[rows, :]], axis=0)
            res.append(_swa_block(q_ref[rows, :], kk, vv, gq_ref[...], gk_ref[...], sk_ref[...],
                                  seq_start if g == 0 else False, do=do_ref[rows, :]))
        lane = lax.broadcasted_iota(jnp.int32, (8, 128), 1)
        upd = jnp.zeros((8, 128), F32)
        for g, (dq, dkk, dvv, dgq2, dgk2, dsinks) in enumerate(res):
            rows = slice(g * BLK, (g + 1) * BLK)
            dq_ref[rows, :] = dq
            dk_ref[rows, :] = dkk[BLK:] + res[g + 1][1][:BLK] if g + 1 < grp else dkk[BLK:]
            dv_ref[rows, :] = dvv[BLK:] + res[g + 1][2][:BLK] if g + 1 < grp else dvv[BLK:]
            dgq_ref[...] += dgq2
            dgk_ref[...] += dgk2
            for h, d in enumerate(dsinks):
                upd = upd + jnp.where(lane == h, d, 0.0)
        dkh_ref[0] = res[0][1][:BLK]
        dvh_ref[0] = res[0][2][:BLK]
        dsk_ref[...] += upd

    one = pl.BlockSpec((1, BLK, 128), lambda i: (i, 0, 0))
    halo_shape = jax.ShapeDtypeStruct((steps, BLK, 128), F32)
    return _call(
        body, name="swa_bwd", grid=(steps,),
        in_specs=[tok(D_B), halo, tok(128), halo, tok(128), vec, vec, sk, tok(D_B)],
        out_specs=[tok(D_B), tok(128), tok(128), one, one, vec, vec, pl.BlockSpec((8, 128), lambda i: (0, 0))],
        out_shape=[jax.ShapeDtypeStruct((T, D_B), F32), jax.ShapeDtypeStruct((T, 128), F32),
                   jax.ShapeDtypeStruct((T, 128), F32), halo_shape, halo_shape, jax.ShapeDtypeStruct((1, 128), F32),
                   jax.ShapeDtypeStruct((1, 128), F32), jax.ShapeDtypeStruct((8, 128), F32)],
        args=(q, k, k, v, v, gq2, gk2, sinks, do), semantics=("arbitrary",), plans=plans)


def _swa_add_halo(dk, dk_halo):
    steps = dk_halo.shape[0]
    nxt = jnp.concatenate([dk_halo[1:], jnp.zeros_like(dk_halo[:1])], axis=0)[:, None]
    dk = dk.reshape(steps, -1, BLK, 128)
    return jnp.concatenate([dk[:, :-1], dk[:, -1:] + nxt], axis=1).reshape(-1, 128)


def _sgu_norm(av, g_ln, b_ln):
    t, th = _gelu(av)
    mu = jnp.mean(t, axis=-1, keepdims=True)
    tc = t - mu
    rstd = lax.rsqrt(jnp.mean(tc * tc, axis=-1, keepdims=True) + EPS)
    vhat = tc * rstd
    return vhat * g_ln + b_ln, vhat, rstd, th


def _masked_ws(ws_ref):
    tril = lax.broadcasted_iota(jnp.int32, (BLK, BLK), 0) >= lax.broadcasted_iota(jnp.int32, (BLK, BLK), 1)
    return [jnp.where(tril, ws_ref[g], 0.0).astype(BF16) for g in range(N_GRP)]


def _mix_fwd(au, av, gta, gtb, ob, h, mods, g_ln, b_ln, ws, bsb, wa, wb, wout, seq, plans=()):
    T = h.shape[0]
    tm = _token_tile(seq)
    tps = seq // tm

    def body(au_ref, av_ref, gta_ref, gtb_ref, ob_ref, h_ref, m_ref, gl_ref, bl_ref, ws_ref, bs_ref,
             wa_ref, wb_ref, wo_ref, out_ref, vvb_s, z_s):
        u, _ = _gelu(au_ref[...])
        vv, _, _, _ = _sgu_norm(av_ref[...], gl_ref[...], bl_ref[...])
        vvb_s[...] = vv.astype(BF16)
        wsm = _masked_ws(ws_ref)
        for c in range(tm // BLK):
            rows = slice(c * BLK, (c + 1) * BLK)
            for g in range(N_GRP):
                cols = slice(g * BLK, (g + 1) * BLK)
                z_s[rows, cols] = _dot(wsm[g], vvb_s[rows, cols]) + bs_ref[g]
        ya = _dot_nt((u * z_s[...]).astype(BF16), wa_ref[...])
        yb = _dot_nt(ob_ref[...].astype(BF16), wb_ref[...])
        merged = jax.nn.sigmoid(gta_ref[...]) * ya + jax.nn.sigmoid(gtb_ref[...]) * yb
        out_ref[...] = h_ref[...] + m_ref[0, 5:6, :] * _dot(merged.astype(BF16), wo_ref[...])

    def tok(w):
        return pl.BlockSpec((tm, w), lambda i: (i, 0))

    def full(shape):
        return pl.BlockSpec(shape, lambda i: (0,) * len(shape))

    (out,), plan_outs = _call(
        body, name="mix_fwd", grid=(T // tm,),
        in_specs=[tok(D_A), tok(D_A), tok(D), tok(D), tok(D_B), tok(D),
                  pl.BlockSpec((1, 9, D), lambda i: (i // tps, 0, 0)), full((1, D_A)), full((1, D_A)),
                  full((N_GRP, BLK, BLK)), full((N_GRP, BLK, BLK)), _vmem(), _vmem(), _vmem()],
        out_specs=[tok(D)], out_shape=[jax.ShapeDtypeStruct((T, D), F32)],
        scratch_shapes=[pltpu.VMEM((tm, D_A), BF16), pltpu.VMEM((tm, D_A), F32)],
        args=(au, av, gta, gtb, ob, h, mods, g_ln, b_ln, ws, bsb, wa, wb, wout), semantics=("parallel",),
        plans=plans)
    return out, plan_outs


def _mix_bwd(au, av, gta, gtb, ob, dh, mods, g_ln, b_ln, ws, bsb, wa, wb, wout, seq, plans=()):
    T = dh.shape[0]
    B = T // seq
    tm = _token_tile(seq, 256)
    tps = seq // tm

    def body(au_ref, av_ref, gta_ref, gtb_ref, ob_ref, dh_ref, m_ref, gl_ref, bl_ref, ws_ref, bs_ref,
             wa_ref, wb_ref, wo_ref,
             dau_ref, dav_ref, dgta_ref, dgtb_ref, dob_ref, dwo_ref, dwa_ref, dwb_ref, dws_ref, dbs_ref, dln_ref,
             mg_ref, vvb_s, z_s, dz_s, dzb_s, dvv_s):
        i = pl.program_id(0)

        @pl.when(i == 0)
        def _():
            for r in (dwo_ref, dwa_ref, dwb_ref, dws_ref, dbs_ref, dln_ref):
                r[...] = jnp.zeros(r.shape, F32)

        @pl.when(i % tps == 0)
        def _():
            mg_ref[...] = jnp.zeros(mg_ref.shape, F32)

        auv = au_ref[...]
        avv = av_ref[...]
        u, thu = _gelu(auv)
        g_ln = gl_ref[...]
        vv, vhat, rstd, thv = _sgu_norm(avv, g_ln, bl_ref[...])
        vvb_s[...] = vv.astype(BF16)
        wsm = _masked_ws(ws_ref)
        for c in range(tm // BLK):
            rows = slice(c * BLK, (c + 1) * BLK)
            for g in range(N_GRP):
                cols = slice(g * BLK, (g + 1) * BLK)
                z_s[rows, cols] = _dot(wsm[g], vvb_s[rows, cols]) + bs_ref[g]
        z = z_s[...]
        yab = (u * z).astype(BF16)
        obb = ob_ref[...].astype(BF16)
        ya = _dot_nt(yab, wa_ref[...])
        yb = _dot_nt(obb, wb_ref[...])
        sa = jax.nn.sigmoid(gta_ref[...])
        sb = jax.nn.sigmoid(gtb_ref[...])
        mb = (sa * ya + sb * yb).astype(BF16)
        dhv = dh_ref[...]
        mg_ref[0, 0:1, :] += jnp.sum(dhv * _dot(mb, wo_ref[...]), axis=0, keepdims=True)
        dmob = (m_ref[0, 5:6, :] * dhv).astype(BF16)
        dwo_ref[...] += _dot_tn(mb, dmob)
        dmerged = _dot_nt(dmob, wo_ref[...])
        dya = dmerged * sa
        dyb = dmerged * sb
        dgta_ref[...] = dya * ya * (1.0 - sa)
        dgtb_ref[...] = dyb * yb * (1.0 - sb)
        dyab = dya.astype(BF16)
        dybb = dyb.astype(BF16)
        dwa_ref[...] += _dot_tn(dyab, yab)
        dwb_ref[...] += _dot_tn(dybb, obb)
        dob_ref[...] = _dot(dybb, wb_ref[...])
        dyap = _dot(dyab, wa_ref[...])
        dau_ref[...] = (dyap * z) * _gelu_grad(auv, thu)
        dz = dyap * u
        dz_s[...] = dz
        dzb_s[...] = dz.astype(BF16)
        for c in range(tm // BLK):
            rows = slice(c * BLK, (c + 1) * BLK)
            for g in range(N_GRP):
                cols = slice(g * BLK, (g + 1) * BLK)
                dzb = dzb_s[rows, cols]
                dvv_s[rows, cols] = _dot_tn(wsm[g], dzb)
                dws_ref[g] += _dot_nt(dzb, vvb_s[rows, cols])
                dbs_ref[g] += dz_s[rows, cols]
        dvv = dvv_s[...]
        dln_ref[0:1, :] += jnp.sum(dvv * vhat, axis=0, keepdims=True)
        dln_ref[1:2, :] += jnp.sum(dvv, axis=0, keepdims=True)
        dvh = dvv * g_ln
        dt = rstd * (dvh - jnp.mean(dvh, axis=-1, keepdims=True)
                     - vhat * jnp.mean(dvh * vhat, axis=-1, keepdims=True))
        dav_ref[...] = dt * _gelu_grad(avv, thv)

    def tok(w):
        return pl.BlockSpec((tm, w), lambda i: (i, 0))

    def full(shape):
        return pl.BlockSpec(shape, lambda i: (0,) * len(shape))

    return _call(
        body, name="mix_bwd", grid=(T // tm,),
        in_specs=[tok(D_A), tok(D_A), tok(D), tok(D), tok(D_B), tok(D),
                  pl.BlockSpec((1, 9, D), lambda i: (i // tps, 0, 0)), full((1, D_A)), full((1, D_A)),
                  full((N_GRP, BLK, BLK)), full((N_GRP, BLK, BLK)), _vmem(), _vmem(), _vmem()],
        out_specs=[tok(D_A), tok(D_A), tok(D), tok(D), tok(D_B), full((D, D)), full((D, D_A)), full((D, D_B)),
                   full((N_GRP, BLK, BLK)), full((N_GRP, BLK, BLK)), full((8, D_A)),
                   pl.BlockSpec((1, 8, D), lambda i: (i // tps, 0, 0))],
        out_shape=[jax.ShapeDtypeStruct((T, D_A), F32), jax.ShapeDtypeStruct((T, D_A), F32),
                   jax.ShapeDtypeStruct((T, D), F32), jax.ShapeDtypeStruct((T, D), F32),
                   jax.ShapeDtypeStruct((T, D_B), F32), jax.ShapeDtypeStruct((D, D), F32),
                   jax.ShapeDtypeStruct((D, D_A), F32), jax.ShapeDtypeStruct((D, D_B), F32),
                   jax.ShapeDtypeStruct((N_GRP, BLK, BLK), F32), jax.ShapeDtypeStruct((N_GRP, BLK, BLK), F32),
                   jax.ShapeDtypeStruct((8, D_A), F32), jax.ShapeDtypeStruct((B, 8, D), F32)],
        scratch_shapes=[pltpu.VMEM((tm, D_A), BF16), pltpu.VMEM((tm, D_A), F32), pltpu.VMEM((tm, D_A), F32),
                        pltpu.VMEM((tm, D_A), BF16), pltpu.VMEM((tm, D_A), F32)],
        args=(au, av, gta, gtb, ob, dh, mods, g_ln, b_ln, ws, bsb, wa, wb, wout), semantics=("arbitrary",),
        plans=plans)


def _prologue(c_pad, w_ada, first_shards):
    cols = w_ada.shape[1]
    plan_w, plan_c = _GatherRelayed(first_shards), _GatherDirect([c_pad])
    plan_m = _Gather([jax.ShapeDtypeStruct((N_DEV * 8, cols), F32)])
    n_w = len(first_shards)

    def body(c_ref, wada_ref, *rest):
        w_ins, rest = rest[:n_w], rest[n_w:]
        call_ref, mods_ref = rest[:2]
        w_outs, rest = rest[2:2 + n_w], rest[2 + n_w:]
        cvm, part, local_sem = rest[:3]
        sems = rest[3:]
        sems_w, sems_c, sems_m = sems[:3], sems[3:6], sems[6:9]
        plan_c.start([c_ref], [call_ref], sems_c)
        plan_c.relay([c_ref], [call_ref], sems_c)
        plan_c.finish([c_ref], [call_ref], sems_c)
        load = pltpu.make_async_copy(call_ref, cvm, local_sem)
        load.start()
        load.wait()
        cv = cvm[...].reshape(N_DEV * 8, D)
        part[...] = _dot((cv * jax.nn.sigmoid(cv)).astype(BF16), wada_ref[...].astype(BF16))
        plan_m.start([part], [mods_ref], sems_m)
        plan_w.start(w_ins, w_outs, sems_w)
        plan_m.relay([part], [mods_ref], sems_m)
        plan_m.finish([part], [mods_ref], sems_m)
        plan_w.relay(w_ins, w_outs, sems_w)
        plan_w.finish(w_ins, w_outs, sems_w)

    res = pl.pallas_call(
        body, name="prologue", in_specs=[_any(), _vmem()] + [_any()] * n_w,
        out_specs=[_any()] * (2 + n_w), out_shape=plan_c.out_shape + plan_m.out_shape + plan_w.out_shape,
        scratch_shapes=[pltpu.VMEM((N_DEV, 8, D), F32), pltpu.VMEM((N_DEV * 8, cols), F32), pltpu.SemaphoreType.DMA]
        + plan_w.scratch + plan_c.scratch + plan_m.scratch,
        compiler_params=pltpu.CompilerParams(vmem_limit_bytes=VMEM_LIMIT),
    )(c_pad, w_ada, *first_shards)
    return res[0], res[1], list(res[2:])


def _ada_update(c_all, dm_cols, w, m, v):
    n, cols = c_all.shape[0], w.shape[1]
    br = 256

    def body(c_ref, dm_ref, w_ref, m_ref, v_ref, g_ref, d_ref, nm_ref, nv_ref):
        cv = c_ref[...]
        g = _dot_tn((cv * jax.nn.sigmoid(cv)).astype(BF16), dm_ref[...].astype(BF16))
        g_ref[...] = g
        d_ref[...], nm_ref[...], nv_ref[...] = _adamw(w_ref[...], g, m_ref[...], v_ref[...])

    rows = pl.BlockSpec((br, cols), lambda i: (i, 0))
    outs, _ = _call(
        body, name="ada_update", grid=(D // br,),
        in_specs=[pl.BlockSpec((n, br), lambda i: (0, i)), pl.BlockSpec((n, cols), lambda i: (0, 0)), rows, rows, rows],
        out_specs=[rows] * 4, out_shape=[jax.ShapeDtypeStruct(w.shape, F32)] * 4, args=(c_all, dm_cols, w, m, v),
        semantics=("parallel",))
    return outs


def _mod_finish(mg1, mg2, mg2g, mg3, mods, g1, g2, g3):
    B = mods.shape[0]

    def body(mg1_ref, mg2_ref, mg2g_ref, mg3_ref, m_ref, g1_ref, g2_ref, g3_ref, dm_ref, dgn_ref):
        dgn_ref[...] = jnp.zeros(dgn_ref.shape, F32)
        for k, (mg, g_ref) in enumerate(((mg1_ref, g1_ref), (mg2_ref, g2_ref), (mg3_ref, g3_ref))):
            for b in range(B):
                s_dxr = mg[b, 1:2, :]
                dm_ref[b, 3 * k:3 * k + 1, :] = mg[b, 0:1, :]
                dm_ref[b, 3 * k + 1:3 * k + 2, :] = g_ref[...] * s_dxr
                dm_ref[b, 3 * k + 2:3 * k + 3, :] = mg2g_ref[b, 0:1, :] if k == 1 else mg[b, 2:3, :]
                dgn_ref[k:k + 1, :] += (1.0 + m_ref[b, 3 * k + 1:3 * k + 2, :]) * s_dxr

    args = (mg1, mg2, mg2g, mg3, mods, g1, g2, g3)
    out_shape = [jax.ShapeDtypeStruct((B, 9, D), F32), jax.ShapeDtypeStruct((8, D), F32)]
    return pl.pallas_call(body, name="mod_finish", grid=(1,), in_specs=[_whole(a) for a in args],
                          out_specs=[_whole(o) for o in out_shape], out_shape=out_shape,
                          compiler_params=_params("arbitrary"))(*args)


def _small_update(gathered, params, ms, vs):
    n = len(SMALL)
    B = gathered[0].shape[1]

    def body(*refs):
        gdm, ggn, gln, gws, gbs, ggq, ggk, gsk, gloss = refs[:9]
        w = dict(zip(SMALL, refs[9:9 + n]))
        m = dict(zip(SMALL, refs[9 + n:9 + 2 * n]))
        v = dict(zip(SMALL, refs[9 + 2 * n:9 + 3 * n]))
        outs = refs[9 + 3 * n:-1]
        out = {name: outs[4 * k:4 * k + 4] for k, name in enumerate(SMALL)}

        def total(ref, idx):
            acc = ref[(0,) + idx]
            for dev in range(1, N_DEV):
                acc = acc + ref[(dev,) + idx]
            return acc

        def finish(name, g, idx=(Ellipsis,)):
            d, nm, nv = _adamw(w[name][idx], g, m[name][idx], v[name][idx])
            for ref, val in zip(out[name], (g, d, nm, nv)):
                ref[idx] = val

        g_bada = total(gdm, (slice(0, 1),))
        for b in range(1, B):
            g_bada = g_bada + total(gdm, (slice(b, b + 1),))
        finish("b_ada", g_bada)
        finish("g_norm1", total(ggn, (slice(0, 1),)))
        finish("g_norm2", total(ggn, (slice(1, 2),)))
        finish("g_norm3", total(ggn, (slice(2, 3),)))
        finish("g_sgu_ln", total(gln, (slice(0, 1),)))
        finish("b_sgu_ln", total(gln, (slice(1, 2),)))
        tril = lax.broadcasted_iota(jnp.int32, (BLK, BLK), 0) >= lax.broadcasted_iota(jnp.int32, (BLK, BLK), 1)
        for g in range(N_GRP):
            finish("w_spatial", jnp.where(tril, total(gws, (g,)), 0.0), (0, g))
            finish("b_spatial", jnp.sum(total(gbs, (g,)).T, axis=0, keepdims=True), (0, slice(g, g + 1)))
        finish("g_q", total(ggq, ()))
        finish("g_k", total(ggk, ()))
        finish("attn_sinks", total(gsk, (slice(0, 1), slice(0, N_KV * Q_PER_KV))))
        refs[-1][...] = total(gloss, ())

    args = list(gathered) + [params[k] for k in SMALL] + [ms[k] for k in SMALL] + [vs[k] for k in SMALL]
    out_shape = []
    for k in SMALL:
        out_shape += [jax.ShapeDtypeStruct(params[k].shape, F32)] * 4
    out_shape.append(jax.ShapeDtypeStruct((8, 128), F32))
    res = pl.pallas_call(body, name="small_update", grid=(1,), in_specs=[_whole(a) for a in args],
                         out_specs=[_whole(o) for o in out_shape], out_shape=out_shape,
                         compiler_params=_params("arbitrary"))(*args)
    return {k: res[4 * i:4 * i + 4] for i, k in enumerate(SMALL)}, res[-1][0, 0]


def _rs_add(p, r, core, name):
    _, rows, width = p.shape

    def body(c_ref, p_ref, r_ref, o_ref):
        o_ref[...] = (p_ref[...] + r_ref[...]).astype(BF16)

    return pl.pallas_call(
        body, name="rs_add_" + name, out_shape=jax.ShapeDtypeStruct((4, rows, width), BF16),
        grid_spec=pltpu.PrefetchScalarGridSpec(
            num_scalar_prefetch=1, grid=(4,),
            in_specs=[pl.BlockSpec((1, rows, width), lambda k, c_ref: (2 * k + c_ref[0], 0, 0)),
                      pl.BlockSpec((1, rows, width), lambda k, c_ref: (k, 0, 0))],
            out_specs=pl.BlockSpec((1, rows, width), lambda k, c_ref: (k, 0, 0))),
        compiler_params=_params("parallel"),
    )(core, p, r)


def _adam_big(w, m, v, r, name):
    rows, cols = w.shape

    def body(r_ref, w_ref, m_ref, v_ref, go_ref, d_ref, nm_ref, nv_ref):
        g = r_ref[0].astype(F32)
        for k in range(1, 4):
            g = g + r_ref[k].astype(F32)
        go_ref[...] = g
        d_ref[...], nm_ref[...], nv_ref[...] = _adamw(w_ref[...], g, m_ref[...], v_ref[...])

    br = rows // 2
    shard = pl.BlockSpec((br, cols), lambda i: (i, 0))
    return pl.pallas_call(body, name="adam_" + name, grid=(2,),
                          in_specs=[pl.BlockSpec((4, br, cols), lambda i: (0, i, 0)), shard, shard, shard],
                          out_specs=[shard] * 4, out_shape=[jax.ShapeDtypeStruct(w.shape, F32)] * 4,
                          compiler_params=_params("parallel"))(r, w, m, v)


def kernel(x, c, w_ada, b_ada, g_norm1, ffn1_w_gate, ffn1_w_up, ffn1_w_down, g_norm2, w_in, g_sgu_ln, b_sgu_ln, w_spatial, b_spatial, g_q, g_k, attn_sinks, w_branch_a, w_branch_b, w_out, g_norm3, ffn2_w_gate, ffn2_w_up, ffn2_w_down, loss_target, m_w_ada, m_b_ada, m_g_norm1, m_ffn1_w_gate, m_ffn1_w_up, m_ffn1_w_down, m_g_norm2, m_w_in, m_g_sgu_ln, m_b_sgu_ln, m_w_spatial, m_b_spatial, m_g_q, m_g_k, m_attn_sinks, m_w_branch_a, m_w_branch_b, m_w_out, m_g_norm3, m_ffn2_w_gate, m_ffn2_w_up, m_ffn2_w_down, v_w_ada, v_b_ada, v_g_norm1, v_ffn1_w_gate, v_ffn1_w_up, v_ffn1_w_down, v_g_norm2, v_w_in, v_g_sgu_ln, v_b_sgu_ln, v_w_spatial, v_b_spatial, v_g_q, v_g_k, v_attn_sinks, v_w_branch_a, v_w_branch_b, v_w_out, v_g_norm3, v_ffn2_w_gate, v_ffn2_w_up, v_ffn2_w_down):
    names = ("w_ada", "b_ada", "g_norm1", "ffn1_w_gate", "ffn1_w_up", "ffn1_w_down", "g_norm2", "w_in", "g_sgu_ln",
             "b_sgu_ln", "w_spatial", "b_spatial", "g_q", "g_k", "attn_sinks", "w_branch_a", "w_branch_b", "w_out",
             "g_norm3", "ffn2_w_gate", "ffn2_w_up", "ffn2_w_down")
    w = dict(zip(names, (w_ada, b_ada, g_norm1, ffn1_w_gate, ffn1_w_up, ffn1_w_down, g_norm2, w_in, g_sgu_ln,
                         b_sgu_ln, w_spatial, b_spatial, g_q, g_k, attn_sinks, w_branch_a, w_branch_b, w_out, g_norm3,
                         ffn2_w_gate, ffn2_w_up, ffn2_w_down)))
    m = dict(zip(names, (m_w_ada, m_b_ada, m_g_norm1, m_ffn1_w_gate, m_ffn1_w_up, m_ffn1_w_down, m_g_norm2, m_w_in,
                         m_g_sgu_ln, m_b_sgu_ln, m_w_spatial, m_b_spatial, m_g_q, m_g_k, m_attn_sinks, m_w_branch_a,
                         m_w_branch_b, m_w_out, m_g_norm3, m_ffn2_w_gate, m_ffn2_w_up, m_ffn2_w_down)))
    v = dict(zip(names, (v_w_ada, v_b_ada, v_g_norm1, v_ffn1_w_gate, v_ffn1_w_up, v_ffn1_w_down, v_g_norm2, v_w_in,
                         v_g_sgu_ln, v_b_sgu_ln, v_w_spatial, v_b_spatial, v_g_q, v_g_k, v_attn_sinks, v_w_branch_a,
                         v_w_branch_b, v_w_out, v_g_norm3, v_ffn2_w_gate, v_ffn2_w_up, v_ffn2_w_down)))
    B, seq, _ = x.shape
    T = B * seq
    nbs = seq // BLK
    xi, yi, ci = _place()
    me = 4 * xi + 2 * yi + ci
    core = jnp.reshape(ci, (1,)).astype(jnp.int32)
    layout = {name: (tform, n, width) for name, tform, n, width in BIG}
    shard = {name: w[name][0].astype(BF16).T if tform else w[name][0].astype(BF16) for name, tform, _, _ in BIG}
    wts, parts, big_out = {}, {}, {}

    def gather_plan(group):
        return _Gather([shard[k] for k in group])

    def take(group, gathered):
        for k, g in zip(group, gathered):
            wts[k] = g.reshape(N_DEV * layout[k][1], layout[k][2])

    def blocks(name, grad):
        return grad.reshape(N_DEV, layout[name][1], layout[name][2])

    def to_sibling(names, grads):
        return _RsSibling([blocks(k, g) for k, g in zip(names, grads)])

    def add(names, grads, from_sibling):
        for k, g, r in zip(names, grads, from_sibling):
            parts[k] = _rs_add(blocks(k, g), r, core, k)

    def to_chips(names):
        return _RsChips([parts[k] for k in names])

    def update(names, from_chips):
        for k, r in zip(names, from_chips):
            if layout[k][0]:
                big_out[k] = [o.T[None] for o in _adam_big(w[k][0].T, m[k][0].T, v[k][0].T, r, k)]
            else:
                big_out[k] = [o[None] for o in _adam_big(w[k][0], m[k][0], v[k][0], r, k)]

    g1, g2, g3, g_ln, b_ln = g_norm1, g_norm2, g_norm3, g_sgu_ln, b_sgu_ln
    gq2, gk2 = jnp.tile(g_q, (1, 2)), jnp.tile(g_k, (1, 2))
    ws = w_spatial[0]
    bsb = jnp.broadcast_to(b_spatial[0][:, :, None], (N_GRP, BLK, BLK))
    xf = x.reshape(T, D)
    tgt = loss_target.reshape(T, D)

    ffn1 = ("ffn1_w_gate", "ffn1_w_up", "ffn1_w_down")
    ffn2 = ("ffn2_w_gate", "ffn2_w_up", "ffn2_w_down")
    c_pad = jnp.concatenate([c, jnp.zeros((8 - B, D), F32)], axis=0)
    c_all, mods_cols, gathered = _prologue(c_pad, w_ada[0], [shard[k] for k in ffn1[:2]])
    take(ffn1[:2], gathered)
    c_all = c_all[:, :B].reshape(N_DEV * B, D)
    mine = lax.dynamic_slice_in_dim(mods_cols, 8 * me, B, axis=1)
    mods = (mine.transpose(1, 0, 2).reshape(B, 9 * D) + b_ada).reshape(B, 9, D)

    group = ("ffn1_w_down", "w_in")
    (gb1, ub1), (gathered,) = _ffn_gate_up(xf, mods, g1, wts["ffn1_w_gate"], wts["ffn1_w_up"], 0, seq,
                                           plans=[gather_plan(group)])
    take(group, gathered)
    (h1, y1), _ = _ffn_down(xf, gb1, ub1, mods, wts["ffn1_w_down"], 0, seq)
    group = ("ffn2_w_gate",)
    (au, av, q_tok, k_tok, v_tok, gta, gtb), (gathered,) = _inproj_fwd(h1, mods, g2, wts["w_in"], seq,
                                                                       plans=[gather_plan(group)])
    take(group, gathered)
    group = ("w_branch_a", "w_branch_b", "w_out", "ffn2_w_up")
    ob, (gathered,) = _swa_fwd(q_tok, k_tok, v_tok, gq2, gk2, attn_sinks, nbs, plans=[gather_plan(group)])
    take(group, gathered)
    mixw = (wts["w_branch_a"], wts["w_branch_b"], wts["w_out"])
    group = ("ffn2_w_down",)
    h2, (gathered,) = _mix_fwd(au, av, gta, gtb, ob, h1, mods, g_ln, b_ln, ws, bsb, *mixw, seq,
                               plans=[gather_plan(group)])
    take(group, gathered)
    dh3, y3, gb3, ub3, loss_parts = _ffn_fwd_loss(h2, mods, g3, *[wts[k] for k in ffn2], 6, seq, tgt)
    loss_part = jnp.full((8, 128), jnp.sum(loss_parts[:, 0, 0]))

    (dh2, xb, dyb, a, dg, du, mg3), _ = _ffn_bwd(h2, dh3, y3, gb3, ub3, mods, g3, *[wts[k] for k in ffn2], 6, seq)
    d_gate, _ = _wgrad(dg, xb, "wgrad_ffn2_gate")
    d_up, (r,) = _wgrad(du, xb, "wgrad_ffn2_up", plans=[to_sibling(ffn2[:1], [d_gate])])
    add(ffn2[:1], [d_gate], r)
    d_down, (r,) = _wgrad(a, dyb, "wgrad_ffn2_down", plans=[to_sibling(ffn2[1:2], [d_up])])
    add(ffn2[1:2], [d_up], r)

    (dau, dav, dgta, dgtb, dob, d_out, d_a, d_b, dws, dbs, dln, mg2g), (r, from_chips) = _mix_bwd(
        au, av, gta, gtb, ob, dh2, mods, g_ln, b_ln, ws, bsb, *mixw, seq,
        plans=[to_sibling(ffn2[2:], [d_down]), to_chips(ffn2[:2])])
    add(ffn2[2:], [d_down], r)
    update(ffn2[:2], from_chips)
    mixers = ("w_out", "w_branch_a", "w_branch_b")
    (dq, dk, dv, dk_halo, dv_halo, dgq2, dgk2, dsk), (from_chips, r) = _swa_bwd(
        q_tok, k_tok, v_tok, gq2, gk2, attn_sinks, dob, nbs,
        plans=[to_chips(ffn2[2:]), to_sibling(mixers, [d_out, d_a, d_b])])
    update(ffn2[2:], from_chips)
    add(mixers, [d_out, d_a, d_b], r)
    dk, dv = _swa_add_halo(dk, dk_halo), _swa_add_halo(dv, dv_halo)
    dgq, dgk = dgq2[:, :HD] + dgq2[:, HD:], dgk2[:, :HD] + dgk2[:, HD:]
    (dh1, xb2, dpb, mg2), (from_chips, early) = _inproj_bwd(
        h1, dh2, (dau, dav, dq, dk, dv, dgta, dgtb), mods, g2, wts["w_in"], seq,
        plans=[to_chips(mixers), _Gather([dln, dws, dbs, dgq, dgk, dsk, loss_part])])
    update(mixers, from_chips)

    d_in, _ = _wgrad(dpb, xb2, "wgrad_w_in")

    (dx, xb, dyb, a, dg, du, mg1), _ = _ffn_bwd(xf, dh1, y1, gb1, ub1, mods, g1, *[wts[k] for k in ffn1], 0, seq)
    dmods, dgn = _mod_finish(mg1, mg2, mg2g, mg3, mods, g1, g2, g3)
    d_gate, (r, late) = _wgrad(dg, xb, "wgrad_ffn1_gate",
                               plans=[to_sibling(("w_in",), [d_in]), _GatherDirect([dmods.reshape(B, 9 * D), dgn])])
    add(("w_in",), [d_in], r)
    gathered = late + early
    d_up, (r, from_chips) = _wgrad(du, xb, "wgrad_ffn1_up",
                                   plans=[to_sibling(ffn1[:1], [d_gate]), to_chips(("w_in",))])
    add(ffn1[:1], [d_gate], r)
    update(("w_in",), from_chips)
    d_down, (r, from_chips) = _wgrad(a, dyb, "wgrad_ffn1_down",
                                     plans=[to_sibling(ffn1[1:2], [d_up]), to_chips(ffn1[:1])])
    add(ffn1[1:2], [d_up], r)
    update(ffn1[:1], from_chips)
    small_out, loss = _small_update(gathered, w, m, v)
    dm_cols = lax.dynamic_slice_in_dim(gathered[0].reshape(N_DEV * B, 9 * D), (9 * D // N_DEV) * me,
                                       9 * D // N_DEV, axis=1)
    ada_out = _ada_update(c_all, dm_cols, w_ada[0], m_w_ada[0], v_w_ada[0])
    r, from_chips = _exchange([to_sibling(ffn1[2:], [d_down]), to_chips(ffn1[1:2])], "rs_tail")
    add(ffn1[2:], [d_down], r)
    update(ffn1[1:2], from_chips)
    (from_chips,) = _exchange([to_chips(ffn1[2:])], "rs_last")
    update(ffn1[2:], from_chips)

    def leaf(kind, name):
        if name == "w_ada":
            return ada_out[kind][None]
        if name in SMALL:
            return small_out[name][kind]
        return big_out[name][kind]

    return (loss, dx.reshape(B, seq, D), *[leaf(kind, name) for kind in range(4) for name in names])
```

```python
import math

import jax
import jax.numpy as jnp
from jax import lax
from jax.experimental import pallas as pl
from jax.experimental.pallas import tpu as pltpu

F32 = jnp.float32
BF16 = jnp.bfloat16
MESH = pl.DeviceIdType.MESH
N_DEV = 8

VMEM_LIMIT = 56 * 1024 * 1024

D = 1024
FF = 2816
FC = 2816
D_A = 512
D_B = 512
HD = 64
N_KV = 2
Q_PER_KV = 4
BLK = 128
N_GRP = 4
IN_COLS = 3840
PIECES = (("au", 0, 512), ("av", 512, 512), ("q", 1024, 512), ("k", 1536, 128), ("v", 1664, 128),
          ("ga", 1792, 1024), ("gb", 2816, 1024))
EPS = 1e-6
NEG = -1e30
GELU_C = math.sqrt(2.0 / math.pi)

ADAM_LR = 0.001
ADAM_B1 = 0.9
ADAM_B2 = 0.999
ADAM_EPS = 1e-08
ADAM_WD = 0.01
ADAM_STEP = 10

NT = (((1,), (1,)), ((), ()))
TN = (((0,), (0,)), ((), ()))

BIG = (("ffn1_w_gate", True, FF // N_DEV, D), ("ffn1_w_up", True, FF // N_DEV, D),
       ("ffn1_w_down", False, FF // N_DEV, D), ("w_in", True, IN_COLS // N_DEV, D),
       ("w_branch_a", True, D // N_DEV, D_A), ("w_branch_b", True, D // N_DEV, D_B), ("w_out", False, D // N_DEV, D),
       ("ffn2_w_gate", True, FF // N_DEV, D), ("ffn2_w_up", True, FF // N_DEV, D),
       ("ffn2_w_down", False, FF // N_DEV, D))
SMALL = ("b_ada", "g_norm1", "g_norm2", "g_sgu_ln", "b_sgu_ln", "w_spatial", "b_spatial", "g_q", "g_k",
         "attn_sinks", "g_norm3")


def _dot(a, b):
    return jnp.dot(a, b, preferred_element_type=F32)


def _dot_nt(a, b):
    return lax.dot_general(a, b, NT, preferred_element_type=F32)


def _dot_tn(a, b):
    return lax.dot_general(a, b, TN, preferred_element_type=F32)


def _vmem():
    return pl.BlockSpec(memory_space=pltpu.VMEM)


def _any():
    return pl.BlockSpec(memory_space=pl.ANY)


def _whole(a):
    return pl.BlockSpec(a.shape, lambda i: (0,) * len(a.shape))


def _rms_mod(h, g, sh, sc):
    inv = lax.rsqrt(jnp.mean(h * h, axis=-1, keepdims=True) + EPS)
    r = h * inv
    return (r * g) * (1.0 + sc) + sh, r, inv


def _rms_mod_bwd(dxn, r, inv, g, sc):
    dr = dxn * (g * (1.0 + sc))
    dh = inv * (dr - r * jnp.mean(dr * r, axis=-1, keepdims=True))
    return dh, jnp.sum(dxn, axis=0, keepdims=True), jnp.sum(dxn * r, axis=0, keepdims=True)


def _gelu(x):
    t = jnp.tanh(GELU_C * (x + 0.044715 * (x * x * x)))
    return 0.5 * x * (1.0 + t), t


def _gelu_grad(x, t):
    return 0.5 * (1.0 + t) + 0.5 * x * (1.0 - t * t) * (GELU_C * (1.0 + 3.0 * 0.044715 * x * x))


def _adamw(w, g, m, v):
    m = ADAM_B1 * m + (1.0 - ADAM_B1) * g
    v = ADAM_B2 * v + (1.0 - ADAM_B2) * (g * g)
    m_hat = m / (1.0 - ADAM_B1 ** ADAM_STEP)
    v_hat = v / (1.0 - ADAM_B2 ** ADAM_STEP)
    delta = -ADAM_LR * (m_hat / (jnp.sqrt(v_hat) + ADAM_EPS) + ADAM_WD * w)
    return delta, m, v


def _token_tile(seq, cap=512):
    return min(cap, seq)


def _params(*semantics):
    return pltpu.CompilerParams(dimension_semantics=semantics, vmem_limit_bytes=VMEM_LIMIT)


def _place():
    return lax.axis_index("x"), lax.axis_index("y"), lax.axis_index("c")


class _Gather:
    def __init__(self, arrays):
        n = len(arrays)
        self.ins = list(arrays)
        self.out_shape = [jax.ShapeDtypeStruct((N_DEV,) + a.shape, a.dtype) for a in arrays]
        self.scratch = [pltpu.SemaphoreType.DMA((n, 7)), pltpu.SemaphoreType.DMA((n, 7)),
                        pltpu.SemaphoreType.DMA((n,))]

    def _copies(self, ins, outs, sems):
        send_sems, recv_sems, local_sems = sems
        n = len(ins)
        x, y, c = _place()
        me, sibling = (x, y, c), (x, y, 1 - c)
        chips = [(1 - x, y), (x, 1 - y), (1 - x, 1 - y)]

        def slot(a, px, py, pc):
            return outs[a].at[4 * px + 2 * py + pc]

        def copy(a, k, block, to, src=None):
            return pltpu.make_async_remote_copy(
                src_ref=slot(a, *block) if src is None else src, dst_ref=slot(a, *block),
                send_sem=send_sems.at[a, k], recv_sem=recv_sems.at[a, k], device_id=to, device_id_type=MESH)

        mine = [pltpu.make_async_copy(ins[a], slot(a, *me), local_sems.at[a]) for a in range(n)]
        first = [copy(a, 0, me, sibling, src=ins[a]) for a in range(n)]
        first += [copy(a, 1 + j, me, (*chip, c), src=ins[a]) for a in range(n) for j, chip in enumerate(chips)]
        landed = [[copy(a, 1 + j, (*chip, c), me) for a in range(n)] for j, chip in enumerate(chips)]
        passed = [[copy(a, 4 + j, (*chip, c), sibling) for a in range(n)] for j, chip in enumerate(chips)]
        from_sibling = [copy(a, 0, sibling, me) for a in range(n)]
        from_sibling += [copy(a, 4 + j, (*chip, 1 - c), me) for a in range(n) for j, chip in enumerate(chips)]
        return mine, first, landed, passed, from_sibling

    def start(self, ins, outs, sems):
        mine, first, _, _, _ = self._copies(ins, outs, sems)
        for cp in mine + first:
            cp.start()

    def relay(self, ins, outs, sems):
        _, _, landed, passed, _ = self._copies(ins, outs, sems)
        for arrivals, forwards in zip(landed, passed):
            for arrival, forward in zip(arrivals, forwards):
                arrival.wait_recv()
                forward.start()

    def finish(self, ins, outs, sems):
        mine, first, _, passed, from_sibling = self._copies(ins, outs, sems)
        for cp in from_sibling:
            cp.wait_recv()
        for cp in first + [f for fs in passed for f in fs]:
            cp.wait_send()
        for cp in mine:
            cp.wait()


class _GatherRelayed:
    def __init__(self, arrays):
        n = len(arrays)
        self.ins = list(arrays)
        self.out_shape = [jax.ShapeDtypeStruct((N_DEV,) + a.shape, a.dtype) for a in arrays]
        self.scratch = [pltpu.SemaphoreType.DMA((n, 7)), pltpu.SemaphoreType.DMA((n, 7)),
                        pltpu.SemaphoreType.DMA((n,))]

    def _copies(self, ins, outs, sems):
        send_sems, recv_sems, local_sems = sems
        n = len(ins)
        x, y, c = _place()
        me, sibling = (x, y, c), (x, y, 1 - c)
        south = c == 0
        via = (jnp.where(south, 1 - x, x), jnp.where(south, y, 1 - y))
        onward = (jnp.where(south, x, 1 - x), jnp.where(south, 1 - y, y))
        via_sem = jnp.where(south, 1, 2)

        def slot(a, px, py, pc):
            return outs[a].at[4 * px + 2 * py + pc]

        def copy(a, k, block, to, src=None):
            return pltpu.make_async_remote_copy(
                src_ref=slot(a, *block) if src is None else src, dst_ref=slot(a, *block),
                send_sem=send_sems.at[a, k], recv_sem=recv_sems.at[a, k], device_id=to, device_id_type=MESH)

        chips = [(1 - x, y), (x, 1 - y), (1 - x, 1 - y)]
        mine = [pltpu.make_async_copy(ins[a], slot(a, *me), local_sems.at[a]) for a in range(n)]
        first = [copy(a, 0, me, sibling, src=ins[a]) for a in range(n)]
        first += [copy(a, 1 + j, me, (*chips[j], c), src=ins[a]) for a in range(n) for j in range(2)]
        to_pass = [copy(a, via_sem, (*via, c), me) for a in range(n)]
        passed_on = [copy(a, 3, (*via, c), (*onward, c)) for a in range(n)]
        landed = [[copy(a, 1 + j, (*chips[j], c), me) for a in range(n)] for j in range(3)]
        to_sibling = [[copy(a, 4 + j, (*chips[j], c), sibling) for a in range(n)] for j in range(3)]
        from_sibling = [copy(a, 0, sibling, me) for a in range(n)]
        from_sibling += [copy(a, 4 + j, (*chips[j], 1 - c), me) for a in range(n) for j in range(3)]
        return mine, first, to_pass, passed_on, landed, to_sibling, from_sibling

    def start(self, ins, outs, sems):
        mine, first, _, _, _, _, _ = self._copies(ins, outs, sems)
        for cp in mine + first:
            cp.start()

    def relay(self, ins, outs, sems):
        _, _, to_pass, passed_on, landed, to_sibling, _ = self._copies(ins, outs, sems)
        for arrival, onward in zip(to_pass, passed_on):
            arrival.wait_recv()
            onward.start()
        x, y, c = _place()
        for j in range(3):
            for a, (arrival, forward) in enumerate(zip(landed[j], to_sibling[j])):
                if j < 2:
                    pl.when((c == 0) != (j == 0))(arrival.wait_recv)
                else:
                    arrival.wait_recv()
                forward.start()

    def finish(self, ins, outs, sems):
        mine, first, _, passed_on, _, to_sibling, from_sibling = self._copies(ins, outs, sems)
        for cp in from_sibling:
            cp.wait_recv()
        for cp in first + passed_on + [f for fs in to_sibling for f in fs]:
            cp.wait_send()
        for cp in mine:
            cp.wait()


class _GatherDirect:
    def __init__(self, arrays):
        n = len(arrays)
        self.ins = list(arrays)
        self.out_shape = [jax.ShapeDtypeStruct((N_DEV,) + a.shape, a.dtype) for a in arrays]
        self.scratch = [pltpu.SemaphoreType.DMA((n, 7)), pltpu.SemaphoreType.DMA((n, 7)),
                        pltpu.SemaphoreType.DMA((n,))]

    def _copies(self, ins, outs, sems):
        send_sems, recv_sems, local_sems = sems
        n = len(ins)
        x, y, c = _place()
        peers = [(x ^ (k >> 2 & 1), y ^ (k >> 1 & 1), c ^ (k & 1)) for k in range(1, N_DEV)]

        def slot(a, px, py, pc):
            return outs[a].at[4 * px + 2 * py + pc]

        def copy(a, k, block, to, src=None):
            return pltpu.make_async_remote_copy(
                src_ref=slot(a, *block) if src is None else src, dst_ref=slot(a, *block),
                send_sem=send_sems.at[a, k], recv_sem=recv_sems.at[a, k], device_id=to, device_id_type=MESH)

        mine = [pltpu.make_async_copy(ins[a], slot(a, x, y, c), local_sems.at[a]) for a in range(n)]
        sends = [copy(a, k, (x, y, c), peer, src=ins[a]) for a in range(n) for k, peer in enumerate(peers)]
        arrivals = [copy(a, k, peer, (x, y, c)) for a in range(n) for k, peer in enumerate(peers)]
        return mine, sends, arrivals

    def start(self, ins, outs, sems):
        mine, sends, _ = self._copies(ins, outs, sems)
        for cp in mine + sends:
            cp.start()

    def relay(self, ins, outs, sems):
        pass

    def finish(self, ins, outs, sems):
        mine, sends, arrivals = self._copies(ins, outs, sems)
        for cp in arrivals:
            cp.wait_recv()
        for cp in sends:
            cp.wait_send()
        for cp in mine:
            cp.wait()


class _RsSibling:
    def __init__(self, ps):
        n = len(ps)
        self.ins = list(ps)
        self.out_shape = [jax.ShapeDtypeStruct((4,) + p.shape[1:], p.dtype) for p in ps]
        self.scratch = [pltpu.SemaphoreType.DMA((n, 4)), pltpu.SemaphoreType.DMA((n, 4))]

    def _copies(self, ins, outs, sems):
        send_sems, recv_sems = sems
        x, y, c = _place()
        return [pltpu.make_async_remote_copy(
            src_ref=ins[a].at[2 * q + (1 - c)], dst_ref=outs[a].at[q], send_sem=send_sems.at[a, q],
            recv_sem=recv_sems.at[a, q], device_id=(x, y, 1 - c), device_id_type=MESH)
            for a in range(len(ins)) for q in range(4)]

    def start(self, ins, outs, sems):
        for cp in self._copies(ins, outs, sems):
            cp.start()

    def relay(self, ins, outs, sems):
        pass

    def finish(self, ins, outs, sems):
        for cp in self._copies(ins, outs, sems):
            cp.wait()


class _RsChips:
    def __init__(self, qs):
        n = len(qs)
        self.ins = list(qs)
        self.out_shape = [jax.ShapeDtypeStruct(q.shape, q.dtype) for q in qs]
        self.scratch = [pltpu.SemaphoreType.DMA((n, 3)), pltpu.SemaphoreType.DMA((n, 3)),
                        pltpu.SemaphoreType.DMA((n,))]

    def _copies(self, ins, outs, sems):
        send_sems, recv_sems, local_sems = sems
        n = len(ins)
        x, y, c = _place()
        my_chip = 2 * x + y
        chips = [(1 - x, y), (x, 1 - y), (1 - x, 1 - y)]

        def copy(a, j, src_slot, dst_slot):
            px, py = chips[j]
            return pltpu.make_async_remote_copy(
                src_ref=ins[a].at[src_slot], dst_ref=outs[a].at[dst_slot], send_sem=send_sems.at[a, j],
                recv_sem=recv_sems.at[a, j], device_id=(px, py, c), device_id_type=MESH)

        own = [pltpu.make_async_copy(ins[a].at[my_chip], outs[a].at[my_chip], local_sems.at[a]) for a in range(n)]
        sends = [copy(a, j, 2 * px + py, my_chip) for a in range(n) for j, (px, py) in enumerate(chips)]
        arrivals = [copy(a, j, my_chip, 2 * px + py) for a in range(n) for j, (px, py) in enumerate(chips)]
        return own, sends, arrivals

    def start(self, ins, outs, sems):
        own, sends, _ = self._copies(ins, outs, sems)
        for cp in own + sends:
            cp.start()

    def relay(self, ins, outs, sems):
        pass

    def finish(self, ins, outs, sems):
        own, sends, arrivals = self._copies(ins, outs, sems)
        for cp in arrivals:
            cp.wait_recv()
        for cp in sends:
            cp.wait_send()
        for cp in own:
            cp.wait()


def _split_plans(plans, refs_in, refs_out, refs_scr, phase):
    i = o = s = 0
    for p in plans:
        ni, no, ns = len(p.ins), len(p.out_shape), len(p.scratch)
        getattr(p, phase)(refs_in[i:i + ni], refs_out[o:o + no], refs_scr[s:s + ns])
        i, o, s = i + ni, o + no, s + ns


def _plan_results(plans, res):
    out = []
    for p in plans:
        out.append(list(res[:len(p.out_shape)]))
        res = res[len(p.out_shape):]
    return out


def _exchange(plans, name):
    c_in = [a for p in plans for a in p.ins]
    c_out = [s for p in plans for s in p.out_shape]
    c_scr = [s for p in plans for s in p.scratch]

    def body(*refs):
        cin, cout, cscr = refs[:len(c_in)], refs[len(c_in):len(c_in) + len(c_out)], refs[len(c_in) + len(c_out):]
        for phase in ("start", "relay", "finish"):
            _split_plans(plans, cin, cout, cscr, phase)

    res = pl.pallas_call(body, name=name, in_specs=[_any()] * len(c_in), out_specs=[_any()] * len(c_out),
                         out_shape=c_out, scratch_shapes=c_scr)(*c_in)
    return _plan_results(plans, res)


def _call(body, *, name, grid, in_specs, out_specs, out_shape, args, semantics, scratch_shapes=(), plans=()):
    n_in, n_out, n_scr = len(in_specs), len(out_specs), len(scratch_shapes)
    c_in = [a for p in plans for a in p.ins]
    c_out = [s for p in plans for s in p.out_shape]
    c_scr = [s for p in plans for s in p.scratch]
    n_steps = math.prod(grid)

    def wrapped(*refs):
        ins, refs = refs[:n_in], refs[n_in:]
        cin, refs = refs[:len(c_in)], refs[len(c_in):]
        outs, refs = refs[:n_out], refs[n_out:]
        cout, refs = refs[:len(c_out)], refs[len(c_out):]
        scr, cscr = refs[:n_scr], refs[n_scr:]
        if plans:
            step = 0
            for d, g in enumerate(grid):
                step = step * g + pl.program_id(d)
            pl.when(step == 0)(lambda: _split_plans(plans, cin, cout, cscr, "start"))
        body(*ins, *outs, *scr)
        if plans:
            pl.when(step == max(n_steps - 2, 0))(lambda: _split_plans(plans, cin, cout, cscr, "relay"))
            pl.when(step == n_steps - 1)(lambda: _split_plans(plans, cin, cout, cscr, "finish"))

    res = pl.pallas_call(
        wrapped, name=name, grid=grid, in_specs=list(in_specs) + [_any()] * len(c_in),
        out_specs=list(out_specs) + [_any()] * len(c_out), out_shape=list(out_shape) + c_out,
        scratch_shapes=list(scratch_shapes) + c_scr,
        compiler_params=_params(*(("arbitrary",) * len(grid) if plans else semantics)),
    )(*args, *c_in)
    return list(res[:n_out]), _plan_results(plans, res[n_out:])


def _ffn_fwd_loss(h, mods, gn, wg, wu, wd, row0, seq, tgt):
    T = h.shape[0]
    tm = _token_tile(seq, 256)
    tps = seq // tm
    n_t = T // tm

    def body(h_ref, m_ref, g_ref, wg_ref, wu_ref, wd_ref, tgt_ref, out_ref, y_ref, gb_ref, ub_ref, loss_ref):
        hv = h_ref[...]
        sh = m_ref[0, row0:row0 + 1, :]
        sc = m_ref[0, row0 + 1:row0 + 2, :]
        ga = m_ref[0, row0 + 2:row0 + 3, :]
        xn, _, _ = _rms_mod(hv, g_ref[...], sh, sc)
        xb = xn.astype(BF16)
        acc = jnp.zeros((tm, D), F32)
        for c0 in range(0, FF, FC):
            gg = _dot_nt(xb, wg_ref[c0:c0 + FC, :])
            uu = _dot_nt(xb, wu_ref[c0:c0 + FC, :])
            gg, uu = gg.astype(BF16), uu.astype(BF16)
            gb_ref[:, c0:c0 + FC] = gg
            ub_ref[:, c0:c0 + FC] = uu
            acc = acc + _dot((gg * jax.nn.sigmoid(gg)) * uu, wd_ref[c0:c0 + FC, :])
        y_ref[...] = acc
        d = hv + (0.5 * ga) * acc - tgt_ref[...]
        out_ref[...] = d * (1.0 / D)
        loss_ref[...] = jnp.full((1, 8, 128), 0.5 / D, F32) * jnp.sum(d * d)

    tok = pl.BlockSpec((tm, D), lambda i: (i, 0))
    tokf = pl.BlockSpec((tm, FF), lambda i: (i, 0))
    outs, _ = _call(
        body, name="ffn_fwd_loss", grid=(n_t,),
        in_specs=[tok, pl.BlockSpec((1, 9, D), lambda i: (i // tps, 0, 0)), pl.BlockSpec((1, D), lambda i: (0, 0)),
                  _vmem(), _vmem(), _vmem(), tok],
        out_specs=[tok, tok, tokf, tokf, pl.BlockSpec((1, 8, 128), lambda i: (i, 0, 0))],
        out_shape=[jax.ShapeDtypeStruct((T, D), F32), jax.ShapeDtypeStruct((T, D), F32),
                   jax.ShapeDtypeStruct((T, FF), BF16), jax.ShapeDtypeStruct((T, FF), BF16),
                   jax.ShapeDtypeStruct((n_t, 8, 128), F32)],
        args=(h, mods, gn, wg, wu, wd, tgt), semantics=("parallel",))
    return outs


def _ffn_gate_up(h, mods, gn, wg, wu, row0, seq, plans=()):
    T = h.shape[0]
    tm = _token_tile(seq, 512)
    tps = seq // tm

    def body(h_ref, m_ref, g_ref, wg_ref, wu_ref, gb_ref, ub_ref):
        xn, _, _ = _rms_mod(h_ref[...], g_ref[...], m_ref[0, row0:row0 + 1, :], m_ref[0, row0 + 1:row0 + 2, :])
        xb = xn.astype(BF16)
        for c0 in range(0, FF, FC):
            gb_ref[:, c0:c0 + FC] = _dot_nt(xb, wg_ref[c0:c0 + FC, :]).astype(BF16)
            ub_ref[:, c0:c0 + FC] = _dot_nt(xb, wu_ref[c0:c0 + FC, :]).astype(BF16)

    tokf = pl.BlockSpec((tm, FF), lambda i: (i, 0))
    return _call(
        body, name="ffn_gate_up", grid=(T // tm,),
        in_specs=[pl.BlockSpec((tm, D), lambda i: (i, 0)), pl.BlockSpec((1, 9, D), lambda i: (i // tps, 0, 0)),
                  pl.BlockSpec((1, D), lambda i: (0, 0)), _vmem(), _vmem()],
        out_specs=[tokf, tokf], out_shape=[jax.ShapeDtypeStruct((T, FF), BF16)] * 2, args=(h, mods, gn, wg, wu),
        semantics=("parallel",), plans=plans)


def _ffn_down(h, gb, ub, mods, wd, row0, seq, plans=()):
    T = h.shape[0]
    tm = _token_tile(seq, 512)
    tps = seq // tm

    def body(h_ref, gb_ref, ub_ref, m_ref, wd_ref, out_ref, y_ref):
        acc = jnp.zeros((tm, D), F32)
        for c0 in range(0, FF, FC):
            gg = gb_ref[:, c0:c0 + FC]
            acc = acc + _dot((gg * jax.nn.sigmoid(gg)) * ub_ref[:, c0:c0 + FC], wd_ref[c0:c0 + FC, :])
        y_ref[...] = acc
        out_ref[...] = h_ref[...] + (0.5 * m_ref[0, row0 + 2:row0 + 3, :]) * acc

    tok = pl.BlockSpec((tm, D), lambda i: (i, 0))
    tokf = pl.BlockSpec((tm, FF), lambda i: (i, 0))
    return _call(
        body, name="ffn_down", grid=(T // tm,),
        in_specs=[tok, tokf, tokf, pl.BlockSpec((1, 9, D), lambda i: (i // tps, 0, 0)), _vmem()],
        out_specs=[tok, tok], out_shape=[jax.ShapeDtypeStruct((T, D), F32)] * 2, args=(h, gb, ub, mods, wd),
        semantics=("parallel",), plans=plans)


def _ffn_bwd(h, dhn, y, gb, ub, mods, gn, wg, wu, wd, row0, seq, plans=()):
    T = h.shape[0]
    B = T // seq
    tm = _token_tile(seq, 256)
    tps = seq // tm
    n_t = T // tm

    def body(h_ref, dhn_ref, y_ref, gb_ref, ub_ref, m_ref, g_ref, wg_ref, wu_ref, wd_ref,
             dh_ref, xb_ref, dyb_ref, a_ref, dg_ref, du_ref, mg_ref):
        i = pl.program_id(0)
        hv = h_ref[...]
        dhn = dhn_ref[...]
        sh = m_ref[0, row0:row0 + 1, :]
        sc = m_ref[0, row0 + 1:row0 + 2, :]
        ga = m_ref[0, row0 + 2:row0 + 3, :]
        g = g_ref[...]
        xn, r, inv = _rms_mod(hv, g, sh, sc)
        xb_ref[...] = xn.astype(BF16)
        dyb = ((0.5 * ga) * dhn).astype(BF16)
        dyb_ref[...] = dyb
        dga = 0.5 * jnp.sum(dhn * y_ref[...], axis=0, keepdims=True)
        dxn = jnp.zeros((tm, D), F32)
        for c0 in range(0, FF, FC):
            wgc = wg_ref[c0:c0 + FC, :]
            wuc = wu_ref[c0:c0 + FC, :]
            gg = gb_ref[:, c0:c0 + FC]
            uu = ub_ref[:, c0:c0 + FC]
            sig = jax.nn.sigmoid(gg)
            s = gg * sig
            a_ref[:, c0:c0 + FC] = s * uu
            da = _dot_nt(dyb, wd_ref[c0:c0 + FC, :]).astype(BF16)
            dub = da * s
            dgb = (da * uu) * (sig * (1.0 + gg * (1.0 - sig)))
            dg_ref[:, c0:c0 + FC] = dgb
            du_ref[:, c0:c0 + FC] = dub
            dxn = dxn + _dot(dgb, wgc) + _dot(dub, wuc)
        dh, s_dxn, s_dxr = _rms_mod_bwd(dxn, r, inv, g, sc)
        dh_ref[...] = dhn + dh

        @pl.when(i % tps == 0)
        def _():
            mg_ref[...] = jnp.zeros(mg_ref.shape, F32)

        mg_ref[0, 0:1, :] += s_dxn
        mg_ref[0, 1:2, :] += s_dxr
        mg_ref[0, 2:3, :] += dga

    tok = pl.BlockSpec((tm, D), lambda i: (i, 0))
    tokf = pl.BlockSpec((tm, FF), lambda i: (i, 0))
    return _call(
        body, name="ffn_bwd", grid=(n_t,),
        in_specs=[tok, tok, tok, tokf, tokf, pl.BlockSpec((1, 9, D), lambda i: (i // tps, 0, 0)),
                  pl.BlockSpec((1, D), lambda i: (0, 0)), _vmem(), _vmem(), _vmem()],
        out_specs=[tok, tok, tok, tokf, tokf, tokf, pl.BlockSpec((1, 8, D), lambda i: (i // tps, 0, 0))],
        out_shape=[jax.ShapeDtypeStruct((T, D), F32), jax.ShapeDtypeStruct((T, D), BF16),
                   jax.ShapeDtypeStruct((T, D), BF16), jax.ShapeDtypeStruct((T, FF), BF16),
                   jax.ShapeDtypeStruct((T, FF), BF16), jax.ShapeDtypeStruct((T, FF), BF16),
                   jax.ShapeDtypeStruct((B, 8, D), F32)],
        args=(h, dhn, y, gb, ub, mods, gn, wg, wu, wd), semantics=("arbitrary",), plans=plans)


def _wgrad(a, b, name, plans=()):
    T, da = a.shape
    db = b.shape[1]
    bm = {2816: 1408, 3840: 1280}[da]
    bn = db
    tk = min(4096, T)
    nk = T // tk

    def body(a_ref, b_ref, o_ref):
        @pl.when(pl.program_id(2) == 0)
        def _():
            o_ref[...] = jnp.zeros(o_ref.shape, F32)

        o_ref[...] += _dot_tn(a_ref[...], b_ref[...])

    (out,), plan_outs = _call(
        body, name=name, grid=(da // bm, db // bn, nk),
        in_specs=[pl.BlockSpec((tk, bm), lambda i, j, k: (k, i)), pl.BlockSpec((tk, bn), lambda i, j, k: (k, j))],
        out_specs=[pl.BlockSpec((bm, bn), lambda i, j, k: (i, j))], out_shape=[jax.ShapeDtypeStruct((da, db), F32)],
        args=(a, b), semantics=("parallel", "parallel", "arbitrary"), plans=plans)
    return out, plan_outs


def _inproj_fwd(h, mods, gn, w_in, seq, plans=()):
    T = h.shape[0]
    tm = _token_tile(seq)
    tps = seq // tm

    def body(h_ref, m_ref, g_ref, w_ref, *outs):
        xn, _, _ = _rms_mod(h_ref[...], g_ref[...], m_ref[0, 3:4, :], m_ref[0, 4:5, :])
        xb = xn.astype(BF16)
        for (_, c0, w), o_ref in zip(PIECES, outs):
            o_ref[...] = _dot_nt(xb, w_ref[c0:c0 + w, :])

    return _call(
        body, name="inproj_fwd", grid=(T // tm,),
        in_specs=[pl.BlockSpec((tm, D), lambda i: (i, 0)), pl.BlockSpec((1, 9, D), lambda i: (i // tps, 0, 0)),
                  pl.BlockSpec((1, D), lambda i: (0, 0)), _vmem()],
        out_specs=[pl.BlockSpec((tm, w), lambda i: (i, 0)) for _, _, w in PIECES],
        out_shape=[jax.ShapeDtypeStruct((T, w), F32) for _, _, w in PIECES], args=(h, mods, gn, w_in),
        semantics=("parallel",), plans=plans)


def _inproj_bwd(h, dh_res, dpieces, mods, gn, w_in, seq, plans=()):
    T = h.shape[0]
    B = T // seq
    tm = _token_tile(seq, 256)
    tps = seq // tm

    def body(h_ref, dres_ref, *rest):
        dp_refs = rest[:len(PIECES)]
        m_ref, g_ref, w_ref, dh_ref, xb_ref, dpb_ref, mg_ref = rest[len(PIECES):]
        i = pl.program_id(0)
        g = g_ref[...]
        sc = m_ref[0, 4:5, :]
        xn, r, inv = _rms_mod(h_ref[...], g, m_ref[0, 3:4, :], sc)
        xb_ref[...] = xn.astype(BF16)
        dxn = jnp.zeros((tm, D), F32)
        for (_, c0, w), dp_ref in zip(PIECES, dp_refs):
            dpb = dp_ref[...].astype(BF16)
            dpb_ref[:, c0:c0 + w] = dpb
            dxn = dxn + _dot(dpb, w_ref[c0:c0 + w, :])
        dh, s_dxn, s_dxr = _rms_mod_bwd(dxn, r, inv, g, sc)
        dh_ref[...] = dres_ref[...] + dh

        @pl.when(i % tps == 0)
        def _():
            mg_ref[...] = jnp.zeros(mg_ref.shape, F32)

        mg_ref[0, 0:1, :] += s_dxn
        mg_ref[0, 1:2, :] += s_dxr

    tok = pl.BlockSpec((tm, D), lambda i: (i, 0))
    return _call(
        body, name="inproj_bwd", grid=(T // tm,),
        in_specs=[tok, tok] + [pl.BlockSpec((tm, w), lambda i: (i, 0)) for _, _, w in PIECES]
        + [pl.BlockSpec((1, 9, D), lambda i: (i // tps, 0, 0)), pl.BlockSpec((1, D), lambda i: (0, 0)), _vmem()],
        out_specs=[tok, tok, pl.BlockSpec((tm, IN_COLS), lambda i: (i, 0)),
                   pl.BlockSpec((1, 8, D), lambda i: (i // tps, 0, 0))],
        out_shape=[jax.ShapeDtypeStruct((T, D), F32), jax.ShapeDtypeStruct((T, D), BF16),
                   jax.ShapeDtypeStruct((T, IN_COLS), BF16), jax.ShapeDtypeStruct((B, 8, D), F32)],
        args=(h, dh_res, *dpieces, mods, gn, w_in), semantics=("arbitrary",), plans=plans)


def _seg_mean(x):
    i = lax.broadcasted_iota(jnp.int32, (128, 128), 0) >> 6
    j = lax.broadcasted_iota(jnp.int32, (128, 128), 1) >> 6
    ones = jnp.where(i == j, 1.0 / HD, 0.0).astype(BF16)
    hi = x.astype(BF16)
    lo = (x - hi.astype(F32)).astype(BF16)
    return _dot(hi, ones) + _dot(lo, ones)


def _head_norm(x, g2):
    inv = lax.rsqrt(_seg_mean(x * x) + EPS)
    y = x * inv
    return y * g2, y, inv


def _head_norm_bwd(dxn, y, inv, g2):
    dy = dxn * g2
    return inv * (dy - y * _seg_mean(dy * y)), jnp.sum(dxn * y, axis=0, keepdims=True)


def _swa_block(q, kk, vv, gq2, gk2, sinks, first, do=None):
    lo = lax.broadcasted_iota(jnp.int32, (1, 128), 1) < HD
    kn, ky, kinv = _head_norm(kk, gk2)

    def operands(x):
        xr = pltpu.roll(x, HD, 1)
        own_lo, own_hi = jnp.where(lo, x, 0.0).astype(BF16), jnp.where(lo, 0.0, x).astype(BF16)
        rolled_lo, rolled_hi = jnp.where(lo, xr, 0.0).astype(BF16), jnp.where(lo, 0.0, xr).astype(BF16)
        return (own_lo, rolled_hi), (rolled_lo, own_hi)

    def restore(parts):
        (own_lo, rolled_hi), (rolled_lo, own_hi) = parts
        return (jnp.where(lo, own_lo, own_hi)
                + pltpu.roll(jnp.where(lo, rolled_lo, rolled_hi), HD, 1))

    k_ops, v_ops = operands(kn), operands(vv)
    k2 = [jnp.concatenate(pair, axis=0) for pair in k_ops]
    def stack(x):
        return jnp.concatenate([x[:, 128 * p:128 * (p + 1)] for p in range(4)], axis=0)

    def unstack(x):
        return jnp.concatenate([x[BLK * p:BLK * (p + 1)] for p in range(4)], axis=1)

    def of_head(x, kh):
        return x[2 * BLK * kh:2 * BLK * (kh + 1)]

    nq = 4 * BLK
    pair = lax.broadcasted_iota(jnp.int32, (nq, 1), 0) >> 7
    row = lax.broadcasted_iota(jnp.int32, (nq, 2 * BLK), 0) & (BLK - 1)
    col = lax.broadcasted_iota(jnp.int32, (nq, 2 * BLK), 1)
    valid = (col <= row + BLK) & (col > row) & (col >= jnp.where(first, BLK, 0))
    scale = HD ** -0.5
    qn, qy, qinv = _head_norm(stack(q), gq2)
    qnb = qn.astype(BF16)
    s2 = jnp.concatenate([_dot_nt(of_head(qnb, kh), k2[kh]) for kh in range(N_KV)], axis=0) * scale
    probs, p_sink = [], []
    for j in range(2):
        s = jnp.where(valid, s2[:, 2 * BLK * j:2 * BLK * (j + 1)], NEG)
        sink = jnp.zeros((nq, 1), F32)
        for p in range(4):
            sink = jnp.where(pair == p, sinks[:, 2 * p + j:2 * p + j + 1], sink)
        m = jnp.maximum(jnp.max(s, axis=-1, keepdims=True), sink)
        e = jnp.exp(s - m)
        e_sink = jnp.exp(sink - m)
        rden = 1.0 / (jnp.sum(e, axis=-1, keepdims=True) + e_sink)
        probs.append(e * rden)
        p_sink.append(e_sink * rden)
    pb = [p.astype(BF16) for p in probs]
    if do is None:
        return unstack(jnp.concatenate(
            [_dot(of_head(pb[0], kh), v_ops[kh][0]) + _dot(of_head(pb[1], kh), v_ops[kh][1]) for kh in range(N_KV)],
            axis=0))
    dob = stack(do).astype(BF16)
    ds, dsinks = [], [None] * 8
    for j in range(2):
        dp = jnp.concatenate([_dot_nt(of_head(dob, kh), v_ops[kh][j]) for kh in range(N_KV)], axis=0)
        t = jnp.sum(dp * probs[j], axis=-1, keepdims=True)
        ds.append(probs[j] * (dp - t) * scale)
        lost = p_sink[j] * t
        for p in range(4):
            dsinks[2 * p + j] = -jnp.sum(lost[BLK * p:BLK * (p + 1)])
    dsb = jnp.concatenate(ds, axis=1).astype(BF16)
    dqn = jnp.concatenate([_dot(of_head(dsb, kh), k2[kh]) for kh in range(N_KV)], axis=0)
    dq, dgq2 = _head_norm_bwd(dqn, qy, qinv, gq2)
    dk2 = [_dot_tn(of_head(dsb, kh), of_head(qnb, kh)) for kh in range(N_KV)]
    dv_ops = [[_dot_tn(of_head(pb[j], kh), of_head(dob, kh)) for j in range(2)] for kh in range(N_KV)]
    dkn = restore(tuple((d[:2 * BLK], d[2 * BLK:]) for d in dk2))
    dvv = restore(tuple(tuple(d) for d in dv_ops))
    dkk, dgk2 = _head_norm_bwd(dkn, ky, kinv, gk2)
    return unstack(dq), dkk, dvv, dgq2, dgk2, dsinks


SWA_GROUP = 16


def _swa_specs(nbs):
    grp = min(SWA_GROUP, nbs)
    rows = grp * BLK

    def tok(w):
        return pl.BlockSpec((rows, w), lambda i: (i, 0))

    halo = pl.BlockSpec((BLK, 128), lambda i: (jnp.maximum(i * grp - 1, 0), 0))
    vec = pl.BlockSpec((1, 128), lambda i: (0, 0))
    sk = pl.BlockSpec((1, 8), lambda i: (0, 0))
    return grp, tok, halo, vec, sk


def _swa_fwd(q, k, v, gq2, gk2, sinks, nbs, plans=()):
    T = q.shape[0]
    grp, tok, halo, vec, sk = _swa_specs(nbs)

    def body(q_ref, kh_ref, kc_ref, vh_ref, vc_ref, gq_ref, gk_ref, sk_ref, o_ref):
        seq_start = ((pl.program_id(0) * grp) % nbs) == 0
        for g in range(grp):
            rows = slice(g * BLK, (g + 1) * BLK)
            prev = slice((g - 1) * BLK, g * BLK)
            kk = jnp.concatenate([kh_ref[...] if g == 0 else kc_ref[prev, :], kc_ref[rows, :]], axis=0)
            vv = jnp.concatenate([vh_ref[...] if g == 0 else vc_ref[prev, :], vc_ref[rows, :]], axis=0)
            o_ref[rows, :] = _swa_block(q_ref[rows, :], kk, vv, gq_ref[...], gk_ref[...], sk_ref[...],
                                        seq_start if g == 0 else False)

    (out,), plan_outs = _call(
        body, name="swa_fwd", grid=(T // (grp * BLK),),
        in_specs=[tok(D_B), halo, tok(128), halo, tok(128), vec, vec, sk], out_specs=[tok(D_B)],
        out_shape=[jax.ShapeDtypeStruct((T, D_B), F32)], args=(q, k, k, v, v, gq2, gk2, sinks),
        semantics=("parallel",), plans=plans)
    return out, plan_outs


def _swa_bwd(q, k, v, gq2, gk2, sinks, do, nbs, plans=()):
    T = q.shape[0]
    grp, tok, halo, vec, sk = _swa_specs(nbs)
    steps = T // (grp * BLK)

    def body(q_ref, kh_ref, kc_ref, vh_ref, vc_ref, gq_ref, gk_ref, sk_ref, do_ref,
             dq_ref, dk_ref, dv_ref, dkh_ref, dvh_ref, dgq_ref, dgk_ref, dsk_ref):
        i = pl.program_id(0)
        seq_start = ((i * grp) % nbs) == 0

        @pl.when(i == 0)
        def _():
            for r in (dgq_ref, dgk_ref, dsk_ref):
                r[...] = jnp.zeros(r.shape, F32)

        res = []
        for g in range(grp):
            rows = slice(g * BLK, (g + 1) * BLK)
            prev = slice((g - 1) * BLK, g * BLK)
            kk = jnp.concatenate([kh_ref[...] if g == 0 else kc_ref[prev, :], kc_ref[rows, :]], axis=0)
            vv = jnp.concatenate([vh_ref[...] if g == 0 else vc_ref[prev, :], vc_ref[rows, :]], axis=0)
            res.append(_swa_block(q_ref[rows, :], kk, vv, gq_ref[...], gk_ref[...], sk_ref[...],
                                  seq_start if g == 0 else False, do=do_ref[rows, :]))
        lane = lax.broadcasted_iota(jnp.int32, (8, 128), 1)
        upd = jnp.zeros((8, 128), F32)
        for g, (dq, dkk, dvv, dgq2, dgk2, dsinks) in enumerate(res):
            rows = slice(g * BLK, (g + 1) * BLK)
            dq_ref[rows, :] = dq
            dk_ref[rows, :] = dkk[BLK:] + res[g + 1][1][:BLK] if g + 1 < grp else dkk[BLK:]
            dv_ref[rows, :] = dvv[BLK:] + res[g + 1][2][:BLK] if g + 1 < grp else dvv[BLK:]
            dgq_ref[...] += dgq2
            dgk_ref[...] += dgk2
            for h, d in enumerate(dsinks):
                upd = upd + jnp.where(lane == h, d, 0.0)
        dkh_ref[0] = res[0][1][:BLK]
        dvh_ref[0] = res[0][2][:BLK]
        dsk_ref[...] += upd

    one = pl.BlockSpec((1, BLK, 128), lambda i: (i, 0, 0))
    halo_shape = jax.ShapeDtypeStruct((steps, BLK, 128), F32)
    return _call(
        body, name="swa_bwd", grid=(steps,),
        in_specs=[tok(D_B), halo, tok(128), halo, tok(128), vec, vec, sk, tok(D_B)],
        out_specs=[tok(D_B), tok(128), tok(128), one, one, vec, vec, pl.BlockSpec((8, 128), lambda i: (0, 0))],
        out_shape=[jax.ShapeDtypeStruct((T, D_B), F32), jax.ShapeDtypeStruct((T, 128), F32),
                   jax.ShapeDtypeStruct((T, 128), F32), halo_shape, halo_shape, jax.ShapeDtypeStruct((1, 128), F32),
                   jax.ShapeDtypeStruct((1, 128), F32), jax.ShapeDtypeStruct((8, 128), F32)],
        args=(q, k, k, v, v, gq2, gk2, sinks, do), semantics=("arbitrary",), plans=plans)


def _swa_add_halo(dk, dk_halo):
    steps = dk_halo.shape[0]
    nxt = jnp.concatenate([dk_halo[1:], jnp.zeros_like(dk_halo[:1])], axis=0)[:, None]
    dk = dk.reshape(steps, -1, BLK, 128)
    return jnp.concatenate([dk[:, :-1], dk[:, -1:] + nxt], axis=1).reshape(-1, 128)


def _sgu_norm(av, g_ln, b_ln):
    t, th = _gelu(av)
    mu = jnp.mean(t, axis=-1, keepdims=True)
    tc = t - mu
    rstd = lax.rsqrt(jnp.mean(tc * tc, axis=-1, keepdims=True) + EPS)
    vhat = tc * rstd
    return vhat * g_ln + b_ln, vhat, rstd, th


def _masked_ws(ws_ref):
    tril = lax.broadcasted_iota(jnp.int32, (BLK, BLK), 0) >= lax.broadcasted_iota(jnp.int32, (BLK, BLK), 1)
    return [jnp.where(tril, ws_ref[g], 0.0).astype(BF16) for g in range(N_GRP)]


def _mix_fwd(au, av, gta, gtb, ob, h, mods, g_ln, b_ln, ws, bsb, wa, wb, wout, seq, plans=()):
    T = h.shape[0]
    tm = _token_tile(seq)
    tps = seq // tm

    def body(au_ref, av_ref, gta_ref, gtb_ref, ob_ref, h_ref, m_ref, gl_ref, bl_ref, ws_ref, bs_ref,
             wa_ref, wb_ref, wo_ref, out_ref, vvb_s, z_s):
        u, _ = _gelu(au_ref[...])
        vv, _, _, _ = _sgu_norm(av_ref[...], gl_ref[...], bl_ref[...])
        vvb_s[...] = vv.astype(BF16)
        wsm = _masked_ws(ws_ref)
        for c in range(tm // BLK):
            rows = slice(c * BLK, (c + 1) * BLK)
            for g in range(N_GRP):
                cols = slice(g * BLK, (g + 1) * BLK)
                z_s[rows, cols] = _dot(wsm[g], vvb_s[rows, cols]) + bs_ref[g]
        ya = _dot_nt((u * z_s[...]).astype(BF16), wa_ref[...])
        yb = _dot_nt(ob_ref[...].astype(BF16), wb_ref[...])
        merged = jax.nn.sigmoid(gta_ref[...]) * ya + jax.nn.sigmoid(gtb_ref[...]) * yb
        out_ref[...] = h_ref[...] + m_ref[0, 5:6, :] * _dot(merged.astype(BF16), wo_ref[...])

    def tok(w):
        return pl.BlockSpec((tm, w), lambda i: (i, 0))

    def full(shape):
        return pl.BlockSpec(shape, lambda i: (0,) * len(shape))

    (out,), plan_outs = _call(
        body, name="mix_fwd", grid=(T // tm,),
        in_specs=[tok(D_A), tok(D_A), tok(D), tok(D), tok(D_B), tok(D),
                  pl.BlockSpec((1, 9, D), lambda i: (i // tps, 0, 0)), full((1, D_A)), full((1, D_A)),
                  full((N_GRP, BLK, BLK)), full((N_GRP, BLK, BLK)), _vmem(), _vmem(), _vmem()],
        out_specs=[tok(D)], out_shape=[jax.ShapeDtypeStruct((T, D), F32)],
        scratch_shapes=[pltpu.VMEM((tm, D_A), BF16), pltpu.VMEM((tm, D_A), F32)],
        args=(au, av, gta, gtb, ob, h, mods, g_ln, b_ln, ws, bsb, wa, wb, wout), semantics=("parallel",),
        plans=plans)
    return out, plan_outs


def _mix_bwd(au, av, gta, gtb, ob, dh, mods, g_ln, b_ln, ws, bsb, wa, wb, wout, seq, plans=()):
    T = dh.shape[0]
    B = T // seq
    tm = _token_tile(seq, 256)
    tps = seq // tm

    def body(au_ref, av_ref, gta_ref, gtb_ref, ob_ref, dh_ref, m_ref, gl_ref, bl_ref, ws_ref, bs_ref,
             wa_ref, wb_ref, wo_ref,
             dau_ref, dav_ref, dgta_ref, dgtb_ref, dob_ref, dwo_ref, dwa_ref, dwb_ref, dws_ref, dbs_ref, dln_ref,
             mg_ref, vvb_s, z_s, dz_s, dzb_s, dvv_s):
        i = pl.program_id(0)

        @pl.when(i == 0)
        def _():
            for r in (dwo_ref, dwa_ref, dwb_ref, dws_ref, dbs_ref, dln_ref):
                r[...] = jnp.zeros(r.shape, F32)

        @pl.when(i % tps == 0)
        def _():
            mg_ref[...] = jnp.zeros(mg_ref.shape, F32)

        auv = au_ref[...]
        avv = av_ref[...]
        u, thu = _gelu(auv)
        g_ln = gl_ref[...]
        vv, vhat, rstd, thv = _sgu_norm(avv, g_ln, bl_ref[...])
        vvb_s[...] = vv.astype(BF16)
        wsm = _masked_ws(ws_ref)
        for c in range(tm // BLK):
            rows = slice(c * BLK, (c + 1) * BLK)
            for g in range(N_GRP):
                cols = slice(g * BLK, (g + 1) * BLK)
                z_s[rows, cols] = _dot(wsm[g], vvb_s[rows, cols]) + bs_ref[g]
        z = z_s[...]
        yab = (u * z).astype(BF16)
        obb = ob_ref[...].astype(BF16)
        ya = _dot_nt(yab, wa_ref[...])
        yb = _dot_nt(obb, wb_ref[...])
        sa = jax.nn.sigmoid(gta_ref[...])
        sb = jax.nn.sigmoid(gtb_ref[...])
        mb = (sa * ya + sb * yb).astype(BF16)
        dhv = dh_ref[...]
        mg_ref[0, 0:1, :] += jnp.sum(dhv * _dot(mb, wo_ref[...]), axis=0, keepdims=True)
        dmob = (m_ref[0, 5:6, :] * dhv).astype(BF16)
        dwo_ref[...] += _dot_tn(mb, dmob)
        dmerged = _dot_nt(dmob, wo_ref[...])
        dya = dmerged * sa
        dyb = dmerged * sb
        dgta_ref[...] = dya * ya * (1.0 - sa)
        dgtb_ref[...] = dyb * yb * (1.0 - sb)
        dyab = dya.astype(BF16)
        dybb = dyb.astype(BF16)
        dwa_ref[...] += _dot_tn(dyab, yab)
        dwb_ref[...] += _dot_tn(dybb, obb)
        dob_ref[...] = _dot(dybb, wb_ref[...])
        dyap = _dot(dyab, wa_ref[...])
        dau_ref[...] = (dyap * z) * _gelu_grad(auv, thu)
        dz = dyap * u
        dz_s[...] = dz
        dzb_s[...] = dz.astype(BF16)
        for c in range(tm // BLK):
            rows = slice(c * BLK, (c + 1) * BLK)
            for g in range(N_GRP):
                cols = slice(g * BLK, (g + 1) * BLK)
                dzb = dzb_s[rows, cols]
                dvv_s[rows, cols] = _dot_tn(wsm[g], dzb)
                dws_ref[g] += _dot_nt(dzb, vvb_s[rows, cols])
                dbs_ref[g] += dz_s[rows, cols]
        dvv = dvv_s[...]
        dln_ref[0:1, :] += jnp.sum(dvv * vhat, axis=0, keepdims=True)
        dln_ref[1:2, :] += jnp.sum(dvv, axis=0, keepdims=True)
        dvh = dvv * g_ln
        dt = rstd * (dvh - jnp.mean(dvh, axis=-1, keepdims=True)
                     - vhat * jnp.mean(dvh * vhat, axis=-1, keepdims=True))
        dav_ref[...] = dt * _gelu_grad(avv, thv)

    def tok(w):
        return pl.BlockSpec((tm, w), lambda i: (i, 0))

    def full(shape):
        return pl.BlockSpec(shape, lambda i: (0,) * len(shape))

    return _call(
        body, name="mix_bwd", grid=(T // tm,),
        in_specs=[tok(D_A), tok(D_A), tok(D), tok(D), tok(D_B), tok(D),
                  pl.BlockSpec((1, 9, D), lambda i: (i // tps, 0, 0)), full((1, D_A)), full((1, D_A)),
                  full((N_GRP, BLK, BLK)), full((N_GRP, BLK, BLK)), _vmem(), _vmem(), _vmem()],
        out_specs=[tok(D_A), tok(D_A), tok(D), tok(D), tok(D_B), full((D, D)), full((D, D_A)), full((D, D_B)),
                   full((N_GRP, BLK, BLK)), full((N_GRP, BLK, BLK)), full((8, D_A)),
                   pl.BlockSpec((1, 8, D), lambda i: (i // tps, 0, 0))],
        out_shape=[jax.ShapeDtypeStruct((T, D_A), F32), jax.ShapeDtypeStruct((T, D_A), F32),
                   jax.ShapeDtypeStruct((T, D), F32), jax.ShapeDtypeStruct((T, D), F32),
                   jax.ShapeDtypeStruct((T, D_B), F32), jax.ShapeDtypeStruct((D, D), F32),
                   jax.ShapeDtypeStruct((D, D_A), F32), jax.ShapeDtypeStruct((D, D_B), F32),
                   jax.ShapeDtypeStruct((N_GRP, BLK, BLK), F32), jax.ShapeDtypeStruct((N_GRP, BLK, BLK), F32),
                   jax.ShapeDtypeStruct((8, D_A), F32), jax.ShapeDtypeStruct((B, 8, D), F32)],
        scratch_shapes=[pltpu.VMEM((tm, D_A), BF16), pltpu.VMEM((tm, D_A), F32), pltpu.VMEM((tm, D_A), F32),
                        pltpu.VMEM((tm, D_A), BF16), pltpu.VMEM((tm, D_A), F32)],
        args=(au, av, gta, gtb, ob, dh, mods, g_ln, b_ln, ws, bsb, wa, wb, wout), semantics=("arbitrary",),
        plans=plans)


def _prologue(c_pad, w_ada, first_shards):
    cols = w_ada.shape[1]
    plan_w, plan_c = _GatherRelayed(first_shards), _GatherDirect([c_pad])
    plan_m = _Gather([jax.ShapeDtypeStruct((N_DEV * 8, cols), F32)])
    n_w = len(first_shards)

    def body(c_ref, wada_ref, *rest):
        w_ins, rest = rest[:n_w], rest[n_w:]
        call_ref, mods_ref = rest[:2]
        w_outs, rest = rest[2:2 + n_w], rest[2 + n_w:]
        cvm, part, local_sem = rest[:3]
        sems = rest[3:]
        sems_w, sems_c, sems_m = sems[:3], sems[3:6], sems[6:9]
        plan_c.start([c_ref], [call_ref], sems_c)
        plan_c.relay([c_ref], [call_ref], sems_c)
        plan_c.finish([c_ref], [call_ref], sems_c)
        load = pltpu.make_async_copy(call_ref, cvm, local_sem)
        load.start()
        load.wait()
        cv = cvm[...].reshape(N_DEV * 8, D)
        part[...] = _dot((cv * jax.nn.sigmoid(cv)).astype(BF16), wada_ref[...].astype(BF16))
        plan_m.start([part], [mods_ref], sems_m)
        plan_w.start(w_ins, w_outs, sems_w)
        plan_m.relay([part], [mods_ref], sems_m)
        plan_m.finish([part], [mods_ref], sems_m)
        plan_w.relay(w_ins, w_outs, sems_w)
        plan_w.finish(w_ins, w_outs, sems_w)

    res = pl.pallas_call(
        body, name="prologue", in_specs=[_any(), _vmem()] + [_any()] * n_w,
        out_specs=[_any()] * (2 + n_w), out_shape=plan_c.out_shape + plan_m.out_shape + plan_w.out_shape,
        scratch_shapes=[pltpu.VMEM((N_DEV, 8, D), F32), pltpu.VMEM((N_DEV * 8, cols), F32), pltpu.SemaphoreType.DMA]
        + plan_w.scratch + plan_c.scratch + plan_m.scratch,
        compiler_params=pltpu.CompilerParams(vmem_limit_bytes=VMEM_LIMIT),
    )(c_pad, w_ada, *first_shards)
    return res[0], res[1], list(res[2:])


def _ada_update(c_all, dm_cols, w, m, v):
    n, cols = c_all.shape[0], w.shape[1]
    br = 256

    def body(c_ref, dm_ref, w_ref, m_ref, v_ref, g_ref, d_ref, nm_ref, nv_ref):
        cv = c_ref[...]
        g = _dot_tn((cv * jax.nn.sigmoid(cv)).astype(BF16), dm_ref[...].astype(BF16))
        g_ref[...] = g
        d_ref[...], nm_ref[...], nv_ref[...] = _adamw(w_ref[...], g, m_ref[...], v_ref[...])

    rows = pl.BlockSpec((br, cols), lambda i: (i, 0))
    outs, _ = _call(
        body, name="ada_update", grid=(D // br,),
        in_specs=[pl.BlockSpec((n, br), lambda i: (0, i)), pl.BlockSpec((n, cols), lambda i: (0, 0)), rows, rows, rows],
        out_specs=[rows] * 4, out_shape=[jax.ShapeDtypeStruct(w.shape, F32)] * 4, args=(c_all, dm_cols, w, m, v),
        semantics=("parallel",))
    return outs


def _mod_finish(mg1, mg2, mg2g, mg3, mods, g1, g2, g3):
    B = mods.shape[0]

    def body(mg1_ref, mg2_ref, mg2g_ref, mg3_ref, m_ref, g1_ref, g2_ref, g3_ref, dm_ref, dgn_ref):
        dgn_ref[...] = jnp.zeros(dgn_ref.shape, F32)
        for k, (mg, g_ref) in enumerate(((mg1_ref, g1_ref), (mg2_ref, g2_ref), (mg3_ref, g3_ref))):
            for b in range(B):
                s_dxr = mg[b, 1:2, :]
                dm_ref[b, 3 * k:3 * k + 1, :] = mg[b, 0:1, :]
                dm_ref[b, 3 * k + 1:3 * k + 2, :] = g_ref[...] * s_dxr
                dm_ref[b, 3 * k + 2:3 * k + 3, :] = mg2g_ref[b, 0:1, :] if k == 1 else mg[b, 2:3, :]
                dgn_ref[k:k + 1, :] += (1.0 + m_ref[b, 3 * k + 1:3 * k + 2, :]) * s_dxr

    args = (mg1, mg2, mg2g, mg3, mods, g1, g2, g3)
    out_shape = [jax.ShapeDtypeStruct((B, 9, D), F32), jax.ShapeDtypeStruct((8, D), F32)]
    return pl.pallas_call(body, name="mod_finish", grid=(1,), in_specs=[_whole(a) for a in args],
                          out_specs=[_whole(o) for o in out_shape], out_shape=out_shape,
                          compiler_params=_params("arbitrary"))(*args)


def _small_update(gathered, params, ms, vs):
    n = len(SMALL)
    B = gathered[0].shape[1]

    def body(*refs):
        gdm, ggn, gln, gws, gbs, ggq, ggk, gsk, gloss = refs[:9]
        w = dict(zip(SMALL, refs[9:9 + n]))
        m = dict(zip(SMALL, refs[9 + n:9 + 2 * n]))
        v = dict(zip(SMALL, refs[9 + 2 * n:9 + 3 * n]))
        outs = refs[9 + 3 * n:-1]
        out = {name: outs[4 * k:4 * k + 4] for k, name in enumerate(SMALL)}

        def total(ref, idx):
            acc = ref[(0,) + idx]
            for dev in range(1, N_DEV):
                acc = acc + ref[(dev,) + idx]
            return acc

        def finish(name, g, idx=(Ellipsis,)):
            d, nm, nv = _adamw(w[name][idx], g, m[name][idx], v[name][idx])
            for ref, val in zip(out[name], (g, d, nm, nv)):
                ref[idx] = val

        g_bada = total(gdm, (slice(0, 1),))
        for b in range(1, B):
            g_bada = g_bada + total(gdm, (slice(b, b + 1),))
        finish("b_ada", g_bada)
        finish("g_norm1", total(ggn, (slice(0, 1),)))
        finish("g_norm2", total(ggn, (slice(1, 2),)))
        finish("g_norm3", total(ggn, (slice(2, 3),)))
        finish("g_sgu_ln", total(gln, (slice(0, 1),)))
        finish("b_sgu_ln", total(gln, (slice(1, 2),)))
        tril = lax.broadcasted_iota(jnp.int32, (BLK, BLK), 0) >= lax.broadcasted_iota(jnp.int32, (BLK, BLK), 1)
        for g in range(N_GRP):
            finish("w_spatial", jnp.where(tril, total(gws, (g,)), 0.0), (0, g))
            finish("b_spatial", jnp.sum(total(gbs, (g,)).T, axis=0, keepdims=True), (0, slice(g, g + 1)))
        finish("g_q", total(ggq, ()))
        finish("g_k", total(ggk, ()))
        finish("attn_sinks", total(gsk, (slice(0, 1), slice(0, N_KV * Q_PER_KV))))
        refs[-1][...] = total(gloss, ())

    args = list(gathered) + [params[k] for k in SMALL] + [ms[k] for k in SMALL] + [vs[k] for k in SMALL]
    out_shape = []
    for k in SMALL:
        out_shape += [jax.ShapeDtypeStruct(params[k].shape, F32)] * 4
    out_shape.append(jax.ShapeDtypeStruct((8, 128), F32))
    res = pl.pallas_call(body, name="small_update", grid=(1,), in_specs=[_whole(a) for a in args],
                         out_specs=[_whole(o) for o in out_shape], out_shape=out_shape,
                         compiler_params=_params("arbitrary"))(*args)
    return {k: res[4 * i:4 * i + 4] for i, k in enumerate(SMALL)}, res[-1][0, 0]


def _rs_add(p, r, core, name):
    _, rows, width = p.shape

    def body(c_ref, p_ref, r_ref, o_ref):
        o_ref[...] = (p_ref[...] + r_ref[...]).astype(BF16)

    return pl.pallas_call(
        body, name="rs_add_" + name, out_shape=jax.ShapeDtypeStruct((4, rows, width), BF16),
        grid_spec=pltpu.PrefetchScalarGridSpec(
            num_scalar_prefetch=1, grid=(4,),
            in_specs=[pl.BlockSpec((1, rows, width), lambda k, c_ref: (2 * k + c_ref[0], 0, 0)),
                      pl.BlockSpec((1, rows, width), lambda k, c_ref: (k, 0, 0))],
            out_specs=pl.BlockSpec((1, rows, width), lambda k, c_ref: (k, 0, 0))),
        compiler_params=_params("parallel"),
    )(core, p, r)


def _adam_big(w, m, v, r, name):
    rows, cols = w.shape

    def body(r_ref, w_ref, m_ref, v_ref, go_ref, d_ref, nm_ref, nv_ref):
        g = r_ref[0].astype(F32)
        for k in range(1, 4):
            g = g + r_ref[k].astype(F32)
        go_ref[...] = g
        d_ref[...], nm_ref[...], nv_ref[...] = _adamw(w_ref[...], g, m_ref[...], v_ref[...])

    br = rows // 2
    shard = pl.BlockSpec((br, cols), lambda i: (i, 0))
    return pl.pallas_call(body, name="adam_" + name, grid=(2,),
                          in_specs=[pl.BlockSpec((4, br, cols), lambda i: (0, i, 0)), shard, shard, shard],
                          out_specs=[shard] * 4, out_shape=[jax.ShapeDtypeStruct(w.shape, F32)] * 4,
                          compiler_params=_params("parallel"))(r, w, m, v)


def kernel(x, c, w_ada, b_ada, g_norm1, ffn1_w_gate, ffn1_w_up, ffn1_w_down, g_norm2, w_in, g_sgu_ln, b_sgu_ln, w_spatial, b_spatial, g_q, g_k, attn_sinks, w_branch_a, w_branch_b, w_out, g_norm3, ffn2_w_gate, ffn2_w_up, ffn2_w_down, loss_target, m_w_ada, m_b_ada, m_g_norm1, m_ffn1_w_gate, m_ffn1_w_up, m_ffn1_w_down, m_g_norm2, m_w_in, m_g_sgu_ln, m_b_sgu_ln, m_w_spatial, m_b_spatial, m_g_q, m_g_k, m_attn_sinks, m_w_branch_a, m_w_branch_b, m_w_out, m_g_norm3, m_ffn2_w_gate, m_ffn2_w_up, m_ffn2_w_down, v_w_ada, v_b_ada, v_g_norm1, v_ffn1_w_gate, v_ffn1_w_up, v_ffn1_w_down, v_g_norm2, v_w_in, v_g_sgu_ln, v_b_sgu_ln, v_w_spatial, v_b_spatial, v_g_q, v_g_k, v_attn_sinks, v_w_branch_a, v_w_branch_b, v_w_out, v_g_norm3, v_ffn2_w_gate, v_ffn2_w_up, v_ffn2_w_down):
    names = ("w_ada", "b_ada", "g_norm1", "ffn1_w_gate", "ffn1_w_up", "ffn1_w_down", "g_norm2", "w_in", "g_sgu_ln",
             "b_sgu_ln", "w_spatial", "b_spatial", "g_q", "g_k", "attn_sinks", "w_branch_a", "w_branch_b", "w_out",
             "g_norm3", "ffn2_w_gate", "ffn2_w_up", "ffn2_w_down")
    w = dict(zip(names, (w_ada, b_ada, g_norm1, ffn1_w_gate, ffn1_w_up, ffn1_w_down, g_norm2, w_in, g_sgu_ln,
                         b_sgu_ln, w_spatial, b_spatial, g_q, g_k, attn_sinks, w_branch_a, w_branch_b, w_out, g_norm3,
                         ffn2_w_gate, ffn2_w_up, ffn2_w_down)))
    m = dict(zip(names, (m_w_ada, m_b_ada, m_g_norm1, m_ffn1_w_gate, m_ffn1_w_up, m_ffn1_w_down, m_g_norm2, m_w_in,
                         m_g_sgu_ln, m_b_sgu_ln, m_w_spatial, m_b_spatial, m_g_q, m_g_k, m_attn_sinks, m_w_branch_a,
                         m_w_branch_b, m_w_out, m_g_norm3, m_ffn2_w_gate, m_ffn2_w_up, m_ffn2_w_down)))
    v = dict(zip(names, (v_w_ada, v_b_ada, v_g_norm1, v_ffn1_w_gate, v_ffn1_w_up, v_ffn1_w_down, v_g_norm2, v_w_in,
                         v_g_sgu_ln, v_b_sgu_ln, v_w_spatial, v_b_spatial, v_g_q, v_g_k, v_attn_sinks, v_w_branch_a,
                         v_w_branch_b, v_w_out, v_g_norm3, v_ffn2_w_gate, v_ffn2_w_up, v_ffn2_w_down)))
    B, seq, _ = x.shape
    T = B * seq
    nbs = seq // BLK
    xi, yi, ci = _place()
    me = 4 * xi + 2 * yi + ci
    core = jnp.reshape(ci, (1,)).astype(jnp.int32)
    layout = {name: (tform, n, width) for name, tform, n, width in BIG}
    shard = {name: w[name][0].astype(BF16).T if tform else w[name][0].astype(BF16) for name, tform, _, _ in BIG}
    wts, parts, big_out = {}, {}, {}

    def gather_plan(group):
        return _Gather([shard[k] for k in group])

    def take(group, gathered):
        for k, g in zip(group, gathered):
            wts[k] = g.reshape(N_DEV * layout[k][1], layout[k][2])

    def blocks(name, grad):
        return grad.reshape(N_DEV, layout[name][1], layout[name][2])

    def to_sibling(names, grads):
        return _RsSibling([blocks(k, g) for k, g in zip(names, grads)])

    def add(names, grads, from_sibling):
        for k, g, r in zip(names, grads, from_sibling):
            parts[k] = _rs_add(blocks(k, g), r, core, k)

    def to_chips(names):
        return _RsChips([parts[k] for k in names])

    def update(names, from_chips):
        for k, r in zip(names, from_chips):
            if layout[k][0]:
                big_out[k] = [o.T[None] for o in _adam_big(w[k][0].T, m[k][0].T, v[k][0].T, r, k)]
            else:
                big_out[k] = [o[None] for o in _adam_big(w[k][0], m[k][0], v[k][0], r, k)]

    g1, g2, g3, g_ln, b_ln = g_norm1, g_norm2, g_norm3, g_sgu_ln, b_sgu_ln
    gq2, gk2 = jnp.tile(g_q, (1, 2)), jnp.tile(g_k, (1, 2))
    ws = w_spatial[0]
    bsb = jnp.broadcast_to(b_spatial[0][:, :, None], (N_GRP, BLK, BLK))
    xf = x.reshape(T, D)
    tgt = loss_target.reshape(T, D)

    ffn1 = ("ffn1_w_gate", "ffn1_w_up", "ffn1_w_down")
    ffn2 = ("ffn2_w_gate", "ffn2_w_up", "ffn2_w_down")
    c_pad = jnp.concatenate([c, jnp.zeros((8 - B, D), F32)], axis=0)
    c_all, mods_cols, gathered = _prologue(c_pad, w_ada[0], [shard[k] for k in ffn1[:2]])
    take(ffn1[:2], gathered)
    c_all = c_all[:, :B].reshape(N_DEV * B, D)
    mine = lax.dynamic_slice_in_dim(mods_cols, 8 * me, B, axis=1)
    mods = (mine.transpose(1, 0, 2).reshape(B, 9 * D) + b_ada).reshape(B, 9, D)

    group = ("ffn1_w_down", "w_branch_a", "w_branch_b", "w_out")
    (gb1, ub1), (gathered,) = _ffn_gate_up(xf, mods, g1, wts["ffn1_w_gate"], wts["ffn1_w_up"], 0, seq,
                                           plans=[gather_plan(group)])
    take(group, gathered)
    group = ("w_in",)
    (h1, y1), (gathered,) = _ffn_down(xf, gb1, ub1, mods, wts["ffn1_w_down"], 0, seq, plans=[gather_plan(group)])
    take(group, gathered)
    group = ("ffn2_w_gate",)
    (au, av, q_tok, k_tok, v_tok, gta, gtb), (gathered,) = _inproj_fwd(h1, mods, g2, wts["w_in"], seq,
                                                                       plans=[gather_plan(group)])
    take(group, gathered)
    group = ("ffn2_w_up",)
    ob, (gathered,) = _swa_fwd(q_tok, k_tok, v_tok, gq2, gk2, attn_sinks, nbs, plans=[gather_plan(group)])
    take(group, gathered)
    mixw = (wts["w_branch_a"], wts["w_branch_b"], wts["w_out"])
    group = ("ffn2_w_down",)
    h2, (gathered,) = _mix_fwd(au, av, gta, gtb, ob, h1, mods, g_ln, b_ln, ws, bsb, *mixw, seq,
                               plans=[gather_plan(group)])
    take(group, gathered)
    dh3, y3, gb3, ub3, loss_parts = _ffn_fwd_loss(h2, mods, g3, *[wts[k] for k in ffn2], 6, seq, tgt)
    loss_part = jnp.full((8, 128), jnp.sum(loss_parts[:, 0, 0]))

    (dh2, xb, dyb, a, dg, du, mg3), _ = _ffn_bwd(h2, dh3, y3, gb3, ub3, mods, g3, *[wts[k] for k in ffn2], 6, seq)
    d_gate, _ = _wgrad(dg, xb, "wgrad_ffn2_gate")
    d_up, (r,) = _wgrad(du, xb, "wgrad_ffn2_up", plans=[to_sibling(ffn2[:1], [d_gate])])
    add(ffn2[:1], [d_gate], r)
    d_down, (r,) = _wgrad(a, dyb, "wgrad_ffn2_down", plans=[to_sibling(ffn2[1:2], [d_up])])
    add(ffn2[1:2], [d_up], r)

    (dau, dav, dgta, dgtb, dob, d_out, d_a, d_b, dws, dbs, dln, mg2g), (r, from_chips) = _mix_bwd(
        au, av, gta, gtb, ob, dh2, mods, g_ln, b_ln, ws, bsb, *mixw, seq,
        plans=[to_sibling(ffn2[2:], [d_down]), to_chips(ffn2[:2])])
    add(ffn2[2:], [d_down], r)
    update(ffn2[:2], from_chips)
    mixers = ("w_out", "w_branch_a", "w_branch_b")
    (dq, dk, dv, dk_halo, dv_halo, dgq2, dgk2, dsk), (from_chips, r) = _swa_bwd(
        q_tok, k_tok, v_tok, gq2, gk2, attn_sinks, dob, nbs,
        plans=[to_chips(ffn2[2:]), to_sibling(mixers, [d_out, d_a, d_b])])
    update(ffn2[2:], from_chips)
    add(mixers, [d_out, d_a, d_b], r)
    dk, dv = _swa_add_halo(dk, dk_halo), _swa_add_halo(dv, dv_halo)
    dgq, dgk = dgq2[:, :HD] + dgq2[:, HD:], dgk2[:, :HD] + dgk2[:, HD:]
    (dh1, xb2, dpb, mg2), (from_chips, early) = _inproj_bwd(
        h1, dh2, (dau, dav, dq, dk, dv, dgta, dgtb), mods, g2, wts["w_in"], seq,
        plans=[to_chips(mixers), _Gather([dln, dws, dbs, dgq, dgk, dsk, loss_part])])
    update(mixers, from_chips)

    d_in, _ = _wgrad(dpb, xb2, "wgrad_w_in")

    (dx, xb, dyb, a, dg, du, mg1), _ = _ffn_bwd(xf, dh1, y1, gb1, ub1, mods, g1, *[wts[k] for k in ffn1], 0, seq)
    dmods, dgn = _mod_finish(mg1, mg2, mg2g, mg3, mods, g1, g2, g3)
    d_gate, (r, late) = _wgrad(dg, xb, "wgrad_ffn1_gate",
                               plans=[to_sibling(("w_in",), [d_in]), _GatherDirect([dmods.reshape(B, 9 * D), dgn])])
    add(("w_in",), [d_in], r)
    gathered = late + early
    d_up, (r, from_chips) = _wgrad(du, xb, "wgrad_ffn1_up",
                                   plans=[to_sibling(ffn1[:1], [d_gate]), to_chips(("w_in",))])
    add(ffn1[:1], [d_gate], r)
    update(("w_in",), from_chips)
    d_down, (r, from_chips) = _wgrad(a, dyb, "wgrad_ffn1_down",
                                     plans=[to_sibling(ffn1[1:2], [d_up]), to_chips(ffn1[:1])])
    add(ffn1[1:2], [d_up], r)
    update(ffn1[:1], from_chips)
    small_out, loss = _small_update(gathered, w, m, v)
    dm_cols = lax.dynamic_slice_in_dim(gathered[0].reshape(N_DEV * B, 9 * D), (9 * D // N_DEV) * me,
                                       9 * D // N_DEV, axis=1)
    ada_out = _ada_update(c_all, dm_cols, w_ada[0], m_w_ada[0], v_w_ada[0])
    r, from_chips = _exchange([to_sibling(ffn1[2:], [d_down]), to_chips(ffn1[1:2])], "rs_tail")
    add(ffn1[2:], [d_down], r)
    update(ffn1[1:2], from_chips)
    (from_chips,) = _exchange([to_chips(ffn1[2:])], "rs_last")
    update(ffn1[2:], from_chips)

    def leaf(kind, name):
        if name == "w_ada":
            return ada_out[kind][None]
        if name in SMALL:
            return small_out[name][kind]
        return big_out[name][kind]

    return (loss, dx.reshape(B, seq, D), *[leaf(kind, name) for kind in range(4) for name in names])
```

```python
import math

import jax
import jax.numpy as jnp
from jax import lax
from jax.experimental import pallas as pl
from jax.experimental.pallas import tpu as pltpu

F32 = jnp.float32
BF16 = jnp.bfloat16
MESH = pl.DeviceIdType.MESH
N_DEV = 8

VMEM_LIMIT = 56 * 1024 * 1024

D = 1024
FF = 2816
FC = 2816
D_A = 512
D_B = 512
HD = 64
N_KV = 2
Q_PER_KV = 4
BLK = 128
N_GRP = 4
IN_COLS = 3840
PIECES = (("au", 0, 512), ("av", 512, 512), ("q", 1024, 512), ("k", 1536, 128), ("v", 1664, 128),
          ("ga", 1792, 1024), ("gb", 2816, 1024))
EPS = 1e-6
NEG = -1e30
GELU_C = math.sqrt(2.0 / math.pi)

ADAM_LR = 0.001
ADAM_B1 = 0.9
ADAM_B2 = 0.999
ADAM_EPS = 1e-08
ADAM_WD = 0.01
ADAM_STEP = 10

NT = (((1,), (1,)), ((), ()))
TN = (((0,), (0,)), ((), ()))

BIG = (("ffn1_w_gate", True, FF // N_DEV, D), ("ffn1_w_up", True, FF // N_DEV, D),
       ("ffn1_w_down", False, FF // N_DEV, D), ("w_in", True, IN_COLS // N_DEV, D),
       ("w_branch_a", True, D // N_DEV, D_A), ("w_branch_b", True, D // N_DEV, D_B), ("w_out", False, D // N_DEV, D),
       ("ffn2_w_gate", True, FF // N_DEV, D), ("ffn2_w_up", True, FF // N_DEV, D),
       ("ffn2_w_down", False, FF // N_DEV, D))
SMALL = ("b_ada", "g_norm1", "g_norm2", "g_sgu_ln", "b_sgu_ln", "w_spatial", "b_spatial", "g_q", "g_k",
         "attn_sinks", "g_norm3")


def _dot(a, b):
    return jnp.dot(a, b, preferred_element_type=F32)


def _dot_nt(a, b):
    return lax.dot_general(a, b, NT, preferred_element_type=F32)


def _dot_tn(a, b):
    return lax.dot_general(a, b, TN, preferred_element_type=F32)


def _vmem():
    return pl.BlockSpec(memory_space=pltpu.VMEM)


def _any():
    return pl.BlockSpec(memory_space=pl.ANY)


def _whole(a):
    return pl.BlockSpec(a.shape, lambda i: (0,) * len(a.shape))


def _rms_mod(h, g, sh, sc):
    inv = lax.rsqrt(jnp.mean(h * h, axis=-1, keepdims=True) + EPS)
    r = h * inv
    return (r * g) * (1.0 + sc) + sh, r, inv


def _rms_mod_bwd(dxn, r, inv, g, sc):
    dr = dxn * (g * (1.0 + sc))
    dh = inv * (dr - r * jnp.mean(dr * r, axis=-1, keepdims=True))
    return dh, jnp.sum(dxn, axis=0, keepdims=True), jnp.sum(dxn * r, axis=0, keepdims=True)


def _gelu(x):
    t = jnp.tanh(GELU_C * (x + 0.044715 * (x * x * x)))
    return 0.5 * x * (1.0 + t), t


def _gelu_grad(x, t):
    return 0.5 * (1.0 + t) + 0.5 * x * (1.0 - t * t) * (GELU_C * (1.0 + 3.0 * 0.044715 * x * x))


def _adamw(w, g, m, v):
    m = ADAM_B1 * m + (1.0 - ADAM_B1) * g
    v = ADAM_B2 * v + (1.0 - ADAM_B2) * (g * g)
    m_hat = m / (1.0 - ADAM_B1 ** ADAM_STEP)
    v_hat = v / (1.0 - ADAM_B2 ** ADAM_STEP)
    delta = -ADAM_LR * (m_hat / (jnp.sqrt(v_hat) + ADAM_EPS) + ADAM_WD * w)
    return delta, m, v


def _token_tile(seq, cap=512):
    return min(cap, seq)


def _params(*semantics):
    return pltpu.CompilerParams(dimension_semantics=semantics, vmem_limit_bytes=VMEM_LIMIT)


def _place():
    return lax.axis_index("x"), lax.axis_index("y"), lax.axis_index("c")


class _Gather:
    def __init__(self, arrays):
        n = len(arrays)
        self.ins = list(arrays)
        self.out_shape = [jax.ShapeDtypeStruct((N_DEV,) + a.shape, a.dtype) for a in arrays]
        self.scratch = [pltpu.SemaphoreType.DMA((n, 7)), pltpu.SemaphoreType.DMA((n, 7)),
                        pltpu.SemaphoreType.DMA((n,))]

    def _copies(self, ins, outs, sems):
        send_sems, recv_sems, local_sems = sems
        n = len(ins)
        x, y, c = _place()
        me, sibling = (x, y, c), (x, y, 1 - c)
        chips = [(1 - x, y), (x, 1 - y), (1 - x, 1 - y)]

        def slot(a, px, py, pc):
            return outs[a].at[4 * px + 2 * py + pc]

        def copy(a, k, block, to, src=None):
            return pltpu.make_async_remote_copy(
                src_ref=slot(a, *block) if src is None else src, dst_ref=slot(a, *block),
                send_sem=send_sems.at[a, k], recv_sem=recv_sems.at[a, k], device_id=to, device_id_type=MESH)

        mine = [pltpu.make_async_copy(ins[a], slot(a, *me), local_sems.at[a]) for a in range(n)]
        first = [copy(a, 0, me, sibling, src=ins[a]) for a in range(n)]
        first += [copy(a, 1 + j, me, (*chip, c), src=ins[a]) for a in range(n) for j, chip in enumerate(chips)]
        landed = [[copy(a, 1 + j, (*chip, c), me) for a in range(n)] for j, chip in enumerate(chips)]
        passed = [[copy(a, 4 + j, (*chip, c), sibling) for a in range(n)] for j, chip in enumerate(chips)]
        from_sibling = [copy(a, 0, sibling, me) for a in range(n)]
        from_sibling += [copy(a, 4 + j, (*chip, 1 - c), me) for a in range(n) for j, chip in enumerate(chips)]
        return mine, first, landed, passed, from_sibling

    def start(self, ins, outs, sems):
        mine, first, _, _, _ = self._copies(ins, outs, sems)
        for cp in mine + first:
            cp.start()

    def relay(self, ins, outs, sems):
        _, _, landed, passed, _ = self._copies(ins, outs, sems)
        for arrivals, forwards in zip(landed, passed):
            for arrival, forward in zip(arrivals, forwards):
                arrival.wait_recv()
                forward.start()

    def finish(self, ins, outs, sems):
        mine, first, _, passed, from_sibling = self._copies(ins, outs, sems)
        for cp in from_sibling:
            cp.wait_recv()
        for cp in first + [f for fs in passed for f in fs]:
            cp.wait_send()
        for cp in mine:
            cp.wait()


class _GatherRelayed:
    def __init__(self, arrays):
        n = len(arrays)
        self.ins = list(arrays)
        self.out_shape = [jax.ShapeDtypeStruct((N_DEV,) + a.shape, a.dtype) for a in arrays]
        self.scratch = [pltpu.SemaphoreType.DMA((n, 7)), pltpu.SemaphoreType.DMA((n, 7)),
                        pltpu.SemaphoreType.DMA((n,))]

    def _copies(self, ins, outs, sems):
        send_sems, recv_sems, local_sems = sems
        n = len(ins)
        x, y, c = _place()
        me, sibling = (x, y, c), (x, y, 1 - c)
        south = c == 0
        via = (jnp.where(south, 1 - x, x), jnp.where(south, y, 1 - y))
        onward = (jnp.where(south, x, 1 - x), jnp.where(south, 1 - y, y))
        via_sem = jnp.where(south, 1, 2)

        def slot(a, px, py, pc):
            return outs[a].at[4 * px + 2 * py + pc]

        def copy(a, k, block, to, src=None):
            return pltpu.make_async_remote_copy(
                src_ref=slot(a, *block) if src is None else src, dst_ref=slot(a, *block),
                send_sem=send_sems.at[a, k], recv_sem=recv_sems.at[a, k], device_id=to, device_id_type=MESH)

        chips = [(1 - x, y), (x, 1 - y), (1 - x, 1 - y)]
        mine = [pltpu.make_async_copy(ins[a], slot(a, *me), local_sems.at[a]) for a in range(n)]
        first = [copy(a, 0, me, sibling, src=ins[a]) for a in range(n)]
        first += [copy(a, 1 + j, me, (*chips[j], c), src=ins[a]) for a in range(n) for j in range(2)]
        to_pass = [copy(a, via_sem, (*via, c), me) for a in range(n)]
        passed_on = [copy(a, 3, (*via, c), (*onward, c)) for a in range(n)]
        landed = [[copy(a, 1 + j, (*chips[j], c), me) for a in range(n)] for j in range(3)]
        to_sibling = [[copy(a, 4 + j, (*chips[j], c), sibling) for a in range(n)] for j in range(3)]
        from_sibling = [copy(a, 0, sibling, me) for a in range(n)]
        from_sibling += [copy(a, 4 + j, (*chips[j], 1 - c), me) for a in range(n) for j in range(3)]
        return mine, first, to_pass, passed_on, landed, to_sibling, from_sibling

    def start(self, ins, outs, sems):
        mine, first, _, _, _, _, _ = self._copies(ins, outs, sems)
        for cp in mine + first:
            cp.start()

    def relay(self, ins, outs, sems):
        _, _, to_pass, passed_on, landed, to_sibling, _ = self._copies(ins, outs, sems)
        for arrival, onward in zip(to_pass, passed_on):
            arrival.wait_recv()
            onward.start()
        x, y, c = _place()
        for j in range(3):
            for a, (arrival, forward) in enumerate(zip(landed[j], to_sibling[j])):
                if j < 2:
                    pl.when((c == 0) != (j == 0))(arrival.wait_recv)
                else:
                    arrival.wait_recv()
                forward.start()

    def finish(self, ins, outs, sems):
        mine, first, _, passed_on, _, to_sibling, from_sibling = self._copies(ins, outs, sems)
        for cp in from_sibling:
            cp.wait_recv()
        for cp in first + passed_on + [f for fs in to_sibling for f in fs]:
            cp.wait_send()
        for cp in mine:
            cp.wait()


class _GatherDirect:
    def __init__(self, arrays):
        n = len(arrays)
        self.ins = list(arrays)
        self.out_shape = [jax.ShapeDtypeStruct((N_DEV,) + a.shape, a.dtype) for a in arrays]
        self.scratch = [pltpu.SemaphoreType.DMA((n, 7)), pltpu.SemaphoreType.DMA((n, 7)),
                        pltpu.SemaphoreType.DMA((n,))]

    def _copies(self, ins, outs, sems):
        send_sems, recv_sems, local_sems = sems
        n = len(ins)
        x, y, c = _place()
        peers = [(x ^ (k >> 2 & 1), y ^ (k >> 1 & 1), c ^ (k & 1)) for k in range(1, N_DEV)]

        def slot(a, px, py, pc):
            return outs[a].at[4 * px + 2 * py + pc]

        def copy(a, k, block, to, src=None):
            return pltpu.make_async_remote_copy(
                src_ref=slot(a, *block) if src is None else src, dst_ref=slot(a, *block),
                send_sem=send_sems.at[a, k], recv_sem=recv_sems.at[a, k], device_id=to, device_id_type=MESH)

        mine = [pltpu.make_async_copy(ins[a], slot(a, x, y, c), local_sems.at[a]) for a in range(n)]
        sends = [copy(a, k, (x, y, c), peer, src=ins[a]) for a in range(n) for k, peer in enumerate(peers)]
        arrivals = [copy(a, k, peer, (x, y, c)) for a in range(n) for k, peer in enumerate(peers)]
        return mine, sends, arrivals

    def start(self, ins, outs, sems):
        mine, sends, _ = self._copies(ins, outs, sems)
        for cp in mine + sends:
            cp.start()

    def relay(self, ins, outs, sems):
        pass

    def finish(self, ins, outs, sems):
        mine, sends, arrivals = self._copies(ins, outs, sems)
        for cp in arrivals:
            cp.wait_recv()
        for cp in sends:
            cp.wait_send()
        for cp in mine:
            cp.wait()


class _RsSibling:
    def __init__(self, ps):
        n = len(ps)
        self.ins = list(ps)
        self.out_shape = [jax.ShapeDtypeStruct((4,) + p.shape[1:], p.dtype) for p in ps]
        self.scratch = [pltpu.SemaphoreType.DMA((n, 4)), pltpu.SemaphoreType.DMA((n, 4))]

    def _copies(self, ins, outs, sems):
        send_sems, recv_sems = sems
        x, y, c = _place()
        return [pltpu.make_async_remote_copy(
            src_ref=ins[a].at[2 * q + (1 - c)], dst_ref=outs[a].at[q], send_sem=send_sems.at[a, q],
            recv_sem=recv_sems.at[a, q], device_id=(x, y, 1 - c), device_id_type=MESH)
            for a in range(len(ins)) for q in range(4)]

    def start(self, ins, outs, sems):
        for cp in self._copies(ins, outs, sems):
            cp.start()

    def relay(self, ins, outs, sems):
        pass

    def finish(self, ins, outs, sems):
        for cp in self._copies(ins, outs, sems):
            cp.wait()


class _RsChips:
    def __init__(self, qs):
        n = len(qs)
        self.ins = list(qs)
        self.out_shape = [jax.ShapeDtypeStruct(q.shape, q.dtype) for q in qs]
        self.scratch = [pltpu.SemaphoreType.DMA((n, 3)), pltpu.SemaphoreType.DMA((n, 3)),
                        pltpu.SemaphoreType.DMA((n,))]

    def _copies(self, ins, outs, sems):
        send_sems, recv_sems, local_sems = sems
        n = len(ins)
        x, y, c = _place()
        my_chip = 2 * x + y
        chips = [(1 - x, y), (x, 1 - y), (1 - x, 1 - y)]

        def copy(a, j, src_slot, dst_slot):
            px, py = chips[j]
            return pltpu.make_async_remote_copy(
                src_ref=ins[a].at[src_slot], dst_ref=outs[a].at[dst_slot], send_sem=send_sems.at[a, j],
                recv_sem=recv_sems.at[a, j], device_id=(px, py, c), device_id_type=MESH)

        own = [pltpu.make_async_copy(ins[a].at[my_chip], outs[a].at[my_chip], local_sems.at[a]) for a in range(n)]
        sends = [copy(a, j, 2 * px + py, my_chip) for a in range(n) for j, (px, py) in enumerate(chips)]
        arrivals = [copy(a, j, my_chip, 2 * px + py) for a in range(n) for j, (px, py) in enumerate(chips)]
        return own, sends, arrivals

    def start(self, ins, outs, sems):
        own, sends, _ = self._copies(ins, outs, sems)
        for cp in own + sends:
            cp.start()

    def relay(self, ins, outs, sems):
        pass

    def finish(self, ins, outs, sems):
        own, sends, arrivals = self._copies(ins, outs, sems)
        for cp in arrivals:
            cp.wait_recv()
        for cp in sends:
            cp.wait_send()
        for cp in own:
            cp.wait()


def _split_plans(plans, refs_in, refs_out, refs_scr, phase):
    i = o = s = 0
    for p in plans:
        ni, no, ns = len(p.ins), len(p.out_shape), len(p.scratch)
        getattr(p, phase)(refs_in[i:i + ni], refs_out[o:o + no], refs_scr[s:s + ns])
        i, o, s = i + ni, o + no, s + ns


def _plan_results(plans, res):
    out = []
    for p in plans:
        out.append(list(res[:len(p.out_shape)]))
        res = res[len(p.out_shape):]
    return out


def _exchange(plans, name):
    c_in = [a for p in plans for a in p.ins]
    c_out = [s for p in plans for s in p.out_shape]
    c_scr = [s for p in plans for s in p.scratch]

    def body(*refs):
        cin, cout, cscr = refs[:len(c_in)], refs[len(c_in):len(c_in) + len(c_out)], refs[len(c_in) + len(c_out):]
        for phase in ("start", "relay", "finish"):
            _split_plans(plans, cin, cout, cscr, phase)

    res = pl.pallas_call(body, name=name, in_specs=[_any()] * len(c_in), out_specs=[_any()] * len(c_out),
                         out_shape=c_out, scratch_shapes=c_scr)(*c_in)
    return _plan_results(plans, res)


def _call(body, *, name, grid, in_specs, out_specs, out_shape, args, semantics, scratch_shapes=(), plans=()):
    n_in, n_out, n_scr = len(in_specs), len(out_specs), len(scratch_shapes)
    c_in = [a for p in plans for a in p.ins]
    c_out = [s for p in plans for s in p.out_shape]
    c_scr = [s for p in plans for s in p.scratch]
    n_steps = math.prod(grid)

    def wrapped(*refs):
        ins, refs = refs[:n_in], refs[n_in:]
        cin, refs = refs[:len(c_in)], refs[len(c_in):]
        outs, refs = refs[:n_out], refs[n_out:]
        cout, refs = refs[:len(c_out)], refs[len(c_out):]
        scr, cscr = refs[:n_scr], refs[n_scr:]
        if plans:
            step = 0
            for d, g in enumerate(grid):
                step = step * g + pl.program_id(d)
            pl.when(step == 0)(lambda: _split_plans(plans, cin, cout, cscr, "start"))
        body(*ins, *outs, *scr)
        if plans:
            pl.when(step == max(n_steps - 2, 0))(lambda: _split_plans(plans, cin, cout, cscr, "relay"))
            pl.when(step == n_steps - 1)(lambda: _split_plans(plans, cin, cout, cscr, "finish"))

    res = pl.pallas_call(
        wrapped, name=name, grid=grid, in_specs=list(in_specs) + [_any()] * len(c_in),
        out_specs=list(out_specs) + [_any()] * len(c_out), out_shape=list(out_shape) + c_out,
        scratch_shapes=list(scratch_shapes) + c_scr,
        compiler_params=_params(*(("arbitrary",) * len(grid) if plans else semantics)),
    )(*args, *c_in)
    return list(res[:n_out]), _plan_results(plans, res[n_out:])


def _ffn_fwd_loss(h, mods, gn, wg, wu, wd, row0, seq, tgt):
    T = h.shape[0]
    tm = _token_tile(seq, 256)
    tps = seq // tm
    n_t = T // tm

    def body(h_ref, m_ref, g_ref, wg_ref, wu_ref, wd_ref, tgt_ref, out_ref, y_ref, gb_ref, ub_ref, loss_ref):
        hv = h_ref[...]
        sh = m_ref[0, row0:row0 + 1, :]
        sc = m_ref[0, row0 + 1:row0 + 2, :]
        ga = m_ref[0, row0 + 2:row0 + 3, :]
        xn, _, _ = _rms_mod(hv, g_ref[...], sh, sc)
        xb = xn.astype(BF16)
        acc = jnp.zeros((tm, D), F32)
        for c0 in range(0, FF, FC):
            gg = _dot_nt(xb, wg_ref[c0:c0 + FC, :])
            uu = _dot_nt(xb, wu_ref[c0:c0 + FC, :])
            gg, uu = gg.astype(BF16), uu.astype(BF16)
            gb_ref[:, c0:c0 + FC] = gg
            ub_ref[:, c0:c0 + FC] = uu
            acc = acc + _dot((gg * jax.nn.sigmoid(gg)) * uu, wd_ref[c0:c0 + FC, :])
        y_ref[...] = acc
        d = hv + (0.5 * ga) * acc - tgt_ref[...]
        out_ref[...] = d * (1.0 / D)
        loss_ref[...] = jnp.full((1, 8, 128), 0.5 / D, F32) * jnp.sum(d * d)

    tok = pl.BlockSpec((tm, D), lambda i: (i, 0))
    tokf = pl.BlockSpec((tm, FF), lambda i: (i, 0))
    outs, _ = _call(
        body, name="ffn_fwd_loss", grid=(n_t,),
        in_specs=[tok, pl.BlockSpec((1, 9, D), lambda i: (i // tps, 0, 0)), pl.BlockSpec((1, D), lambda i: (0, 0)),
                  _vmem(), _vmem(), _vmem(), tok],
        out_specs=[tok, tok, tokf, tokf, pl.BlockSpec((1, 8, 128), lambda i: (i, 0, 0))],
        out_shape=[jax.ShapeDtypeStruct((T, D), F32), jax.ShapeDtypeStruct((T, D), F32),
                   jax.ShapeDtypeStruct((T, FF), BF16), jax.ShapeDtypeStruct((T, FF), BF16),
                   jax.ShapeDtypeStruct((n_t, 8, 128), F32)],
        args=(h, mods, gn, wg, wu, wd, tgt), semantics=("parallel",))
    return outs


def _ffn_gate_up(h, mods, gn, wg, wu, row0, seq, plans=()):
    T = h.shape[0]
    tm = _token_tile(seq, 512)
    tps = seq // tm

    def body(h_ref, m_ref, g_ref, wg_ref, wu_ref, gb_ref, ub_ref):
        xn, _, _ = _rms_mod(h_ref[...], g_ref[...], m_ref[0, row0:row0 + 1, :], m_ref[0, row0 + 1:row0 + 2, :])
        xb = xn.astype(BF16)
        for c0 in range(0, FF, FC):
            gb_ref[:, c0:c0 + FC] = _dot_nt(xb, wg_ref[c0:c0 + FC, :]).astype(BF16)
            ub_ref[:, c0:c0 + FC] = _dot_nt(xb, wu_ref[c0:c0 + FC, :]).astype(BF16)

    tokf = pl.BlockSpec((tm, FF), lambda i: (i, 0))
    return _call(
        body, name="ffn_gate_up", grid=(T // tm,),
        in_specs=[pl.BlockSpec((tm, D), lambda i: (i, 0)), pl.BlockSpec((1, 9, D), lambda i: (i // tps, 0, 0)),
                  pl.BlockSpec((1, D), lambda i: (0, 0)), _vmem(), _vmem()],
        out_specs=[tokf, tokf], out_shape=[jax.ShapeDtypeStruct((T, FF), BF16)] * 2, args=(h, mods, gn, wg, wu),
        semantics=("parallel",), plans=plans)


def _ffn_down(h, gb, ub, mods, wd, row0, seq, plans=()):
    T = h.shape[0]
    tm = _token_tile(seq, 512)
    tps = seq // tm

    def body(h_ref, gb_ref, ub_ref, m_ref, wd_ref, out_ref, y_ref):
        acc = jnp.zeros((tm, D), F32)
        for c0 in range(0, FF, FC):
            gg = gb_ref[:, c0:c0 + FC]
            acc = acc + _dot((gg * jax.nn.sigmoid(gg)) * ub_ref[:, c0:c0 + FC], wd_ref[c0:c0 + FC, :])
        y_ref[...] = acc
        out_ref[...] = h_ref[...] + (0.5 * m_ref[0, row0 + 2:row0 + 3, :]) * acc

    tok = pl.BlockSpec((tm, D), lambda i: (i, 0))
    tokf = pl.BlockSpec((tm, FF), lambda i: (i, 0))
    return _call(
        body, name="ffn_down", grid=(T // tm,),
        in_specs=[tok, tokf, tokf, pl.BlockSpec((1, 9, D), lambda i: (i // tps, 0, 0)), _vmem()],
        out_specs=[tok, tok], out_shape=[jax.ShapeDtypeStruct((T, D), F32)] * 2, args=(h, gb, ub, mods, wd),
        semantics=("parallel",), plans=plans)


def _ffn_bwd(h, dhn, y, gb, ub, mods, gn, wg, wu, wd, row0, seq, plans=()):
    T = h.shape[0]
    B = T // seq
    tm = _token_tile(seq, 256)
    tps = seq // tm
    n_t = T // tm

    def body(h_ref, dhn_ref, y_ref, gb_ref, ub_ref, m_ref, g_ref, wg_ref, wu_ref, wd_ref,
             dh_ref, xb_ref, dyb_ref, a_ref, dg_ref, du_ref, mg_ref):
        i = pl.program_id(0)
        hv = h_ref[...]
        dhn = dhn_ref[...]
        sh = m_ref[0, row0:row0 + 1, :]
        sc = m_ref[0, row0 + 1:row0 + 2, :]
        ga = m_ref[0, row0 + 2:row0 + 3, :]
        g = g_ref[...]
        xn, r, inv = _rms_mod(hv, g, sh, sc)
        xb_ref[...] = xn.astype(BF16)
        dyb = ((0.5 * ga) * dhn).astype(BF16)
        dyb_ref[...] = dyb
        dga = 0.5 * jnp.sum(dhn * y_ref[...], axis=0, keepdims=True)
        dxn = jnp.zeros((tm, D), F32)
        for c0 in range(0, FF, FC):
            wgc = wg_ref[c0:c0 + FC, :]
            wuc = wu_ref[c0:c0 + FC, :]
            gg = gb_ref[:, c0:c0 + FC]
            uu = ub_ref[:, c0:c0 + FC]
            sig = jax.nn.sigmoid(gg)
            s = gg * sig
            a_ref[:, c0:c0 + FC] = s * uu
            da = _dot_nt(dyb, wd_ref[c0:c0 + FC, :]).astype(BF16)
            dub = da * s
            dgb = (da * uu) * (sig * (1.0 + gg * (1.0 - sig)))
            dg_ref[:, c0:c0 + FC] = dgb
            du_ref[:, c0:c0 + FC] = dub
            dxn = dxn + _dot(dgb, wgc) + _dot(dub, wuc)
        dh, s_dxn, s_dxr = _rms_mod_bwd(dxn, r, inv, g, sc)
        dh_ref[...] = dhn + dh

        @pl.when(i % tps == 0)
        def _():
            mg_ref[...] = jnp.zeros(mg_ref.shape, F32)

        mg_ref[0, 0:1, :] += s_dxn
        mg_ref[0, 1:2, :] += s_dxr
        mg_ref[0, 2:3, :] += dga

    tok = pl.BlockSpec((tm, D), lambda i: (i, 0))
    tokf = pl.BlockSpec((tm, FF), lambda i: (i, 0))
    return _call(
        body, name="ffn_bwd", grid=(n_t,),
        in_specs=[tok, tok, tok, tokf, tokf, pl.BlockSpec((1, 9, D), lambda i: (i // tps, 0, 0)),
                  pl.BlockSpec((1, D), lambda i: (0, 0)), _vmem(), _vmem(), _vmem()],
        out_specs=[tok, tok, tok, tokf, tokf, tokf, pl.BlockSpec((1, 8, D), lambda i: (i // tps, 0, 0))],
        out_shape=[jax.ShapeDtypeStruct((T, D), F32), jax.ShapeDtypeStruct((T, D), BF16),
                   jax.ShapeDtypeStruct((T, D), BF16), jax.ShapeDtypeStruct((T, FF), BF16),
                   jax.ShapeDtypeStruct((T, FF), BF16), jax.ShapeDtypeStruct((T, FF), BF16),
                   jax.ShapeDtypeStruct((B, 8, D), F32)],
        args=(h, dhn, y, gb, ub, mods, gn, wg, wu, wd), semantics=("arbitrary",), plans=plans)


def _wgrad(a, b, name, plans=()):
    T, da = a.shape
    db = b.shape[1]
    bm = {2816: 1408, 3840: 1280}[da]
    bn = db
    tk = min(2048, T)
    nk = T // tk

    def body(a_ref, b_ref, o_ref):
        @pl.when(pl.program_id(2) == 0)
        def _():
            o_ref[...] = jnp.zeros(o_ref.shape, F32)

        o_ref[...] += _dot_tn(a_ref[...], b_ref[...])

    (out,), plan_outs = _call(
        body, name=name, grid=(da // bm, db // bn, nk),
        in_specs=[pl.BlockSpec((tk, bm), lambda i, j, k: (k, i)), pl.BlockSpec((tk, bn), lambda i, j, k: (k, j))],
        out_specs=[pl.BlockSpec((bm, bn), lambda i, j, k: (i, j))], out_shape=[jax.ShapeDtypeStruct((da, db), F32)],
        args=(a, b), semantics=("parallel", "parallel", "arbitrary"), plans=plans)
    return out, plan_outs


def _inproj_fwd(h, mods, gn, w_in, seq, plans=()):
    T = h.shape[0]
    tm = _token_tile(seq)
    tps = seq // tm

    def body(h_ref, m_ref, g_ref, w_ref, *outs):
        xn, _, _ = _rms_mod(h_ref[...], g_ref[...], m_ref[0, 3:4, :], m_ref[0, 4:5, :])
        xb = xn.astype(BF16)
        for (_, c0, w), o_ref in zip(PIECES, outs):
            o_ref[...] = _dot_nt(xb, w_ref[c0:c0 + w, :])

    return _call(
        body, name="inproj_fwd", grid=(T // tm,),
        in_specs=[pl.BlockSpec((tm, D), lambda i: (i, 0)), pl.BlockSpec((1, 9, D), lambda i: (i // tps, 0, 0)),
                  pl.BlockSpec((1, D), lambda i: (0, 0)), _vmem()],
        out_specs=[pl.BlockSpec((tm, w), lambda i: (i, 0)) for _, _, w in PIECES],
        out_shape=[jax.ShapeDtypeStruct((T, w), F32) for _, _, w in PIECES], args=(h, mods, gn, w_in),
        semantics=("parallel",), plans=plans)


def _inproj_bwd(h, dh_res, dpieces, mods, gn, w_in, seq, plans=()):
    T = h.shape[0]
    B = T // seq
    tm = _token_tile(seq, 512)
    tps = seq // tm

    def body(h_ref, dres_ref, *rest):
        dp_refs = rest[:len(PIECES)]
        m_ref, g_ref, w_ref, dh_ref, xb_ref, dpb_ref, mg_ref = rest[len(PIECES):]
        i = pl.program_id(0)
        g = g_ref[...]
        sc = m_ref[0, 4:5, :]
        xn, r, inv = _rms_mod(h_ref[...], g, m_ref[0, 3:4, :], sc)
        xb_ref[...] = xn.astype(BF16)
        dxn = jnp.zeros((tm, D), F32)
        for (_, c0, w), dp_ref in zip(PIECES, dp_refs):
            dpb = dp_ref[...].astype(BF16)
            dpb_ref[:, c0:c0 + w] = dpb
            dxn = dxn + _dot(dpb, w_ref[c0:c0 + w, :])
        dh, s_dxn, s_dxr = _rms_mod_bwd(dxn, r, inv, g, sc)
        dh_ref[...] = dres_ref[...] + dh

        @pl.when(i % tps == 0)
        def _():
            mg_ref[...] = jnp.zeros(mg_ref.shape, F32)

        mg_ref[0, 0:1, :] += s_dxn
        mg_ref[0, 1:2, :] += s_dxr

    tok = pl.BlockSpec((tm, D), lambda i: (i, 0))
    return _call(
        body, name="inproj_bwd", grid=(T // tm,),
        in_specs=[tok, tok] + [pl.BlockSpec((tm, w), lambda i: (i, 0)) for _, _, w in PIECES]
        + [pl.BlockSpec((1, 9, D), lambda i: (i // tps, 0, 0)), pl.BlockSpec((1, D), lambda i: (0, 0)), _vmem()],
        out_specs=[tok, tok, pl.BlockSpec((tm, IN_COLS), lambda i: (i, 0)),
                   pl.BlockSpec((1, 8, D), lambda i: (i // tps, 0, 0))],
        out_shape=[jax.ShapeDtypeStruct((T, D), F32), jax.ShapeDtypeStruct((T, D), BF16),
                   jax.ShapeDtypeStruct((T, IN_COLS), BF16), jax.ShapeDtypeStruct((B, 8, D), F32)],
        args=(h, dh_res, *dpieces, mods, gn, w_in), semantics=("arbitrary",), plans=plans)


def _seg_mean(x):
    i = lax.broadcasted_iota(jnp.int32, (128, 128), 0) >> 6
    j = lax.broadcasted_iota(jnp.int32, (128, 128), 1) >> 6
    ones = jnp.where(i == j, 1.0 / HD, 0.0).astype(BF16)
    hi = x.astype(BF16)
    lo = (x - hi.astype(F32)).astype(BF16)
    return _dot(hi, ones) + _dot(lo, ones)


def _head_norm(x, g2):
    inv = lax.rsqrt(_seg_mean(x * x) + EPS)
    y = x * inv
    return y * g2, y, inv


def _head_norm_bwd(dxn, y, inv, g2):
    dy = dxn * g2
    return inv * (dy - y * _seg_mean(dy * y)), jnp.sum(dxn * y, axis=0, keepdims=True)


def _swa_block(q, kk, vv, gq2, gk2, sinks, first, do=None):
    lo = lax.broadcasted_iota(jnp.int32, (1, 128), 1) < HD
    kn, ky, kinv = _head_norm(kk, gk2)

    def operands(x):
        xr = pltpu.roll(x, HD, 1)
        own_lo, own_hi = jnp.where(lo, x, 0.0).astype(BF16), jnp.where(lo, 0.0, x).astype(BF16)
        rolled_lo, rolled_hi = jnp.where(lo, xr, 0.0).astype(BF16), jnp.where(lo, 0.0, xr).astype(BF16)
        return (own_lo, rolled_hi), (rolled_lo, own_hi)

    def restore(parts):
        (own_lo, rolled_hi), (rolled_lo, own_hi) = parts
        return (jnp.where(lo, own_lo, own_hi)
                + pltpu.roll(jnp.where(lo, rolled_lo, rolled_hi), HD, 1))

    k_ops, v_ops = operands(kn), operands(vv)
    k2 = [jnp.concatenate(pair, axis=0) for pair in k_ops]
    def stack(x):
        return jnp.concatenate([x[:, 128 * p:128 * (p + 1)] for p in range(4)], axis=0)

    def unstack(x):
        return jnp.concatenate([x[BLK * p:BLK * (p + 1)] for p in range(4)], axis=1)

    def of_head(x, kh):
        return x[2 * BLK * kh:2 * BLK * (kh + 1)]

    nq = 4 * BLK
    pair = lax.broadcasted_iota(jnp.int32, (nq, 1), 0) >> 7
    row = lax.broadcasted_iota(jnp.int32, (nq, 2 * BLK), 0) & (BLK - 1)
    col = lax.broadcasted_iota(jnp.int32, (nq, 2 * BLK), 1)
    valid = (col <= row + BLK) & (col > row) & (col >= jnp.where(first, BLK, 0))
    scale = HD ** -0.5
    qn, qy, qinv = _head_norm(stack(q), gq2)
    qnb = qn.astype(BF16)
    s2 = jnp.concatenate([_dot_nt(of_head(qnb, kh), k2[kh]) for kh in range(N_KV)], axis=0) * scale
    probs, p_sink = [], []
    for j in range(2):
        s = jnp.where(valid, s2[:, 2 * BLK * j:2 * BLK * (j + 1)], NEG)
        sink = jnp.zeros((nq, 1), F32)
        for p in range(4):
            sink = jnp.where(pair == p, sinks[:, 2 * p + j:2 * p + j + 1], sink)
        m = jnp.maximum(jnp.max(s, axis=-1, keepdims=True), sink)
        e = jnp.exp(s - m)
        e_sink = jnp.exp(sink - m)
        rden = 1.0 / (jnp.sum(e, axis=-1, keepdims=True) + e_sink)
        probs.append(e * rden)
        p_sink.append(e_sink * rden)
    pb = [p.astype(BF16) for p in probs]
    if do is None:
        return unstack(jnp.concatenate(
            [_dot(of_head(pb[0], kh), v_ops[kh][0]) + _dot(of_head(pb[1], kh), v_ops[kh][1]) for kh in range(N_KV)],
            axis=0))
    dob = stack(do).astype(BF16)
    ds, dsinks = [], [None] * 8
    for j in range(2):
        dp = jnp.concatenate([_dot_nt(of_head(dob, kh), v_ops[kh][j]) for kh in range(N_KV)], axis=0)
        t = jnp.sum(dp * probs[j], axis=-1, keepdims=True)
        ds.append(probs[j] * (dp - t) * scale)
        lost = p_sink[j] * t
        for p in range(4):
            dsinks[2 * p + j] = -jnp.sum(lost[BLK * p:BLK * (p + 1)])
    dsb = jnp.concatenate(ds, axis=1).astype(BF16)
    dqn = jnp.concatenate([_dot(of_head(dsb, kh), k2[kh]) for kh in range(N_KV)], axis=0)
    dq, dgq2 = _head_norm_bwd(dqn, qy, qinv, gq2)
    dk2 = [_dot_tn(of_head(dsb, kh), of_head(qnb, kh)) for kh in range(N_KV)]
    dv_ops = [[_dot_tn(of_head(pb[j], kh), of_head(dob, kh)) for j in range(2)] for kh in range(N_KV)]
    dkn = restore(tuple((d[:2 * BLK], d[2 * BLK:]) for d in dk2))
    dvv = restore(tuple(tuple(d) for d in dv_ops))
    dkk, dgk2 = _head_norm_bwd(dkn, ky, kinv, gk2)
    return unstack(dq), dkk, dvv, dgq2, dgk2, dsinks


SWA_GROUP = 8


def _swa_specs(nbs):
    grp = min(SWA_GROUP, nbs)
    rows = grp * BLK

    def tok(w):
        return pl.BlockSpec((rows, w), lambda i: (i, 0))

    halo = pl.BlockSpec((BLK, 128), lambda i: (jnp.maximum(i * grp - 1, 0), 0))
    vec = pl.BlockSpec((1, 128), lambda i: (0, 0))
    sk = pl.BlockSpec((1, 8), lambda i: (0, 0))
    return grp, tok, halo, vec, sk


def _swa_fwd(q, k, v, gq2, gk2, sinks, nbs, plans=()):
    T = q.shape[0]
    grp, tok, halo, vec, sk = _swa_specs(nbs)

    def body(q_ref, kh_ref, kc_ref, vh_ref, vc_ref, gq_ref, gk_ref, sk_ref, o_ref):
        seq_start = ((pl.program_id(0) * grp) % nbs) == 0
        for g in range(grp):
            rows = slice(g * BLK, (g + 1) * BLK)
            prev = slice((g - 1) * BLK, g * BLK)
            kk = jnp.concatenate([kh_ref[...] if g == 0 else kc_ref[prev, :], kc_ref[rows, :]], axis=0)
            vv = jnp.concatenate([vh_ref[...] if g == 0 else vc_ref[prev, :], vc_ref[rows, :]], axis=0)
            o_ref[rows, :] = _swa_block(q_ref[rows, :], kk, vv, gq_ref[...], gk_ref[...], sk_ref[...],
                                        seq_start if g == 0 else False)

    (out,), plan_outs = _call(
        body, name="swa_fwd", grid=(T // (grp * BLK),),
        in_specs=[tok(D_B), halo, tok(128), halo, tok(128), vec, vec, sk], out_specs=[tok(D_B)],
        out_shape=[jax.ShapeDtypeStruct((T, D_B), F32)], args=(q, k, k, v, v, gq2, gk2, sinks),
        semantics=("parallel",), plans=plans)
    return out, plan_outs


def _swa_bwd(q, k, v, gq2, gk2, sinks, do, nbs, plans=()):
    T = q.shape[0]
    grp, tok, halo, vec, sk = _swa_specs(nbs)
    steps = T // (grp * BLK)

    def body(q_ref, kh_ref, kc_ref, vh_ref, vc_ref, gq_ref, gk_ref, sk_ref, do_ref,
             dq_ref, dk_ref, dv_ref, dkh_ref, dvh_ref, dgq_ref, dgk_ref, dsk_ref):
        i = pl.program_id(0)
        seq_start = ((i * grp) % nbs) == 0

        @pl.when(i == 0)
        def _():
            for r in (dgq_ref, dgk_ref, dsk_ref):
                r[...] = jnp.zeros(r.shape, F32)

        res = []
        for g in range(grp):
            rows = slice(g * BLK, (g + 1) * BLK)
            prev = slice((g - 1) * BLK, g * BLK)
            kk = jnp.concatenate([kh_ref[...] if g == 0 else kc_ref[prev, :], kc_ref[rows, :]], axis=0)
            vv = jnp.concatenate([vh_ref[...] if g == 0 else vc_ref[prev, :], vc_ref[rows, :]], axis=0)
            res.append(_swa_block(q_ref[rows, :], kk, vv, gq_ref[...], gk_ref[...], sk_ref[...],
                                  seq_start if g == 0 else False, do=do_ref[rows, :]))
        lane = lax.broadcasted_iota(jnp.int32, (8, 128), 1)
        upd = jnp.zeros((8, 128), F32)
        for g, (dq, dkk, dvv, dgq2, dgk2, dsinks) in enumerate(res):
            rows = slice(g * BLK, (g + 1) * BLK)
            dq_ref[rows, :] = dq
            dk_ref[rows, :] = dkk[BLK:] + res[g + 1][1][:BLK] if g + 1 < grp else dkk[BLK:]
            dv_ref[rows, :] = dvv[BLK:] + res[g + 1][2][:BLK] if g + 1 < grp else dvv[BLK:]
            dgq_ref[...] += dgq2
            dgk_ref[...] += dgk2
            for h, d in enumerate(dsinks):
                upd = upd + jnp.where(lane == h, d, 0.0)
        dkh_ref[0] = res[0][1][:BLK]
        dvh_ref[0] = res[0][2][:BLK]
        dsk_ref[...] += upd

    one = pl.BlockSpec((1, BLK, 128), lambda i: (i, 0, 0))
    halo_shape = jax.ShapeDtypeStruct((steps, BLK, 128), F32)
    return _call(
        body, name="swa_bwd", grid=(steps,),
        in_specs=[tok(D_B), halo, tok(128), halo, tok(128), vec, vec, sk, tok(D_B)],
        out_specs=[tok(D_B), tok(128), tok(128), one, one, vec, vec, pl.BlockSpec((8, 128), lambda i: (0, 0))],
        out_shape=[jax.ShapeDtypeStruct((T, D_B), F32), jax.ShapeDtypeStruct((T, 128), F32),
                   jax.ShapeDtypeStruct((T, 128), F32), halo_shape, halo_shape, jax.ShapeDtypeStruct((1, 128), F32),
                   jax.ShapeDtypeStruct((1, 128), F32), jax.ShapeDtypeStruct((8, 128), F32)],
        args=(q, k, k, v, v, gq2, gk2, sinks, do), semantics=("arbitrary",), plans=plans)


def _swa_add_halo(dk, dk_halo):
    steps = dk_halo.shape[0]
    nxt = jnp.concatenate([dk_halo[1:], jnp.zeros_like(dk_halo[:1])], axis=0)[:, None]
    dk = dk.reshape(steps, -1, BLK, 128)
    return jnp.concatenate([dk[:, :-1], dk[:, -1:] + nxt], axis=1).reshape(-1, 128)


def _sgu_norm(av, g_ln, b_ln):
    t, th = _gelu(av)
    mu = jnp.mean(t, axis=-1, keepdims=True)
    tc = t - mu
    rstd = lax.rsqrt(jnp.mean(tc * tc, axis=-1, keepdims=True) + EPS)
    vhat = tc * rstd
    return vhat * g_ln + b_ln, vhat, rstd, th


def _masked_ws(ws_ref):
    tril = lax.broadcasted_iota(jnp.int32, (BLK, BLK), 0) >= lax.broadcasted_iota(jnp.int32, (BLK, BLK), 1)
    return [jnp.where(tril, ws_ref[g], 0.0).astype(BF16) for g in range(N_GRP)]


def _mix_fwd(au, av, gta, gtb, ob, h, mods, g_ln, b_ln, ws, bsb, wa, wb, wout, seq, plans=()):
    T = h.shape[0]
    tm = _token_tile(seq)
    tps = seq // tm

    def body(au_ref, av_ref, gta_ref, gtb_ref, ob_ref, h_ref, m_ref, gl_ref, bl_ref, ws_ref, bs_ref,
             wa_ref, wb_ref, wo_ref, out_ref, vvb_s, z_s):
        u, _ = _gelu(au_ref[...])
        vv, _, _, _ = _sgu_norm(av_ref[...], gl_ref[...], bl_ref[...])
        vvb_s[...] = vv.astype(BF16)
        wsm = _masked_ws(ws_ref)
        for c in range(tm // BLK):
            rows = slice(c * BLK, (c + 1) * BLK)
            for g in range(N_GRP):
                cols = slice(g * BLK, (g + 1) * BLK)
                z_s[rows, cols] = _dot(wsm[g], vvb_s[rows, cols]) + bs_ref[g]
        ya = _dot_nt((u * z_s[...]).astype(BF16), wa_ref[...])
        yb = _dot_nt(ob_ref[...].astype(BF16), wb_ref[...])
        merged = jax.nn.sigmoid(gta_ref[...]) * ya + jax.nn.sigmoid(gtb_ref[...]) * yb
        out_ref[...] = h_ref[...] + m_ref[0, 5:6, :] * _dot(merged.astype(BF16), wo_ref[...])

    def tok(w):
        return pl.BlockSpec((tm, w), lambda i: (i, 0))

    def full(shape):
        return pl.BlockSpec(shape, lambda i: (0,) * len(shape))

    (out,), plan_outs = _call(
        body, name="mix_fwd", grid=(T // tm,),
        in_specs=[tok(D_A), tok(D_A), tok(D), tok(D), tok(D_B), tok(D),
                  pl.BlockSpec((1, 9, D), lambda i: (i // tps, 0, 0)), full((1, D_A)), full((1, D_A)),
                  full((N_GRP, BLK, BLK)), full((N_GRP, BLK, BLK)), _vmem(), _vmem(), _vmem()],
        out_specs=[tok(D)], out_shape=[jax.ShapeDtypeStruct((T, D), F32)],
        scratch_shapes=[pltpu.VMEM((tm, D_A), BF16), pltpu.VMEM((tm, D_A), F32)],
        args=(au, av, gta, gtb, ob, h, mods, g_ln, b_ln, ws, bsb, wa, wb, wout), semantics=("parallel",),
        plans=plans)
    return out, plan_outs


def _mix_bwd(au, av, gta, gtb, ob, dh, mods, g_ln, b_ln, ws, bsb, wa, wb, wout, seq, plans=()):
    T = dh.shape[0]
    B = T // seq
    tm = _token_tile(seq, 256)
    tps = seq // tm

    def body(au_ref, av_ref, gta_ref, gtb_ref, ob_ref, dh_ref, m_ref, gl_ref, bl_ref, ws_ref, bs_ref,
             wa_ref, wb_ref, wo_ref,
             dau_ref, dav_ref, dgta_ref, dgtb_ref, dob_ref, dwo_ref, dwa_ref, dwb_ref, dws_ref, dbs_ref, dln_ref,
             mg_ref, vvb_s, z_s, dz_s, dzb_s, dvv_s):
        i = pl.program_id(0)

        @pl.when(i == 0)
        def _():
            for r in (dwo_ref, dwa_ref, dwb_ref, dws_ref, dbs_ref, dln_ref):
                r[...] = jnp.zeros(r.shape, F32)

        @pl.when(i % tps == 0)
        def _():
            mg_ref[...] = jnp.zeros(mg_ref.shape, F32)

        auv = au_ref[...]
        avv = av_ref[...]
        u, thu = _gelu(auv)
        g_ln = gl_ref[...]
        vv, vhat, rstd, thv = _sgu_norm(avv, g_ln, bl_ref[...])
        vvb_s[...] = vv.astype(BF16)
        wsm = _masked_ws(ws_ref)
        for c in range(tm // BLK):
            rows = slice(c * BLK, (c + 1) * BLK)
            for g in range(N_GRP):
                cols = slice(g * BLK, (g + 1) * BLK)
                z_s[rows, cols] = _dot(wsm[g], vvb_s[rows, cols]) + bs_ref[g]
        z = z_s[...]
        yab = (u * z).astype(BF16)
        obb = ob_ref[...].astype(BF16)
        ya = _dot_nt(yab, wa_ref[...])
        yb = _dot_nt(obb, wb_ref[...])
        sa = jax.nn.sigmoid(gta_ref[...])
        sb = jax.nn.sigmoid(gtb_ref[...])
        mb = (sa * ya + sb * yb).astype(BF16)
        dhv = dh_ref[...]
        mg_ref[0, 0:1, :] += jnp.sum(dhv * _dot(mb, wo_ref[...]), axis=0, keepdims=True)
        dmob = (m_ref[0, 5:6, :] * dhv).astype(BF16)
        dwo_ref[...] += _dot_tn(mb, dmob)
        dmerged = _dot_nt(dmob, wo_ref[...])
        dya = dmerged * sa
        dyb = dmerged * sb
        dgta_ref[...] = dya * ya * (1.0 - sa)
        dgtb_ref[...] = dyb * yb * (1.0 - sb)
        dyab = dya.astype(BF16)
        dybb = dyb.astype(BF16)
        dwa_ref[...] += _dot_tn(dyab, yab)
        dwb_ref[...] += _dot_tn(dybb, obb)
        dob_ref[...] = _dot(dybb, wb_ref[...])
        dyap = _dot(dyab, wa_ref[...])
        dau_ref[...] = (dyap * z) * _gelu_grad(auv, thu)
        dz = dyap * u
        dz_s[...] = dz
        dzb_s[...] = dz.astype(BF16)
        for c in range(tm // BLK):
            rows = slice(c * BLK, (c + 1) * BLK)
            for g in range(N_GRP):
                cols = slice(g * BLK, (g + 1) * BLK)
                dzb = dzb_s[rows, cols]
                dvv_s[rows, cols] = _dot_tn(wsm[g], dzb)
                dws_ref[g] += _dot_nt(dzb, vvb_s[rows, cols])
                dbs_ref[g] += dz_s[rows, cols]
        dvv = dvv_s[...]
        dln_ref[0:1, :] += jnp.sum(dvv * vhat, axis=0, keepdims=True)
        dln_ref[1:2, :] += jnp.sum(dvv, axis=0, keepdims=True)
        dvh = dvv * g_ln
        dt = rstd * (dvh - jnp.mean(dvh, axis=-1, keepdims=True)
                     - vhat * jnp.mean(dvh * vhat, axis=-1, keepdims=True))
        dav_ref[...] = dt * _gelu_grad(avv, thv)

    def tok(w):
        return pl.BlockSpec((tm, w), lambda i: (i, 0))

    def full(shape):
        return pl.BlockSpec(shape, lambda i: (0,) * len(shape))

    return _call(
        body, name="mix_bwd", grid=(T // tm,),
        in_specs=[tok(D_A), tok(D_A), tok(D), tok(D), tok(D_B), tok(D),
                  pl.BlockSpec((1, 9, D), lambda i: (i // tps, 0, 0)), full((1, D_A)), full((1, D_A)),
                  full((N_GRP, BLK, BLK)), full((N_GRP, BLK, BLK)), _vmem(), _vmem(), _vmem()],
        out_specs=[tok(D_A), tok(D_A), tok(D), tok(D), tok(D_B), full((D, D)), full((D, D_A)), full((D, D_B)),
                   full((N_GRP, BLK, BLK)), full((N_GRP, BLK, BLK)), full((8, D_A)),
                   pl.BlockSpec((1, 8, D), lambda i: (i // tps, 0, 0))],
        out_shape=[jax.ShapeDtypeStruct((T, D_A), F32), jax.ShapeDtypeStruct((T, D_A), F32),
                   jax.ShapeDtypeStruct((T, D), F32), jax.ShapeDtypeStruct((T, D), F32),
                   jax.ShapeDtypeStruct((T, D_B), F32), jax.ShapeDtypeStruct((D, D), F32),
                   jax.ShapeDtypeStruct((D, D_A), F32), jax.ShapeDtypeStruct((D, D_B), F32),
                   jax.ShapeDtypeStruct((N_GRP, BLK, BLK), F32), jax.ShapeDtypeStruct((N_GRP, BLK, BLK), F32),
                   jax.ShapeDtypeStruct((8, D_A), F32), jax.ShapeDtypeStruct((B, 8, D), F32)],
        scratch_shapes=[pltpu.VMEM((tm, D_A), BF16), pltpu.VMEM((tm, D_A), F32), pltpu.VMEM((tm, D_A), F32),
                        pltpu.VMEM((tm, D_A), BF16), pltpu.VMEM((tm, D_A), F32)],
        args=(au, av, gta, gtb, ob, dh, mods, g_ln, b_ln, ws, bsb, wa, wb, wout), semantics=("arbitrary",),
        plans=plans)


def _prologue(c_pad, w_ada, first_shards):
    cols = w_ada.shape[1]
    plan_w, plan_c = _GatherRelayed(first_shards), _GatherDirect([c_pad])
    plan_m = _Gather([jax.ShapeDtypeStruct((N_DEV * 8, cols), F32)])
    n_w = len(first_shards)

    def body(c_ref, wada_ref, *rest):
        w_ins, rest = rest[:n_w], rest[n_w:]
        call_ref, mods_ref = rest[:2]
        w_outs, rest = rest[2:2 + n_w], rest[2 + n_w:]
        cvm, part, local_sem = rest[:3]
        sems = rest[3:]
        sems_w, sems_c, sems_m = sems[:3], sems[3:6], sems[6:9]
        plan_c.start([c_ref], [call_ref], sems_c)
        plan_c.relay([c_ref], [call_ref], sems_c)
        plan_c.finish([c_ref], [call_ref], sems_c)
        load = pltpu.make_async_copy(call_ref, cvm, local_sem)
        load.start()
        load.wait()
        cv = cvm[...].reshape(N_DEV * 8, D)
        part[...] = _dot((cv * jax.nn.sigmoid(cv)).astype(BF16), wada_ref[...].astype(BF16))
        plan_m.start([part], [mods_ref], sems_m)
        plan_w.start(w_ins, w_outs, sems_w)
        plan_m.relay([part], [mods_ref], sems_m)
        plan_m.finish([part], [mods_ref], sems_m)
        plan_w.relay(w_ins, w_outs, sems_w)
        plan_w.finish(w_ins, w_outs, sems_w)

    res = pl.pallas_call(
        body, name="prologue", in_specs=[_any(), _vmem()] + [_any()] * n_w,
        out_specs=[_any()] * (2 + n_w), out_shape=plan_c.out_shape + plan_m.out_shape + plan_w.out_shape,
        scratch_shapes=[pltpu.VMEM((N_DEV, 8, D), F32), pltpu.VMEM((N_DEV * 8, cols), F32), pltpu.SemaphoreType.DMA]
        + plan_w.scratch + plan_c.scratch + plan_m.scratch,
        compiler_params=pltpu.CompilerParams(vmem_limit_bytes=VMEM_LIMIT),
    )(c_pad, w_ada, *first_shards)
    return res[0], res[1], list(res[2:])


def _ada_update(c_all, dm_cols, w, m, v):
    n, cols = c_all.shape[0], w.shape[1]
    br = 256

    def body(c_ref, dm_ref, w_ref, m_ref, v_ref, g_ref, d_ref, nm_ref, nv_ref):
        cv = c_ref[...]
        g = _dot_tn((cv * jax.nn.sigmoid(cv)).astype(BF16), dm_ref[...].astype(BF16))
        g_ref[...] = g
        d_ref[...], nm_ref[...], nv_ref[...] = _adamw(w_ref[...], g, m_ref[...], v_ref[...])

    rows = pl.BlockSpec((br, cols), lambda i: (i, 0))
    outs, _ = _call(
        body, name="ada_update", grid=(D // br,),
        in_specs=[pl.BlockSpec((n, br), lambda i: (0, i)), pl.BlockSpec((n, cols), lambda i: (0, 0)), rows, rows, rows],
        out_specs=[rows] * 4, out_shape=[jax.ShapeDtypeStruct(w.shape, F32)] * 4, args=(c_all, dm_cols, w, m, v),
        semantics=("parallel",))
    return outs


def _mod_finish(mg1, mg2, mg2g, mg3, mods, g1, g2, g3):
    B = mods.shape[0]

    def body(mg1_ref, mg2_ref, mg2g_ref, mg3_ref, m_ref, g1_ref, g2_ref, g3_ref, dm_ref, dgn_ref):
        dgn_ref[...] = jnp.zeros(dgn_ref.shape, F32)
        for k, (mg, g_ref) in enumerate(((mg1_ref, g1_ref), (mg2_ref, g2_ref), (mg3_ref, g3_ref))):
            for b in range(B):
                s_dxr = mg[b, 1:2, :]
                dm_ref[b, 3 * k:3 * k + 1, :] = mg[b, 0:1, :]
                dm_ref[b, 3 * k + 1:3 * k + 2, :] = g_ref[...] * s_dxr
                dm_ref[b, 3 * k + 2:3 * k + 3, :] = mg2g_ref[b, 0:1, :] if k == 1 else mg[b, 2:3, :]
                dgn_ref[k:k + 1, :] += (1.0 + m_ref[b, 3 * k + 1:3 * k + 2, :]) * s_dxr

    args = (mg1, mg2, mg2g, mg3, mods, g1, g2, g3)
    out_shape = [jax.ShapeDtypeStruct((B, 9, D), F32), jax.ShapeDtypeStruct((8, D), F32)]
    return pl.pallas_call(body, name="mod_finish", grid=(1,), in_specs=[_whole(a) for a in args],
                          out_specs=[_whole(o) for o in out_shape], out_shape=out_shape,
                          compiler_params=_params("arbitrary"))(*args)


def _small_update(gathered, params, ms, vs):
    n = len(SMALL)
    B = gathered[0].shape[1]

    def body(*refs):
        gdm, ggn, gln, gws, gbs, ggq, ggk, gsk, gloss = refs[:9]
        w = dict(zip(SMALL, refs[9:9 + n]))
        m = dict(zip(SMALL, refs[9 + n:9 + 2 * n]))
        v = dict(zip(SMALL, refs[9 + 2 * n:9 + 3 * n]))
        outs = refs[9 + 3 * n:-1]
        out = {name: outs[4 * k:4 * k + 4] for k, name in enumerate(SMALL)}

        def total(ref, idx):
            acc = ref[(0,) + idx]
            for dev in range(1, N_DEV):
                acc = acc + ref[(dev,) + idx]
            return acc

        def finish(name, g, idx=(Ellipsis,)):
            d, nm, nv = _adamw(w[name][idx], g, m[name][idx], v[name][idx])
            for ref, val in zip(out[name], (g, d, nm, nv)):
                ref[idx] = val

        g_bada = total(gdm, (slice(0, 1),))
        for b in range(1, B):
            g_bada = g_bada + total(gdm, (slice(b, b + 1),))
        finish("b_ada", g_bada)
        finish("g_norm1", total(ggn, (slice(0, 1),)))
        finish("g_norm2", total(ggn, (slice(1, 2),)))
        finish("g_norm3", total(ggn, (slice(2, 3),)))
        finish("g_sgu_ln", total(gln, (slice(0, 1),)))
        finish("b_sgu_ln", total(gln, (slice(1, 2),)))
        tril = lax.broadcasted_iota(jnp.int32, (BLK, BLK), 0) >= lax.broadcasted_iota(jnp.int32, (BLK, BLK), 1)
        for g in range(N_GRP):
            finish("w_spatial", jnp.where(tril, total(gws, (g,)), 0.0), (0, g))
            finish("b_spatial", jnp.sum(total(gbs, (g,)).T, axis=0, keepdims=True), (0, slice(g, g + 1)))
        finish("g_q", total(ggq, ()))
        finish("g_k", total(ggk, ()))
        finish("attn_sinks", total(gsk, (slice(0, 1), slice(0, N_KV * Q_PER_KV))))
        refs[-1][...] = total(gloss, ())

    args = list(gathered) + [params[k] for k in SMALL] + [ms[k] for k in SMALL] + [vs[k] for k in SMALL]
    out_shape = []
    for k in SMALL:
        out_shape += [jax.ShapeDtypeStruct(params[k].shape, F32)] * 4
    out_shape.append(jax.ShapeDtypeStruct((8, 128), F32))
    res = pl.pallas_call(body, name="small_update", grid=(1,), in_specs=[_whole(a) for a in args],
                         out_specs=[_whole(o) for o in out_shape], out_shape=out_shape,
                         compiler_params=_params("arbitrary"))(*args)
    return {k: res[4 * i:4 * i + 4] for i, k in enumerate(SMALL)}, res[-1][0, 0]


def _rs_add(p, r, core, name):
    _, rows, width = p.shape

    def body(c_ref, p_ref, r_ref, o_ref):
        o_ref[...] = (p_ref[...] + r_ref[...]).astype(BF16)

    return pl.pallas_call(
        body, name="rs_add_" + name, out_shape=jax.ShapeDtypeStruct((4, rows, width), BF16),
        grid_spec=pltpu.PrefetchScalarGridSpec(
            num_scalar_prefetch=1, grid=(4,),
            in_specs=[pl.BlockSpec((1, rows, width), lambda k, c_ref: (2 * k + c_ref[0], 0, 0)),
                      pl.BlockSpec((1, rows, width), lambda k, c_ref: (k, 0, 0))],
            out_specs=pl.BlockSpec((1, rows, width), lambda k, c_ref: (k, 0, 0))),
        compiler_params=_params("parallel"),
    )(core, p, r)


def _adam_big(w, m, v, r, name):
    rows, cols = w.shape

    def body(r_ref, w_ref, m_ref, v_ref, go_ref, d_ref, nm_ref, nv_ref):
        g = r_ref[0].astype(F32)
        for k in range(1, 4):
            g = g + r_ref[k].astype(F32)
        go_ref[...] = g
        d_ref[...], nm_ref[...], nv_ref[...] = _adamw(w_ref[...], g, m_ref[...], v_ref[...])

    br = rows // 2
    shard = pl.BlockSpec((br, cols), lambda i: (i, 0))
    return pl.pallas_call(body, name="adam_" + name, grid=(2,),
                          in_specs=[pl.BlockSpec((4, br, cols), lambda i: (0, i, 0)), shard, shard, shard],
                          out_specs=[shard] * 4, out_shape=[jax.ShapeDtypeStruct(w.shape, F32)] * 4,
                          compiler_params=_params("parallel"))(r, w, m, v)


def kernel(x, c, w_ada, b_ada, g_norm1, ffn1_w_gate, ffn1_w_up, ffn1_w_down, g_norm2, w_in, g_sgu_ln, b_sgu_ln, w_spatial, b_spatial, g_q, g_k, attn_sinks, w_branch_a, w_branch_b, w_out, g_norm3, ffn2_w_gate, ffn2_w_up, ffn2_w_down, loss_target, m_w_ada, m_b_ada, m_g_norm1, m_ffn1_w_gate, m_ffn1_w_up, m_ffn1_w_down, m_g_norm2, m_w_in, m_g_sgu_ln, m_b_sgu_ln, m_w_spatial, m_b_spatial, m_g_q, m_g_k, m_attn_sinks, m_w_branch_a, m_w_branch_b, m_w_out, m_g_norm3, m_ffn2_w_gate, m_ffn2_w_up, m_ffn2_w_down, v_w_ada, v_b_ada, v_g_norm1, v_ffn1_w_gate, v_ffn1_w_up, v_ffn1_w_down, v_g_norm2, v_w_in, v_g_sgu_ln, v_b_sgu_ln, v_w_spatial, v_b_spatial, v_g_q, v_g_k, v_attn_sinks, v_w_branch_a, v_w_branch_b, v_w_out, v_g_norm3, v_ffn2_w_gate, v_ffn2_w_up, v_ffn2_w_down):
    names = ("w_ada", "b_ada", "g_norm1", "ffn1_w_gate", "ffn1_w_up", "ffn1_w_down", "g_norm2", "w_in", "g_sgu_ln",
             "b_sgu_ln", "w_spatial", "b_spatial", "g_q", "g_k", "attn_sinks", "w_branch_a", "w_branch_b", "w_out",
             "g_norm3", "ffn2_w_gate", "ffn2_w_up", "ffn2_w_down")
    w = dict(zip(names, (w_ada, b_ada, g_norm1, ffn1_w_gate, ffn1_w_up, ffn1_w_down, g_norm2, w_in, g_sgu_ln,
                         b_sgu_ln, w_spatial, b_spatial, g_q, g_k, attn_sinks, w_branch_a, w_branch_b, w_out, g_norm3,
                         ffn2_w_gate, ffn2_w_up, ffn2_w_down)))
    m = dict(zip(names, (m_w_ada, m_b_ada, m_g_norm1, m_ffn1_w_gate, m_ffn1_w_up, m_ffn1_w_down, m_g_norm2, m_w_in,
                         m_g_sgu_ln, m_b_sgu_ln, m_w_spatial, m_b_spatial, m_g_q, m_g_k, m_attn_sinks, m_w_branch_a,
                         m_w_branch_b, m_w_out, m_g_norm3, m_ffn2_w_gate, m_ffn2_w_up, m_ffn2_w_down)))
    v = dict(zip(names, (v_w_ada, v_b_ada, v_g_norm1, v_ffn1_w_gate, v_ffn1_w_up, v_ffn1_w_down, v_g_norm2, v_w_in,
                         v_g_sgu_ln, v_b_sgu_ln, v_w_spatial, v_b_spatial, v_g_q, v_g_k, v_attn_sinks, v_w_branch_a,
                         v_w_branch_b, v_w_out, v_g_norm3, v_ffn2_w_gate, v_ffn2_w_up, v_ffn2_w_down)))
    B, seq, _ = x.shape
    T = B * seq
    nbs = seq // BLK
    xi, yi, ci = _place()
    me = 4 * xi + 2 * yi + ci
    core = jnp.reshape(ci, (1,)).astype(jnp.int32)
    layout = {name: (tform, n, width) for name, tform, n, width in BIG}
    shard = {name: w[name][0].astype(BF16).T if tform else w[name][0].astype(BF16) for name, tform, _, _ in BIG}
    wts, parts, big_out = {}, {}, {}

    def gather_plan(group):
        return _Gather([shard[k] for k in group])

    def take(group, gathered):
        for k, g in zip(group, gathered):
            wts[k] = g.reshape(N_DEV * layout[k][1], layout[k][2])

    def blocks(name, grad):
        return grad.reshape(N_DEV, layout[name][1], layout[name][2])

    def to_sibling(names, grads):
        return _RsSibling([blocks(k, g) for k, g in zip(names, grads)])

    def add(names, grads, from_sibling):
        for k, g, r in zip(names, grads, from_sibling):
            parts[k] = _rs_add(blocks(k, g), r, core, k)

    def to_chips(names):
        return _RsChips([parts[k] for k in names])

    def update(names, from_chips):
        for k, r in zip(names, from_chips):
            if layout[k][0]:
                big_out[k] = [o.T[None] for o in _adam_big(w[k][0].T, m[k][0].T, v[k][0].T, r, k)]
            else:
                big_out[k] = [o[None] for o in _adam_big(w[k][0], m[k][0], v[k][0], r, k)]

    g1, g2, g3, g_ln, b_ln = g_norm1, g_norm2, g_norm3, g_sgu_ln, b_sgu_ln
    gq2, gk2 = jnp.tile(g_q, (1, 2)), jnp.tile(g_k, (1, 2))
    ws = w_spatial[0]
    bsb = jnp.broadcast_to(b_spatial[0][:, :, None], (N_GRP, BLK, BLK))
    xf = x.reshape(T, D)
    tgt = loss_target.reshape(T, D)

    ffn1 = ("ffn1_w_gate", "ffn1_w_up", "ffn1_w_down")
    ffn2 = ("ffn2_w_gate", "ffn2_w_up", "ffn2_w_down")
    c_pad = jnp.concatenate([c, jnp.zeros((8 - B, D), F32)], axis=0)
    c_all, mods_cols, gathered = _prologue(c_pad, w_ada[0], [shard[k] for k in ffn1[:2]])
    take(ffn1[:2], gathered)
    c_all = c_all[:, :B].reshape(N_DEV * B, D)
    mine = lax.dynamic_slice_in_dim(mods_cols, 8 * me, B, axis=1)
    mods = (mine.transpose(1, 0, 2).reshape(B, 9 * D) + b_ada).reshape(B, 9, D)

    group = ("ffn1_w_down", "w_branch_a", "w_branch_b", "w_out")
    (gb1, ub1), (gathered,) = _ffn_gate_up(xf, mods, g1, wts["ffn1_w_gate"], wts["ffn1_w_up"], 0, seq,
                                           plans=[gather_plan(group)])
    take(group, gathered)
    group = ("w_in",)
    (h1, y1), (gathered,) = _ffn_down(xf, gb1, ub1, mods, wts["ffn1_w_down"], 0, seq, plans=[gather_plan(group)])
    take(group, gathered)
    group = ("ffn2_w_gate",)
    (au, av, q_tok, k_tok, v_tok, gta, gtb), (gathered,) = _inproj_fwd(h1, mods, g2, wts["w_in"], seq,
                                                                       plans=[gather_plan(group)])
    take(group, gathered)
    group = ("ffn2_w_up",)
    ob, (gathered,) = _swa_fwd(q_tok, k_tok, v_tok, gq2, gk2, attn_sinks, nbs, plans=[gather_plan(group)])
    take(group, gathered)
    mixw = (wts["w_branch_a"], wts["w_branch_b"], wts["w_out"])
    group = ("ffn2_w_down",)
    h2, (gathered,) = _mix_fwd(au, av, gta, gtb, ob, h1, mods, g_ln, b_ln, ws, bsb, *mixw, seq,
                               plans=[gather_plan(group)])
    take(group, gathered)
    dh3, y3, gb3, ub3, loss_parts = _ffn_fwd_loss(h2, mods, g3, *[wts[k] for k in ffn2], 6, seq, tgt)
    loss_part = jnp.full((8, 128), jnp.sum(loss_parts[:, 0, 0]))

    (dh2, xb, dyb, a, dg, du, mg3), _ = _ffn_bwd(h2, dh3, y3, gb3, ub3, mods, g3, *[wts[k] for k in ffn2], 6, seq)
    d_gate, _ = _wgrad(dg, xb, "wgrad_ffn2_gate")
    d_up, (r,) = _wgrad(du, xb, "wgrad_ffn2_up", plans=[to_sibling(ffn2[:1], [d_gate])])
    add(ffn2[:1], [d_gate], r)
    d_down, (r,) = _wgrad(a, dyb, "wgrad_ffn2_down", plans=[to_sibling(ffn2[1:2], [d_up])])
    add(ffn2[1:2], [d_up], r)

    (dau, dav, dgta, dgtb, dob, d_out, d_a, d_b, dws, dbs, dln, mg2g), (r, from_chips) = _mix_bwd(
        au, av, gta, gtb, ob, dh2, mods, g_ln, b_ln, ws, bsb, *mixw, seq,
        plans=[to_sibling(ffn2[2:], [d_down]), to_chips(ffn2[:2])])
    add(ffn2[2:], [d_down], r)
    update(ffn2[:2], from_chips)
    mixers = ("w_out", "w_branch_a", "w_branch_b")
    (dq, dk, dv, dk_halo, dv_halo, dgq2, dgk2, dsk), (from_chips, r) = _swa_bwd(
        q_tok, k_tok, v_tok, gq2, gk2, attn_sinks, dob, nbs,
        plans=[to_chips(ffn2[2:]), to_sibling(mixers, [d_out, d_a, d_b])])
    update(ffn2[2:], from_chips)
    add(mixers, [d_out, d_a, d_b], r)
    dk, dv = _swa_add_halo(dk, dk_halo), _swa_add_halo(dv, dv_halo)
    dgq, dgk = dgq2[:, :HD] + dgq2[:, HD:], dgk2[:, :HD] + dgk2[:, HD:]
    (dh1, xb2, dpb, mg2), (from_chips, early) = _inproj_bwd(
        h1, dh2, (dau, dav, dq, dk, dv, dgta, dgtb), mods, g2, wts["w_in"], seq,
        plans=[to_chips(mixers), _Gather([dln, dws, dbs, dgq, dgk, dsk, loss_part])])
    update(mixers, from_chips)

    d_in, _ = _wgrad(dpb, xb2, "wgrad_w_in")

    (dx, xb, dyb, a, dg, du, mg1), _ = _ffn_bwd(xf, dh1, y1, gb1, ub1, mods, g1, *[wts[k] for k in ffn1], 0, seq)
    dmods, dgn = _mod_finish(mg1, mg2, mg2g, mg3, mods, g1, g2, g3)
    d_gate, (r, late) = _wgrad(dg, xb, "wgrad_ffn1_gate",
                               plans=[to_sibling(("w_in",), [d_in]), _GatherDirect([dmods.reshape(B, 9 * D), dgn])])
    add(("w_in",), [d_in], r)
    gathered = late + early
    d_up, (r, from_chips) = _wgrad(du, xb, "wgrad_ffn1_up",
                                   plans=[to_sibling(ffn1[:1], [d_gate]), to_chips(("w_in",))])
    add(ffn1[:1], [d_gate], r)
    update(("w_in",), from_chips)
    d_down, (r, from_chips) = _wgrad(a, dyb, "wgrad_ffn1_down",
                                     plans=[to_sibling(ffn1[1:2], [d_up]), to_chips(ffn1[:1])])
    add(ffn1[1:2], [d_up], r)
    update(ffn1[:1], from_chips)
    small_out, loss = _small_update(gathered, w, m, v)
    dm_cols = lax.dynamic_slice_in_dim(gathered[0].reshape(N_DEV * B, 9 * D), (9 * D // N_DEV) * me,
                                       9 * D // N_DEV, axis=1)
    ada_out = _ada_update(c_all, dm_cols, w_ada[0], m_w_ada[0], v_w_ada[0])
    r, from_chips = _exchange([to_sibling(ffn1[2:], [d_down]), to_chips(ffn1[1:2])], "rs_tail")
    add(ffn1[2:], [d_down], r)
    update(ffn1[1:2], from_chips)
    (from_chips,) = _exchange([to_chips(ffn1[2:])], "rs_last")
    update(ffn1[2:], from_chips)

    def leaf(kind, name):
        if name == "w_ada":
            return ada_out[kind][None]
        if name in SMALL:
            return small_out[name][kind]
        return big_out[name][kind]

    return (loss, dx.reshape(B, seq, D), *[leaf(kind, name) for kind in range(4) for name in names])
```

```python
import math

import jax
import jax.numpy as jnp
from jax import lax
from jax.experimental import pallas as pl
from jax.experimental.pallas import tpu as pltpu

F32 = jnp.float32
BF16 = jnp.bfloat16
MESH = pl.DeviceIdType.MESH
N_DEV = 8

VMEM_LIMIT = 56 * 1024 * 1024

D = 1024
FF = 2816
FC = 2816
D_A = 512
D_B = 512
HD = 64
N_KV = 2
Q_PER_KV = 4
BLK = 128
N_GRP = 4
IN_COLS = 3840
PIECES = (("au", 0, 512), ("av", 512, 512), ("q", 1024, 512), ("k", 1536, 128), ("v", 1664, 128),
          ("ga", 1792, 1024), ("gb", 2816, 1024))
EPS = 1e-6
NEG = -1e30
GELU_C = math.sqrt(2.0 / math.pi)

ADAM_LR = 0.001
ADAM_B1 = 0.9
ADAM_B2 = 0.999
ADAM_EPS = 1e-08
ADAM_WD = 0.01
ADAM_STEP = 10

NT = (((1,), (1,)), ((), ()))
TN = (((0,), (0,)), ((), ()))

BIG = (("ffn1_w_gate", True, FF // N_DEV, D), ("ffn1_w_up", True, FF // N_DEV, D),
       ("ffn1_w_down", False, FF // N_DEV, D), ("w_in", True, IN_COLS // N_DEV, D),
       ("w_branch_a", True, D // N_DEV, D_A), ("w_branch_b", True, D // N_DEV, D_B), ("w_out", False, D // N_DEV, D),
       ("ffn2_w_gate", True, FF // N_DEV, D), ("ffn2_w_up", True, FF // N_DEV, D),
       ("ffn2_w_down", False, FF // N_DEV, D))
SMALL = ("b_ada", "g_norm1", "g_norm2", "g_sgu_ln", "b_sgu_ln", "w_spatial", "b_spatial", "g_q", "g_k",
         "attn_sinks", "g_norm3")


def _dot(a, b):
    return jnp.dot(a, b, preferred_element_type=F32)


def _dot_nt(a, b):
    return lax.dot_general(a, b, NT, preferred_element_type=F32)


def _dot_tn(a, b):
    return lax.dot_general(a, b, TN, preferred_element_type=F32)


def _vmem():
    return pl.BlockSpec(memory_space=pltpu.VMEM)


def _any():
    return pl.BlockSpec(memory_space=pl.ANY)


def _whole(a):
    return pl.BlockSpec(a.shape, lambda i: (0,) * len(a.shape))


def _rms_mod(h, g, sh, sc):
    inv = lax.rsqrt(jnp.mean(h * h, axis=-1, keepdims=True) + EPS)
    r = h * inv
    return (r * g) * (1.0 + sc) + sh, r, inv


def _rms_mod_bwd(dxn, r, inv, g, sc):
    dr = dxn * (g * (1.0 + sc))
    dh = inv * (dr - r * jnp.mean(dr * r, axis=-1, keepdims=True))
    return dh, jnp.sum(dxn, axis=0, keepdims=True), jnp.sum(dxn * r, axis=0, keepdims=True)


def _gelu(x):
    t = jnp.tanh(GELU_C * (x + 0.044715 * (x * x * x)))
    return 0.5 * x * (1.0 + t), t


def _gelu_grad(x, t):
    return 0.5 * (1.0 + t) + 0.5 * x * (1.0 - t * t) * (GELU_C * (1.0 + 3.0 * 0.044715 * x * x))


def _adamw(w, g, m, v):
    m = ADAM_B1 * m + (1.0 - ADAM_B1) * g
    v = ADAM_B2 * v + (1.0 - ADAM_B2) * (g * g)
    m_hat = m / (1.0 - ADAM_B1 ** ADAM_STEP)
    v_hat = v / (1.0 - ADAM_B2 ** ADAM_STEP)
    delta = -ADAM_LR * (m_hat / (jnp.sqrt(v_hat) + ADAM_EPS) + ADAM_WD * w)
    return delta, m, v


def _token_tile(seq, cap=512):
    return min(cap, seq)


def _params(*semantics):
    return pltpu.CompilerParams(dimension_semantics=semantics, vmem_limit_bytes=VMEM_LIMIT)


def _place():
    return lax.axis_index("x"), lax.axis_index("y"), lax.axis_index("c")


class _Gather:
    def __init__(self, arrays):
        n = len(arrays)
        self.ins = list(arrays)
        self.out_shape = [jax.ShapeDtypeStruct((N_DEV,) + a.shape, a.dtype) for a in arrays]
        self.scratch = [pltpu.SemaphoreType.DMA((n, 7)), pltpu.SemaphoreType.DMA((n, 7)),
                        pltpu.SemaphoreType.DMA((n,))]

    def _copies(self, ins, outs, sems):
        send_sems, recv_sems, local_sems = sems
        n = len(ins)
        x, y, c = _place()
        me, sibling = (x, y, c), (x, y, 1 - c)
        chips = [(1 - x, y), (x, 1 - y), (1 - x, 1 - y)]

        def slot(a, px, py, pc):
            return outs[a].at[4 * px + 2 * py + pc]

        def copy(a, k, block, to, src=None):
            return pltpu.make_async_remote_copy(
                src_ref=slot(a, *block) if src is None else src, dst_ref=slot(a, *block),
                send_sem=send_sems.at[a, k], recv_sem=recv_sems.at[a, k], device_id=to, device_id_type=MESH)

        mine = [pltpu.make_async_copy(ins[a], slot(a, *me), local_sems.at[a]) for a in range(n)]
        first = [copy(a, 0, me, sibling, src=ins[a]) for a in range(n)]
        first += [copy(a, 1 + j, me, (*chip, c), src=ins[a]) for a in range(n) for j, chip in enumerate(chips)]
        landed = [[copy(a, 1 + j, (*chip, c), me) for a in range(n)] for j, chip in enumerate(chips)]
        passed = [[copy(a, 4 + j, (*chip, c), sibling) for a in range(n)] for j, chip in enumerate(chips)]
        from_sibling = [copy(a, 0, sibling, me) for a in range(n)]
        from_sibling += [copy(a, 4 + j, (*chip, 1 - c), me) for a in range(n) for j, chip in enumerate(chips)]
        return mine, first, landed, passed, from_sibling

    def start(self, ins, outs, sems):
        mine, first, _, _, _ = self._copies(ins, outs, sems)
        for cp in mine + first:
            cp.start()

    def relay(self, ins, outs, sems):
        _, _, landed, passed, _ = self._copies(ins, outs, sems)
        for arrivals, forwards in zip(landed, passed):
            for arrival, forward in zip(arrivals, forwards):
                arrival.wait_recv()
                forward.start()

    def finish(self, ins, outs, sems):
        mine, first, _, passed, from_sibling = self._copies(ins, outs, sems)
        for cp in from_sibling:
            cp.wait_recv()
        for cp in first + [f for fs in passed for f in fs]:
            cp.wait_send()
        for cp in mine:
            cp.wait()


class _GatherRelayed:
    def __init__(self, arrays):
        n = len(arrays)
        self.ins = list(arrays)
        self.out_shape = [jax.ShapeDtypeStruct((N_DEV,) + a.shape, a.dtype) for a in arrays]
        self.scratch = [pltpu.SemaphoreType.DMA((n, 7)), pltpu.SemaphoreType.DMA((n, 7)),
                        pltpu.SemaphoreType.DMA((n,))]

    def _copies(self, ins, outs, sems):
        send_sems, recv_sems, local_sems = sems
        n = len(ins)
        x, y, c = _place()
        me, sibling = (x, y, c), (x, y, 1 - c)
        south = c == 0
        via = (jnp.where(south, 1 - x, x), jnp.where(south, y, 1 - y))
        onward = (jnp.where(south, x, 1 - x), jnp.where(south, 1 - y, y))
        via_sem = jnp.where(south, 1, 2)

        def slot(a, px, py, pc):
            return outs[a].at[4 * px + 2 * py + pc]

        def copy(a, k, block, to, src=None):
            return pltpu.make_async_remote_copy(
                src_ref=slot(a, *block) if src is None else src, dst_ref=slot(a, *block),
                send_sem=send_sems.at[a, k], recv_sem=recv_sems.at[a, k], device_id=to, device_id_type=MESH)

        chips = [(1 - x, y), (x, 1 - y), (1 - x, 1 - y)]
        mine = [pltpu.make_async_copy(ins[a], slot(a, *me), local_sems.at[a]) for a in range(n)]
        first = [copy(a, 0, me, sibling, src=ins[a]) for a in range(n)]
        first += [copy(a, 1 + j, me, (*chips[j], c), src=ins[a]) for a in range(n) for j in range(2)]
        to_pass = [copy(a, via_sem, (*via, c), me) for a in range(n)]
        passed_on = [copy(a, 3, (*via, c), (*onward, c)) for a in range(n)]
        landed = [[copy(a, 1 + j, (*chips[j], c), me) for a in range(n)] for j in range(3)]
        to_sibling = [[copy(a, 4 + j, (*chips[j], c), sibling) for a in range(n)] for j in range(3)]
        from_sibling = [copy(a, 0, sibling, me) for a in range(n)]
        from_sibling += [copy(a, 4 + j, (*chips[j], 1 - c), me) for a in range(n) for j in range(3)]
        return mine, first, to_pass, passed_on, landed, to_sibling, from_sibling

    def start(self, ins, outs, sems):
        mine, first, _, _, _, _, _ = self._copies(ins, outs, sems)
        for cp in mine + first:
            cp.start()

    def relay(self, ins, outs, sems):
        _, _, to_pass, passed_on, landed, to_sibling, _ = self._copies(ins, outs, sems)
        for arrival, onward in zip(to_pass, passed_on):
            arrival.wait_recv()
            onward.start()
        x, y, c = _place()
        for j in range(3):
            for a, (arrival, forward) in enumerate(zip(landed[j], to_sibling[j])):
                if j < 2:
                    pl.when((c == 0) != (j == 0))(arrival.wait_recv)
                else:
                    arrival.wait_recv()
                forward.start()

    def finish(self, ins, outs, sems):
        mine, first, _, passed_on, _, to_sibling, from_sibling = self._copies(ins, outs, sems)
        for cp in from_sibling:
            cp.wait_recv()
        for cp in first + passed_on + [f for fs in to_sibling for f in fs]:
            cp.wait_send()
        for cp in mine:
            cp.wait()


class _GatherDirect:
    def __init__(self, arrays):
        n = len(arrays)
        self.ins = list(arrays)
        self.out_shape = [jax.ShapeDtypeStruct((N_DEV,) + a.shape, a.dtype) for a in arrays]
        self.scratch = [pltpu.SemaphoreType.DMA((n, 7)), pltpu.SemaphoreType.DMA((n, 7)),
                        pltpu.SemaphoreType.DMA((n,))]

    def _copies(self, ins, outs, sems):
        send_sems, recv_sems, local_sems = sems
        n = len(ins)
        x, y, c = _place()
        peers = [(x ^ (k >> 2 & 1), y ^ (k >> 1 & 1), c ^ (k & 1)) for k in range(1, N_DEV)]

        def slot(a, px, py, pc):
            return outs[a].at[4 * px + 2 * py + pc]

        def copy(a, k, block, to, src=None):
            return pltpu.make_async_remote_copy(
                src_ref=slot(a, *block) if src is None else src, dst_ref=slot(a, *block),
                send_sem=send_sems.at[a, k], recv_sem=recv_sems.at[a, k], device_id=to, device_id_type=MESH)

        mine = [pltpu.make_async_copy(ins[a], slot(a, x, y, c), local_sems.at[a]) for a in range(n)]
        sends = [copy(a, k, (x, y, c), peer, src=ins[a]) for a in range(n) for k, peer in enumerate(peers)]
        arrivals = [copy(a, k, peer, (x, y, c)) for a in range(n) for k, peer in enumerate(peers)]
        return mine, sends, arrivals

    def start(self, ins, outs, sems):
        mine, sends, _ = self._copies(ins, outs, sems)
        for cp in mine + sends:
            cp.start()

    def relay(self, ins, outs, sems):
        pass

    def finish(self, ins, outs, sems):
        mine, sends, arrivals = self._copies(ins, outs, sems)
        for cp in arrivals:
            cp.wait_recv()
        for cp in sends:
            cp.wait_send()
        for cp in mine:
            cp.wait()


class _RsSibling:
    def __init__(self, ps):
        n = len(ps)
        self.ins = list(ps)
        self.out_shape = [jax.ShapeDtypeStruct((4,) + p.shape[1:], p.dtype) for p in ps]
        self.scratch = [pltpu.SemaphoreType.DMA((n, 4)), pltpu.SemaphoreType.DMA((n, 4))]

    def _copies(self, ins, outs, sems):
        send_sems, recv_sems = sems
        x, y, c = _place()
        return [pltpu.make_async_remote_copy(
            src_ref=ins[a].at[2 * q + (1 - c)], dst_ref=outs[a].at[q], send_sem=send_sems.at[a, q],
            recv_sem=recv_sems.at[a, q], device_id=(x, y, 1 - c), device_id_type=MESH)
            for a in range(len(ins)) for q in range(4)]

    def start(self, ins, outs, sems):
        for cp in self._copies(ins, outs, sems):
            cp.start()

    def relay(self, ins, outs, sems):
        pass

    def finish(self, ins, outs, sems):
        for cp in self._copies(ins, outs, sems):
            cp.wait()


class _RsChips:
    def __init__(self, qs):
        n = len(qs)
        self.ins = list(qs)
        self.out_shape = [jax.ShapeDtypeStruct(q.shape, q.dtype) for q in qs]
        self.scratch = [pltpu.SemaphoreType.DMA((n, 3)), pltpu.SemaphoreType.DMA((n, 3)),
                        pltpu.SemaphoreType.DMA((n,))]

    def _copies(self, ins, outs, sems):
        send_sems, recv_sems, local_sems = sems
        n = len(ins)
        x, y, c = _place()
        my_chip = 2 * x + y
        chips = [(1 - x, y), (x, 1 - y), (1 - x, 1 - y)]

        def copy(a, j, src_slot, dst_slot):
            px, py = chips[j]
            return pltpu.make_async_remote_copy(
                src_ref=ins[a].at[src_slot], dst_ref=outs[a].at[dst_slot], send_sem=send_sems.at[a, j],
                recv_sem=recv_sems.at[a, j], device_id=(px, py, c), device_id_type=MESH)

        own = [pltpu.make_async_copy(ins[a].at[my_chip], outs[a].at[my_chip], local_sems.at[a]) for a in range(n)]
        sends = [copy(a, j, 2 * px + py, my_chip) for a in range(n) for j, (px, py) in enumerate(chips)]
        arrivals = [copy(a, j, my_chip, 2 * px + py) for a in range(n) for j, (px, py) in enumerate(chips)]
        return own, sends, arrivals

    def start(self, ins, outs, sems):
        own, sends, _ = self._copies(ins, outs, sems)
        for cp in own + sends:
            cp.start()

    def relay(self, ins, outs, sems):
        pass

    def finish(self, ins, outs, sems):
        own, sends, arrivals = self._copies(ins, outs, sems)
        for cp in arrivals:
            cp.wait_recv()
        for cp in sends:
            cp.wait_send()
        for cp in own:
            cp.wait()


def _split_plans(plans, refs_in, refs_out, refs_scr, phase):
    i = o = s = 0
    for p in plans:
        ni, no, ns = len(p.ins), len(p.out_shape), len(p.scratch)
        getattr(p, phase)(refs_in[i:i + ni], refs_out[o:o + no], refs_scr[s:s + ns])
        i, o, s = i + ni, o + no, s + ns


def _plan_results(plans, res):
    out = []
    for p in plans:
        out.append(list(res[:len(p.out_shape)]))
        res = res[len(p.out_shape):]
    return out


def _exchange(plans, name):
    c_in = [a for p in plans for a in p.ins]
    c_out = [s for p in plans for s in p.out_shape]
    c_scr = [s for p in plans for s in p.scratch]

    def body(*refs):
        cin, cout, cscr = refs[:len(c_in)], refs[len(c_in):len(c_in) + len(c_out)], refs[len(c_in) + len(c_out):]
        for phase in ("start", "relay", "finish"):
            _split_plans(plans, cin, cout, cscr, phase)

    res = pl.pallas_call(body, name=name, in_specs=[_any()] * len(c_in), out_specs=[_any()] * len(c_out),
                         out_shape=c_out, scratch_shapes=c_scr)(*c_in)
    return _plan_results(plans, res)


def _call(body, *, name, grid, in_specs, out_specs, out_shape, args, semantics, scratch_shapes=(), plans=()):
    n_in, n_out, n_scr = len(in_specs), len(out_specs), len(scratch_shapes)
    c_in = [a for p in plans for a in p.ins]
    c_out = [s for p in plans for s in p.out_shape]
    c_scr = [s for p in plans for s in p.scratch]
    n_steps = math.prod(grid)

    def wrapped(*refs):
        ins, refs = refs[:n_in], refs[n_in:]
        cin, refs = refs[:len(c_in)], refs[len(c_in):]
        outs, refs = refs[:n_out], refs[n_out:]
        cout, refs = refs[:len(c_out)], refs[len(c_out):]
        scr, cscr = refs[:n_scr], refs[n_scr:]
        if plans:
            step = 0
            for d, g in enumerate(grid):
                step = step * g + pl.program_id(d)
            pl.when(step == 0)(lambda: _split_plans(plans, cin, cout, cscr, "start"))
        body(*ins, *outs, *scr)
        if plans:
            pl.when(step == max(n_steps - 2, 0))(lambda: _split_plans(plans, cin, cout, cscr, "relay"))
            pl.when(step == n_steps - 1)(lambda: _split_plans(plans, cin, cout, cscr, "finish"))

    res = pl.pallas_call(
        wrapped, name=name, grid=grid, in_specs=list(in_specs) + [_any()] * len(c_in),
        out_specs=list(out_specs) + [_any()] * len(c_out), out_shape=list(out_shape) + c_out,
        scratch_shapes=list(scratch_shapes) + c_scr,
        compiler_params=_params(*(("arbitrary",) * len(grid) if plans else semantics)),
    )(*args, *c_in)
    return list(res[:n_out]), _plan_results(plans, res[n_out:])


def _ffn_fwd_loss(h, mods, gn, wg, wu, wd, row0, seq, tgt):
    T = h.shape[0]
    tm = _token_tile(seq, 256)
    tps = seq // tm
    n_t = T // tm

    def body(h_ref, m_ref, g_ref, wg_ref, wu_ref, wd_ref, tgt_ref, out_ref, y_ref, gb_ref, ub_ref, loss_ref):
        hv = h_ref[...]
        sh = m_ref[0, row0:row0 + 1, :]
        sc = m_ref[0, row0 + 1:row0 + 2, :]
        ga = m_ref[0, row0 + 2:row0 + 3, :]
        xn, _, _ = _rms_mod(hv, g_ref[...], sh, sc)
        xb = xn.astype(BF16)
        acc = jnp.zeros((tm, D), F32)
        for c0 in range(0, FF, FC):
            gg = _dot_nt(xb, wg_ref[c0:c0 + FC, :])
            uu = _dot_nt(xb, wu_ref[c0:c0 + FC, :])
            gg, uu = gg.astype(BF16), uu.astype(BF16)
            gb_ref[:, c0:c0 + FC] = gg
            ub_ref[:, c0:c0 + FC] = uu
            acc = acc + _dot((gg * jax.nn.sigmoid(gg)) * uu, wd_ref[c0:c0 + FC, :])
        y_ref[...] = acc
        d = hv + (0.5 * ga) * acc - tgt_ref[...]
        out_ref[...] = d * (1.0 / D)
        loss_ref[...] = jnp.full((1, 8, 128), 0.5 / D, F32) * jnp.sum(d * d)

    tok = pl.BlockSpec((tm, D), lambda i: (i, 0))
    tokf = pl.BlockSpec((tm, FF), lambda i: (i, 0))
    outs, _ = _call(
        body, name="ffn_fwd_loss", grid=(n_t,),
        in_specs=[tok, pl.BlockSpec((1, 9, D), lambda i: (i // tps, 0, 0)), pl.BlockSpec((1, D), lambda i: (0, 0)),
                  _vmem(), _vmem(), _vmem(), tok],
        out_specs=[tok, tok, tokf, tokf, pl.BlockSpec((1, 8, 128), lambda i: (i, 0, 0))],
        out_shape=[jax.ShapeDtypeStruct((T, D), F32), jax.ShapeDtypeStruct((T, D), F32),
                   jax.ShapeDtypeStruct((T, FF), BF16), jax.ShapeDtypeStruct((T, FF), BF16),
                   jax.ShapeDtypeStruct((n_t, 8, 128), F32)],
        args=(h, mods, gn, wg, wu, wd, tgt), semantics=("parallel",))
    return outs


def _ffn_gate_up(h, mods, gn, wg, wu, row0, seq, plans=()):
    T = h.shape[0]
    tm = _token_tile(seq, 512)
    tps = seq // tm

    def body(h_ref, m_ref, g_ref, wg_ref, wu_ref, gb_ref, ub_ref):
        xn, _, _ = _rms_mod(h_ref[...], g_ref[...], m_ref[0, row0:row0 + 1, :], m_ref[0, row0 + 1:row0 + 2, :])
        xb = xn.astype(BF16)
        for c0 in range(0, FF, FC):
            gb_ref[:, c0:c0 + FC] = _dot_nt(xb, wg_ref[c0:c0 + FC, :]).astype(BF16)
            ub_ref[:, c0:c0 + FC] = _dot_nt(xb, wu_ref[c0:c0 + FC, :]).astype(BF16)

    tokf = pl.BlockSpec((tm, FF), lambda i: (i, 0))
    return _call(
        body, name="ffn_gate_up", grid=(T // tm,),
        in_specs=[pl.BlockSpec((tm, D), lambda i: (i, 0)), pl.BlockSpec((1, 9, D), lambda i: (i // tps, 0, 0)),
                  pl.BlockSpec((1, D), lambda i: (0, 0)), _vmem(), _vmem()],
        out_specs=[tokf, tokf], out_shape=[jax.ShapeDtypeStruct((T, FF), BF16)] * 2, args=(h, mods, gn, wg, wu),
        semantics=("parallel",), plans=plans)


def _ffn_down(h, gb, ub, mods, wd, row0, seq, plans=()):
    T = h.shape[0]
    tm = _token_tile(seq, 512)
    tps = seq // tm

    def body(h_ref, gb_ref, ub_ref, m_ref, wd_ref, out_ref, y_ref):
        acc = jnp.zeros((tm, D), F32)
        for c0 in range(0, FF, FC):
            gg = gb_ref[:, c0:c0 + FC]
            acc = acc + _dot((gg * jax.nn.sigmoid(gg)) * ub_ref[:, c0:c0 + FC], wd_ref[c0:c0 + FC, :])
        y_ref[...] = acc
        out_ref[...] = h_ref[...] + (0.5 * m_ref[0, row0 + 2:row0 + 3, :]) * acc

    tok = pl.BlockSpec((tm, D), lambda i: (i, 0))
    tokf = pl.BlockSpec((tm, FF), lambda i: (i, 0))
    return _call(
        body, name="ffn_down", grid=(T // tm,),
        in_specs=[tok, tokf, tokf, pl.BlockSpec((1, 9, D), lambda i: (i // tps, 0, 0)), _vmem()],
        out_specs=[tok, tok], out_shape=[jax.ShapeDtypeStruct((T, D), F32)] * 2, args=(h, gb, ub, mods, wd),
        semantics=("parallel",), plans=plans)


def _ffn_bwd(h, dhn, y, gb, ub, mods, gn, wg, wu, wd, row0, seq, plans=()):
    T = h.shape[0]
    B = T // seq
    tm = _token_tile(seq, 256)
    tps = seq // tm
    n_t = T // tm

    def body(h_ref, dhn_ref, y_ref, gb_ref, ub_ref, m_ref, g_ref, wg_ref, wu_ref, wd_ref,
             dh_ref, xb_ref, dyb_ref, a_ref, dg_ref, du_ref, mg_ref):
        i = pl.program_id(0)
        hv = h_ref[...]
        dhn = dhn_ref[...]
        sh = m_ref[0, row0:row0 + 1, :]
        sc = m_ref[0, row0 + 1:row0 + 2, :]
        ga = m_ref[0, row0 + 2:row0 + 3, :]
        g = g_ref[...]
        xn, r, inv = _rms_mod(hv, g, sh, sc)
        xb_ref[...] = xn.astype(BF16)
        dyb = ((0.5 * ga) * dhn).astype(BF16)
        dyb_ref[...] = dyb
        dga = 0.5 * jnp.sum(dhn * y_ref[...], axis=0, keepdims=True)
        dxn = jnp.zeros((tm, D), F32)
        for c0 in range(0, FF, FC):
            wgc = wg_ref[c0:c0 + FC, :]
            wuc = wu_ref[c0:c0 + FC, :]
            gg = gb_ref[:, c0:c0 + FC]
            uu = ub_ref[:, c0:c0 + FC]
            sig = jax.nn.sigmoid(gg)
            s = gg * sig
            a_ref[:, c0:c0 + FC] = s * uu
            da = _dot_nt(dyb, wd_ref[c0:c0 + FC, :]).astype(BF16)
            dub = da * s
            dgb = (da * uu) * (sig * (1.0 + gg * (1.0 - sig)))
            dg_ref[:, c0:c0 + FC] = dgb
            du_ref[:, c0:c0 + FC] = dub
            dxn = dxn + _dot(dgb, wgc) + _dot(dub, wuc)
        dh, s_dxn, s_dxr = _rms_mod_bwd(dxn, r, inv, g, sc)
        dh_ref[...] = dhn + dh

        @pl.when(i % tps == 0)
        def _():
            mg_ref[...] = jnp.zeros(mg_ref.shape, F32)

        mg_ref[0, 0:1, :] += s_dxn
        mg_ref[0, 1:2, :] += s_dxr
        mg_ref[0, 2:3, :] += dga

    tok = pl.BlockSpec((tm, D), lambda i: (i, 0))
    tokf = pl.BlockSpec((tm, FF), lambda i: (i, 0))
    return _call(
        body, name="ffn_bwd", grid=(n_t,),
        in_specs=[tok, tok, tok, tokf, tokf, pl.BlockSpec((1, 9, D), lambda i: (i // tps, 0, 0)),
                  pl.BlockSpec((1, D), lambda i: (0, 0)), _vmem(), _vmem(), _vmem()],
        out_specs=[tok, tok, tok, tokf, tokf, tokf, pl.BlockSpec((1, 8, D), lambda i: (i // tps, 0, 0))],
        out_shape=[jax.ShapeDtypeStruct((T, D), F32), jax.ShapeDtypeStruct((T, D), BF16),
                   jax.ShapeDtypeStruct((T, D), BF16), jax.ShapeDtypeStruct((T, FF), BF16),
                   jax.ShapeDtypeStruct((T, FF), BF16), jax.ShapeDtypeStruct((T, FF), BF16),
                   jax.ShapeDtypeStruct((B, 8, D), F32)],
        args=(h, dhn, y, gb, ub, mods, gn, wg, wu, wd), semantics=("arbitrary",), plans=plans)


def _wgrad(a, b, name, plans=()):
    T, da = a.shape
    db = b.shape[1]
    bm = {2816: 1408, 3840: 1280}[da]
    bn = db
    tk = min(2048, T)
    nk = T // tk

    def body(a_ref, b_ref, o_ref):
        @pl.when(pl.program_id(2) == 0)
        def _():
            o_ref[...] = jnp.zeros(o_ref.shape, F32)

        o_ref[...] += _dot_tn(a_ref[...], b_ref[...])

    (out,), plan_outs = _call(
        body, name=name, grid=(da // bm, db // bn, nk),
        in_specs=[pl.BlockSpec((tk, bm), lambda i, j, k: (k, i)), pl.BlockSpec((tk, bn), lambda i, j, k: (k, j))],
        out_specs=[pl.BlockSpec((bm, bn), lambda i, j, k: (i, j))], out_shape=[jax.ShapeDtypeStruct((da, db), F32)],
        args=(a, b), semantics=("parallel", "parallel", "arbitrary"), plans=plans)
    return out, plan_outs


def _inproj_fwd(h, mods, gn, w_in, seq, plans=()):
    T = h.shape[0]
    tm = _token_tile(seq)
    tps = seq // tm

    def body(h_ref, m_ref, g_ref, w_ref, *outs):
        xn, _, _ = _rms_mod(h_ref[...], g_ref[...], m_ref[0, 3:4, :], m_ref[0, 4:5, :])
        xb = xn.astype(BF16)
        for (_, c0, w), o_ref in zip(PIECES, outs):
            o_ref[...] = _dot_nt(xb, w_ref[c0:c0 + w, :])

    return _call(
        body, name="inproj_fwd", grid=(T // tm,),
        in_specs=[pl.BlockSpec((tm, D), lambda i: (i, 0)), pl.BlockSpec((1, 9, D), lambda i: (i // tps, 0, 0)),
                  pl.BlockSpec((1, D), lambda i: (0, 0)), _vmem()],
        out_specs=[pl.BlockSpec((tm, w), lambda i: (i, 0)) for _, _, w in PIECES],
        out_shape=[jax.ShapeDtypeStruct((T, w), F32) for _, _, w in PIECES], args=(h, mods, gn, w_in),
        semantics=("parallel",), plans=plans)


def _inproj_bwd(h, dh_res, dpieces, mods, gn, w_in, seq, plans=()):
    T = h.shape[0]
    B = T // seq
    tm = _token_tile(seq, 512)
    tps = seq // tm

    def body(h_ref, dres_ref, *rest):
        dp_refs = rest[:len(PIECES)]
        m_ref, g_ref, w_ref, dh_ref, xb_ref, dpb_ref, mg_ref = rest[len(PIECES):]
        i = pl.program_id(0)
        g = g_ref[...]
        sc = m_ref[0, 4:5, :]
        xn, r, inv = _rms_mod(h_ref[...], g, m_ref[0, 3:4, :], sc)
        xb_ref[...] = xn.astype(BF16)
        dxn = jnp.zeros((tm, D), F32)
        for (_, c0, w), dp_ref in zip(PIECES, dp_refs):
            dpb = dp_ref[...].astype(BF16)
            dpb_ref[:, c0:c0 + w] = dpb
            dxn = dxn + _dot(dpb, w_ref[c0:c0 + w, :])
        dh, s_dxn, s_dxr = _rms_mod_bwd(dxn, r, inv, g, sc)
        dh_ref[...] = dres_ref[...] + dh

        @pl.when(i % tps == 0)
        def _():
            mg_ref[...] = jnp.zeros(mg_ref.shape, F32)

        mg_ref[0, 0:1, :] += s_dxn
        mg_ref[0, 1:2, :] += s_dxr

    tok = pl.BlockSpec((tm, D), lambda i: (i, 0))
    return _call(
        body, name="inproj_bwd", grid=(T // tm,),
        in_specs=[tok, tok] + [pl.BlockSpec((tm, w), lambda i: (i, 0)) for _, _, w in PIECES]
        + [pl.BlockSpec((1, 9, D), lambda i: (i // tps, 0, 0)), pl.BlockSpec((1, D), lambda i: (0, 0)), _vmem()],
        out_specs=[tok, tok, pl.BlockSpec((tm, IN_COLS), lambda i: (i, 0)),
                   pl.BlockSpec((1, 8, D), lambda i: (i // tps, 0, 0))],
        out_shape=[jax.ShapeDtypeStruct((T, D), F32), jax.ShapeDtypeStruct((T, D), BF16),
                   jax.ShapeDtypeStruct((T, IN_COLS), BF16), jax.ShapeDtypeStruct((B, 8, D), F32)],
        args=(h, dh_res, *dpieces, mods, gn, w_in), semantics=("arbitrary",), plans=plans)


def _seg_mean(x):
    i = lax.broadcasted_iota(jnp.int32, (128, 128), 0) >> 6
    j = lax.broadcasted_iota(jnp.int32, (128, 128), 1) >> 6
    ones = jnp.where(i == j, 1.0 / HD, 0.0).astype(BF16)
    hi = x.astype(BF16)
    lo = (x - hi.astype(F32)).astype(BF16)
    return _dot(hi, ones) + _dot(lo, ones)


def _head_norm(x, g2):
    inv = lax.rsqrt(_seg_mean(x * x) + EPS)
    y = x * inv
    return y * g2, y, inv


def _head_norm_bwd(dxn, y, inv, g2):
    dy = dxn * g2
    return inv * (dy - y * _seg_mean(dy * y)), jnp.sum(dxn * y, axis=0, keepdims=True)


def _swa_block(q, kk, vv, gq2, gk2, sinks, first, do=None):
    lo = lax.broadcasted_iota(jnp.int32, (1, 128), 1) < HD
    kn, ky, kinv = _head_norm(kk, gk2)

    def operands(x):
        xr = pltpu.roll(x, HD, 1)
        own_lo, own_hi = jnp.where(lo, x, 0.0).astype(BF16), jnp.where(lo, 0.0, x).astype(BF16)
        rolled_lo, rolled_hi = jnp.where(lo, xr, 0.0).astype(BF16), jnp.where(lo, 0.0, xr).astype(BF16)
        return (own_lo, rolled_hi), (rolled_lo, own_hi)

    def restore(parts):
        (own_lo, rolled_hi), (rolled_lo, own_hi) = parts
        return (jnp.where(lo, own_lo, own_hi)
                + pltpu.roll(jnp.where(lo, rolled_lo, rolled_hi), HD, 1))

    k_ops, v_ops = operands(kn), operands(vv)
    k2 = [jnp.concatenate(pair, axis=0) for pair in k_ops]
    def stack(x):
        return jnp.concatenate([x[:, 128 * p:128 * (p + 1)] for p in range(4)], axis=0)

    def unstack(x):
        return jnp.concatenate([x[BLK * p:BLK * (p + 1)] for p in range(4)], axis=1)

    def of_head(x, kh):
        return x[2 * BLK * kh:2 * BLK * (kh + 1)]

    nq = 4 * BLK
    pair = lax.broadcasted_iota(jnp.int32, (nq, 1), 0) >> 7
    row = lax.broadcasted_iota(jnp.int32, (nq, 2 * BLK), 0) & (BLK - 1)
    col = lax.broadcasted_iota(jnp.int32, (nq, 2 * BLK), 1)
    valid = (col <= row + BLK) & (col > row) & (col >= jnp.where(first, BLK, 0))
    scale = HD ** -0.5
    qn, qy, qinv = _head_norm(stack(q), gq2)
    qnb = qn.astype(BF16)
    s2 = jnp.concatenate([_dot_nt(of_head(qnb, kh), k2[kh]) for kh in range(N_KV)], axis=0) * scale
    probs, p_sink = [], []
    for j in range(2):
        s = jnp.where(valid, s2[:, 2 * BLK * j:2 * BLK * (j + 1)], NEG)
        sink = jnp.zeros((nq, 1), F32)
        for p in range(4):
            sink = jnp.where(pair == p, sinks[:, 2 * p + j:2 * p + j + 1], sink)
        m = jnp.maximum(jnp.max(s, axis=-1, keepdims=True), sink)
        e = jnp.exp(s - m)
        e_sink = jnp.exp(sink - m)
        rden = 1.0 / (jnp.sum(e, axis=-1, keepdims=True) + e_sink)
        probs.append(e * rden)
        p_sink.append(e_sink * rden)
    pb = [p.astype(BF16) for p in probs]
    if do is None:
        return unstack(jnp.concatenate(
            [_dot(of_head(pb[0], kh), v_ops[kh][0]) + _dot(of_head(pb[1], kh), v_ops[kh][1]) for kh in range(N_KV)],
            axis=0))
    dob = stack(do).astype(BF16)
    ds, dsinks = [], [None] * 8
    for j in range(2):
        dp = jnp.concatenate([_dot_nt(of_head(dob, kh), v_ops[kh][j]) for kh in range(N_KV)], axis=0)
        t = jnp.sum(dp * probs[j], axis=-1, keepdims=True)
        ds.append(probs[j] * (dp - t) * scale)
        lost = p_sink[j] * t
        for p in range(4):
            dsinks[2 * p + j] = -jnp.sum(lost[BLK * p:BLK * (p + 1)])
    dsb = jnp.concatenate(ds, axis=1).astype(BF16)
    dqn = jnp.concatenate([_dot(of_head(dsb, kh), k2[kh]) for kh in range(N_KV)], axis=0)
    dq, dgq2 = _head_norm_bwd(dqn, qy, qinv, gq2)
    dk2 = [_dot_tn(of_head(dsb, kh), of_head(qnb, kh)) for kh in range(N_KV)]
    dv_ops = [[_dot_tn(of_head(pb[j], kh), of_head(dob, kh)) for j in range(2)] for kh in range(N_KV)]
    dkn = restore(tuple((d[:2 * BLK], d[2 * BLK:]) for d in dk2))
    dvv = restore(tuple(tuple(d) for d in dv_ops))
    dkk, dgk2 = _head_norm_bwd(dkn, ky, kinv, gk2)
    return unstack(dq), dkk, dvv, dgq2, dgk2, dsinks


SWA_GROUP = 8


def _swa_specs(nbs):
    grp = min(SWA_GROUP, nbs)
    rows = grp * BLK

    def tok(w):
        return pl.BlockSpec((rows, w), lambda i: (i, 0))

    halo = pl.BlockSpec((BLK, 128), lambda i: (jnp.maximum(i * grp - 1, 0), 0))
    vec = pl.BlockSpec((1, 128), lambda i: (0, 0))
    sk = pl.BlockSpec((1, 8), lambda i: (0, 0))
    return grp, tok, halo, vec, sk


def _swa_fwd(q, k, v, gq2, gk2, sinks, nbs, plans=()):
    T = q.shape[0]
    grp, tok, halo, vec, sk = _swa_specs(nbs)

    def body(q_ref, kh_ref, kc_ref, vh_ref, vc_ref, gq_ref, gk_ref, sk_ref, o_ref):
        seq_start = ((pl.program_id(0) * grp) % nbs) == 0
        for g in range(grp):
            rows = slice(g * BLK, (g + 1) * BLK)
            prev = slice((g - 1) * BLK, g * BLK)
            kk = jnp.concatenate([kh_ref[...] if g == 0 else kc_ref[prev, :], kc_ref[rows, :]], axis=0)
            vv = jnp.concatenate([vh_ref[...] if g == 0 else vc_ref[prev, :], vc_ref[rows, :]], axis=0)
            o_ref[rows, :] = _swa_block(q_ref[rows, :], kk, vv, gq_ref[...], gk_ref[...], sk_ref[...],
                                        seq_start if g == 0 else False)

    (out,), plan_outs = _call(
        body, name="swa_fwd", grid=(T // (grp * BLK),),
        in_specs=[tok(D_B), halo, tok(128), halo, tok(128), vec, vec, sk], out_specs=[tok(D_B)],
        out_shape=[jax.ShapeDtypeStruct((T, D_B), F32)], args=(q, k, k, v, v, gq2, gk2, sinks),
        semantics=("parallel",), plans=plans)
    return out, plan_outs


def _swa_bwd(q, k, v, gq2, gk2, sinks, do, nbs, plans=()):
    T = q.shape[0]
    grp, tok, halo, vec, sk = _swa_specs(nbs)
    steps = T // (grp * BLK)

    def body(q_ref, kh_ref, kc_ref, vh_ref, vc_ref, gq_ref, gk_ref, sk_ref, do_ref,
             dq_ref, dk_ref, dv_ref, dkh_ref, dvh_ref, dgq_ref, dgk_ref, dsk_ref):
        i = pl.program_id(0)
        seq_start = ((i * grp) % nbs) == 0

        @pl.when(i == 0)
        def _():
            for r in (dgq_ref, dgk_ref, dsk_ref):
                r[...] = jnp.zeros(r.shape, F32)

        res = []
        for g in range(grp):
            rows = slice(g * BLK, (g + 1) * BLK)
            prev = slice((g - 1) * BLK, g * BLK)
            kk = jnp.concatenate([kh_ref[...] if g == 0 else kc_ref[prev, :], kc_ref[rows, :]], axis=0)
            vv = jnp.concatenate([vh_ref[...] if g == 0 else vc_ref[prev, :], vc_ref[rows, :]], axis=0)
            res.append(_swa_block(q_ref[rows, :], kk, vv, gq_ref[...], gk_ref[...], sk_ref[...],
                                  seq_start if g == 0 else False, do=do_ref[rows, :]))
        lane = lax.broadcasted_iota(jnp.int32, (8, 128), 1)
        upd = jnp.zeros((8, 128), F32)
        for g, (dq, dkk, dvv, dgq2, dgk2, dsinks) in enumerate(res):
            rows = slice(g * BLK, (g + 1) * BLK)
            dq_ref[rows, :] = dq
            dk_ref[rows, :] = dkk[BLK:] + res[g + 1][1][:BLK] if g + 1 < grp else dkk[BLK:]
            dv_ref[rows, :] = dvv[BLK:] + res[g + 1][2][:BLK] if g + 1 < grp else dvv[BLK:]
            dgq_ref[...] += dgq2
            dgk_ref[...] += dgk2
            for h, d in enumerate(dsinks):
                upd = upd + jnp.where(lane == h, d, 0.0)
        dkh_ref[0] = res[0][1][:BLK]
        dvh_ref[0] = res[0][2][:BLK]
        dsk_ref[...] += upd

    one = pl.BlockSpec((1, BLK, 128), lambda i: (i, 0, 0))
    halo_shape = jax.ShapeDtypeStruct((steps, BLK, 128), F32)
    return _call(
        body, name="swa_bwd", grid=(steps,),
        in_specs=[tok(D_B), halo, tok(128), halo, tok(128), vec, vec, sk, tok(D_B)],
        out_specs=[tok(D_B), tok(128), tok(128), one, one, vec, vec, pl.BlockSpec((8, 128), lambda i: (0, 0))],
        out_shape=[jax.ShapeDtypeStruct((T, D_B), F32), jax.ShapeDtypeStruct((T, 128), F32),
                   jax.ShapeDtypeStruct((T, 128), F32), halo_shape, halo_shape, jax.ShapeDtypeStruct((1, 128), F32),
                   jax.ShapeDtypeStruct((1, 128), F32), jax.ShapeDtypeStruct((8, 128), F32)],
        args=(q, k, k, v, v, gq2, gk2, sinks, do), semantics=("arbitrary",), plans=plans)


def _swa_add_halo(dk, dk_halo):
    steps = dk_halo.shape[0]
    nxt = jnp.concatenate([dk_halo[1:], jnp.zeros_like(dk_halo[:1])], axis=0)[:, None]
    dk = dk.reshape(steps, -1, BLK, 128)
    return jnp.concatenate([dk[:, :-1], dk[:, -1:] + nxt], axis=1).reshape(-1, 128)


def _sgu_norm(av, g_ln, b_ln):
    t, th = _gelu(av)
    mu = jnp.mean(t, axis=-1, keepdims=True)
    tc = t - mu
    rstd = lax.rsqrt(jnp.mean(tc * tc, axis=-1, keepdims=True) + EPS)
    vhat = tc * rstd
    return vhat * g_ln + b_ln, vhat, rstd, th


def _masked_ws(ws_ref):
    tril = lax.broadcasted_iota(jnp.int32, (BLK, BLK), 0) >= lax.broadcasted_iota(jnp.int32, (BLK, BLK), 1)
    return [jnp.where(tril, ws_ref[g], 0.0).astype(BF16) for g in range(N_GRP)]


def _mix_fwd(au, av, gta, gtb, ob, h, mods, g_ln, b_ln, ws, bsb, wa, wb, wout, seq, plans=()):
    T = h.shape[0]
    tm = _token_tile(seq)
    tps = seq // tm

    def body(au_ref, av_ref, gta_ref, gtb_ref, ob_ref, h_ref, m_ref, gl_ref, bl_ref, ws_ref, bs_ref,
             wa_ref, wb_ref, wo_ref, out_ref, vvb_s, z_s):
        u, _ = _gelu(au_ref[...])
        vv, _, _, _ = _sgu_norm(av_ref[...], gl_ref[...], bl_ref[...])
        vvb_s[...] = vv.astype(BF16)
        wsm = _masked_ws(ws_ref)
        for c in range(tm // BLK):
            rows = slice(c * BLK, (c + 1) * BLK)
            for g in range(N_GRP):
                cols = slice(g * BLK, (g + 1) * BLK)
                z_s[rows, cols] = _dot(wsm[g], vvb_s[rows, cols]) + bs_ref[g]
        ya = _dot_nt((u * z_s[...]).astype(BF16), wa_ref[...])
        yb = _dot_nt(ob_ref[...].astype(BF16), wb_ref[...])
        merged = jax.nn.sigmoid(gta_ref[...]) * ya + jax.nn.sigmoid(gtb_ref[...]) * yb
        out_ref[...] = h_ref[...] + m_ref[0, 5:6, :] * _dot(merged.astype(BF16), wo_ref[...])

    def tok(w):
        return pl.BlockSpec((tm, w), lambda i: (i, 0))

    def full(shape):
        return pl.BlockSpec(shape, lambda i: (0,) * len(shape))

    (out,), plan_outs = _call(
        body, name="mix_fwd", grid=(T // tm,),
        in_specs=[tok(D_A), tok(D_A), tok(D), tok(D), tok(D_B), tok(D),
                  pl.BlockSpec((1, 9, D), lambda i: (i // tps, 0, 0)), full((1, D_A)), full((1, D_A)),
                  full((N_GRP, BLK, BLK)), full((N_GRP, BLK, BLK)), _vmem(), _vmem(), _vmem()],
        out_specs=[tok(D)], out_shape=[jax.ShapeDtypeStruct((T, D), F32)],
        scratch_shapes=[pltpu.VMEM((tm, D_A), BF16), pltpu.VMEM((tm, D_A), F32)],
        args=(au, av, gta, gtb, ob, h, mods, g_ln, b_ln, ws, bsb, wa, wb, wout), semantics=("parallel",),
        plans=plans)
    return out, plan_outs


def _mix_bwd(au, av, gta, gtb, ob, dh, mods, g_ln, b_ln, ws, bsb, wa, wb, wout, seq, plans=()):
    T = dh.shape[0]
    B = T // seq
    tm = _token_tile(seq, 256)
    tps = seq // tm

    def body(au_ref, av_ref, gta_ref, gtb_ref, ob_ref, dh_ref, m_ref, gl_ref, bl_ref, ws_ref, bs_ref,
             wa_ref, wb_ref, wo_ref,
             dau_ref, dav_ref, dgta_ref, dgtb_ref, dob_ref, dwo_ref, dwa_ref, dwb_ref, dws_ref, dbs_ref, dln_ref,
             mg_ref, vvb_s, z_s, dz_s, dzb_s, dvv_s):
        i = pl.program_id(0)

        @pl.when(i == 0)
        def _():
            for r in (dwo_ref, dwa_ref, dwb_ref, dws_ref, dbs_ref, dln_ref):
                r[...] = jnp.zeros(r.shape, F32)

        @pl.when(i % tps == 0)
        def _():
            mg_ref[...] = jnp.zeros(mg_ref.shape, F32)

        auv = au_ref[...]
        avv = av_ref[...]
        u, thu = _gelu(auv)
        g_ln = gl_ref[...]
        vv, vhat, rstd, thv = _sgu_norm(avv, g_ln, bl_ref[...])
        vvb_s[...] = vv.astype(BF16)
        wsm = _masked_ws(ws_ref)
        for c in range(tm // BLK):
            rows = slice(c * BLK, (c + 1) * BLK)
            for g in range(N_GRP):
                cols = slice(g * BLK, (g + 1) * BLK)
                z_s[rows, cols] = _dot(wsm[g], vvb_s[rows, cols]) + bs_ref[g]
        z = z_s[...]
        yab = (u * z).astype(BF16)
        obb = ob_ref[...].astype(BF16)
        ya = _dot_nt(yab, wa_ref[...])
        yb = _dot_nt(obb, wb_ref[...])
        sa = jax.nn.sigmoid(gta_ref[...])
        sb = jax.nn.sigmoid(gtb_ref[...])
        mb = (sa * ya + sb * yb).astype(BF16)
        dhv = dh_ref[...]
        mg_ref[0, 0:1, :] += jnp.sum(dhv * _dot(mb, wo_ref[...]), axis=0, keepdims=True)
        dmob = (m_ref[0, 5:6, :] * dhv).astype(BF16)
        dwo_ref[...] += _dot_tn(mb, dmob)
        dmerged = _dot_nt(dmob, wo_ref[...])
        dya = dmerged * sa
        dyb = dmerged * sb
        dgta_ref[...] = dya * ya * (1.0 - sa)
        dgtb_ref[...] = dyb * yb * (1.0 - sb)
        dyab = dya.astype(BF16)
        dybb = dyb.astype(BF16)
        dwa_ref[...] += _dot_tn(dyab, yab)
        dwb_ref[...] += _dot_tn(dybb, obb)
        dob_ref[...] = _dot(dybb, wb_ref[...])
        dyap = _dot(dyab, wa_ref[...])
        dau_ref[...] = (dyap * z) * _gelu_grad(auv, thu)
        dz = dyap * u
        dz_s[...] = dz
        dzb_s[...] = dz.astype(BF16)
        for c in range(tm // BLK):
            rows = slice(c * BLK, (c + 1) * BLK)
            for g in range(N_GRP):
                cols = slice(g * BLK, (g + 1) * BLK)
                dzb = dzb_s[rows, cols]
                dvv_s[rows, cols] = _dot_tn(wsm[g], dzb)
                dws_ref[g] += _dot_nt(dzb, vvb_s[rows, cols])
                dbs_ref[g] += dz_s[rows, cols]
        dvv = dvv_s[...]
        dln_ref[0:1, :] += jnp.sum(dvv * vhat, axis=0, keepdims=True)
        dln_ref[1:2, :] += jnp.sum(dvv, axis=0, keepdims=True)
        dvh = dvv * g_ln
        dt = rstd * (dvh - jnp.mean(dvh, axis=-1, keepdims=True)
                     - vhat * jnp.mean(dvh * vhat, axis=-1, keepdims=True))
        dav_ref[...] = dt * _gelu_grad(avv, thv)

    def tok(w):
        return pl.BlockSpec((tm, w), lambda i: (i, 0))

    def full(shape):
        return pl.BlockSpec(shape, lambda i: (0,) * len(shape))

    return _call(
        body, name="mix_bwd", grid=(T // tm,),
        in_specs=[tok(D_A), tok(D_A), tok(D), tok(D), tok(D_B), tok(D),
                  pl.BlockSpec((1, 9, D), lambda i: (i // tps, 0, 0)), full((1, D_A)), full((1, D_A)),
                  full((N_GRP, BLK, BLK)), full((N_GRP, BLK, BLK)), _vmem(), _vmem(), _vmem()],
        out_specs=[tok(D_A), tok(D_A), tok(D), tok(D), tok(D_B), full((D, D)), full((D, D_A)), full((D, D_B)),
                   full((N_GRP, BLK, BLK)), full((N_GRP, BLK, BLK)), full((8, D_A)),
                   pl.BlockSpec((1, 8, D), lambda i: (i // tps, 0, 0))],
        out_shape=[jax.ShapeDtypeStruct((T, D_A), F32), jax.ShapeDtypeStruct((T, D_A), F32),
                   jax.ShapeDtypeStruct((T, D), F32), jax.ShapeDtypeStruct((T, D), F32),
                   jax.ShapeDtypeStruct((T, D_B), F32), jax.ShapeDtypeStruct((D, D), F32),
                   jax.ShapeDtypeStruct((D, D_A), F32), jax.ShapeDtypeStruct((D, D_B), F32),
                   jax.ShapeDtypeStruct((N_GRP, BLK, BLK), F32), jax.ShapeDtypeStruct((N_GRP, BLK, BLK), F32),
                   jax.ShapeDtypeStruct((8, D_A), F32), jax.ShapeDtypeStruct((B, 8, D), F32)],
        scratch_shapes=[pltpu.VMEM((tm, D_A), BF16), pltpu.VMEM((tm, D_A), F32), pltpu.VMEM((tm, D_A), F32),
                        pltpu.VMEM((tm, D_A), BF16), pltpu.VMEM((tm, D_A), F32)],
        args=(au, av, gta, gtb, ob, dh, mods, g_ln, b_ln, ws, bsb, wa, wb, wout), semantics=("arbitrary",),
        plans=plans)


def _prologue(c_pad, w_ada, first_shards):
    cols = w_ada.shape[1]
    plan_w, plan_c = _GatherRelayed(first_shards), _GatherDirect([c_pad])
    plan_m = _Gather([jax.ShapeDtypeStruct((N_DEV * 8, cols), F32)])
    n_w = len(first_shards)

    def body(c_ref, wada_ref, *rest):
        w_ins, rest = rest[:n_w], rest[n_w:]
        call_ref, mods_ref = rest[:2]
        w_outs, rest = rest[2:2 + n_w], rest[2 + n_w:]
        cvm, part, local_sem = rest[:3]
        sems = rest[3:]
        sems_w, sems_c, sems_m = sems[:3], sems[3:6], sems[6:9]
        plan_c.start([c_ref], [call_ref], sems_c)
        plan_c.relay([c_ref], [call_ref], sems_c)
        plan_c.finish([c_ref], [call_ref], sems_c)
        load = pltpu.make_async_copy(call_ref, cvm, local_sem)
        load.start()
        load.wait()
        cv = cvm[...].reshape(N_DEV * 8, D)
        part[...] = _dot((cv * jax.nn.sigmoid(cv)).astype(BF16), wada_ref[...].astype(BF16))
        plan_m.start([part], [mods_ref], sems_m)
        plan_w.start(w_ins, w_outs, sems_w)
        plan_m.relay([part], [mods_ref], sems_m)
        plan_m.finish([part], [mods_ref], sems_m)
        plan_w.relay(w_ins, w_outs, sems_w)
        plan_w.finish(w_ins, w_outs, sems_w)

    res = pl.pallas_call(
        body, name="prologue", in_specs=[_any(), _vmem()] + [_any()] * n_w,
        out_specs=[_any()] * (2 + n_w), out_shape=plan_c.out_shape + plan_m.out_shape + plan_w.out_shape,
        scratch_shapes=[pltpu.VMEM((N_DEV, 8, D), F32), pltpu.VMEM((N_DEV * 8, cols), F32), pltpu.SemaphoreType.DMA]
        + plan_w.scratch + plan_c.scratch + plan_m.scratch,
        compiler_params=pltpu.CompilerParams(vmem_limit_bytes=VMEM_LIMIT),
    )(c_pad, w_ada, *first_shards)
    return res[0], res[1], list(res[2:])


def _ada_update(c_all, dm_cols, w, m, v):
    n, cols = c_all.shape[0], w.shape[1]
    br = 256

    def body(c_ref, dm_ref, w_ref, m_ref, v_ref, g_ref, d_ref, nm_ref, nv_ref):
        cv = c_ref[...]
        g = _dot_tn((cv * jax.nn.sigmoid(cv)).astype(BF16), dm_ref[...].astype(BF16))
        g_ref[...] = g
        d_ref[...], nm_ref[...], nv_ref[...] = _adamw(w_ref[...], g, m_ref[...], v_ref[...])

    rows = pl.BlockSpec((br, cols), lambda i: (i, 0))
    outs, _ = _call(
        body, name="ada_update", grid=(D // br,),
        in_specs=[pl.BlockSpec((n, br), lambda i: (0, i)), pl.BlockSpec((n, cols), lambda i: (0, 0)), rows, rows, rows],
        out_specs=[rows] * 4, out_shape=[jax.ShapeDtypeStruct(w.shape, F32)] * 4, args=(c_all, dm_cols, w, m, v),
        semantics=("parallel",))
    return outs


def _mod_finish(mg1, mg2, mg2g, mg3, mods, g1, g2, g3):
    B = mods.shape[0]

    def body(mg1_ref, mg2_ref, mg2g_ref, mg3_ref, m_ref, g1_ref, g2_ref, g3_ref, dm_ref, dgn_ref):
        dgn_ref[...] = jnp.zeros(dgn_ref.shape, F32)
        for k, (mg, g_ref) in enumerate(((mg1_ref, g1_ref), (mg2_ref, g2_ref), (mg3_ref, g3_ref))):
            for b in range(B):
                s_dxr = mg[b, 1:2, :]
                dm_ref[b, 3 * k:3 * k + 1, :] = mg[b, 0:1, :]
                dm_ref[b, 3 * k + 1:3 * k + 2, :] = g_ref[...] * s_dxr
                dm_ref[b, 3 * k + 2:3 * k + 3, :] = mg2g_ref[b, 0:1, :] if k == 1 else mg[b, 2:3, :]
                dgn_ref[k:k + 1, :] += (1.0 + m_ref[b, 3 * k + 1:3 * k + 2, :]) * s_dxr

    args = (mg1, mg2, mg2g, mg3, mods, g1, g2, g3)
    out_shape = [jax.ShapeDtypeStruct((B, 9, D), F32), jax.ShapeDtypeStruct((8, D), F32)]
    return pl.pallas_call(body, name="mod_finish", grid=(1,), in_specs=[_whole(a) for a in args],
                          out_specs=[_whole(o) for o in out_shape], out_shape=out_shape,
                          compiler_params=_params("arbitrary"))(*args)


def _small_update(gathered, params, ms, vs):
    n = len(SMALL)
    B = gathered[0].shape[1]

    def body(*refs):
        gdm, ggn, gln, gws, gbs, ggq, ggk, gsk, gloss = refs[:9]
        w = dict(zip(SMALL, refs[9:9 + n]))
        m = dict(zip(SMALL, refs[9 + n:9 + 2 * n]))
        v = dict(zip(SMALL, refs[9 + 2 * n:9 + 3 * n]))
        outs = refs[9 + 3 * n:-1]
        out = {name: outs[4 * k:4 * k + 4] for k, name in enumerate(SMALL)}

        def total(ref, idx):
            acc = ref[(0,) + idx]
            for dev in range(1, N_DEV):
                acc = acc + ref[(dev,) + idx]
            return acc

        def finish(name, g, idx=(Ellipsis,)):
            d, nm, nv = _adamw(w[name][idx], g, m[name][idx], v[name][idx])
            for ref, val in zip(out[name], (g, d, nm, nv)):
                ref[idx] = val

        g_bada = total(gdm, (slice(0, 1),))
        for b in range(1, B):
            g_bada = g_bada + total(gdm, (slice(b, b + 1),))
        finish("b_ada", g_bada)
        finish("g_norm1", total(ggn, (slice(0, 1),)))
        finish("g_norm2", total(ggn, (slice(1, 2),)))
        finish("g_norm3", total(ggn, (slice(2, 3),)))
        finish("g_sgu_ln", total(gln, (slice(0, 1),)))
        finish("b_sgu_ln", total(gln, (slice(1, 2),)))
        tril = lax.broadcasted_iota(jnp.int32, (BLK, BLK), 0) >= lax.broadcasted_iota(jnp.int32, (BLK, BLK), 1)
        for g in range(N_GRP):
            finish("w_spatial", jnp.where(tril, total(gws, (g,)), 0.0), (0, g))
            finish("b_spatial", jnp.sum(total(gbs, (g,)).T, axis=0, keepdims=True), (0, slice(g, g + 1)))
        finish("g_q", total(ggq, ()))
        finish("g_k", total(ggk, ()))
        finish("attn_sinks", total(gsk, (slice(0, 1), slice(0, N_KV * Q_PER_KV))))
        refs[-1][...] = total(gloss, ())

    args = list(gathered) + [params[k] for k in SMALL] + [ms[k] for k in SMALL] + [vs[k] for k in SMALL]
    out_shape = []
    for k in SMALL:
        out_shape += [jax.ShapeDtypeStruct(params[k].shape, F32)] * 4
    out_shape.append(jax.ShapeDtypeStruct((8, 128), F32))
    res = pl.pallas_call(body, name="small_update", grid=(1,), in_specs=[_whole(a) for a in args],
                         out_specs=[_whole(o) for o in out_shape], out_shape=out_shape,
                         compiler_params=_params("arbitrary"))(*args)
    return {k: res[4 * i:4 * i + 4] for i, k in enumerate(SMALL)}, res[-1][0, 0]


def _rs_add(p, r, core, name):
    _, rows, width = p.shape

    def body(c_ref, p_ref, r_ref, o_ref):
        o_ref[...] = (p_ref[...] + r_ref[...]).astype(BF16)

    return pl.pallas_call(
        body, name="rs_add_" + name, out_shape=jax.ShapeDtypeStruct((4, rows, width), BF16),
        grid_spec=pltpu.PrefetchScalarGridSpec(
            num_scalar_prefetch=1, grid=(4,),
            in_specs=[pl.BlockSpec((1, rows, width), lambda k, c_ref: (2 * k + c_ref[0], 0, 0)),
                      pl.BlockSpec((1, rows, width), lambda k, c_ref: (k, 0, 0))],
            out_specs=pl.BlockSpec((1, rows, width), lambda k, c_ref: (k, 0, 0))),
        compiler_params=_params("parallel"),
    )(core, p, r)


def _adam_big(w, m, v, r, name):
    rows, cols = w.shape

    def body(r_ref, w_ref, m_ref, v_ref, go_ref, d_ref, nm_ref, nv_ref):
        g = r_ref[0].astype(F32)
        for k in range(1, 4):
            g = g + r_ref[k].astype(F32)
        go_ref[...] = g
        d_ref[...], nm_ref[...], nv_ref[...] = _adamw(w_ref[...], g, m_ref[...], v_ref[...])

    br = rows // 2
    shard = pl.BlockSpec((br, cols), lambda i: (i, 0))
    return pl.pallas_call(body, name="adam_" + name, grid=(2,),
                          in_specs=[pl.BlockSpec((4, br, cols), lambda i: (0, i, 0)), shard, shard, shard],
                          out_specs=[shard] * 4, out_shape=[jax.ShapeDtypeStruct(w.shape, F32)] * 4,
                          compiler_params=_params("parallel"))(r, w, m, v)


def kernel(x, c, w_ada, b_ada, g_norm1, ffn1_w_gate, ffn1_w_up, ffn1_w_down, g_norm2, w_in, g_sgu_ln, b_sgu_ln, w_spatial, b_spatial, g_q, g_k, attn_sinks, w_branch_a, w_branch_b, w_out, g_norm3, ffn2_w_gate, ffn2_w_up, ffn2_w_down, loss_target, m_w_ada, m_b_ada, m_g_norm1, m_ffn1_w_gate, m_ffn1_w_up, m_ffn1_w_down, m_g_norm2, m_w_in, m_g_sgu_ln, m_b_sgu_ln, m_w_spatial, m_b_spatial, m_g_q, m_g_k, m_attn_sinks, m_w_branch_a, m_w_branch_b, m_w_out, m_g_norm3, m_ffn2_w_gate, m_ffn2_w_up, m_ffn2_w_down, v_w_ada, v_b_ada, v_g_norm1, v_ffn1_w_gate, v_ffn1_w_up, v_ffn1_w_down, v_g_norm2, v_w_in, v_g_sgu_ln, v_b_sgu_ln, v_w_spatial, v_b_spatial, v_g_q, v_g_k, v_attn_sinks, v_w_branch_a, v_w_branch_b, v_w_out, v_g_norm3, v_ffn2_w_gate, v_ffn2_w_up, v_ffn2_w_down):
    names = ("w_ada", "b_ada", "g_norm1", "ffn1_w_gate", "ffn1_w_up", "ffn1_w_down", "g_norm2", "w_in", "g_sgu_ln",
             "b_sgu_ln", "w_spatial", "b_spatial", "g_q", "g_k", "attn_sinks", "w_branch_a", "w_branch_b", "w_out",
             "g_norm3", "ffn2_w_gate", "ffn2_w_up", "ffn2_w_down")
    w = dict(zip(names, (w_ada, b_ada, g_norm1, ffn1_w_gate, ffn1_w_up, ffn1_w_down, g_norm2, w_in, g_sgu_ln,
                         b_sgu_ln, w_spatial, b_spatial, g_q, g_k, attn_sinks, w_branch_a, w_branch_b, w_out, g_norm3,
                         ffn2_w_gate, ffn2_w_up, ffn2_w_down)))
    m = dict(zip(names, (m_w_ada, m_b_ada, m_g_norm1, m_ffn1_w_gate, m_ffn1_w_up, m_ffn1_w_down, m_g_norm2, m_w_in,
                         m_g_sgu_ln, m_b_sgu_ln, m_w_spatial, m_b_spatial, m_g_q, m_g_k, m_attn_sinks, m_w_branch_a,
                         m_w_branch_b, m_w_out, m_g_norm3, m_ffn2_w_gate, m_ffn2_w_up, m_ffn2_w_down)))
    v = dict(zip(names, (v_w_ada, v_b_ada, v_g_norm1, v_ffn1_w_gate, v_ffn1_w_up, v_ffn1_w_down, v_g_norm2, v_w_in,
                         v_g_sgu_ln, v_b_sgu_ln, v_w_spatial, v_b_spatial, v_g_q, v_g_k, v_attn_sinks, v_w_branch_a,
                         v_w_branch_b, v_w_out, v_g_norm3, v_ffn2_w_gate, v_ffn2_w_up, v_ffn2_w_down)))
    B, seq, _ = x.shape
    T = B * seq
    nbs = seq // BLK
    xi, yi, ci = _place()
    me = 4 * xi + 2 * yi + ci
    core = jnp.reshape(ci, (1,)).astype(jnp.int32)
    layout = {name: (tform, n, width) for name, tform, n, width in BIG}
    shard = {name: w[name][0].astype(BF16).T if tform else w[name][0].astype(BF16) for name, tform, _, _ in BIG}
    wts, parts, big_out = {}, {}, {}

    def gather_plan(group):
        return _Gather([shard[k] for k in group])

    def take(group, gathered):
        for k, g in zip(group, gathered):
            wts[k] = g.reshape(N_DEV * layout[k][1], layout[k][2])

    def blocks(name, grad):
        return grad.reshape(N_DEV, layout[name][1], layout[name][2])

    def to_sibling(names, grads):
        return _RsSibling([blocks(k, g) for k, g in zip(names, grads)])

    def add(names, grads, from_sibling):
        for k, g, r in zip(names, grads, from_sibling):
            parts[k] = _rs_add(blocks(k, g), r, core, k)

    def to_chips(names):
        return _RsChips([parts[k] for k in names])

    def update(names, from_chips):
        for k, r in zip(names, from_chips):
            if layout[k][0]:
                big_out[k] = [o.T[None] for o in _adam_big(w[k][0].T, m[k][0].T, v[k][0].T, r, k)]
            else:
                big_out[k] = [o[None] for o in _adam_big(w[k][0], m[k][0], v[k][0], r, k)]

    g1, g2, g3, g_ln, b_ln = g_norm1, g_norm2, g_norm3, g_sgu_ln, b_sgu_ln
    gq2, gk2 = jnp.tile(g_q, (1, 2)), jnp.tile(g_k, (1, 2))
    ws = w_spatial[0]
    bsb = jnp.broadcast_to(b_spatial[0][:, :, None], (N_GRP, BLK, BLK))
    xf = x.reshape(T, D)
    tgt = loss_target.reshape(T, D)

    ffn1 = ("ffn1_w_gate", "ffn1_w_up", "ffn1_w_down")
    ffn2 = ("ffn2_w_gate", "ffn2_w_up", "ffn2_w_down")
    c_pad = jnp.concatenate([c, jnp.zeros((8 - B, D), F32)], axis=0)
    c_all, mods_cols, gathered = _prologue(c_pad, w_ada[0], [shard[k] for k in ffn1[:2]])
    take(ffn1[:2], gathered)
    c_all = c_all[:, :B].reshape(N_DEV * B, D)
    mine = lax.dynamic_slice_in_dim(mods_cols, 8 * me, B, axis=1)
    mods = (mine.transpose(1, 0, 2).reshape(B, 9 * D) + b_ada).reshape(B, 9, D)

    group = ("ffn1_w_down", "w_branch_a", "w_branch_b", "w_out")
    (gb1, ub1), (gathered,) = _ffn_gate_up(xf, mods, g1, wts["ffn1_w_gate"], wts["ffn1_w_up"], 0, seq,
                                           plans=[gather_plan(group)])
    take(group, gathered)
    group = ("w_in",)
    (h1, y1), (gathered,) = _ffn_down(xf, gb1, ub1, mods, wts["ffn1_w_down"], 0, seq, plans=[gather_plan(group)])
    take(group, gathered)
    group = ("ffn2_w_gate",)
    (au, av, q_tok, k_tok, v_tok, gta, gtb), (gathered,) = _inproj_fwd(h1, mods, g2, wts["w_in"], seq,
                                                                       plans=[gather_plan(group)])
    take(group, gathered)
    group = ("ffn2_w_up",)
    ob, (gathered,) = _swa_fwd(q_tok, k_tok, v_tok, gq2, gk2, attn_sinks, nbs, plans=[gather_plan(group)])
    take(group, gathered)
    mixw = (wts["w_branch_a"], wts["w_branch_b"], wts["w_out"])
    group = ("ffn2_w_down",)
    h2, (gathered,) = _mix_fwd(au, av, gta, gtb, ob, h1, mods, g_ln, b_ln, ws, bsb, *mixw, seq,
                               plans=[gather_plan(group)])
    take(group, gathered)
    dh3, y3, gb3, ub3, loss_parts = _ffn_fwd_loss(h2, mods, g3, *[wts[k] for k in ffn2], 6, seq, tgt)
    loss_part = jnp.full((8, 128), jnp.sum(loss_parts[:, 0, 0]))

    (dh2, xb, dyb, a, dg, du, mg3), _ = _ffn_bwd(h2, dh3, y3, gb3, ub3, mods, g3, *[wts[k] for k in ffn2], 6, seq)
    d_gate, _ = _wgrad(dg, xb, "wgrad_ffn2_gate")
    d_up, (r,) = _wgrad(du, xb, "wgrad_ffn2_up", plans=[to_sibling(ffn2[:1], [d_gate])])
    add(ffn2[:1], [d_gate], r)
    d_down, (r,) = _wgrad(a, dyb, "wgrad_ffn2_down", plans=[to_sibling(ffn2[1:2], [d_up])])
    add(ffn2[1:2], [d_up], r)

    (dau, dav, dgta, dgtb, dob, d_out, d_a, d_b, dws, dbs, dln, mg2g), (r, from_chips) = _mix_bwd(
        au, av, gta, gtb, ob, dh2, mods, g_ln, b_ln, ws, bsb, *mixw, seq,
        plans=[to_sibling(ffn2[2:], [d_down]), to_chips(ffn2[:2])])
    add(ffn2[2:], [d_down], r)
    update(ffn2[:2], from_chips)
    mixers = ("w_out", "w_branch_a", "w_branch_b")
    (dq, dk, dv, dk_halo, dv_halo, dgq2, dgk2, dsk), (from_chips, r, early) = _swa_bwd(
        q_tok, k_tok, v_tok, gq2, gk2, attn_sinks, dob, nbs,
        plans=[to_chips(ffn2[2:]), to_sibling(mixers, [d_out, d_a, d_b]), _Gather([dln, dws, dbs, loss_part])])
    update(ffn2[2:], from_chips)
    add(mixers, [d_out, d_a, d_b], r)
    dk, dv = _swa_add_halo(dk, dk_halo), _swa_add_halo(dv, dv_halo)
    dgq, dgk = dgq2[:, :HD] + dgq2[:, HD:], dgk2[:, :HD] + dgk2[:, HD:]
    (dh1, xb2, dpb, mg2), (from_chips,) = _inproj_bwd(
        h1, dh2, (dau, dav, dq, dk, dv, dgta, dgtb), mods, g2, wts["w_in"], seq, plans=[to_chips(mixers)])
    update(mixers, from_chips)

    d_in, _ = _wgrad(dpb, xb2, "wgrad_w_in")

    (dx, xb, dyb, a, dg, du, mg1), _ = _ffn_bwd(xf, dh1, y1, gb1, ub1, mods, g1, *[wts[k] for k in ffn1], 0, seq)
    dmods, dgn = _mod_finish(mg1, mg2, mg2g, mg3, mods, g1, g2, g3)
    d_gate, (r, late) = _wgrad(dg, xb, "wgrad_ffn1_gate",
                               plans=[to_sibling(("w_in",), [d_in]),
                                      _GatherDirect([dmods.reshape(B, 9 * D), dgn, dgq, dgk, dsk])])
    add(("w_in",), [d_in], r)
    gathered = late[:2] + early[:3] + late[2:] + early[3:]
    d_up, (r, from_chips) = _wgrad(du, xb, "wgrad_ffn1_up",
                                   plans=[to_sibling(ffn1[:1], [d_gate]), to_chips(("w_in",))])
    add(ffn1[:1], [d_gate], r)
    update(("w_in",), from_chips)
    d_down, (r, from_chips) = _wgrad(a, dyb, "wgrad_ffn1_down",
                                     plans=[to_sibling(ffn1[1:2], [d_up]), to_chips(ffn1[:1])])
    add(ffn1[1:2], [d_up], r)
    update(ffn1[:1], from_chips)
    small_out, loss = _small_update(gathered, w, m, v)
    dm_cols = lax.dynamic_slice_in_dim(gathered[0].reshape(N_DEV * B, 9 * D), (9 * D // N_DEV) * me,
                                       9 * D // N_DEV, axis=1)
    ada_out = _ada_update(c_all, dm_cols, w_ada[0], m_w_ada[0], v_w_ada[0])
    r, from_chips = _exchange([to_sibling(ffn1[2:], [d_down]), to_chips(ffn1[1:2])], "rs_tail")
    add(ffn1[2:], [d_down], r)
    update(ffn1[1:2], from_chips)
    (from_chips,) = _exchange([to_chips(ffn1[2:])], "rs_last")
    update(ffn1[2:], from_chips)

    def leaf(kind, name):
        if name == "w_ada":
            return ada_out[kind][None]
        if name in SMALL:
            return small_out[name][kind]
        return big_out[name][kind]

    return (loss, dx.reshape(B, seq, D), *[leaf(kind, name) for kind in range(4) for name in names])
```

```python
import math

import jax
import jax.numpy as jnp
from jax import lax
from jax.experimental import pallas as pl
from jax.experimental.pallas import tpu as pltpu

F32 = jnp.float32
BF16 = jnp.bfloat16
MESH = pl.DeviceIdType.MESH
N_DEV = 8

VMEM_LIMIT = 56 * 1024 * 1024

D = 1024
FF = 2816
FC = 2816
D_A = 512
D_B = 512
HD = 64
N_KV = 2
Q_PER_KV = 4
BLK = 128
N_GRP = 4
IN_COLS = 3840
PIECES = (("au", 0, 512), ("av", 512, 512), ("q", 1024, 512), ("k", 1536, 128), ("v", 1664, 128),
          ("ga", 1792, 1024), ("gb", 2816, 1024))
EPS = 1e-6
NEG = -1e30
GELU_C = math.sqrt(2.0 / math.pi)

ADAM_LR = 0.001
ADAM_B1 = 0.9
ADAM_B2 = 0.999
ADAM_EPS = 1e-08
ADAM_WD = 0.01
ADAM_STEP = 10

NT = (((1,), (1,)), ((), ()))
TN = (((0,), (0,)), ((), ()))

BIG = (("ffn1_w_gate", True, FF // N_DEV, D), ("ffn1_w_up", True, FF // N_DEV, D),
       ("ffn1_w_down", False, FF // N_DEV, D), ("w_in", True, IN_COLS // N_DEV, D),
       ("w_branch_a", True, D // N_DEV, D_A), ("w_branch_b", True, D // N_DEV, D_B), ("w_out", False, D // N_DEV, D),
       ("ffn2_w_gate", True, FF // N_DEV, D), ("ffn2_w_up", True, FF // N_DEV, D),
       ("ffn2_w_down", False, FF // N_DEV, D))
SMALL = ("b_ada", "g_norm1", "g_norm2", "g_sgu_ln", "b_sgu_ln", "w_spatial", "b_spatial", "g_q", "g_k",
         "attn_sinks", "g_norm3")


def _dot(a, b):
    return jnp.dot(a, b, preferred_element_type=F32)


def _dot_nt(a, b):
    return lax.dot_general(a, b, NT, preferred_element_type=F32)


def _dot_tn(a, b):
    return lax.dot_general(a, b, TN, preferred_element_type=F32)


def _vmem():
    return pl.BlockSpec(memory_space=pltpu.VMEM)


def _any():
    return pl.BlockSpec(memory_space=pl.ANY)


def _whole(a):
    return pl.BlockSpec(a.shape, lambda i: (0,) * len(a.shape))


def _rms_mod(h, g, sh, sc):
    inv = lax.rsqrt(jnp.mean(h * h, axis=-1, keepdims=True) + EPS)
    r = h * inv
    return (r * g) * (1.0 + sc) + sh, r, inv


def _rms_mod_bwd(dxn, r, inv, g, sc):
    dr = dxn * (g * (1.0 + sc))
    dh = inv * (dr - r * jnp.mean(dr * r, axis=-1, keepdims=True))
    return dh, jnp.sum(dxn, axis=0, keepdims=True), jnp.sum(dxn * r, axis=0, keepdims=True)


def _gelu(x):
    t = jnp.tanh(GELU_C * (x + 0.044715 * (x * x * x)))
    return 0.5 * x * (1.0 + t), t


def _gelu_grad(x, t):
    return 0.5 * (1.0 + t) + 0.5 * x * (1.0 - t * t) * (GELU_C * (1.0 + 3.0 * 0.044715 * x * x))


def _adamw(w, g, m, v):
    m = ADAM_B1 * m + (1.0 - ADAM_B1) * g
    v = ADAM_B2 * v + (1.0 - ADAM_B2) * (g * g)
    m_hat = m / (1.0 - ADAM_B1 ** ADAM_STEP)
    v_hat = v / (1.0 - ADAM_B2 ** ADAM_STEP)
    delta = -ADAM_LR * (m_hat / (jnp.sqrt(v_hat) + ADAM_EPS) + ADAM_WD * w)
    return delta, m, v


def _token_tile(seq, cap=512):
    return min(cap, seq)


def _params(*semantics):
    return pltpu.CompilerParams(dimension_semantics=semantics, vmem_limit_bytes=VMEM_LIMIT)


def _place():
    return lax.axis_index("x"), lax.axis_index("y"), lax.axis_index("c")


class _Gather:
    def __init__(self, arrays):
        n = len(arrays)
        self.ins = list(arrays)
        self.out_shape = [jax.ShapeDtypeStruct((N_DEV,) + a.shape, a.dtype) for a in arrays]
        self.scratch = [pltpu.SemaphoreType.DMA((n, 7)), pltpu.SemaphoreType.DMA((n, 7)),
                        pltpu.SemaphoreType.DMA((n,))]

    def _copies(self, ins, outs, sems):
        send_sems, recv_sems, local_sems = sems
        n = len(ins)
        x, y, c = _place()
        me, sibling = (x, y, c), (x, y, 1 - c)
        chips = [(1 - x, y), (x, 1 - y), (1 - x, 1 - y)]

        def slot(a, px, py, pc):
            return outs[a].at[4 * px + 2 * py + pc]

        def copy(a, k, block, to, src=None):
            return pltpu.make_async_remote_copy(
                src_ref=slot(a, *block) if src is None else src, dst_ref=slot(a, *block),
                send_sem=send_sems.at[a, k], recv_sem=recv_sems.at[a, k], device_id=to, device_id_type=MESH)

        mine = [pltpu.make_async_copy(ins[a], slot(a, *me), local_sems.at[a]) for a in range(n)]
        first = [copy(a, 0, me, sibling, src=ins[a]) for a in range(n)]
        first += [copy(a, 1 + j, me, (*chip, c), src=ins[a]) for a in range(n) for j, chip in enumerate(chips)]
        landed = [[copy(a, 1 + j, (*chip, c), me) for a in range(n)] for j, chip in enumerate(chips)]
        passed = [[copy(a, 4 + j, (*chip, c), sibling) for a in range(n)] for j, chip in enumerate(chips)]
        from_sibling = [copy(a, 0, sibling, me) for a in range(n)]
        from_sibling += [copy(a, 4 + j, (*chip, 1 - c), me) for a in range(n) for j, chip in enumerate(chips)]
        return mine, first, landed, passed, from_sibling

    def start(self, ins, outs, sems):
        mine, first, _, _, _ = self._copies(ins, outs, sems)
        for cp in mine + first:
            cp.start()

    def relay(self, ins, outs, sems):
        _, _, landed, passed, _ = self._copies(ins, outs, sems)
        for arrivals, forwards in zip(landed, passed):
            for arrival, forward in zip(arrivals, forwards):
                arrival.wait_recv()
                forward.start()

    def finish(self, ins, outs, sems):
        mine, first, _, passed, from_sibling = self._copies(ins, outs, sems)
        for cp in from_sibling:
            cp.wait_recv()
        for cp in first + [f for fs in passed for f in fs]:
            cp.wait_send()
        for cp in mine:
            cp.wait()


class _GatherRelayed:
    def __init__(self, arrays):
        n = len(arrays)
        self.ins = list(arrays)
        self.out_shape = [jax.ShapeDtypeStruct((N_DEV,) + a.shape, a.dtype) for a in arrays]
        self.scratch = [pltpu.SemaphoreType.DMA((n, 7)), pltpu.SemaphoreType.DMA((n, 7)),
                        pltpu.SemaphoreType.DMA((n,))]

    def _copies(self, ins, outs, sems):
        send_sems, recv_sems, local_sems = sems
        n = len(ins)
        x, y, c = _place()
        me, sibling = (x, y, c), (x, y, 1 - c)
        south = c == 0
        via = (jnp.where(south, 1 - x, x), jnp.where(south, y, 1 - y))
        onward = (jnp.where(south, x, 1 - x), jnp.where(south, 1 - y, y))
        via_sem = jnp.where(south, 1, 2)

        def slot(a, px, py, pc):
            return outs[a].at[4 * px + 2 * py + pc]

        def copy(a, k, block, to, src=None):
            return pltpu.make_async_remote_copy(
                src_ref=slot(a, *block) if src is None else src, dst_ref=slot(a, *block),
                send_sem=send_sems.at[a, k], recv_sem=recv_sems.at[a, k], device_id=to, device_id_type=MESH)

        chips = [(1 - x, y), (x, 1 - y), (1 - x, 1 - y)]
        mine = [pltpu.make_async_copy(ins[a], slot(a, *me), local_sems.at[a]) for a in range(n)]
        first = [copy(a, 0, me, sibling, src=ins[a]) for a in range(n)]
        first += [copy(a, 1 + j, me, (*chips[j], c), src=ins[a]) for a in range(n) for j in range(2)]
        to_pass = [copy(a, via_sem, (*via, c), me) for a in range(n)]
        passed_on = [copy(a, 3, (*via, c), (*onward, c)) for a in range(n)]
        landed = [[copy(a, 1 + j, (*chips[j], c), me) for a in range(n)] for j in range(3)]
        to_sibling = [[copy(a, 4 + j, (*chips[j], c), sibling) for a in range(n)] for j in range(3)]
        from_sibling = [copy(a, 0, sibling, me) for a in range(n)]
        from_sibling += [copy(a, 4 + j, (*chips[j], 1 - c), me) for a in range(n) for j in range(3)]
        return mine, first, to_pass, passed_on, landed, to_sibling, from_sibling

    def start(self, ins, outs, sems):
        mine, first, _, _, _, _, _ = self._copies(ins, outs, sems)
        for cp in mine + first:
            cp.start()

    def relay(self, ins, outs, sems):
        _, _, to_pass, passed_on, landed, to_sibling, _ = self._copies(ins, outs, sems)
        for arrival, onward in zip(to_pass, passed_on):
            arrival.wait_recv()
            onward.start()
        x, y, c = _place()
        for j in range(3):
            for a, (arrival, forward) in enumerate(zip(landed[j], to_sibling[j])):
                if j < 2:
                    pl.when((c == 0) != (j == 0))(arrival.wait_recv)
                else:
                    arrival.wait_recv()
                forward.start()

    def finish(self, ins, outs, sems):
        mine, first, _, passed_on, _, to_sibling, from_sibling = self._copies(ins, outs, sems)
        for cp in from_sibling:
            cp.wait_recv()
        for cp in first + passed_on + [f for fs in to_sibling for f in fs]:
            cp.wait_send()
        for cp in mine:
            cp.wait()


class _GatherDirect:
    def __init__(self, arrays):
        n = len(arrays)
        self.ins = list(arrays)
        self.out_shape = [jax.ShapeDtypeStruct((N_DEV,) + a.shape, a.dtype) for a in arrays]
        self.scratch = [pltpu.SemaphoreType.DMA((n, 7)), pltpu.SemaphoreType.DMA((n, 7)),
                        pltpu.SemaphoreType.DMA((n,))]

    def _copies(self, ins, outs, sems):
        send_sems, recv_sems, local_sems = sems
        n = len(ins)
        x, y, c = _place()
        peers = [(x ^ (k >> 2 & 1), y ^ (k >> 1 & 1), c ^ (k & 1)) for k in range(1, N_DEV)]

        def slot(a, px, py, pc):
            return outs[a].at[4 * px + 2 * py + pc]

        def copy(a, k, block, to, src=None):
            return pltpu.make_async_remote_copy(
                src_ref=slot(a, *block) if src is None else src, dst_ref=slot(a, *block),
                send_sem=send_sems.at[a, k], recv_sem=recv_sems.at[a, k], device_id=to, device_id_type=MESH)

        mine = [pltpu.make_async_copy(ins[a], slot(a, x, y, c), local_sems.at[a]) for a in range(n)]
        sends = [copy(a, k, (x, y, c), peer, src=ins[a]) for a in range(n) for k, peer in enumerate(peers)]
        arrivals = [copy(a, k, peer, (x, y, c)) for a in range(n) for k, peer in enumerate(peers)]
        return mine, sends, arrivals

    def start(self, ins, outs, sems):
        mine, sends, _ = self._copies(ins, outs, sems)
        for cp in mine + sends:
            cp.start()

    def relay(self, ins, outs, sems):
        pass

    def finish(self, ins, outs, sems):
        mine, sends, arrivals = self._copies(ins, outs, sems)
        for cp in arrivals:
            cp.wait_recv()
        for cp in sends:
            cp.wait_send()
        for cp in mine:
            cp.wait()


class _RsSibling:
    def __init__(self, ps):
        n = len(ps)
        self.ins = list(ps)
        self.out_shape = [jax.ShapeDtypeStruct((4,) + p.shape[1:], p.dtype) for p in ps]
        self.scratch = [pltpu.SemaphoreType.DMA((n, 4)), pltpu.SemaphoreType.DMA((n, 4))]

    def _copies(self, ins, outs, sems):
        send_sems, recv_sems = sems
        x, y, c = _place()
        return [pltpu.make_async_remote_copy(
            src_ref=ins[a].at[2 * q + (1 - c)], dst_ref=outs[a].at[q], send_sem=send_sems.at[a, q],
            recv_sem=recv_sems.at[a, q], device_id=(x, y, 1 - c), device_id_type=MESH)
            for a in range(len(ins)) for q in range(4)]

    def start(self, ins, outs, sems):
        for cp in self._copies(ins, outs, sems):
            cp.start()

    def relay(self, ins, outs, sems):
        pass

    def finish(self, ins, outs, sems):
        for cp in self._copies(ins, outs, sems):
            cp.wait()


class _RsChips:
    def __init__(self, qs):
        n = len(qs)
        self.ins = list(qs)
        self.out_shape = [jax.ShapeDtypeStruct(q.shape, q.dtype) for q in qs]
        self.scratch = [pltpu.SemaphoreType.DMA((n, 3)), pltpu.SemaphoreType.DMA((n, 3)),
                        pltpu.SemaphoreType.DMA((n,))]

    def _copies(self, ins, outs, sems):
        send_sems, recv_sems, local_sems = sems
        n = len(ins)
        x, y, c = _place()
        my_chip = 2 * x + y
        chips = [(1 - x, y), (x, 1 - y), (1 - x, 1 - y)]

        def copy(a, j, src_slot, dst_slot):
            px, py = chips[j]
            return pltpu.make_async_remote_copy(
                src_ref=ins[a].at[src_slot], dst_ref=outs[a].at[dst_slot], send_sem=send_sems.at[a, j],
                recv_sem=recv_sems.at[a, j], device_id=(px, py, c), device_id_type=MESH)

        own = [pltpu.make_async_copy(ins[a].at[my_chip], outs[a].at[my_chip], local_sems.at[a]) for a in range(n)]
        sends = [copy(a, j, 2 * px + py, my_chip) for a in range(n) for j, (px, py) in enumerate(chips)]
        arrivals = [copy(a, j, my_chip, 2 * px + py) for a in range(n) for j, (px, py) in enumerate(chips)]
        return own, sends, arrivals

    def start(self, ins, outs, sems):
        own, sends, _ = self._copies(ins, outs, sems)
        for cp in own + sends:
            cp.start()

    def relay(self, ins, outs, sems):
        pass

    def finish(self, ins, outs, sems):
        own, sends, arrivals = self._copies(ins, outs, sems)
        for cp in arrivals:
            cp.wait_recv()
        for cp in sends:
            cp.wait_send()
        for cp in own:
            cp.wait()


def _split_plans(plans, refs_in, refs_out, refs_scr, phase):
    i = o = s = 0
    for p in plans:
        ni, no, ns = len(p.ins), len(p.out_shape), len(p.scratch)
        getattr(p, phase)(refs_in[i:i + ni], refs_out[o:o + no], refs_scr[s:s + ns])
        i, o, s = i + ni, o + no, s + ns


def _plan_results(plans, res):
    out = []
    for p in plans:
        out.append(list(res[:len(p.out_shape)]))
        res = res[len(p.out_shape):]
    return out


def _exchange(plans, name):
    c_in = [a for p in plans for a in p.ins]
    c_out = [s for p in plans for s in p.out_shape]
    c_scr = [s for p in plans for s in p.scratch]

    def body(*refs):
        cin, cout, cscr = refs[:len(c_in)], refs[len(c_in):len(c_in) + len(c_out)], refs[len(c_in) + len(c_out):]
        for phase in ("start", "relay", "finish"):
            _split_plans(plans, cin, cout, cscr, phase)

    res = pl.pallas_call(body, name=name, in_specs=[_any()] * len(c_in), out_specs=[_any()] * len(c_out),
                         out_shape=c_out, scratch_shapes=c_scr)(*c_in)
    return _plan_results(plans, res)


def _call(body, *, name, grid, in_specs, out_specs, out_shape, args, semantics, scratch_shapes=(), plans=()):
    n_in, n_out, n_scr = len(in_specs), len(out_specs), len(scratch_shapes)
    c_in = [a for p in plans for a in p.ins]
    c_out = [s for p in plans for s in p.out_shape]
    c_scr = [s for p in plans for s in p.scratch]
    n_steps = math.prod(grid)

    def wrapped(*refs):
        ins, refs = refs[:n_in], refs[n_in:]
        cin, refs = refs[:len(c_in)], refs[len(c_in):]
        outs, refs = refs[:n_out], refs[n_out:]
        cout, refs = refs[:len(c_out)], refs[len(c_out):]
        scr, cscr = refs[:n_scr], refs[n_scr:]
        if plans:
            step = 0
            for d, g in enumerate(grid):
                step = step * g + pl.program_id(d)
            pl.when(step == 0)(lambda: _split_plans(plans, cin, cout, cscr, "start"))
        body(*ins, *outs, *scr)
        if plans:
            pl.when(step == max(n_steps - 2, 0))(lambda: _split_plans(plans, cin, cout, cscr, "relay"))
            pl.when(step == n_steps - 1)(lambda: _split_plans(plans, cin, cout, cscr, "finish"))

    res = pl.pallas_call(
        wrapped, name=name, grid=grid, in_specs=list(in_specs) + [_any()] * len(c_in),
        out_specs=list(out_specs) + [_any()] * len(c_out), out_shape=list(out_shape) + c_out,
        scratch_shapes=list(scratch_shapes) + c_scr,
        compiler_params=_params(*(("arbitrary",) * len(grid) if plans else semantics)),
    )(*args, *c_in)
    return list(res[:n_out]), _plan_results(plans, res[n_out:])


def _ffn_fwd_loss(h, mods, gn, wg, wu, wd, row0, seq, tgt):
    T = h.shape[0]
    tm = _token_tile(seq, 256)
    tps = seq // tm
    n_t = T // tm

    def body(h_ref, m_ref, g_ref, wg_ref, wu_ref, wd_ref, tgt_ref, out_ref, y_ref, gb_ref, ub_ref, loss_ref):
        hv = h_ref[...]
        sh = m_ref[0, row0:row0 + 1, :]
        sc = m_ref[0, row0 + 1:row0 + 2, :]
        ga = m_ref[0, row0 + 2:row0 + 3, :]
        xn, _, _ = _rms_mod(hv, g_ref[...], sh, sc)
        xb = xn.astype(BF16)
        acc = jnp.zeros((tm, D), F32)
        for c0 in range(0, FF, FC):
            gg = _dot_nt(xb, wg_ref[c0:c0 + FC, :])
            uu = _dot_nt(xb, wu_ref[c0:c0 + FC, :])
            gg, uu = gg.astype(BF16), uu.astype(BF16)
            gb_ref[:, c0:c0 + FC] = gg
            ub_ref[:, c0:c0 + FC] = uu
            acc = acc + _dot((gg * jax.nn.sigmoid(gg)) * uu, wd_ref[c0:c0 + FC, :])
        y_ref[...] = acc
        d = hv + (0.5 * ga) * acc - tgt_ref[...]
        out_ref[...] = d * (1.0 / D)
        loss_ref[...] = jnp.full((1, 8, 128), 0.5 / D, F32) * jnp.sum(d * d)

    tok = pl.BlockSpec((tm, D), lambda i: (i, 0))
    tokf = pl.BlockSpec((tm, FF), lambda i: (i, 0))
    outs, _ = _call(
        body, name="ffn_fwd_loss", grid=(n_t,),
        in_specs=[tok, pl.BlockSpec((1, 9, D), lambda i: (i // tps, 0, 0)), pl.BlockSpec((1, D), lambda i: (0, 0)),
                  _vmem(), _vmem(), _vmem(), tok],
        out_specs=[tok, tok, tokf, tokf, pl.BlockSpec((1, 8, 128), lambda i: (i, 0, 0))],
        out_shape=[jax.ShapeDtypeStruct((T, D), F32), jax.ShapeDtypeStruct((T, D), F32),
                   jax.ShapeDtypeStruct((T, FF), BF16), jax.ShapeDtypeStruct((T, FF), BF16),
                   jax.ShapeDtypeStruct((n_t, 8, 128), F32)],
        args=(h, mods, gn, wg, wu, wd, tgt), semantics=("parallel",))
    return outs


def _ffn_gate_up(h, mods, gn, wg, wu, row0, seq, plans=()):
    T = h.shape[0]
    tm = _token_tile(seq, 512)
    tps = seq // tm

    def body(h_ref, m_ref, g_ref, wg_ref, wu_ref, gb_ref, ub_ref):
        xn, _, _ = _rms_mod(h_ref[...], g_ref[...], m_ref[0, row0:row0 + 1, :], m_ref[0, row0 + 1:row0 + 2, :])
        xb = xn.astype(BF16)
        for c0 in range(0, FF, FC):
            gb_ref[:, c0:c0 + FC] = _dot_nt(xb, wg_ref[c0:c0 + FC, :]).astype(BF16)
            ub_ref[:, c0:c0 + FC] = _dot_nt(xb, wu_ref[c0:c0 + FC, :]).astype(BF16)

    tokf = pl.BlockSpec((tm, FF), lambda i: (i, 0))
    return _call(
        body, name="ffn_gate_up", grid=(T // tm,),
        in_specs=[pl.BlockSpec((tm, D), lambda i: (i, 0)), pl.BlockSpec((1, 9, D), lambda i: (i // tps, 0, 0)),
                  pl.BlockSpec((1, D), lambda i: (0, 0)), _vmem(), _vmem()],
        out_specs=[tokf, tokf], out_shape=[jax.ShapeDtypeStruct((T, FF), BF16)] * 2, args=(h, mods, gn, wg, wu),
        semantics=("parallel",), plans=plans)


def _ffn_down(h, gb, ub, mods, wd, row0, seq, plans=()):
    T = h.shape[0]
    tm = _token_tile(seq, 512)
    tps = seq // tm

    def body(h_ref, gb_ref, ub_ref, m_ref, wd_ref, out_ref, y_ref):
        acc = jnp.zeros((tm, D), F32)
        for c0 in range(0, FF, FC):
            gg = gb_ref[:, c0:c0 + FC]
            acc = acc + _dot((gg * jax.nn.sigmoid(gg)) * ub_ref[:, c0:c0 + FC], wd_ref[c0:c0 + FC, :])
        y_ref[...] = acc
        out_ref[...] = h_ref[...] + (0.5 * m_ref[0, row0 + 2:row0 + 3, :]) * acc

    tok = pl.BlockSpec((tm, D), lambda i: (i, 0))
    tokf = pl.BlockSpec((tm, FF), lambda i: (i, 0))
    return _call(
        body, name="ffn_down", grid=(T // tm,),
        in_specs=[tok, tokf, tokf, pl.BlockSpec((1, 9, D), lambda i: (i // tps, 0, 0)), _vmem()],
        out_specs=[tok, tok], out_shape=[jax.ShapeDtypeStruct((T, D), F32)] * 2, args=(h, gb, ub, mods, wd),
        semantics=("parallel",), plans=plans)


def _ffn_down_loss(h, gb, ub, mods, wd, row0, seq, tgt):
    T = h.shape[0]
    tm = _token_tile(seq, 512)
    tps = seq // tm
    n_t = T // tm

    def body(h_ref, gb_ref, ub_ref, m_ref, wd_ref, tgt_ref, out_ref, y_ref, loss_ref):
        acc = jnp.zeros((tm, D), F32)
        for c0 in range(0, FF, FC):
            gg = gb_ref[:, c0:c0 + FC]
            acc = acc + _dot((gg * jax.nn.sigmoid(gg)) * ub_ref[:, c0:c0 + FC], wd_ref[c0:c0 + FC, :])
        y_ref[...] = acc
        d = h_ref[...] + (0.5 * m_ref[0, row0 + 2:row0 + 3, :]) * acc - tgt_ref[...]
        out_ref[...] = d * (1.0 / D)
        loss_ref[...] = jnp.full((1, 8, 128), 0.5 / D, F32) * jnp.sum(d * d)

    tok = pl.BlockSpec((tm, D), lambda i: (i, 0))
    tokf = pl.BlockSpec((tm, FF), lambda i: (i, 0))
    outs, _ = _call(
        body, name="ffn_down_loss", grid=(n_t,),
        in_specs=[tok, tokf, tokf, pl.BlockSpec((1, 9, D), lambda i: (i // tps, 0, 0)), _vmem(), tok],
        out_specs=[tok, tok, pl.BlockSpec((1, 8, 128), lambda i: (i, 0, 0))],
        out_shape=[jax.ShapeDtypeStruct((T, D), F32)] * 2 + [jax.ShapeDtypeStruct((n_t, 8, 128), F32)],
        args=(h, gb, ub, mods, wd, tgt), semantics=("parallel",))
    return outs


def _ffn_bwd(h, dhn, y, gb, ub, mods, gn, wg, wu, wd, row0, seq, plans=()):
    T = h.shape[0]
    B = T // seq
    tm = _token_tile(seq, 256)
    tps = seq // tm
    n_t = T // tm

    def body(h_ref, dhn_ref, y_ref, gb_ref, ub_ref, m_ref, g_ref, wg_ref, wu_ref, wd_ref,
             dh_ref, xb_ref, dyb_ref, a_ref, dg_ref, du_ref, mg_ref):
        i = pl.program_id(0)
        hv = h_ref[...]
        dhn = dhn_ref[...]
        sh = m_ref[0, row0:row0 + 1, :]
        sc = m_ref[0, row0 + 1:row0 + 2, :]
        ga = m_ref[0, row0 + 2:row0 + 3, :]
        g = g_ref[...]
        xn, r, inv = _rms_mod(hv, g, sh, sc)
        xb_ref[...] = xn.astype(BF16)
        dyb = ((0.5 * ga) * dhn).astype(BF16)
        dyb_ref[...] = dyb
        dga = 0.5 * jnp.sum(dhn * y_ref[...], axis=0, keepdims=True)
        dxn = jnp.zeros((tm, D), F32)
        for c0 in range(0, FF, FC):
            wgc = wg_ref[c0:c0 + FC, :]
            wuc = wu_ref[c0:c0 + FC, :]
            gg = gb_ref[:, c0:c0 + FC]
            uu = ub_ref[:, c0:c0 + FC]
            sig = jax.nn.sigmoid(gg)
            s = gg * sig
            a_ref[:, c0:c0 + FC] = s * uu
            da = _dot_nt(dyb, wd_ref[c0:c0 + FC, :]).astype(BF16)
            dub = da * s
            dgb = (da * uu) * (sig * (1.0 + gg * (1.0 - sig)))
            dg_ref[:, c0:c0 + FC] = dgb
            du_ref[:, c0:c0 + FC] = dub
            dxn = dxn + _dot(dgb, wgc) + _dot(dub, wuc)
        dh, s_dxn, s_dxr = _rms_mod_bwd(dxn, r, inv, g, sc)
        dh_ref[...] = dhn + dh

        @pl.when(i % tps == 0)
        def _():
            mg_ref[...] = jnp.zeros(mg_ref.shape, F32)

        mg_ref[0, 0:1, :] += s_dxn
        mg_ref[0, 1:2, :] += s_dxr
        mg_ref[0, 2:3, :] += dga

    tok = pl.BlockSpec((tm, D), lambda i: (i, 0))
    tokf = pl.BlockSpec((tm, FF), lambda i: (i, 0))
    return _call(
        body, name="ffn_bwd", grid=(n_t,),
        in_specs=[tok, tok, tok, tokf, tokf, pl.BlockSpec((1, 9, D), lambda i: (i // tps, 0, 0)),
                  pl.BlockSpec((1, D), lambda i: (0, 0)), _vmem(), _vmem(), _vmem()],
        out_specs=[tok, tok, tok, tokf, tokf, tokf, pl.BlockSpec((1, 8, D), lambda i: (i // tps, 0, 0))],
        out_shape=[jax.ShapeDtypeStruct((T, D), F32), jax.ShapeDtypeStruct((T, D), BF16),
                   jax.ShapeDtypeStruct((T, D), BF16), jax.ShapeDtypeStruct((T, FF), BF16),
                   jax.ShapeDtypeStruct((T, FF), BF16), jax.ShapeDtypeStruct((T, FF), BF16),
                   jax.ShapeDtypeStruct((B, 8, D), F32)],
        args=(h, dhn, y, gb, ub, mods, gn, wg, wu, wd), semantics=("arbitrary",), plans=plans)


def _wgrad(a, b, name, plans=()):
    T, da = a.shape
    db = b.shape[1]
    bm = {2816: 1408, 3840: 1280}[da]
    bn = db
    tk = min(2048, T)
    nk = T // tk

    def body(a_ref, b_ref, o_ref):
        @pl.when(pl.program_id(2) == 0)
        def _():
            o_ref[...] = jnp.zeros(o_ref.shape, F32)

        o_ref[...] += _dot_tn(a_ref[...], b_ref[...])

    (out,), plan_outs = _call(
        body, name=name, grid=(da // bm, db // bn, nk),
        in_specs=[pl.BlockSpec((tk, bm), lambda i, j, k: (k, i)), pl.BlockSpec((tk, bn), lambda i, j, k: (k, j))],
        out_specs=[pl.BlockSpec((bm, bn), lambda i, j, k: (i, j))], out_shape=[jax.ShapeDtypeStruct((da, db), F32)],
        args=(a, b), semantics=("parallel", "parallel", "arbitrary"), plans=plans)
    return out, plan_outs


def _inproj_fwd(h, mods, gn, w_in, seq, plans=()):
    T = h.shape[0]
    tm = _token_tile(seq)
    tps = seq // tm

    def body(h_ref, m_ref, g_ref, w_ref, *outs):
        xn, _, _ = _rms_mod(h_ref[...], g_ref[...], m_ref[0, 3:4, :], m_ref[0, 4:5, :])
        xb = xn.astype(BF16)
        for (_, c0, w), o_ref in zip(PIECES, outs):
            o_ref[...] = _dot_nt(xb, w_ref[c0:c0 + w, :])

    return _call(
        body, name="inproj_fwd", grid=(T // tm,),
        in_specs=[pl.BlockSpec((tm, D), lambda i: (i, 0)), pl.BlockSpec((1, 9, D), lambda i: (i // tps, 0, 0)),
                  pl.BlockSpec((1, D), lambda i: (0, 0)), _vmem()],
        out_specs=[pl.BlockSpec((tm, w), lambda i: (i, 0)) for _, _, w in PIECES],
        out_shape=[jax.ShapeDtypeStruct((T, w), F32) for _, _, w in PIECES], args=(h, mods, gn, w_in),
        semantics=("parallel",), plans=plans)


def _inproj_bwd(h, dh_res, dpieces, mods, gn, w_in, seq, plans=()):
    T = h.shape[0]
    B = T // seq
    tm = _token_tile(seq, 512)
    tps = seq // tm

    def body(h_ref, dres_ref, *rest):
        dp_refs = rest[:len(PIECES)]
        m_ref, g_ref, w_ref, dh_ref, xb_ref, dpb_ref, mg_ref = rest[len(PIECES):]
        i = pl.program_id(0)
        g = g_ref[...]
        sc = m_ref[0, 4:5, :]
        xn, r, inv = _rms_mod(h_ref[...], g, m_ref[0, 3:4, :], sc)
        xb_ref[...] = xn.astype(BF16)
        dxn = jnp.zeros((tm, D), F32)
        for (_, c0, w), dp_ref in zip(PIECES, dp_refs):
            dpb = dp_ref[...].astype(BF16)
            dpb_ref[:, c0:c0 + w] = dpb
            dxn = dxn + _dot(dpb, w_ref[c0:c0 + w, :])
        dh, s_dxn, s_dxr = _rms_mod_bwd(dxn, r, inv, g, sc)
        dh_ref[...] = dres_ref[...] + dh

        @pl.when(i % tps == 0)
        def _():
            mg_ref[...] = jnp.zeros(mg_ref.shape, F32)

        mg_ref[0, 0:1, :] += s_dxn
        mg_ref[0, 1:2, :] += s_dxr

    tok = pl.BlockSpec((tm, D), lambda i: (i, 0))
    return _call(
        body, name="inproj_bwd", grid=(T // tm,),
        in_specs=[tok, tok] + [pl.BlockSpec((tm, w), lambda i: (i, 0)) for _, _, w in PIECES]
        + [pl.BlockSpec((1, 9, D), lambda i: (i // tps, 0, 0)), pl.BlockSpec((1, D), lambda i: (0, 0)), _vmem()],
        out_specs=[tok, tok, pl.BlockSpec((tm, IN_COLS), lambda i: (i, 0)),
                   pl.BlockSpec((1, 8, D), lambda i: (i // tps, 0, 0))],
        out_shape=[jax.ShapeDtypeStruct((T, D), F32), jax.ShapeDtypeStruct((T, D), BF16),
                   jax.ShapeDtypeStruct((T, IN_COLS), BF16), jax.ShapeDtypeStruct((B, 8, D), F32)],
        args=(h, dh_res, *dpieces, mods, gn, w_in), semantics=("arbitrary",), plans=plans)


def _seg_mean(x):
    i = lax.broadcasted_iota(jnp.int32, (128, 128), 0) >> 6
    j = lax.broadcasted_iota(jnp.int32, (128, 128), 1) >> 6
    ones = jnp.where(i == j, 1.0 / HD, 0.0).astype(BF16)
    hi = x.astype(BF16)
    lo = (x - hi.astype(F32)).astype(BF16)
    return _dot(hi, ones) + _dot(lo, ones)


def _head_norm(x, g2):
    inv = lax.rsqrt(_seg_mean(x * x) + EPS)
    y = x * inv
    return y * g2, y, inv


def _head_norm_bwd(dxn, y, inv, g2):
    dy = dxn * g2
    return inv * (dy - y * _seg_mean(dy * y)), jnp.sum(dxn * y, axis=0, keepdims=True)


def _swa_block(q, kk, vv, gq2, gk2, sinks, first, do=None):
    lo = lax.broadcasted_iota(jnp.int32, (1, 128), 1) < HD
    kn, ky, kinv = _head_norm(kk, gk2)

    def operands(x):
        xr = pltpu.roll(x, HD, 1)
        own_lo, own_hi = jnp.where(lo, x, 0.0).astype(BF16), jnp.where(lo, 0.0, x).astype(BF16)
        rolled_lo, rolled_hi = jnp.where(lo, xr, 0.0).astype(BF16), jnp.where(lo, 0.0, xr).astype(BF16)
        return (own_lo, rolled_hi), (rolled_lo, own_hi)

    def restore(parts):
        (own_lo, rolled_hi), (rolled_lo, own_hi) = parts
        return (jnp.where(lo, own_lo, own_hi)
                + pltpu.roll(jnp.where(lo, rolled_lo, rolled_hi), HD, 1))

    k_ops, v_ops = operands(kn), operands(vv)
    k2 = [jnp.concatenate(pair, axis=0) for pair in k_ops]
    def stack(x):
        return jnp.concatenate([x[:, 128 * p:128 * (p + 1)] for p in range(4)], axis=0)

    def unstack(x):
        return jnp.concatenate([x[BLK * p:BLK * (p + 1)] for p in range(4)], axis=1)

    def of_head(x, kh):
        return x[2 * BLK * kh:2 * BLK * (kh + 1)]

    nq = 4 * BLK
    pair = lax.broadcasted_iota(jnp.int32, (nq, 1), 0) >> 7
    row = lax.broadcasted_iota(jnp.int32, (nq, 2 * BLK), 0) & (BLK - 1)
    col = lax.broadcasted_iota(jnp.int32, (nq, 2 * BLK), 1)
    valid = (col <= row + BLK) & (col > row) & (col >= jnp.where(first, BLK, 0))
    scale = HD ** -0.5
    qn, qy, qinv = _head_norm(stack(q), gq2)
    qnb = qn.astype(BF16)
    s2 = jnp.concatenate([_dot_nt(of_head(qnb, kh), k2[kh]) for kh in range(N_KV)], axis=0) * scale
    probs, p_sink = [], []
    for j in range(2):
        s = jnp.where(valid, s2[:, 2 * BLK * j:2 * BLK * (j + 1)], NEG)
        sink = jnp.zeros((nq, 1), F32)
        for p in range(4):
            sink = jnp.where(pair == p, sinks[:, 2 * p + j:2 * p + j + 1], sink)
        m = jnp.maximum(jnp.max(s, axis=-1, keepdims=True), sink)
        e = jnp.exp(s - m)
        e_sink = jnp.exp(sink - m)
        rden = 1.0 / (jnp.sum(e, axis=-1, keepdims=True) + e_sink)
        probs.append(e * rden)
        p_sink.append(e_sink * rden)
    pb = [p.astype(BF16) for p in probs]
    if do is None:
        return unstack(jnp.concatenate(
            [_dot(of_head(pb[0], kh), v_ops[kh][0]) + _dot(of_head(pb[1], kh), v_ops[kh][1]) for kh in range(N_KV)],
            axis=0))
    dob = stack(do).astype(BF16)
    ds, dsinks = [], [None] * 8
    for j in range(2):
        dp = jnp.concatenate([_dot_nt(of_head(dob, kh), v_ops[kh][j]) for kh in range(N_KV)], axis=0)
        t = jnp.sum(dp * probs[j], axis=-1, keepdims=True)
        ds.append(probs[j] * (dp - t) * scale)
        lost = p_sink[j] * t
        for p in range(4):
            dsinks[2 * p + j] = -jnp.sum(lost[BLK * p:BLK * (p + 1)])
    dsb = jnp.concatenate(ds, axis=1).astype(BF16)
    dqn = jnp.concatenate([_dot(of_head(dsb, kh), k2[kh]) for kh in range(N_KV)], axis=0)
    dq, dgq2 = _head_norm_bwd(dqn, qy, qinv, gq2)
    dk2 = [_dot_tn(of_head(dsb, kh), of_head(qnb, kh)) for kh in range(N_KV)]
    dv_ops = [[_dot_tn(of_head(pb[j], kh), of_head(dob, kh)) for j in range(2)] for kh in range(N_KV)]
    dkn = restore(tuple((d[:2 * BLK], d[2 * BLK:]) for d in dk2))
    dvv = restore(tuple(tuple(d) for d in dv_ops))
    dkk, dgk2 = _head_norm_bwd(dkn, ky, kinv, gk2)
    return unstack(dq), dkk, dvv, dgq2, dgk2, dsinks


SWA_GROUP = 8


def _swa_specs(nbs):
    grp = min(SWA_GROUP, nbs)
    rows = grp * BLK

    def tok(w):
        return pl.BlockSpec((rows, w), lambda i: (i, 0))

    halo = pl.BlockSpec((BLK, 128), lambda i: (jnp.maximum(i * grp - 1, 0), 0))
    vec = pl.BlockSpec((1, 128), lambda i: (0, 0))
    sk = pl.BlockSpec((1, 8), lambda i: (0, 0))
    return grp, tok, halo, vec, sk


def _swa_fwd(q, k, v, gq2, gk2, sinks, nbs, plans=()):
    T = q.shape[0]
    grp, tok, halo, vec, sk = _swa_specs(nbs)

    def body(q_ref, kh_ref, kc_ref, vh_ref, vc_ref, gq_ref, gk_ref, sk_ref, o_ref):
        seq_start = ((pl.program_id(0) * grp) % nbs) == 0
        for g in range(grp):
            rows = slice(g * BLK, (g + 1) * BLK)
            prev = slice((g - 1) * BLK, g * BLK)
            kk = jnp.concatenate([kh_ref[...] if g == 0 else kc_ref[prev, :], kc_ref[rows, :]], axis=0)
            vv = jnp.concatenate([vh_ref[...] if g == 0 else vc_ref[prev, :], vc_ref[rows, :]], axis=0)
            o_ref[rows, :] = _swa_block(q_ref[rows, :], kk, vv, gq_ref[...], gk_ref[...], sk_ref[...],
                                        seq_start if g == 0 else False)

    (out,), plan_outs = _call(
        body, name="swa_fwd", grid=(T // (grp * BLK),),
        in_specs=[tok(D_B), halo, tok(128), halo, tok(128), vec, vec, sk], out_specs=[tok(D_B)],
        out_shape=[jax.ShapeDtypeStruct((T, D_B), F32)], args=(q, k, k, v, v, gq2, gk2, sinks),
        semantics=("parallel",), plans=plans)
    return out, plan_outs


def _swa_bwd(q, k, v, gq2, gk2, sinks, do, nbs, plans=()):
    T = q.shape[0]
    grp, tok, halo, vec, sk = _swa_specs(nbs)
    steps = T // (grp * BLK)

    def body(q_ref, kh_ref, kc_ref, vh_ref, vc_ref, gq_ref, gk_ref, sk_ref, do_ref,
             dq_ref, dk_ref, dv_ref, dkh_ref, dvh_ref, dgq_ref, dgk_ref, dsk_ref):
        i = pl.program_id(0)
        seq_start = ((i * grp) % nbs) == 0

        @pl.when(i == 0)
        def _():
            for r in (dgq_ref, dgk_ref, dsk_ref):
                r[...] = jnp.zeros(r.shape, F32)

        res = []
        for g in range(grp):
            rows = slice(g * BLK, (g + 1) * BLK)
            prev = slice((g - 1) * BLK, g * BLK)
            kk = jnp.concatenate([kh_ref[...] if g == 0 else kc_ref[prev, :], kc_ref[rows, :]], axis=0)
            vv = jnp.concatenate([vh_ref[...] if g == 0 else vc_ref[prev, :], vc_ref[rows, :]], axis=0)
            res.append(_swa_block(q_ref[rows, :], kk, vv, gq_ref[...], gk_ref[...], sk_ref[...],
                                  seq_start if g == 0 else False, do=do_ref[rows, :]))
        lane = lax.broadcasted_iota(jnp.int32, (8, 128), 1)
        upd = jnp.zeros((8, 128), F32)
        for g, (dq, dkk, dvv, dgq2, dgk2, dsinks) in enumerate(res):
            rows = slice(g * BLK, (g + 1) * BLK)
            dq_ref[rows, :] = dq
            dk_ref[rows, :] = dkk[BLK:] + res[g + 1][1][:BLK] if g + 1 < grp else dkk[BLK:]
            dv_ref[rows, :] = dvv[BLK:] + res[g + 1][2][:BLK] if g + 1 < grp else dvv[BLK:]
            dgq_ref[...] += dgq2
            dgk_ref[...] += dgk2
            for h, d in enumerate(dsinks):
                upd = upd + jnp.where(lane == h, d, 0.0)
        dkh_ref[0] = res[0][1][:BLK]
        dvh_ref[0] = res[0][2][:BLK]
        dsk_ref[...] += upd

    one = pl.BlockSpec((1, BLK, 128), lambda i: (i, 0, 0))
    halo_shape = jax.ShapeDtypeStruct((steps, BLK, 128), F32)
    return _call(
        body, name="swa_bwd", grid=(steps,),
        in_specs=[tok(D_B), halo, tok(128), halo, tok(128), vec, vec, sk, tok(D_B)],
        out_specs=[tok(D_B), tok(128), tok(128), one, one, vec, vec, pl.BlockSpec((8, 128), lambda i: (0, 0))],
        out_shape=[jax.ShapeDtypeStruct((T, D_B), F32), jax.ShapeDtypeStruct((T, 128), F32),
                   jax.ShapeDtypeStruct((T, 128), F32), halo_shape, halo_shape, jax.ShapeDtypeStruct((1, 128), F32),
                   jax.ShapeDtypeStruct((1, 128), F32), jax.ShapeDtypeStruct((8, 128), F32)],
        args=(q, k, k, v, v, gq2, gk2, sinks, do), semantics=("arbitrary",), plans=plans)


def _swa_add_halo(dk, dk_halo):
    steps = dk_halo.shape[0]
    nxt = jnp.concatenate([dk_halo[1:], jnp.zeros_like(dk_halo[:1])], axis=0)[:, None]
    dk = dk.reshape(steps, -1, BLK, 128)
    return jnp.concatenate([dk[:, :-1], dk[:, -1:] + nxt], axis=1).reshape(-1, 128)


def _sgu_norm(av, g_ln, b_ln):
    t, th = _gelu(av)
    mu = jnp.mean(t, axis=-1, keepdims=True)
    tc = t - mu
    rstd = lax.rsqrt(jnp.mean(tc * tc, axis=-1, keepdims=True) + EPS)
    vhat = tc * rstd
    return vhat * g_ln + b_ln, vhat, rstd, th


def _masked_ws(ws_ref):
    tril = lax.broadcasted_iota(jnp.int32, (BLK, BLK), 0) >= lax.broadcasted_iota(jnp.int32, (BLK, BLK), 1)
    return [jnp.where(tril, ws_ref[g], 0.0).astype(BF16) for g in range(N_GRP)]


def _mix_fwd(au, av, gta, gtb, ob, h, mods, g_ln, b_ln, ws, bsb, wa, wb, wout, seq, plans=()):
    T = h.shape[0]
    tm = _token_tile(seq)
    tps = seq // tm

    def body(au_ref, av_ref, gta_ref, gtb_ref, ob_ref, h_ref, m_ref, gl_ref, bl_ref, ws_ref, bs_ref,
             wa_ref, wb_ref, wo_ref, out_ref, vvb_s, z_s):
        u, _ = _gelu(au_ref[...])
        vv, _, _, _ = _sgu_norm(av_ref[...], gl_ref[...], bl_ref[...])
        vvb_s[...] = vv.astype(BF16)
        wsm = _masked_ws(ws_ref)
        for c in range(tm // BLK):
            rows = slice(c * BLK, (c + 1) * BLK)
            for g in range(N_GRP):
                cols = slice(g * BLK, (g + 1) * BLK)
                z_s[rows, cols] = _dot(wsm[g], vvb_s[rows, cols]) + bs_ref[g]
        ya = _dot_nt((u * z_s[...]).astype(BF16), wa_ref[...])
        yb = _dot_nt(ob_ref[...].astype(BF16), wb_ref[...])
        merged = jax.nn.sigmoid(gta_ref[...]) * ya + jax.nn.sigmoid(gtb_ref[...]) * yb
        out_ref[...] = h_ref[...] + m_ref[0, 5:6, :] * _dot(merged.astype(BF16), wo_ref[...])

    def tok(w):
        return pl.BlockSpec((tm, w), lambda i: (i, 0))

    def full(shape):
        return pl.BlockSpec(shape, lambda i: (0,) * len(shape))

    (out,), plan_outs = _call(
        body, name="mix_fwd", grid=(T // tm,),
        in_specs=[tok(D_A), tok(D_A), tok(D), tok(D), tok(D_B), tok(D),
                  pl.BlockSpec((1, 9, D), lambda i: (i // tps, 0, 0)), full((1, D_A)), full((1, D_A)),
                  full((N_GRP, BLK, BLK)), full((N_GRP, BLK, BLK)), _vmem(), _vmem(), _vmem()],
        out_specs=[tok(D)], out_shape=[jax.ShapeDtypeStruct((T, D), F32)],
        scratch_shapes=[pltpu.VMEM((tm, D_A), BF16), pltpu.VMEM((tm, D_A), F32)],
        args=(au, av, gta, gtb, ob, h, mods, g_ln, b_ln, ws, bsb, wa, wb, wout), semantics=("parallel",),
        plans=plans)
    return out, plan_outs


def _mix_bwd(au, av, gta, gtb, ob, dh, mods, g_ln, b_ln, ws, bsb, wa, wb, wout, seq, plans=()):
    T = dh.shape[0]
    B = T // seq
    tm = _token_tile(seq, 256)
    tps = seq // tm

    def body(au_ref, av_ref, gta_ref, gtb_ref, ob_ref, dh_ref, m_ref, gl_ref, bl_ref, ws_ref, bs_ref,
             wa_ref, wb_ref, wo_ref,
             dau_ref, dav_ref, dgta_ref, dgtb_ref, dob_ref, dwo_ref, dwa_ref, dwb_ref, dws_ref, dbs_ref, dln_ref,
             mg_ref, vvb_s, z_s, dz_s, dzb_s, dvv_s):
        i = pl.program_id(0)

        @pl.when(i == 0)
        def _():
            for r in (dwo_ref, dwa_ref, dwb_ref, dws_ref, dbs_ref, dln_ref):
                r[...] = jnp.zeros(r.shape, F32)

        @pl.when(i % tps == 0)
        def _():
            mg_ref[...] = jnp.zeros(mg_ref.shape, F32)

        auv = au_ref[...]
        avv = av_ref[...]
        u, thu = _gelu(auv)
        g_ln = gl_ref[...]
        vv, vhat, rstd, thv = _sgu_norm(avv, g_ln, bl_ref[...])
        vvb_s[...] = vv.astype(BF16)
        wsm = _masked_ws(ws_ref)
        for c in range(tm // BLK):
            rows = slice(c * BLK, (c + 1) * BLK)
            for g in range(N_GRP):
                cols = slice(g * BLK, (g + 1) * BLK)
                z_s[rows, cols] = _dot(wsm[g], vvb_s[rows, cols]) + bs_ref[g]
        z = z_s[...]
        yab = (u * z).astype(BF16)
        obb = ob_ref[...].astype(BF16)
        ya = _dot_nt(yab, wa_ref[...])
        yb = _dot_nt(obb, wb_ref[...])
        sa = jax.nn.sigmoid(gta_ref[...])
        sb = jax.nn.sigmoid(gtb_ref[...])
        mb = (sa * ya + sb * yb).astype(BF16)
        dhv = dh_ref[...]
        mg_ref[0, 0:1, :] += jnp.sum(dhv * _dot(mb, wo_ref[...]), axis=0, keepdims=True)
        dmob = (m_ref[0, 5:6, :] * dhv).astype(BF16)
        dwo_ref[...] += _dot_tn(mb, dmob)
        dmerged = _dot_nt(dmob, wo_ref[...])
        dya = dmerged * sa
        dyb = dmerged * sb
        dgta_ref[...] = dya * ya * (1.0 - sa)
        dgtb_ref[...] = dyb * yb * (1.0 - sb)
        dyab = dya.astype(BF16)
        dybb = dyb.astype(BF16)
        dwa_ref[...] += _dot_tn(dyab, yab)
        dwb_ref[...] += _dot_tn(dybb, obb)
        dob_ref[...] = _dot(dybb, wb_ref[...])
        dyap = _dot(dyab, wa_ref[...])
        dau_ref[...] = (dyap * z) * _gelu_grad(auv, thu)
        dz = dyap * u
        dz_s[...] = dz
        dzb_s[...] = dz.astype(BF16)
        for c in range(tm // BLK):
            rows = slice(c * BLK, (c + 1) * BLK)
            for g in range(N_GRP):
                cols = slice(g * BLK, (g + 1) * BLK)
                dzb = dzb_s[rows, cols]
                dvv_s[rows, cols] = _dot_tn(wsm[g], dzb)
                dws_ref[g] += _dot_nt(dzb, vvb_s[rows, cols])
                dbs_ref[g] += dz_s[rows, cols]
        dvv = dvv_s[...]
        dln_ref[0:1, :] += jnp.sum(dvv * vhat, axis=0, keepdims=True)
        dln_ref[1:2, :] += jnp.sum(dvv, axis=0, keepdims=True)
        dvh = dvv * g_ln
        dt = rstd * (dvh - jnp.mean(dvh, axis=-1, keepdims=True)
                     - vhat * jnp.mean(dvh * vhat, axis=-1, keepdims=True))
        dav_ref[...] = dt * _gelu_grad(avv, thv)

    def tok(w):
        return pl.BlockSpec((tm, w), lambda i: (i, 0))

    def full(shape):
        return pl.BlockSpec(shape, lambda i: (0,) * len(shape))

    return _call(
        body, name="mix_bwd", grid=(T // tm,),
        in_specs=[tok(D_A), tok(D_A), tok(D), tok(D), tok(D_B), tok(D),
                  pl.BlockSpec((1, 9, D), lambda i: (i // tps, 0, 0)), full((1, D_A)), full((1, D_A)),
                  full((N_GRP, BLK, BLK)), full((N_GRP, BLK, BLK)), _vmem(), _vmem(), _vmem()],
        out_specs=[tok(D_A), tok(D_A), tok(D), tok(D), tok(D_B), full((D, D)), full((D, D_A)), full((D, D_B)),
                   full((N_GRP, BLK, BLK)), full((N_GRP, BLK, BLK)), full((8, D_A)),
                   pl.BlockSpec((1, 8, D), lambda i: (i // tps, 0, 0))],
        out_shape=[jax.ShapeDtypeStruct((T, D_A), F32), jax.ShapeDtypeStruct((T, D_A), F32),
                   jax.ShapeDtypeStruct((T, D), F32), jax.ShapeDtypeStruct((T, D), F32),
                   jax.ShapeDtypeStruct((T, D_B), F32), jax.ShapeDtypeStruct((D, D), F32),
                   jax.ShapeDtypeStruct((D, D_A), F32), jax.ShapeDtypeStruct((D, D_B), F32),
                   jax.ShapeDtypeStruct((N_GRP, BLK, BLK), F32), jax.ShapeDtypeStruct((N_GRP, BLK, BLK), F32),
                   jax.ShapeDtypeStruct((8, D_A), F32), jax.ShapeDtypeStruct((B, 8, D), F32)],
        scratch_shapes=[pltpu.VMEM((tm, D_A), BF16), pltpu.VMEM((tm, D_A), F32), pltpu.VMEM((tm, D_A), F32),
                        pltpu.VMEM((tm, D_A), BF16), pltpu.VMEM((tm, D_A), F32)],
        args=(au, av, gta, gtb, ob, dh, mods, g_ln, b_ln, ws, bsb, wa, wb, wout), semantics=("arbitrary",),
        plans=plans)


def _prologue(c_pad, w_ada, first_shards):
    cols = w_ada.shape[1]
    plan_w, plan_c = _GatherRelayed(first_shards), _GatherDirect([c_pad])
    plan_m = _Gather([jax.ShapeDtypeStruct((N_DEV * 8, cols), F32)])
    n_w = len(first_shards)

    def body(c_ref, wada_ref, *rest):
        w_ins, rest = rest[:n_w], rest[n_w:]
        call_ref, mods_ref = rest[:2]
        w_outs, rest = rest[2:2 + n_w], rest[2 + n_w:]
        cvm, part, local_sem = rest[:3]
        sems = rest[3:]
        sems_w, sems_c, sems_m = sems[:3], sems[3:6], sems[6:9]
        plan_c.start([c_ref], [call_ref], sems_c)
        plan_c.relay([c_ref], [call_ref], sems_c)
        plan_c.finish([c_ref], [call_ref], sems_c)
        load = pltpu.make_async_copy(call_ref, cvm, local_sem)
        load.start()
        load.wait()
        cv = cvm[...].reshape(N_DEV * 8, D)
        part[...] = _dot((cv * jax.nn.sigmoid(cv)).astype(BF16), wada_ref[...].astype(BF16))
        plan_m.start([part], [mods_ref], sems_m)
        plan_w.start(w_ins, w_outs, sems_w)
        plan_m.relay([part], [mods_ref], sems_m)
        plan_m.finish([part], [mods_ref], sems_m)
        plan_w.relay(w_ins, w_outs, sems_w)
        plan_w.finish(w_ins, w_outs, sems_w)

    res = pl.pallas_call(
        body, name="prologue", in_specs=[_any(), _vmem()] + [_any()] * n_w,
        out_specs=[_any()] * (2 + n_w), out_shape=plan_c.out_shape + plan_m.out_shape + plan_w.out_shape,
        scratch_shapes=[pltpu.VMEM((N_DEV, 8, D), F32), pltpu.VMEM((N_DEV * 8, cols), F32), pltpu.SemaphoreType.DMA]
        + plan_w.scratch + plan_c.scratch + plan_m.scratch,
        compiler_params=pltpu.CompilerParams(vmem_limit_bytes=VMEM_LIMIT),
    )(c_pad, w_ada, *first_shards)
    return res[0], res[1], list(res[2:])


def _ada_update(c_all, dm_cols, w, m, v):
    n, cols = c_all.shape[0], w.shape[1]
    br = 256

    def body(c_ref, dm_ref, w_ref, m_ref, v_ref, g_ref, d_ref, nm_ref, nv_ref):
        cv = c_ref[...]
        g = _dot_tn((cv * jax.nn.sigmoid(cv)).astype(BF16), dm_ref[...].astype(BF16))
        g_ref[...] = g
        d_ref[...], nm_ref[...], nv_ref[...] = _adamw(w_ref[...], g, m_ref[...], v_ref[...])

    rows = pl.BlockSpec((br, cols), lambda i: (i, 0))
    outs, _ = _call(
        body, name="ada_update", grid=(D // br,),
        in_specs=[pl.BlockSpec((n, br), lambda i: (0, i)), pl.BlockSpec((n, cols), lambda i: (0, 0)), rows, rows, rows],
        out_specs=[rows] * 4, out_shape=[jax.ShapeDtypeStruct(w.shape, F32)] * 4, args=(c_all, dm_cols, w, m, v),
        semantics=("parallel",))
    return outs


def _mod_finish(mg1, mg2, mg2g, mg3, mods, g1, g2, g3):
    B = mods.shape[0]

    def body(mg1_ref, mg2_ref, mg2g_ref, mg3_ref, m_ref, g1_ref, g2_ref, g3_ref, dm_ref, dgn_ref):
        dgn_ref[...] = jnp.zeros(dgn_ref.shape, F32)
        for k, (mg, g_ref) in enumerate(((mg1_ref, g1_ref), (mg2_ref, g2_ref), (mg3_ref, g3_ref))):
            for b in range(B):
                s_dxr = mg[b, 1:2, :]
                dm_ref[b, 3 * k:3 * k + 1, :] = mg[b, 0:1, :]
                dm_ref[b, 3 * k + 1:3 * k + 2, :] = g_ref[...] * s_dxr
                dm_ref[b, 3 * k + 2:3 * k + 3, :] = mg2g_ref[b, 0:1, :] if k == 1 else mg[b, 2:3, :]
                dgn_ref[k:k + 1, :] += (1.0 + m_ref[b, 3 * k + 1:3 * k + 2, :]) * s_dxr

    args = (mg1, mg2, mg2g, mg3, mods, g1, g2, g3)
    out_shape = [jax.ShapeDtypeStruct((B, 9, D), F32), jax.ShapeDtypeStruct((8, D), F32)]
    return pl.pallas_call(body, name="mod_finish", grid=(1,), in_specs=[_whole(a) for a in args],
                          out_specs=[_whole(o) for o in out_shape], out_shape=out_shape,
                          compiler_params=_params("arbitrary"))(*args)


def _small_update(gathered, params, ms, vs):
    n = len(SMALL)
    B = gathered[0].shape[1]

    def body(*refs):
        gdm, ggn, gln, gws, gbs, ggq, ggk, gsk, gloss = refs[:9]
        w = dict(zip(SMALL, refs[9:9 + n]))
        m = dict(zip(SMALL, refs[9 + n:9 + 2 * n]))
        v = dict(zip(SMALL, refs[9 + 2 * n:9 + 3 * n]))
        outs = refs[9 + 3 * n:-1]
        out = {name: outs[4 * k:4 * k + 4] for k, name in enumerate(SMALL)}

        def total(ref, idx):
            acc = ref[(0,) + idx]
            for dev in range(1, N_DEV):
                acc = acc + ref[(dev,) + idx]
            return acc

        def finish(name, g, idx=(Ellipsis,)):
            d, nm, nv = _adamw(w[name][idx], g, m[name][idx], v[name][idx])
            for ref, val in zip(out[name], (g, d, nm, nv)):
                ref[idx] = val

        g_bada = total(gdm, (slice(0, 1),))
        for b in range(1, B):
            g_bada = g_bada + total(gdm, (slice(b, b + 1),))
        finish("b_ada", g_bada)
        finish("g_norm1", total(ggn, (slice(0, 1),)))
        finish("g_norm2", total(ggn, (slice(1, 2),)))
        finish("g_norm3", total(ggn, (slice(2, 3),)))
        finish("g_sgu_ln", total(gln, (slice(0, 1),)))
        finish("b_sgu_ln", total(gln, (slice(1, 2),)))
        tril = lax.broadcasted_iota(jnp.int32, (BLK, BLK), 0) >= lax.broadcasted_iota(jnp.int32, (BLK, BLK), 1)
        for g in range(N_GRP):
            finish("w_spatial", jnp.where(tril, total(gws, (g,)), 0.0), (0, g))
            finish("b_spatial", jnp.sum(total(gbs, (g,)).T, axis=0, keepdims=True), (0, slice(g, g + 1)))
        finish("g_q", total(ggq, ()))
        finish("g_k", total(ggk, ()))
        finish("attn_sinks", total(gsk, (slice(0, 1), slice(0, N_KV * Q_PER_KV))))
        refs[-1][...] = total(gloss, ())

    args = list(gathered) + [params[k] for k in SMALL] + [ms[k] for k in SMALL] + [vs[k] for k in SMALL]
    out_shape = []
    for k in SMALL:
        out_shape += [jax.ShapeDtypeStruct(params[k].shape, F32)] * 4
    out_shape.append(jax.ShapeDtypeStruct((8, 128), F32))
    res = pl.pallas_call(body, name="small_update", grid=(1,), in_specs=[_whole(a) for a in args],
                         out_specs=[_whole(o) for o in out_shape], out_shape=out_shape,
                         compiler_params=_params("arbitrary"))(*args)
    return {k: res[4 * i:4 * i + 4] for i, k in enumerate(SMALL)}, res[-1][0, 0]


def _rs_add(p, r, core, name):
    _, rows, width = p.shape

    def body(c_ref, p_ref, r_ref, o_ref):
        o_ref[...] = (p_ref[...] + r_ref[...]).astype(BF16)

    return pl.pallas_call(
        body, name="rs_add_" + name, out_shape=jax.ShapeDtypeStruct((4, rows, width), BF16),
        grid_spec=pltpu.PrefetchScalarGridSpec(
            num_scalar_prefetch=1, grid=(4,),
            in_specs=[pl.BlockSpec((1, rows, width), lambda k, c_ref: (2 * k + c_ref[0], 0, 0)),
                      pl.BlockSpec((1, rows, width), lambda k, c_ref: (k, 0, 0))],
            out_specs=pl.BlockSpec((1, rows, width), lambda k, c_ref: (k, 0, 0))),
        compiler_params=_params("parallel"),
    )(core, p, r)


def _adam_big(w, m, v, r, name):
    rows, cols = w.shape

    def body(r_ref, w_ref, m_ref, v_ref, go_ref, d_ref, nm_ref, nv_ref):
        g = r_ref[0].astype(F32)
        for k in range(1, 4):
            g = g + r_ref[k].astype(F32)
        go_ref[...] = g
        d_ref[...], nm_ref[...], nv_ref[...] = _adamw(w_ref[...], g, m_ref[...], v_ref[...])

    br = rows // 2
    shard = pl.BlockSpec((br, cols), lambda i: (i, 0))
    return pl.pallas_call(body, name="adam_" + name, grid=(2,),
                          in_specs=[pl.BlockSpec((4, br, cols), lambda i: (0, i, 0)), shard, shard, shard],
                          out_specs=[shard] * 4, out_shape=[jax.ShapeDtypeStruct(w.shape, F32)] * 4,
                          compiler_params=_params("parallel"))(r, w, m, v)


def kernel(x, c, w_ada, b_ada, g_norm1, ffn1_w_gate, ffn1_w_up, ffn1_w_down, g_norm2, w_in, g_sgu_ln, b_sgu_ln, w_spatial, b_spatial, g_q, g_k, attn_sinks, w_branch_a, w_branch_b, w_out, g_norm3, ffn2_w_gate, ffn2_w_up, ffn2_w_down, loss_target, m_w_ada, m_b_ada, m_g_norm1, m_ffn1_w_gate, m_ffn1_w_up, m_ffn1_w_down, m_g_norm2, m_w_in, m_g_sgu_ln, m_b_sgu_ln, m_w_spatial, m_b_spatial, m_g_q, m_g_k, m_attn_sinks, m_w_branch_a, m_w_branch_b, m_w_out, m_g_norm3, m_ffn2_w_gate, m_ffn2_w_up, m_ffn2_w_down, v_w_ada, v_b_ada, v_g_norm1, v_ffn1_w_gate, v_ffn1_w_up, v_ffn1_w_down, v_g_norm2, v_w_in, v_g_sgu_ln, v_b_sgu_ln, v_w_spatial, v_b_spatial, v_g_q, v_g_k, v_attn_sinks, v_w_branch_a, v_w_branch_b, v_w_out, v_g_norm3, v_ffn2_w_gate, v_ffn2_w_up, v_ffn2_w_down):
    names = ("w_ada", "b_ada", "g_norm1", "ffn1_w_gate", "ffn1_w_up", "ffn1_w_down", "g_norm2", "w_in", "g_sgu_ln",
             "b_sgu_ln", "w_spatial", "b_spatial", "g_q", "g_k", "attn_sinks", "w_branch_a", "w_branch_b", "w_out",
             "g_norm3", "ffn2_w_gate", "ffn2_w_up", "ffn2_w_down")
    w = dict(zip(names, (w_ada, b_ada, g_norm1, ffn1_w_gate, ffn1_w_up, ffn1_w_down, g_norm2, w_in, g_sgu_ln,
                         b_sgu_ln, w_spatial, b_spatial, g_q, g_k, attn_sinks, w_branch_a, w_branch_b, w_out, g_norm3,
                         ffn2_w_gate, ffn2_w_up, ffn2_w_down)))
    m = dict(zip(names, (m_w_ada, m_b_ada, m_g_norm1, m_ffn1_w_gate, m_ffn1_w_up, m_ffn1_w_down, m_g_norm2, m_w_in,
                         m_g_sgu_ln, m_b_sgu_ln, m_w_spatial, m_b_spatial, m_g_q, m_g_k, m_attn_sinks, m_w_branch_a,
                         m_w_branch_b, m_w_out, m_g_norm3, m_ffn2_w_gate, m_ffn2_w_up, m_ffn2_w_down)))
    v = dict(zip(names, (v_w_ada, v_b_ada, v_g_norm1, v_ffn1_w_gate, v_ffn1_w_up, v_ffn1_w_down, v_g_norm2, v_w_in,
                         v_g_sgu_ln, v_b_sgu_ln, v_w_spatial, v_b_spatial, v_g_q, v_g_k, v_attn_sinks, v_w_branch_a,
                         v_w_branch_b, v_w_out, v_g_norm3, v_ffn2_w_gate, v_ffn2_w_up, v_ffn2_w_down)))
    B, seq, _ = x.shape
    T = B * seq
    nbs = seq // BLK
    xi, yi, ci = _place()
    me = 4 * xi + 2 * yi + ci
    core = jnp.reshape(ci, (1,)).astype(jnp.int32)
    layout = {name: (tform, n, width) for name, tform, n, width in BIG}
    shard = {name: w[name][0].astype(BF16).T if tform else w[name][0].astype(BF16) for name, tform, _, _ in BIG}
    wts, parts, big_out = {}, {}, {}

    def gather_plan(group):
        return _Gather([shard[k] for k in group])

    def take(group, gathered):
        for k, g in zip(group, gathered):
            wts[k] = g.reshape(N_DEV * layout[k][1], layout[k][2])

    def blocks(name, grad):
        return grad.reshape(N_DEV, layout[name][1], layout[name][2])

    def to_sibling(names, grads):
        return _RsSibling([blocks(k, g) for k, g in zip(names, grads)])

    def add(names, grads, from_sibling):
        for k, g, r in zip(names, grads, from_sibling):
            parts[k] = _rs_add(blocks(k, g), r, core, k)

    def to_chips(names):
        return _RsChips([parts[k] for k in names])

    def update(names, from_chips):
        for k, r in zip(names, from_chips):
            if layout[k][0]:
                big_out[k] = [o.T[None] for o in _adam_big(w[k][0].T, m[k][0].T, v[k][0].T, r, k)]
            else:
                big_out[k] = [o[None] for o in _adam_big(w[k][0], m[k][0], v[k][0], r, k)]

    g1, g2, g3, g_ln, b_ln = g_norm1, g_norm2, g_norm3, g_sgu_ln, b_sgu_ln
    gq2, gk2 = jnp.tile(g_q, (1, 2)), jnp.tile(g_k, (1, 2))
    ws = w_spatial[0]
    bsb = jnp.broadcast_to(b_spatial[0][:, :, None], (N_GRP, BLK, BLK))
    xf = x.reshape(T, D)
    tgt = loss_target.reshape(T, D)

    ffn1 = ("ffn1_w_gate", "ffn1_w_up", "ffn1_w_down")
    ffn2 = ("ffn2_w_gate", "ffn2_w_up", "ffn2_w_down")
    c_pad = jnp.concatenate([c, jnp.zeros((8 - B, D), F32)], axis=0)
    c_all, mods_cols, gathered = _prologue(c_pad, w_ada[0], [shard[k] for k in ffn1[:2]])
    take(ffn1[:2], gathered)
    c_all = c_all[:, :B].reshape(N_DEV * B, D)
    mine = lax.dynamic_slice_in_dim(mods_cols, 8 * me, B, axis=1)
    mods = (mine.transpose(1, 0, 2).reshape(B, 9 * D) + b_ada).reshape(B, 9, D)

    group = ("ffn1_w_down", "w_branch_a", "w_branch_b", "w_out")
    (gb1, ub1), (gathered,) = _ffn_gate_up(xf, mods, g1, wts["ffn1_w_gate"], wts["ffn1_w_up"], 0, seq,
                                           plans=[gather_plan(group)])
    take(group, gathered)
    group = ("w_in",)
    (h1, y1), (gathered,) = _ffn_down(xf, gb1, ub1, mods, wts["ffn1_w_down"], 0, seq, plans=[gather_plan(group)])
    take(group, gathered)
    group = ("ffn2_w_gate",)
    (au, av, q_tok, k_tok, v_tok, gta, gtb), (gathered,) = _inproj_fwd(h1, mods, g2, wts["w_in"], seq,
                                                                       plans=[gather_plan(group)])
    take(group, gathered)
    group = ("ffn2_w_up",)
    ob, (gathered,) = _swa_fwd(q_tok, k_tok, v_tok, gq2, gk2, attn_sinks, nbs, plans=[gather_plan(group)])
    take(group, gathered)
    mixw = (wts["w_branch_a"], wts["w_branch_b"], wts["w_out"])
    h2, _ = _mix_fwd(au, av, gta, gtb, ob, h1, mods, g_ln, b_ln, ws, bsb, *mixw, seq)
    group = ("ffn2_w_down",)
    (gb3, ub3), (gathered,) = _ffn_gate_up(h2, mods, g3, wts["ffn2_w_gate"], wts["ffn2_w_up"], 6, seq,
                                           plans=[gather_plan(group)])
    take(group, gathered)
    dh3, y3, loss_parts = _ffn_down_loss(h2, gb3, ub3, mods, wts["ffn2_w_down"], 6, seq, tgt)
    loss_part = jnp.full((8, 128), jnp.sum(loss_parts[:, 0, 0]))

    (dh2, xb, dyb, a, dg, du, mg3), _ = _ffn_bwd(h2, dh3, y3, gb3, ub3, mods, g3, *[wts[k] for k in ffn2], 6, seq)
    d_gate, _ = _wgrad(dg, xb, "wgrad_ffn2_gate")
    d_up, (r,) = _wgrad(du, xb, "wgrad_ffn2_up", plans=[to_sibling(ffn2[:1], [d_gate])])
    add(ffn2[:1], [d_gate], r)
    d_down, (r,) = _wgrad(a, dyb, "wgrad_ffn2_down", plans=[to_sibling(ffn2[1:2], [d_up])])
    add(ffn2[1:2], [d_up], r)

    (dau, dav, dgta, dgtb, dob, d_out, d_a, d_b, dws, dbs, dln, mg2g), (r, from_chips) = _mix_bwd(
        au, av, gta, gtb, ob, dh2, mods, g_ln, b_ln, ws, bsb, *mixw, seq,
        plans=[to_sibling(ffn2[2:], [d_down]), to_chips(ffn2[:2])])
    add(ffn2[2:], [d_down], r)
    update(ffn2[:2], from_chips)
    mixers = ("w_out", "w_branch_a", "w_branch_b")
    (dq, dk, dv, dk_halo, dv_halo, dgq2, dgk2, dsk), (from_chips, r, early) = _swa_bwd(
        q_tok, k_tok, v_tok, gq2, gk2, attn_sinks, dob, nbs,
        plans=[to_chips(ffn2[2:]), to_sibling(mixers, [d_out, d_a, d_b]), _Gather([dln, dws, dbs, loss_part])])
    update(ffn2[2:], from_chips)
    add(mixers, [d_out, d_a, d_b], r)
    dk, dv = _swa_add_halo(dk, dk_halo), _swa_add_halo(dv, dv_halo)
    dgq, dgk = dgq2[:, :HD] + dgq2[:, HD:], dgk2[:, :HD] + dgk2[:, HD:]
    (dh1, xb2, dpb, mg2), (from_chips,) = _inproj_bwd(
        h1, dh2, (dau, dav, dq, dk, dv, dgta, dgtb), mods, g2, wts["w_in"], seq, plans=[to_chips(mixers)])
    update(mixers, from_chips)

    d_in, _ = _wgrad(dpb, xb2, "wgrad_w_in")

    (dx, xb, dyb, a, dg, du, mg1), _ = _ffn_bwd(xf, dh1, y1, gb1, ub1, mods, g1, *[wts[k] for k in ffn1], 0, seq)
    dmods, dgn = _mod_finish(mg1, mg2, mg2g, mg3, mods, g1, g2, g3)
    d_gate, (r, late) = _wgrad(dg, xb, "wgrad_ffn1_gate",
                               plans=[to_sibling(("w_in",), [d_in]),
                                      _GatherDirect([dmods.reshape(B, 9 * D), dgn, dgq, dgk, dsk])])
    add(("w_in",), [d_in], r)
    gathered = late[:2] + early[:3] + late[2:] + early[3:]
    d_up, (r, from_chips) = _wgrad(du, xb, "wgrad_ffn1_up",
                                   plans=[to_sibling(ffn1[:1], [d_gate]), to_chips(("w_in",))])
    add(ffn1[:1], [d_gate], r)
    update(("w_in",), from_chips)
    d_down, (r, from_chips) = _wgrad(a, dyb, "wgrad_ffn1_down",
                                     plans=[to_sibling(ffn1[1:2], [d_up]), to_chips(ffn1[:1])])
    add(ffn1[1:2], [d_up], r)
    update(ffn1[:1], from_chips)
    small_out, loss = _small_update(gathered, w, m, v)
    dm_cols = lax.dynamic_slice_in_dim(gathered[0].reshape(N_DEV * B, 9 * D), (9 * D // N_DEV) * me,
                                       9 * D // N_DEV, axis=1)
    ada_out = _ada_update(c_all, dm_cols, w_ada[0], m_w_ada[0], v_w_ada[0])
    r, from_chips = _exchange([to_sibling(ffn1[2:], [d_down]), to_chips(ffn1[1:2])], "rs_tail")
    add(ffn1[2:], [d_down], r)
    update(ffn1[1:2], from_chips)
    (from_chips,) = _exchange([to_chips(ffn1[2:])], "rs_last")
    update(ffn1[2:], from_chips)

    def leaf(kind, name):
        if name == "w_ada":
            return ada_out[kind][None]
        if name in SMALL:
            return small_out[name][kind]
        return big_out[name][kind]

    return (loss, dx.reshape(B, seq, D), *[leaf(kind, name) for kind in range(4) for name in names])
```
